```python
import math
import jax
import jax.numpy as jnp
from jax import lax
import numpy as np


D_MODEL = 2048
BATCH = 8
SEQ = 4096
DEPTH = 4

N_MIXERS = 4
N_SSD = (DEPTH + 3) // 4
N_FOX = (DEPTH + 2) // 4
N_MLA = (DEPTH + 1) // 4
N_S5 = DEPTH // 4
ALPHA = (2.0 * DEPTH) ** 0.25
BETA = (8.0 * DEPTH) ** -0.25
LN_EPS = 1e-5
RMS_EPS = 1e-6
Q_BLOCK = 128

SSD_EXPAND = 2
SSD_D_INNER = SSD_EXPAND * D_MODEL
SSD_HEADDIM = 64
SSD_HEADS = SSD_D_INNER // SSD_HEADDIM
SSD_GROUPS = 8
SSD_STATE = 128
SSD_CONV = 4
SSD_CHUNK = 128
SSD_CONV_DIM = SSD_D_INNER + 2 * SSD_GROUPS * SSD_STATE
SSD_IN_DIM = SSD_D_INNER + SSD_CONV_DIM + SSD_HEADS
DT_PROJ_SCALE = 0.1

FOX_HEADS = 16
FOX_HEAD_DIM = D_MODEL // FOX_HEADS
FOX_WIDTH = FOX_HEADS * FOX_HEAD_DIM
FOX_IN_DIM = 4 * FOX_WIDTH + FOX_HEADS
FGATE_PROJ_SCALE = 0.1

MLA_HEADS = 16
MLA_Q_RANK = 512
MLA_KV_RANK = 512
MLA_NOPE = 128
MLA_ROPE = 64
MLA_V = 128
MLA_WIDTH = MLA_HEADS * MLA_V
MLA_IN_DIM = MLA_Q_RANK + MLA_KV_RANK + MLA_ROPE + MLA_WIDTH
ROPE_BASE = 10000.0

S5_WIDTH = D_MODEL
S5_GROUP = 16
S5_GROUPS = S5_WIDTH // S5_GROUP
S5_STATE = 64
S5_IN_DIM = 2 * S5_WIDTH

kernel_name = 'hybrid_ssd_fox_mla_s5_deepnorm'


def layer_norm(x, g, b):
    xf = x.astype(jnp.float32)
    mu = jnp.mean(xf, -1, keepdims=True)
    var = jnp.mean(jnp.square(xf - mu), -1, keepdims=True)
    return ((xf - mu) * lax.rsqrt(var + LN_EPS) * g.astype(jnp.float32) + b.astype(jnp.float32)).astype(x.dtype)


def rms_norm(x, w):
    xf = x.astype(jnp.float32)
    y = xf * lax.rsqrt(jnp.mean(jnp.square(xf), -1, keepdims=True) + RMS_EPS)
    return (y * w.astype(jnp.float32)).astype(x.dtype)


def causal_depthwise_conv(x, w, bias):
    k = w.shape[0]
    y = lax.conv_general_dilated(x, w[:, None, :].astype(x.dtype), window_strides=(1,), padding=[(k - 1, 0)], dimension_numbers=('NWC', 'WIO', 'NWC'), feature_group_count=x.shape[-1])
    return y + bias.astype(x.dtype)


def rope_cos_sin(positions, dim):
    inv_freq = ROPE_BASE ** (-jnp.arange(0, dim, 2, dtype=jnp.float32) / dim)
    ang = positions.astype(jnp.float32)[..., None] * inv_freq
    return jnp.cos(ang), jnp.sin(ang)


def apply_rope(x, cos, sin):
    xf = x.astype(jnp.float32)
    x1, x2 = jnp.split(xf, 2, axis=-1)
    return jnp.concatenate([x1 * cos - x2 * sin, x1 * sin + x2 * cos], axis=-1).astype(x.dtype)


def causal_block_attention(q, k, v, scale, cum=None):
    bsz, T, H, dq = q.shape
    nblk = T // Q_BLOCK
    key_pos = jnp.arange(T)
    q_blocks = jnp.moveaxis(q.reshape(bsz, nblk, Q_BLOCK, H, dq), 1, 0)
    starts = jnp.arange(nblk) * Q_BLOCK
    if cum is None:
        xs = (q_blocks, starts)
    else:
        cum = cum.astype(jnp.float32)
        cum_k = jnp.swapaxes(cum, 1, 2)
        xs = (q_blocks, starts, jnp.moveaxis(cum.reshape(bsz, nblk, Q_BLOCK, H), 1, 0))

    def attend(blk):
        qb, s0 = blk[0], blk[1]
        logits = jnp.einsum('bqhd,bkhd->bhqk', qb, k, preferred_element_type=jnp.float32) * scale
        if cum is not None:
            cq = jnp.swapaxes(blk[2], 1, 2)
            logits = logits + cq[..., :, None] - cum_k[:, :, None, :]
        query_pos = s0 + jnp.arange(Q_BLOCK)
        logits = jnp.where(key_pos[None, :] <= query_pos[:, None], logits, -jnp.inf)
        probs = jax.nn.softmax(logits, axis=-1)
        return jnp.einsum('bhqk,bkhd->bqhd', probs.astype(v.dtype), v)

    out = lax.map(attend, xs)
    return jnp.moveaxis(out, 0, 1).reshape(bsz, T, H, v.shape[-1])


def ssd_mixer(h, w_in, conv_w, conv_b, dt_bias, a_log, d_skip, norm_w, w_out):
    bsz, T, _ = h.shape
    G, E, P, N, Q = SSD_GROUPS, SSD_HEADS // SSD_GROUPS, SSD_HEADDIM, SSD_STATE, SSD_CHUNK
    nc = T // Q
    f32 = jnp.float32
    z, xbc, dt = jnp.split(h @ w_in, [SSD_D_INNER, SSD_D_INNER + SSD_CONV_DIM], axis=-1)
    xbc = jax.nn.silu(causal_depthwise_conv(xbc, conv_w, conv_b)).astype(f32)
    xs, bm, cm = jnp.split(xbc, [SSD_D_INNER, SSD_D_INNER + G * N], axis=-1)
    x = xs.reshape(bsz, nc, Q, G, E, P)
    bm = bm.reshape(bsz, nc, Q, G, N)
    cm = cm.reshape(bsz, nc, Q, G, N)
    dt = jax.nn.softplus(dt.astype(f32) + dt_bias.astype(f32)).reshape(bsz, nc, Q, G, E)
    a = -jnp.exp(a_log.astype(f32)).reshape(G, E)
    xdt = x * dt[..., None]
    acs = jnp.moveaxis(jnp.cumsum(dt * a, axis=2), 2, -1)
    idx = jnp.arange(Q)
    seg = acs[..., :, None] - acs[..., None, :]
    decay = jnp.exp(jnp.where(idx[:, None] >= idx[None, :], seg, -jnp.inf))
    cb = jnp.einsum('bclgn,bcsgn->bcgls', cm, bm)
    y_diag = jnp.einsum('bcgls,bcgels,bcsgep->bclgep', cb, decay, xdt)
    decay_to_end = jnp.exp(acs[..., -1:] - acs)
    states = jnp.einsum('bclgn,bcgel,bclgep->bcgepn', bm, decay_to_end, xdt)
    chunk_decay = jnp.exp(acs[..., -1])

    def carry_state(s, inp):
        st, dec = inp
        return s * dec[..., None, None] + st, s

    _, prev = lax.scan(carry_state, jnp.zeros_like(states[:, 0]), (jnp.moveaxis(states, 1, 0), jnp.moveaxis(chunk_decay, 1, 0)))
    prev = jnp.moveaxis(prev, 0, 1)
    y_off = jnp.einsum('bclgn,bcgepn,bcgel->bclgep', cm, prev, jnp.exp(acs))
    y = y_diag + y_off + d_skip.astype(f32).reshape(G, E, 1) * x
    y = y.reshape(bsz, T, SSD_D_INNER) * jax.nn.silu(z.astype(f32))
    yg = y.reshape(bsz, T, G, SSD_D_INNER // G)
    yg = yg * lax.rsqrt(jnp.mean(jnp.square(yg), -1, keepdims=True) + RMS_EPS)
    y = yg.reshape(bsz, T, SSD_D_INNER) * norm_w.astype(f32)
    return y.astype(h.dtype) @ w_out


def fox_mixer(h, w_in, f_bias, w_out):
    bsz, T, _ = h.shape
    W = FOX_WIDTH
    q, k, v, z, f = jnp.split(h @ w_in, [W, 2 * W, 3 * W, 4 * W], axis=-1)
    shp = (bsz, T, FOX_HEADS, FOX_HEAD_DIM)
    log_f = jax.nn.log_sigmoid(f.astype(jnp.float32) + f_bias.astype(jnp.float32))
    cum = jnp.cumsum(log_f, axis=1)
    o = causal_block_attention(q.reshape(shp), k.reshape(shp), v.reshape(shp), FOX_HEAD_DIM ** -0.5, cum)
    y = o.reshape(bsz, T, W) * jax.nn.silu(z)
    return y @ w_out


def mla_mixer(h, positions, w_in, q_norm, kv_norm, w_q_up, w_kv_up, w_out):
    bsz, T, _ = h.shape
    H = MLA_HEADS
    q_lat, kv_lat, k_pe, z = jnp.split(h @ w_in, [MLA_Q_RANK, MLA_Q_RANK + MLA_KV_RANK, MLA_Q_RANK + MLA_KV_RANK + MLA_ROPE], axis=-1)
    q = (rms_norm(q_lat, q_norm) @ w_q_up).reshape(bsz, T, H, MLA_NOPE + MLA_ROPE)
    kv = (rms_norm(kv_lat, kv_norm) @ w_kv_up).reshape(bsz, T, H, MLA_NOPE + MLA_V)
    cos, sin = rope_cos_sin(positions, MLA_ROPE)
    q_pe = apply_rope(q[..., MLA_NOPE:], cos[:, :, None, :], sin[:, :, None, :])
    k_pe = apply_rope(k_pe, cos, sin)
    q = jnp.concatenate([q[..., :MLA_NOPE], q_pe], axis=-1)
    k = jnp.concatenate([kv[..., :MLA_NOPE], jnp.broadcast_to(k_pe[:, :, None, :], (bsz, T, H, MLA_ROPE))], axis=-1)
    o = causal_block_attention(q, k, kv[..., MLA_NOPE:], (MLA_NOPE + MLA_ROPE) ** -0.5)
    y = o.reshape(bsz, T, MLA_WIDTH) * jax.nn.silu(z)
    return y @ w_out


def _diag_linear_combine(left, right):
    a_l, b_l = left
    a_r, b_r = right
    return a_r * a_l, a_r * b_l + b_r


def s5_mixer(h, w_in, lam_re, lam_im, log_step, b_re, b_im, c_re, c_im, d_skip, w_glu, b_glu, w_out):
    bsz, T, _ = h.shape
    f32 = jnp.float32
    u, z = jnp.split(h @ w_in, 2, axis=-1)
    u = u.astype(f32)
    lam = lax.complex(lam_re.astype(f32), lam_im.astype(f32))
    step = jnp.exp(log_step.astype(f32))[:, None]
    lam_bar = jnp.exp(lam * step)
    b_bar = ((lam_bar - 1.0) / lam)[..., None] * lax.complex(b_re.astype(f32), b_im.astype(f32))
    ug = u.reshape(bsz, T, S5_GROUPS, S5_GROUP).astype(jnp.complex64)
    bu = jnp.einsum('gpi,btgi->btgp', b_bar, ug)
    a_seq = jnp.broadcast_to(lam_bar, (1, T, S5_GROUPS, S5_STATE))
    _, states = lax.associative_scan(_diag_linear_combine, (a_seq, bu), axis=1)
    c = lax.complex(c_re.astype(f32), c_im.astype(f32))
    y = jnp.einsum('gip,btgp->btgi', c, states).real.reshape(bsz, T, S5_WIDTH) + d_skip.astype(f32) * u
    y = jax.nn.gelu(y)
    y = y * jax.nn.sigmoid(y @ w_glu.astype(f32) + b_glu.astype(f32))
    y = y * jax.nn.silu(z.astype(f32))
    return y.astype(h.dtype) @ w_out


def _normal(key, shape, std):
    return jax.random.normal(key, shape, jnp.float32) * std


def _fwd_setup_inputs(seed: int = 0) -> dict:
    key = jax.random.key(seed)
    keys = iter(jax.random.split(key, 48))
    D = D_MODEL
    x = _normal(next(keys), (BATCH, SEQ, D), 1.0)
    positions = (jax.random.randint(next(keys), (BATCH, 1), 0, 1024) + jnp.arange(SEQ)[None, :]).astype(jnp.int32)
    ln_g = 1.0 + _normal(next(keys), (DEPTH, D), 0.02)
    ln_b = _normal(next(keys), (DEPTH, D), 0.02)
    L = N_SSD
    ssd_cols = jnp.concatenate([jnp.ones((SSD_IN_DIM - SSD_HEADS,), jnp.float32), jnp.full((SSD_HEADS,), DT_PROJ_SCALE, jnp.float32)])
    ssd_w_in = _normal(next(keys), (L, D, SSD_IN_DIM), D ** -0.5) * ssd_cols
    ssd_conv_w = _normal(next(keys), (L, SSD_CONV, SSD_CONV_DIM), SSD_CONV ** -0.5)
    ssd_conv_b = _normal(next(keys), (L, SSD_CONV_DIM), 0.02)
    dt0 = jnp.exp(jax.random.uniform(next(keys), (L, SSD_HEADS), jnp.float32, math.log(1e-3), math.log(1e-1)))
    ssd_dt_bias = dt0 + jnp.log(-jnp.expm1(-dt0))
    ssd_a_log = jnp.log(jax.random.uniform(next(keys), (L, SSD_HEADS), jnp.float32, 1.0, 16.0))
    ssd_d = 1.0 + _normal(next(keys), (L, SSD_HEADS), 0.1)
    ssd_norm_w = 1.0 + _normal(next(keys), (L, SSD_D_INNER), 0.02)
    ssd_w_out = _normal(next(keys), (L, SSD_D_INNER, D), SSD_D_INNER ** -0.5 * BETA)
    L = N_FOX
    fox_cols = jnp.concatenate([jnp.ones((4 * FOX_WIDTH,), jnp.float32), jnp.full((FOX_HEADS,), FGATE_PROJ_SCALE, jnp.float32)])
    fox_w_in = _normal(next(keys), (L, D, FOX_IN_DIM), D ** -0.5) * fox_cols
    fox_f_bias = jax.random.uniform(next(keys), (L, FOX_HEADS), jnp.float32, 1.0, 6.0)
    fox_w_out = _normal(next(keys), (L, FOX_WIDTH, D), FOX_WIDTH ** -0.5 * BETA)
    L = N_MLA
    mla_w_in = _normal(next(keys), (L, D, MLA_IN_DIM), D ** -0.5)
    mla_q_norm = 1.0 + _normal(next(keys), (L, MLA_Q_RANK), 0.02)
    mla_kv_norm = 1.0 + _normal(next(keys), (L, MLA_KV_RANK), 0.02)
    mla_w_q_up = _normal(next(keys), (L, MLA_Q_RANK, MLA_HEADS * (MLA_NOPE + MLA_ROPE)), MLA_Q_RANK ** -0.5)
    mla_w_kv_up = _normal(next(keys), (L, MLA_KV_RANK, MLA_HEADS * (MLA_NOPE + MLA_V)), MLA_KV_RANK ** -0.5)
    mla_w_out = _normal(next(keys), (L, MLA_WIDTH, D), MLA_WIDTH ** -0.5 * BETA)
    L = N_S5
    s5_w_in = _normal(next(keys), (L, D, S5_IN_DIM), D ** -0.5)
    s5_lambda_re = -0.5 + _normal(next(keys), (L, S5_GROUPS, S5_STATE), 0.01)
    s5_lambda_im = math.pi * jnp.arange(S5_STATE, dtype=jnp.float32) + _normal(next(keys), (L, S5_GROUPS, S5_STATE), 0.01)
    s5_log_step = jax.random.uniform(next(keys), (L, S5_GROUPS), jnp.float32, math.log(1e-3), math.log(1e-1))
    s5_b_re = _normal(next(keys), (L, S5_GROUPS, S5_STATE, S5_GROUP), (2.0 * S5_GROUP) ** -0.5)
    s5_b_im = _normal(next(keys), (L, S5_GROUPS, S5_STATE, S5_GROUP), (2.0 * S5_GROUP) ** -0.5)
    s5_c_re = _normal(next(keys), (L, S5_GROUPS, S5_GROUP, S5_STATE), S5_STATE ** -0.5)
    s5_c_im = _normal(next(keys), (L, S5_GROUPS, S5_GROUP, S5_STATE), S5_STATE ** -0.5)
    s5_d = 1.0 + _normal(next(keys), (L, S5_WIDTH), 0.1)
    s5_w_glu = _normal(next(keys), (L, S5_WIDTH, S5_WIDTH), S5_WIDTH ** -0.5)
    s5_b_glu = _normal(next(keys), (L, S5_WIDTH), 0.02)
    s5_w_out = _normal(next(keys), (L, S5_WIDTH, D), S5_WIDTH ** -0.5 * BETA)
    return {'x': x, 'positions': positions, 'ln_g': ln_g, 'ln_b': ln_b,
            'ssd_w_in': ssd_w_in, 'ssd_conv_w': ssd_conv_w, 'ssd_conv_b': ssd_conv_b, 'ssd_dt_bias': ssd_dt_bias,
            'ssd_a_log': ssd_a_log, 'ssd_d': ssd_d, 'ssd_norm_w': ssd_norm_w, 'ssd_w_out': ssd_w_out,
            'fox_w_in': fox_w_in, 'fox_f_bias': fox_f_bias, 'fox_w_out': fox_w_out,
            'mla_w_in': mla_w_in, 'mla_q_norm': mla_q_norm, 'mla_kv_norm': mla_kv_norm, 'mla_w_q_up': mla_w_q_up,
            'mla_w_kv_up': mla_w_kv_up, 'mla_w_out': mla_w_out,
            's5_w_in': s5_w_in, 's5_lambda_re': s5_lambda_re, 's5_lambda_im': s5_lambda_im, 's5_log_step': s5_log_step,
            's5_b_re': s5_b_re, 's5_b_im': s5_b_im, 's5_c_re': s5_c_re, 's5_c_im': s5_c_im, 's5_d': s5_d,
            's5_w_glu': s5_w_glu, 's5_b_glu': s5_b_glu, 's5_w_out': s5_w_out}


def _fwd_reference(x, positions, ln_g, ln_b,
              ssd_w_in, ssd_conv_w, ssd_conv_b, ssd_dt_bias, ssd_a_log, ssd_d, ssd_norm_w, ssd_w_out,
              fox_w_in, fox_f_bias, fox_w_out,
              mla_w_in, mla_q_norm, mla_kv_norm, mla_w_q_up, mla_w_kv_up, mla_w_out,
              s5_w_in, s5_lambda_re, s5_lambda_im, s5_log_step, s5_b_re, s5_b_im, s5_c_re, s5_c_im, s5_d,
              s5_w_glu, s5_b_glu, s5_w_out):
    h = x
    for i in range(DEPTH):
        j = i // N_MIXERS
        kind = i % N_MIXERS
        if kind == 0:
            out = ssd_mixer(h, ssd_w_in[j], ssd_conv_w[j], ssd_conv_b[j], ssd_dt_bias[j], ssd_a_log[j], ssd_d[j], ssd_norm_w[j], ssd_w_out[j])
        elif kind == 1:
            out = fox_mixer(h, fox_w_in[j], fox_f_bias[j], fox_w_out[j])
        elif kind == 2:
            out = mla_mixer(h, positions, mla_w_in[j], mla_q_norm[j], mla_kv_norm[j], mla_w_q_up[j], mla_w_kv_up[j], mla_w_out[j])
        else:
            out = s5_mixer(h, s5_w_in[j], s5_lambda_re[j], s5_lambda_im[j], s5_log_step[j], s5_b_re[j], s5_b_im[j], s5_c_re[j], s5_c_im[j], s5_d[j], s5_w_glu[j], s5_b_glu[j], s5_w_out[j])
        h = layer_norm(ALPHA * h + out.astype(h.dtype), ln_g[i], ln_b[i])
    return h


import jax as _jax
import jax.numpy as _jnp

TWIN_FORMAT = 'train_step'
FWD_PARAMS = ['x', 'positions', 'ln_g', 'ln_b', 'ssd_w_in', 'ssd_conv_w', 'ssd_conv_b', 'ssd_dt_bias', 'ssd_a_log', 'ssd_d', 'ssd_norm_w', 'ssd_w_out', 'fox_w_in', 'fox_f_bias', 'fox_w_out', 'mla_w_in', 'mla_q_norm', 'mla_kv_norm', 'mla_w_q_up', 'mla_w_kv_up', 'mla_w_out', 's5_w_in', 's5_lambda_re', 's5_lambda_im', 's5_log_step', 's5_b_re', 's5_b_im', 's5_c_re', 's5_c_im', 's5_d', 's5_w_glu', 's5_b_glu', 's5_w_out']
TWIN_WEIGHTS = ['ln_g', 'ln_b', 'ssd_w_in', 'ssd_conv_w', 'ssd_conv_b', 'ssd_dt_bias', 'ssd_a_log', 'ssd_d', 'ssd_norm_w', 'ssd_w_out', 'fox_w_in', 'fox_f_bias', 'fox_w_out', 'mla_w_in', 'mla_q_norm', 'mla_kv_norm', 'mla_w_q_up', 'mla_w_kv_up', 'mla_w_out', 's5_w_in', 's5_lambda_re', 's5_lambda_im', 's5_log_step', 's5_b_re', 's5_b_im', 's5_c_re', 's5_c_im', 's5_d', 's5_w_glu', 's5_b_glu', 's5_w_out']
TWIN_DIFF_INPUT = 'x'
TWIN_INPUTS = ['x', 'positions', 'ln_g', 'ln_b', 'ssd_w_in', 'ssd_conv_w', 'ssd_conv_b', 'ssd_dt_bias', 'ssd_a_log', 'ssd_d', 'ssd_norm_w', 'ssd_w_out', 'fox_w_in', 'fox_f_bias', 'fox_w_out', 'mla_w_in', 'mla_q_norm', 'mla_kv_norm', 'mla_w_q_up', 'mla_w_kv_up', 'mla_w_out', 's5_w_in', 's5_lambda_re', 's5_lambda_im', 's5_log_step', 's5_b_re', 's5_b_im', 's5_c_re', 's5_c_im', 's5_d', 's5_w_glu', 's5_b_glu', 's5_w_out', 'loss_target', 'm_ln_g', 'm_ln_b', 'm_ssd_w_in', 'm_ssd_conv_w', 'm_ssd_conv_b', 'm_ssd_dt_bias', 'm_ssd_a_log', 'm_ssd_d', 'm_ssd_norm_w', 'm_ssd_w_out', 'm_fox_w_in', 'm_fox_f_bias', 'm_fox_w_out', 'm_mla_w_in', 'm_mla_q_norm', 'm_mla_kv_norm', 'm_mla_w_q_up', 'm_mla_w_kv_up', 'm_mla_w_out', 'm_s5_w_in', 'm_s5_lambda_re', 'm_s5_lambda_im', 'm_s5_log_step', 'm_s5_b_re', 'm_s5_b_im', 'm_s5_c_re', 'm_s5_c_im', 'm_s5_d', 'm_s5_w_glu', 'm_s5_b_glu', 'm_s5_w_out', 'v_ln_g', 'v_ln_b', 'v_ssd_w_in', 'v_ssd_conv_w', 'v_ssd_conv_b', 'v_ssd_dt_bias', 'v_ssd_a_log', 'v_ssd_d', 'v_ssd_norm_w', 'v_ssd_w_out', 'v_fox_w_in', 'v_fox_f_bias', 'v_fox_w_out', 'v_mla_w_in', 'v_mla_q_norm', 'v_mla_kv_norm', 'v_mla_w_q_up', 'v_mla_w_kv_up', 'v_mla_w_out', 'v_s5_w_in', 'v_s5_lambda_re', 'v_s5_lambda_im', 'v_s5_log_step', 'v_s5_b_re', 'v_s5_b_im', 'v_s5_c_re', 'v_s5_c_im', 'v_s5_d', 'v_s5_w_glu', 'v_s5_b_glu', 'v_s5_w_out']
TWIN_OUTPUTS = ['loss', 'grad_x', 'grad_ln_g', 'grad_ln_b', 'grad_ssd_w_in', 'grad_ssd_conv_w', 'grad_ssd_conv_b', 'grad_ssd_dt_bias', 'grad_ssd_a_log', 'grad_ssd_d', 'grad_ssd_norm_w', 'grad_ssd_w_out', 'grad_fox_w_in', 'grad_fox_f_bias', 'grad_fox_w_out', 'grad_mla_w_in', 'grad_mla_q_norm', 'grad_mla_kv_norm', 'grad_mla_w_q_up', 'grad_mla_w_kv_up', 'grad_mla_w_out', 'grad_s5_w_in', 'grad_s5_lambda_re', 'grad_s5_lambda_im', 'grad_s5_log_step', 'grad_s5_b_re', 'grad_s5_b_im', 'grad_s5_c_re', 'grad_s5_c_im', 'grad_s5_d', 'grad_s5_w_glu', 'grad_s5_b_glu', 'grad_s5_w_out', 'delta_ln_g', 'delta_ln_b', 'delta_ssd_w_in', 'delta_ssd_conv_w', 'delta_ssd_conv_b', 'delta_ssd_dt_bias', 'delta_ssd_a_log', 'delta_ssd_d', 'delta_ssd_norm_w', 'delta_ssd_w_out', 'delta_fox_w_in', 'delta_fox_f_bias', 'delta_fox_w_out', 'delta_mla_w_in', 'delta_mla_q_norm', 'delta_mla_kv_norm', 'delta_mla_w_q_up', 'delta_mla_w_kv_up', 'delta_mla_w_out', 'delta_s5_w_in', 'delta_s5_lambda_re', 'delta_s5_lambda_im', 'delta_s5_log_step', 'delta_s5_b_re', 'delta_s5_b_im', 'delta_s5_c_re', 'delta_s5_c_im', 'delta_s5_d', 'delta_s5_w_glu', 'delta_s5_b_glu', 'delta_s5_w_out', 'new_m_ln_g', 'new_m_ln_b', 'new_m_ssd_w_in', 'new_m_ssd_conv_w', 'new_m_ssd_conv_b', 'new_m_ssd_dt_bias', 'new_m_ssd_a_log', 'new_m_ssd_d', 'new_m_ssd_norm_w', 'new_m_ssd_w_out', 'new_m_fox_w_in', 'new_m_fox_f_bias', 'new_m_fox_w_out', 'new_m_mla_w_in', 'new_m_mla_q_norm', 'new_m_mla_kv_norm', 'new_m_mla_w_q_up', 'new_m_mla_w_kv_up', 'new_m_mla_w_out', 'new_m_s5_w_in', 'new_m_s5_lambda_re', 'new_m_s5_lambda_im', 'new_m_s5_log_step', 'new_m_s5_b_re', 'new_m_s5_b_im', 'new_m_s5_c_re', 'new_m_s5_c_im', 'new_m_s5_d', 'new_m_s5_w_glu', 'new_m_s5_b_glu', 'new_m_s5_w_out', 'new_v_ln_g', 'new_v_ln_b', 'new_v_ssd_w_in', 'new_v_ssd_conv_w', 'new_v_ssd_conv_b', 'new_v_ssd_dt_bias', 'new_v_ssd_a_log', 'new_v_ssd_d', 'new_v_ssd_norm_w', 'new_v_ssd_w_out', 'new_v_fox_w_in', 'new_v_fox_f_bias', 'new_v_fox_w_out', 'new_v_mla_w_in', 'new_v_mla_q_norm', 'new_v_mla_kv_norm', 'new_v_mla_w_q_up', 'new_v_mla_w_kv_up', 'new_v_mla_w_out', 'new_v_s5_w_in', 'new_v_s5_lambda_re', 'new_v_s5_lambda_im', 'new_v_s5_log_step', 'new_v_s5_b_re', 'new_v_s5_b_im', 'new_v_s5_c_re', 'new_v_s5_c_im', 'new_v_s5_d', 'new_v_s5_w_glu', 'new_v_s5_b_glu', 'new_v_s5_w_out']
TWIN_LEAF_KINDS = {'loss': 'loss', 'grad_x': 'grad_x', 'grad_ln_g': 'grad_w', 'grad_ln_b': 'grad_w', 'grad_ssd_w_in': 'grad_w', 'grad_ssd_conv_w': 'grad_w', 'grad_ssd_conv_b': 'grad_w', 'grad_ssd_dt_bias': 'grad_w', 'grad_ssd_a_log': 'grad_w', 'grad_ssd_d': 'grad_w', 'grad_ssd_norm_w': 'grad_w', 'grad_ssd_w_out': 'grad_w', 'grad_fox_w_in': 'grad_w', 'grad_fox_f_bias': 'grad_w', 'grad_fox_w_out': 'grad_w', 'grad_mla_w_in': 'grad_w', 'grad_mla_q_norm': 'grad_w', 'grad_mla_kv_norm': 'grad_w', 'grad_mla_w_q_up': 'grad_w', 'grad_mla_w_kv_up': 'grad_w', 'grad_mla_w_out': 'grad_w', 'grad_s5_w_in': 'grad_w', 'grad_s5_lambda_re': 'grad_w', 'grad_s5_lambda_im': 'grad_w', 'grad_s5_log_step': 'grad_w', 'grad_s5_b_re': 'grad_w', 'grad_s5_b_im': 'grad_w', 'grad_s5_c_re': 'grad_w', 'grad_s5_c_im': 'grad_w', 'grad_s5_d': 'grad_w', 'grad_s5_w_glu': 'grad_w', 'grad_s5_b_glu': 'grad_w', 'grad_s5_w_out': 'grad_w', 'delta_ln_g': 'delta_w', 'delta_ln_b': 'delta_w', 'delta_ssd_w_in': 'delta_w', 'delta_ssd_conv_w': 'delta_w', 'delta_ssd_conv_b': 'delta_w', 'delta_ssd_dt_bias': 'delta_w', 'delta_ssd_a_log': 'delta_w', 'delta_ssd_d': 'delta_w', 'delta_ssd_norm_w': 'delta_w', 'delta_ssd_w_out': 'delta_w', 'delta_fox_w_in': 'delta_w', 'delta_fox_f_bias': 'delta_w', 'delta_fox_w_out': 'delta_w', 'delta_mla_w_in': 'delta_w', 'delta_mla_q_norm': 'delta_w', 'delta_mla_kv_norm': 'delta_w', 'delta_mla_w_q_up': 'delta_w', 'delta_mla_w_kv_up': 'delta_w', 'delta_mla_w_out': 'delta_w', 'delta_s5_w_in': 'delta_w', 'delta_s5_lambda_re': 'delta_w', 'delta_s5_lambda_im': 'delta_w', 'delta_s5_log_step': 'delta_w', 'delta_s5_b_re': 'delta_w', 'delta_s5_b_im': 'delta_w', 'delta_s5_c_re': 'delta_w', 'delta_s5_c_im': 'delta_w', 'delta_s5_d': 'delta_w', 'delta_s5_w_glu': 'delta_w', 'delta_s5_b_glu': 'delta_w', 'delta_s5_w_out': 'delta_w', 'new_m_ln_g': 'new_m', 'new_m_ln_b': 'new_m', 'new_m_ssd_w_in': 'new_m', 'new_m_ssd_conv_w': 'new_m', 'new_m_ssd_conv_b': 'new_m', 'new_m_ssd_dt_bias': 'new_m', 'new_m_ssd_a_log': 'new_m', 'new_m_ssd_d': 'new_m', 'new_m_ssd_norm_w': 'new_m', 'new_m_ssd_w_out': 'new_m', 'new_m_fox_w_in': 'new_m', 'new_m_fox_f_bias': 'new_m', 'new_m_fox_w_out': 'new_m', 'new_m_mla_w_in': 'new_m', 'new_m_mla_q_norm': 'new_m', 'new_m_mla_kv_norm': 'new_m', 'new_m_mla_w_q_up': 'new_m', 'new_m_mla_w_kv_up': 'new_m', 'new_m_mla_w_out': 'new_m', 'new_m_s5_w_in': 'new_m', 'new_m_s5_lambda_re': 'new_m', 'new_m_s5_lambda_im': 'new_m', 'new_m_s5_log_step': 'new_m', 'new_m_s5_b_re': 'new_m', 'new_m_s5_b_im': 'new_m', 'new_m_s5_c_re': 'new_m', 'new_m_s5_c_im': 'new_m', 'new_m_s5_d': 'new_m', 'new_m_s5_w_glu': 'new_m', 'new_m_s5_b_glu': 'new_m', 'new_m_s5_w_out': 'new_m', 'new_v_ln_g': 'new_v', 'new_v_ln_b': 'new_v', 'new_v_ssd_w_in': 'new_v', 'new_v_ssd_conv_w': 'new_v', 'new_v_ssd_conv_b': 'new_v', 'new_v_ssd_dt_bias': 'new_v', 'new_v_ssd_a_log': 'new_v', 'new_v_ssd_d': 'new_v', 'new_v_ssd_norm_w': 'new_v', 'new_v_ssd_w_out': 'new_v', 'new_v_fox_w_in': 'new_v', 'new_v_fox_f_bias': 'new_v', 'new_v_fox_w_out': 'new_v', 'new_v_mla_w_in': 'new_v', 'new_v_mla_q_norm': 'new_v', 'new_v_mla_kv_norm': 'new_v', 'new_v_mla_w_q_up': 'new_v', 'new_v_mla_w_kv_up': 'new_v', 'new_v_mla_w_out': 'new_v', 'new_v_s5_w_in': 'new_v', 'new_v_s5_lambda_re': 'new_v', 'new_v_s5_lambda_im': 'new_v', 'new_v_s5_log_step': 'new_v', 'new_v_s5_b_re': 'new_v', 'new_v_s5_b_im': 'new_v', 'new_v_s5_c_re': 'new_v', 'new_v_s5_c_im': 'new_v', 'new_v_s5_d': 'new_v', 'new_v_s5_w_glu': 'new_v', 'new_v_s5_b_glu': 'new_v', 'new_v_s5_w_out': 'new_v'}


def _forward(args):
    return _fwd_reference(*[args[k] for k in FWD_PARAMS])


def _output_shape():
    def fwd():
        inp = _fwd_setup_inputs(0)
        return _fwd_reference(*[inp[k] for k in FWD_PARAMS])
    out = _jax.eval_shape(fwd)
    return out.shape, out.dtype

N_MICROBATCH = 1
ADAM_LR = 0.001
ADAM_B1 = 0.9
ADAM_B2 = 0.999
ADAM_EPS = 1e-08
ADAM_WD = 0.01
ADAM_STEP = 10
PER_EXAMPLE_BATCH_AXIS = {'x': 0, 'positions': 0, 'loss_target': 0}
SHARED_INPUTS = []
_WEIGHT_DTYPES = {'ln_g': _jnp.float32, 'ln_b': _jnp.float32, 'ssd_w_in': _jnp.float32, 'ssd_conv_w': _jnp.float32, 'ssd_conv_b': _jnp.float32, 'ssd_dt_bias': _jnp.float32, 'ssd_a_log': _jnp.float32, 'ssd_d': _jnp.float32, 'ssd_norm_w': _jnp.float32, 'ssd_w_out': _jnp.float32, 'fox_w_in': _jnp.float32, 'fox_f_bias': _jnp.float32, 'fox_w_out': _jnp.float32, 'mla_w_in': _jnp.float32, 'mla_q_norm': _jnp.float32, 'mla_kv_norm': _jnp.float32, 'mla_w_q_up': _jnp.float32, 'mla_w_kv_up': _jnp.float32, 'mla_w_out': _jnp.float32, 's5_w_in': _jnp.float32, 's5_lambda_re': _jnp.float32, 's5_lambda_im': _jnp.float32, 's5_log_step': _jnp.float32, 's5_b_re': _jnp.float32, 's5_b_im': _jnp.float32, 's5_c_re': _jnp.float32, 's5_c_im': _jnp.float32, 's5_d': _jnp.float32, 's5_w_glu': _jnp.float32, 's5_b_glu': _jnp.float32, 's5_w_out': _jnp.float32}
MOMENT_SCALE = {'ln_g': 8.027127e+00, 'ln_b': 5.349055e-01, 'ssd_w_in': 1.387865e-02, 'ssd_conv_w': 1.299838e-02, 'ssd_conv_b': 2.344712e-02, 'ssd_dt_bias': 2.044556e-02, 'ssd_a_log': 7.744781e-02, 'ssd_d': 9.271318e-02, 'ssd_norm_w': 1.698866e-02, 'ssd_w_out': 5.672560e-02, 'fox_w_in': 3.265610e-03, 'fox_f_bias': 1.841568e-02, 'fox_w_out': 8.540526e-03, 'mla_w_in': 4.335446e-03, 'mla_q_norm': 4.804894e-03, 'mla_kv_norm': 7.111102e-03, 'mla_w_q_up': 1.926380e-03, 'mla_w_kv_up': 2.509677e-03, 'mla_w_out': 7.071857e-03, 's5_w_in': 4.798287e-03, 's5_lambda_re': 3.537249e-04, 's5_lambda_im': 3.442221e-04, 's5_log_step': 2.314158e-01, 's5_b_re': 2.364734e-04, 's5_b_im': 2.383893e-04, 's5_c_re': 3.324741e-04, 's5_c_im': 3.391801e-04, 's5_d': 5.997618e-03, 's5_w_glu': 1.330235e-03, 's5_b_glu': 2.168040e-03, 's5_w_out': 1.143602e-02}


def _to_microbatches(a, axis):
    t = _jnp.moveaxis(a, axis, 0)
    t = t.reshape((N_MICROBATCH, t.shape[0] // N_MICROBATCH) + t.shape[1:])
    return _jnp.moveaxis(t, 1, axis + 1)


def setup_inputs(seed: int = 0) -> dict:
    inp = _fwd_setup_inputs(seed)
    key = _jax.random.fold_in(_jax.random.key(seed), 7919)
    shape, _ = _output_shape()
    out = dict(inp)
    out["loss_target"] = _jax.random.normal(_jax.random.fold_in(key, 0), shape, _jnp.float32)
    for i, name in enumerate(TWIN_WEIGHTS):
        w = inp[name].astype(_jnp.float32)
        if MOMENT_SCALE is None:
            s = _jnp.sqrt(_jnp.mean(_jnp.square(w)) + 1e-30)
        else:
            s = MOMENT_SCALE[name]
        km, kv = _jax.random.split(_jax.random.fold_in(key, i + 1))
        out[name] = w
        out["m_" + name] = s * _jax.random.normal(km, w.shape, _jnp.float32)
        out["v_" + name] = (s * s) * _jax.random.uniform(kv, w.shape, _jnp.float32, 0.5, 1.5)
    if N_MICROBATCH > 1:
        for name, axis in PER_EXAMPLE_BATCH_AXIS.items():
            out[name] = _to_microbatches(out[name], axis)
    return {'x': out['x'], 'positions': out['positions'], 'ln_g': out['ln_g'], 'ln_b': out['ln_b'], 'ssd_w_in': out['ssd_w_in'], 'ssd_conv_w': out['ssd_conv_w'], 'ssd_conv_b': out['ssd_conv_b'], 'ssd_dt_bias': out['ssd_dt_bias'], 'ssd_a_log': out['ssd_a_log'], 'ssd_d': out['ssd_d'], 'ssd_norm_w': out['ssd_norm_w'], 'ssd_w_out': out['ssd_w_out'], 'fox_w_in': out['fox_w_in'], 'fox_f_bias': out['fox_f_bias'], 'fox_w_out': out['fox_w_out'], 'mla_w_in': out['mla_w_in'], 'mla_q_norm': out['mla_q_norm'], 'mla_kv_norm': out['mla_kv_norm'], 'mla_w_q_up': out['mla_w_q_up'], 'mla_w_kv_up': out['mla_w_kv_up'], 'mla_w_out': out['mla_w_out'], 's5_w_in': out['s5_w_in'], 's5_lambda_re': out['s5_lambda_re'], 's5_lambda_im': out['s5_lambda_im'], 's5_log_step': out['s5_log_step'], 's5_b_re': out['s5_b_re'], 's5_b_im': out['s5_b_im'], 's5_c_re': out['s5_c_re'], 's5_c_im': out['s5_c_im'], 's5_d': out['s5_d'], 's5_w_glu': out['s5_w_glu'], 's5_b_glu': out['s5_b_glu'], 's5_w_out': out['s5_w_out'], 'loss_target': out['loss_target'], 'm_ln_g': out['m_ln_g'], 'm_ln_b': out['m_ln_b'], 'm_ssd_w_in': out['m_ssd_w_in'], 'm_ssd_conv_w': out['m_ssd_conv_w'], 'm_ssd_conv_b': out['m_ssd_conv_b'], 'm_ssd_dt_bias': out['m_ssd_dt_bias'], 'm_ssd_a_log': out['m_ssd_a_log'], 'm_ssd_d': out['m_ssd_d'], 'm_ssd_norm_w': out['m_ssd_norm_w'], 'm_ssd_w_out': out['m_ssd_w_out'], 'm_fox_w_in': out['m_fox_w_in'], 'm_fox_f_bias': out['m_fox_f_bias'], 'm_fox_w_out': out['m_fox_w_out'], 'm_mla_w_in': out['m_mla_w_in'], 'm_mla_q_norm': out['m_mla_q_norm'], 'm_mla_kv_norm': out['m_mla_kv_norm'], 'm_mla_w_q_up': out['m_mla_w_q_up'], 'm_mla_w_kv_up': out['m_mla_w_kv_up'], 'm_mla_w_out': out['m_mla_w_out'], 'm_s5_w_in': out['m_s5_w_in'], 'm_s5_lambda_re': out['m_s5_lambda_re'], 'm_s5_lambda_im': out['m_s5_lambda_im'], 'm_s5_log_step': out['m_s5_log_step'], 'm_s5_b_re': out['m_s5_b_re'], 'm_s5_b_im': out['m_s5_b_im'], 'm_s5_c_re': out['m_s5_c_re'], 'm_s5_c_im': out['m_s5_c_im'], 'm_s5_d': out['m_s5_d'], 'm_s5_w_glu': out['m_s5_w_glu'], 'm_s5_b_glu': out['m_s5_b_glu'], 'm_s5_w_out': out['m_s5_w_out'], 'v_ln_g': out['v_ln_g'], 'v_ln_b': out['v_ln_b'], 'v_ssd_w_in': out['v_ssd_w_in'], 'v_ssd_conv_w': out['v_ssd_conv_w'], 'v_ssd_conv_b': out['v_ssd_conv_b'], 'v_ssd_dt_bias': out['v_ssd_dt_bias'], 'v_ssd_a_log': out['v_ssd_a_log'], 'v_ssd_d': out['v_ssd_d'], 'v_ssd_norm_w': out['v_ssd_norm_w'], 'v_ssd_w_out': out['v_ssd_w_out'], 'v_fox_w_in': out['v_fox_w_in'], 'v_fox_f_bias': out['v_fox_f_bias'], 'v_fox_w_out': out['v_fox_w_out'], 'v_mla_w_in': out['v_mla_w_in'], 'v_mla_q_norm': out['v_mla_q_norm'], 'v_mla_kv_norm': out['v_mla_kv_norm'], 'v_mla_w_q_up': out['v_mla_w_q_up'], 'v_mla_w_kv_up': out['v_mla_w_kv_up'], 'v_mla_w_out': out['v_mla_w_out'], 'v_s5_w_in': out['v_s5_w_in'], 'v_s5_lambda_re': out['v_s5_lambda_re'], 'v_s5_lambda_im': out['v_s5_lambda_im'], 'v_s5_log_step': out['v_s5_log_step'], 'v_s5_b_re': out['v_s5_b_re'], 'v_s5_b_im': out['v_s5_b_im'], 'v_s5_c_re': out['v_s5_c_re'], 'v_s5_c_im': out['v_s5_c_im'], 'v_s5_d': out['v_s5_d'], 'v_s5_w_glu': out['v_s5_w_glu'], 'v_s5_b_glu': out['v_s5_b_glu'], 'v_s5_w_out': out['v_s5_w_out']}


def _loss(weights, diff, rest, loss_target):
    with _jax.named_scope("forward"):
        args = {**rest, TWIN_DIFF_INPUT: diff, **{k: w.astype(_WEIGHT_DTYPES[k]) for k, w in weights.items()}}
        y = _forward(args)
    with _jax.named_scope("loss_head"):
        err = _jnp.square(y.astype(_jnp.float32) - loss_target)
        return 0.5 * _jnp.sum(_jnp.mean(err, axis=-1)) if err.ndim else 0.5 * err


def _adamw(w, g, m, v):
    m = ADAM_B1 * m + (1.0 - ADAM_B1) * g
    v = ADAM_B2 * v + (1.0 - ADAM_B2) * _jnp.square(g)
    m_hat = m / (1.0 - ADAM_B1 ** ADAM_STEP)
    v_hat = v / (1.0 - ADAM_B2 ** ADAM_STEP)
    delta = -ADAM_LR * (m_hat / (_jnp.sqrt(v_hat) + ADAM_EPS) + ADAM_WD * w)
    return delta, m, v


def reference(x, positions, ln_g, ln_b, ssd_w_in, ssd_conv_w, ssd_conv_b, ssd_dt_bias, ssd_a_log, ssd_d, ssd_norm_w, ssd_w_out, fox_w_in, fox_f_bias, fox_w_out, mla_w_in, mla_q_norm, mla_kv_norm, mla_w_q_up, mla_w_kv_up, mla_w_out, s5_w_in, s5_lambda_re, s5_lambda_im, s5_log_step, s5_b_re, s5_b_im, s5_c_re, s5_c_im, s5_d, s5_w_glu, s5_b_glu, s5_w_out, loss_target, m_ln_g, m_ln_b, m_ssd_w_in, m_ssd_conv_w, m_ssd_conv_b, m_ssd_dt_bias, m_ssd_a_log, m_ssd_d, m_ssd_norm_w, m_ssd_w_out, m_fox_w_in, m_fox_f_bias, m_fox_w_out, m_mla_w_in, m_mla_q_norm, m_mla_kv_norm, m_mla_w_q_up, m_mla_w_kv_up, m_mla_w_out, m_s5_w_in, m_s5_lambda_re, m_s5_lambda_im, m_s5_log_step, m_s5_b_re, m_s5_b_im, m_s5_c_re, m_s5_c_im, m_s5_d, m_s5_w_glu, m_s5_b_glu, m_s5_w_out, v_ln_g, v_ln_b, v_ssd_w_in, v_ssd_conv_w, v_ssd_conv_b, v_ssd_dt_bias, v_ssd_a_log, v_ssd_d, v_ssd_norm_w, v_ssd_w_out, v_fox_w_in, v_fox_f_bias, v_fox_w_out, v_mla_w_in, v_mla_q_norm, v_mla_kv_norm, v_mla_w_q_up, v_mla_w_kv_up, v_mla_w_out, v_s5_w_in, v_s5_lambda_re, v_s5_lambda_im, v_s5_log_step, v_s5_b_re, v_s5_b_im, v_s5_c_re, v_s5_c_im, v_s5_d, v_s5_w_glu, v_s5_b_glu, v_s5_w_out):
    given = dict(x=x, positions=positions, ln_g=ln_g, ln_b=ln_b, ssd_w_in=ssd_w_in, ssd_conv_w=ssd_conv_w, ssd_conv_b=ssd_conv_b, ssd_dt_bias=ssd_dt_bias, ssd_a_log=ssd_a_log, ssd_d=ssd_d, ssd_norm_w=ssd_norm_w, ssd_w_out=ssd_w_out, fox_w_in=fox_w_in, fox_f_bias=fox_f_bias, fox_w_out=fox_w_out, mla_w_in=mla_w_in, mla_q_norm=mla_q_norm, mla_kv_norm=mla_kv_norm, mla_w_q_up=mla_w_q_up, mla_w_kv_up=mla_w_kv_up, mla_w_out=mla_w_out, s5_w_in=s5_w_in, s5_lambda_re=s5_lambda_re, s5_lambda_im=s5_lambda_im, s5_log_step=s5_log_step, s5_b_re=s5_b_re, s5_b_im=s5_b_im, s5_c_re=s5_c_re, s5_c_im=s5_c_im, s5_d=s5_d, s5_w_glu=s5_w_glu, s5_b_glu=s5_b_glu, s5_w_out=s5_w_out, loss_target=loss_target, m_ln_g=m_ln_g, m_ln_b=m_ln_b, m_ssd_w_in=m_ssd_w_in, m_ssd_conv_w=m_ssd_conv_w, m_ssd_conv_b=m_ssd_conv_b, m_ssd_dt_bias=m_ssd_dt_bias, m_ssd_a_log=m_ssd_a_log, m_ssd_d=m_ssd_d, m_ssd_norm_w=m_ssd_norm_w, m_ssd_w_out=m_ssd_w_out, m_fox_w_in=m_fox_w_in, m_fox_f_bias=m_fox_f_bias, m_fox_w_out=m_fox_w_out, m_mla_w_in=m_mla_w_in, m_mla_q_norm=m_mla_q_norm, m_mla_kv_norm=m_mla_kv_norm, m_mla_w_q_up=m_mla_w_q_up, m_mla_w_kv_up=m_mla_w_kv_up, m_mla_w_out=m_mla_w_out, m_s5_w_in=m_s5_w_in, m_s5_lambda_re=m_s5_lambda_re, m_s5_lambda_im=m_s5_lambda_im, m_s5_log_step=m_s5_log_step, m_s5_b_re=m_s5_b_re, m_s5_b_im=m_s5_b_im, m_s5_c_re=m_s5_c_re, m_s5_c_im=m_s5_c_im, m_s5_d=m_s5_d, m_s5_w_glu=m_s5_w_glu, m_s5_b_glu=m_s5_b_glu, m_s5_w_out=m_s5_w_out, v_ln_g=v_ln_g, v_ln_b=v_ln_b, v_ssd_w_in=v_ssd_w_in, v_ssd_conv_w=v_ssd_conv_w, v_ssd_conv_b=v_ssd_conv_b, v_ssd_dt_bias=v_ssd_dt_bias, v_ssd_a_log=v_ssd_a_log, v_ssd_d=v_ssd_d, v_ssd_norm_w=v_ssd_norm_w, v_ssd_w_out=v_ssd_w_out, v_fox_w_in=v_fox_w_in, v_fox_f_bias=v_fox_f_bias, v_fox_w_out=v_fox_w_out, v_mla_w_in=v_mla_w_in, v_mla_q_norm=v_mla_q_norm, v_mla_kv_norm=v_mla_kv_norm, v_mla_w_q_up=v_mla_w_q_up, v_mla_w_kv_up=v_mla_w_kv_up, v_mla_w_out=v_mla_w_out, v_s5_w_in=v_s5_w_in, v_s5_lambda_re=v_s5_lambda_re, v_s5_lambda_im=v_s5_lambda_im, v_s5_log_step=v_s5_log_step, v_s5_b_re=v_s5_b_re, v_s5_b_im=v_s5_b_im, v_s5_c_re=v_s5_c_re, v_s5_c_im=v_s5_c_im, v_s5_d=v_s5_d, v_s5_w_glu=v_s5_w_glu, v_s5_b_glu=v_s5_b_glu, v_s5_w_out=v_s5_w_out)
    weights = {n: given[n] for n in TWIN_WEIGHTS}
    shared = {n: given[n] for n in SHARED_INPUTS}
    per_example = {n: given[n] for n in ['x', 'positions']}
    grad_fn = _jax.value_and_grad(_loss, argnums=(0, 1))

    def one_microbatch(ex, loss_target):
        ex = dict(ex)
        diff = ex.pop(TWIN_DIFF_INPUT)
        return grad_fn(weights, diff, {**shared, **ex}, loss_target)

    if N_MICROBATCH == 1:
        loss, (grad_w, grad_x) = one_microbatch(per_example, given["loss_target"])
    else:
        def body(carry, xs):
            loss_sum, grad_sum = carry
            l_k, (gw_k, gx_k) = one_microbatch(xs[0], xs[1])
            with _jax.named_scope("update"):
                return (loss_sum + l_k, _jax.tree.map(_jnp.add, grad_sum, gw_k)), gx_k

        init = (_jnp.zeros((), _jnp.float32), _jax.tree.map(_jnp.zeros_like, weights))
        (loss, grad_w), grad_x = _jax.lax.scan(body, init, (per_example, given["loss_target"]))
    with _jax.named_scope("update"):
        delta_w, new_m, new_v = {}, {}, {}
        for n in TWIN_WEIGHTS:
            delta_w[n], new_m[n], new_v[n] = _adamw(weights[n], grad_w[n], given["m_" + n], given["v_" + n])
    return (loss, grad_x, *[grad_w[n] for n in TWIN_WEIGHTS], *[delta_w[n] for n in TWIN_WEIGHTS],
            *[new_m[n] for n in TWIN_WEIGHTS], *[new_v[n] for n in TWIN_WEIGHTS])
```

```python
import functools
import math

import jax
import jax.numpy as jnp
from jax import lax
from jax.experimental import pallas as pl
from jax.experimental.pallas import tpu as pltpu

f32 = jnp.float32
bf16 = jnp.bfloat16

N_DEV = 8
DEPTH = 4
ALPHA = (2.0 * DEPTH) ** 0.25
LN_EPS = 1e-5
RMS_EPS = 1e-6
LANE = 128

SSD_STATE = 128
SSD_HEADDIM = 64
SSD_CHUNK = 128
FOX_HEAD_DIM = 128
MLA_NOPE = 128
MLA_ROPE = 64
MLA_V = 128
ROPE_BASE = 10000.0
S5_GROUP = 16
S5_BLOCK_GROUPS = 8
S5_CHUNK = 128

ADAM_LR = 0.001
ADAM_B1 = 0.9
ADAM_B2 = 0.999
ADAM_EPS = 1e-08
ADAM_WD = 0.01
ADAM_STEP = 10

VMEM_BLOCK_BUDGET = 20 * 1024 * 1024
HIGHEST = lax.Precision.HIGHEST


def _pick(n, cands):
    for c in cands:
        if n % c == 0:
            return c
    return n


def _dg(a, b, ca, cb):
    return lax.dot_general(a.astype(bf16), b.astype(bf16), (((ca,), (cb,)), ((), ())),
                           preferred_element_type=f32)


@functools.partial(jax.custom_vjp, nondiff_argnums=(2, 3))
def bdot(a, b, ca, cb):
    return _dg(a, b, ca, cb)


def _bdot_fwd(a, b, ca, cb):
    return _dg(a, b, ca, cb), (a, b)


def _bdot_bwd(ca, cb, res, g):
    a, b = res
    nb = 1 - cb
    ma = 1 - ca
    da = _dg(g, b, 1, nb) if ca == 1 else _dg(b, g, nb, 1)
    db = _dg(a, g, ma, 0) if cb == 0 else _dg(g, a, 0, ma)
    return da, db


bdot.defvjp(_bdot_fwd, _bdot_bwd)


def _silu(x):
    return x * jax.nn.sigmoid(x)


def _softplus(x):
    return jnp.maximum(x, 0.0) + jnp.log(1.0 + jnp.exp(-jnp.abs(x)))


def _mm(a, b, *, ta=False, tb=False, add=None, name):
    m, k = (a.shape[1], a.shape[0]) if ta else a.shape
    n = b.shape[0] if tb else b.shape[1]
    kb = b.shape[1] if tb else b.shape[0]
    assert k == kb, (name, a.shape, b.shape)
    tm = _pick(m, (1024, 512, 256, 128))
    tn = _pick(n, (1024, 512, 384, 256, 128))
    tk = _pick(k, (512, 256, 128))
    nk = k // tk
    has_add = add is not None

    def body(*refs):
        if has_add:
            a_ref, b_ref, add_ref, o_ref, acc = refs
        else:
            a_ref, b_ref, o_ref, acc = refs
        kk = pl.program_id(2)

        @pl.when(kk == 0)
        def _():
            acc[...] = jnp.zeros_like(acc)

        acc[...] += _dg(a_ref[...], b_ref[...], 0 if ta else 1, 1 if tb else 0)

        @pl.when(kk == nk - 1)
        def _():
            r = acc[...]
            if has_add:
                r = r + add_ref[...]
            o_ref[...] = r

    a_spec = (pl.BlockSpec((tk, tm), lambda i, j, kk: (kk, i)) if ta
              else pl.BlockSpec((tm, tk), lambda i, j, kk: (i, kk)))
    b_spec = (pl.BlockSpec((tn, tk), lambda i, j, kk: (j, kk)) if tb
              else pl.BlockSpec((tk, tn), lambda i, j, kk: (kk, j)))
    o_spec = pl.BlockSpec((tm, tn), lambda i, j, kk: (i, j))
    in_specs = [a_spec, b_spec] + ([o_spec] if has_add else [])
    args = (a, b) + ((add,) if has_add else ())
    return pl.pallas_call(
        body, name=name, grid=(m // tm, n // tn, nk),
        in_specs=in_specs, out_specs=o_spec,
        out_shape=jax.ShapeDtypeStruct((m, n), f32),
        scratch_shapes=[pltpu.VMEM((tm, tn), f32)],
        compiler_params=pltpu.CompilerParams(dimension_semantics=("parallel", "parallel", "arbitrary")),
    )(*args)


def _row_operand(op):
    if isinstance(op, tuple):
        arr, w, cb = op
    else:
        arr, w, cb = op, op.shape[1], 0
    return arr, w, cb


def _rw_tm(t, widths):
    per_row = 2 * 4 * sum(widths)
    tm = 512
    while tm > 8 and (tm * per_row > VMEM_BLOCK_BUDGET or t % tm):
        tm //= 2
    return tm


def _rw_specs(ops, tm):
    specs, arrs = [], []
    for op in ops:
        arr, w, cb = _row_operand(op)
        specs.append(pl.BlockSpec((tm, w), functools.partial(lambda i, cb: (i, cb), cb=cb)))
        arrs.append(arr)
    return specs, arrs


def _param_spec(p):
    nd = p.ndim
    return pl.BlockSpec(p.shape, lambda i: (0,) * nd)


def _rw_fwd(fn, rows, params, out_widths, *, name):
    t = _row_operand(rows[0])[0].shape[0]
    tm = _rw_tm(t, [_row_operand(r)[1] for r in rows] + list(out_widths))
    nr, npar = len(rows), len(params)

    def body(*refs):
        vals = [r[...] for r in refs[:nr + npar]]
        outs = fn(*vals)
        for o_ref, v in zip(refs[nr + npar:], outs):
            o_ref[...] = v

    rspecs, rarrs = _rw_specs(rows, tm)
    return pl.pallas_call(
        body, name=name, grid=(t // tm,),
        in_specs=rspecs + [_param_spec(p) for p in params],
        out_specs=[pl.BlockSpec((tm, w), lambda i: (i, 0)) for w in out_widths],
        out_shape=[jax.ShapeDtypeStruct((t, w), f32) for w in out_widths],
        compiler_params=pltpu.CompilerParams(dimension_semantics=("parallel",)),
    )(*rarrs, *params)


def _rw_bwd(fn, rows, params, cots, *, name, n_const=0):
    t = _row_operand(rows[0])[0].shape[0]
    nr, npar, nc = len(rows), len(params), len(cots)
    nd = nr - n_const
    widths = [_row_operand(r)[1] for r in rows]
    tm = _rw_tm(t, widths + widths[:nd] + [c.shape[1] for c in cots])

    def body(*refs):
        rv = [r[...] for r in refs[:nr]]
        pv = [r[...] for r in refs[nr:nr + npar]]
        cv = [r[...] for r in refs[nr + npar:nr + npar + nc]]
        outs = refs[nr + npar + nc:]
        consts = rv[nd:]

        def f(*diff):
            return fn(*diff[:nd], *consts, *diff[nd:])

        _, vjp = jax.vjp(f, *rv[:nd], *pv)
        g = vjp(tuple(cv))
        for o_ref, v in zip(outs[:nd], g[:nd]):
            o_ref[...] = v
        if npar:
            @pl.when(pl.program_id(0) == 0)
            def _():
                for o_ref in outs[nd:]:
                    o_ref[...] = jnp.zeros_like(o_ref)

            for o_ref, v in zip(outs[nd:], g[nd:]):
                o_ref[...] += v

    rspecs, rarrs = _rw_specs(rows, tm)
    cspecs = [pl.BlockSpec((tm, c.shape[1]), lambda i: (i, 0)) for c in cots]
    return pl.pallas_call(
        body, name=name, grid=(t // tm,),
        in_specs=rspecs + [_param_spec(p) for p in params] + cspecs,
        out_specs=[pl.BlockSpec((tm, w), lambda i: (i, 0)) for w in widths[:nd]]
        + [_param_spec(p) for p in params],
        out_shape=[jax.ShapeDtypeStruct((t, w), f32) for w in widths[:nd]]
        + [jax.ShapeDtypeStruct(p.shape, f32) for p in params],
        compiler_params=pltpu.CompilerParams(dimension_semantics=("arbitrary",)),
    )(*rarrs, *params, *cots)


def _ln_fn(h, o, g, b):
    x = ALPHA * h + o
    mu = jnp.mean(x, -1, keepdims=True)
    var = jnp.mean(jnp.square(x - mu), -1, keepdims=True)
    return ((x - mu) * lax.rsqrt(var + LN_EPS) * g + b,)


def _gate_fn(o, z):
    return (o * _silu(z),)


def _make_ssd_post_fn(groups):
    def fn(y, z, nw):
        yz = y * _silu(z)
        gw = yz.shape[1] // groups
        outs = []
        for g in range(groups):
            blk = yz[:, g * gw:(g + 1) * gw]
            outs.append(blk * lax.rsqrt(jnp.mean(jnp.square(blk), -1, keepdims=True) + RMS_EPS))
        return (jnp.concatenate(outs, axis=1) * nw,)
    return fn


def _logf_fn(f, b):
    return (jax.nn.log_sigmoid(f + b),)


def _make_mla_norm_fn(rank):
    def fn(lat, qn, kvn):
        def rms(x, w):
            return x * lax.rsqrt(jnp.mean(jnp.square(x), -1, keepdims=True) + RMS_EPS) * w
        return rms(lat[:, :rank], qn), rms(lat[:, rank:], kvn)
    return fn


def _rope_block(xb, cos, sin):
    half = MLA_ROPE // 2
    x1, x2 = xb[:, :half], xb[:, half:MLA_ROPE]
    return jnp.concatenate([x1 * cos - x2 * sin, x1 * sin + x2 * cos, xb[:, MLA_ROPE:]], axis=1)


def _make_rope_fn(heads):
    def fn(x, cos, sin):
        return (jnp.concatenate([_rope_block(x[:, h * LANE:(h + 1) * LANE], cos, sin)
                                 for h in range(heads)], axis=1),)
    return fn


def _s5_act_fn(ys, u, d):
    return (jax.nn.gelu(ys + d * u),)


def _s5_glu_fn(y1, gp, z, bg):
    return (y1 * jax.nn.sigmoid(gp + bg) * _silu(z),)


def _loss_head(y, tgt):
    t, d = y.shape
    tm = _rw_tm(t, [d, d, d])

    def body(y_ref, t_ref, dy_ref, l_ref):
        @pl.when(pl.program_id(0) == 0)
        def _():
            l_ref[...] = jnp.zeros_like(l_ref)

        e = y_ref[...] - t_ref[...]
        dy_ref[...] = e * (1.0 / d)
        l_ref[...] += 0.5 * jnp.sum(jnp.mean(jnp.square(e), axis=-1, keepdims=True), axis=0, keepdims=True)

    spec = pl.BlockSpec((tm, d), lambda i: (i, 0))
    return pl.pallas_call(
        body, name="loss_head", grid=(t // tm,), in_specs=[spec, spec],
        out_specs=[spec, pl.BlockSpec((1, 1), lambda i: (0, 0))],
        out_shape=[jax.ShapeDtypeStruct((t, d), f32), jax.ShapeDtypeStruct((1, 1), f32)],
        compiler_params=pltpu.CompilerParams(dimension_semantics=("arbitrary",)),
    )(y, tgt)


def _attn_tile(t):
    return _pick(t, (512, 256, 128))


def _logits(q1, k1, q2, k2, ck, scale, i, j, tq, tk):
    s = _dg(q1, k1, 1, 1)
    if q2 is not None:
        s = s + _dg(q2, k2, 1, 1)
    s = s * scale
    if ck is not None:
        s = s - ck
    row = i * tq + lax.broadcasted_iota(jnp.int32, (tq, tk), 0)
    col = j * tk + lax.broadcasted_iota(jnp.int32, (tq, tk), 1)
    return jnp.where(col <= row, s, -jnp.inf)


def _attn_fwd(q1, q1o, k1, k1o, v, vo, q2, k2, ck, *, heads, scale, name):
    t = q1.shape[0]
    tq = tk = _attn_tile(t)
    nq = t // tq
    has2, hasb = q2 is not None, ck is not None

    def body(*refs):
        it = iter(refs)
        q1r, k1r, vr = next(it), next(it), next(it)
        q2r, k2r = (next(it), next(it)) if has2 else (None, None)
        ckr = next(it) if hasb else None
        o_r, lse_r, m_s, l_s, acc = next(it), next(it), next(it), next(it), next(it)
        i, j = pl.program_id(1), pl.program_id(2)

        @pl.when(j == 0)
        def _():
            m_s[...] = jnp.full_like(m_s, -jnp.inf)
            l_s[...] = jnp.zeros_like(l_s)
            acc[...] = jnp.zeros_like(acc)

        @pl.when(j <= i)
        def _():
            s = _logits(q1r[...], k1r[...], q2r[...] if has2 else None, k2r[...] if has2 else None,
                        ckr[0] if hasb else None, scale, i, j, tq, tk)
            m_prev = m_s[...]
            m_new = jnp.maximum(m_prev, jnp.max(s, axis=1, keepdims=True))
            p = jnp.exp(s - m_new)
            a = jnp.exp(m_prev - m_new)
            l_s[...] = a * l_s[...] + jnp.sum(p, axis=1, keepdims=True)
            acc[...] = a * acc[...] + _dg(p, vr[...], 1, 0)
            m_s[...] = m_new

        @pl.when(j == i)
        def _():
            o_r[...] = acc[...] / l_s[...]
            lse_r[0] = jnp.broadcast_to(m_s[...] + jnp.log(l_s[...]), (tq, LANE))

    def qmap(off):
        return lambda h, i, j: (i, off + h)

    def kmap(off):
        return lambda h, i, j: (jnp.minimum(j, i), off + h)

    in_specs = [pl.BlockSpec((tq, LANE), qmap(q1o)), pl.BlockSpec((tk, LANE), kmap(k1o)),
                pl.BlockSpec((tk, LANE), kmap(vo))]
    args = [q1, k1, v]
    if has2:
        in_specs += [pl.BlockSpec((tq, LANE), qmap(0)),
                     pl.BlockSpec((tk, LANE), lambda h, i, j: (jnp.minimum(j, i), 0))]
        args += [q2, k2]
    if hasb:
        in_specs += [pl.BlockSpec((1, 1, tk), lambda h, i, j: (h, 0, jnp.minimum(j, i)))]
        args += [ck]
    return pl.pallas_call(
        body, name=name, grid=(heads, nq, nq), in_specs=in_specs,
        out_specs=[pl.BlockSpec((tq, LANE), lambda h, i, j: (i, h)),
                   pl.BlockSpec((1, tq, LANE), lambda h, i, j: (h, i, 0))],
        out_shape=[jax.ShapeDtypeStruct((t, heads * LANE), f32),
                   jax.ShapeDtypeStruct((heads, t, LANE), f32)],
        scratch_shapes=[pltpu.VMEM((tq, 1), f32), pltpu.VMEM((tq, 1), f32), pltpu.VMEM((tq, LANE), f32)],
        compiler_params=pltpu.CompilerParams(dimension_semantics=("parallel", "parallel", "arbitrary")),
    )(*args)


def _attn_delta(do, o, *, heads, name):
    t = do.shape[0]
    tq = _attn_tile(t)

    def body(do_ref, o_ref, d_ref):
        d_ref[0] = jnp.broadcast_to(jnp.sum(do_ref[...] * o_ref[...], axis=1, keepdims=True), (tq, LANE))

    spec = pl.BlockSpec((tq, LANE), lambda i, h: (i, h))
    return pl.pallas_call(
        body, name=name, grid=(t // tq, heads), in_specs=[spec, spec],
        out_specs=pl.BlockSpec((1, tq, LANE), lambda i, h: (h, i, 0)),
        out_shape=jax.ShapeDtypeStruct((heads, t, LANE), f32),
        compiler_params=pltpu.CompilerParams(dimension_semantics=("parallel", "parallel")),
    )(do, o)


def _attn_bwd_dq(q1, q1o, k1, k1o, v, vo, q2, k2, ck, do, lse, delta, *, heads, scale, name):
    t = q1.shape[0]
    tq = tk = _attn_tile(t)
    nq = t // tq
    has2, hasb = q2 is not None, ck is not None

    def body(*refs):
        it = iter(refs)
        q1r, k1r, vr = next(it), next(it), next(it)
        q2r, k2r = (next(it), next(it)) if has2 else (None, None)
        ckr = next(it) if hasb else None
        dor, lser, dlr = next(it), next(it), next(it)
        dq1r = next(it)
        dq2r = next(it) if has2 else None
        drowr = next(it) if hasb else None
        a1 = next(it)
        a2 = next(it) if has2 else None
        ar = next(it) if hasb else None
        i, j = pl.program_id(1), pl.program_id(2)

        @pl.when(j == 0)
        def _():
            a1[...] = jnp.zeros_like(a1)
            if has2:
                a2[...] = jnp.zeros_like(a2)
            if hasb:
                ar[...] = jnp.zeros_like(ar)

        @pl.when(j <= i)
        def _():
            s = _logits(q1r[...], k1r[...], q2r[...] if has2 else None, k2r[...] if has2 else None,
                        ckr[0] if hasb else None, scale, i, j, tq, tk)
            p = jnp.exp(s - lser[0][:, 0:1])
            dp = _dg(dor[...], vr[...], 1, 1)
            dsr = p * (dp - dlr[0][:, 0:1])
            ds = dsr * scale
            a1[...] += _dg(ds, k1r[...], 1, 0)
            if has2:
                a2[...] += _dg(ds, k2r[...], 1, 0)
            if hasb:
                ar[...] += jnp.sum(dsr, axis=1, keepdims=True)

        @pl.when(j == i)
        def _():
            dq1r[...] = a1[...]
            if has2:
                dq2r[...] = a2[...]
            if hasb:
                drowr[0] = jnp.broadcast_to(ar[...], (tq, LANE))

    def qmap(off):
        return lambda h, i, j: (i, off + h)

    def kmap(off):
        return lambda h, i, j: (jnp.minimum(j, i), off + h)

    in_specs = [pl.BlockSpec((tq, LANE), qmap(q1o)), pl.BlockSpec((tk, LANE), kmap(k1o)),
                pl.BlockSpec((tk, LANE), kmap(vo))]
    args = [q1, k1, v]
    if has2:
        in_specs += [pl.BlockSpec((tq, LANE), qmap(0)),
                     pl.BlockSpec((tk, LANE), lambda h, i, j: (jnp.minimum(j, i), 0))]
        args += [q2, k2]
    if hasb:
        in_specs += [pl.BlockSpec((1, 1, tk), lambda h, i, j: (h, 0, jnp.minimum(j, i)))]
        args += [ck]
    row3 = pl.BlockSpec((1, tq, LANE), lambda h, i, j: (h, i, 0))
    in_specs += [pl.BlockSpec((tq, LANE), qmap(0)), row3, row3]
    args += [do, lse, delta]
    n_out = 2 if has2 else 1
    oshape = jax.ShapeDtypeStruct((t, heads * LANE), f32)
    out_specs = [pl.BlockSpec((tq, LANE), qmap(0))] * n_out
    out_shape = [oshape] * n_out
    scratch = [pltpu.VMEM((tq, LANE), f32)] * n_out
    if hasb:
        out_specs.append(row3)
        out_shape.append(jax.ShapeDtypeStruct((heads, t, LANE), f32))
        scratch.append(pltpu.VMEM((tq, 1), f32))
    return pl.pallas_call(
        body, name=name, grid=(heads, nq, nq), in_specs=in_specs,
        out_specs=out_specs, out_shape=out_shape, scratch_shapes=scratch,
        compiler_params=pltpu.CompilerParams(dimension_semantics=("parallel", "parallel", "arbitrary")),
    )(*args)


def _attn_bwd_dkv(q1, q1o, k1, k1o, v, vo, q2, k2, ck, do, lse, delta, *, heads, scale, name):
    t = q1.shape[0]
    tq = tk = _attn_tile(t)
    nq = t // tq
    has2, hasb = q2 is not None, ck is not None

    def body(*refs):
        it = iter(refs)
        q1r, k1r, vr = next(it), next(it), next(it)
        q2r, k2r = (next(it), next(it)) if has2 else (None, None)
        ckr = next(it) if hasb else None
        dor, lser, dlr = next(it), next(it), next(it)
        dk1r, dvr = next(it), next(it)
        dk2r = next(it) if has2 else None
        dckr = next(it) if hasb else None
        ak, av = next(it), next(it)
        ak2 = next(it) if has2 else None
        ac = next(it) if hasb else None
        j, h, i = pl.program_id(0), pl.program_id(1), pl.program_id(2)

        @pl.when(i == j)
        def _():
            ak[...] = jnp.zeros_like(ak)
            av[...] = jnp.zeros_like(av)
            if hasb:
                ac[...] = jnp.zeros_like(ac)

        if has2:
            @pl.when((i == j) & (h == 0))
            def _():
                ak2[...] = jnp.zeros_like(ak2)

        @pl.when(i >= j)
        def _():
            s = _logits(q1r[...], k1r[...], q2r[...] if has2 else None, k2r[...] if has2 else None,
                        ckr[0] if hasb else None, scale, i, j, tq, tk)
            p = jnp.exp(s - lser[0][:, 0:1])
            dp = _dg(dor[...], vr[...], 1, 1)
            dsr = p * (dp - dlr[0][:, 0:1])
            ds = dsr * scale
            av[...] += _dg(p, dor[...], 0, 0)
            ak[...] += _dg(ds, q1r[...], 0, 0)
            if has2:
                ak2[...] += _dg(ds, q2r[...], 0, 0)
            if hasb:
                ac[...] -= jnp.sum(dsr, axis=0, keepdims=True)

        @pl.when(i == nq - 1)
        def _():
            dk1r[...] = ak[...]
            dvr[...] = av[...]
            if hasb:
                dckr[0] = ac[...]

        if has2:
            @pl.when((i == nq - 1) & (h == heads - 1))
            def _():
                dk2r[...] = ak2[...]

    def qmap(off):
        return lambda j, h, i: (jnp.maximum(i, j), off + h)

    def kmap(off):
        return lambda j, h, i: (j, off + h)

    in_specs = [pl.BlockSpec((tq, LANE), qmap(q1o)), pl.BlockSpec((tk, LANE), kmap(k1o)),
                pl.BlockSpec((tk, LANE), kmap(vo))]
    args = [q1, k1, v]
    if has2:
        in_specs += [pl.BlockSpec((tq, LANE), qmap(0)), pl.BlockSpec((tk, LANE), lambda j, h, i: (j, 0))]
        args += [q2, k2]
    if hasb:
        in_specs += [pl.BlockSpec((1, 1, tk), lambda j, h, i: (h, 0, j))]
        args += [ck]
    row3 = pl.BlockSpec((1, tq, LANE), lambda j, h, i: (h, jnp.maximum(i, j), 0))
    in_specs += [pl.BlockSpec((tq, LANE), qmap(0)), row3, row3]
    args += [do, lse, delta]
    hd = jax.ShapeDtypeStruct((t, heads * LANE), f32)
    out_specs = [pl.BlockSpec((tk, LANE), kmap(0))] * 2
    out_shape = [hd, hd]
    scratch = [pltpu.VMEM((tk, LANE), f32)] * 2
    if has2:
        out_specs.append(pl.BlockSpec((tk, LANE), lambda j, h, i: (j, 0)))
        out_shape.append(jax.ShapeDtypeStruct((t, LANE), f32))
        scratch.append(pltpu.VMEM((tk, LANE), f32))
    if hasb:
        out_specs.append(pl.BlockSpec((1, 1, tk), lambda j, h, i: (h, 0, j)))
        out_shape.append(jax.ShapeDtypeStruct((heads, 1, t), f32))
        scratch.append(pltpu.VMEM((1, tk), f32))
    return pl.pallas_call(
        body, name=name, grid=(nq, heads, nq), in_specs=in_specs, out_specs=out_specs, out_shape=out_shape,
        scratch_shapes=scratch,
        compiler_params=pltpu.CompilerParams(dimension_semantics=("arbitrary", "arbitrary", "arbitrary")),
    )(*args)


def _cumsum_rows(x, *, reverse, name):
    t, w = x.shape
    blk = 128
    nb = t // blk

    def body(x_ref, o_ref):
        r = lax.broadcasted_iota(jnp.int32, (blk, blk), 0)
        c = lax.broadcasted_iota(jnp.int32, (blk, blk), 1)
        tri = ((r <= c) if reverse else (r >= c)).astype(f32)
        carry = jnp.zeros((1, w), f32)
        for b in (reversed(range(nb)) if reverse else range(nb)):
            seg = x_ref[b * blk:(b + 1) * blk, :]
            cs = jnp.dot(tri, seg, precision=HIGHEST, preferred_element_type=f32) + carry
            o_ref[b * blk:(b + 1) * blk, :] = cs
            carry = cs[0:1, :] if reverse else cs[blk - 1:blk, :]

    return pl.pallas_call(body, name=name, out_shape=jax.ShapeDtypeStruct((t, w), f32))(x)


CONV_PAD = 8


def _conv_fn(x, w, b):
    t, taps = x.shape[0], w.shape[0]
    xx = jnp.concatenate([jnp.zeros((CONV_PAD, x.shape[1]), f32), x], axis=0)
    y = b
    for k in range(taps):
        off = CONV_PAD - (taps - 1) + k
        y = y + w[k:k + 1, :] * xx[off:off + t, :]
    return _silu(y)


def _conv_fwd(x, w, b, *, name):
    t, c = x.shape
    tc = LANE
    taps = w.shape[0]

    def body(x_ref, w_ref, b_ref, o_ref):
        o_ref[...] = _conv_fn(x_ref[...], w_ref[...], b_ref[...])

    xs = pl.BlockSpec((t, tc), lambda i: (0, i))
    return pl.pallas_call(
        body, name=name, grid=(c // tc,),
        in_specs=[xs, pl.BlockSpec((taps, tc), lambda i: (0, i)), pl.BlockSpec((1, tc), lambda i: (0, i))],
        out_specs=xs, out_shape=jax.ShapeDtypeStruct((t, c), f32),
        compiler_params=pltpu.CompilerParams(dimension_semantics=("parallel",)),
    )(x, w, b)


def _conv_bwd(x, w, b, ds, *, name):
    t, c = x.shape
    tc = LANE
    taps = w.shape[0]

    def body(x_ref, w_ref, b_ref, ds_ref, dx_ref, dw_ref, db_ref):
        _, vjp = jax.vjp(_conv_fn, x_ref[...], w_ref[...], b_ref[...])
        dx, dw, db = vjp(ds_ref[...])
        dx_ref[...] = dx
        dw_ref[...] = dw
        db_ref[...] = db

    xs = pl.BlockSpec((t, tc), lambda i: (0, i))
    ws = pl.BlockSpec((taps, tc), lambda i: (0, i))
    bs = pl.BlockSpec((1, tc), lambda i: (0, i))
    return pl.pallas_call(
        body, name=name, grid=(c // tc,), in_specs=[xs, ws, bs, xs], out_specs=[xs, ws, bs],
        out_shape=[jax.ShapeDtypeStruct((t, c), f32), jax.ShapeDtypeStruct((taps, c), f32),
                   jax.ShapeDtypeStruct((1, c), f32)],
        compiler_params=pltpu.CompilerParams(dimension_semantics=("parallel",)),
    )(x, w, b, ds)


def _ssd_chunk(x, bm, cm, dtr, st, dtb, alog, dsk, *, e_heads):
    ln, p = x.shape[0], SSD_HEADDIM
    dt = _softplus(dtr + dtb)
    da = dt * (-jnp.exp(alog))
    r = lax.broadcasted_iota(jnp.int32, (ln, ln), 0)
    c = lax.broadcasted_iota(jnp.int32, (ln, ln), 1)
    lower = r >= c
    acs = jnp.dot(lower.astype(f32), da, precision=HIGHEST, preferred_element_type=f32)
    acs_t = acs.T

    def per_head(v):
        return jnp.concatenate([jnp.broadcast_to(v[:, e:e + 1], (v.shape[0], p)) for e in range(e_heads)], axis=1)

    dt_x, acs_x = per_head(dt), per_head(acs)
    last_x = acs_x[ln - 1:ln, :]
    xdt = x * dt_x
    cb = bdot(cm, bm, 1, 1)
    y_off = bdot(cm, st, 1, 0) * jnp.exp(acs_x)
    st_new = st * jnp.exp(last_x) + bdot(bm, xdt * jnp.exp(last_x - acs_x), 0, 0)
    ys = []
    for e in range(e_heads):
        seg = acs[:, e:e + 1] - acs_t[e:e + 1, :]
        dec = jnp.exp(jnp.where(lower, seg, -jnp.inf))
        ys.append(bdot(cb * dec, xdt[:, e * p:(e + 1) * p], 1, 0))
    y = jnp.concatenate(ys, axis=1) + y_off + per_head(dsk) * x
    return y, st_new


def _ssd_specs(t, d_inner, groups):
    ln, n = SSD_CHUNK, SSD_STATE
    gw = d_inner // groups
    nc = t // ln
    return ln, n, gw, nc


def _ssd_fwd(xbc, dtp, dtb, alog, dsk, *, d_inner, groups, name):
    t = xbc.shape[0]
    ln, n, gw, nc = _ssd_specs(t, d_inner, groups)
    e_heads = gw // SSD_HEADDIM
    boff = d_inner // n

    def body(x_ref, b_ref, c_ref, dt_ref, dtb_ref, alog_ref, dsk_ref, y_ref, save_ref, st_scr):
        c, g = pl.program_id(0), pl.program_id(1)

        @pl.when(c == 0)
        def _():
            st_scr[g] = jnp.zeros((n, gw), f32)

        st = st_scr[g]
        save_ref[0] = st
        y, st_new = _ssd_chunk(x_ref[...], b_ref[...], c_ref[...], dt_ref[...], st,
                               dtb_ref[0], alog_ref[0], dsk_ref[0], e_heads=e_heads)
        y_ref[...] = y
        st_scr[g] = st_new

    par = pl.BlockSpec((1, 1, LANE), lambda c, g: (g, 0, 0))
    return pl.pallas_call(
        body, name=name, grid=(nc, groups),
        in_specs=[pl.BlockSpec((ln, gw), lambda c, g: (c, g)),
                  pl.BlockSpec((ln, n), lambda c, g: (c, boff + g)),
                  pl.BlockSpec((ln, n), lambda c, g: (c, boff + groups + g)),
                  pl.BlockSpec((ln, LANE), lambda c, g: (c, g)), par, par, par],
        out_specs=[pl.BlockSpec((ln, gw), lambda c, g: (c, g)),
                   pl.BlockSpec((1, n, gw), lambda c, g: (c * groups + g, 0, 0))],
        out_shape=[jax.ShapeDtypeStruct((t, d_inner), f32),
                   jax.ShapeDtypeStruct((nc * groups, n, gw), f32)],
        scratch_shapes=[pltpu.VMEM((groups, n, gw), f32)],
        compiler_params=pltpu.CompilerParams(dimension_semantics=("arbitrary", "arbitrary")),
    )(xbc, xbc, xbc, dtp, dtb, alog, dsk)


def _ssd_bwd(xbc, dtp, dtb, alog, dsk, saved, dy, *, d_inner, groups, name):
    t = xbc.shape[0]
    ln, n, gw, nc = _ssd_specs(t, d_inner, groups)
    e_heads = gw // SSD_HEADDIM
    boff = d_inner // n

    def body(x_ref, b_ref, c_ref, dt_ref, dtb_ref, alog_ref, dsk_ref, save_ref, dy_ref,
             dx_ref, db_ref, dc_ref, ddt_ref, dpar_ref, dst_scr):
        c, g = pl.program_id(0), pl.program_id(1)

        @pl.when(c == 0)
        def _():
            dst_scr[g] = jnp.zeros((n, gw), f32)

        @pl.when((c == 0) & (g == 0))
        def _():
            dpar_ref[...] = jnp.zeros_like(dpar_ref)

        fn = functools.partial(_ssd_chunk, e_heads=e_heads)
        _, vjp = jax.vjp(fn, x_ref[...], b_ref[...], c_ref[...], dt_ref[...], save_ref[0],
                         dtb_ref[0], alog_ref[0], dsk_ref[0])
        gx, gb, gc, gdt, gst, gdtb, galog, gdsk = vjp((dy_ref[...], dst_scr[g]))
        dx_ref[...] = gx
        db_ref[...] = gb
        dc_ref[...] = gc
        ddt_ref[...] = gdt
        dst_scr[g] = gst
        dpar_ref[g, 0:1, :] += gdtb
        dpar_ref[g, 1:2, :] += galog
        dpar_ref[g, 2:3, :] += gdsk

    def rc(c):
        return nc - 1 - c

    par = pl.BlockSpec((1, 1, LANE), lambda c, g: (g, 0, 0))
    xs = pl.BlockSpec((ln, gw), lambda c, g: (rc(c), g))
    ns = pl.BlockSpec((ln, n), lambda c, g: (rc(c), g))
    return pl.pallas_call(
        body, name=name, grid=(nc, groups),
        in_specs=[xs,
                  pl.BlockSpec((ln, n), lambda c, g: (rc(c), boff + g)),
                  pl.BlockSpec((ln, n), lambda c, g: (rc(c), boff + groups + g)),
                  pl.BlockSpec((ln, LANE), lambda c, g: (rc(c), g)), par, par, par,
                  pl.BlockSpec((1, n, gw), lambda c, g: (rc(c) * groups + g, 0, 0)), xs],
        out_specs=[xs, ns, ns, pl.BlockSpec((ln, LANE), lambda c, g: (rc(c), g)),
                   pl.BlockSpec((groups, 8, LANE), lambda c, g: (0, 0, 0))],
        out_shape=[jax.ShapeDtypeStruct((t, d_inner), f32),
                   jax.ShapeDtypeStruct((t, groups * n), f32),
                   jax.ShapeDtypeStruct((t, groups * n), f32),
                   jax.ShapeDtypeStruct((t, groups * LANE), f32),
                   jax.ShapeDtypeStruct((groups, 8, LANE), f32)],
        scratch_shapes=[pltpu.VMEM((groups, n, gw), f32)],
        compiler_params=pltpu.CompilerParams(dimension_semantics=("arbitrary", "arbitrary")),
    )(xbc, xbc, xbc, dtp, dtb, alog, dsk, saved, dy)


S5_TOP = 8


def _cmul(ar, ai, br, bi):
    return ar * br - ai * bi, ar * bi + ai * br


def _s5_scan(scr, base, ln, pw_ref, sp, *, reverse):
    s, k = 1, 0
    while s < ln:
        pr = pw_ref[0, k:k + 1, :sp]
        pi = pw_ref[0, k:k + 1, sp:]
        if reverse:
            pi = -pi
            src, dst = base + s, base
        else:
            src, dst = base, base + s
        ar, ai = scr[src:src + ln - s, :sp], scr[src:src + ln - s, sp:]
        mr, mi = _cmul(ar, ai, pr, pi)
        scr[dst:dst + ln - s, :sp] = scr[dst:dst + ln - s, :sp] + mr
        scr[dst:dst + ln - s, sp:] = scr[dst:dst + ln - s, sp:] + mi
        s *= 2
        k += 1


def _s5_states(u_ref, b_ref, lam_ref, pw_ref, scr, carry_r, carry_i, ln, sp):
    scr[S5_TOP:S5_TOP + ln, :] = _dg(u_ref[...], b_ref[0], 1, 0)
    mr, mi = _cmul(carry_r, carry_i, lam_ref[0][:, :sp], lam_ref[0][:, sp:])
    scr[S5_TOP:S5_TOP + 1, :sp] = scr[S5_TOP:S5_TOP + 1, :sp] + mr
    scr[S5_TOP:S5_TOP + 1, sp:] = scr[S5_TOP:S5_TOP + 1, sp:] + mi
    _s5_scan(scr, S5_TOP, ln, pw_ref, sp, reverse=False)


def _s5_dims(t, u_width):
    cbw = S5_BLOCK_GROUPS * S5_GROUP
    return S5_CHUNK, cbw, u_width // cbw, t // S5_CHUNK


def _s5_fwd(u, lamv, pw, bmat, cmat, *, name):
    t, w = u.shape
    ln, cbw, nb, nc = _s5_dims(t, w)
    sp2 = lamv.shape[2]
    sp = sp2 // 2

    def body(u_ref, lam_ref, pw_ref, b_ref, c_ref, ys_ref, xp_ref, scr, carry):
        @pl.when(pl.program_id(1) == 0)
        def _():
            carry[...] = jnp.zeros_like(carry)

        xp_ref[0] = carry[0:1, :]
        _s5_states(u_ref, b_ref, lam_ref, pw_ref, scr, carry[0:1, :sp], carry[0:1, sp:], ln, sp)
        carry[0:1, :] = scr[S5_TOP + ln - 1:S5_TOP + ln, :]
        ys_ref[...] = _dg(scr[S5_TOP:S5_TOP + ln, :], c_ref[0], 1, 0)

    def blk(shape):
        return pl.BlockSpec((1,) + shape, lambda j, c: (j, 0, 0))

    return pl.pallas_call(
        body, name=name, grid=(nb, nc),
        in_specs=[pl.BlockSpec((ln, cbw), lambda j, c: (c, j)), blk((1, sp2)), blk((8, sp2)),
                  blk((cbw, sp2)), blk((sp2, cbw))],
        out_specs=[pl.BlockSpec((ln, cbw), lambda j, c: (c, j)),
                   pl.BlockSpec((1, 1, sp2), lambda j, c: (j * nc + c, 0, 0))],
        out_shape=[jax.ShapeDtypeStruct((t, w), f32), jax.ShapeDtypeStruct((nb * nc, 1, sp2), f32)],
        scratch_shapes=[pltpu.VMEM((S5_TOP + ln, sp2), f32), pltpu.VMEM((8, sp2), f32)],
        compiler_params=pltpu.CompilerParams(dimension_semantics=("parallel", "arbitrary")),
    )(u, lamv, pw, bmat, cmat)


def _s5_bwd(u, dy, xp, lamv, pw, bmat, cmat, *, name):
    t, w = u.shape
    ln, cbw, nb, nc = _s5_dims(t, w)
    sp2 = lamv.shape[2]
    sp = sp2 // 2

    def body(u_ref, dy_ref, xp_ref, lam_ref, pw_ref, b_ref, c_ref,
             du_ref, db_ref, dc_ref, dl_ref, scr, gsc, gcarry):
        @pl.when(pl.program_id(1) == 0)
        def _():
            gcarry[...] = jnp.zeros_like(gcarry)
            db_ref[...] = jnp.zeros_like(db_ref)
            dc_ref[...] = jnp.zeros_like(dc_ref)
            dl_ref[...] = jnp.zeros_like(dl_ref)

        lr, li = lam_ref[0][:, :sp], lam_ref[0][:, sp:]
        xin = xp_ref[0]
        scr[S5_TOP - 1:S5_TOP, :] = xin
        _s5_states(u_ref, b_ref, lam_ref, pw_ref, scr, xin[:, :sp], xin[:, sp:], ln, sp)
        dy = dy_ref[...]
        gsc[0:ln, :] = _dg(dy, c_ref[0], 1, 1)
        mr, mi = _cmul(gcarry[0:1, :sp], gcarry[0:1, sp:], lr, -li)
        gsc[ln - 1:ln, :sp] = gsc[ln - 1:ln, :sp] + mr
        gsc[ln - 1:ln, sp:] = gsc[ln - 1:ln, sp:] + mi
        _s5_scan(gsc, 0, ln, pw_ref, sp, reverse=True)
        gcarry[0:1, :] = gsc[0:1, :]
        g = gsc[0:ln, :]
        du_ref[...] = _dg(g, b_ref[0], 1, 1)
        db_ref[0] += _dg(u_ref[...], g, 0, 0)
        dc_ref[0] += _dg(scr[S5_TOP:S5_TOP + ln, :], dy, 0, 0)
        pr, pi = scr[S5_TOP - 1:S5_TOP - 1 + ln, :sp], scr[S5_TOP - 1:S5_TOP - 1 + ln, sp:]
        gr, gi = g[:, :sp], g[:, sp:]
        dl_ref[0, 0:1, :sp] += jnp.sum(pr * gr + pi * gi, axis=0, keepdims=True)
        dl_ref[0, 0:1, sp:] += jnp.sum(pr * gi - pi * gr, axis=0, keepdims=True)

    def rc(c):
        return nc - 1 - c

    def blk(shape):
        return pl.BlockSpec((1,) + shape, lambda j, c: (j, 0, 0))

    row = pl.BlockSpec((ln, cbw), lambda j, c: (rc(c), j))
    return pl.pallas_call(
        body, name=name, grid=(nb, nc),
        in_specs=[row, row, pl.BlockSpec((1, 1, sp2), lambda j, c: (j * nc + rc(c), 0, 0)),
                  blk((1, sp2)), blk((8, sp2)), blk((cbw, sp2)), blk((sp2, cbw))],
        out_specs=[row, blk((cbw, sp2)), blk((sp2, cbw)), blk((8, sp2))],
        out_shape=[jax.ShapeDtypeStruct((t, w), f32), jax.ShapeDtypeStruct((nb, cbw, sp2), f32),
                   jax.ShapeDtypeStruct((nb, sp2, cbw), f32), jax.ShapeDtypeStruct((nb, 8, sp2), f32)],
        scratch_shapes=[pltpu.VMEM((S5_TOP + ln, sp2), f32), pltpu.VMEM((ln, sp2), f32),
                        pltpu.VMEM((8, sp2), f32)],
        compiler_params=pltpu.CompilerParams(dimension_semantics=("parallel", "arbitrary")),
    )(u, dy, xp, lamv, pw, bmat, cmat)


def _s5_discretize(lre, lim, log_step, bre, bim, cre, cim):
    g, p = lre.shape
    gb = S5_BLOCK_GROUPS
    nb = g // gb
    lam = lax.complex(lre, lim)
    lam_bar = jnp.exp(lam * jnp.exp(log_step)[:, None])
    b_bar = ((lam_bar - 1.0) / lam)[..., None] * lax.complex(bre, bim)
    lb = lam_bar.reshape(nb, 1, gb * p)
    lamv = jnp.concatenate([jnp.real(lb), jnp.imag(lb)], axis=-1)
    eye = jnp.eye(gb, dtype=f32)
    bb = b_bar.reshape(nb, gb, p, S5_GROUP)

    def bdiag(v):
        return jnp.einsum('jgpi,gh->jgihp', v, eye).reshape(nb, gb * S5_GROUP, gb * p)

    bmat = jnp.concatenate([bdiag(jnp.real(bb)), bdiag(jnp.imag(bb))], axis=-1)

    def cdiag(v):
        return jnp.einsum('jgip,gh->jhpgi', v, eye).reshape(nb, gb * p, gb * S5_GROUP)

    cmat = jnp.concatenate([cdiag(cre.reshape(nb, gb, S5_GROUP, p)),
                            -cdiag(cim.reshape(nb, gb, S5_GROUP, p))], axis=1)
    return lamv, bmat, cmat


def _s5_powers(lamv):
    sp = lamv.shape[2] // 2
    pr, pi = lamv[:, :, :sp], lamv[:, :, sp:]
    rows = []
    for _ in range(8):
        rows.append(jnp.concatenate([pr, pi], axis=-1))
        pr, pi = pr * pr - pi * pi, 2.0 * pr * pi
    return lax.stop_gradient(jnp.concatenate(rows, axis=1))


def _ln_fwd(h, out, p, tag):
    return _rw_fwd(_ln_fn, [h, out], [p["ln_g"], p["ln_b"]], [h.shape[1]], name=tag + "_ln")[0]


def _ln_bwd(h, out, p, dh2, tag):
    return _rw_bwd(_ln_fn, [h, out], [p["ln_g"], p["ln_b"]], [dh2], name=tag + "_ln_bwd")


def _ssd_layer_fwd(h, p):
    di, groups = p["wz"].shape[1], p["dtb"].shape[0]
    z = _mm(h, p["wz"], name="ssd_z")
    xr = _mm(h, p["wxbc"], name="ssd_xbc")
    dtp = _mm(h, p["wdt"], name="ssd_dt")
    xbc = _conv_fwd(xr, p["conv_w"], p["conv_b"], name="ssd_conv")
    y, states = _ssd_fwd(xbc, dtp, p["dtb"], p["alog"], p["dsk"], d_inner=di, groups=groups, name="ssd_scan")
    yn = _rw_fwd(_make_ssd_post_fn(groups), [y, z], [p["norm_w"]], [di], name="ssd_post")[0]
    out = _mm(yn, p["wout"], name="ssd_out")
    return _ln_fwd(h, out, p, "ssd"), (h, z, xr, dtp, xbc, states, y, yn, out)


def _ssd_layer_bwd(saved, p, dh2):
    h, z, xr, dtp, xbc, states, y, yn, out = saved
    di, groups = p["wz"].shape[1], p["dtb"].shape[0]
    dh, dout, dg, db = _ln_bwd(h, out, p, dh2, "ssd")
    dyn = _mm(dout, p["wout"], tb=True, name="ssd_out_dx")
    g = {"ln_g": dg, "ln_b": db, "wout": _mm(yn, dout, ta=True, name="ssd_out_dw")}
    dy, dz, g["norm_w"] = _rw_bwd(_make_ssd_post_fn(groups), [y, z], [p["norm_w"]], [dyn], name="ssd_post_bwd")
    dx, dbm, dcm, ddt, dpar = _ssd_bwd(xbc, dtp, p["dtb"], p["alog"], p["dsk"], states, dy,
                                       d_inner=di, groups=groups, name="ssd_scan_bwd")
    g["dtb"], g["alog"], g["dsk"] = dpar[:, 0:1, :], dpar[:, 1:2, :], dpar[:, 2:3, :]
    dxr, g["conv_w"], g["conv_b"] = _conv_bwd(xr, p["conv_w"], p["conv_b"],
                                               jnp.concatenate([dx, dbm, dcm], axis=1), name="ssd_conv_bwd")
    dh = _mm(dz, p["wz"], tb=True, add=dh, name="ssd_z_dx")
    dh = _mm(dxr, p["wxbc"], tb=True, add=dh, name="ssd_xbc_dx")
    dh = _mm(ddt, p["wdt"], tb=True, add=dh, name="ssd_dt_dx")
    g["wz"] = _mm(h, dz, ta=True, name="ssd_z_dw")
    g["wxbc"] = _mm(h, dxr, ta=True, name="ssd_xbc_dw")
    g["wdt"] = _mm(h, ddt, ta=True, name="ssd_dt_dw")
    return dh, g


def _fox_layer_fwd(h, p):
    w = p["wout"].shape[0]
    heads = w // FOX_HEAD_DIM
    t = h.shape[0]
    qkvz = _mm(h, p["wqkvz"], name="fox_qkvz")
    fr = _mm(h, p["wf"], name="fox_f")
    logf = _rw_fwd(_logf_fn, [fr], [p["fb"]], [LANE], name="fox_logf")[0]
    cum = _cumsum_rows(logf, reverse=False, name="fox_cumsum")
    ck = cum[:, :heads].T.reshape(heads, 1, t)
    o, lse = _attn_fwd(qkvz, 0, qkvz, heads, qkvz, 2 * heads, None, None, ck,
                       heads=heads, scale=FOX_HEAD_DIM ** -0.5, name="fox_attn")
    yg = _rw_fwd(_gate_fn, [o, (qkvz, w, 3)], [], [w], name="fox_gate")[0]
    out = _mm(yg, p["wout"], name="fox_out")
    return _ln_fwd(h, out, p, "fox"), (h, qkvz, fr, ck, o, lse, yg, out)


def _fox_layer_bwd(saved, p, dh2):
    h, qkvz, fr, ck, o, lse, yg, out = saved
    w = p["wout"].shape[0]
    heads = w // FOX_HEAD_DIM
    t = h.shape[0]
    dh, dout, dg, db = _ln_bwd(h, out, p, dh2, "fox")
    dyg = _mm(dout, p["wout"], tb=True, name="fox_out_dx")
    g = {"ln_g": dg, "ln_b": db, "wout": _mm(yg, dout, ta=True, name="fox_out_dw")}
    do, dz = _rw_bwd(_gate_fn, [o, (qkvz, w, 3)], [], [dyg], name="fox_gate_bwd")
    delta = _attn_delta(do, o, heads=heads, name="fox_delta")
    kw = dict(heads=heads, scale=FOX_HEAD_DIM ** -0.5)
    dq, drow = _attn_bwd_dq(qkvz, 0, qkvz, heads, qkvz, 2 * heads, None, None, ck, do, lse, delta,
                            name="fox_attn_dq", **kw)
    dk, dv, dck = _attn_bwd_dkv(qkvz, 0, qkvz, heads, qkvz, 2 * heads, None, None, ck, do, lse, delta,
                                name="fox_attn_dkv", **kw)
    dqkvz = jnp.concatenate([dq, dk, dv, dz], axis=1)
    dcum = jnp.pad((drow[:, :, 0] + dck.reshape(heads, t)).T, ((0, 0), (0, LANE - heads)))
    dlogf = _cumsum_rows(dcum, reverse=True, name="fox_cumsum_bwd")
    dfr, g["fb"] = _rw_bwd(_logf_fn, [fr], [p["fb"]], [dlogf], name="fox_logf_bwd")
    dh = _mm(dqkvz, p["wqkvz"], tb=True, add=dh, name="fox_qkvz_dx")
    dh = _mm(dfr, p["wf"], tb=True, add=dh, name="fox_f_dx")
    g["wqkvz"] = _mm(h, dqkvz, ta=True, name="fox_qkvz_dw")
    g["wf"] = _mm(h, dfr, ta=True, name="fox_f_dw")
    return dh, g


def _mla_layer_fwd(h, p, cos, sin):
    rank = p["qn"].shape[1]
    w = p["wout"].shape[0]
    heads = w // MLA_V
    lat = _mm(h, p["wlat"], name="mla_lat")
    kpr = _mm(h, p["wkpe"], name="mla_kpe")
    z = _mm(h, p["wz"], name="mla_z")
    qnl, kvnl = _rw_fwd(_make_mla_norm_fn(rank), [lat], [p["qn"], p["kvn"]], [rank, rank], name="mla_norm")
    qno = _mm(qnl, p["wqn"], name="mla_qn")
    qpr = _mm(qnl, p["wqp"], name="mla_qp")
    kno = _mm(kvnl, p["wkn"], name="mla_kn")
    v = _mm(kvnl, p["wv"], name="mla_v")
    qp = _rw_fwd(_make_rope_fn(heads), [qpr, cos, sin], [], [heads * LANE], name="mla_rope_q")[0]
    kp = _rw_fwd(_make_rope_fn(1), [kpr, cos, sin], [], [LANE], name="mla_rope_k")[0]
    scale = (MLA_NOPE + MLA_ROPE) ** -0.5
    o, lse = _attn_fwd(qno, 0, kno, 0, v, 0, qp, kp, None, heads=heads, scale=scale, name="mla_attn")
    yg = _rw_fwd(_gate_fn, [o, z], [], [w], name="mla_gate")[0]
    out = _mm(yg, p["wout"], name="mla_out")
    return _ln_fwd(h, out, p, "mla"), (h, lat, kpr, z, qnl, kvnl, qno, qpr, kno, v, qp, kp, o, lse, yg, out)


def _mla_layer_bwd(saved, p, dh2, cos, sin):
    h, lat, kpr, z, qnl, kvnl, qno, qpr, kno, v, qp, kp, o, lse, yg, out = saved
    rank = p["qn"].shape[1]
    w = p["wout"].shape[0]
    heads = w // MLA_V
    dh, dout, dg, db = _ln_bwd(h, out, p, dh2, "mla")
    dyg = _mm(dout, p["wout"], tb=True, name="mla_out_dx")
    g = {"ln_g": dg, "ln_b": db, "wout": _mm(yg, dout, ta=True, name="mla_out_dw")}
    do, dz = _rw_bwd(_gate_fn, [o, z], [], [dyg], name="mla_gate_bwd")
    delta = _attn_delta(do, o, heads=heads, name="mla_delta")
    kw = dict(heads=heads, scale=(MLA_NOPE + MLA_ROPE) ** -0.5)
    dqno, dqp = _attn_bwd_dq(qno, 0, kno, 0, v, 0, qp, kp, None, do, lse, delta, name="mla_attn_dq", **kw)
    dkno, dv, dkp = _attn_bwd_dkv(qno, 0, kno, 0, v, 0, qp, kp, None, do, lse, delta, name="mla_attn_dkv", **kw)
    (dqpr,) = _rw_bwd(_make_rope_fn(heads), [qpr, cos, sin], [], [dqp], n_const=2, name="mla_rope_q_bwd")
    (dkpr,) = _rw_bwd(_make_rope_fn(1), [kpr, cos, sin], [], [dkp], n_const=2, name="mla_rope_k_bwd")
    dqnl = _mm(dqno, p["wqn"], tb=True, name="mla_qn_dx")
    dqnl = _mm(dqpr, p["wqp"], tb=True, add=dqnl, name="mla_qp_dx")
    dkvnl = _mm(dkno, p["wkn"], tb=True, name="mla_kn_dx")
    dkvnl = _mm(dv, p["wv"], tb=True, add=dkvnl, name="mla_v_dx")
    g["wqn"] = _mm(qnl, dqno, ta=True, name="mla_qn_dw")
    g["wqp"] = _mm(qnl, dqpr, ta=True, name="mla_qp_dw")
    g["wkn"] = _mm(kvnl, dkno, ta=True, name="mla_kn_dw")
    g["wv"] = _mm(kvnl, dv, ta=True, name="mla_v_dw")
    dlat, g["qn"], g["kvn"] = _rw_bwd(_make_mla_norm_fn(rank), [lat], [p["qn"], p["kvn"]], [dqnl, dkvnl],
                                     name="mla_norm_bwd")
    dh = _mm(dlat, p["wlat"], tb=True, add=dh, name="mla_lat_dx")
    dh = _mm(dkpr, p["wkpe"], tb=True, add=dh, name="mla_kpe_dx")
    dh = _mm(dz, p["wz"], tb=True, add=dh, name="mla_z_dx")
    g["wlat"] = _mm(h, dlat, ta=True, name="mla_lat_dw")
    g["wkpe"] = _mm(h, dkpr, ta=True, name="mla_kpe_dw")
    g["wz"] = _mm(h, dz, ta=True, name="mla_z_dw")
    return dh, g


def _s5_layer_fwd(h, p):
    w = p["wout"].shape[0]
    u = _mm(h, p["wu"], name="s5_u")
    z = _mm(h, p["wz"], name="s5_z")
    ys, xp = _s5_fwd(u, p["lamv"], p["pw"], p["bmat"], p["cmat"], name="s5_scan")
    y1 = _rw_fwd(_s5_act_fn, [ys, u], [p["d"]], [w], name="s5_act")[0]
    gp = _mm(y1, p["wglu"], name="s5_glu")
    y2 = _rw_fwd(_s5_glu_fn, [y1, gp, z], [p["bglu"]], [w], name="s5_gate")[0]
    out = _mm(y2, p["wout"], name="s5_out")
    return _ln_fwd(h, out, p, "s5"), (h, u, z, ys, xp, y1, gp, y2, out)


def _s5_layer_bwd(saved, p, dh2):
    h, u, z, ys, xp, y1, gp, y2, out = saved
    dh, dout, dg, db = _ln_bwd(h, out, p, dh2, "s5")
    dy2 = _mm(dout, p["wout"], tb=True, name="s5_out_dx")
    g = {"ln_g": dg, "ln_b": db, "wout": _mm(y2, dout, ta=True, name="s5_out_dw")}
    dy1, dgp, dz, g["bglu"] = _rw_bwd(_s5_glu_fn, [y1, gp, z], [p["bglu"]], [dy2], name="s5_gate_bwd")
    dy1 = _mm(dgp, p["wglu"], tb=True, add=dy1, name="s5_glu_dx")
    g["wglu"] = _mm(y1, dgp, ta=True, name="s5_glu_dw")
    dys, du, g["d"] = _rw_bwd(_s5_act_fn, [ys, u], [p["d"]], [dy1], name="s5_act_bwd")
    du_s, g["bmat"], g["cmat"], dlam = _s5_bwd(u, dys, xp, p["lamv"], p["pw"], p["bmat"], p["cmat"],
                                               name="s5_scan_bwd")
    g["lamv"] = dlam[:, 0:1, :]
    dh = _mm(du, p["wu"], tb=True, add=dh, name="s5_u_dx")
    dh = _mm(du_s, p["wu"], tb=True, add=dh, name="s5_us_dx")
    dh = _mm(dz, p["wz"], tb=True, add=dh, name="s5_z_dx")
    g["wu"] = _mm(h, du, ta=True, add=_mm(h, du_s, ta=True, name="s5_us_dw"), name="s5_u_dw")
    g["wz"] = _mm(h, dz, ta=True, name="s5_z_dw")
    return dh, g


def _pad_last(a, to):
    return jnp.pad(a, [(0, 0)] * (a.ndim - 1) + [(0, to - a.shape[-1])])


def _row(v):
    return v.reshape(1, -1)


def _prep_ssd(w):
    di, cd, hh = w["ssd_w_out"].shape[0], w["ssd_conv_b"].shape[0], w["ssd_dt_bias"].shape[0]
    groups = (cd - di) // (2 * SSD_STATE)
    e = hh // groups

    def perm(v):
        lead = v.shape[:-1]
        return _pad_last(v.reshape(lead + (groups, e)), LANE).reshape(lead + (groups * LANE,))

    w_in = w["ssd_w_in"]
    return dict(
        wz=w_in[:, :di], wxbc=w_in[:, di:di + cd], wdt=perm(w_in[:, di + cd:]), wout=w["ssd_w_out"],
        conv_w=w["ssd_conv_w"], conv_b=_row(w["ssd_conv_b"]),
        dtb=perm(w["ssd_dt_bias"]).reshape(groups, 1, LANE), alog=perm(w["ssd_a_log"]).reshape(groups, 1, LANE),
        dsk=perm(w["ssd_d"]).reshape(groups, 1, LANE), norm_w=_row(w["ssd_norm_w"]),
        ln_g=_row(w["ln_g"][0]), ln_b=_row(w["ln_b"][0]))


def _prep_fox(w):
    wd = w["fox_w_out"].shape[0]
    w_in = w["fox_w_in"]
    return dict(wqkvz=w_in[:, :4 * wd], wf=_pad_last(w_in[:, 4 * wd:], LANE), wout=w["fox_w_out"],
                fb=_pad_last(_row(w["fox_f_bias"]), LANE), ln_g=_row(w["ln_g"][1]), ln_b=_row(w["ln_b"][1]))


def _prep_mla(w):
    rank = w["mla_q_norm"].shape[0]
    wd = w["mla_w_out"].shape[0]
    heads = wd // MLA_V
    w_in = w["mla_w_in"]
    qu = w["mla_w_q_up"].reshape(rank, heads, MLA_NOPE + MLA_ROPE)
    kvu = w["mla_w_kv_up"].reshape(w["mla_w_kv_up"].shape[0], heads, MLA_NOPE + MLA_V)
    return dict(
        wlat=w_in[:, :2 * rank], wkpe=_pad_last(w_in[:, 2 * rank:2 * rank + MLA_ROPE], LANE),
        wz=w_in[:, 2 * rank + MLA_ROPE:],
        wqn=qu[:, :, :MLA_NOPE].reshape(rank, heads * LANE),
        wqp=_pad_last(qu[:, :, MLA_NOPE:], LANE).reshape(rank, heads * LANE),
        wkn=kvu[:, :, :MLA_NOPE].reshape(-1, heads * LANE), wv=kvu[:, :, MLA_NOPE:].reshape(-1, heads * LANE),
        wout=w["mla_w_out"], qn=_row(w["mla_q_norm"]), kvn=_row(w["mla_kv_norm"]),
        ln_g=_row(w["ln_g"][2]), ln_b=_row(w["ln_b"][2]))


def _prep_s5(w):
    wd = w["s5_w_out"].shape[0]
    w_in = w["s5_w_in"]
    lamv, bmat, cmat = _s5_discretize(w["s5_lambda_re"], w["s5_lambda_im"], w["s5_log_step"],
                                      w["s5_b_re"], w["s5_b_im"], w["s5_c_re"], w["s5_c_im"])
    return dict(wu=w_in[:, :wd], wz=w_in[:, wd:], wglu=w["s5_w_glu"], wout=w["s5_w_out"],
                d=_row(w["s5_d"]), bglu=_row(w["s5_b_glu"]), lamv=lamv, bmat=bmat, cmat=cmat,
                ln_g=_row(w["ln_g"][3]), ln_b=_row(w["ln_b"][3]))


def _rope_tables(positions):
    inv_freq = ROPE_BASE ** (-jnp.arange(0, MLA_ROPE, 2, dtype=f32) / MLA_ROPE)
    ang = positions.astype(f32).reshape(-1, 1) * inv_freq
    return jnp.cos(ang), jnp.sin(ang)


def _local_step(x, positions, tgt, w):
    cos, sin = _rope_tables(positions)
    preps = (_prep_ssd, _prep_fox, _prep_mla, _prep_s5)
    w32 = jax.tree.map(lambda a: a.astype(f32), w)
    ps, vjps = [], []
    for prep in preps:
        ps.append(prep(w))
        vjps.append(jax.vjp(prep, w32)[1])
    ps[3]["pw"] = _s5_powers(ps[3]["lamv"])
    h1, s0 = _ssd_layer_fwd(x, ps[0])
    h2, s1 = _fox_layer_fwd(h1, ps[1])
    h3, s2 = _mla_layer_fwd(h2, ps[2], cos, sin)
    h4, s3 = _s5_layer_fwd(h3, ps[3])
    dh, loss = _loss_head(h4, tgt)
    dh, g3 = _s5_layer_bwd(s3, ps[3], dh)
    dh, g2 = _mla_layer_bwd(s2, ps[2], dh, cos, sin)
    dh, g1 = _fox_layer_bwd(s1, ps[1], dh)
    dh, g0 = _ssd_layer_bwd(s0, ps[0], dh)
    gw = None
    for p, vjp, g in zip(ps, vjps, (g0, g1, g2, g3)):
        ct = {k: g[k] for k in p if k != "pw"}
        (gi,) = vjp(ct)
        gw = gi if gw is None else jax.tree.map(jnp.add, gw, gi)
    return loss, dh, gw


FLAT_COLS = 1024
FLAT_ROW_ALIGN = 128


def _pack(arrs):
    flat = jnp.concatenate([a.reshape(-1) for a in arrs])
    unit = FLAT_COLS * FLAT_ROW_ALIGN
    return jnp.pad(flat, (0, -flat.shape[0] % unit)).reshape(-1, FLAT_COLS)


def _unpack(flat2d, shapes):
    flat = flat2d.reshape(-1)
    out, off = [], 0
    for s in shapes:
        sz = math.prod(s)
        out.append(flat[off:off + sz].reshape(s))
        off += sz
    return out


def _unpack_gathered(gathered, shapes, axes):
    flat = gathered.reshape(N_DEV, -1)
    out, off = [], 0
    for s, ax in zip(shapes, axes):
        sz = math.prod(s)
        seg = flat[:, off:off + sz].reshape((N_DEV,) + tuple(s))
        out.append(jnp.concatenate([seg[d] for d in range(N_DEV)], axis=ax)[0])
        off += sz
    return out


MESH = pl.DeviceIdType.MESH
HBM_SPEC = pl.BlockSpec(memory_space=pl.ANY)


def _all_gather(xl, *, name):
    r, c = xl.shape

    def body(x_ref, out_ref, send_sems, recv_sems, local_sem):
        x, y, cc = lax.axis_index("x"), lax.axis_index("y"), lax.axis_index("c")
        me, sibling = (x, y, cc), (x, y, 1 - cc)
        chips = [(1 - x, y), (x, 1 - y), (1 - x, 1 - y)]

        def rows(px, py, pc):
            return out_ref.at[4 * px + 2 * py + pc]

        def copy(k, block, to, src=None):
            return pltpu.make_async_remote_copy(
                src_ref=rows(*block) if src is None else src, dst_ref=rows(*block),
                send_sem=send_sems.at[k], recv_sem=recv_sems.at[k], device_id=to, device_id_type=MESH)

        mine = pltpu.make_async_copy(x_ref, rows(*me), local_sem)
        mine.start()
        first = [copy(0, me, sibling, src=x_ref)]
        first += [copy(1 + j, me, (*chip, cc), src=x_ref) for j, chip in enumerate(chips)]
        for cp in first:
            cp.start()
        passed = [copy(4 + j, (*chip, cc), sibling) for j, chip in enumerate(chips)]
        for j, chip in enumerate(chips):
            copy(1 + j, (*chip, cc), me).wait_recv()
            passed[j].start()
        copy(0, sibling, me).wait_recv()
        for j, chip in enumerate(chips):
            copy(4 + j, (*chip, 1 - cc), me).wait_recv()
        for cp in first + passed:
            cp.wait_send()
        mine.wait()

    return pl.pallas_call(
        body, name=name, out_shape=jax.ShapeDtypeStruct((N_DEV, r, c), xl.dtype),
        in_specs=[HBM_SPEC], out_specs=HBM_SPEC,
        scratch_shapes=[pltpu.SemaphoreType.DMA((7,)), pltpu.SemaphoreType.DMA((7,)), pltpu.SemaphoreType.DMA],
    )(xl)


def _exchange(g, *, name):
    _, r, c = g.shape

    def body(g_ref, out_ref, send_sems, recv_sems, local_sem):
        x, y, cc = lax.axis_index("x"), lax.axis_index("y"), lax.axis_index("c")
        me = 4 * x + 2 * y + cc
        mine = pltpu.make_async_copy(g_ref.at[me], out_ref.at[me], local_sem)
        mine.start()
        peers = []
        for k in range(1, N_DEV):
            px = 1 - x if (k >> 2) & 1 else x
            py = 1 - y if (k >> 1) & 1 else y
            pc = 1 - cc if k & 1 else cc
            peers.append((k, (px, py, pc), 4 * px + 2 * py + pc))

        def copy(k, peer, slot_src, slot_dst):
            return pltpu.make_async_remote_copy(
                src_ref=g_ref.at[slot_src], dst_ref=out_ref.at[slot_dst],
                send_sem=send_sems.at[k - 1], recv_sem=recv_sems.at[k - 1], device_id=peer, device_id_type=MESH)

        sends = [copy(k, peer, idx, me) for k, peer, idx in peers]
        for cp in sends:
            cp.start()
        for k, peer, idx in peers:
            copy(k, peer, idx, idx).wait_recv()
        for cp in sends:
            cp.wait_send()
        mine.wait()

    return pl.pallas_call(
        body, name=name, out_shape=jax.ShapeDtypeStruct(g.shape, g.dtype),
        in_specs=[HBM_SPEC], out_specs=HBM_SPEC,
        scratch_shapes=[pltpu.SemaphoreType.DMA((7,)), pltpu.SemaphoreType.DMA((7,)), pltpu.SemaphoreType.DMA],
    )(g)


def _sum_blocks(g, *, name):
    _, r, c = g.shape
    tr = FLAT_ROW_ALIGN

    def body(g_ref, o_ref):
        acc = g_ref[0]
        for s in range(1, N_DEV):
            acc = acc + g_ref[s]
        o_ref[...] = acc

    return pl.pallas_call(
        body, name=name, grid=(r // tr,),
        in_specs=[pl.BlockSpec((N_DEV, tr, c), lambda i: (0, i, 0))],
        out_specs=pl.BlockSpec((tr, c), lambda i: (i, 0)),
        out_shape=jax.ShapeDtypeStruct((r, c), f32),
        compiler_params=pltpu.CompilerParams(dimension_semantics=("parallel",)),
    )(g)


def _adamw(w, g, m, v, *, name):
    r, c = w.shape
    tr = FLAT_ROW_ALIGN

    def body(w_ref, g_ref, m_ref, v_ref, d_ref, m2_ref, v2_ref):
        gv = g_ref[...]
        m2 = ADAM_B1 * m_ref[...] + (1.0 - ADAM_B1) * gv
        v2 = ADAM_B2 * v_ref[...] + (1.0 - ADAM_B2) * jnp.square(gv)
        m_hat = m2 / (1.0 - ADAM_B1 ** ADAM_STEP)
        v_hat = v2 / (1.0 - ADAM_B2 ** ADAM_STEP)
        d_ref[...] = -ADAM_LR * (m_hat / (jnp.sqrt(v_hat) + ADAM_EPS) + ADAM_WD * w_ref[...])
        m2_ref[...] = m2
        v2_ref[...] = v2

    spec = pl.BlockSpec((tr, c), lambda i: (i, 0))
    shp = jax.ShapeDtypeStruct((r, c), f32)
    return pl.pallas_call(
        body, name=name, grid=(r // tr,), in_specs=[spec] * 4, out_specs=[spec] * 3, out_shape=[shp] * 3,
        compiler_params=pltpu.CompilerParams(dimension_semantics=("parallel",)),
    )(w, g, m, v)


BIG = (("ssd_w_in", 2), ("ssd_w_out", 1), ("fox_w_in", 2), ("fox_w_out", 1), ("mla_w_in", 2),
       ("mla_w_q_up", 2), ("mla_w_kv_up", 2), ("mla_w_out", 1), ("s5_w_in", 2), ("s5_w_glu", 1), ("s5_w_out", 1))
SMALL = (("ssd_conv_w", 2), ("mla_q_norm", 1), ("mla_kv_norm", 1), ("s5_d", 1), ("s5_b_glu", 1))
REPLICATED = ("ln_g", "ln_b", "ssd_conv_b", "ssd_dt_bias", "ssd_a_log", "ssd_d", "ssd_norm_w", "fox_f_bias",
              "s5_lambda_re", "s5_lambda_im", "s5_log_step", "s5_b_re", "s5_b_im", "s5_c_re", "s5_c_im")
WEIGHT_ORDER = ("ln_g", "ln_b", "ssd_w_in", "ssd_conv_w", "ssd_conv_b", "ssd_dt_bias", "ssd_a_log", "ssd_d",
                "ssd_norm_w", "ssd_w_out", "fox_w_in", "fox_f_bias", "fox_w_out", "mla_w_in", "mla_q_norm",
                "mla_kv_norm", "mla_w_q_up", "mla_w_kv_up", "mla_w_out", "s5_w_in", "s5_lambda_re", "s5_lambda_im",
                "s5_log_step", "s5_b_re", "s5_b_im", "s5_c_re", "s5_c_im", "s5_d", "s5_w_glu", "s5_b_glu", "s5_w_out")


def _train_step(a):
    big_names, big_axes = [n for n, _ in BIG], [ax for _, ax in BIG]
    small_names, small_axes = [n for n, _ in SMALL], [ax for _, ax in SMALL]
    shard_names = big_names + small_names
    big_shapes = [a[n].shape for n in big_names]
    small_shapes = [a[n].shape for n in small_names]
    rep_shapes = [a[n].shape for n in REPLICATED]

    big_flat = _pack([a[n] for n in big_names])
    small_flat = _pack([a[n] for n in small_names])
    big_all = _all_gather(big_flat.astype(bf16), name="gather_matmul_weights")
    small_all = _all_gather(small_flat, name="gather_small_weights")

    def assemble(big_g, small_g):
        w = dict(zip(big_names, _unpack_gathered(big_g, big_shapes, big_axes)))
        w.update(zip(small_names, _unpack_gathered(small_g, small_shapes, small_axes)))
        return w

    w = assemble(big_all, small_all)
    for n in REPLICATED:
        w[n] = a[n] if n in ("ln_g", "ln_b") else a[n][0]

    loss, grad_x, gw = _local_step(a["x"][0], a["positions"][0], a["loss_target"][0], w)

    to_owner = jax.linear_transpose(
        assemble, jax.ShapeDtypeStruct(big_all.shape, f32), jax.ShapeDtypeStruct(small_all.shape, f32))
    g_big, g_small = to_owner({n: gw[n] for n in shard_names})
    parts = _exchange(jnp.concatenate([g_big, g_small], axis=1), name="exchange_gradients")
    g_shard = _sum_blocks(parts, name="sum_gradients")
    g_rep_local = _pack([gw[n] if n in ("ln_g", "ln_b") else gw[n][None] for n in REPLICATED])
    g_rep = _sum_blocks(_all_gather(g_rep_local, name="gather_replicated_gradients"), name="sum_replicated_gradients")

    def flat_shard(prefix):
        return jnp.concatenate([_pack([a[prefix + n] for n in big_names]),
                                _pack([a[prefix + n] for n in small_names])], axis=0)

    def flat_rep(prefix):
        return _pack([a[prefix + n] for n in REPLICATED])

    upd_shard = _adamw(flat_shard(""), g_shard, flat_shard("m_"), flat_shard("v_"), name="adamw_sharded")
    upd_rep = _adamw(flat_rep(""), g_rep, flat_rep("m_"), flat_rep("v_"), name="adamw_replicated")

    rb = big_flat.shape[0]
    out = {"grad": {}, "delta": {}, "new_m": {}, "new_v": {}}
    for kind, fs, fr in (("grad", g_shard, g_rep), ("delta", upd_shard[0], upd_rep[0]),
                         ("new_m", upd_shard[1], upd_rep[1]), ("new_v", upd_shard[2], upd_rep[2])):
        out[kind].update(zip(big_names, _unpack(fs[:rb], big_shapes)))
        out[kind].update(zip(small_names, _unpack(fs[rb:], small_shapes)))
        out[kind].update(zip(REPLICATED, _unpack(fr, rep_shapes)))
    loss_total = lax.psum(loss[0, 0], ("x", "y", "c"))
    return (loss_total, grad_x[None],
            *[out[kind][n] for kind in ("grad", "delta", "new_m", "new_v") for n in WEIGHT_ORDER])


def kernel(x, positions, ln_g, ln_b, ssd_w_in, ssd_conv_w, ssd_conv_b, ssd_dt_bias, ssd_a_log, ssd_d, ssd_norm_w, ssd_w_out, fox_w_in, fox_f_bias, fox_w_out, mla_w_in, mla_q_norm, mla_kv_norm, mla_w_q_up, mla_w_kv_up, mla_w_out, s5_w_in, s5_lambda_re, s5_lambda_im, s5_log_step, s5_b_re, s5_b_im, s5_c_re, s5_c_im, s5_d, s5_w_glu, s5_b_glu, s5_w_out, loss_target, m_ln_g, m_ln_b, m_ssd_w_in, m_ssd_conv_w, m_ssd_conv_b, m_ssd_dt_bias, m_ssd_a_log, m_ssd_d, m_ssd_norm_w, m_ssd_w_out, m_fox_w_in, m_fox_f_bias, m_fox_w_out, m_mla_w_in, m_mla_q_norm, m_mla_kv_norm, m_mla_w_q_up, m_mla_w_kv_up, m_mla_w_out, m_s5_w_in, m_s5_lambda_re, m_s5_lambda_im, m_s5_log_step, m_s5_b_re, m_s5_b_im, m_s5_c_re, m_s5_c_im, m_s5_d, m_s5_w_glu, m_s5_b_glu, m_s5_w_out, v_ln_g, v_ln_b, v_ssd_w_in, v_ssd_conv_w, v_ssd_conv_b, v_ssd_dt_bias, v_ssd_a_log, v_ssd_d, v_ssd_norm_w, v_ssd_w_out, v_fox_w_in, v_fox_f_bias, v_fox_w_out, v_mla_w_in, v_mla_q_norm, v_mla_kv_norm, v_mla_w_q_up, v_mla_w_kv_up, v_mla_w_out, v_s5_w_in, v_s5_lambda_re, v_s5_lambda_im, v_s5_log_step, v_s5_b_re, v_s5_b_im, v_s5_c_re, v_s5_c_im, v_s5_d, v_s5_w_glu, v_s5_b_glu, v_s5_w_out):
    return _train_step(dict(locals()))
```

```python
import functools
import math

import jax
import jax.numpy as jnp
from jax import lax
from jax.experimental import pallas as pl
from jax.experimental.pallas import tpu as pltpu

f32 = jnp.float32
bf16 = jnp.bfloat16

N_DEV = 8
DEPTH = 4
ALPHA = (2.0 * DEPTH) ** 0.25
LN_EPS = 1e-5
RMS_EPS = 1e-6
LANE = 128

SSD_STATE = 128
SSD_HEADDIM = 64
SSD_CHUNK = 128
FOX_HEAD_DIM = 128
MLA_NOPE = 128
MLA_ROPE = 64
MLA_V = 128
ROPE_BASE = 10000.0
S5_GROUP = 16
S5_BLOCK_GROUPS = 8
S5_CHUNK = 128

ADAM_LR = 0.001
ADAM_B1 = 0.9
ADAM_B2 = 0.999
ADAM_EPS = 1e-08
ADAM_WD = 0.01
ADAM_STEP = 10

VMEM_BLOCK_BUDGET = 20 * 1024 * 1024
MM_FULL_K = 2048
HIGHEST = lax.Precision.HIGHEST


def _pick(n, cands):
    for c in cands:
        if n % c == 0:
            return c
    return n


def _dg(a, b, ca, cb):
    return lax.dot_general(a.astype(bf16), b.astype(bf16), (((ca,), (cb,)), ((), ())),
                           preferred_element_type=f32)


@functools.partial(jax.custom_vjp, nondiff_argnums=(2, 3))
def bdot(a, b, ca, cb):
    return _dg(a, b, ca, cb)


def _bdot_fwd(a, b, ca, cb):
    return _dg(a, b, ca, cb), (a, b)


def _bdot_bwd(ca, cb, res, g):
    a, b = res
    nb = 1 - cb
    ma = 1 - ca
    da = _dg(g, b, 1, nb) if ca == 1 else _dg(b, g, nb, 1)
    db = _dg(a, g, ma, 0) if cb == 0 else _dg(g, a, 0, ma)
    return da, db


bdot.defvjp(_bdot_fwd, _bdot_bwd)


def _silu(x):
    return x * jax.nn.sigmoid(x)


def _softplus(x):
    return jnp.maximum(x, 0.0) + jnp.log(1.0 + jnp.exp(-jnp.abs(x)))


def _mm(a, b, *, ta=False, tb=False, add=None, name):
    m, k = (a.shape[1], a.shape[0]) if ta else a.shape
    n = b.shape[0] if tb else b.shape[1]
    kb = b.shape[1] if tb else b.shape[0]
    assert k == kb, (name, a.shape, b.shape)
    if k <= MM_FULL_K:
        tm = _pick(m, (512, 256, 128))
        tn = _pick(n, (1024, 512, 384, 256, 128))
        tk = k
    else:
        tm = _pick(m, (1024, 512, 256, 128))
        tn = _pick(n, (1024, 512, 384, 256, 128))
        tk = _pick(k, (512, 256, 128))
    nk = k // tk
    has_add = add is not None

    def body(*refs):
        if has_add:
            a_ref, b_ref, add_ref, o_ref, acc = refs
        else:
            a_ref, b_ref, o_ref, acc = refs
        kk = pl.program_id(2)

        def prod():
            return _dg(a_ref[...], b_ref[...], 0 if ta else 1, 1 if tb else 0)

        def finish(r):
            o_ref[...] = r + add_ref[...] if has_add else r

        if nk == 1:
            finish(prod())
            return

        @pl.when(kk == 0)
        def _():
            acc[...] = prod()

        @pl.when((kk > 0) & (kk < nk - 1))
        def _():
            acc[...] += prod()

        @pl.when(kk == nk - 1)
        def _():
            finish(acc[...] + prod())

    a_spec = (pl.BlockSpec((tk, tm), lambda i, j, kk: (kk, i)) if ta
              else pl.BlockSpec((tm, tk), lambda i, j, kk: (i, kk)))
    b_spec = (pl.BlockSpec((tn, tk), lambda i, j, kk: (j, kk)) if tb
              else pl.BlockSpec((tk, tn), lambda i, j, kk: (kk, j)))
    o_spec = pl.BlockSpec((tm, tn), lambda i, j, kk: (i, j))
    in_specs = [a_spec, b_spec] + ([o_spec] if has_add else [])
    args = (a, b) + ((add,) if has_add else ())
    return pl.pallas_call(
        body, name=name, grid=(m // tm, n // tn, nk),
        in_specs=in_specs, out_specs=o_spec,
        out_shape=jax.ShapeDtypeStruct((m, n), f32),
        scratch_shapes=[pltpu.VMEM((tm, tn), f32)],
        compiler_params=pltpu.CompilerParams(dimension_semantics=("parallel", "parallel", "arbitrary")),
    )(*args)


def _row_operand(op):
    if isinstance(op, tuple):
        arr, w, cb = op
    else:
        arr, w, cb = op, op.shape[1], 0
    return arr, w, cb


def _rw_tm(t, widths):
    per_row = 2 * 4 * sum(widths)
    tm = 512
    while tm > 8 and (tm * per_row > VMEM_BLOCK_BUDGET or t % tm):
        tm //= 2
    return tm


def _rw_specs(ops, tm):
    specs, arrs = [], []
    for op in ops:
        arr, w, cb = _row_operand(op)
        specs.append(pl.BlockSpec((tm, w), functools.partial(lambda i, cb: (i, cb), cb=cb)))
        arrs.append(arr)
    return specs, arrs


def _param_spec(p):
    nd = p.ndim
    return pl.BlockSpec(p.shape, lambda i: (0,) * nd)


def _rw_fwd(fn, rows, params, out_widths, *, name):
    t = _row_operand(rows[0])[0].shape[0]
    tm = _rw_tm(t, [_row_operand(r)[1] for r in rows] + list(out_widths))
    nr, npar = len(rows), len(params)

    def body(*refs):
        vals = [r[...] for r in refs[:nr + npar]]
        outs = fn(*vals)
        for o_ref, v in zip(refs[nr + npar:], outs):
            o_ref[...] = v

    rspecs, rarrs = _rw_specs(rows, tm)
    return pl.pallas_call(
        body, name=name, grid=(t // tm,),
        in_specs=rspecs + [_param_spec(p) for p in params],
        out_specs=[pl.BlockSpec((tm, w), lambda i: (i, 0)) for w in out_widths],
        out_shape=[jax.ShapeDtypeStruct((t, w), f32) for w in out_widths],
        compiler_params=pltpu.CompilerParams(dimension_semantics=("parallel",)),
    )(*rarrs, *params)


def _rw_bwd(fn, rows, params, cots, *, name, n_const=0):
    t = _row_operand(rows[0])[0].shape[0]
    nr, npar, nc = len(rows), len(params), len(cots)
    nd = nr - n_const
    widths = [_row_operand(r)[1] for r in rows]
    tm = _rw_tm(t, widths + widths[:nd] + [c.shape[1] for c in cots])

    def body(*refs):
        rv = [r[...] for r in refs[:nr]]
        pv = [r[...] for r in refs[nr:nr + npar]]
        cv = [r[...] for r in refs[nr + npar:nr + npar + nc]]
        outs = refs[nr + npar + nc:]
        consts = rv[nd:]

        def f(*diff):
            return fn(*diff[:nd], *consts, *diff[nd:])

        _, vjp = jax.vjp(f, *rv[:nd], *pv)
        g = vjp(tuple(cv))
        for o_ref, v in zip(outs[:nd], g[:nd]):
            o_ref[...] = v
        if npar:
            @pl.when(pl.program_id(0) == 0)
            def _():
                for o_ref in outs[nd:]:
                    o_ref[...] = jnp.zeros_like(o_ref)

            for o_ref, v in zip(outs[nd:], g[nd:]):
                o_ref[...] += v

    rspecs, rarrs = _rw_specs(rows, tm)
    cspecs = [pl.BlockSpec((tm, c.shape[1]), lambda i: (i, 0)) for c in cots]
    return pl.pallas_call(
        body, name=name, grid=(t // tm,),
        in_specs=rspecs + [_param_spec(p) for p in params] + cspecs,
        out_specs=[pl.BlockSpec((tm, w), lambda i: (i, 0)) for w in widths[:nd]]
        + [_param_spec(p) for p in params],
        out_shape=[jax.ShapeDtypeStruct((t, w), f32) for w in widths[:nd]]
        + [jax.ShapeDtypeStruct(p.shape, f32) for p in params],
        compiler_params=pltpu.CompilerParams(dimension_semantics=("arbitrary",)),
    )(*rarrs, *params, *cots)


def _ln_fn(h, o, g, b):
    x = ALPHA * h + o
    mu = jnp.mean(x, -1, keepdims=True)
    var = jnp.mean(jnp.square(x - mu), -1, keepdims=True)
    return ((x - mu) * lax.rsqrt(var + LN_EPS) * g + b,)


def _gate_fn(o, z):
    return (o * _silu(z),)


def _make_ssd_post_fn(groups):
    def fn(y, z, nw):
        yz = y * _silu(z)
        gw = yz.shape[1] // groups
        outs = []
        for g in range(groups):
            blk = yz[:, g * gw:(g + 1) * gw]
            outs.append(blk * lax.rsqrt(jnp.mean(jnp.square(blk), -1, keepdims=True) + RMS_EPS))
        return (jnp.concatenate(outs, axis=1) * nw,)
    return fn


def _logf_fn(f, b):
    return (jax.nn.log_sigmoid(f + b),)


def _make_mla_norm_fn(rank):
    def fn(lat, qn, kvn):
        def rms(x, w):
            return x * lax.rsqrt(jnp.mean(jnp.square(x), -1, keepdims=True) + RMS_EPS) * w
        return rms(lat[:, :rank], qn), rms(lat[:, rank:], kvn)
    return fn


def _rope_block(xb, cos, sin):
    half = MLA_ROPE // 2
    x1, x2 = xb[:, :half], xb[:, half:MLA_ROPE]
    return jnp.concatenate([x1 * cos - x2 * sin, x1 * sin + x2 * cos, xb[:, MLA_ROPE:]], axis=1)


def _make_rope_fn(heads):
    def fn(x, cos, sin):
        return (jnp.concatenate([_rope_block(x[:, h * LANE:(h + 1) * LANE], cos, sin)
                                 for h in range(heads)], axis=1),)
    return fn


def _s5_act_fn(ys, u, d):
    return (jax.nn.gelu(ys + d * u),)


def _s5_glu_fn(y1, gp, z, bg):
    return (y1 * jax.nn.sigmoid(gp + bg) * _silu(z),)


def _loss_head(y, tgt):
    t, d = y.shape
    tm = _rw_tm(t, [d, d, d])

    def body(y_ref, t_ref, dy_ref, l_ref):
        @pl.when(pl.program_id(0) == 0)
        def _():
            l_ref[...] = jnp.zeros_like(l_ref)

        e = y_ref[...] - t_ref[...]
        dy_ref[...] = e * (1.0 / d)
        l_ref[...] += 0.5 * jnp.sum(jnp.mean(jnp.square(e), axis=-1, keepdims=True), axis=0, keepdims=True)

    spec = pl.BlockSpec((tm, d), lambda i: (i, 0))
    return pl.pallas_call(
        body, name="loss_head", grid=(t // tm,), in_specs=[spec, spec],
        out_specs=[spec, pl.BlockSpec((1, 1), lambda i: (0, 0))],
        out_shape=[jax.ShapeDtypeStruct((t, d), f32), jax.ShapeDtypeStruct((1, 1), f32)],
        compiler_params=pltpu.CompilerParams(dimension_semantics=("arbitrary",)),
    )(y, tgt)


def _attn_tile(t):
    return _pick(t, (512, 256, 128))


ATTN_SUB = 512


def _logits(q1, k1, q2, k2, ck, scale, row0, col0):
    s = _dg(q1, k1, 1, 1)
    if q2 is not None:
        s = s + _dg(q2, k2, 1, 1)
    s = s * scale
    if ck is not None:
        s = s - ck
    row = row0 + lax.broadcasted_iota(jnp.int32, s.shape, 0)
    col = col0 + lax.broadcasted_iota(jnp.int32, s.shape, 1)
    return jnp.where(col <= row, s, -jnp.inf)


def _sub_blocks(tq):
    sub = min(ATTN_SUB, tq)
    return [(r * sub, pl.ds(r * sub, sub)) for r in range(tq // sub)]


def _attn_fwd(q1, q1o, k1, k1o, v, vo, q2, k2, ck, *, heads, scale, name):
    t = q1.shape[0]
    tq = tk = _attn_tile(t)
    nq = t // tq
    has2, hasb = q2 is not None, ck is not None

    def body(*refs):
        it = iter(refs)
        q1r, k1r, vr = next(it), next(it), next(it)
        q2r, k2r = (next(it), next(it)) if has2 else (None, None)
        ckr = next(it) if hasb else None
        o_r, lse_r, m_s, l_s, acc = next(it), next(it), next(it), next(it), next(it)
        i, j = pl.program_id(1), pl.program_id(2)

        @pl.when(j == 0)
        def _():
            m_s[...] = jnp.full_like(m_s, -jnp.inf)
            l_s[...] = jnp.zeros_like(l_s)
            acc[...] = jnp.zeros_like(acc)

        @pl.when(j <= i)
        def _():
            k1v, vv = k1r[...].astype(bf16), vr[...].astype(bf16)
            k2v = k2r[...].astype(bf16) if has2 else None
            ckv = ckr[0] if hasb else None
            for r0, rs in _sub_blocks(tq):
                s = _logits(q1r[rs, :], k1v, q2r[rs, :] if has2 else None, k2v, ckv, scale,
                            i * tq + r0, j * tk)
                m_prev = m_s[rs, :]
                m_new = jnp.maximum(m_prev, jnp.max(s, axis=1, keepdims=True))
                p = jnp.exp(s - m_new)
                a = jnp.exp(m_prev - m_new)
                l_s[rs, :] = a * l_s[rs, :] + jnp.sum(p, axis=1, keepdims=True)
                acc[rs, :] = a * acc[rs, :] + _dg(p, vv, 1, 0)
                m_s[rs, :] = m_new

        @pl.when(j == i)
        def _():
            o_r[...] = acc[...] / l_s[...]
            lse_r[0] = jnp.broadcast_to(m_s[...] + jnp.log(l_s[...]), (tq, LANE))

    def qmap(off):
        return lambda h, i, j: (i, off + h)

    def kmap(off):
        return lambda h, i, j: (jnp.minimum(j, i), off + h)

    in_specs = [pl.BlockSpec((tq, LANE), qmap(q1o)), pl.BlockSpec((tk, LANE), kmap(k1o)),
                pl.BlockSpec((tk, LANE), kmap(vo))]
    args = [q1, k1, v]
    if has2:
        in_specs += [pl.BlockSpec((tq, LANE), qmap(0)),
                     pl.BlockSpec((tk, LANE), lambda h, i, j: (jnp.minimum(j, i), 0))]
        args += [q2, k2]
    if hasb:
        in_specs += [pl.BlockSpec((1, 1, tk), lambda h, i, j: (h, 0, jnp.minimum(j, i)))]
        args += [ck]
    return pl.pallas_call(
        body, name=name, grid=(heads, nq, nq), in_specs=in_specs,
        out_specs=[pl.BlockSpec((tq, LANE), lambda h, i, j: (i, h)),
                   pl.BlockSpec((1, tq, LANE), lambda h, i, j: (h, i, 0))],
        out_shape=[jax.ShapeDtypeStruct((t, heads * LANE), f32),
                   jax.ShapeDtypeStruct((heads, t, LANE), f32)],
        scratch_shapes=[pltpu.VMEM((tq, 1), f32), pltpu.VMEM((tq, 1), f32), pltpu.VMEM((tq, LANE), f32)],
        compiler_params=pltpu.CompilerParams(dimension_semantics=("parallel", "parallel", "arbitrary")),
    )(*args)


def _attn_delta(do, o, *, heads, name):
    t = do.shape[0]
    tq = _attn_tile(t)

    def body(do_ref, o_ref, d_ref):
        d_ref[0] = jnp.broadcast_to(jnp.sum(do_ref[...] * o_ref[...], axis=1, keepdims=True), (tq, LANE))

    spec = pl.BlockSpec((tq, LANE), lambda i, h: (i, h))
    return pl.pallas_call(
        body, name=name, grid=(t // tq, heads), in_specs=[spec, spec],
        out_specs=pl.BlockSpec((1, tq, LANE), lambda i, h: (h, i, 0)),
        out_shape=jax.ShapeDtypeStruct((heads, t, LANE), f32),
        compiler_params=pltpu.CompilerParams(dimension_semantics=("parallel", "parallel")),
    )(do, o)


def _attn_bwd_dq(q1, q1o, k1, k1o, v, vo, q2, k2, ck, do, lse, delta, *, heads, scale, name):
    t = q1.shape[0]
    tq = tk = _attn_tile(t)
    nq = t // tq
    has2, hasb = q2 is not None, ck is not None

    def body(*refs):
        it = iter(refs)
        q1r, k1r, vr = next(it), next(it), next(it)
        q2r, k2r = (next(it), next(it)) if has2 else (None, None)
        ckr = next(it) if hasb else None
        dor, lser, dlr = next(it), next(it), next(it)
        dq1r = next(it)
        dq2r = next(it) if has2 else None
        drowr = next(it) if hasb else None
        a1 = next(it)
        a2 = next(it) if has2 else None
        ar = next(it) if hasb else None
        i, j = pl.program_id(1), pl.program_id(2)

        @pl.when(j == 0)
        def _():
            a1[...] = jnp.zeros_like(a1)
            if has2:
                a2[...] = jnp.zeros_like(a2)
            if hasb:
                ar[...] = jnp.zeros_like(ar)

        @pl.when(j <= i)
        def _():
            k1v, vv = k1r[...].astype(bf16), vr[...].astype(bf16)
            k2v = k2r[...].astype(bf16) if has2 else None
            ckv = ckr[0] if hasb else None
            for r0, rs in _sub_blocks(tq):
                s = _logits(q1r[rs, :], k1v, q2r[rs, :] if has2 else None, k2v, ckv, scale,
                            i * tq + r0, j * tk)
                p = jnp.exp(s - lser[0, rs, :][:, 0:1])
                dp = _dg(dor[rs, :], vv, 1, 1)
                dsr = p * (dp - dlr[0, rs, :][:, 0:1])
                ds = (dsr * scale).astype(bf16)
                a1[rs, :] += _dg(ds, k1v, 1, 0)
                if has2:
                    a2[rs, :] += _dg(ds, k2v, 1, 0)
                if hasb:
                    ar[rs, :] += jnp.sum(dsr, axis=1, keepdims=True)

        @pl.when(j == i)
        def _():
            dq1r[...] = a1[...]
            if has2:
                dq2r[...] = a2[...]
            if hasb:
                drowr[0] = jnp.broadcast_to(ar[...], (tq, LANE))

    def qmap(off):
        return lambda h, i, j: (i, off + h)

    def kmap(off):
        return lambda h, i, j: (jnp.minimum(j, i), off + h)

    in_specs = [pl.BlockSpec((tq, LANE), qmap(q1o)), pl.BlockSpec((tk, LANE), kmap(k1o)),
                pl.BlockSpec((tk, LANE), kmap(vo))]
    args = [q1, k1, v]
    if has2:
        in_specs += [pl.BlockSpec((tq, LANE), qmap(0)),
                     pl.BlockSpec((tk, LANE), lambda h, i, j: (jnp.minimum(j, i), 0))]
        args += [q2, k2]
    if hasb:
        in_specs += [pl.BlockSpec((1, 1, tk), lambda h, i, j: (h, 0, jnp.minimum(j, i)))]
        args += [ck]
    row3 = pl.BlockSpec((1, tq, LANE), lambda h, i, j: (h, i, 0))
    in_specs += [pl.BlockSpec((tq, LANE), qmap(0)), row3, row3]
    args += [do, lse, delta]
    n_out = 2 if has2 else 1
    oshape = jax.ShapeDtypeStruct((t, heads * LANE), f32)
    out_specs = [pl.BlockSpec((tq, LANE), qmap(0))] * n_out
    out_shape = [oshape] * n_out
    scratch = [pltpu.VMEM((tq, LANE), f32)] * n_out
    if hasb:
        out_specs.append(row3)
        out_shape.append(jax.ShapeDtypeStruct((heads, t, LANE), f32))
        scratch.append(pltpu.VMEM((tq, 1), f32))
    return pl.pallas_call(
        body, name=name, grid=(heads, nq, nq), in_specs=in_specs,
        out_specs=out_specs, out_shape=out_shape, scratch_shapes=scratch,
        compiler_params=pltpu.CompilerParams(dimension_semantics=("parallel", "parallel", "arbitrary")),
    )(*args)


def _attn_bwd_dkv(q1, q1o, k1, k1o, v, vo, q2, k2, ck, do, lse, delta, *, heads, scale, name):
    t = q1.shape[0]
    tq = tk = _attn_tile(t)
    nq = t // tq
    has2, hasb = q2 is not None, ck is not None

    def body(*refs):
        it = iter(refs)
        q1r, k1r, vr = next(it), next(it), next(it)
        q2r, k2r = (next(it), next(it)) if has2 else (None, None)
        ckr = next(it) if hasb else None
        dor, lser, dlr = next(it), next(it), next(it)
        dk1r, dvr = next(it), next(it)
        dk2r = next(it) if has2 else None
        dckr = next(it) if hasb else None
        ak, av = next(it), next(it)
        ak2 = next(it) if has2 else None
        ac = next(it) if hasb else None
        j, h, i = pl.program_id(0), pl.program_id(1), pl.program_id(2)

        @pl.when(i == j)
        def _():
            ak[...] = jnp.zeros_like(ak)
            av[...] = jnp.zeros_like(av)
            if hasb:
                ac[...] = jnp.zeros_like(ac)

        if has2:
            @pl.when((i == j) & (h == 0))
            def _():
                ak2[...] = jnp.zeros_like(ak2)

        @pl.when(i >= j)
        def _():
            k1v, vv = k1r[...].astype(bf16), vr[...].astype(bf16)
            k2v = k2r[...].astype(bf16) if has2 else None
            ckv = ckr[0] if hasb else None
            for r0, rs in _sub_blocks(tq):
                q1v = q1r[rs, :].astype(bf16)
                q2v = q2r[rs, :].astype(bf16) if has2 else None
                dov = dor[rs, :].astype(bf16)
                s = _logits(q1v, k1v, q2v, k2v, ckv, scale, i * tq + r0, j * tk)
                p = jnp.exp(s - lser[0, rs, :][:, 0:1])
                dp = _dg(dov, vv, 1, 1)
                dsr = p * (dp - dlr[0, rs, :][:, 0:1])
                ds = (dsr * scale).astype(bf16)
                av[...] += _dg(p, dov, 0, 0)
                ak[...] += _dg(ds, q1v, 0, 0)
                if has2:
                    ak2[...] += _dg(ds, q2v, 0, 0)
                if hasb:
                    ac[...] -= jnp.sum(dsr, axis=0, keepdims=True)

        @pl.when(i == nq - 1)
        def _():
            dk1r[...] = ak[...]
            dvr[...] = av[...]
            if hasb:
                dckr[0] = ac[...]

        if has2:
            @pl.when((i == nq - 1) & (h == heads - 1))
            def _():
                dk2r[...] = ak2[...]

    def qmap(off):
        return lambda j, h, i: (jnp.maximum(i, j), off + h)

    def kmap(off):
        return lambda j, h, i: (j, off + h)

    in_specs = [pl.BlockSpec((tq, LANE), qmap(q1o)), pl.BlockSpec((tk, LANE), kmap(k1o)),
                pl.BlockSpec((tk, LANE), kmap(vo))]
    args = [q1, k1, v]
    if has2:
        in_specs += [pl.BlockSpec((tq, LANE), qmap(0)), pl.BlockSpec((tk, LANE), lambda j, h, i: (j, 0))]
        args += [q2, k2]
    if hasb:
        in_specs += [pl.BlockSpec((1, 1, tk), lambda j, h, i: (h, 0, j))]
        args += [ck]
    row3 = pl.BlockSpec((1, tq, LANE), lambda j, h, i: (h, jnp.maximum(i, j), 0))
    in_specs += [pl.BlockSpec((tq, LANE), qmap(0)), row3, row3]
    args += [do, lse, delta]
    hd = jax.ShapeDtypeStruct((t, heads * LANE), f32)
    out_specs = [pl.BlockSpec((tk, LANE), kmap(0))] * 2
    out_shape = [hd, hd]
    scratch = [pltpu.VMEM((tk, LANE), f32)] * 2
    if has2:
        out_specs.append(pl.BlockSpec((tk, LANE), lambda j, h, i: (j, 0)))
        out_shape.append(jax.ShapeDtypeStruct((t, LANE), f32))
        scratch.append(pltpu.VMEM((tk, LANE), f32))
    if hasb:
        out_specs.append(pl.BlockSpec((1, 1, tk), lambda j, h, i: (h, 0, j)))
        out_shape.append(jax.ShapeDtypeStruct((heads, 1, t), f32))
        scratch.append(pltpu.VMEM((1, tk), f32))
    return pl.pallas_call(
        body, name=name, grid=(nq, heads, nq), in_specs=in_specs, out_specs=out_specs, out_shape=out_shape,
        scratch_shapes=scratch,
        compiler_params=pltpu.CompilerParams(dimension_semantics=("arbitrary", "arbitrary", "arbitrary")),
    )(*args)


def _cumsum_rows(x, *, reverse, name):
    t, w = x.shape
    blk = 128
    nb = t // blk

    def body(x_ref, o_ref):
        r = lax.broadcasted_iota(jnp.int32, (blk, blk), 0)
        c = lax.broadcasted_iota(jnp.int32, (blk, blk), 1)
        tri = ((r <= c) if reverse else (r >= c)).astype(f32)
        carry = jnp.zeros((1, w), f32)
        for b in (reversed(range(nb)) if reverse else range(nb)):
            seg = x_ref[b * blk:(b + 1) * blk, :]
            cs = jnp.dot(tri, seg, precision=HIGHEST, preferred_element_type=f32) + carry
            o_ref[b * blk:(b + 1) * blk, :] = cs
            carry = cs[0:1, :] if reverse else cs[blk - 1:blk, :]

    return pl.pallas_call(body, name=name, out_shape=jax.ShapeDtypeStruct((t, w), f32))(x)


CONV_PAD = 8


def _conv_fn(x, w, b):
    t, taps = x.shape[0], w.shape[0]
    xx = jnp.concatenate([jnp.zeros((CONV_PAD, x.shape[1]), f32), x], axis=0)
    y = b
    for k in range(taps):
        off = CONV_PAD - (taps - 1) + k
        y = y + w[k:k + 1, :] * xx[off:off + t, :]
    return _silu(y)


def _conv_fwd(x, w, b, *, name):
    t, c = x.shape
    tc = LANE
    taps = w.shape[0]

    def body(x_ref, w_ref, b_ref, o_ref):
        o_ref[...] = _conv_fn(x_ref[...], w_ref[...], b_ref[...])

    xs = pl.BlockSpec((t, tc), lambda i: (0, i))
    return pl.pallas_call(
        body, name=name, grid=(c // tc,),
        in_specs=[xs, pl.BlockSpec((taps, tc), lambda i: (0, i)), pl.BlockSpec((1, tc), lambda i: (0, i))],
        out_specs=xs, out_shape=jax.ShapeDtypeStruct((t, c), f32),
        compiler_params=pltpu.CompilerParams(dimension_semantics=("parallel",)),
    )(x, w, b)


def _conv_bwd(x, w, b, ds, *, name):
    t, c = x.shape
    tc = LANE
    taps = w.shape[0]

    def body(x_ref, w_ref, b_ref, ds_ref, dx_ref, dw_ref, db_ref):
        _, vjp = jax.vjp(_conv_fn, x_ref[...], w_ref[...], b_ref[...])
        dx, dw, db = vjp(ds_ref[...])
        dx_ref[...] = dx
        dw_ref[...] = dw
        db_ref[...] = db

    xs = pl.BlockSpec((t, tc), lambda i: (0, i))
    ws = pl.BlockSpec((taps, tc), lambda i: (0, i))
    bs = pl.BlockSpec((1, tc), lambda i: (0, i))
    return pl.pallas_call(
        body, name=name, grid=(c // tc,), in_specs=[xs, ws, bs, xs], out_specs=[xs, ws, bs],
        out_shape=[jax.ShapeDtypeStruct((t, c), f32), jax.ShapeDtypeStruct((taps, c), f32),
                   jax.ShapeDtypeStruct((1, c), f32)],
        compiler_params=pltpu.CompilerParams(dimension_semantics=("parallel",)),
    )(x, w, b, ds)


def _ssd_chunk(x, bm, cm, dtr, st, dtb, alog, dsk, *, e_heads):
    ln, p = x.shape[0], SSD_HEADDIM
    dt = _softplus(dtr + dtb)
    da = dt * (-jnp.exp(alog))
    r = lax.broadcasted_iota(jnp.int32, (ln, ln), 0)
    c = lax.broadcasted_iota(jnp.int32, (ln, ln), 1)
    lower = r >= c
    acs = jnp.dot(lower.astype(f32), da, precision=HIGHEST, preferred_element_type=f32)
    acs_t = acs.T

    def per_head(v):
        return jnp.concatenate([jnp.broadcast_to(v[:, e:e + 1], (v.shape[0], p)) for e in range(e_heads)], axis=1)

    dt_x, acs_x = per_head(dt), per_head(acs)
    last_x = acs_x[ln - 1:ln, :]
    xdt = x * dt_x
    cb = bdot(cm, bm, 1, 1)
    y_off = bdot(cm, st, 1, 0) * jnp.exp(acs_x)
    st_new = st * jnp.exp(last_x) + bdot(bm, xdt * jnp.exp(last_x - acs_x), 0, 0)
    ys = []
    for e in range(e_heads):
        seg = acs[:, e:e + 1] - acs_t[e:e + 1, :]
        dec = jnp.exp(jnp.where(lower, seg, -jnp.inf))
        ys.append(bdot(cb * dec, xdt[:, e * p:(e + 1) * p], 1, 0))
    y = jnp.concatenate(ys, axis=1) + y_off + per_head(dsk) * x
    return y, st_new


def _ssd_specs(t, d_inner, groups):
    ln, n = SSD_CHUNK, SSD_STATE
    gw = d_inner // groups
    nc = t // ln
    return ln, n, gw, nc


def _ssd_fwd(xbc, dtp, dtb, alog, dsk, *, d_inner, groups, name):
    t = xbc.shape[0]
    ln, n, gw, nc = _ssd_specs(t, d_inner, groups)
    e_heads = gw // SSD_HEADDIM
    boff = d_inner // n

    def body(x_ref, b_ref, c_ref, dt_ref, dtb_ref, alog_ref, dsk_ref, y_ref, save_ref, st_scr):
        c, g = pl.program_id(0), pl.program_id(1)

        @pl.when(c == 0)
        def _():
            st_scr[g] = jnp.zeros((n, gw), f32)

        st = st_scr[g]
        save_ref[0] = st
        y, st_new = _ssd_chunk(x_ref[...], b_ref[...], c_ref[...], dt_ref[...], st,
                               dtb_ref[0], alog_ref[0], dsk_ref[0], e_heads=e_heads)
        y_ref[...] = y
        st_scr[g] = st_new

    par = pl.BlockSpec((1, 1, LANE), lambda c, g: (g, 0, 0))
    return pl.pallas_call(
        body, name=name, grid=(nc, groups),
        in_specs=[pl.BlockSpec((ln, gw), lambda c, g: (c, g)),
                  pl.BlockSpec((ln, n), lambda c, g: (c, boff + g)),
                  pl.BlockSpec((ln, n), lambda c, g: (c, boff + groups + g)),
                  pl.BlockSpec((ln, LANE), lambda c, g: (c, g)), par, par, par],
        out_specs=[pl.BlockSpec((ln, gw), lambda c, g: (c, g)),
                   pl.BlockSpec((1, n, gw), lambda c, g: (c * groups + g, 0, 0))],
        out_shape=[jax.ShapeDtypeStruct((t, d_inner), f32),
                   jax.ShapeDtypeStruct((nc * groups, n, gw), f32)],
        scratch_shapes=[pltpu.VMEM((groups, n, gw), f32)],
        compiler_params=pltpu.CompilerParams(dimension_semantics=("arbitrary", "arbitrary")),
    )(xbc, xbc, xbc, dtp, dtb, alog, dsk)


def _ssd_bwd(xbc, dtp, dtb, alog, dsk, saved, dy, *, d_inner, groups, name):
    t = xbc.shape[0]
    ln, n, gw, nc = _ssd_specs(t, d_inner, groups)
    e_heads = gw // SSD_HEADDIM
    boff = d_inner // n

    def body(x_ref, b_ref, c_ref, dt_ref, dtb_ref, alog_ref, dsk_ref, save_ref, dy_ref,
             dx_ref, db_ref, dc_ref, ddt_ref, dpar_ref, dst_scr):
        c, g = pl.program_id(0), pl.program_id(1)

        @pl.when(c == 0)
        def _():
            dst_scr[g] = jnp.zeros((n, gw), f32)

        @pl.when((c == 0) & (g == 0))
        def _():
            dpar_ref[...] = jnp.zeros_like(dpar_ref)

        fn = functools.partial(_ssd_chunk, e_heads=e_heads)
        _, vjp = jax.vjp(fn, x_ref[...], b_ref[...], c_ref[...], dt_ref[...], save_ref[0],
                         dtb_ref[0], alog_ref[0], dsk_ref[0])
        gx, gb, gc, gdt, gst, gdtb, galog, gdsk = vjp((dy_ref[...], dst_scr[g]))
        dx_ref[...] = gx
        db_ref[...] = gb
        dc_ref[...] = gc
        ddt_ref[...] = gdt
        dst_scr[g] = gst
        dpar_ref[g, 0:1, :] += gdtb
        dpar_ref[g, 1:2, :] += galog
        dpar_ref[g, 2:3, :] += gdsk

    def rc(c):
        return nc - 1 - c

    par = pl.BlockSpec((1, 1, LANE), lambda c, g: (g, 0, 0))
    xs = pl.BlockSpec((ln, gw), lambda c, g: (rc(c), g))
    ns = pl.BlockSpec((ln, n), lambda c, g: (rc(c), g))
    return pl.pallas_call(
        body, name=name, grid=(nc, groups),
        in_specs=[xs,
                  pl.BlockSpec((ln, n), lambda c, g: (rc(c), boff + g)),
                  pl.BlockSpec((ln, n), lambda c, g: (rc(c), boff + groups + g)),
                  pl.BlockSpec((ln, LANE), lambda c, g: (rc(c), g)), par, par, par,
                  pl.BlockSpec((1, n, gw), lambda c, g: (rc(c) * groups + g, 0, 0)), xs],
        out_specs=[xs, ns, ns, pl.BlockSpec((ln, LANE), lambda c, g: (rc(c), g)),
                   pl.BlockSpec((groups, 8, LANE), lambda c, g: (0, 0, 0))],
        out_shape=[jax.ShapeDtypeStruct((t, d_inner), f32),
                   jax.ShapeDtypeStruct((t, groups * n), f32),
                   jax.ShapeDtypeStruct((t, groups * n), f32),
                   jax.ShapeDtypeStruct((t, groups * LANE), f32),
                   jax.ShapeDtypeStruct((groups, 8, LANE), f32)],
        scratch_shapes=[pltpu.VMEM((groups, n, gw), f32)],
        compiler_params=pltpu.CompilerParams(dimension_semantics=("arbitrary", "arbitrary")),
    )(xbc, xbc, xbc, dtp, dtb, alog, dsk, saved, dy)


S5_TOP = 8


def _cmul(ar, ai, br, bi):
    return ar * br - ai * bi, ar * bi + ai * br


def _s5_scan(scr, base, ln, pw_ref, sp, *, reverse):
    s, k = 1, 0
    while s < ln:
        pr = pw_ref[0, k:k + 1, :sp]
        pi = pw_ref[0, k:k + 1, sp:]
        if reverse:
            pi = -pi
            src, dst = base + s, base
        else:
            src, dst = base, base + s
        ar, ai = scr[src:src + ln - s, :sp], scr[src:src + ln - s, sp:]
        mr, mi = _cmul(ar, ai, pr, pi)
        scr[dst:dst + ln - s, :sp] = scr[dst:dst + ln - s, :sp] + mr
        scr[dst:dst + ln - s, sp:] = scr[dst:dst + ln - s, sp:] + mi
        s *= 2
        k += 1


def _s5_states(u_ref, b_ref, lam_ref, pw_ref, scr, carry_r, carry_i, ln, sp):
    scr[S5_TOP:S5_TOP + ln, :] = _dg(u_ref[...], b_ref[0], 1, 0)
    mr, mi = _cmul(carry_r, carry_i, lam_ref[0][:, :sp], lam_ref[0][:, sp:])
    scr[S5_TOP:S5_TOP + 1, :sp] = scr[S5_TOP:S5_TOP + 1, :sp] + mr
    scr[S5_TOP:S5_TOP + 1, sp:] = scr[S5_TOP:S5_TOP + 1, sp:] + mi
    _s5_scan(scr, S5_TOP, ln, pw_ref, sp, reverse=False)


def _s5_dims(t, u_width):
    cbw = S5_BLOCK_GROUPS * S5_GROUP
    return S5_CHUNK, cbw, u_width // cbw, t // S5_CHUNK


def _s5_fwd(u, lamv, pw, bmat, cmat, *, name):
    t, w = u.shape
    ln, cbw, nb, nc = _s5_dims(t, w)
    sp2 = lamv.shape[2]
    sp = sp2 // 2

    def body(u_ref, lam_ref, pw_ref, b_ref, c_ref, ys_ref, xp_ref, scr, carry):
        @pl.when(pl.program_id(1) == 0)
        def _():
            carry[...] = jnp.zeros_like(carry)

        xp_ref[0] = carry[0:1, :]
        _s5_states(u_ref, b_ref, lam_ref, pw_ref, scr, carry[0:1, :sp], carry[0:1, sp:], ln, sp)
        carry[0:1, :] = scr[S5_TOP + ln - 1:S5_TOP + ln, :]
        ys_ref[...] = _dg(scr[S5_TOP:S5_TOP + ln, :], c_ref[0], 1, 0)

    def blk(shape):
        return pl.BlockSpec((1,) + shape, lambda j, c: (j, 0, 0))

    return pl.pallas_call(
        body, name=name, grid=(nb, nc),
        in_specs=[pl.BlockSpec((ln, cbw), lambda j, c: (c, j)), blk((1, sp2)), blk((8, sp2)),
                  blk((cbw, sp2)), blk((sp2, cbw))],
        out_specs=[pl.BlockSpec((ln, cbw), lambda j, c: (c, j)),
                   pl.BlockSpec((1, 1, sp2), lambda j, c: (j * nc + c, 0, 0))],
        out_shape=[jax.ShapeDtypeStruct((t, w), f32), jax.ShapeDtypeStruct((nb * nc, 1, sp2), f32)],
        scratch_shapes=[pltpu.VMEM((S5_TOP + ln, sp2), f32), pltpu.VMEM((8, sp2), f32)],
        compiler_params=pltpu.CompilerParams(dimension_semantics=("parallel", "arbitrary")),
    )(u, lamv, pw, bmat, cmat)


def _s5_bwd(u, dy, xp, lamv, pw, bmat, cmat, *, name):
    t, w = u.shape
    ln, cbw, nb, nc = _s5_dims(t, w)
    sp2 = lamv.shape[2]
    sp = sp2 // 2

    def body(u_ref, dy_ref, xp_ref, lam_ref, pw_ref, b_ref, c_ref,
             du_ref, db_ref, dc_ref, dl_ref, scr, gsc, gcarry):
        @pl.when(pl.program_id(1) == 0)
        def _():
            gcarry[...] = jnp.zeros_like(gcarry)
            db_ref[...] = jnp.zeros_like(db_ref)
            dc_ref[...] = jnp.zeros_like(dc_ref)
            dl_ref[...] = jnp.zeros_like(dl_ref)

        lr, li = lam_ref[0][:, :sp], lam_ref[0][:, sp:]
        xin = xp_ref[0]
        scr[S5_TOP - 1:S5_TOP, :] = xin
        _s5_states(u_ref, b_ref, lam_ref, pw_ref, scr, xin[:, :sp], xin[:, sp:], ln, sp)
        dy = dy_ref[...]
        gsc[0:ln, :] = _dg(dy, c_ref[0], 1, 1)
        mr, mi = _cmul(gcarry[0:1, :sp], gcarry[0:1, sp:], lr, -li)
        gsc[ln - 1:ln, :sp] = gsc[ln - 1:ln, :sp] + mr
        gsc[ln - 1:ln, sp:] = gsc[ln - 1:ln, sp:] + mi
        _s5_scan(gsc, 0, ln, pw_ref, sp, reverse=True)
        gcarry[0:1, :] = gsc[0:1, :]
        g = gsc[0:ln, :]
        du_ref[...] = _dg(g, b_ref[0], 1, 1)
        db_ref[0] += _dg(u_ref[...], g, 0, 0)
        dc_ref[0] += _dg(scr[S5_TOP:S5_TOP + ln, :], dy, 0, 0)
        pr, pi = scr[S5_TOP - 1:S5_TOP - 1 + ln, :sp], scr[S5_TOP - 1:S5_TOP - 1 + ln, sp:]
        gr, gi = g[:, :sp], g[:, sp:]
        dl_ref[0, 0:1, :sp] += jnp.sum(pr * gr + pi * gi, axis=0, keepdims=True)
        dl_ref[0, 0:1, sp:] += jnp.sum(pr * gi - pi * gr, axis=0, keepdims=True)

    def rc(c):
        return nc - 1 - c

    def blk(shape):
        return pl.BlockSpec((1,) + shape, lambda j, c: (j, 0, 0))

    row = pl.BlockSpec((ln, cbw), lambda j, c: (rc(c), j))
    return pl.pallas_call(
        body, name=name, grid=(nb, nc),
        in_specs=[row, row, pl.BlockSpec((1, 1, sp2), lambda j, c: (j * nc + rc(c), 0, 0)),
                  blk((1, sp2)), blk((8, sp2)), blk((cbw, sp2)), blk((sp2, cbw))],
        out_specs=[row, blk((cbw, sp2)), blk((sp2, cbw)), blk((8, sp2))],
        out_shape=[jax.ShapeDtypeStruct((t, w), f32), jax.ShapeDtypeStruct((nb, cbw, sp2), f32),
                   jax.ShapeDtypeStruct((nb, sp2, cbw), f32), jax.ShapeDtypeStruct((nb, 8, sp2), f32)],
        scratch_shapes=[pltpu.VMEM((S5_TOP + ln, sp2), f32), pltpu.VMEM((ln, sp2), f32),
                        pltpu.VMEM((8, sp2), f32)],
        compiler_params=pltpu.CompilerParams(dimension_semantics=("parallel", "arbitrary")),
    )(u, dy, xp, lamv, pw, bmat, cmat)


def _s5_discretize(lre, lim, log_step, bre, bim, cre, cim):
    g, p = lre.shape
    gb = S5_BLOCK_GROUPS
    nb = g // gb
    lam = lax.complex(lre, lim)
    lam_bar = jnp.exp(lam * jnp.exp(log_step)[:, None])
    b_bar = ((lam_bar - 1.0) / lam)[..., None] * lax.complex(bre, bim)
    lb = lam_bar.reshape(nb, 1, gb * p)
    lamv = jnp.concatenate([jnp.real(lb), jnp.imag(lb)], axis=-1)
    eye = jnp.eye(gb, dtype=f32)
    bb = b_bar.reshape(nb, gb, p, S5_GROUP)

    def bdiag(v):
        return jnp.einsum('jgpi,gh->jgihp', v, eye).reshape(nb, gb * S5_GROUP, gb * p)

    bmat = jnp.concatenate([bdiag(jnp.real(bb)), bdiag(jnp.imag(bb))], axis=-1)

    def cdiag(v):
        return jnp.einsum('jgip,gh->jhpgi', v, eye).reshape(nb, gb * p, gb * S5_GROUP)

    cmat = jnp.concatenate([cdiag(cre.reshape(nb, gb, S5_GROUP, p)),
                            -cdiag(cim.reshape(nb, gb, S5_GROUP, p))], axis=1)
    return lamv, bmat, cmat


def _s5_powers(lamv):
    sp = lamv.shape[2] // 2
    pr, pi = lamv[:, :, :sp], lamv[:, :, sp:]
    rows = []
    for _ in range(8):
        rows.append(jnp.concatenate([pr, pi], axis=-1))
        pr, pi = pr * pr - pi * pi, 2.0 * pr * pi
    return lax.stop_gradient(jnp.concatenate(rows, axis=1))


def _ln_fwd(h, out, p, tag):
    return _rw_fwd(_ln_fn, [h, out], [p["ln_g"], p["ln_b"]], [h.shape[1]], name=tag + "_ln")[0]


def _ln_bwd(h, out, p, dh2, tag):
    return _rw_bwd(_ln_fn, [h, out], [p["ln_g"], p["ln_b"]], [dh2], name=tag + "_ln_bwd")


def _ssd_layer_fwd(h, p):
    di, groups = p["wz"].shape[1], p["dtb"].shape[0]
    z = _mm(h, p["wz"], name="ssd_z")
    xr = _mm(h, p["wxbc"], name="ssd_xbc")
    dtp = _mm(h, p["wdt"], name="ssd_dt")
    xbc = _conv_fwd(xr, p["conv_w"], p["conv_b"], name="ssd_conv")
    y, states = _ssd_fwd(xbc, dtp, p["dtb"], p["alog"], p["dsk"], d_inner=di, groups=groups, name="ssd_scan")
    yn = _rw_fwd(_make_ssd_post_fn(groups), [y, z], [p["norm_w"]], [di], name="ssd_post")[0]
    out = _mm(yn, p["wout"], name="ssd_out")
    return _ln_fwd(h, out, p, "ssd"), (h, z, xr, dtp, xbc, states, y, yn, out)


def _ssd_layer_bwd(saved, p, dh2):
    h, z, xr, dtp, xbc, states, y, yn, out = saved
    di, groups = p["wz"].shape[1], p["dtb"].shape[0]
    dh, dout, dg, db = _ln_bwd(h, out, p, dh2, "ssd")
    dyn = _mm(dout, p["wout"], tb=True, name="ssd_out_dx")
    g = {"ln_g": dg, "ln_b": db, "wout": _mm(yn, dout, ta=True, name="ssd_out_dw")}
    dy, dz, g["norm_w"] = _rw_bwd(_make_ssd_post_fn(groups), [y, z], [p["norm_w"]], [dyn], name="ssd_post_bwd")
    dx, dbm, dcm, ddt, dpar = _ssd_bwd(xbc, dtp, p["dtb"], p["alog"], p["dsk"], states, dy,
                                       d_inner=di, groups=groups, name="ssd_scan_bwd")
    g["dtb"], g["alog"], g["dsk"] = dpar[:, 0:1, :], dpar[:, 1:2, :], dpar[:, 2:3, :]
    dxr, g["conv_w"], g["conv_b"] = _conv_bwd(xr, p["conv_w"], p["conv_b"],
                                               jnp.concatenate([dx, dbm, dcm], axis=1), name="ssd_conv_bwd")
    dh = _mm(dz, p["wz"], tb=True, add=dh, name="ssd_z_dx")
    dh = _mm(dxr, p["wxbc"], tb=True, add=dh, name="ssd_xbc_dx")
    dh = _mm(ddt, p["wdt"], tb=True, add=dh, name="ssd_dt_dx")
    g["wz"] = _mm(h, dz, ta=True, name="ssd_z_dw")
    g["wxbc"] = _mm(h, dxr, ta=True, name="ssd_xbc_dw")
    g["wdt"] = _mm(h, ddt, ta=True, name="ssd_dt_dw")
    return dh, g


def _fox_layer_fwd(h, p):
    w = p["wout"].shape[0]
    heads = w // FOX_HEAD_DIM
    t = h.shape[0]
    qkvz = _mm(h, p["wqkvz"], name="fox_qkvz")
    fr = _mm(h, p["wf"], name="fox_f")
    logf = _rw_fwd(_logf_fn, [fr], [p["fb"]], [LANE], name="fox_logf")[0]
    cum = _cumsum_rows(logf, reverse=False, name="fox_cumsum")
    ck = cum[:, :heads].T.reshape(heads, 1, t)
    o, lse = _attn_fwd(qkvz, 0, qkvz, heads, qkvz, 2 * heads, None, None, ck,
                       heads=heads, scale=FOX_HEAD_DIM ** -0.5, name="fox_attn")
    yg = _rw_fwd(_gate_fn, [o, (qkvz, w, 3)], [], [w], name="fox_gate")[0]
    out = _mm(yg, p["wout"], name="fox_out")
    return _ln_fwd(h, out, p, "fox"), (h, qkvz, fr, ck, o, lse, yg, out)


def _fox_layer_bwd(saved, p, dh2):
    h, qkvz, fr, ck, o, lse, yg, out = saved
    w = p["wout"].shape[0]
    heads = w // FOX_HEAD_DIM
    t = h.shape[0]
    dh, dout, dg, db = _ln_bwd(h, out, p, dh2, "fox")
    dyg = _mm(dout, p["wout"], tb=True, name="fox_out_dx")
    g = {"ln_g": dg, "ln_b": db, "wout": _mm(yg, dout, ta=True, name="fox_out_dw")}
    do, dz = _rw_bwd(_gate_fn, [o, (qkvz, w, 3)], [], [dyg], name="fox_gate_bwd")
    delta = _attn_delta(do, o, heads=heads, name="fox_delta")
    kw = dict(heads=heads, scale=FOX_HEAD_DIM ** -0.5)
    dq, drow = _attn_bwd_dq(qkvz, 0, qkvz, heads, qkvz, 2 * heads, None, None, ck, do, lse, delta,
                            name="fox_attn_dq", **kw)
    dk, dv, dck = _attn_bwd_dkv(qkvz, 0, qkvz, heads, qkvz, 2 * heads, None, None, ck, do, lse, delta,
                                name="fox_attn_dkv", **kw)
    dqkvz = jnp.concatenate([dq, dk, dv, dz], axis=1)
    dcum = jnp.pad((drow[:, :, 0] + dck.reshape(heads, t)).T, ((0, 0), (0, LANE - heads)))
    dlogf = _cumsum_rows(dcum, reverse=True, name="fox_cumsum_bwd")
    dfr, g["fb"] = _rw_bwd(_logf_fn, [fr], [p["fb"]], [dlogf], name="fox_logf_bwd")
    dh = _mm(dqkvz, p["wqkvz"], tb=True, add=dh, name="fox_qkvz_dx")
    dh = _mm(dfr, p["wf"], tb=True, add=dh, name="fox_f_dx")
    g["wqkvz"] = _mm(h, dqkvz, ta=True, name="fox_qkvz_dw")
    g["wf"] = _mm(h, dfr, ta=True, name="fox_f_dw")
    return dh, g


def _mla_layer_fwd(h, p, cos, sin):
    rank = p["qn"].shape[1]
    w = p["wout"].shape[0]
    heads = w // MLA_V
    lat = _mm(h, p["wlat"], name="mla_lat")
    kpr = _mm(h, p["wkpe"], name="mla_kpe")
    z = _mm(h, p["wz"], name="mla_z")
    qnl, kvnl = _rw_fwd(_make_mla_norm_fn(rank), [lat], [p["qn"], p["kvn"]], [rank, rank], name="mla_norm")
    qno = _mm(qnl, p["wqn"], name="mla_qn")
    qpr = _mm(qnl, p["wqp"], name="mla_qp")
    kno = _mm(kvnl, p["wkn"], name="mla_kn")
    v = _mm(kvnl, p["wv"], name="mla_v")
    qp = _rw_fwd(_make_rope_fn(heads), [qpr, cos, sin], [], [heads * LANE], name="mla_rope_q")[0]
    kp = _rw_fwd(_make_rope_fn(1), [kpr, cos, sin], [], [LANE], name="mla_rope_k")[0]
    scale = (MLA_NOPE + MLA_ROPE) ** -0.5
    o, lse = _attn_fwd(qno, 0, kno, 0, v, 0, qp, kp, None, heads=heads, scale=scale, name="mla_attn")
    yg = _rw_fwd(_gate_fn, [o, z], [], [w], name="mla_gate")[0]
    out = _mm(yg, p["wout"], name="mla_out")
    return _ln_fwd(h, out, p, "mla"), (h, lat, kpr, z, qnl, kvnl, qno, qpr, kno, v, qp, kp, o, lse, yg, out)


def _mla_layer_bwd(saved, p, dh2, cos, sin):
    h, lat, kpr, z, qnl, kvnl, qno, qpr, kno, v, qp, kp, o, lse, yg, out = saved
    rank = p["qn"].shape[1]
    w = p["wout"].shape[0]
    heads = w // MLA_V
    dh, dout, dg, db = _ln_bwd(h, out, p, dh2, "mla")
    dyg = _mm(dout, p["wout"], tb=True, name="mla_out_dx")
    g = {"ln_g": dg, "ln_b": db, "wout": _mm(yg, dout, ta=True, name="mla_out_dw")}
    do, dz = _rw_bwd(_gate_fn, [o, z], [], [dyg], name="mla_gate_bwd")
    delta = _attn_delta(do, o, heads=heads, name="mla_delta")
    kw = dict(heads=heads, scale=(MLA_NOPE + MLA_ROPE) ** -0.5)
    dqno, dqp = _attn_bwd_dq(qno, 0, kno, 0, v, 0, qp, kp, None, do, lse, delta, name="mla_attn_dq", **kw)
    dkno, dv, dkp = _attn_bwd_dkv(qno, 0, kno, 0, v, 0, qp, kp, None, do, lse, delta, name="mla_attn_dkv", **kw)
    (dqpr,) = _rw_bwd(_make_rope_fn(heads), [qpr, cos, sin], [], [dqp], n_const=2, name="mla_rope_q_bwd")
    (dkpr,) = _rw_bwd(_make_rope_fn(1), [kpr, cos, sin], [], [dkp], n_const=2, name="mla_rope_k_bwd")
    dqnl = _mm(dqno, p["wqn"], tb=True, name="mla_qn_dx")
    dqnl = _mm(dqpr, p["wqp"], tb=True, add=dqnl, name="mla_qp_dx")
    dkvnl = _mm(dkno, p["wkn"], tb=True, name="mla_kn_dx")
    dkvnl = _mm(dv, p["wv"], tb=True, add=dkvnl, name="mla_v_dx")
    g["wqn"] = _mm(qnl, dqno, ta=True, name="mla_qn_dw")
    g["wqp"] = _mm(qnl, dqpr, ta=True, name="mla_qp_dw")
    g["wkn"] = _mm(kvnl, dkno, ta=True, name="mla_kn_dw")
    g["wv"] = _mm(kvnl, dv, ta=True, name="mla_v_dw")
    dlat, g["qn"], g["kvn"] = _rw_bwd(_make_mla_norm_fn(rank), [lat], [p["qn"], p["kvn"]], [dqnl, dkvnl],
                                     name="mla_norm_bwd")
    dh = _mm(dlat, p["wlat"], tb=True, add=dh, name="mla_lat_dx")
    dh = _mm(dkpr, p["wkpe"], tb=True, add=dh, name="mla_kpe_dx")
    dh = _mm(dz, p["wz"], tb=True, add=dh, name="mla_z_dx")
    g["wlat"] = _mm(h, dlat, ta=True, name="mla_lat_dw")
    g["wkpe"] = _mm(h, dkpr, ta=True, name="mla_kpe_dw")
    g["wz"] = _mm(h, dz, ta=True, name="mla_z_dw")
    return dh, g


def _s5_layer_fwd(h, p):
    w = p["wout"].shape[0]
    u = _mm(h, p["wu"], name="s5_u")
    z = _mm(h, p["wz"], name="s5_z")
    ys, xp = _s5_fwd(u, p["lamv"], p["pw"], p["bmat"], p["cmat"], name="s5_scan")
    y1 = _rw_fwd(_s5_act_fn, [ys, u], [p["d"]], [w], name="s5_act")[0]
    gp = _mm(y1, p["wglu"], name="s5_glu")
    y2 = _rw_fwd(_s5_glu_fn, [y1, gp, z], [p["bglu"]], [w], name="s5_gate")[0]
    out = _mm(y2, p["wout"], name="s5_out")
    return _ln_fwd(h, out, p, "s5"), (h, u, z, ys, xp, y1, gp, y2, out)


def _s5_layer_bwd(saved, p, dh2):
    h, u, z, ys, xp, y1, gp, y2, out = saved
    dh, dout, dg, db = _ln_bwd(h, out, p, dh2, "s5")
    dy2 = _mm(dout, p["wout"], tb=True, name="s5_out_dx")
    g = {"ln_g": dg, "ln_b": db, "wout": _mm(y2, dout, ta=True, name="s5_out_dw")}
    dy1, dgp, dz, g["bglu"] = _rw_bwd(_s5_glu_fn, [y1, gp, z], [p["bglu"]], [dy2], name="s5_gate_bwd")
    dy1 = _mm(dgp, p["wglu"], tb=True, add=dy1, name="s5_glu_dx")
    g["wglu"] = _mm(y1, dgp, ta=True, name="s5_glu_dw")
    dys, du, g["d"] = _rw_bwd(_s5_act_fn, [ys, u], [p["d"]], [dy1], name="s5_act_bwd")
    du_s, g["bmat"], g["cmat"], dlam = _s5_bwd(u, dys, xp, p["lamv"], p["pw"], p["bmat"], p["cmat"],
                                               name="s5_scan_bwd")
    g["lamv"] = dlam[:, 0:1, :]
    dh = _mm(du, p["wu"], tb=True, add=dh, name="s5_u_dx")
    dh = _mm(du_s, p["wu"], tb=True, add=dh, name="s5_us_dx")
    dh = _mm(dz, p["wz"], tb=True, add=dh, name="s5_z_dx")
    g["wu"] = _mm(h, du, ta=True, add=_mm(h, du_s, ta=True, name="s5_us_dw"), name="s5_u_dw")
    g["wz"] = _mm(h, dz, ta=True, name="s5_z_dw")
    return dh, g


def _pad_last(a, to):
    return jnp.pad(a, [(0, 0)] * (a.ndim - 1) + [(0, to - a.shape[-1])])


def _row(v):
    return v.reshape(1, -1)


def _prep_ssd(w):
    di, cd, hh = w["ssd_w_out"].shape[0], w["ssd_conv_b"].shape[0], w["ssd_dt_bias"].shape[0]
    groups = (cd - di) // (2 * SSD_STATE)
    e = hh // groups

    def perm(v):
        lead = v.shape[:-1]
        return _pad_last(v.reshape(lead + (groups, e)), LANE).reshape(lead + (groups * LANE,))

    w_in = w["ssd_w_in"]
    return dict(
        wz=w_in[:, :di], wxbc=w_in[:, di:di + cd], wdt=perm(w_in[:, di + cd:]), wout=w["ssd_w_out"],
        conv_w=w["ssd_conv_w"], conv_b=_row(w["ssd_conv_b"]),
        dtb=perm(w["ssd_dt_bias"]).reshape(groups, 1, LANE), alog=perm(w["ssd_a_log"]).reshape(groups, 1, LANE),
        dsk=perm(w["ssd_d"]).reshape(groups, 1, LANE), norm_w=_row(w["ssd_norm_w"]),
        ln_g=_row(w["ln_g"][0]), ln_b=_row(w["ln_b"][0]))


def _prep_fox(w):
    wd = w["fox_w_out"].shape[0]
    w_in = w["fox_w_in"]
    return dict(wqkvz=w_in[:, :4 * wd], wf=_pad_last(w_in[:, 4 * wd:], LANE), wout=w["fox_w_out"],
                fb=_pad_last(_row(w["fox_f_bias"]), LANE), ln_g=_row(w["ln_g"][1]), ln_b=_row(w["ln_b"][1]))


def _prep_mla(w):
    rank = w["mla_q_norm"].shape[0]
    wd = w["mla_w_out"].shape[0]
    heads = wd // MLA_V
    w_in = w["mla_w_in"]
    qu = w["mla_w_q_up"].reshape(rank, heads, MLA_NOPE + MLA_ROPE)
    kvu = w["mla_w_kv_up"].reshape(w["mla_w_kv_up"].shape[0], heads, MLA_NOPE + MLA_V)
    return dict(
        wlat=w_in[:, :2 * rank], wkpe=_pad_last(w_in[:, 2 * rank:2 * rank + MLA_ROPE], LANE),
        wz=w_in[:, 2 * rank + MLA_ROPE:],
        wqn=qu[:, :, :MLA_NOPE].reshape(rank, heads * LANE),
        wqp=_pad_last(qu[:, :, MLA_NOPE:], LANE).reshape(rank, heads * LANE),
        wkn=kvu[:, :, :MLA_NOPE].reshape(-1, heads * LANE), wv=kvu[:, :, MLA_NOPE:].reshape(-1, heads * LANE),
        wout=w["mla_w_out"], qn=_row(w["mla_q_norm"]), kvn=_row(w["mla_kv_norm"]),
        ln_g=_row(w["ln_g"][2]), ln_b=_row(w["ln_b"][2]))


def _prep_s5(w):
    wd = w["s5_w_out"].shape[0]
    w_in = w["s5_w_in"]
    lamv, bmat, cmat = _s5_discretize(w["s5_lambda_re"], w["s5_lambda_im"], w["s5_log_step"],
                                      w["s5_b_re"], w["s5_b_im"], w["s5_c_re"], w["s5_c_im"])
    return dict(wu=w_in[:, :wd], wz=w_in[:, wd:], wglu=w["s5_w_glu"], wout=w["s5_w_out"],
                d=_row(w["s5_d"]), bglu=_row(w["s5_b_glu"]), lamv=lamv, bmat=bmat, cmat=cmat,
                ln_g=_row(w["ln_g"][3]), ln_b=_row(w["ln_b"][3]))


def _rope_tables(positions):
    inv_freq = ROPE_BASE ** (-jnp.arange(0, MLA_ROPE, 2, dtype=f32) / MLA_ROPE)
    ang = positions.astype(f32).reshape(-1, 1) * inv_freq
    return jnp.cos(ang), jnp.sin(ang)


def _local_step(x, positions, tgt, w):
    cos, sin = _rope_tables(positions)
    preps = (_prep_ssd, _prep_fox, _prep_mla, _prep_s5)
    big = {n: w[n] for n, _ in BIG}
    small = {n: w[n] for n in w if n not in big}
    ps, vjps = [], []
    for prep in preps:
        p, vjp = jax.vjp(functools.partial(lambda s, prep: prep({**big, **s}), prep=prep), small)
        ps.append(p)
        vjps.append(vjp)
    ps[3]["pw"] = _s5_powers(ps[3]["lamv"])
    h1, s0 = _ssd_layer_fwd(x, ps[0])
    h2, s1 = _fox_layer_fwd(h1, ps[1])
    h3, s2 = _mla_layer_fwd(h2, ps[2], cos, sin)
    h4, s3 = _s5_layer_fwd(h3, ps[3])
    dh, loss = _loss_head(h4, tgt)
    dh, g3 = _s5_layer_bwd(s3, ps[3], dh)
    dh, g2 = _mla_layer_bwd(s2, ps[2], dh, cos, sin)
    dh, g1 = _fox_layer_bwd(s1, ps[1], dh)
    dh, g0 = _ssd_layer_bwd(s0, ps[0], dh)
    gw = None
    for p, vjp, g in zip(ps, vjps, (g0, g1, g2, g3)):
        ct = {k: g[k].astype(p[k].dtype) for k in p if k != "pw"}
        (gi,) = vjp(ct)
        gw = gi if gw is None else jax.tree.map(jnp.add, gw, gi)
    gw.update(_assemble_big_grads(g0, g1, g2, g3, e_heads=w["ssd_dt_bias"].shape[0] // ps[0]["dtb"].shape[0],
                                  fox_heads=w["fox_f_bias"].shape[0]))
    return loss, dh, gw


def _assemble_big_grads(g0, g1, g2, g3, *, e_heads, fox_heads):
    def cat(*v):
        return jnp.concatenate(v, axis=1)

    def heads3(v):
        return v.reshape(v.shape[0], -1, LANE)

    def flat2(v):
        return v.reshape(v.shape[0], -1)

    return {
        "ssd_w_in": cat(g0["wz"], g0["wxbc"], flat2(heads3(g0["wdt"])[:, :, :e_heads])),
        "ssd_w_out": g0["wout"],
        "fox_w_in": cat(g1["wqkvz"], g1["wf"][:, :fox_heads]),
        "fox_w_out": g1["wout"],
        "mla_w_in": cat(g2["wlat"], g2["wkpe"][:, :MLA_ROPE], g2["wz"]),
        "mla_w_q_up": flat2(jnp.concatenate([heads3(g2["wqn"]), heads3(g2["wqp"])[:, :, :MLA_ROPE]], axis=2)),
        "mla_w_kv_up": flat2(jnp.concatenate([heads3(g2["wkn"]), heads3(g2["wv"])], axis=2)),
        "mla_w_out": g2["wout"],
        "s5_w_in": cat(g3["wu"], g3["wz"]),
        "s5_w_glu": g3["wglu"],
        "s5_w_out": g3["wout"],
    }


FLAT_COLS = 1024
FLAT_ROW_ALIGN = 128


def _pack(arrs):
    flat = jnp.concatenate([a.reshape(-1) for a in arrs])
    unit = FLAT_COLS * FLAT_ROW_ALIGN
    return jnp.pad(flat, (0, -flat.shape[0] % unit)).reshape(-1, FLAT_COLS)


def _unpack(flat2d, shapes):
    flat = flat2d.reshape(-1)
    out, off = [], 0
    for s in shapes:
        sz = math.prod(s)
        out.append(flat[off:off + sz].reshape(s))
        off += sz
    return out


def _unpack_gathered(gathered, shapes, axes):
    flat = gathered.reshape(N_DEV, -1)
    out, off = [], 0
    for s, ax in zip(shapes, axes):
        sz = math.prod(s)
        seg = flat[:, off:off + sz].reshape((N_DEV,) + tuple(s))
        out.append(jnp.concatenate([seg[d] for d in range(N_DEV)], axis=ax)[0])
        off += sz
    return out


MESH = pl.DeviceIdType.MESH
HBM_SPEC = pl.BlockSpec(memory_space=pl.ANY)


def _comm_scratch(n):
    return [pltpu.SemaphoreType.DMA((n, 7)), pltpu.SemaphoreType.DMA((n, 7)), pltpu.SemaphoreType.DMA((n,))]


def _all_gather(xs, *, name):
    n = len(xs)

    def body(*refs):
        x_refs, out_refs = refs[:n], refs[n:2 * n]
        send_sems, recv_sems, local_sems = refs[2 * n:]
        x, y, cc = lax.axis_index("x"), lax.axis_index("y"), lax.axis_index("c")
        me, sibling = (x, y, cc), (x, y, 1 - cc)
        chips = [(1 - x, y), (x, 1 - y), (1 - x, 1 - y)]

        def rows(w, px, py, pc):
            return out_refs[w].at[4 * px + 2 * py + pc]

        def copy(w, k, block, to, src=None):
            return pltpu.make_async_remote_copy(
                src_ref=rows(w, *block) if src is None else src, dst_ref=rows(w, *block),
                send_sem=send_sems.at[w, k], recv_sem=recv_sems.at[w, k], device_id=to, device_id_type=MESH)

        mine = [pltpu.make_async_copy(x_refs[w], rows(w, *me), local_sems.at[w]) for w in range(n)]
        first = []
        for w in range(n):
            mine[w].start()
            first.append(copy(w, 0, me, sibling, src=x_refs[w]))
            first += [copy(w, 1 + j, me, (*chip, cc), src=x_refs[w]) for j, chip in enumerate(chips)]
        for cp in first:
            cp.start()
        passed = []
        for w in range(n):
            for j, chip in enumerate(chips):
                copy(w, 1 + j, (*chip, cc), me).wait_recv()
                cp = copy(w, 4 + j, (*chip, cc), sibling)
                cp.start()
                passed.append(cp)
        for w in range(n):
            copy(w, 0, sibling, me).wait_recv()
            for j, chip in enumerate(chips):
                copy(w, 4 + j, (*chip, 1 - cc), me).wait_recv()
        for cp in first + passed:
            cp.wait_send()
        for cp in mine:
            cp.wait()

    return pl.pallas_call(
        body, name=name, out_shape=[jax.ShapeDtypeStruct((N_DEV,) + v.shape, v.dtype) for v in xs],
        in_specs=[HBM_SPEC] * n, out_specs=[HBM_SPEC] * n, scratch_shapes=_comm_scratch(n),
    )(*xs)


def _exchange(gs, *, name):
    n = len(gs)

    def body(*refs):
        g_refs, out_refs = refs[:n], refs[n:2 * n]
        send_sems, recv_sems, local_sems = refs[2 * n:]
        x, y, cc = lax.axis_index("x"), lax.axis_index("y"), lax.axis_index("c")
        me = 4 * x + 2 * y + cc
        peers = []
        for k in range(1, N_DEV):
            px = 1 - x if (k >> 2) & 1 else x
            py = 1 - y if (k >> 1) & 1 else y
            pc = 1 - cc if k & 1 else cc
            peers.append((k, (px, py, pc), 4 * px + 2 * py + pc))

        def copy(w, k, peer, slot_src, slot_dst):
            return pltpu.make_async_remote_copy(
                src_ref=g_refs[w].at[slot_src], dst_ref=out_refs[w].at[slot_dst],
                send_sem=send_sems.at[w, k - 1], recv_sem=recv_sems.at[w, k - 1], device_id=peer, device_id_type=MESH)

        mine = [pltpu.make_async_copy(g_refs[w].at[me], out_refs[w].at[me], local_sems.at[w]) for w in range(n)]
        sends = []
        for w in range(n):
            mine[w].start()
            sends += [copy(w, k, peer, idx, me) for k, peer, idx in peers]
        for cp in sends:
            cp.start()
        for w in range(n):
            for k, peer, idx in peers:
                copy(w, k, peer, idx, idx).wait_recv()
        for cp in sends:
            cp.wait_send()
        for cp in mine:
            cp.wait()

    return pl.pallas_call(
        body, name=name, out_shape=[jax.ShapeDtypeStruct(v.shape, v.dtype) for v in gs],
        in_specs=[HBM_SPEC] * n, out_specs=[HBM_SPEC] * n, scratch_shapes=_comm_scratch(n),
    )(*gs)


BIG = (("ssd_w_in", 2), ("ssd_w_out", 1), ("fox_w_in", 2), ("fox_w_out", 1), ("mla_w_in", 2),
       ("mla_w_q_up", 2), ("mla_w_kv_up", 2), ("mla_w_out", 1), ("s5_w_in", 2), ("s5_w_glu", 1), ("s5_w_out", 1))
SMALL = (("ssd_conv_w", 2), ("mla_q_norm", 1), ("mla_kv_norm", 1), ("s5_d", 1), ("s5_b_glu", 1))
REPLICATED = ("ln_g", "ln_b", "ssd_conv_b", "ssd_dt_bias", "ssd_a_log", "ssd_d", "ssd_norm_w", "fox_f_bias",
              "s5_lambda_re", "s5_lambda_im", "s5_log_step", "s5_b_re", "s5_b_im", "s5_c_re", "s5_c_im")
WEIGHT_ORDER = ("ln_g", "ln_b", "ssd_w_in", "ssd_conv_w", "ssd_conv_b", "ssd_dt_bias", "ssd_a_log", "ssd_d",
                "ssd_norm_w", "ssd_w_out", "fox_w_in", "fox_f_bias", "fox_w_out", "mla_w_in", "mla_q_norm",
                "mla_kv_norm", "mla_w_q_up", "mla_w_kv_up", "mla_w_out", "s5_w_in", "s5_lambda_re", "s5_lambda_im",
                "s5_log_step", "s5_b_re", "s5_b_im", "s5_c_re", "s5_c_im", "s5_d", "s5_w_glu", "s5_b_glu", "s5_w_out")


def _sum_adamw(parts, w, m, v, *, name):
    r, c = w.shape
    row_bytes = 2 * c * (N_DEV * parts.dtype.itemsize + 7 * 4)
    tr = _pick(r, [t for t in (256, 128, 64, 32, 16) if t * row_bytes <= VMEM_BLOCK_BUDGET])

    def body(p_ref, w_ref, m_ref, v_ref, g_ref, d_ref, m2_ref, v2_ref):
        gv = p_ref[0].astype(f32)
        for s in range(1, N_DEV):
            gv = gv + p_ref[s].astype(f32)
        g_ref[...] = gv
        m2 = ADAM_B1 * m_ref[...] + (1.0 - ADAM_B1) * gv
        v2 = ADAM_B2 * v_ref[...] + (1.0 - ADAM_B2) * jnp.square(gv)
        m_hat = m2 / (1.0 - ADAM_B1 ** ADAM_STEP)
        v_hat = v2 / (1.0 - ADAM_B2 ** ADAM_STEP)
        d_ref[...] = -ADAM_LR * (m_hat / (jnp.sqrt(v_hat) + ADAM_EPS) + ADAM_WD * w_ref[...])
        m2_ref[...] = m2
        v2_ref[...] = v2

    spec = pl.BlockSpec((tr, c), lambda i: (i, 0))
    shp = jax.ShapeDtypeStruct((r, c), f32)
    return pl.pallas_call(
        body, name=name, grid=(r // tr,),
        in_specs=[pl.BlockSpec((N_DEV, tr, c), lambda i: (0, i, 0))] + [spec] * 3,
        out_specs=[spec] * 4, out_shape=[shp] * 4,
        compiler_params=pltpu.CompilerParams(dimension_semantics=("parallel",)),
    )(parts, w, m, v)


def _train_step(a):
    small_names, small_axes = [n for n, _ in SMALL], [ax for _, ax in SMALL]
    small_shapes = [a[n].shape for n in small_names]
    rep_shapes = [a[n].shape for n in REPLICATED]

    small_flat = _pack([a[n] for n in small_names])
    gathered = _all_gather([a[n][0].astype(bf16) for n, _ in BIG] + [small_flat], name="gather_weights")
    w = {}
    for (n, ax), g in zip(BIG, gathered):
        if ax == 1:
            w[n] = g.reshape(-1, g.shape[2])
        else:
            w[n] = jnp.transpose(g, (1, 0, 2)).reshape(g.shape[1], -1)
    w.update(zip(small_names, _unpack_gathered(gathered[-1], small_shapes, small_axes)))
    for n in REPLICATED:
        w[n] = a[n] if n in ("ln_g", "ln_b") else a[n][0]

    loss, grad_x, gw = _local_step(a["x"][0], a["positions"][0], a["loss_target"][0], w)

    to_owner = []
    for n, ax in BIG:
        g = gw[n]
        if ax == 1:
            g = g.reshape(N_DEV, -1, g.shape[1])
        else:
            g = jnp.transpose(g.reshape(g.shape[0], N_DEV, -1), (1, 0, 2))
        to_owner.append(g.astype(bf16))
    small_to_owner = jax.linear_transpose(
        lambda s: _unpack_gathered(s, small_shapes, small_axes), jax.ShapeDtypeStruct(gathered[-1].shape, f32))
    (g_small,) = small_to_owner([gw[n] for n in small_names])
    parts = _exchange(to_owner + [g_small], name="exchange_gradients")

    out = {"grad": {}, "delta": {}, "new_m": {}, "new_v": {}}
    kinds = ("grad", "delta", "new_m", "new_v")
    for (n, _), p in zip(BIG, parts):
        res = _sum_adamw(p, a[n][0], a["m_" + n][0], a["v_" + n][0], name="adamw_" + n)
        for kind, r in zip(kinds, res):
            out[kind][n] = r[None]
    upd_small = _sum_adamw(parts[-1], small_flat, _pack([a["m_" + n] for n in small_names]),
                           _pack([a["v_" + n] for n in small_names]), name="adamw_small_sharded")
    for kind, r in zip(kinds, upd_small):
        out[kind].update(zip(small_names, _unpack(r, small_shapes)))
    g_rep_local = _pack([gw[n] if n in ("ln_g", "ln_b") else gw[n][None] for n in REPLICATED])
    (g_rep_all,) = _all_gather([g_rep_local], name="gather_replicated_gradients")
    upd_rep = _sum_adamw(g_rep_all, _pack([a[n] for n in REPLICATED]), _pack([a["m_" + n] for n in REPLICATED]),
                         _pack([a["v_" + n] for n in REPLICATED]), name="adamw_replicated")
    for kind, r in zip(kinds, upd_rep):
        out[kind].update(zip(REPLICATED, _unpack(r, rep_shapes)))
    loss_total = lax.psum(loss[0, 0], ("x", "y", "c"))
    return (loss_total, grad_x[None],
            *[out[kind][n] for kind in ("grad", "delta", "new_m", "new_v") for n in WEIGHT_ORDER])


def kernel(x, positions, ln_g, ln_b, ssd_w_in, ssd_conv_w, ssd_conv_b, ssd_dt_bias, ssd_a_log, ssd_d, ssd_norm_w, ssd_w_out, fox_w_in, fox_f_bias, fox_w_out, mla_w_in, mla_q_norm, mla_kv_norm, mla_w_q_up, mla_w_kv_up, mla_w_out, s5_w_in, s5_lambda_re, s5_lambda_im, s5_log_step, s5_b_re, s5_b_im, s5_c_re, s5_c_im, s5_d, s5_w_glu, s5_b_glu, s5_w_out, loss_target, m_ln_g, m_ln_b, m_ssd_w_in, m_ssd_conv_w, m_ssd_conv_b, m_ssd_dt_bias, m_ssd_a_log, m_ssd_d, m_ssd_norm_w, m_ssd_w_out, m_fox_w_in, m_fox_f_bias, m_fox_w_out, m_mla_w_in, m_mla_q_norm, m_mla_kv_norm, m_mla_w_q_up, m_mla_w_kv_up, m_mla_w_out, m_s5_w_in, m_s5_lambda_re, m_s5_lambda_im, m_s5_log_step, m_s5_b_re, m_s5_b_im, m_s5_c_re, m_s5_c_im, m_s5_d, m_s5_w_glu, m_s5_b_glu, m_s5_w_out, v_ln_g, v_ln_b, v_ssd_w_in, v_ssd_conv_w, v_ssd_conv_b, v_ssd_dt_bias, v_ssd_a_log, v_ssd_d, v_ssd_norm_w, v_ssd_w_out, v_fox_w_in, v_fox_f_bias, v_fox_w_out, v_mla_w_in, v_mla_q_norm, v_mla_kv_norm, v_mla_w_q_up, v_mla_w_kv_up, v_mla_w_out, v_s5_w_in, v_s5_lambda_re, v_s5_lambda_im, v_s5_log_step, v_s5_b_re, v_s5_b_im, v_s5_c_re, v_s5_c_im, v_s5_d, v_s5_w_glu, v_s5_b_glu, v_s5_w_out):
    return _train_step(dict(locals()))
```

```python
import functools
import math

import jax
import jax.numpy as jnp
from jax import lax
from jax.experimental import pallas as pl
from jax.experimental.pallas import tpu as pltpu

f32 = jnp.float32
bf16 = jnp.bfloat16

N_DEV = 8
DEPTH = 4
ALPHA = (2.0 * DEPTH) ** 0.25
LN_EPS = 1e-5
RMS_EPS = 1e-6
LANE = 128

SSD_STATE = 128
SSD_HEADDIM = 64
SSD_CHUNK = 128
FOX_HEAD_DIM = 128
MLA_NOPE = 128
MLA_ROPE = 64
MLA_V = 128
ROPE_BASE = 10000.0
S5_GROUP = 16
S5_BLOCK_GROUPS = 8
S5_CHUNK = 128

ADAM_LR = 0.001
ADAM_B1 = 0.9
ADAM_B2 = 0.999
ADAM_EPS = 1e-08
ADAM_WD = 0.01
ADAM_STEP = 10

VMEM_BLOCK_BUDGET = 20 * 1024 * 1024
MM_FULL_K = 2048
HIGHEST = lax.Precision.HIGHEST


def _pick(n, cands):
    for c in cands:
        if n % c == 0:
            return c
    return n


def _dg(a, b, ca, cb):
    return lax.dot_general(a.astype(bf16), b.astype(bf16), (((ca,), (cb,)), ((), ())),
                           preferred_element_type=f32)


@functools.partial(jax.custom_vjp, nondiff_argnums=(2, 3))
def bdot(a, b, ca, cb):
    return _dg(a, b, ca, cb)


def _bdot_fwd(a, b, ca, cb):
    return _dg(a, b, ca, cb), (a, b)


def _bdot_bwd(ca, cb, res, g):
    a, b = res
    nb = 1 - cb
    ma = 1 - ca
    da = _dg(g, b, 1, nb) if ca == 1 else _dg(b, g, nb, 1)
    db = _dg(a, g, ma, 0) if cb == 0 else _dg(g, a, 0, ma)
    return da, db


bdot.defvjp(_bdot_fwd, _bdot_bwd)


def _silu(x):
    return x * jax.nn.sigmoid(x)


def _softplus(x):
    return jnp.maximum(x, 0.0) + jnp.log(1.0 + jnp.exp(-jnp.abs(x)))


def _mm(a, b, *, ta=False, tb=False, add=None, name):
    m, k = (a.shape[1], a.shape[0]) if ta else a.shape
    n = b.shape[0] if tb else b.shape[1]
    kb = b.shape[1] if tb else b.shape[0]
    assert k == kb, (name, a.shape, b.shape)
    if k <= MM_FULL_K:
        tm = _pick(m, (512, 256, 128))
        tn = _pick(n, (1024, 512, 384, 256, 128))
        tk = k
    else:
        tm = _pick(m, (1024, 512, 256, 128))
        tn = _pick(n, (1024, 512, 384, 256, 128))
        tk = _pick(k, (512, 256, 128))
    nk = k // tk
    has_add = add is not None

    def body(*refs):
        if has_add:
            a_ref, b_ref, add_ref, o_ref, acc = refs
        else:
            a_ref, b_ref, o_ref, acc = refs
        kk = pl.program_id(2)

        def prod():
            return _dg(a_ref[...], b_ref[...], 0 if ta else 1, 1 if tb else 0)

        def finish(r):
            o_ref[...] = r + add_ref[...] if has_add else r

        if nk == 1:
            finish(prod())
            return

        @pl.when(kk == 0)
        def _():
            acc[...] = prod()

        @pl.when((kk > 0) & (kk < nk - 1))
        def _():
            acc[...] += prod()

        @pl.when(kk == nk - 1)
        def _():
            finish(acc[...] + prod())

    a_spec = (pl.BlockSpec((tk, tm), lambda i, j, kk: (kk, i)) if ta
              else pl.BlockSpec((tm, tk), lambda i, j, kk: (i, kk)))
    b_spec = (pl.BlockSpec((tn, tk), lambda i, j, kk: (j, kk)) if tb
              else pl.BlockSpec((tk, tn), lambda i, j, kk: (kk, j)))
    o_spec = pl.BlockSpec((tm, tn), lambda i, j, kk: (i, j))
    in_specs = [a_spec, b_spec] + ([o_spec] if has_add else [])
    args = (a, b) + ((add,) if has_add else ())
    return pl.pallas_call(
        body, name=name, grid=(m // tm, n // tn, nk),
        in_specs=in_specs, out_specs=o_spec,
        out_shape=jax.ShapeDtypeStruct((m, n), f32),
        scratch_shapes=[pltpu.VMEM((tm, tn), f32)],
        compiler_params=pltpu.CompilerParams(dimension_semantics=("parallel", "parallel", "arbitrary")),
    )(*args)


def _row_operand(op):
    if isinstance(op, tuple):
        arr, w, cb = op
    else:
        arr, w, cb = op, op.shape[1], 0
    return arr, w, cb


def _rw_tm(t, widths):
    per_row = 2 * 4 * sum(widths)
    tm = 512
    while tm > 8 and (tm * per_row > VMEM_BLOCK_BUDGET or t % tm):
        tm //= 2
    return tm


def _rw_specs(ops, tm):
    specs, arrs = [], []
    for op in ops:
        arr, w, cb = _row_operand(op)
        specs.append(pl.BlockSpec((tm, w), functools.partial(lambda i, cb: (i, cb), cb=cb)))
        arrs.append(arr)
    return specs, arrs


def _param_spec(p):
    nd = p.ndim
    return pl.BlockSpec(p.shape, lambda i: (0,) * nd)


def _rw_fwd(fn, rows, params, out_widths, *, name):
    t = _row_operand(rows[0])[0].shape[0]
    tm = _rw_tm(t, [_row_operand(r)[1] for r in rows] + list(out_widths))
    nr, npar = len(rows), len(params)

    def body(*refs):
        vals = [r[...] for r in refs[:nr + npar]]
        outs = fn(*vals)
        for o_ref, v in zip(refs[nr + npar:], outs):
            o_ref[...] = v

    rspecs, rarrs = _rw_specs(rows, tm)
    return pl.pallas_call(
        body, name=name, grid=(t // tm,),
        in_specs=rspecs + [_param_spec(p) for p in params],
        out_specs=[pl.BlockSpec((tm, w), lambda i: (i, 0)) for w in out_widths],
        out_shape=[jax.ShapeDtypeStruct((t, w), f32) for w in out_widths],
        compiler_params=pltpu.CompilerParams(dimension_semantics=("parallel",)),
    )(*rarrs, *params)


def _rw_bwd(fn, rows, params, cots, *, name, n_const=0):
    t = _row_operand(rows[0])[0].shape[0]
    nr, npar, nc = len(rows), len(params), len(cots)
    nd = nr - n_const
    widths = [_row_operand(r)[1] for r in rows]
    tm = _rw_tm(t, widths + widths[:nd] + [c.shape[1] for c in cots])

    def body(*refs):
        rv = [r[...] for r in refs[:nr]]
        pv = [r[...] for r in refs[nr:nr + npar]]
        cv = [r[...] for r in refs[nr + npar:nr + npar + nc]]
        outs = refs[nr + npar + nc:]
        consts = rv[nd:]

        def f(*diff):
            return fn(*diff[:nd], *consts, *diff[nd:])

        _, vjp = jax.vjp(f, *rv[:nd], *pv)
        g = vjp(tuple(cv))
        for o_ref, v in zip(outs[:nd], g[:nd]):
            o_ref[...] = v
        if npar:
            @pl.when(pl.program_id(0) == 0)
            def _():
                for o_ref in outs[nd:]:
                    o_ref[...] = jnp.zeros_like(o_ref)

            for o_ref, v in zip(outs[nd:], g[nd:]):
                o_ref[...] += v

    rspecs, rarrs = _rw_specs(rows, tm)
    cspecs = [pl.BlockSpec((tm, c.shape[1]), lambda i: (i, 0)) for c in cots]
    return pl.pallas_call(
        body, name=name, grid=(t // tm,),
        in_specs=rspecs + [_param_spec(p) for p in params] + cspecs,
        out_specs=[pl.BlockSpec((tm, w), lambda i: (i, 0)) for w in widths[:nd]]
        + [_param_spec(p) for p in params],
        out_shape=[jax.ShapeDtypeStruct((t, w), f32) for w in widths[:nd]]
        + [jax.ShapeDtypeStruct(p.shape, f32) for p in params],
        compiler_params=pltpu.CompilerParams(dimension_semantics=("arbitrary",)),
    )(*rarrs, *params, *cots)


def _ln_fn(h, o, g, b):
    x = ALPHA * h + o
    mu = jnp.mean(x, -1, keepdims=True)
    var = jnp.mean(jnp.square(x - mu), -1, keepdims=True)
    return ((x - mu) * lax.rsqrt(var + LN_EPS) * g + b,)


def _gate_fn(o, z):
    return (o * _silu(z),)


def _make_ssd_post_fn(groups):
    def fn(y, z, nw):
        yz = y * _silu(z)
        gw = yz.shape[1] // groups
        outs = []
        for g in range(groups):
            blk = yz[:, g * gw:(g + 1) * gw]
            outs.append(blk * lax.rsqrt(jnp.mean(jnp.square(blk), -1, keepdims=True) + RMS_EPS))
        return (jnp.concatenate(outs, axis=1) * nw,)
    return fn


def _logf_fn(f, b):
    return (jax.nn.log_sigmoid(f + b),)


def _make_mla_norm_fn(rank):
    def fn(lat, qn, kvn):
        def rms(x, w):
            return x * lax.rsqrt(jnp.mean(jnp.square(x), -1, keepdims=True) + RMS_EPS) * w
        return rms(lat[:, :rank], qn), rms(lat[:, rank:], kvn)
    return fn


def _rope_block(xb, cos, sin):
    half = MLA_ROPE // 2
    x1, x2 = xb[:, :half], xb[:, half:MLA_ROPE]
    return jnp.concatenate([x1 * cos - x2 * sin, x1 * sin + x2 * cos, xb[:, MLA_ROPE:]], axis=1)


def _make_rope_fn(heads):
    def fn(x, cos, sin):
        return (jnp.concatenate([_rope_block(x[:, h * LANE:(h + 1) * LANE], cos, sin)
                                 for h in range(heads)], axis=1),)
    return fn


def _s5_act_fn(ys, u, d):
    return (jax.nn.gelu(ys + d * u),)


def _s5_glu_fn(y1, gp, z, bg):
    return (y1 * jax.nn.sigmoid(gp + bg) * _silu(z),)


def _loss_head(y, tgt):
    t, d = y.shape
    tm = _rw_tm(t, [d, d, d])

    def body(y_ref, t_ref, dy_ref, l_ref):
        @pl.when(pl.program_id(0) == 0)
        def _():
            l_ref[...] = jnp.zeros_like(l_ref)

        e = y_ref[...] - t_ref[...]
        dy_ref[...] = e * (1.0 / d)
        l_ref[...] += 0.5 * jnp.sum(jnp.mean(jnp.square(e), axis=-1, keepdims=True), axis=0, keepdims=True)

    spec = pl.BlockSpec((tm, d), lambda i: (i, 0))
    return pl.pallas_call(
        body, name="loss_head", grid=(t // tm,), in_specs=[spec, spec],
        out_specs=[spec, pl.BlockSpec((1, 1), lambda i: (0, 0))],
        out_shape=[jax.ShapeDtypeStruct((t, d), f32), jax.ShapeDtypeStruct((1, 1), f32)],
        compiler_params=pltpu.CompilerParams(dimension_semantics=("arbitrary",)),
    )(y, tgt)


def _attn_tile(t):
    return _pick(t, (512, 256, 128))


ATTN_SUB = 512


def _logits(q1, k1, q2, k2, ck, scale, row0, col0):
    s = _dg(q1, k1, 1, 1)
    if q2 is not None:
        s = s + _dg(q2, k2, 1, 1)
    s = s * scale
    if ck is not None:
        s = s - ck
    row = row0 + lax.broadcasted_iota(jnp.int32, s.shape, 0)
    col = col0 + lax.broadcasted_iota(jnp.int32, s.shape, 1)
    return jnp.where(col <= row, s, -jnp.inf)


def _sub_blocks(tq):
    sub = min(ATTN_SUB, tq)
    return [(r * sub, pl.ds(r * sub, sub)) for r in range(tq // sub)]


def _attn_fwd(q1, q1o, k1, k1o, v, vo, q2, k2, ck, *, heads, scale, name, side=None):
    t = q1.shape[0]
    tq = tk = _attn_tile(t)
    nq = t // tq
    has2, hasb = q2 is not None, ck is not None

    def body(*refs):
        it = iter(refs)
        q1r, k1r, vr = next(it), next(it), next(it)
        q2r, k2r = (next(it), next(it)) if has2 else (None, None)
        ckr = next(it) if hasb else None
        o_r, lse_r, m_s, l_s, acc = next(it), next(it), next(it), next(it), next(it)
        i, j = pl.program_id(1), pl.program_id(2)

        @pl.when(j == 0)
        def _():
            m_s[...] = jnp.full_like(m_s, -jnp.inf)
            l_s[...] = jnp.zeros_like(l_s)
            acc[...] = jnp.zeros_like(acc)

        @pl.when(j <= i)
        def _():
            k1v, vv = k1r[...].astype(bf16), vr[...].astype(bf16)
            k2v = k2r[...].astype(bf16) if has2 else None
            ckv = ckr[0] if hasb else None
            for r0, rs in _sub_blocks(tq):
                s = _logits(q1r[rs, :], k1v, q2r[rs, :] if has2 else None, k2v, ckv, scale,
                            i * tq + r0, j * tk)
                m_prev = m_s[rs, :]
                m_new = jnp.maximum(m_prev, jnp.max(s, axis=1, keepdims=True))
                p = jnp.exp(s - m_new)
                a = jnp.exp(m_prev - m_new)
                l_s[rs, :] = a * l_s[rs, :] + jnp.sum(p, axis=1, keepdims=True)
                acc[rs, :] = a * acc[rs, :] + _dg(p, vv, 1, 0)
                m_s[rs, :] = m_new

        @pl.when(j == i)
        def _():
            o_r[...] = acc[...] / l_s[...]
            lse_r[0] = jnp.broadcast_to(m_s[...] + jnp.log(l_s[...]), (tq, LANE))

    def qmap(off):
        return lambda h, i, j: (i, off + h)

    def kmap(off):
        return lambda h, i, j: (jnp.minimum(j, i), off + h)

    in_specs = [pl.BlockSpec((tq, LANE), qmap(q1o)), pl.BlockSpec((tk, LANE), kmap(k1o)),
                pl.BlockSpec((tk, LANE), kmap(vo))]
    args = [q1, k1, v]
    if has2:
        in_specs += [pl.BlockSpec((tq, LANE), qmap(0)),
                     pl.BlockSpec((tk, LANE), lambda h, i, j: (jnp.minimum(j, i), 0))]
        args += [q2, k2]
    if hasb:
        in_specs += [pl.BlockSpec((1, 1, tk), lambda h, i, j: (h, 0, jnp.minimum(j, i)))]
        args += [ck]
    return _hosted_call(
        body, side, name=name, grid=(heads, nq, nq), in_specs=in_specs,
        out_specs=[pl.BlockSpec((tq, LANE), lambda h, i, j: (i, h)),
                   pl.BlockSpec((1, tq, LANE), lambda h, i, j: (h, i, 0))],
        out_shape=[jax.ShapeDtypeStruct((t, heads * LANE), f32),
                   jax.ShapeDtypeStruct((heads, t, LANE), f32)],
        scratch_shapes=[pltpu.VMEM((tq, 1), f32), pltpu.VMEM((tq, 1), f32), pltpu.VMEM((tq, LANE), f32)],
        args=args)


def _attn_delta(do, o, *, heads, name):
    t = do.shape[0]
    tq = _attn_tile(t)

    def body(do_ref, o_ref, d_ref):
        d_ref[0] = jnp.broadcast_to(jnp.sum(do_ref[...] * o_ref[...], axis=1, keepdims=True), (tq, LANE))

    spec = pl.BlockSpec((tq, LANE), lambda i, h: (i, h))
    return pl.pallas_call(
        body, name=name, grid=(t // tq, heads), in_specs=[spec, spec],
        out_specs=pl.BlockSpec((1, tq, LANE), lambda i, h: (h, i, 0)),
        out_shape=jax.ShapeDtypeStruct((heads, t, LANE), f32),
        compiler_params=pltpu.CompilerParams(dimension_semantics=("parallel", "parallel")),
    )(do, o)


def _attn_bwd(q1, q1o, k1, k1o, v, vo, q2, k2, ck, do, lse, delta, *, heads, scale, name, side=None):
    t = q1.shape[0]
    tq = tk = _attn_tile(t)
    nq = t // tq
    has2, hasb = q2 is not None, ck is not None

    def body(*refs):
        it = iter(refs)
        q1r, k1r, vr = next(it), next(it), next(it)
        q2r, k2r = (next(it), next(it)) if has2 else (None, None)
        ckr = next(it) if hasb else None
        dor, lser, dlr = next(it), next(it), next(it)
        dq1r, dk1r, dvr = next(it), next(it), next(it)
        dq2r, dk2r = (next(it), next(it)) if has2 else (None, None)
        drowr, dckr = (next(it), next(it)) if hasb else (None, None)
        ak, av = next(it), next(it)
        ac = next(it) if hasb else None
        h, j, i = pl.program_id(0), pl.program_id(1), pl.program_id(2)

        @pl.when((j == 0) & (i == 0))
        def _():
            dq1r[...] = jnp.zeros_like(dq1r)
            if has2:
                dq2r[...] = jnp.zeros_like(dq2r)
            if hasb:
                drowr[...] = jnp.zeros_like(drowr)

        if has2:
            @pl.when((h == 0) & (j == 0) & (i == 0))
            def _():
                dk2r[...] = jnp.zeros_like(dk2r)

        @pl.when(i == j)
        def _():
            ak[...] = jnp.zeros_like(ak)
            av[...] = jnp.zeros_like(av)
            if hasb:
                ac[...] = jnp.zeros_like(ac)

        @pl.when(i >= j)
        def _():
            rows = pl.ds(pl.multiple_of(i * tq, tq), tq)
            cols = pl.ds(pl.multiple_of(j * tk, tk), tk)
            q1v, k1v, vv, dov = (r[...].astype(bf16) for r in (q1r, k1r, vr, dor))
            q2v, k2v = (q2r[...].astype(bf16), k2r[...].astype(bf16)) if has2 else (None, None)
            s = _logits(q1v, k1v, q2v, k2v, ckr[0] if hasb else None, scale, i * tq, j * tk)
            p = jnp.exp(s - lser[0][:, 0:1])
            dp = _dg(dov, vv, 1, 1)
            dsr = p * (dp - dlr[0][:, 0:1])
            ds = (dsr * scale).astype(bf16)
            av[...] += _dg(p, dov, 0, 0)
            ak[...] += _dg(ds, q1v, 0, 0)
            dq1r[rows, :] += _dg(ds, k1v, 1, 0)
            if has2:
                dq2r[rows, :] += _dg(ds, k2v, 1, 0)
                dk2r[cols, :] += _dg(ds, q2v, 0, 0)
            if hasb:
                drowr[0, rows, :] += jnp.broadcast_to(jnp.sum(dsr, axis=1, keepdims=True), (tq, LANE))
                ac[...] -= jnp.sum(dsr, axis=0, keepdims=True)

        @pl.when(i == nq - 1)
        def _():
            dk1r[...] = ak[...]
            dvr[...] = av[...]
            if hasb:
                dckr[0] = ac[...]

    def qmap(off):
        return lambda h, j, i: (jnp.maximum(i, j), off + h)

    def kmap(off):
        return lambda h, j, i: (j, off + h)

    in_specs = [pl.BlockSpec((tq, LANE), qmap(q1o)), pl.BlockSpec((tk, LANE), kmap(k1o)),
                pl.BlockSpec((tk, LANE), kmap(vo))]
    args = [q1, k1, v]
    if has2:
        in_specs += [pl.BlockSpec((tq, LANE), qmap(0)), pl.BlockSpec((tk, LANE), lambda h, j, i: (j, 0))]
        args += [q2, k2]
    if hasb:
        in_specs += [pl.BlockSpec((1, 1, tk), lambda h, j, i: (h, 0, j))]
        args += [ck]
    row3 = pl.BlockSpec((1, tq, LANE), lambda h, j, i: (h, jnp.maximum(i, j), 0))
    in_specs += [pl.BlockSpec((tq, LANE), qmap(0)), row3, row3]
    args += [do, lse, delta]
    hd = jax.ShapeDtypeStruct((t, heads * LANE), f32)
    head_cols = pl.BlockSpec((t, LANE), lambda h, j, i: (0, h))
    kv_blk = pl.BlockSpec((tk, LANE), kmap(0))
    out_specs, out_shape = [head_cols, kv_blk, kv_blk], [hd, hd, hd]
    scratch = [pltpu.VMEM((tk, LANE), f32)] * 2
    if has2:
        out_specs += [head_cols, pl.BlockSpec((t, LANE), lambda h, j, i: (0, 0))]
        out_shape += [hd, jax.ShapeDtypeStruct((t, LANE), f32)]
    if hasb:
        out_specs += [pl.BlockSpec((1, t, LANE), lambda h, j, i: (h, 0, 0)),
                      pl.BlockSpec((1, 1, tk), lambda h, j, i: (h, 0, j))]
        out_shape += [jax.ShapeDtypeStruct((heads, t, LANE), f32), jax.ShapeDtypeStruct((heads, 1, t), f32)]
        scratch.append(pltpu.VMEM((1, tk), f32))
    return _hosted_call(body, side, name=name, grid=(heads, nq, nq), in_specs=in_specs, out_specs=out_specs,
                        out_shape=out_shape, scratch_shapes=scratch, args=args)


def _cumsum_rows(x, *, reverse, name):
    t, w = x.shape
    blk = 128
    nb = t // blk

    def body(x_ref, o_ref):
        r = lax.broadcasted_iota(jnp.int32, (blk, blk), 0)
        c = lax.broadcasted_iota(jnp.int32, (blk, blk), 1)
        tri = ((r <= c) if reverse else (r >= c)).astype(f32)
        carry = jnp.zeros((1, w), f32)
        for b in (reversed(range(nb)) if reverse else range(nb)):
            seg = x_ref[b * blk:(b + 1) * blk, :]
            cs = jnp.dot(tri, seg, precision=HIGHEST, preferred_element_type=f32) + carry
            o_ref[b * blk:(b + 1) * blk, :] = cs
            carry = cs[0:1, :] if reverse else cs[blk - 1:blk, :]

    return pl.pallas_call(body, name=name, out_shape=jax.ShapeDtypeStruct((t, w), f32))(x)


CONV_PAD = 8


def _conv_fn(x, w, b):
    t, taps = x.shape[0], w.shape[0]
    xx = jnp.concatenate([jnp.zeros((CONV_PAD, x.shape[1]), f32), x], axis=0)
    y = b
    for k in range(taps):
        off = CONV_PAD - (taps - 1) + k
        y = y + w[k:k + 1, :] * xx[off:off + t, :]
    return _silu(y)


def _conv_fwd(x, w, b, *, name):
    t, c = x.shape
    tc = LANE
    taps = w.shape[0]

    def body(x_ref, w_ref, b_ref, o_ref):
        o_ref[...] = _conv_fn(x_ref[...], w_ref[...], b_ref[...])

    xs = pl.BlockSpec((t, tc), lambda i: (0, i))
    return pl.pallas_call(
        body, name=name, grid=(c // tc,),
        in_specs=[xs, pl.BlockSpec((taps, tc), lambda i: (0, i)), pl.BlockSpec((1, tc), lambda i: (0, i))],
        out_specs=xs, out_shape=jax.ShapeDtypeStruct((t, c), f32),
        compiler_params=pltpu.CompilerParams(dimension_semantics=("parallel",)),
    )(x, w, b)


def _conv_bwd(x, w, b, ds, *, name):
    t, c = x.shape
    tc = LANE
    taps = w.shape[0]

    def body(x_ref, w_ref, b_ref, ds_ref, dx_ref, dw_ref, db_ref):
        _, vjp = jax.vjp(_conv_fn, x_ref[...], w_ref[...], b_ref[...])
        dx, dw, db = vjp(ds_ref[...])
        dx_ref[...] = dx
        dw_ref[...] = dw
        db_ref[...] = db

    xs = pl.BlockSpec((t, tc), lambda i: (0, i))
    ws = pl.BlockSpec((taps, tc), lambda i: (0, i))
    bs = pl.BlockSpec((1, tc), lambda i: (0, i))
    return pl.pallas_call(
        body, name=name, grid=(c // tc,), in_specs=[xs, ws, bs, xs], out_specs=[xs, ws, bs],
        out_shape=[jax.ShapeDtypeStruct((t, c), f32), jax.ShapeDtypeStruct((taps, c), f32),
                   jax.ShapeDtypeStruct((1, c), f32)],
        compiler_params=pltpu.CompilerParams(dimension_semantics=("parallel",)),
    )(x, w, b, ds)


def _ssd_chunk(x, bm, cm, dtr, st, dtb, alog, dsk, *, e_heads):
    ln, p = x.shape[0], SSD_HEADDIM
    dt = _softplus(dtr + dtb)
    da = dt * (-jnp.exp(alog))
    r = lax.broadcasted_iota(jnp.int32, (ln, ln), 0)
    c = lax.broadcasted_iota(jnp.int32, (ln, ln), 1)
    lower = r >= c
    acs = jnp.dot(lower.astype(f32), da, precision=HIGHEST, preferred_element_type=f32)
    acs_t = acs.T

    def per_head(v):
        return jnp.concatenate([jnp.broadcast_to(v[:, e:e + 1], (v.shape[0], p)) for e in range(e_heads)], axis=1)

    dt_x, acs_x = per_head(dt), per_head(acs)
    last_x = acs_x[ln - 1:ln, :]
    xdt = x * dt_x
    cb = bdot(cm, bm, 1, 1)
    y_off = bdot(cm, st, 1, 0) * jnp.exp(acs_x)
    st_new = st * jnp.exp(last_x) + bdot(bm, xdt * jnp.exp(last_x - acs_x), 0, 0)
    ys = []
    for e in range(e_heads):
        seg = acs[:, e:e + 1] - acs_t[e:e + 1, :]
        dec = jnp.exp(jnp.where(lower, seg, -jnp.inf))
        ys.append(bdot(cb * dec, xdt[:, e * p:(e + 1) * p], 1, 0))
    y = jnp.concatenate(ys, axis=1) + y_off + per_head(dsk) * x
    return y, st_new


def _ssd_specs(t, d_inner, groups):
    ln, n = SSD_CHUNK, SSD_STATE
    gw = d_inner // groups
    nc = t // ln
    return ln, n, gw, nc


def _ssd_fwd(xbc, dtp, dtb, alog, dsk, *, d_inner, groups, name, side=None):
    t = xbc.shape[0]
    ln, n, gw, nc = _ssd_specs(t, d_inner, groups)
    e_heads = gw // SSD_HEADDIM
    boff = d_inner // n

    def body(x_ref, b_ref, c_ref, dt_ref, dtb_ref, alog_ref, dsk_ref, y_ref, save_ref, st_scr):
        c, g = pl.program_id(0), pl.program_id(1)

        @pl.when(c == 0)
        def _():
            st_scr[g] = jnp.zeros((n, gw), f32)

        st = st_scr[g]
        save_ref[0] = st
        y, st_new = _ssd_chunk(x_ref[...], b_ref[...], c_ref[...], dt_ref[...], st,
                               dtb_ref[0], alog_ref[0], dsk_ref[0], e_heads=e_heads)
        y_ref[...] = y
        st_scr[g] = st_new

    par = pl.BlockSpec((1, 1, LANE), lambda c, g: (g, 0, 0))
    return _hosted_call(
        body, side, name=name, grid=(nc, groups),
        in_specs=[pl.BlockSpec((ln, gw), lambda c, g: (c, g)),
                  pl.BlockSpec((ln, n), lambda c, g: (c, boff + g)),
                  pl.BlockSpec((ln, n), lambda c, g: (c, boff + groups + g)),
                  pl.BlockSpec((ln, LANE), lambda c, g: (c, g)), par, par, par],
        out_specs=[pl.BlockSpec((ln, gw), lambda c, g: (c, g)),
                   pl.BlockSpec((1, n, gw), lambda c, g: (c * groups + g, 0, 0))],
        out_shape=[jax.ShapeDtypeStruct((t, d_inner), f32),
                   jax.ShapeDtypeStruct((nc * groups, n, gw), f32)],
        scratch_shapes=[pltpu.VMEM((groups, n, gw), f32)],
        args=(xbc, xbc, xbc, dtp, dtb, alog, dsk))


def _ssd_bwd(xbc, dtp, dtb, alog, dsk, saved, dy, *, d_inner, groups, name, side=None):
    t = xbc.shape[0]
    ln, n, gw, nc = _ssd_specs(t, d_inner, groups)
    e_heads = gw // SSD_HEADDIM
    boff = d_inner // n

    def body(x_ref, b_ref, c_ref, dt_ref, dtb_ref, alog_ref, dsk_ref, save_ref, dy_ref,
             dx_ref, db_ref, dc_ref, ddt_ref, dpar_ref, dst_scr):
        c, g = pl.program_id(0), pl.program_id(1)

        @pl.when(c == 0)
        def _():
            dst_scr[g] = jnp.zeros((n, gw), f32)

        @pl.when((c == 0) & (g == 0))
        def _():
            dpar_ref[...] = jnp.zeros_like(dpar_ref)

        fn = functools.partial(_ssd_chunk, e_heads=e_heads)
        _, vjp = jax.vjp(fn, x_ref[...], b_ref[...], c_ref[...], dt_ref[...], save_ref[0],
                         dtb_ref[0], alog_ref[0], dsk_ref[0])
        gx, gb, gc, gdt, gst, gdtb, galog, gdsk = vjp((dy_ref[...], dst_scr[g]))
        dx_ref[...] = gx
        db_ref[...] = gb
        dc_ref[...] = gc
        ddt_ref[...] = gdt
        dst_scr[g] = gst
        dpar_ref[g, 0:1, :] += gdtb
        dpar_ref[g, 1:2, :] += galog
        dpar_ref[g, 2:3, :] += gdsk

    def rc(c):
        return nc - 1 - c

    par = pl.BlockSpec((1, 1, LANE), lambda c, g: (g, 0, 0))
    xs = pl.BlockSpec((ln, gw), lambda c, g: (rc(c), g))
    ns = pl.BlockSpec((ln, n), lambda c, g: (rc(c), g))
    return _hosted_call(
        body, side, name=name, grid=(nc, groups),
        in_specs=[xs,
                  pl.BlockSpec((ln, n), lambda c, g: (rc(c), boff + g)),
                  pl.BlockSpec((ln, n), lambda c, g: (rc(c), boff + groups + g)),
                  pl.BlockSpec((ln, LANE), lambda c, g: (rc(c), g)), par, par, par,
                  pl.BlockSpec((1, n, gw), lambda c, g: (rc(c) * groups + g, 0, 0)), xs],
        out_specs=[xs, ns, ns, pl.BlockSpec((ln, LANE), lambda c, g: (rc(c), g)),
                   pl.BlockSpec((groups, 8, LANE), lambda c, g: (0, 0, 0))],
        out_shape=[jax.ShapeDtypeStruct((t, d_inner), f32),
                   jax.ShapeDtypeStruct((t, groups * n), f32),
                   jax.ShapeDtypeStruct((t, groups * n), f32),
                   jax.ShapeDtypeStruct((t, groups * LANE), f32),
                   jax.ShapeDtypeStruct((groups, 8, LANE), f32)],
        scratch_shapes=[pltpu.VMEM((groups, n, gw), f32)],
        args=(xbc, xbc, xbc, dtp, dtb, alog, dsk, saved, dy))


S5_TOP = 8


def _cmul(ar, ai, br, bi):
    return ar * br - ai * bi, ar * bi + ai * br


def _s5_scan(scr, base, ln, pw_ref, sp, *, reverse):
    s, k = 1, 0
    while s < ln:
        pr = pw_ref[0, k:k + 1, :sp]
        pi = pw_ref[0, k:k + 1, sp:]
        if reverse:
            pi = -pi
            src, dst = base + s, base
        else:
            src, dst = base, base + s
        ar, ai = scr[src:src + ln - s, :sp], scr[src:src + ln - s, sp:]
        mr, mi = _cmul(ar, ai, pr, pi)
        scr[dst:dst + ln - s, :sp] = scr[dst:dst + ln - s, :sp] + mr
        scr[dst:dst + ln - s, sp:] = scr[dst:dst + ln - s, sp:] + mi
        s *= 2
        k += 1


def _s5_states(u_ref, b_ref, lam_ref, pw_ref, scr, carry_r, carry_i, ln, sp):
    scr[S5_TOP:S5_TOP + ln, :] = _dg(u_ref[...], b_ref[0], 1, 0)
    mr, mi = _cmul(carry_r, carry_i, lam_ref[0][:, :sp], lam_ref[0][:, sp:])
    scr[S5_TOP:S5_TOP + 1, :sp] = scr[S5_TOP:S5_TOP + 1, :sp] + mr
    scr[S5_TOP:S5_TOP + 1, sp:] = scr[S5_TOP:S5_TOP + 1, sp:] + mi
    _s5_scan(scr, S5_TOP, ln, pw_ref, sp, reverse=False)


def _s5_dims(t, u_width):
    cbw = S5_BLOCK_GROUPS * S5_GROUP
    return S5_CHUNK, cbw, u_width // cbw, t // S5_CHUNK


def _s5_fwd(u, lamv, pw, bmat, cmat, *, name):
    t, w = u.shape
    ln, cbw, nb, nc = _s5_dims(t, w)
    sp2 = lamv.shape[2]
    sp = sp2 // 2

    def body(u_ref, lam_ref, pw_ref, b_ref, c_ref, ys_ref, xp_ref, scr, carry):
        @pl.when(pl.program_id(1) == 0)
        def _():
            carry[...] = jnp.zeros_like(carry)

        xp_ref[0] = carry[0:1, :]
        _s5_states(u_ref, b_ref, lam_ref, pw_ref, scr, carry[0:1, :sp], carry[0:1, sp:], ln, sp)
        carry[0:1, :] = scr[S5_TOP + ln - 1:S5_TOP + ln, :]
        ys_ref[...] = _dg(scr[S5_TOP:S5_TOP + ln, :], c_ref[0], 1, 0)

    def blk(shape):
        return pl.BlockSpec((1,) + shape, lambda j, c: (j, 0, 0))

    return pl.pallas_call(
        body, name=name, grid=(nb, nc),
        in_specs=[pl.BlockSpec((ln, cbw), lambda j, c: (c, j)), blk((1, sp2)), blk((8, sp2)),
                  blk((cbw, sp2)), blk((sp2, cbw))],
        out_specs=[pl.BlockSpec((ln, cbw), lambda j, c: (c, j)),
                   pl.BlockSpec((1, 1, sp2), lambda j, c: (j * nc + c, 0, 0))],
        out_shape=[jax.ShapeDtypeStruct((t, w), f32), jax.ShapeDtypeStruct((nb * nc, 1, sp2), f32)],
        scratch_shapes=[pltpu.VMEM((S5_TOP + ln, sp2), f32), pltpu.VMEM((8, sp2), f32)],
        compiler_params=pltpu.CompilerParams(dimension_semantics=("parallel", "arbitrary")),
    )(u, lamv, pw, bmat, cmat)


def _s5_bwd(u, dy, du_skip, xp, lamv, pw, bmat, cmat, *, name):
    t, w = u.shape
    ln, cbw, nb, nc = _s5_dims(t, w)
    sp2 = lamv.shape[2]
    sp = sp2 // 2

    def body(u_ref, dy_ref, dus_ref, xp_ref, lam_ref, pw_ref, b_ref, c_ref,
             du_ref, db_ref, dc_ref, dl_ref, scr, gsc, gcarry):
        @pl.when(pl.program_id(1) == 0)
        def _():
            gcarry[...] = jnp.zeros_like(gcarry)
            db_ref[...] = jnp.zeros_like(db_ref)
            dc_ref[...] = jnp.zeros_like(dc_ref)
            dl_ref[...] = jnp.zeros_like(dl_ref)

        lr, li = lam_ref[0][:, :sp], lam_ref[0][:, sp:]
        xin = xp_ref[0]
        scr[S5_TOP - 1:S5_TOP, :] = xin
        _s5_states(u_ref, b_ref, lam_ref, pw_ref, scr, xin[:, :sp], xin[:, sp:], ln, sp)
        dy = dy_ref[...]
        gsc[0:ln, :] = _dg(dy, c_ref[0], 1, 1)
        mr, mi = _cmul(gcarry[0:1, :sp], gcarry[0:1, sp:], lr, -li)
        gsc[ln - 1:ln, :sp] = gsc[ln - 1:ln, :sp] + mr
        gsc[ln - 1:ln, sp:] = gsc[ln - 1:ln, sp:] + mi
        _s5_scan(gsc, 0, ln, pw_ref, sp, reverse=True)
        gcarry[0:1, :] = gsc[0:1, :]
        g = gsc[0:ln, :]
        du_ref[...] = dus_ref[...] + _dg(g, b_ref[0], 1, 1)
        db_ref[0] += _dg(u_ref[...], g, 0, 0)
        dc_ref[0] += _dg(scr[S5_TOP:S5_TOP + ln, :], dy, 0, 0)
        pr, pi = scr[S5_TOP - 1:S5_TOP - 1 + ln, :sp], scr[S5_TOP - 1:S5_TOP - 1 + ln, sp:]
        gr, gi = g[:, :sp], g[:, sp:]
        dl_ref[0, 0:1, :sp] += jnp.sum(pr * gr + pi * gi, axis=0, keepdims=True)
        dl_ref[0, 0:1, sp:] += jnp.sum(pr * gi - pi * gr, axis=0, keepdims=True)

    def rc(c):
        return nc - 1 - c

    def blk(shape):
        return pl.BlockSpec((1,) + shape, lambda j, c: (j, 0, 0))

    row = pl.BlockSpec((ln, cbw), lambda j, c: (rc(c), j))
    return pl.pallas_call(
        body, name=name, grid=(nb, nc),
        in_specs=[row, row, row, pl.BlockSpec((1, 1, sp2), lambda j, c: (j * nc + rc(c), 0, 0)),
                  blk((1, sp2)), blk((8, sp2)), blk((cbw, sp2)), blk((sp2, cbw))],
        out_specs=[row, blk((cbw, sp2)), blk((sp2, cbw)), blk((8, sp2))],
        out_shape=[jax.ShapeDtypeStruct((t, w), f32), jax.ShapeDtypeStruct((nb, cbw, sp2), f32),
                   jax.ShapeDtypeStruct((nb, sp2, cbw), f32), jax.ShapeDtypeStruct((nb, 8, sp2), f32)],
        scratch_shapes=[pltpu.VMEM((S5_TOP + ln, sp2), f32), pltpu.VMEM((ln, sp2), f32),
                        pltpu.VMEM((8, sp2), f32)],
        compiler_params=pltpu.CompilerParams(dimension_semantics=("parallel", "arbitrary")),
    )(u, dy, du_skip, xp, lamv, pw, bmat, cmat)


def _s5_discretize(lre, lim, log_step, bre, bim, cre, cim):
    g, p = lre.shape
    gb = S5_BLOCK_GROUPS
    nb = g // gb
    lam = lax.complex(lre, lim)
    lam_bar = jnp.exp(lam * jnp.exp(log_step)[:, None])
    b_bar = ((lam_bar - 1.0) / lam)[..., None] * lax.complex(bre, bim)
    lb = lam_bar.reshape(nb, 1, gb * p)
    lamv = jnp.concatenate([jnp.real(lb), jnp.imag(lb)], axis=-1)
    eye = jnp.eye(gb, dtype=f32)
    bb = b_bar.reshape(nb, gb, p, S5_GROUP)

    def bdiag(v):
        return jnp.einsum('jgpi,gh->jgihp', v, eye).reshape(nb, gb * S5_GROUP, gb * p)

    bmat = jnp.concatenate([bdiag(jnp.real(bb)), bdiag(jnp.imag(bb))], axis=-1)

    def cdiag(v):
        return jnp.einsum('jgip,gh->jhpgi', v, eye).reshape(nb, gb * p, gb * S5_GROUP)

    cmat = jnp.concatenate([cdiag(cre.reshape(nb, gb, S5_GROUP, p)),
                            -cdiag(cim.reshape(nb, gb, S5_GROUP, p))], axis=1)
    return lamv, bmat, cmat


def _s5_powers(lamv):
    sp = lamv.shape[2] // 2
    pr, pi = lamv[:, :, :sp], lamv[:, :, sp:]
    rows = []
    for _ in range(8):
        rows.append(jnp.concatenate([pr, pi], axis=-1))
        pr, pi = pr * pr - pi * pi, 2.0 * pr * pi
    return lax.stop_gradient(jnp.concatenate(rows, axis=1))


def _ln_fwd(h, out, p, tag):
    return _rw_fwd(_ln_fn, [h, out], [p["ln_g"], p["ln_b"]], [h.shape[1]], name=tag + "_ln")[0]


def _ln_bwd(h, out, p, dh2, tag):
    return _rw_bwd(_ln_fn, [h, out], [p["ln_g"], p["ln_b"]], [dh2], name=tag + "_ln_bwd")


def _ssd_layer_fwd(h, p, side=None):
    di, groups = p["wz"].shape[1], p["dtb"].shape[0]
    z = _mm(h, p["wz"], name="ssd_z")
    xr = _mm(h, p["wxbc"], name="ssd_xbc")
    dtp = _mm(h, p["wdt"], name="ssd_dt")
    xbc = _conv_fwd(xr, p["conv_w"], p["conv_b"], name="ssd_conv")
    (y, states), got = _ssd_fwd(xbc, dtp, p["dtb"], p["alog"], p["dsk"], d_inner=di, groups=groups,
                                name="ssd_scan", side=side)
    yn = _rw_fwd(_make_ssd_post_fn(groups), [y, z], [p["norm_w"]], [di], name="ssd_post")[0]
    out = _mm(yn, p["wout"], name="ssd_out")
    return _ln_fwd(h, out, p, "ssd"), (h, z, xr, dtp, xbc, states, y, yn, out), got


def _ssd_layer_bwd(saved, p, dh2, side=None):
    h, z, xr, dtp, xbc, states, y, yn, out = saved
    di, groups = p["wz"].shape[1], p["dtb"].shape[0]
    dh, dout, dg, db = _ln_bwd(h, out, p, dh2, "ssd")
    dyn = _mm(dout, p["wout"], tb=True, name="ssd_out_dx")
    g = {"ln_g": dg, "ln_b": db, "wout": _mm(yn, dout, ta=True, name="ssd_out_dw")}
    dy, dz, g["norm_w"] = _rw_bwd(_make_ssd_post_fn(groups), [y, z], [p["norm_w"]], [dyn], name="ssd_post_bwd")
    (dx, dbm, dcm, ddt, dpar), got = _ssd_bwd(xbc, dtp, p["dtb"], p["alog"], p["dsk"], states, dy,
                                              d_inner=di, groups=groups, name="ssd_scan_bwd", side=side)
    g["dtb"], g["alog"], g["dsk"] = dpar[:, 0:1, :], dpar[:, 1:2, :], dpar[:, 2:3, :]
    dxr, g["conv_w"], g["conv_b"] = _conv_bwd(xr, p["conv_w"], p["conv_b"],
                                               jnp.concatenate([dx, dbm, dcm], axis=1), name="ssd_conv_bwd")
    dh = _mm(dz, p["wz"], tb=True, add=dh, name="ssd_z_dx")
    dh = _mm(dxr, p["wxbc"], tb=True, add=dh, name="ssd_xbc_dx")
    dh = _mm(ddt, p["wdt"], tb=True, add=dh, name="ssd_dt_dx")
    g["wz"] = _mm(h, dz, ta=True, name="ssd_z_dw")
    g["wxbc"] = _mm(h, dxr, ta=True, name="ssd_xbc_dw")
    g["wdt"] = _mm(h, ddt, ta=True, name="ssd_dt_dw")
    return dh, g, got


def _fox_layer_fwd(h, p, side=None):
    w = p["wout"].shape[0]
    heads = w // FOX_HEAD_DIM
    t = h.shape[0]
    qkvz = _mm(h, p["wqkvz"], name="fox_qkvz")
    fr = _mm(h, p["wf"], name="fox_f")
    logf = _rw_fwd(_logf_fn, [fr], [p["fb"]], [LANE], name="fox_logf")[0]
    cum = _cumsum_rows(logf, reverse=False, name="fox_cumsum")
    ck = cum[:, :heads].T.reshape(heads, 1, t)
    (o, lse), got = _attn_fwd(qkvz, 0, qkvz, heads, qkvz, 2 * heads, None, None, ck,
                              heads=heads, scale=FOX_HEAD_DIM ** -0.5, name="fox_attn", side=side)
    yg = _rw_fwd(_gate_fn, [o, (qkvz, w, 3)], [], [w], name="fox_gate")[0]
    out = _mm(yg, p["wout"], name="fox_out")
    return _ln_fwd(h, out, p, "fox"), (h, qkvz, fr, ck, o, lse, yg, out), got


def _fox_layer_bwd(saved, p, dh2, side=None):
    h, qkvz, fr, ck, o, lse, yg, out = saved
    w = p["wout"].shape[0]
    heads = w // FOX_HEAD_DIM
    t = h.shape[0]
    dh, dout, dg, db = _ln_bwd(h, out, p, dh2, "fox")
    dyg = _mm(dout, p["wout"], tb=True, name="fox_out_dx")
    g = {"ln_g": dg, "ln_b": db, "wout": _mm(yg, dout, ta=True, name="fox_out_dw")}
    do, dz = _rw_bwd(_gate_fn, [o, (qkvz, w, 3)], [], [dyg], name="fox_gate_bwd")
    delta = _attn_delta(do, o, heads=heads, name="fox_delta")
    (dq, dk, dv, drow, dck), got = _attn_bwd(qkvz, 0, qkvz, heads, qkvz, 2 * heads, None, None, ck, do, lse, delta,
                                             heads=heads, scale=FOX_HEAD_DIM ** -0.5, name="fox_attn_bwd", side=side)
    dqkvz = jnp.concatenate([dq, dk, dv, dz], axis=1)
    dcum = jnp.pad((drow[:, :, 0] + dck.reshape(heads, t)).T, ((0, 0), (0, LANE - heads)))
    dlogf = _cumsum_rows(dcum, reverse=True, name="fox_cumsum_bwd")
    dfr, g["fb"] = _rw_bwd(_logf_fn, [fr], [p["fb"]], [dlogf], name="fox_logf_bwd")
    dh = _mm(dqkvz, p["wqkvz"], tb=True, add=dh, name="fox_qkvz_dx")
    dh = _mm(dfr, p["wf"], tb=True, add=dh, name="fox_f_dx")
    g["wqkvz"] = _mm(h, dqkvz, ta=True, name="fox_qkvz_dw")
    g["wf"] = _mm(h, dfr, ta=True, name="fox_f_dw")
    return dh, g, got


def _mla_layer_fwd(h, p, cos, sin):
    rank = p["qn"].shape[1]
    w = p["wout"].shape[0]
    heads = w // MLA_V
    lat = _mm(h, p["wlat"], name="mla_lat")
    kpr = _mm(h, p["wkpe"], name="mla_kpe")
    z = _mm(h, p["wz"], name="mla_z")
    qnl, kvnl = _rw_fwd(_make_mla_norm_fn(rank), [lat], [p["qn"], p["kvn"]], [rank, rank], name="mla_norm")
    qno = _mm(qnl, p["wqn"], name="mla_qn")
    qpr = _mm(qnl, p["wqp"], name="mla_qp")
    kno = _mm(kvnl, p["wkn"], name="mla_kn")
    v = _mm(kvnl, p["wv"], name="mla_v")
    qp = _rw_fwd(_make_rope_fn(heads), [qpr, cos, sin], [], [heads * LANE], name="mla_rope_q")[0]
    kp = _rw_fwd(_make_rope_fn(1), [kpr, cos, sin], [], [LANE], name="mla_rope_k")[0]
    scale = (MLA_NOPE + MLA_ROPE) ** -0.5
    (o, lse), _ = _attn_fwd(qno, 0, kno, 0, v, 0, qp, kp, None, heads=heads, scale=scale, name="mla_attn")
    yg = _rw_fwd(_gate_fn, [o, z], [], [w], name="mla_gate")[0]
    out = _mm(yg, p["wout"], name="mla_out")
    return _ln_fwd(h, out, p, "mla"), (h, lat, kpr, z, qnl, kvnl, qno, qpr, kno, v, qp, kp, o, lse, yg, out)


def _mla_layer_bwd(saved, p, dh2, cos, sin, side=None):
    h, lat, kpr, z, qnl, kvnl, qno, qpr, kno, v, qp, kp, o, lse, yg, out = saved
    rank = p["qn"].shape[1]
    w = p["wout"].shape[0]
    heads = w // MLA_V
    dh, dout, dg, db = _ln_bwd(h, out, p, dh2, "mla")
    dyg = _mm(dout, p["wout"], tb=True, name="mla_out_dx")
    g = {"ln_g": dg, "ln_b": db, "wout": _mm(yg, dout, ta=True, name="mla_out_dw")}
    do, dz = _rw_bwd(_gate_fn, [o, z], [], [dyg], name="mla_gate_bwd")
    delta = _attn_delta(do, o, heads=heads, name="mla_delta")
    (dqno, dkno, dv, dqp, dkp), got = _attn_bwd(qno, 0, kno, 0, v, 0, qp, kp, None, do, lse, delta, heads=heads,
                                                scale=(MLA_NOPE + MLA_ROPE) ** -0.5, name="mla_attn_bwd", side=side)
    (dqpr,) = _rw_bwd(_make_rope_fn(heads), [qpr, cos, sin], [], [dqp], n_const=2, name="mla_rope_q_bwd")
    (dkpr,) = _rw_bwd(_make_rope_fn(1), [kpr, cos, sin], [], [dkp], n_const=2, name="mla_rope_k_bwd")
    dqnl = _mm(dqno, p["wqn"], tb=True, name="mla_qn_dx")
    dqnl = _mm(dqpr, p["wqp"], tb=True, add=dqnl, name="mla_qp_dx")
    dkvnl = _mm(dkno, p["wkn"], tb=True, name="mla_kn_dx")
    dkvnl = _mm(dv, p["wv"], tb=True, add=dkvnl, name="mla_v_dx")
    g["wqn"] = _mm(qnl, dqno, ta=True, name="mla_qn_dw")
    g["wqp"] = _mm(qnl, dqpr, ta=True, name="mla_qp_dw")
    g["wkn"] = _mm(kvnl, dkno, ta=True, name="mla_kn_dw")
    g["wv"] = _mm(kvnl, dv, ta=True, name="mla_v_dw")
    dlat, g["qn"], g["kvn"] = _rw_bwd(_make_mla_norm_fn(rank), [lat], [p["qn"], p["kvn"]], [dqnl, dkvnl],
                                     name="mla_norm_bwd")
    dh = _mm(dlat, p["wlat"], tb=True, add=dh, name="mla_lat_dx")
    dh = _mm(dkpr, p["wkpe"], tb=True, add=dh, name="mla_kpe_dx")
    dh = _mm(dz, p["wz"], tb=True, add=dh, name="mla_z_dx")
    g["wlat"] = _mm(h, dlat, ta=True, name="mla_lat_dw")
    g["wkpe"] = _mm(h, dkpr, ta=True, name="mla_kpe_dw")
    g["wz"] = _mm(h, dz, ta=True, name="mla_z_dw")
    return dh, g, got


def _s5_layer_fwd(h, p):
    w = p["wout"].shape[0]
    u = _mm(h, p["wu"], name="s5_u")
    z = _mm(h, p["wz"], name="s5_z")
    ys, xp = _s5_fwd(u, p["lamv"], p["pw"], p["bmat"], p["cmat"], name="s5_scan")
    y1 = _rw_fwd(_s5_act_fn, [ys, u], [p["d"]], [w], name="s5_act")[0]
    gp = _mm(y1, p["wglu"], name="s5_glu")
    y2 = _rw_fwd(_s5_glu_fn, [y1, gp, z], [p["bglu"]], [w], name="s5_gate")[0]
    out = _mm(y2, p["wout"], name="s5_out")
    return _ln_fwd(h, out, p, "s5"), (h, u, z, ys, xp, y1, gp, y2, out)


def _s5_layer_bwd(saved, p, dh2):
    h, u, z, ys, xp, y1, gp, y2, out = saved
    dh, dout, dg, db = _ln_bwd(h, out, p, dh2, "s5")
    dy2 = _mm(dout, p["wout"], tb=True, name="s5_out_dx")
    g = {"ln_g": dg, "ln_b": db, "wout": _mm(y2, dout, ta=True, name="s5_out_dw")}
    dy1, dgp, dz, g["bglu"] = _rw_bwd(_s5_glu_fn, [y1, gp, z], [p["bglu"]], [dy2], name="s5_gate_bwd")
    dy1 = _mm(dgp, p["wglu"], tb=True, add=dy1, name="s5_glu_dx")
    g["wglu"] = _mm(y1, dgp, ta=True, name="s5_glu_dw")
    dys, du, g["d"] = _rw_bwd(_s5_act_fn, [ys, u], [p["d"]], [dy1], name="s5_act_bwd")
    du, g["bmat"], g["cmat"], dlam = _s5_bwd(u, dys, du, xp, p["lamv"], p["pw"], p["bmat"], p["cmat"],
                                             name="s5_scan_bwd")
    g["lamv"] = dlam[:, 0:1, :]
    dh = _mm(du, p["wu"], tb=True, add=dh, name="s5_u_dx")
    dh = _mm(dz, p["wz"], tb=True, add=dh, name="s5_z_dx")
    g["wu"] = _mm(h, du, ta=True, name="s5_u_dw")
    g["wz"] = _mm(h, dz, ta=True, name="s5_z_dw")
    return dh, g


def _pad_last(a, to):
    return jnp.pad(a, [(0, 0)] * (a.ndim - 1) + [(0, to - a.shape[-1])])


def _row(v):
    return v.reshape(1, -1)


def _prep_ssd(w):
    di, cd, hh = w["ssd_w_out"].shape[0], w["ssd_conv_b"].shape[0], w["ssd_dt_bias"].shape[0]
    groups = (cd - di) // (2 * SSD_STATE)
    e = hh // groups

    def perm(v):
        lead = v.shape[:-1]
        return _pad_last(v.reshape(lead + (groups, e)), LANE).reshape(lead + (groups * LANE,))

    w_in = w["ssd_w_in"]
    return dict(
        wz=w_in[:, :di], wxbc=w_in[:, di:di + cd], wdt=perm(w_in[:, di + cd:]), wout=w["ssd_w_out"],
        conv_w=w["ssd_conv_w"], conv_b=_row(w["ssd_conv_b"]),
        dtb=perm(w["ssd_dt_bias"]).reshape(groups, 1, LANE), alog=perm(w["ssd_a_log"]).reshape(groups, 1, LANE),
        dsk=perm(w["ssd_d"]).reshape(groups, 1, LANE), norm_w=_row(w["ssd_norm_w"]),
        ln_g=_row(w["ln_g"][0]), ln_b=_row(w["ln_b"][0]))


def _prep_fox(w):
    wd = w["fox_w_out"].shape[0]
    w_in = w["fox_w_in"]
    return dict(wqkvz=w_in[:, :4 * wd], wf=_pad_last(w_in[:, 4 * wd:], LANE), wout=w["fox_w_out"],
                fb=_pad_last(_row(w["fox_f_bias"]), LANE), ln_g=_row(w["ln_g"][1]), ln_b=_row(w["ln_b"][1]))


def _prep_mla(w):
    rank = w["mla_q_norm"].shape[0]
    wd = w["mla_w_out"].shape[0]
    heads = wd // MLA_V
    w_in = w["mla_w_in"]
    qu = w["mla_w_q_up"].reshape(rank, heads, MLA_NOPE + MLA_ROPE)
    kvu = w["mla_w_kv_up"].reshape(w["mla_w_kv_up"].shape[0], heads, MLA_NOPE + MLA_V)
    return dict(
        wlat=w_in[:, :2 * rank], wkpe=_pad_last(w_in[:, 2 * rank:2 * rank + MLA_ROPE], LANE),
        wz=w_in[:, 2 * rank + MLA_ROPE:],
        wqn=qu[:, :, :MLA_NOPE].reshape(rank, heads * LANE),
        wqp=_pad_last(qu[:, :, MLA_NOPE:], LANE).reshape(rank, heads * LANE),
        wkn=kvu[:, :, :MLA_NOPE].reshape(-1, heads * LANE), wv=kvu[:, :, MLA_NOPE:].reshape(-1, heads * LANE),
        wout=w["mla_w_out"], qn=_row(w["mla_q_norm"]), kvn=_row(w["mla_kv_norm"]),
        ln_g=_row(w["ln_g"][2]), ln_b=_row(w["ln_b"][2]))


def _prep_s5(w):
    wd = w["s5_w_out"].shape[0]
    w_in = w["s5_w_in"]
    lamv, bmat, cmat = _s5_discretize(w["s5_lambda_re"], w["s5_lambda_im"], w["s5_log_step"],
                                      w["s5_b_re"], w["s5_b_im"], w["s5_c_re"], w["s5_c_im"])
    return dict(wu=w_in[:, :wd], wz=w_in[:, wd:], wglu=w["s5_w_glu"], wout=w["s5_w_out"],
                d=_row(w["s5_d"]), bglu=_row(w["s5_b_glu"]), lamv=lamv, bmat=bmat, cmat=cmat,
                ln_g=_row(w["ln_g"][3]), ln_b=_row(w["ln_b"][3]))


def _rope_tables(positions):
    inv_freq = ROPE_BASE ** (-jnp.arange(0, MLA_ROPE, 2, dtype=f32) / MLA_ROPE)
    ang = positions.astype(f32).reshape(-1, 1) * inv_freq
    return jnp.cos(ang), jnp.sin(ang)


def _heads3(v):
    return v.reshape(v.shape[0], -1, LANE)


def _flat2(v):
    return v.reshape(v.shape[0], -1)


def _natural_grads_ssd(g, e_heads):
    return {"ssd_w_in": jnp.concatenate([g["wz"], g["wxbc"], _flat2(_heads3(g["wdt"])[:, :, :e_heads])], axis=1),
            "ssd_w_out": g["wout"]}


def _natural_grads_fox(g, heads):
    return {"fox_w_in": jnp.concatenate([g["wqkvz"], g["wf"][:, :heads]], axis=1), "fox_w_out": g["wout"]}


def _natural_grads_mla(g):
    return {"mla_w_in": jnp.concatenate([g["wlat"], g["wkpe"][:, :MLA_ROPE], g["wz"]], axis=1),
            "mla_w_q_up": _flat2(jnp.concatenate([_heads3(g["wqn"]), _heads3(g["wqp"])[:, :, :MLA_ROPE]], axis=2)),
            "mla_w_kv_up": _flat2(jnp.concatenate([_heads3(g["wkn"]), _heads3(g["wv"])], axis=2)),
            "mla_w_out": g["wout"]}


def _natural_grads_s5(g):
    return {"s5_w_in": jnp.concatenate([g["wu"], g["wz"]], axis=1), "s5_w_glu": g["wglu"], "s5_w_out": g["wout"]}


def _sequence_step(x, positions, tgt, small, big_of_layer, gather_sides=(None, None), exchange_side=None):
    cos, sin = _rope_tables(positions)
    exchange_side = exchange_side or (lambda nat: None)
    vjps = []

    def prepared(prep, big):
        p, vjp = jax.vjp(lambda s: prep({**big, **s}), small)
        vjps.append(vjp)
        return p

    p0 = prepared(_prep_ssd, big_of_layer[0](None))
    h1, s0, got1 = _ssd_layer_fwd(x, p0, side=gather_sides[0])
    p1 = prepared(_prep_fox, big_of_layer[1](got1))
    h2, s1, got23 = _fox_layer_fwd(h1, p1, side=gather_sides[1])
    p2 = prepared(_prep_mla, big_of_layer[2](got23))
    p3 = prepared(_prep_s5, big_of_layer[3](got23))
    p3["pw"] = _s5_powers(p3["lamv"])
    h3, s2 = _mla_layer_fwd(h2, p2, cos, sin)
    h4, s3 = _s5_layer_fwd(h3, p3)
    dh, loss = _loss_head(h4, tgt)
    dh, g3 = _s5_layer_bwd(s3, p3, dh)
    nat3 = _natural_grads_s5(g3)
    dh, g2, parts3 = _mla_layer_bwd(s2, p2, dh, cos, sin, side=exchange_side(nat3))
    nat2 = _natural_grads_mla(g2)
    dh, g1, parts2 = _fox_layer_bwd(s1, p1, dh, side=exchange_side(nat2))
    nat1 = _natural_grads_fox(g1, small["fox_f_bias"].shape[0])
    dh, g0, parts1 = _ssd_layer_bwd(s0, p0, dh, side=exchange_side(nat1))
    nat0 = _natural_grads_ssd(g0, small["ssd_dt_bias"].shape[0] // p0["dtb"].shape[0])
    g_small = None
    for p, vjp, g in zip((p0, p1, p2, p3), vjps, (g0, g1, g2, g3)):
        (gi,) = vjp({k: g[k].astype(p[k].dtype) for k in p if k != "pw"})
        g_small = gi if g_small is None else jax.tree.map(jnp.add, g_small, gi)
    return loss, dh, g_small, (nat0, nat1, nat2, nat3), (parts1, parts2, parts3)


FLAT_COLS = 1024
FLAT_ROW_ALIGN = 128


def _pack(arrs):
    flat = jnp.concatenate([a.reshape(-1) for a in arrs])
    unit = FLAT_COLS * FLAT_ROW_ALIGN
    return jnp.pad(flat, (0, -flat.shape[0] % unit)).reshape(-1, FLAT_COLS)


def _unpack(flat2d, shapes):
    flat = flat2d.reshape(-1)
    out, off = [], 0
    for s in shapes:
        sz = math.prod(s)
        out.append(flat[off:off + sz].reshape(s))
        off += sz
    return out


def _unpack_gathered(gathered, shapes, axes):
    flat = gathered.reshape(N_DEV, -1)
    out, off = [], 0
    for s, ax in zip(shapes, axes):
        sz = math.prod(s)
        seg = flat[:, off:off + sz].reshape((N_DEV,) + tuple(s))
        out.append(jnp.concatenate([seg[d] for d in range(N_DEV)], axis=ax)[0])
        off += sz
    return out


MESH = pl.DeviceIdType.MESH
HBM_SPEC = pl.BlockSpec(memory_space=pl.ANY)


def _comm_scratch(n):
    return [pltpu.SemaphoreType.DMA((n, 7)), pltpu.SemaphoreType.DMA((n, 7)), pltpu.SemaphoreType.DMA((n,))]


class _Gather:
    def __init__(self, xs):
        self.ins = list(xs)
        self.out_shape = [jax.ShapeDtypeStruct((N_DEV,) + v.shape, v.dtype) for v in xs]

    def _plan(self, x_refs, out_refs, sems):
        send_sems, recv_sems, local_sems = sems
        x, y, cc = lax.axis_index("x"), lax.axis_index("y"), lax.axis_index("c")
        me, sibling = (x, y, cc), (x, y, 1 - cc)
        chips = [(1 - x, y), (x, 1 - y), (1 - x, 1 - y)]

        def rows(w, px, py, pc):
            return out_refs[w].at[4 * px + 2 * py + pc]

        def copy(w, k, block, to, src=None):
            return pltpu.make_async_remote_copy(
                src_ref=rows(w, *block) if src is None else src, dst_ref=rows(w, *block),
                send_sem=send_sems.at[w, k], recv_sem=recv_sems.at[w, k], device_id=to, device_id_type=MESH)

        n = len(x_refs)
        mine = [pltpu.make_async_copy(x_refs[w], rows(w, *me), local_sems.at[w]) for w in range(n)]
        first = []
        for w in range(n):
            first.append(copy(w, 0, me, sibling, src=x_refs[w]))
            first += [copy(w, 1 + j, me, (*chip, cc), src=x_refs[w]) for j, chip in enumerate(chips)]
        return n, me, sibling, chips, cc, copy, mine, first

    def start(self, x_refs, out_refs, sems):
        *_, mine, first = self._plan(x_refs, out_refs, sems)
        for cp in mine + first:
            cp.start()

    def finish(self, x_refs, out_refs, sems):
        n, me, sibling, chips, cc, copy, mine, first = self._plan(x_refs, out_refs, sems)
        passed = []
        for w in range(n):
            for j, chip in enumerate(chips):
                copy(w, 1 + j, (*chip, cc), me).wait_recv()
                cp = copy(w, 4 + j, (*chip, cc), sibling)
                cp.start()
                passed.append(cp)
        for w in range(n):
            copy(w, 0, sibling, me).wait_recv()
            for j, chip in enumerate(chips):
                copy(w, 4 + j, (*chip, 1 - cc), me).wait_recv()
        for cp in first + passed:
            cp.wait_send()
        for cp in mine:
            cp.wait()


class _Exchange:
    def __init__(self, gs):
        self.ins = list(gs)
        self.out_shape = [jax.ShapeDtypeStruct(v.shape, v.dtype) for v in gs]

    def _plan(self, g_refs, out_refs, sems):
        send_sems, recv_sems, local_sems = sems
        x, y, cc = lax.axis_index("x"), lax.axis_index("y"), lax.axis_index("c")
        me = 4 * x + 2 * y + cc
        peers = []
        for k in range(1, N_DEV):
            px = 1 - x if (k >> 2) & 1 else x
            py = 1 - y if (k >> 1) & 1 else y
            pc = 1 - cc if k & 1 else cc
            peers.append((k, (px, py, pc), 4 * px + 2 * py + pc))

        def copy(w, k, peer, slot_src, slot_dst):
            return pltpu.make_async_remote_copy(
                src_ref=g_refs[w].at[slot_src], dst_ref=out_refs[w].at[slot_dst],
                send_sem=send_sems.at[w, k - 1], recv_sem=recv_sems.at[w, k - 1], device_id=peer, device_id_type=MESH)

        n = len(g_refs)
        mine = [pltpu.make_async_copy(g_refs[w].at[me], out_refs[w].at[me], local_sems.at[w]) for w in range(n)]
        sends = [copy(w, k, peer, idx, me) for w in range(n) for k, peer, idx in peers]
        return n, peers, copy, mine, sends

    def start(self, g_refs, out_refs, sems):
        *_, mine, sends = self._plan(g_refs, out_refs, sems)
        for cp in mine + sends:
            cp.start()

    def finish(self, g_refs, out_refs, sems):
        n, peers, copy, mine, sends = self._plan(g_refs, out_refs, sems)
        for w in range(n):
            for k, peer, idx in peers:
                copy(w, k, peer, idx, idx).wait_recv()
        for cp in sends:
            cp.wait_send()
        for cp in mine:
            cp.wait()


def _run_comm(side, *, name):
    n = len(side.ins)

    def body(*refs):
        ins, outs, sems = refs[:n], refs[n:2 * n], refs[2 * n:]
        side.start(ins, outs, sems)
        side.finish(ins, outs, sems)

    return pl.pallas_call(
        body, name=name, out_shape=side.out_shape,
        in_specs=[HBM_SPEC] * n, out_specs=[HBM_SPEC] * n, scratch_shapes=_comm_scratch(n),
    )(*side.ins)


def _hosted_call(body, side, *, name, grid, in_specs, out_specs, out_shape, scratch_shapes, args):
    out_specs, out_shape = list(out_specs), list(out_shape)
    if side is None:
        res = pl.pallas_call(
            body, name=name, grid=grid, in_specs=in_specs, out_specs=out_specs, out_shape=out_shape,
            scratch_shapes=scratch_shapes,
            compiler_params=pltpu.CompilerParams(dimension_semantics=("arbitrary",) * len(grid)))(*args)
        return res, None
    n_in, n_out, n_scr, ns = len(in_specs), len(out_specs), len(scratch_shapes), len(side.ins)

    def wrapped(*refs):
        ins, s_ins = refs[:n_in], refs[n_in:n_in + ns]
        outs = refs[n_in + ns:n_in + ns + n_out]
        s_outs = refs[n_in + ns + n_out:n_in + 2 * ns + n_out]
        scr = refs[n_in + 2 * ns + n_out:n_in + 2 * ns + n_out + n_scr]
        sems = refs[n_in + 2 * ns + n_out + n_scr:]
        ids = [pl.program_id(ax) for ax in range(len(grid))]
        first = functools.reduce(lambda p, q: p & q, [i == 0 for i in ids])
        last = functools.reduce(lambda p, q: p & q, [i == g - 1 for i, g in zip(ids, grid)])

        @pl.when(first)
        def _():
            side.start(s_ins, s_outs, sems)

        body(*ins, *outs, *scr)

        @pl.when(last)
        def _():
            side.finish(s_ins, s_outs, sems)

    res = pl.pallas_call(
        wrapped, name=name, grid=grid, in_specs=list(in_specs) + [HBM_SPEC] * ns,
        out_specs=out_specs + [HBM_SPEC] * ns, out_shape=out_shape + side.out_shape,
        scratch_shapes=list(scratch_shapes) + _comm_scratch(ns),
        compiler_params=pltpu.CompilerParams(dimension_semantics=("arbitrary",) * len(grid)))(*args, *side.ins)
    return res[:n_out], res[n_out:]


BIG = (("ssd_w_in", 2), ("ssd_w_out", 1), ("fox_w_in", 2), ("fox_w_out", 1), ("mla_w_in", 2),
       ("mla_w_q_up", 2), ("mla_w_kv_up", 2), ("mla_w_out", 1), ("s5_w_in", 2), ("s5_w_glu", 1), ("s5_w_out", 1))
LAYER_BIG = (("ssd_w_in", "ssd_w_out"), ("fox_w_in", "fox_w_out"),
             ("mla_w_in", "mla_w_q_up", "mla_w_kv_up", "mla_w_out"), ("s5_w_in", "s5_w_glu", "s5_w_out"))
SMALL = (("ssd_conv_w", 2), ("mla_q_norm", 1), ("mla_kv_norm", 1), ("s5_d", 1), ("s5_b_glu", 1))
REPLICATED = ("ln_g", "ln_b", "ssd_conv_b", "ssd_dt_bias", "ssd_a_log", "ssd_d", "ssd_norm_w", "fox_f_bias",
              "s5_lambda_re", "s5_lambda_im", "s5_log_step", "s5_b_re", "s5_b_im", "s5_c_re", "s5_c_im")
WEIGHT_ORDER = ("ln_g", "ln_b", "ssd_w_in", "ssd_conv_w", "ssd_conv_b", "ssd_dt_bias", "ssd_a_log", "ssd_d",
                "ssd_norm_w", "ssd_w_out", "fox_w_in", "fox_f_bias", "fox_w_out", "mla_w_in", "mla_q_norm",
                "mla_kv_norm", "mla_w_q_up", "mla_w_kv_up", "mla_w_out", "s5_w_in", "s5_lambda_re", "s5_lambda_im",
                "s5_log_step", "s5_b_re", "s5_b_im", "s5_c_re", "s5_c_im", "s5_d", "s5_w_glu", "s5_b_glu", "s5_w_out")


def _sum_adamw(parts, w, m, v, *, name):
    r, c = w.shape
    row_bytes = 2 * c * (N_DEV * parts.dtype.itemsize + 7 * 4)
    tr = _pick(r, [t for t in (256, 128, 64, 32, 16) if t * row_bytes <= VMEM_BLOCK_BUDGET])

    def body(p_ref, w_ref, m_ref, v_ref, g_ref, d_ref, m2_ref, v2_ref):
        gv = p_ref[0].astype(f32)
        for s in range(1, N_DEV):
            gv = gv + p_ref[s].astype(f32)
        g_ref[...] = gv
        m2 = ADAM_B1 * m_ref[...] + (1.0 - ADAM_B1) * gv
        v2 = ADAM_B2 * v_ref[...] + (1.0 - ADAM_B2) * jnp.square(gv)
        m_hat = m2 / (1.0 - ADAM_B1 ** ADAM_STEP)
        v_hat = v2 / (1.0 - ADAM_B2 ** ADAM_STEP)
        d_ref[...] = -ADAM_LR * (m_hat / (jnp.sqrt(v_hat) + ADAM_EPS) + ADAM_WD * w_ref[...])
        m2_ref[...] = m2
        v2_ref[...] = v2

    spec = pl.BlockSpec((tr, c), lambda i: (i, 0))
    shp = jax.ShapeDtypeStruct((r, c), f32)
    return pl.pallas_call(
        body, name=name, grid=(r // tr,),
        in_specs=[pl.BlockSpec((N_DEV, tr, c), lambda i: (0, i, 0))] + [spec] * 3,
        out_specs=[spec] * 4, out_shape=[shp] * 4,
        compiler_params=pltpu.CompilerParams(dimension_semantics=("parallel",)),
    )(parts, w, m, v)


def _train_step(a):
    small_names, small_axes = [n for n, _ in SMALL], [ax for _, ax in SMALL]
    small_shapes = [a[n].shape for n in small_names]
    rep_shapes = [a[n].shape for n in REPLICATED]

    big_axis = dict(BIG)

    def shards(names):
        return [a[n][0].astype(bf16) for n in names]

    def natural(names, gathered):
        w = {}
        for n, g in zip(names, gathered):
            if big_axis[n] == 1:
                w[n] = g.reshape(-1, g.shape[2])
            else:
                w[n] = jnp.transpose(g, (1, 0, 2)).reshape(g.shape[1], -1)
        return w

    def to_owner(nat):
        assert tuple(nat) in LAYER_BIG
        blocks = []
        for n, g in nat.items():
            if big_axis[n] == 1:
                g = g.reshape(N_DEV, -1, g.shape[1])
            else:
                g = jnp.transpose(g.reshape(g.shape[0], N_DEV, -1), (1, 0, 2))
            blocks.append(g.astype(bf16))
        return blocks

    small_flat = _pack([a[n] for n in small_names])
    got0 = _run_comm(_Gather(shards(LAYER_BIG[0]) + [small_flat]), name="gather_ssd_weights")
    small = dict(zip(small_names, _unpack_gathered(got0[-1], small_shapes, small_axes)))
    for n in REPLICATED:
        small[n] = a[n] if n in ("ln_g", "ln_b") else a[n][0]
    rest_names = LAYER_BIG[2] + LAYER_BIG[3]
    big_of_layer = (lambda got: natural(LAYER_BIG[0], got0[:-1]),
                    lambda got: natural(LAYER_BIG[1], got),
                    lambda got: natural(LAYER_BIG[2], got[:len(LAYER_BIG[2])]),
                    lambda got: natural(LAYER_BIG[3], got[len(LAYER_BIG[2]):]))
    loss, grad_x, g_small, nats, hosted_parts = _sequence_step(
        a["x"][0], a["positions"][0], a["loss_target"][0], small, big_of_layer,
        gather_sides=(_Gather(shards(LAYER_BIG[1])), _Gather(shards(rest_names))),
        exchange_side=lambda nat: _Exchange(to_owner(nat)))
    small_to_owner = jax.linear_transpose(
        lambda s: _unpack_gathered(s, small_shapes, small_axes), jax.ShapeDtypeStruct(got0[-1].shape, f32))
    (g_small_blocks,) = small_to_owner([g_small[n] for n in small_names])
    parts0 = _run_comm(_Exchange(to_owner(nats[0]) + [g_small_blocks]), name="exchange_ssd_gradients")

    out = {"grad": {}, "delta": {}, "new_m": {}, "new_v": {}}
    kinds = ("grad", "delta", "new_m", "new_v")
    for names, parts in zip(LAYER_BIG, (parts0[:-1],) + tuple(hosted_parts)):
        for n, p in zip(names, parts):
            res = _sum_adamw(p, a[n][0], a["m_" + n][0], a["v_" + n][0], name="adamw_" + n)
            for kind, r in zip(kinds, res):
                out[kind][n] = r[None]
    upd_small = _sum_adamw(parts0[-1], small_flat, _pack([a["m_" + n] for n in small_names]),
                           _pack([a["v_" + n] for n in small_names]), name="adamw_small_sharded")
    for kind, r in zip(kinds, upd_small):
        out[kind].update(zip(small_names, _unpack(r, small_shapes)))
    g_rep_local = _pack([g_small[n] if n in ("ln_g", "ln_b") else g_small[n][None] for n in REPLICATED])
    (g_rep_all,) = _run_comm(_Gather([g_rep_local]), name="gather_replicated_gradients")
    upd_rep = _sum_adamw(g_rep_all, _pack([a[n] for n in REPLICATED]), _pack([a["m_" + n] for n in REPLICATED]),
                         _pack([a["v_" + n] for n in REPLICATED]), name="adamw_replicated")
    for kind, r in zip(kinds, upd_rep):
        out[kind].update(zip(REPLICATED, _unpack(r, rep_shapes)))
    loss_total = lax.psum(loss[0, 0], ("x", "y", "c"))
    return (loss_total, grad_x[None],
            *[out[kind][n] for kind in ("grad", "delta", "new_m", "new_v") for n in WEIGHT_ORDER])


def kernel(x, positions, ln_g, ln_b, ssd_w_in, ssd_conv_w, ssd_conv_b, ssd_dt_bias, ssd_a_log, ssd_d, ssd_norm_w, ssd_w_out, fox_w_in, fox_f_bias, fox_w_out, mla_w_in, mla_q_norm, mla_kv_norm, mla_w_q_up, mla_w_kv_up, mla_w_out, s5_w_in, s5_lambda_re, s5_lambda_im, s5_log_step, s5_b_re, s5_b_im, s5_c_re, s5_c_im, s5_d, s5_w_glu, s5_b_glu, s5_w_out, loss_target, m_ln_g, m_ln_b, m_ssd_w_in, m_ssd_conv_w, m_ssd_conv_b, m_ssd_dt_bias, m_ssd_a_log, m_ssd_d, m_ssd_norm_w, m_ssd_w_out, m_fox_w_in, m_fox_f_bias, m_fox_w_out, m_mla_w_in, m_mla_q_norm, m_mla_kv_norm, m_mla_w_q_up, m_mla_w_kv_up, m_mla_w_out, m_s5_w_in, m_s5_lambda_re, m_s5_lambda_im, m_s5_log_step, m_s5_b_re, m_s5_b_im, m_s5_c_re, m_s5_c_im, m_s5_d, m_s5_w_glu, m_s5_b_glu, m_s5_w_out, v_ln_g, v_ln_b, v_ssd_w_in, v_ssd_conv_w, v_ssd_conv_b, v_ssd_dt_bias, v_ssd_a_log, v_ssd_d, v_ssd_norm_w, v_ssd_w_out, v_fox_w_in, v_fox_f_bias, v_fox_w_out, v_mla_w_in, v_mla_q_norm, v_mla_kv_norm, v_mla_w_q_up, v_mla_w_kv_up, v_mla_w_out, v_s5_w_in, v_s5_lambda_re, v_s5_lambda_im, v_s5_log_step, v_s5_b_re, v_s5_b_im, v_s5_c_re, v_s5_c_im, v_s5_d, v_s5_w_glu, v_s5_b_glu, v_s5_w_out):
    return _train_step(dict(locals()))
```

```python
import functools
import math

import jax
import jax.numpy as jnp
from jax import lax
from jax.experimental import pallas as pl
from jax.experimental.pallas import tpu as pltpu

f32 = jnp.float32
bf16 = jnp.bfloat16

N_DEV = 8
DEPTH = 4
ALPHA = (2.0 * DEPTH) ** 0.25
LN_EPS = 1e-5
RMS_EPS = 1e-6
LANE = 128

SSD_STATE = 128
SSD_HEADDIM = 64
SSD_CHUNK = 128
FOX_HEAD_DIM = 128
MLA_NOPE = 128
MLA_ROPE = 64
MLA_V = 128
ROPE_BASE = 10000.0
S5_GROUP = 16
S5_BLOCK_GROUPS = 8
S5_CHUNK = 128

ADAM_LR = 0.001
ADAM_B1 = 0.9
ADAM_B2 = 0.999
ADAM_EPS = 1e-08
ADAM_WD = 0.01
ADAM_STEP = 10

VMEM_BLOCK_BUDGET = 20 * 1024 * 1024
MM_FULL_K = 2048
HIGHEST = lax.Precision.HIGHEST


def _pick(n, cands):
    for c in cands:
        if n % c == 0:
            return c
    return n


def _dg(a, b, ca, cb):
    return lax.dot_general(a.astype(bf16), b.astype(bf16), (((ca,), (cb,)), ((), ())),
                           preferred_element_type=f32)


@functools.partial(jax.custom_vjp, nondiff_argnums=(2, 3))
def bdot(a, b, ca, cb):
    return _dg(a, b, ca, cb)


def _bdot_fwd(a, b, ca, cb):
    return _dg(a, b, ca, cb), (a, b)


def _bdot_bwd(ca, cb, res, g):
    a, b = res
    nb = 1 - cb
    ma = 1 - ca
    da = _dg(g, b, 1, nb) if ca == 1 else _dg(b, g, nb, 1)
    db = _dg(a, g, ma, 0) if cb == 0 else _dg(g, a, 0, ma)
    return da, db


bdot.defvjp(_bdot_fwd, _bdot_bwd)


def _silu(x):
    return x * jax.nn.sigmoid(x)


def _softplus(x):
    return jnp.maximum(x, 0.0) + jnp.log(1.0 + jnp.exp(-jnp.abs(x)))


def _mm(a, b, *, ta=False, tb=False, add=None, name):
    m, k = (a.shape[1], a.shape[0]) if ta else a.shape
    n = b.shape[0] if tb else b.shape[1]
    kb = b.shape[1] if tb else b.shape[0]
    assert k == kb, (name, a.shape, b.shape)
    if k <= MM_FULL_K:
        tm = _pick(m, (512, 256, 128))
        tn = _pick(n, (1024, 512, 384, 256, 128))
        tk = k
    else:
        tm = _pick(m, (1024, 512, 256, 128))
        tn = _pick(n, (1024, 512, 384, 256, 128))
        tk = _pick(k, (512, 256, 128))
    nk = k // tk
    has_add = add is not None

    def body(*refs):
        if has_add:
            a_ref, b_ref, add_ref, o_ref, acc = refs
        else:
            a_ref, b_ref, o_ref, acc = refs
        kk = pl.program_id(2)

        def prod():
            return _dg(a_ref[...], b_ref[...], 0 if ta else 1, 1 if tb else 0)

        def finish(r):
            o_ref[...] = r + add_ref[...] if has_add else r

        if nk == 1:
            finish(prod())
            return

        @pl.when(kk == 0)
        def _():
            acc[...] = prod()

        @pl.when((kk > 0) & (kk < nk - 1))
        def _():
            acc[...] += prod()

        @pl.when(kk == nk - 1)
        def _():
            finish(acc[...] + prod())

    a_spec = (pl.BlockSpec((tk, tm), lambda i, j, kk: (kk, i)) if ta
              else pl.BlockSpec((tm, tk), lambda i, j, kk: (i, kk)))
    b_spec = (pl.BlockSpec((tn, tk), lambda i, j, kk: (j, kk)) if tb
              else pl.BlockSpec((tk, tn), lambda i, j, kk: (kk, j)))
    o_spec = pl.BlockSpec((tm, tn), lambda i, j, kk: (i, j))
    in_specs = [a_spec, b_spec] + ([o_spec] if has_add else [])
    args = (a, b) + ((add,) if has_add else ())
    return pl.pallas_call(
        body, name=name, grid=(m // tm, n // tn, nk),
        in_specs=in_specs, out_specs=o_spec,
        out_shape=jax.ShapeDtypeStruct((m, n), f32),
        scratch_shapes=[pltpu.VMEM((tm, tn), f32)],
        compiler_params=pltpu.CompilerParams(dimension_semantics=("parallel", "parallel", "arbitrary")),
    )(*args)


def _row_operand(op):
    if isinstance(op, tuple):
        arr, w, cb = op
    else:
        arr, w, cb = op, op.shape[1], 0
    return arr, w, cb


def _rw_tm(t, widths):
    per_row = 2 * 4 * sum(widths)
    tm = 512
    while tm > 8 and (tm * per_row > VMEM_BLOCK_BUDGET or t % tm):
        tm //= 2
    return tm


def _rw_specs(ops, tm):
    specs, arrs = [], []
    for op in ops:
        arr, w, cb = _row_operand(op)
        specs.append(pl.BlockSpec((tm, w), functools.partial(lambda i, cb: (i, cb), cb=cb)))
        arrs.append(arr)
    return specs, arrs


def _param_spec(p):
    nd = p.ndim
    return pl.BlockSpec(p.shape, lambda i: (0,) * nd)


def _rw_fwd(fn, rows, params, out_widths, *, name):
    t = _row_operand(rows[0])[0].shape[0]
    tm = _rw_tm(t, [_row_operand(r)[1] for r in rows] + list(out_widths))
    nr, npar = len(rows), len(params)

    def body(*refs):
        vals = [r[...] for r in refs[:nr + npar]]
        outs = fn(*vals)
        for o_ref, v in zip(refs[nr + npar:], outs):
            o_ref[...] = v

    rspecs, rarrs = _rw_specs(rows, tm)
    return pl.pallas_call(
        body, name=name, grid=(t // tm,),
        in_specs=rspecs + [_param_spec(p) for p in params],
        out_specs=[pl.BlockSpec((tm, w), lambda i: (i, 0)) for w in out_widths],
        out_shape=[jax.ShapeDtypeStruct((t, w), f32) for w in out_widths],
        compiler_params=pltpu.CompilerParams(dimension_semantics=("parallel",)),
    )(*rarrs, *params)


def _rw_bwd(fn, rows, params, cots, *, name, n_const=0):
    t = _row_operand(rows[0])[0].shape[0]
    nr, npar, nc = len(rows), len(params), len(cots)
    nd = nr - n_const
    widths = [_row_operand(r)[1] for r in rows]
    tm = _rw_tm(t, widths + widths[:nd] + [c.shape[1] for c in cots])

    def body(*refs):
        rv = [r[...] for r in refs[:nr]]
        pv = [r[...] for r in refs[nr:nr + npar]]
        cv = [r[...] for r in refs[nr + npar:nr + npar + nc]]
        outs = refs[nr + npar + nc:]
        consts = rv[nd:]

        def f(*diff):
            return fn(*diff[:nd], *consts, *diff[nd:])

        _, vjp = jax.vjp(f, *rv[:nd], *pv)
        g = vjp(tuple(cv))
        for o_ref, v in zip(outs[:nd], g[:nd]):
            o_ref[...] = v
        if npar:
            @pl.when(pl.program_id(0) == 0)
            def _():
                for o_ref in outs[nd:]:
                    o_ref[...] = jnp.zeros_like(o_ref)

            for o_ref, v in zip(outs[nd:], g[nd:]):
                o_ref[...] += v

    rspecs, rarrs = _rw_specs(rows, tm)
    cspecs = [pl.BlockSpec((tm, c.shape[1]), lambda i: (i, 0)) for c in cots]
    return pl.pallas_call(
        body, name=name, grid=(t // tm,),
        in_specs=rspecs + [_param_spec(p) for p in params] + cspecs,
        out_specs=[pl.BlockSpec((tm, w), lambda i: (i, 0)) for w in widths[:nd]]
        + [_param_spec(p) for p in params],
        out_shape=[jax.ShapeDtypeStruct((t, w), f32) for w in widths[:nd]]
        + [jax.ShapeDtypeStruct(p.shape, f32) for p in params],
        compiler_params=pltpu.CompilerParams(dimension_semantics=("arbitrary",)),
    )(*rarrs, *params, *cots)


def _ln_fn(h, o, g, b):
    x = ALPHA * h + o
    mu = jnp.mean(x, -1, keepdims=True)
    var = jnp.mean(jnp.square(x - mu), -1, keepdims=True)
    return ((x - mu) * lax.rsqrt(var + LN_EPS) * g + b,)


def _gate_fn(o, z):
    return (o * _silu(z),)


def _make_ssd_post_fn(groups):
    def fn(y, z, nw):
        yz = y * _silu(z)
        gw = yz.shape[1] // groups
        outs = []
        for g in range(groups):
            blk = yz[:, g * gw:(g + 1) * gw]
            outs.append(blk * lax.rsqrt(jnp.mean(jnp.square(blk), -1, keepdims=True) + RMS_EPS))
        return (jnp.concatenate(outs, axis=1) * nw,)
    return fn


def _logf_fn(f, b):
    return (jax.nn.log_sigmoid(f + b),)


def _make_mla_norm_fn(rank):
    def fn(lat, qn, kvn):
        def rms(x, w):
            return x * lax.rsqrt(jnp.mean(jnp.square(x), -1, keepdims=True) + RMS_EPS) * w
        return rms(lat[:, :rank], qn), rms(lat[:, rank:], kvn)
    return fn


def _rope_block(xb, cos, sin):
    half = MLA_ROPE // 2
    x1, x2 = xb[:, :half], xb[:, half:MLA_ROPE]
    return jnp.concatenate([x1 * cos - x2 * sin, x1 * sin + x2 * cos, xb[:, MLA_ROPE:]], axis=1)


def _make_rope_fn(heads):
    def fn(x, cos, sin):
        return (jnp.concatenate([_rope_block(x[:, h * LANE:(h + 1) * LANE], cos, sin)
                                 for h in range(heads)], axis=1),)
    return fn


def _s5_act_fn(ys, u, d):
    return (jax.nn.gelu(ys + d * u),)


def _s5_glu_fn(y1, gp, z, bg):
    return (y1 * jax.nn.sigmoid(gp + bg) * _silu(z),)


def _loss_head(y, tgt):
    t, d = y.shape
    tm = _rw_tm(t, [d, d, d])

    def body(y_ref, t_ref, dy_ref, l_ref):
        @pl.when(pl.program_id(0) == 0)
        def _():
            l_ref[...] = jnp.zeros_like(l_ref)

        e = y_ref[...] - t_ref[...]
        dy_ref[...] = e * (1.0 / d)
        l_ref[...] += 0.5 * jnp.sum(jnp.mean(jnp.square(e), axis=-1, keepdims=True), axis=0, keepdims=True)

    spec = pl.BlockSpec((tm, d), lambda i: (i, 0))
    return pl.pallas_call(
        body, name="loss_head", grid=(t // tm,), in_specs=[spec, spec],
        out_specs=[spec, pl.BlockSpec((1, 1), lambda i: (0, 0))],
        out_shape=[jax.ShapeDtypeStruct((t, d), f32), jax.ShapeDtypeStruct((1, 1), f32)],
        compiler_params=pltpu.CompilerParams(dimension_semantics=("arbitrary",)),
    )(y, tgt)


def _attn_tile(t):
    return _pick(t, (512, 256, 128))


def _logits_t(k1, q1, k2, q2, ck_col, scale, key0, query0):
    s = _dg(k1, q1, 1, 1)
    if q2 is not None:
        s = s + _dg(k2, q2, 1, 1)
    s = s * scale
    if ck_col is not None:
        s = s - ck_col
    key = key0 + lax.broadcasted_iota(jnp.int32, s.shape, 0)
    query = query0 + lax.broadcasted_iota(jnp.int32, s.shape, 1)
    return jnp.where(key <= query, s, -jnp.inf)


def _lane_bcast(v):
    return jnp.broadcast_to(v[:, :, None], v.shape + (LANE,))


def _attn_fwd(q1, q1o, k1, k1o, v, vo, q2, k2, ck, *, heads, scale, name, side=None):
    t = q1.shape[0]
    tq = tk = _attn_tile(t)
    nq = t // tq
    has2, hasb = q2 is not None, ck is not None

    def body(*refs):
        it = iter(refs)
        q1r, k1r, vr = next(it), next(it), next(it)
        q2r, k2r = (next(it), next(it)) if has2 else (None, None)
        ckr = next(it) if hasb else None
        o_r, lse_r, m_s, l_s, acc = next(it), next(it), next(it), next(it), next(it)
        i, j = pl.program_id(1), pl.program_id(2)

        @pl.when(j == 0)
        def _():
            m_s[...] = jnp.full_like(m_s, -jnp.inf)
            l_s[...] = jnp.zeros_like(l_s)
            acc[...] = jnp.zeros_like(acc)

        @pl.when(j <= i)
        def _():
            q1v, k1v, vv = (r[...].astype(bf16) for r in (q1r, k1r, vr))
            q2v, k2v = (q2r[...].astype(bf16), k2r[...].astype(bf16)) if has2 else (None, None)
            s = _logits_t(k1v, q1v, k2v, q2v, ckr[0][:, 0:1] if hasb else None, scale, j * tk, i * tq)
            m_prev = m_s[...]
            m_new = jnp.maximum(m_prev, jnp.max(s, axis=0, keepdims=True))
            p = jnp.exp(s - m_new)
            a = jnp.exp(m_prev - m_new)
            l_s[...] = a * l_s[...] + jnp.sum(p, axis=0, keepdims=True)
            acc[...] = a * acc[...] + _dg(vv, p, 0, 0)
            m_s[...] = m_new

        @pl.when(j == i)
        def _():
            o_r[...] = (acc[...] / l_s[...]).T
            lse_r[0, 0] = m_s[...] + jnp.log(l_s[...])

    def qmap(off):
        return lambda h, i, j: (i, off + h)

    def kmap(off):
        return lambda h, i, j: (jnp.minimum(j, i), off + h)

    in_specs = [pl.BlockSpec((tq, LANE), qmap(q1o)), pl.BlockSpec((tk, LANE), kmap(k1o)),
                pl.BlockSpec((tk, LANE), kmap(vo))]
    args = [q1, k1, v]
    if has2:
        in_specs += [pl.BlockSpec((tq, LANE), qmap(0)),
                     pl.BlockSpec((tk, LANE), lambda h, i, j: (jnp.minimum(j, i), 0))]
        args += [q2, k2]
    if hasb:
        in_specs += [pl.BlockSpec((1, tk, LANE), lambda h, i, j: (h, jnp.minimum(j, i), 0))]
        args += [ck]
    return _hosted_call(
        body, side, name=name, grid=(heads, nq, nq), in_specs=in_specs,
        out_specs=[pl.BlockSpec((tq, LANE), lambda h, i, j: (i, h)),
                   pl.BlockSpec((1, 1, 1, tq), lambda h, i, j: (h, i, 0, 0))],
        out_shape=[jax.ShapeDtypeStruct((t, heads * LANE), f32),
                   jax.ShapeDtypeStruct((heads, nq, 1, tq), f32)],
        scratch_shapes=[pltpu.VMEM((1, tq), f32), pltpu.VMEM((1, tq), f32), pltpu.VMEM((LANE, tq), f32)],
        args=args)


def _row_sums_as_row(x, first_lane_only=False):
    sel = jnp.ones((8, x.shape[1]), f32)
    if first_lane_only:
        sel = (lax.broadcasted_iota(jnp.int32, sel.shape, 1) == 0).astype(f32)
    return lax.dot_general(sel, x, (((1,), (1,)), ((), ())), precision=HIGHEST, preferred_element_type=f32)[0:1]


def _attn_delta(do, o, *, heads, name):
    t = do.shape[0]
    tq = _attn_tile(t)

    def body(do_ref, o_ref, d_ref):
        d_ref[0, 0] = _row_sums_as_row(do_ref[...] * o_ref[...])

    spec = pl.BlockSpec((tq, LANE), lambda i, h: (i, h))
    return pl.pallas_call(
        body, name=name, grid=(t // tq, heads), in_specs=[spec, spec],
        out_specs=pl.BlockSpec((1, 1, 1, tq), lambda i, h: (h, i, 0, 0)),
        out_shape=jax.ShapeDtypeStruct((heads, t // tq, 1, tq), f32),
        compiler_params=pltpu.CompilerParams(dimension_semantics=("parallel", "parallel")),
    )(do, o)


def _attn_bwd(q1, q1o, k1, k1o, v, vo, q2, k2, ck, do, lse, delta, *, heads, scale, name, side=None):
    t = q1.shape[0]
    tq = tk = _attn_tile(t)
    nq = t // tq
    has2, hasb = q2 is not None, ck is not None

    def body(*refs):
        it = iter(refs)
        q1r, k1r, vr = next(it), next(it), next(it)
        q2r, k2r = (next(it), next(it)) if has2 else (None, None)
        ckr = next(it) if hasb else None
        dor, lser, dlr = next(it), next(it), next(it)
        dq1r, dk1r, dvr = next(it), next(it), next(it)
        dq2r, dk2r = (next(it), next(it)) if has2 else (None, None)
        drowr, dckr = (next(it), next(it)) if hasb else (None, None)
        ak, av = next(it), next(it)
        ac = next(it) if hasb else None
        h, j, i = pl.program_id(0), pl.program_id(1), pl.program_id(2)

        @pl.when((j == 0) & (i == 0))
        def _():
            dq1r[...] = jnp.zeros_like(dq1r)
            if has2:
                dq2r[...] = jnp.zeros_like(dq2r)
            if hasb:
                drowr[...] = jnp.zeros_like(drowr)

        if has2:
            @pl.when((h == 0) & (j == 0) & (i == 0))
            def _():
                dk2r[...] = jnp.zeros_like(dk2r)

        @pl.when(i == j)
        def _():
            ak[...] = jnp.zeros_like(ak)
            av[...] = jnp.zeros_like(av)
            if hasb:
                ac[...] = jnp.zeros_like(ac)

        @pl.when(i >= j)
        def _():
            rows = pl.ds(pl.multiple_of(i * tq, tq), tq)
            cols = pl.ds(pl.multiple_of(j * tk, tk), tk)
            q1v, k1v, vv, dov = (r[...].astype(bf16) for r in (q1r, k1r, vr, dor))
            q2v, k2v = (q2r[...].astype(bf16), k2r[...].astype(bf16)) if has2 else (None, None)
            s = _logits_t(k1v, q1v, k2v, q2v, ckr[0][:, 0:1] if hasb else None, scale, j * tk, i * tq)
            p = jnp.exp(s - lser[0, 0])
            dp = _dg(vv, dov, 1, 1)
            dsr = p * (dp - dlr[0, 0])
            ds = (dsr * scale).astype(bf16)
            av[...] += _dg(p, dov, 1, 0)
            ak[...] += _dg(ds, q1v, 1, 0)
            dq1r[rows, :] += _dg(ds, k1v, 0, 0)
            if has2:
                dq2r[rows, :] += _dg(ds, k2v, 0, 0)
                dk2r[cols, :] += _dg(ds, q2v, 1, 0)
            if hasb:
                drowr[0, pl.ds(i, 1), :] += jnp.sum(dsr, axis=0, keepdims=True)
                ac[...] -= jnp.broadcast_to(jnp.sum(dsr, axis=1, keepdims=True), (tk, LANE))

        @pl.when(i == nq - 1)
        def _():
            dk1r[...] = ak[...]
            dvr[...] = av[...]
            if hasb:
                dckr[0] = _row_sums_as_row(ac[...], first_lane_only=True)

    def qmap(off):
        return lambda h, j, i: (jnp.maximum(i, j), off + h)

    def kmap(off):
        return lambda h, j, i: (j, off + h)

    in_specs = [pl.BlockSpec((tq, LANE), qmap(q1o)), pl.BlockSpec((tk, LANE), kmap(k1o)),
                pl.BlockSpec((tk, LANE), kmap(vo))]
    args = [q1, k1, v]
    if has2:
        in_specs += [pl.BlockSpec((tq, LANE), qmap(0)), pl.BlockSpec((tk, LANE), lambda h, j, i: (j, 0))]
        args += [q2, k2]
    if hasb:
        in_specs += [pl.BlockSpec((1, tk, LANE), lambda h, j, i: (h, j, 0))]
        args += [ck]
    row3 = pl.BlockSpec((1, 1, 1, tq), lambda h, j, i: (h, jnp.maximum(i, j), 0, 0))
    in_specs += [pl.BlockSpec((tq, LANE), qmap(0)), row3, row3]
    args += [do, lse, delta]
    hd = jax.ShapeDtypeStruct((t, heads * LANE), f32)
    head_cols = pl.BlockSpec((t, LANE), lambda h, j, i: (0, h))
    kv_blk = pl.BlockSpec((tk, LANE), kmap(0))
    out_specs, out_shape = [head_cols, kv_blk, kv_blk], [hd, hd, hd]
    scratch = [pltpu.VMEM((tk, LANE), f32)] * 2
    if has2:
        out_specs += [head_cols, pl.BlockSpec((t, LANE), lambda h, j, i: (0, 0))]
        out_shape += [hd, jax.ShapeDtypeStruct((t, LANE), f32)]
    if hasb:
        out_specs += [pl.BlockSpec((1, nq, tq), lambda h, j, i: (h, 0, 0)),
                      pl.BlockSpec((1, 1, tk), lambda h, j, i: (h, 0, j))]
        out_shape += [jax.ShapeDtypeStruct((heads, nq, tq), f32), jax.ShapeDtypeStruct((heads, 1, t), f32)]
        scratch.append(pltpu.VMEM((tk, LANE), f32))
    return _hosted_call(body, side, name=name, grid=(heads, nq, nq), in_specs=in_specs, out_specs=out_specs,
                        out_shape=out_shape, scratch_shapes=scratch, args=args)


def _cumsum_rows(x, *, reverse, name):
    t, w = x.shape
    blk = 128
    nb = t // blk

    def body(x_ref, o_ref):
        r = lax.broadcasted_iota(jnp.int32, (blk, blk), 0)
        c = lax.broadcasted_iota(jnp.int32, (blk, blk), 1)
        tri = ((r <= c) if reverse else (r >= c)).astype(f32)
        carry = jnp.zeros((1, w), f32)
        for b in (reversed(range(nb)) if reverse else range(nb)):
            seg = x_ref[b * blk:(b + 1) * blk, :]
            cs = jnp.dot(tri, seg, precision=HIGHEST, preferred_element_type=f32) + carry
            o_ref[b * blk:(b + 1) * blk, :] = cs
            carry = cs[0:1, :] if reverse else cs[blk - 1:blk, :]

    return pl.pallas_call(body, name=name, out_shape=jax.ShapeDtypeStruct((t, w), f32))(x)


CONV_PAD = 8


def _conv_fn(x, w, b):
    t, taps = x.shape[0], w.shape[0]
    xx = jnp.concatenate([jnp.zeros((CONV_PAD, x.shape[1]), f32), x], axis=0)
    y = b
    for k in range(taps):
        off = CONV_PAD - (taps - 1) + k
        y = y + w[k:k + 1, :] * xx[off:off + t, :]
    return _silu(y)


def _conv_fwd(x, w, b, *, name):
    t, c = x.shape
    tc = LANE
    taps = w.shape[0]

    def body(x_ref, w_ref, b_ref, o_ref):
        o_ref[...] = _conv_fn(x_ref[...], w_ref[...], b_ref[...])

    xs = pl.BlockSpec((t, tc), lambda i: (0, i))
    return pl.pallas_call(
        body, name=name, grid=(c // tc,),
        in_specs=[xs, pl.BlockSpec((taps, tc), lambda i: (0, i)), pl.BlockSpec((1, tc), lambda i: (0, i))],
        out_specs=xs, out_shape=jax.ShapeDtypeStruct((t, c), f32),
        compiler_params=pltpu.CompilerParams(dimension_semantics=("parallel",)),
    )(x, w, b)


def _conv_bwd(x, w, b, ds, *, name):
    t, c = x.shape
    tc = LANE
    taps = w.shape[0]

    def body(x_ref, w_ref, b_ref, ds_ref, dx_ref, dw_ref, db_ref):
        _, vjp = jax.vjp(_conv_fn, x_ref[...], w_ref[...], b_ref[...])
        dx, dw, db = vjp(ds_ref[...])
        dx_ref[...] = dx
        dw_ref[...] = dw
        db_ref[...] = db

    xs = pl.BlockSpec((t, tc), lambda i: (0, i))
    ws = pl.BlockSpec((taps, tc), lambda i: (0, i))
    bs = pl.BlockSpec((1, tc), lambda i: (0, i))
    return pl.pallas_call(
        body, name=name, grid=(c // tc,), in_specs=[xs, ws, bs, xs], out_specs=[xs, ws, bs],
        out_shape=[jax.ShapeDtypeStruct((t, c), f32), jax.ShapeDtypeStruct((taps, c), f32),
                   jax.ShapeDtypeStruct((1, c), f32)],
        compiler_params=pltpu.CompilerParams(dimension_semantics=("parallel",)),
    )(x, w, b, ds)


def _ssd_chunk(x, bm, cm, dtr, st, dtb, alog, dsk, *, e_heads):
    ln, p = x.shape[0], SSD_HEADDIM
    dt = _softplus(dtr + dtb)
    da = dt * (-jnp.exp(alog))
    r = lax.broadcasted_iota(jnp.int32, (ln, ln), 0)
    c = lax.broadcasted_iota(jnp.int32, (ln, ln), 1)
    lower = r >= c
    acs = jnp.dot(lower.astype(f32), da, precision=HIGHEST, preferred_element_type=f32)
    acs_t = acs.T

    def per_head(v):
        return jnp.concatenate([jnp.broadcast_to(v[:, e:e + 1], (v.shape[0], p)) for e in range(e_heads)], axis=1)

    dt_x, acs_x = per_head(dt), per_head(acs)
    last_x = acs_x[ln - 1:ln, :]
    xdt = x * dt_x
    cb = bdot(cm, bm, 1, 1)
    y_off = bdot(cm, st, 1, 0) * jnp.exp(acs_x)
    st_new = st * jnp.exp(last_x) + bdot(bm, xdt * jnp.exp(last_x - acs_x), 0, 0)
    ys = []
    for e in range(e_heads):
        seg = acs[:, e:e + 1] - acs_t[e:e + 1, :]
        dec = jnp.exp(jnp.where(lower, seg, -jnp.inf))
        ys.append(bdot(cb * dec, xdt[:, e * p:(e + 1) * p], 1, 0))
    y = jnp.concatenate(ys, axis=1) + y_off + per_head(dsk) * x
    return y, st_new


def _ssd_specs(t, d_inner, groups):
    ln, n = SSD_CHUNK, SSD_STATE
    gw = d_inner // groups
    nc = t // ln
    return ln, n, gw, nc


def _ssd_fwd(xbc, dtp, dtb, alog, dsk, *, d_inner, groups, name, side=None):
    t = xbc.shape[0]
    ln, n, gw, nc = _ssd_specs(t, d_inner, groups)
    e_heads = gw // SSD_HEADDIM
    boff = d_inner // n

    def body(x_ref, b_ref, c_ref, dt_ref, dtb_ref, alog_ref, dsk_ref, y_ref, save_ref, st_scr):
        c, g = pl.program_id(0), pl.program_id(1)

        @pl.when(c == 0)
        def _():
            st_scr[g] = jnp.zeros((n, gw), f32)

        st = st_scr[g]
        save_ref[0] = st
        y, st_new = _ssd_chunk(x_ref[...], b_ref[...], c_ref[...], dt_ref[...], st,
                               dtb_ref[0], alog_ref[0], dsk_ref[0], e_heads=e_heads)
        y_ref[...] = y
        st_scr[g] = st_new

    par = pl.BlockSpec((1, 1, LANE), lambda c, g: (g, 0, 0))
    return _hosted_call(
        body, side, name=name, grid=(nc, groups),
        in_specs=[pl.BlockSpec((ln, gw), lambda c, g: (c, g)),
                  pl.BlockSpec((ln, n), lambda c, g: (c, boff + g)),
                  pl.BlockSpec((ln, n), lambda c, g: (c, boff + groups + g)),
                  pl.BlockSpec((ln, LANE), lambda c, g: (c, g)), par, par, par],
        out_specs=[pl.BlockSpec((ln, gw), lambda c, g: (c, g)),
                   pl.BlockSpec((1, n, gw), lambda c, g: (c * groups + g, 0, 0))],
        out_shape=[jax.ShapeDtypeStruct((t, d_inner), f32),
                   jax.ShapeDtypeStruct((nc * groups, n, gw), f32)],
        scratch_shapes=[pltpu.VMEM((groups, n, gw), f32)],
        args=(xbc, xbc, xbc, dtp, dtb, alog, dsk))


def _ssd_bwd(xbc, dtp, dtb, alog, dsk, saved, dy, *, d_inner, groups, name, side=None):
    t = xbc.shape[0]
    ln, n, gw, nc = _ssd_specs(t, d_inner, groups)
    e_heads = gw // SSD_HEADDIM
    boff = d_inner // n

    def body(x_ref, b_ref, c_ref, dt_ref, dtb_ref, alog_ref, dsk_ref, save_ref, dy_ref,
             dx_ref, db_ref, dc_ref, ddt_ref, dpar_ref, dst_scr):
        c, g = pl.program_id(0), pl.program_id(1)

        @pl.when(c == 0)
        def _():
            dst_scr[g] = jnp.zeros((n, gw), f32)

        @pl.when((c == 0) & (g == 0))
        def _():
            dpar_ref[...] = jnp.zeros_like(dpar_ref)

        fn = functools.partial(_ssd_chunk, e_heads=e_heads)
        _, vjp = jax.vjp(fn, x_ref[...], b_ref[...], c_ref[...], dt_ref[...], save_ref[0],
                         dtb_ref[0], alog_ref[0], dsk_ref[0])
        gx, gb, gc, gdt, gst, gdtb, galog, gdsk = vjp((dy_ref[...], dst_scr[g]))
        dx_ref[...] = gx
        db_ref[...] = gb
        dc_ref[...] = gc
        ddt_ref[...] = gdt
        dst_scr[g] = gst
        dpar_ref[g, 0:1, :] += gdtb
        dpar_ref[g, 1:2, :] += galog
        dpar_ref[g, 2:3, :] += gdsk

    def rc(c):
        return nc - 1 - c

    par = pl.BlockSpec((1, 1, LANE), lambda c, g: (g, 0, 0))
    xs = pl.BlockSpec((ln, gw), lambda c, g: (rc(c), g))
    ns = pl.BlockSpec((ln, n), lambda c, g: (rc(c), g))
    return _hosted_call(
        body, side, name=name, grid=(nc, groups),
        in_specs=[xs,
                  pl.BlockSpec((ln, n), lambda c, g: (rc(c), boff + g)),
                  pl.BlockSpec((ln, n), lambda c, g: (rc(c), boff + groups + g)),
                  pl.BlockSpec((ln, LANE), lambda c, g: (rc(c), g)), par, par, par,
                  pl.BlockSpec((1, n, gw), lambda c, g: (rc(c) * groups + g, 0, 0)), xs],
        out_specs=[xs, ns, ns, pl.BlockSpec((ln, LANE), lambda c, g: (rc(c), g)),
                   pl.BlockSpec((groups, 8, LANE), lambda c, g: (0, 0, 0))],
        out_shape=[jax.ShapeDtypeStruct((t, d_inner), f32),
                   jax.ShapeDtypeStruct((t, groups * n), f32),
                   jax.ShapeDtypeStruct((t, groups * n), f32),
                   jax.ShapeDtypeStruct((t, groups * LANE), f32),
                   jax.ShapeDtypeStruct((groups, 8, LANE), f32)],
        scratch_shapes=[pltpu.VMEM((groups, n, gw), f32)],
        args=(xbc, xbc, xbc, dtp, dtb, alog, dsk, saved, dy))


S5_TOP = 8


def _cmul(ar, ai, br, bi):
    return ar * br - ai * bi, ar * bi + ai * br


def _s5_scan(scr, base, ln, pw_ref, sp, *, reverse):
    s, k = 1, 0
    while s < ln:
        pr = pw_ref[0, k:k + 1, :sp]
        pi = pw_ref[0, k:k + 1, sp:]
        if reverse:
            pi = -pi
            src, dst = base + s, base
        else:
            src, dst = base, base + s
        ar, ai = scr[src:src + ln - s, :sp], scr[src:src + ln - s, sp:]
        mr, mi = _cmul(ar, ai, pr, pi)
        scr[dst:dst + ln - s, :sp] = scr[dst:dst + ln - s, :sp] + mr
        scr[dst:dst + ln - s, sp:] = scr[dst:dst + ln - s, sp:] + mi
        s *= 2
        k += 1


def _s5_states(u_ref, b_ref, lam_ref, pw_ref, scr, carry_r, carry_i, ln, sp):
    scr[S5_TOP:S5_TOP + ln, :] = _dg(u_ref[...], b_ref[0], 1, 0)
    mr, mi = _cmul(carry_r, carry_i, lam_ref[0][:, :sp], lam_ref[0][:, sp:])
    scr[S5_TOP:S5_TOP + 1, :sp] = scr[S5_TOP:S5_TOP + 1, :sp] + mr
    scr[S5_TOP:S5_TOP + 1, sp:] = scr[S5_TOP:S5_TOP + 1, sp:] + mi
    _s5_scan(scr, S5_TOP, ln, pw_ref, sp, reverse=False)


def _s5_dims(t, u_width):
    cbw = S5_BLOCK_GROUPS * S5_GROUP
    return S5_CHUNK, cbw, u_width // cbw, t // S5_CHUNK


def _s5_fwd(u, lamv, pw, bmat, cmat, *, name):
    t, w = u.shape
    ln, cbw, nb, nc = _s5_dims(t, w)
    sp2 = lamv.shape[2]
    sp = sp2 // 2

    def body(u_ref, lam_ref, pw_ref, b_ref, c_ref, ys_ref, xp_ref, st_ref, scr, carry):
        @pl.when(pl.program_id(1) == 0)
        def _():
            carry[...] = jnp.zeros_like(carry)

        xp_ref[0] = carry[0:1, :]
        _s5_states(u_ref, b_ref, lam_ref, pw_ref, scr, carry[0:1, :sp], carry[0:1, sp:], ln, sp)
        carry[0:1, :] = scr[S5_TOP + ln - 1:S5_TOP + ln, :]
        states = scr[S5_TOP:S5_TOP + ln, :]
        st_ref[0] = states
        ys_ref[...] = _dg(states, c_ref[0], 1, 0)

    def blk(shape):
        return pl.BlockSpec((1,) + shape, lambda j, c: (j, 0, 0))

    return pl.pallas_call(
        body, name=name, grid=(nb, nc),
        in_specs=[pl.BlockSpec((ln, cbw), lambda j, c: (c, j)), blk((1, sp2)), blk((8, sp2)),
                  blk((cbw, sp2)), blk((sp2, cbw))],
        out_specs=[pl.BlockSpec((ln, cbw), lambda j, c: (c, j)),
                   pl.BlockSpec((1, 1, sp2), lambda j, c: (j * nc + c, 0, 0)),
                   pl.BlockSpec((1, ln, sp2), lambda j, c: (j, c, 0))],
        out_shape=[jax.ShapeDtypeStruct((t, w), f32), jax.ShapeDtypeStruct((nb * nc, 1, sp2), f32),
                   jax.ShapeDtypeStruct((nb, t, sp2), f32)],
        scratch_shapes=[pltpu.VMEM((S5_TOP + ln, sp2), f32), pltpu.VMEM((8, sp2), f32)],
        compiler_params=pltpu.CompilerParams(dimension_semantics=("parallel", "arbitrary")),
    )(u, lamv, pw, bmat, cmat)


def _s5_bwd(u, dy, du_skip, xp, states, lamv, pw, bmat, cmat, *, name):
    t, w = u.shape
    ln, cbw, nb, nc = _s5_dims(t, w)
    sp2 = lamv.shape[2]
    sp = sp2 // 2

    def body(u_ref, dy_ref, dus_ref, xp_ref, st_ref, lam_ref, pw_ref, b_ref, c_ref,
             du_ref, db_ref, dc_ref, dl_ref, scr, gsc, gcarry):
        @pl.when(pl.program_id(1) == 0)
        def _():
            gcarry[...] = jnp.zeros_like(gcarry)
            db_ref[...] = jnp.zeros_like(db_ref)
            dc_ref[...] = jnp.zeros_like(dc_ref)
            dl_ref[...] = jnp.zeros_like(dl_ref)

        lr, li = lam_ref[0][:, :sp], lam_ref[0][:, sp:]
        scr[S5_TOP - 1:S5_TOP, :] = xp_ref[0]
        scr[S5_TOP:S5_TOP + ln, :] = st_ref[0]
        dy = dy_ref[...]
        gsc[0:ln, :] = _dg(dy, c_ref[0], 1, 1)
        mr, mi = _cmul(gcarry[0:1, :sp], gcarry[0:1, sp:], lr, -li)
        gsc[ln - 1:ln, :sp] = gsc[ln - 1:ln, :sp] + mr
        gsc[ln - 1:ln, sp:] = gsc[ln - 1:ln, sp:] + mi
        _s5_scan(gsc, 0, ln, pw_ref, sp, reverse=True)
        gcarry[0:1, :] = gsc[0:1, :]
        g = gsc[0:ln, :]
        du_ref[...] = dus_ref[...] + _dg(g, b_ref[0], 1, 1)
        db_ref[0] += _dg(u_ref[...], g, 0, 0)
        dc_ref[0] += _dg(scr[S5_TOP:S5_TOP + ln, :], dy, 0, 0)
        pr, pi = scr[S5_TOP - 1:S5_TOP - 1 + ln, :sp], scr[S5_TOP - 1:S5_TOP - 1 + ln, sp:]
        gr, gi = g[:, :sp], g[:, sp:]
        dl_ref[0, 0:1, :sp] += jnp.sum(pr * gr + pi * gi, axis=0, keepdims=True)
        dl_ref[0, 0:1, sp:] += jnp.sum(pr * gi - pi * gr, axis=0, keepdims=True)

    def rc(c):
        return nc - 1 - c

    def blk(shape):
        return pl.BlockSpec((1,) + shape, lambda j, c: (j, 0, 0))

    row = pl.BlockSpec((ln, cbw), lambda j, c: (rc(c), j))
    return pl.pallas_call(
        body, name=name, grid=(nb, nc),
        in_specs=[row, row, row, pl.BlockSpec((1, 1, sp2), lambda j, c: (j * nc + rc(c), 0, 0)),
                  pl.BlockSpec((1, ln, sp2), lambda j, c: (j, rc(c), 0)),
                  blk((1, sp2)), blk((8, sp2)), blk((cbw, sp2)), blk((sp2, cbw))],
        out_specs=[row, blk((cbw, sp2)), blk((sp2, cbw)), blk((8, sp2))],
        out_shape=[jax.ShapeDtypeStruct((t, w), f32), jax.ShapeDtypeStruct((nb, cbw, sp2), f32),
                   jax.ShapeDtypeStruct((nb, sp2, cbw), f32), jax.ShapeDtypeStruct((nb, 8, sp2), f32)],
        scratch_shapes=[pltpu.VMEM((S5_TOP + ln, sp2), f32), pltpu.VMEM((ln, sp2), f32),
                        pltpu.VMEM((8, sp2), f32)],
        compiler_params=pltpu.CompilerParams(dimension_semantics=("parallel", "arbitrary")),
    )(u, dy, du_skip, xp, states, lamv, pw, bmat, cmat)


def _s5_discretize(lre, lim, log_step, bre, bim, cre, cim):
    g, p = lre.shape
    gb = S5_BLOCK_GROUPS
    nb = g // gb
    lam = lax.complex(lre, lim)
    lam_bar = jnp.exp(lam * jnp.exp(log_step)[:, None])
    b_bar = ((lam_bar - 1.0) / lam)[..., None] * lax.complex(bre, bim)
    lb = lam_bar.reshape(nb, 1, gb * p)
    lamv = jnp.concatenate([jnp.real(lb), jnp.imag(lb)], axis=-1)
    eye = jnp.eye(gb, dtype=f32)
    bb = b_bar.reshape(nb, gb, p, S5_GROUP)

    def bdiag(v):
        return jnp.einsum('jgpi,gh->jgihp', v, eye).reshape(nb, gb * S5_GROUP, gb * p)

    bmat = jnp.concatenate([bdiag(jnp.real(bb)), bdiag(jnp.imag(bb))], axis=-1)

    def cdiag(v):
        return jnp.einsum('jgip,gh->jhpgi', v, eye).reshape(nb, gb * p, gb * S5_GROUP)

    cmat = jnp.concatenate([cdiag(cre.reshape(nb, gb, S5_GROUP, p)),
                            -cdiag(cim.reshape(nb, gb, S5_GROUP, p))], axis=1)
    return lamv, bmat, cmat


def _s5_powers(lamv):
    sp = lamv.shape[2] // 2
    pr, pi = lamv[:, :, :sp], lamv[:, :, sp:]
    rows = []
    for _ in range(8):
        rows.append(jnp.concatenate([pr, pi], axis=-1))
        pr, pi = pr * pr - pi * pi, 2.0 * pr * pi
    return lax.stop_gradient(jnp.concatenate(rows, axis=1))


def _ln_fwd(h, out, p, tag):
    return _rw_fwd(_ln_fn, [h, out], [p["ln_g"], p["ln_b"]], [h.shape[1]], name=tag + "_ln")[0]


def _ln_bwd(h, out, p, dh2, tag):
    return _rw_bwd(_ln_fn, [h, out], [p["ln_g"], p["ln_b"]], [dh2], name=tag + "_ln_bwd")


def _ssd_layer_fwd(h, p, side=None):
    di, groups = p["wz"].shape[1], p["dtb"].shape[0]
    z = _mm(h, p["wz"], name="ssd_z")
    xr = _mm(h, p["wxbc"], name="ssd_xbc")
    dtp = _mm(h, p["wdt"], name="ssd_dt")
    xbc = _conv_fwd(xr, p["conv_w"], p["conv_b"], name="ssd_conv")
    (y, states), got = _ssd_fwd(xbc, dtp, p["dtb"], p["alog"], p["dsk"], d_inner=di, groups=groups,
                                name="ssd_scan", side=side)
    yn = _rw_fwd(_make_ssd_post_fn(groups), [y, z], [p["norm_w"]], [di], name="ssd_post")[0]
    out = _mm(yn, p["wout"], name="ssd_out")
    return _ln_fwd(h, out, p, "ssd"), (h, z, xr, dtp, xbc, states, y, yn, out), got


def _ssd_layer_bwd(saved, p, dh2, side=None):
    h, z, xr, dtp, xbc, states, y, yn, out = saved
    di, groups = p["wz"].shape[1], p["dtb"].shape[0]
    dh, dout, dg, db = _ln_bwd(h, out, p, dh2, "ssd")
    dyn = _mm(dout, p["wout"], tb=True, name="ssd_out_dx")
    g = {"ln_g": dg, "ln_b": db, "wout": _mm(yn, dout, ta=True, name="ssd_out_dw")}
    dy, dz, g["norm_w"] = _rw_bwd(_make_ssd_post_fn(groups), [y, z], [p["norm_w"]], [dyn], name="ssd_post_bwd")
    (dx, dbm, dcm, ddt, dpar), got = _ssd_bwd(xbc, dtp, p["dtb"], p["alog"], p["dsk"], states, dy,
                                              d_inner=di, groups=groups, name="ssd_scan_bwd", side=side)
    g["dtb"], g["alog"], g["dsk"] = dpar[:, 0:1, :], dpar[:, 1:2, :], dpar[:, 2:3, :]
    dxr, g["conv_w"], g["conv_b"] = _conv_bwd(xr, p["conv_w"], p["conv_b"],
                                               jnp.concatenate([dx, dbm, dcm], axis=1), name="ssd_conv_bwd")
    dh = _mm(dz, p["wz"], tb=True, add=dh, name="ssd_z_dx")
    dh = _mm(dxr, p["wxbc"], tb=True, add=dh, name="ssd_xbc_dx")
    dh = _mm(ddt, p["wdt"], tb=True, add=dh, name="ssd_dt_dx")
    g["wz"] = _mm(h, dz, ta=True, name="ssd_z_dw")
    g["wxbc"] = _mm(h, dxr, ta=True, name="ssd_xbc_dw")
    g["wdt"] = _mm(h, ddt, ta=True, name="ssd_dt_dw")
    return dh, g, got


def _fox_layer_fwd(h, p, side=None):
    w = p["wout"].shape[0]
    heads = w // FOX_HEAD_DIM
    t = h.shape[0]
    qkvz = _mm(h, p["wqkvz"], name="fox_qkvz")
    fr = _mm(h, p["wf"], name="fox_f")
    logf = _rw_fwd(_logf_fn, [fr], [p["fb"]], [LANE], name="fox_logf")[0]
    cum = _cumsum_rows(logf, reverse=False, name="fox_cumsum")
    ck = _lane_bcast(cum[:, :heads].T)
    (o, lse), got = _attn_fwd(qkvz, 0, qkvz, heads, qkvz, 2 * heads, None, None, ck,
                              heads=heads, scale=FOX_HEAD_DIM ** -0.5, name="fox_attn", side=side)
    yg = _rw_fwd(_gate_fn, [o, (qkvz, w, 3)], [], [w], name="fox_gate")[0]
    out = _mm(yg, p["wout"], name="fox_out")
    return _ln_fwd(h, out, p, "fox"), (h, qkvz, fr, ck, o, lse, yg, out), got


def _fox_layer_bwd(saved, p, dh2, side=None):
    h, qkvz, fr, ck, o, lse, yg, out = saved
    w = p["wout"].shape[0]
    heads = w // FOX_HEAD_DIM
    t = h.shape[0]
    dh, dout, dg, db = _ln_bwd(h, out, p, dh2, "fox")
    dyg = _mm(dout, p["wout"], tb=True, name="fox_out_dx")
    g = {"ln_g": dg, "ln_b": db, "wout": _mm(yg, dout, ta=True, name="fox_out_dw")}
    do, dz = _rw_bwd(_gate_fn, [o, (qkvz, w, 3)], [], [dyg], name="fox_gate_bwd")
    delta = _attn_delta(do, o, heads=heads, name="fox_delta")
    (dq, dk, dv, drow, dck), got = _attn_bwd(qkvz, 0, qkvz, heads, qkvz, 2 * heads, None, None, ck, do, lse, delta,
                                             heads=heads, scale=FOX_HEAD_DIM ** -0.5, name="fox_attn_bwd", side=side)
    dqkvz = jnp.concatenate([dq, dk, dv, dz], axis=1)
    dcum = jnp.pad((drow.reshape(heads, t) + dck.reshape(heads, t)).T, ((0, 0), (0, LANE - heads)))
    dlogf = _cumsum_rows(dcum, reverse=True, name="fox_cumsum_bwd")
    dfr, g["fb"] = _rw_bwd(_logf_fn, [fr], [p["fb"]], [dlogf], name="fox_logf_bwd")
    dh = _mm(dqkvz, p["wqkvz"], tb=True, add=dh, name="fox_qkvz_dx")
    dh = _mm(dfr, p["wf"], tb=True, add=dh, name="fox_f_dx")
    g["wqkvz"] = _mm(h, dqkvz, ta=True, name="fox_qkvz_dw")
    g["wf"] = _mm(h, dfr, ta=True, name="fox_f_dw")
    return dh, g, got


def _mla_layer_fwd(h, p, cos, sin):
    rank = p["qn"].shape[1]
    w = p["wout"].shape[0]
    heads = w // MLA_V
    lat = _mm(h, p["wlat"], name="mla_lat")
    kpr = _mm(h, p["wkpe"], name="mla_kpe")
    z = _mm(h, p["wz"], name="mla_z")
    qnl, kvnl = _rw_fwd(_make_mla_norm_fn(rank), [lat], [p["qn"], p["kvn"]], [rank, rank], name="mla_norm")
    qno = _mm(qnl, p["wqn"], name="mla_qn")
    qpr = _mm(qnl, p["wqp"], name="mla_qp")
    kno = _mm(kvnl, p["wkn"], name="mla_kn")
    v = _mm(kvnl, p["wv"], name="mla_v")
    qp = _rw_fwd(_make_rope_fn(heads), [qpr, cos, sin], [], [heads * LANE], name="mla_rope_q")[0]
    kp = _rw_fwd(_make_rope_fn(1), [kpr, cos, sin], [], [LANE], name="mla_rope_k")[0]
    scale = (MLA_NOPE + MLA_ROPE) ** -0.5
    (o, lse), _ = _attn_fwd(qno, 0, kno, 0, v, 0, qp, kp, None, heads=heads, scale=scale, name="mla_attn")
    yg = _rw_fwd(_gate_fn, [o, z], [], [w], name="mla_gate")[0]
    out = _mm(yg, p["wout"], name="mla_out")
    return _ln_fwd(h, out, p, "mla"), (h, lat, kpr, z, qnl, kvnl, qno, qpr, kno, v, qp, kp, o, lse, yg, out)


def _mla_layer_bwd(saved, p, dh2, cos, sin, side=None):
    h, lat, kpr, z, qnl, kvnl, qno, qpr, kno, v, qp, kp, o, lse, yg, out = saved
    rank = p["qn"].shape[1]
    w = p["wout"].shape[0]
    heads = w // MLA_V
    dh, dout, dg, db = _ln_bwd(h, out, p, dh2, "mla")
    dyg = _mm(dout, p["wout"], tb=True, name="mla_out_dx")
    g = {"ln_g": dg, "ln_b": db, "wout": _mm(yg, dout, ta=True, name="mla_out_dw")}
    do, dz = _rw_bwd(_gate_fn, [o, z], [], [dyg], name="mla_gate_bwd")
    delta = _attn_delta(do, o, heads=heads, name="mla_delta")
    (dqno, dkno, dv, dqp, dkp), got = _attn_bwd(qno, 0, kno, 0, v, 0, qp, kp, None, do, lse, delta, heads=heads,
                                                scale=(MLA_NOPE + MLA_ROPE) ** -0.5, name="mla_attn_bwd", side=side)
    (dqpr,) = _rw_bwd(_make_rope_fn(heads), [qpr, cos, sin], [], [dqp], n_const=2, name="mla_rope_q_bwd")
    (dkpr,) = _rw_bwd(_make_rope_fn(1), [kpr, cos, sin], [], [dkp], n_const=2, name="mla_rope_k_bwd")
    dqnl = _mm(dqno, p["wqn"], tb=True, name="mla_qn_dx")
    dqnl = _mm(dqpr, p["wqp"], tb=True, add=dqnl, name="mla_qp_dx")
    dkvnl = _mm(dkno, p["wkn"], tb=True, name="mla_kn_dx")
    dkvnl = _mm(dv, p["wv"], tb=True, add=dkvnl, name="mla_v_dx")
    g["wqn"] = _mm(qnl, dqno, ta=True, name="mla_qn_dw")
    g["wqp"] = _mm(qnl, dqpr, ta=True, name="mla_qp_dw")
    g["wkn"] = _mm(kvnl, dkno, ta=True, name="mla_kn_dw")
    g["wv"] = _mm(kvnl, dv, ta=True, name="mla_v_dw")
    dlat, g["qn"], g["kvn"] = _rw_bwd(_make_mla_norm_fn(rank), [lat], [p["qn"], p["kvn"]], [dqnl, dkvnl],
                                     name="mla_norm_bwd")
    dh = _mm(dlat, p["wlat"], tb=True, add=dh, name="mla_lat_dx")
    dh = _mm(dkpr, p["wkpe"], tb=True, add=dh, name="mla_kpe_dx")
    dh = _mm(dz, p["wz"], tb=True, add=dh, name="mla_z_dx")
    g["wlat"] = _mm(h, dlat, ta=True, name="mla_lat_dw")
    g["wkpe"] = _mm(h, dkpr, ta=True, name="mla_kpe_dw")
    g["wz"] = _mm(h, dz, ta=True, name="mla_z_dw")
    return dh, g, got


def _s5_layer_fwd(h, p):
    w = p["wout"].shape[0]
    u = _mm(h, p["wu"], name="s5_u")
    z = _mm(h, p["wz"], name="s5_z")
    ys, xp, states = _s5_fwd(u, p["lamv"], p["pw"], p["bmat"], p["cmat"], name="s5_scan")
    y1 = _rw_fwd(_s5_act_fn, [ys, u], [p["d"]], [w], name="s5_act")[0]
    gp = _mm(y1, p["wglu"], name="s5_glu")
    y2 = _rw_fwd(_s5_glu_fn, [y1, gp, z], [p["bglu"]], [w], name="s5_gate")[0]
    out = _mm(y2, p["wout"], name="s5_out")
    return _ln_fwd(h, out, p, "s5"), (h, u, z, ys, xp, states, y1, gp, y2, out)


def _s5_layer_bwd(saved, p, dh2):
    h, u, z, ys, xp, states, y1, gp, y2, out = saved
    dh, dout, dg, db = _ln_bwd(h, out, p, dh2, "s5")
    dy2 = _mm(dout, p["wout"], tb=True, name="s5_out_dx")
    g = {"ln_g": dg, "ln_b": db, "wout": _mm(y2, dout, ta=True, name="s5_out_dw")}
    dy1, dgp, dz, g["bglu"] = _rw_bwd(_s5_glu_fn, [y1, gp, z], [p["bglu"]], [dy2], name="s5_gate_bwd")
    dy1 = _mm(dgp, p["wglu"], tb=True, add=dy1, name="s5_glu_dx")
    g["wglu"] = _mm(y1, dgp, ta=True, name="s5_glu_dw")
    dys, du, g["d"] = _rw_bwd(_s5_act_fn, [ys, u], [p["d"]], [dy1], name="s5_act_bwd")
    du, g["bmat"], g["cmat"], dlam = _s5_bwd(u, dys, du, xp, states, p["lamv"], p["pw"], p["bmat"], p["cmat"],
                                             name="s5_scan_bwd")
    g["lamv"] = dlam[:, 0:1, :]
    dh = _mm(du, p["wu"], tb=True, add=dh, name="s5_u_dx")
    dh = _mm(dz, p["wz"], tb=True, add=dh, name="s5_z_dx")
    g["wu"] = _mm(h, du, ta=True, name="s5_u_dw")
    g["wz"] = _mm(h, dz, ta=True, name="s5_z_dw")
    return dh, g


def _pad_last(a, to):
    return jnp.pad(a, [(0, 0)] * (a.ndim - 1) + [(0, to - a.shape[-1])])


def _row(v):
    return v.reshape(1, -1)


def _prep_ssd(w):
    di, cd, hh = w["ssd_w_out"].shape[0], w["ssd_conv_b"].shape[0], w["ssd_dt_bias"].shape[0]
    groups = (cd - di) // (2 * SSD_STATE)
    e = hh // groups

    def perm(v):
        lead = v.shape[:-1]
        return _pad_last(v.reshape(lead + (groups, e)), LANE).reshape(lead + (groups * LANE,))

    w_in = w["ssd_w_in"]
    return dict(
        wz=w_in[:, :di], wxbc=w_in[:, di:di + cd], wdt=perm(w_in[:, di + cd:]), wout=w["ssd_w_out"],
        conv_w=w["ssd_conv_w"], conv_b=_row(w["ssd_conv_b"]),
        dtb=perm(w["ssd_dt_bias"]).reshape(groups, 1, LANE), alog=perm(w["ssd_a_log"]).reshape(groups, 1, LANE),
        dsk=perm(w["ssd_d"]).reshape(groups, 1, LANE), norm_w=_row(w["ssd_norm_w"]),
        ln_g=_row(w["ln_g"][0]), ln_b=_row(w["ln_b"][0]))


def _prep_fox(w):
    wd = w["fox_w_out"].shape[0]
    w_in = w["fox_w_in"]
    return dict(wqkvz=w_in[:, :4 * wd], wf=_pad_last(w_in[:, 4 * wd:], LANE), wout=w["fox_w_out"],
                fb=_pad_last(_row(w["fox_f_bias"]), LANE), ln_g=_row(w["ln_g"][1]), ln_b=_row(w["ln_b"][1]))


def _prep_mla(w):
    rank = w["mla_q_norm"].shape[0]
    wd = w["mla_w_out"].shape[0]
    heads = wd // MLA_V
    w_in = w["mla_w_in"]
    qu = w["mla_w_q_up"].reshape(rank, heads, MLA_NOPE + MLA_ROPE)
    kvu = w["mla_w_kv_up"].reshape(w["mla_w_kv_up"].shape[0], heads, MLA_NOPE + MLA_V)
    return dict(
        wlat=w_in[:, :2 * rank], wkpe=_pad_last(w_in[:, 2 * rank:2 * rank + MLA_ROPE], LANE),
        wz=w_in[:, 2 * rank + MLA_ROPE:],
        wqn=qu[:, :, :MLA_NOPE].reshape(rank, heads * LANE),
        wqp=_pad_last(qu[:, :, MLA_NOPE:], LANE).reshape(rank, heads * LANE),
        wkn=kvu[:, :, :MLA_NOPE].reshape(-1, heads * LANE), wv=kvu[:, :, MLA_NOPE:].reshape(-1, heads * LANE),
        wout=w["mla_w_out"], qn=_row(w["mla_q_norm"]), kvn=_row(w["mla_kv_norm"]),
        ln_g=_row(w["ln_g"][2]), ln_b=_row(w["ln_b"][2]))


def _prep_s5(w):
    wd = w["s5_w_out"].shape[0]
    w_in = w["s5_w_in"]
    lamv, bmat, cmat = _s5_discretize(w["s5_lambda_re"], w["s5_lambda_im"], w["s5_log_step"],
                                      w["s5_b_re"], w["s5_b_im"], w["s5_c_re"], w["s5_c_im"])
    return dict(wu=w_in[:, :wd], wz=w_in[:, wd:], wglu=w["s5_w_glu"], wout=w["s5_w_out"],
                d=_row(w["s5_d"]), bglu=_row(w["s5_b_glu"]), lamv=lamv, bmat=bmat, cmat=cmat,
                ln_g=_row(w["ln_g"][3]), ln_b=_row(w["ln_b"][3]))


def _rope_tables(positions):
    inv_freq = ROPE_BASE ** (-jnp.arange(0, MLA_ROPE, 2, dtype=f32) / MLA_ROPE)
    ang = positions.astype(f32).reshape(-1, 1) * inv_freq
    return jnp.cos(ang), jnp.sin(ang)


def _heads3(v):
    return v.reshape(v.shape[0], -1, LANE)


def _flat2(v):
    return v.reshape(v.shape[0], -1)


def _natural_grads_ssd(g, e_heads):
    return {"ssd_w_in": jnp.concatenate([g["wz"], g["wxbc"], _flat2(_heads3(g["wdt"])[:, :, :e_heads])], axis=1),
            "ssd_w_out": g["wout"]}


def _natural_grads_fox(g, heads):
    return {"fox_w_in": jnp.concatenate([g["wqkvz"], g["wf"][:, :heads]], axis=1), "fox_w_out": g["wout"]}


def _natural_grads_mla(g):
    return {"mla_w_in": jnp.concatenate([g["wlat"], g["wkpe"][:, :MLA_ROPE], g["wz"]], axis=1),
            "mla_w_q_up": _flat2(jnp.concatenate([_heads3(g["wqn"]), _heads3(g["wqp"])[:, :, :MLA_ROPE]], axis=2)),
            "mla_w_kv_up": _flat2(jnp.concatenate([_heads3(g["wkn"]), _heads3(g["wv"])], axis=2)),
            "mla_w_out": g["wout"]}


def _natural_grads_s5(g):
    return {"s5_w_in": jnp.concatenate([g["wu"], g["wz"]], axis=1), "s5_w_glu": g["wglu"], "s5_w_out": g["wout"]}


def _sequence_step(x, positions, tgt, small, big_of_layer, gather_sides=(None, None), exchange_side=None):
    cos, sin = _rope_tables(positions)
    exchange_side = exchange_side or (lambda nat: None)
    vjps = []

    def prepared(prep, big):
        p, vjp = jax.vjp(lambda s: prep({**big, **s}), small)
        vjps.append(vjp)
        return p

    p0 = prepared(_prep_ssd, big_of_layer[0](None))
    h1, s0, got1 = _ssd_layer_fwd(x, p0, side=gather_sides[0])
    p1 = prepared(_prep_fox, big_of_layer[1](got1))
    h2, s1, got23 = _fox_layer_fwd(h1, p1, side=gather_sides[1])
    p2 = prepared(_prep_mla, big_of_layer[2](got23))
    p3 = prepared(_prep_s5, big_of_layer[3](got23))
    p3["pw"] = _s5_powers(p3["lamv"])
    h3, s2 = _mla_layer_fwd(h2, p2, cos, sin)
    h4, s3 = _s5_layer_fwd(h3, p3)
    dh, loss = _loss_head(h4, tgt)
    dh, g3 = _s5_layer_bwd(s3, p3, dh)
    nat3 = _natural_grads_s5(g3)
    dh, g2, parts3 = _mla_layer_bwd(s2, p2, dh, cos, sin, side=exchange_side(nat3))
    nat2 = _natural_grads_mla(g2)
    dh, g1, parts2 = _fox_layer_bwd(s1, p1, dh, side=exchange_side(nat2))
    nat1 = _natural_grads_fox(g1, small["fox_f_bias"].shape[0])
    dh, g0, parts1 = _ssd_layer_bwd(s0, p0, dh, side=exchange_side(nat1))
    nat0 = _natural_grads_ssd(g0, small["ssd_dt_bias"].shape[0] // p0["dtb"].shape[0])
    g_small = None
    for p, vjp, g in zip((p0, p1, p2, p3), vjps, (g0, g1, g2, g3)):
        (gi,) = vjp({k: g[k].astype(p[k].dtype) for k in p if k != "pw"})
        g_small = gi if g_small is None else jax.tree.map(jnp.add, g_small, gi)
    return loss, dh, g_small, (nat0, nat1, nat2, nat3), (parts1, parts2, parts3)


FLAT_COLS = 1024
FLAT_ROW_ALIGN = 128


def _pack(arrs):
    flat = jnp.concatenate([a.reshape(-1) for a in arrs])
    unit = FLAT_COLS * FLAT_ROW_ALIGN
    return jnp.pad(flat, (0, -flat.shape[0] % unit)).reshape(-1, FLAT_COLS)


def _unpack(flat2d, shapes):
    flat = flat2d.reshape(-1)
    out, off = [], 0
    for s in shapes:
        sz = math.prod(s)
        out.append(flat[off:off + sz].reshape(s))
        off += sz
    return out


def _unpack_gathered(gathered, shapes, axes):
    flat = gathered.reshape(N_DEV, -1)
    out, off = [], 0
    for s, ax in zip(shapes, axes):
        sz = math.prod(s)
        seg = flat[:, off:off + sz].reshape((N_DEV,) + tuple(s))
        out.append(jnp.concatenate([seg[d] for d in range(N_DEV)], axis=ax)[0])
        off += sz
    return out


MESH = pl.DeviceIdType.MESH
HBM_SPEC = pl.BlockSpec(memory_space=pl.ANY)


def _comm_scratch(n):
    return [pltpu.SemaphoreType.DMA((n, 7)), pltpu.SemaphoreType.DMA((n, 7)), pltpu.SemaphoreType.DMA((n,))]


class _Gather:
    def __init__(self, xs):
        self.ins = list(xs)
        self.out_shape = [jax.ShapeDtypeStruct((N_DEV,) + v.shape, v.dtype) for v in xs]

    def _plan(self, x_refs, out_refs, sems):
        send_sems, recv_sems, local_sems = sems
        x, y, cc = lax.axis_index("x"), lax.axis_index("y"), lax.axis_index("c")
        me, sibling = (x, y, cc), (x, y, 1 - cc)
        chips = [(1 - x, y), (x, 1 - y), (1 - x, 1 - y)]

        def rows(w, px, py, pc):
            return out_refs[w].at[4 * px + 2 * py + pc]

        def copy(w, k, block, to, src=None):
            return pltpu.make_async_remote_copy(
                src_ref=rows(w, *block) if src is None else src, dst_ref=rows(w, *block),
                send_sem=send_sems.at[w, k], recv_sem=recv_sems.at[w, k], device_id=to, device_id_type=MESH)

        n = len(x_refs)
        mine = [pltpu.make_async_copy(x_refs[w], rows(w, *me), local_sems.at[w]) for w in range(n)]
        first = []
        for w in range(n):
            first.append(copy(w, 0, me, sibling, src=x_refs[w]))
            first += [copy(w, 1 + j, me, (*chip, cc), src=x_refs[w]) for j, chip in enumerate(chips)]
        return n, me, sibling, chips, cc, copy, mine, first

    def start(self, x_refs, out_refs, sems):
        *_, mine, first = self._plan(x_refs, out_refs, sems)
        for cp in mine + first:
            cp.start()

    def finish(self, x_refs, out_refs, sems):
        n, me, sibling, chips, cc, copy, mine, first = self._plan(x_refs, out_refs, sems)
        passed = []
        for w in range(n):
            for j, chip in enumerate(chips):
                copy(w, 1 + j, (*chip, cc), me).wait_recv()
                cp = copy(w, 4 + j, (*chip, cc), sibling)
                cp.start()
                passed.append(cp)
        for w in range(n):
            copy(w, 0, sibling, me).wait_recv()
            for j, chip in enumerate(chips):
                copy(w, 4 + j, (*chip, 1 - cc), me).wait_recv()
        for cp in first + passed:
            cp.wait_send()
        for cp in mine:
            cp.wait()


class _Exchange:
    def __init__(self, gs):
        self.ins = list(gs)
        self.out_shape = [jax.ShapeDtypeStruct(v.shape, v.dtype) for v in gs]

    def _plan(self, g_refs, out_refs, sems):
        send_sems, recv_sems, local_sems = sems
        x, y, cc = lax.axis_index("x"), lax.axis_index("y"), lax.axis_index("c")
        me = 4 * x + 2 * y + cc
        peers = []
        for k in range(1, N_DEV):
            px = 1 - x if (k >> 2) & 1 else x
            py = 1 - y if (k >> 1) & 1 else y
            pc = 1 - cc if k & 1 else cc
            peers.append((k, (px, py, pc), 4 * px + 2 * py + pc))

        def copy(w, k, peer, slot_src, slot_dst):
            return pltpu.make_async_remote_copy(
                src_ref=g_refs[w].at[slot_src], dst_ref=out_refs[w].at[slot_dst],
                send_sem=send_sems.at[w, k - 1], recv_sem=recv_sems.at[w, k - 1], device_id=peer, device_id_type=MESH)

        n = len(g_refs)
        mine = [pltpu.make_async_copy(g_refs[w].at[me], out_refs[w].at[me], local_sems.at[w]) for w in range(n)]
        sends = [copy(w, k, peer, idx, me) for w in range(n) for k, peer, idx in peers]
        return n, peers, copy, mine, sends

    def start(self, g_refs, out_refs, sems):
        *_, mine, sends = self._plan(g_refs, out_refs, sems)
        for cp in mine + sends:
            cp.start()

    def finish(self, g_refs, out_refs, sems):
        n, peers, copy, mine, sends = self._plan(g_refs, out_refs, sems)
        for w in range(n):
            for k, peer, idx in peers:
                copy(w, k, peer, idx, idx).wait_recv()
        for cp in sends:
            cp.wait_send()
        for cp in mine:
            cp.wait()


def _run_comm(side, *, name):
    n = len(side.ins)

    def body(*refs):
        ins, outs, sems = refs[:n], refs[n:2 * n], refs[2 * n:]
        side.start(ins, outs, sems)
        side.finish(ins, outs, sems)

    return pl.pallas_call(
        body, name=name, out_shape=side.out_shape,
        in_specs=[HBM_SPEC] * n, out_specs=[HBM_SPEC] * n, scratch_shapes=_comm_scratch(n),
    )(*side.ins)


def _hosted_call(body, side, *, name, grid, in_specs, out_specs, out_shape, scratch_shapes, args):
    out_specs, out_shape = list(out_specs), list(out_shape)
    if side is None:
        res = pl.pallas_call(
            body, name=name, grid=grid, in_specs=in_specs, out_specs=out_specs, out_shape=out_shape,
            scratch_shapes=scratch_shapes,
            compiler_params=pltpu.CompilerParams(dimension_semantics=("arbitrary",) * len(grid)))(*args)
        return res, None
    n_in, n_out, n_scr, ns = len(in_specs), len(out_specs), len(scratch_shapes), len(side.ins)

    def wrapped(*refs):
        ins, s_ins = refs[:n_in], refs[n_in:n_in + ns]
        outs = refs[n_in + ns:n_in + ns + n_out]
        s_outs = refs[n_in + ns + n_out:n_in + 2 * ns + n_out]
        scr = refs[n_in + 2 * ns + n_out:n_in + 2 * ns + n_out + n_scr]
        sems = refs[n_in + 2 * ns + n_out + n_scr:]
        ids = [pl.program_id(ax) for ax in range(len(grid))]
        first = functools.reduce(lambda p, q: p & q, [i == 0 for i in ids])
        last = functools.reduce(lambda p, q: p & q, [i == g - 1 for i, g in zip(ids, grid)])

        @pl.when(first)
        def _():
            side.start(s_ins, s_outs, sems)

        body(*ins, *outs, *scr)

        @pl.when(last)
        def _():
            side.finish(s_ins, s_outs, sems)

    res = pl.pallas_call(
        wrapped, name=name, grid=grid, in_specs=list(in_specs) + [HBM_SPEC] * ns,
        out_specs=out_specs + [HBM_SPEC] * ns, out_shape=out_shape + side.out_shape,
        scratch_shapes=list(scratch_shapes) + _comm_scratch(ns),
        compiler_params=pltpu.CompilerParams(dimension_semantics=("arbitrary",) * len(grid)))(*args, *side.ins)
    return res[:n_out], res[n_out:]


BIG = (("ssd_w_in", 2), ("ssd_w_out", 1), ("fox_w_in", 2), ("fox_w_out", 1), ("mla_w_in", 2),
       ("mla_w_q_up", 2), ("mla_w_kv_up", 2), ("mla_w_out", 1), ("s5_w_in", 2), ("s5_w_glu", 1), ("s5_w_out", 1))
LAYER_BIG = (("ssd_w_in", "ssd_w_out"), ("fox_w_in", "fox_w_out"),
             ("mla_w_in", "mla_w_q_up", "mla_w_kv_up", "mla_w_out"), ("s5_w_in", "s5_w_glu", "s5_w_out"))
SMALL = (("ssd_conv_w", 2), ("mla_q_norm", 1), ("mla_kv_norm", 1), ("s5_d", 1), ("s5_b_glu", 1))
REPLICATED = ("ln_g", "ln_b", "ssd_conv_b", "ssd_dt_bias", "ssd_a_log", "ssd_d", "ssd_norm_w", "fox_f_bias",
              "s5_lambda_re", "s5_lambda_im", "s5_log_step", "s5_b_re", "s5_b_im", "s5_c_re", "s5_c_im")
WEIGHT_ORDER = ("ln_g", "ln_b", "ssd_w_in", "ssd_conv_w", "ssd_conv_b", "ssd_dt_bias", "ssd_a_log", "ssd_d",
                "ssd_norm_w", "ssd_w_out", "fox_w_in", "fox_f_bias", "fox_w_out", "mla_w_in", "mla_q_norm",
                "mla_kv_norm", "mla_w_q_up", "mla_w_kv_up", "mla_w_out", "s5_w_in", "s5_lambda_re", "s5_lambda_im",
                "s5_log_step", "s5_b_re", "s5_b_im", "s5_c_re", "s5_c_im", "s5_d", "s5_w_glu", "s5_b_glu", "s5_w_out")


def _sum_adamw(parts, w, m, v, *, name):
    r, c = w.shape
    row_bytes = 2 * c * (N_DEV * parts.dtype.itemsize + 7 * 4)
    tr = _pick(r, [t for t in (256, 128, 64, 32, 16) if t * row_bytes <= VMEM_BLOCK_BUDGET])

    def body(p_ref, w_ref, m_ref, v_ref, g_ref, d_ref, m2_ref, v2_ref):
        gv = p_ref[0].astype(f32)
        for s in range(1, N_DEV):
            gv = gv + p_ref[s].astype(f32)
        g_ref[...] = gv
        m2 = ADAM_B1 * m_ref[...] + (1.0 - ADAM_B1) * gv
        v2 = ADAM_B2 * v_ref[...] + (1.0 - ADAM_B2) * jnp.square(gv)
        m_hat = m2 / (1.0 - ADAM_B1 ** ADAM_STEP)
        v_hat = v2 / (1.0 - ADAM_B2 ** ADAM_STEP)
        d_ref[...] = -ADAM_LR * (m_hat / (jnp.sqrt(v_hat) + ADAM_EPS) + ADAM_WD * w_ref[...])
        m2_ref[...] = m2
        v2_ref[...] = v2

    spec = pl.BlockSpec((tr, c), lambda i: (i, 0))
    shp = jax.ShapeDtypeStruct((r, c), f32)
    return pl.pallas_call(
        body, name=name, grid=(r // tr,),
        in_specs=[pl.BlockSpec((N_DEV, tr, c), lambda i: (0, i, 0))] + [spec] * 3,
        out_specs=[spec] * 4, out_shape=[shp] * 4,
        compiler_params=pltpu.CompilerParams(dimension_semantics=("parallel",)),
    )(parts, w, m, v)


def _train_step(a):
    small_names, small_axes = [n for n, _ in SMALL], [ax for _, ax in SMALL]
    small_shapes = [a[n].shape for n in small_names]
    rep_shapes = [a[n].shape for n in REPLICATED]

    big_axis = dict(BIG)

    def shards(names):
        return [a[n][0].astype(bf16) for n in names]

    def natural(names, gathered):
        w = {}
        for n, g in zip(names, gathered):
            if big_axis[n] == 1:
                w[n] = g.reshape(-1, g.shape[2])
            else:
                w[n] = jnp.transpose(g, (1, 0, 2)).reshape(g.shape[1], -1)
        return w

    def to_owner(nat):
        assert tuple(nat) in LAYER_BIG
        blocks = []
        for n, g in nat.items():
            if big_axis[n] == 1:
                g = g.reshape(N_DEV, -1, g.shape[1])
            else:
                g = jnp.transpose(g.reshape(g.shape[0], N_DEV, -1), (1, 0, 2))
            blocks.append(g.astype(bf16))
        return blocks

    small_flat = _pack([a[n] for n in small_names])
    got0 = _run_comm(_Gather(shards(LAYER_BIG[0]) + [small_flat]), name="gather_ssd_weights")
    small = dict(zip(small_names, _unpack_gathered(got0[-1], small_shapes, small_axes)))
    for n in REPLICATED:
        small[n] = a[n] if n in ("ln_g", "ln_b") else a[n][0]
    rest_names = LAYER_BIG[2] + LAYER_BIG[3]
    big_of_layer = (lambda got: natural(LAYER_BIG[0], got0[:-1]),
                    lambda got: natural(LAYER_BIG[1], got),
                    lambda got: natural(LAYER_BIG[2], got[:len(LAYER_BIG[2])]),
                    lambda got: natural(LAYER_BIG[3], got[len(LAYER_BIG[2]):]))
    loss, grad_x, g_small, nats, hosted_parts = _sequence_step(
        a["x"][0], a["positions"][0], a["loss_target"][0], small, big_of_layer,
        gather_sides=(_Gather(shards(LAYER_BIG[1])), _Gather(shards(rest_names))),
        exchange_side=lambda nat: _Exchange(to_owner(nat)))
    small_to_owner = jax.linear_transpose(
        lambda s: _unpack_gathered(s, small_shapes, small_axes), jax.ShapeDtypeStruct(got0[-1].shape, f32))
    (g_small_blocks,) = small_to_owner([g_small[n] for n in small_names])
    parts0 = _run_comm(_Exchange(to_owner(nats[0]) + [g_small_blocks]), name="exchange_ssd_gradients")

    out = {"grad": {}, "delta": {}, "new_m": {}, "new_v": {}}
    kinds = ("grad", "delta", "new_m", "new_v")
    for names, parts in zip(LAYER_BIG, (parts0[:-1],) + tuple(hosted_parts)):
        for n, p in zip(names, parts):
            res = _sum_adamw(p, a[n][0], a["m_" + n][0], a["v_" + n][0], name="adamw_" + n)
            for kind, r in zip(kinds, res):
                out[kind][n] = r[None]
    upd_small = _sum_adamw(parts0[-1], small_flat, _pack([a["m_" + n] for n in small_names]),
                           _pack([a["v_" + n] for n in small_names]), name="adamw_small_sharded")
    for kind, r in zip(kinds, upd_small):
        out[kind].update(zip(small_names, _unpack(r, small_shapes)))
    g_rep_local = _pack([g_small[n] if n in ("ln_g", "ln_b") else g_small[n][None] for n in REPLICATED])
    (g_rep_all,) = _run_comm(_Gather([g_rep_local]), name="gather_replicated_gradients")
    upd_rep = _sum_adamw(g_rep_all, _pack([a[n] for n in REPLICATED]), _pack([a["m_" + n] for n in REPLICATED]),
                         _pack([a["v_" + n] for n in REPLICATED]), name="adamw_replicated")
    for kind, r in zip(kinds, upd_rep):
        out[kind].update(zip(REPLICATED, _unpack(r, rep_shapes)))
    loss_total = lax.psum(loss[0, 0], ("x", "y", "c"))
    return (loss_total, grad_x[None],
            *[out[kind][n] for kind in ("grad", "delta", "new_m", "new_v") for n in WEIGHT_ORDER])


def kernel(x, positions, ln_g, ln_b, ssd_w_in, ssd_conv_w, ssd_conv_b, ssd_dt_bias, ssd_a_log, ssd_d, ssd_norm_w, ssd_w_out, fox_w_in, fox_f_bias, fox_w_out, mla_w_in, mla_q_norm, mla_kv_norm, mla_w_q_up, mla_w_kv_up, mla_w_out, s5_w_in, s5_lambda_re, s5_lambda_im, s5_log_step, s5_b_re, s5_b_im, s5_c_re, s5_c_im, s5_d, s5_w_glu, s5_b_glu, s5_w_out, loss_target, m_ln_g, m_ln_b, m_ssd_w_in, m_ssd_conv_w, m_ssd_conv_b, m_ssd_dt_bias, m_ssd_a_log, m_ssd_d, m_ssd_norm_w, m_ssd_w_out, m_fox_w_in, m_fox_f_bias, m_fox_w_out, m_mla_w_in, m_mla_q_norm, m_mla_kv_norm, m_mla_w_q_up, m_mla_w_kv_up, m_mla_w_out, m_s5_w_in, m_s5_lambda_re, m_s5_lambda_im, m_s5_log_step, m_s5_b_re, m_s5_b_im, m_s5_c_re, m_s5_c_im, m_s5_d, m_s5_w_glu, m_s5_b_glu, m_s5_w_out, v_ln_g, v_ln_b, v_ssd_w_in, v_ssd_conv_w, v_ssd_conv_b, v_ssd_dt_bias, v_ssd_a_log, v_ssd_d, v_ssd_norm_w, v_ssd_w_out, v_fox_w_in, v_fox_f_bias, v_fox_w_out, v_mla_w_in, v_mla_q_norm, v_mla_kv_norm, v_mla_w_q_up, v_mla_w_kv_up, v_mla_w_out, v_s5_w_in, v_s5_lambda_re, v_s5_lambda_im, v_s5_log_step, v_s5_b_re, v_s5_b_im, v_s5_c_re, v_s5_c_im, v_s5_d, v_s5_w_glu, v_s5_b_glu, v_s5_w_out):
    return _train_step(dict(locals()))
```

```python
import functools
import math

import jax
import jax.numpy as jnp
from jax import lax
from jax.experimental import pallas as pl
from jax.experimental.pallas import tpu as pltpu

f32 = jnp.float32
bf16 = jnp.bfloat16

N_DEV = 8
DEPTH = 4
ALPHA = (2.0 * DEPTH) ** 0.25
LN_EPS = 1e-5
RMS_EPS = 1e-6
LANE = 128

SSD_STATE = 128
SSD_HEADDIM = 64
SSD_CHUNK = 128
FOX_HEAD_DIM = 128
MLA_NOPE = 128
MLA_ROPE = 64
MLA_V = 128
ROPE_BASE = 10000.0
S5_GROUP = 16
S5_BLOCK_GROUPS = 8
S5_CHUNK = 128

ADAM_LR = 0.001
ADAM_B1 = 0.9
ADAM_B2 = 0.999
ADAM_EPS = 1e-08
ADAM_WD = 0.01
ADAM_STEP = 10

VMEM_BLOCK_BUDGET = 20 * 1024 * 1024
MM_FULL_K = 2048
HIGHEST = lax.Precision.HIGHEST


def _pick(n, cands):
    for c in cands:
        if n % c == 0:
            return c
    return n


def _dg(a, b, ca, cb):
    return lax.dot_general(a.astype(bf16), b.astype(bf16), (((ca,), (cb,)), ((), ())),
                           preferred_element_type=f32)


@functools.partial(jax.custom_vjp, nondiff_argnums=(2, 3))
def bdot(a, b, ca, cb):
    return _dg(a, b, ca, cb)


def _bdot_fwd(a, b, ca, cb):
    return _dg(a, b, ca, cb), (a, b)


def _bdot_bwd(ca, cb, res, g):
    a, b = res
    nb = 1 - cb
    ma = 1 - ca
    da = _dg(g, b, 1, nb) if ca == 1 else _dg(b, g, nb, 1)
    db = _dg(a, g, ma, 0) if cb == 0 else _dg(g, a, 0, ma)
    return da, db


bdot.defvjp(_bdot_fwd, _bdot_bwd)


def _silu(x):
    return x * jax.nn.sigmoid(x)


def _softplus(x):
    return jnp.maximum(x, 0.0) + jnp.log(1.0 + jnp.exp(-jnp.abs(x)))


def _mm(a, b, *, ta=False, tb=False, add=None, out_dtype=f32, name):
    m, k = (a.shape[1], a.shape[0]) if ta else a.shape
    n = b.shape[0] if tb else b.shape[1]
    kb = b.shape[1] if tb else b.shape[0]
    assert k == kb, (name, a.shape, b.shape)
    if k <= MM_FULL_K:
        tm = _pick(m, (512, 256, 128))
        tn = _pick(n, (1024, 512, 384, 256, 128))
        tk = k
    else:
        tm = _pick(m, (1024, 512, 256, 128))
        tn = _pick(n, (1024, 512, 384, 256, 128))
        tk = _pick(k, (512, 256, 128))
    nk = k // tk
    has_add = add is not None

    def body(*refs):
        if has_add:
            a_ref, b_ref, add_ref, o_ref, acc = refs
        else:
            a_ref, b_ref, o_ref, acc = refs
        kk = pl.program_id(2)

        def prod():
            return _dg(a_ref[...], b_ref[...], 0 if ta else 1, 1 if tb else 0)

        def finish(r):
            o_ref[...] = (r + add_ref[...] if has_add else r).astype(o_ref.dtype)

        if nk == 1:
            finish(prod())
            return

        @pl.when(kk == 0)
        def _():
            acc[...] = prod()

        @pl.when((kk > 0) & (kk < nk - 1))
        def _():
            acc[...] += prod()

        @pl.when(kk == nk - 1)
        def _():
            finish(acc[...] + prod())

    a_spec = (pl.BlockSpec((tk, tm), lambda i, j, kk: (kk, i)) if ta
              else pl.BlockSpec((tm, tk), lambda i, j, kk: (i, kk)))
    b_spec = (pl.BlockSpec((tn, tk), lambda i, j, kk: (j, kk)) if tb
              else pl.BlockSpec((tk, tn), lambda i, j, kk: (kk, j)))
    o_spec = pl.BlockSpec((tm, tn), lambda i, j, kk: (i, j))
    in_specs = [a_spec, b_spec] + ([o_spec] if has_add else [])
    args = (a, b) + ((add,) if has_add else ())
    return pl.pallas_call(
        body, name=name, grid=(m // tm, n // tn, nk),
        in_specs=in_specs, out_specs=o_spec,
        out_shape=jax.ShapeDtypeStruct((m, n), out_dtype),
        scratch_shapes=[pltpu.VMEM((tm, tn), f32)],
        compiler_params=pltpu.CompilerParams(dimension_semantics=("parallel", "parallel", "arbitrary")),
    )(*args)


def _row_operand(op):
    if isinstance(op, tuple):
        arr, w, cb = op
    else:
        arr, w, cb = op, op.shape[1], 0
    return arr, w, cb


def _rw_tm(t, widths):
    per_row = 2 * 4 * sum(widths)
    tm = 512
    while tm > 8 and (tm * per_row > VMEM_BLOCK_BUDGET or t % tm):
        tm //= 2
    return tm


def _rw_specs(ops, tm):
    specs, arrs = [], []
    for op in ops:
        arr, w, cb = _row_operand(op)
        specs.append(pl.BlockSpec((tm, w), functools.partial(lambda i, cb: (i, cb), cb=cb)))
        arrs.append(arr)
    return specs, arrs


def _param_spec(p):
    nd = p.ndim
    return pl.BlockSpec(p.shape, lambda i: (0,) * nd)


def _rw_fwd(fn, rows, params, out_widths, *, name):
    t = _row_operand(rows[0])[0].shape[0]
    tm = _rw_tm(t, [_row_operand(r)[1] for r in rows] + list(out_widths))
    nr, npar = len(rows), len(params)

    def body(*refs):
        vals = [r[...] for r in refs[:nr + npar]]
        outs = fn(*vals)
        for o_ref, v in zip(refs[nr + npar:], outs):
            o_ref[...] = v

    rspecs, rarrs = _rw_specs(rows, tm)
    return pl.pallas_call(
        body, name=name, grid=(t // tm,),
        in_specs=rspecs + [_param_spec(p) for p in params],
        out_specs=[pl.BlockSpec((tm, w), lambda i: (i, 0)) for w in out_widths],
        out_shape=[jax.ShapeDtypeStruct((t, w), f32) for w in out_widths],
        compiler_params=pltpu.CompilerParams(dimension_semantics=("parallel",)),
    )(*rarrs, *params)


def _rw_bwd(fn, rows, params, cots, *, name, n_const=0):
    t = _row_operand(rows[0])[0].shape[0]
    nr, npar, nc = len(rows), len(params), len(cots)
    nd = nr - n_const
    widths = [_row_operand(r)[1] for r in rows]
    tm = _rw_tm(t, widths + widths[:nd] + [c.shape[1] for c in cots])

    def body(*refs):
        rv = [r[...] for r in refs[:nr]]
        pv = [r[...] for r in refs[nr:nr + npar]]
        cv = [r[...] for r in refs[nr + npar:nr + npar + nc]]
        outs = refs[nr + npar + nc:]
        consts = rv[nd:]

        def f(*diff):
            return fn(*diff[:nd], *consts, *diff[nd:])

        _, vjp = jax.vjp(f, *rv[:nd], *pv)
        g = vjp(tuple(cv))
        for o_ref, v in zip(outs[:nd], g[:nd]):
            o_ref[...] = v
        if npar:
            @pl.when(pl.program_id(0) == 0)
            def _():
                for o_ref in outs[nd:]:
                    o_ref[...] = jnp.zeros_like(o_ref)

            for o_ref, v in zip(outs[nd:], g[nd:]):
                o_ref[...] += v

    rspecs, rarrs = _rw_specs(rows, tm)
    cspecs = [pl.BlockSpec((tm, c.shape[1]), lambda i: (i, 0)) for c in cots]
    return pl.pallas_call(
        body, name=name, grid=(t // tm,),
        in_specs=rspecs + [_param_spec(p) for p in params] + cspecs,
        out_specs=[pl.BlockSpec((tm, w), lambda i: (i, 0)) for w in widths[:nd]]
        + [_param_spec(p) for p in params],
        out_shape=[jax.ShapeDtypeStruct((t, w), f32) for w in widths[:nd]]
        + [jax.ShapeDtypeStruct(p.shape, f32) for p in params],
        compiler_params=pltpu.CompilerParams(dimension_semantics=("arbitrary",)),
    )(*rarrs, *params, *cots)


def _ln_fn(h, o, g, b):
    x = ALPHA * h + o
    mu = jnp.mean(x, -1, keepdims=True)
    var = jnp.mean(jnp.square(x - mu), -1, keepdims=True)
    return ((x - mu) * lax.rsqrt(var + LN_EPS) * g + b,)


def _gate_fn(o, z):
    return (o * _silu(z),)


def _make_ssd_post_fn(groups):
    def fn(y, z, nw):
        yz = y * _silu(z)
        gw = yz.shape[1] // groups
        outs = []
        for g in range(groups):
            blk = yz[:, g * gw:(g + 1) * gw]
            outs.append(blk * lax.rsqrt(jnp.mean(jnp.square(blk), -1, keepdims=True) + RMS_EPS))
        return (jnp.concatenate(outs, axis=1) * nw,)
    return fn


def _logf_fn(f, b):
    return (jax.nn.log_sigmoid(f + b),)


def _make_mla_norm_fn(rank):
    def fn(lat, qn, kvn):
        def rms(x, w):
            return x * lax.rsqrt(jnp.mean(jnp.square(x), -1, keepdims=True) + RMS_EPS) * w
        return rms(lat[:, :rank], qn), rms(lat[:, rank:], kvn)
    return fn


def _rope_block(xb, cos, sin):
    half = MLA_ROPE // 2
    x1, x2 = xb[:, :half], xb[:, half:MLA_ROPE]
    return jnp.concatenate([x1 * cos - x2 * sin, x1 * sin + x2 * cos, xb[:, MLA_ROPE:]], axis=1)


def _make_rope_fn(heads):
    def fn(x, cos, sin):
        return (jnp.concatenate([_rope_block(x[:, h * LANE:(h + 1) * LANE], cos, sin)
                                 for h in range(heads)], axis=1),)
    return fn


def _s5_act_fn(ys, u, d):
    return (jax.nn.gelu(ys + d * u),)


def _s5_glu_fn(y1, gp, z, bg):
    return (y1 * jax.nn.sigmoid(gp + bg) * _silu(z),)


def _loss_head(y, tgt):
    t, d = y.shape
    tm = _rw_tm(t, [d, d, d])

    def body(y_ref, t_ref, dy_ref, l_ref):
        @pl.when(pl.program_id(0) == 0)
        def _():
            l_ref[...] = jnp.zeros_like(l_ref)

        e = y_ref[...] - t_ref[...]
        dy_ref[...] = e * (1.0 / d)
        l_ref[...] += 0.5 * jnp.sum(jnp.mean(jnp.square(e), axis=-1, keepdims=True), axis=0, keepdims=True)

    spec = pl.BlockSpec((tm, d), lambda i: (i, 0))
    return pl.pallas_call(
        body, name="loss_head", grid=(t // tm,), in_specs=[spec, spec],
        out_specs=[spec, pl.BlockSpec((1, 1), lambda i: (0, 0))],
        out_shape=[jax.ShapeDtypeStruct((t, d), f32), jax.ShapeDtypeStruct((1, 1), f32)],
        compiler_params=pltpu.CompilerParams(dimension_semantics=("arbitrary",)),
    )(y, tgt)


def _attn_tile(t):
    return _pick(t, (512, 256, 128))


ATTN_HEADS_PER_STEP = 4
ATTN_BWD_HEADS_PER_STEP = 2


def _logits_t(k1, q1, k2, q2, ck_col, scale, key0, query0):
    s = _dg(k1, q1, 1, 1)
    if q2 is not None:
        s = s + _dg(k2, q2, 1, 1)
    s = s * scale
    if ck_col is not None:
        s = s - ck_col
    key = key0 + lax.broadcasted_iota(jnp.int32, s.shape, 0)
    query = query0 + lax.broadcasted_iota(jnp.int32, s.shape, 1)
    return jnp.where(key <= query, s, -jnp.inf)


def _lane_bcast(v):
    return jnp.broadcast_to(v[:, :, None], v.shape + (LANE,))


def _attn_fwd(q1, q1o, k1, k1o, v, vo, q2, k2, ck, *, heads, scale, name, side=None):
    t = q1.shape[0]
    tq = tk = _attn_tile(t)
    nq = t // tq
    has2, hasb = q2 is not None, ck is not None
    hp = ATTN_HEADS_PER_STEP
    wd = hp * LANE
    assert heads % hp == 0 and q1o % hp == 0 and k1o % hp == 0 and vo % hp == 0

    def body(*refs):
        it = iter(refs)
        q1r, k1r, vr = next(it), next(it), next(it)
        q2r, k2r = (next(it), next(it)) if has2 else (None, None)
        ckr = next(it) if hasb else None
        o_r, lse_r, m_s, l_s, acc = next(it), next(it), next(it), next(it), next(it)
        i, j = pl.program_id(1), pl.program_id(2)

        @pl.when(j == 0)
        def _():
            m_s[...] = jnp.full_like(m_s, -jnp.inf)
            l_s[...] = jnp.zeros_like(l_s)
            acc[...] = jnp.zeros_like(acc)

        @pl.when(j <= i)
        def _():
            k2v = k2r[...].astype(bf16) if has2 else None
            for e in range(hp):
                ln = slice(e * LANE, (e + 1) * LANE)
                q1v, k1v, vv = (r[:, ln].astype(bf16) for r in (q1r, k1r, vr))
                q2v = q2r[:, ln].astype(bf16) if has2 else None
                s = _logits_t(k1v, q1v, k2v, q2v, ckr[e][:, 0:1] if hasb else None, scale, j * tk, i * tq)
                m_prev = m_s[e]
                m_new = jnp.maximum(m_prev, jnp.max(s, axis=0, keepdims=True))
                p = jnp.exp(s - m_new)
                a = jnp.exp(m_prev - m_new)
                l_s[e] = a * l_s[e] + jnp.sum(p, axis=0, keepdims=True)
                acc[e] = a * acc[e] + _dg(vv, p, 0, 0)
                m_s[e] = m_new

        @pl.when(j == i)
        def _():
            for e in range(hp):
                o_r[:, e * LANE:(e + 1) * LANE] = (acc[e] / l_s[e]).T
                lse_r[e, 0] = m_s[e] + jnp.log(l_s[e])

    def qmap(off):
        return lambda h, i, j: (i, off // hp + h)

    def kmap(off):
        return lambda h, i, j: (jnp.minimum(j, i), off // hp + h)

    in_specs = [pl.BlockSpec((tq, wd), qmap(q1o)), pl.BlockSpec((tk, wd), kmap(k1o)),
                pl.BlockSpec((tk, wd), kmap(vo))]
    args = [q1, k1, v]
    if has2:
        in_specs += [pl.BlockSpec((tq, wd), qmap(0)),
                     pl.BlockSpec((tk, LANE), lambda h, i, j: (jnp.minimum(j, i), 0))]
        args += [q2, k2]
    if hasb:
        in_specs += [pl.BlockSpec((hp, tk, LANE), lambda h, i, j: (h, jnp.minimum(j, i), 0))]
        args += [ck]
    return _hosted_call(
        body, side, name=name, grid=(heads // hp, nq, nq), in_specs=in_specs,
        out_specs=[pl.BlockSpec((tq, wd), lambda h, i, j: (i, h)),
                   pl.BlockSpec((hp, 1, 1, tq), lambda h, i, j: (h, i, 0, 0))],
        out_shape=[jax.ShapeDtypeStruct((t, heads * LANE), f32),
                   jax.ShapeDtypeStruct((heads, nq, 1, tq), f32)],
        scratch_shapes=[pltpu.VMEM((hp, 1, tq), f32), pltpu.VMEM((hp, 1, tq), f32),
                        pltpu.VMEM((hp, LANE, tq), f32)],
        args=args)


def _row_sums_as_row(x, first_lane_only=False):
    sel = jnp.ones((8, x.shape[1]), f32)
    if first_lane_only:
        sel = (lax.broadcasted_iota(jnp.int32, sel.shape, 1) == 0).astype(f32)
    return lax.dot_general(sel, x, (((1,), (1,)), ((), ())), precision=HIGHEST, preferred_element_type=f32)[0:1]


def _attn_delta(do, o, *, heads, name):
    t = do.shape[0]
    tq = _attn_tile(t)
    hp = ATTN_HEADS_PER_STEP
    assert heads % hp == 0

    def body(do_ref, o_ref, d_ref):
        for e in range(hp):
            ln = slice(e * LANE, (e + 1) * LANE)
            d_ref[e, 0] = _row_sums_as_row(do_ref[:, ln] * o_ref[:, ln])

    spec = pl.BlockSpec((tq, hp * LANE), lambda i, h: (i, h))
    return pl.pallas_call(
        body, name=name, grid=(t // tq, heads // hp), in_specs=[spec, spec],
        out_specs=pl.BlockSpec((hp, 1, 1, tq), lambda i, h: (h, i, 0, 0)),
        out_shape=jax.ShapeDtypeStruct((heads, t // tq, 1, tq), f32),
        compiler_params=pltpu.CompilerParams(dimension_semantics=("parallel", "parallel")),
    )(do, o)


def _attn_bwd(q1, q1o, k1, k1o, v, vo, q2, k2, ck, do, lse, delta, *, heads, scale, name, side=None):
    t = q1.shape[0]
    tq = tk = _attn_tile(t)
    nq = t // tq
    has2, hasb = q2 is not None, ck is not None
    hp = ATTN_BWD_HEADS_PER_STEP
    wd = hp * LANE
    assert heads % hp == 0 and q1o % hp == 0 and k1o % hp == 0 and vo % hp == 0

    def body(*refs):
        it = iter(refs)
        q1r, k1r, vr = next(it), next(it), next(it)
        q2r, k2r = (next(it), next(it)) if has2 else (None, None)
        ckr = next(it) if hasb else None
        dor, lser, dlr = next(it), next(it), next(it)
        dq1r, dk1r, dvr = next(it), next(it), next(it)
        dq2r, dk2r = (next(it), next(it)) if has2 else (None, None)
        drowr, dckr = (next(it), next(it)) if hasb else (None, None)
        ak, av = next(it), next(it)
        ac = next(it) if hasb else None
        h, j, i = pl.program_id(0), pl.program_id(1), pl.program_id(2)

        @pl.when((j == 0) & (i == 0))
        def _():
            dq1r[...] = jnp.zeros_like(dq1r)
            if has2:
                dq2r[...] = jnp.zeros_like(dq2r)
            if hasb:
                drowr[...] = jnp.zeros_like(drowr)

        if has2:
            @pl.when((h == 0) & (j == 0) & (i == 0))
            def _():
                dk2r[...] = jnp.zeros_like(dk2r)

        @pl.when(i == j)
        def _():
            ak[...] = jnp.zeros_like(ak)
            av[...] = jnp.zeros_like(av)
            if hasb:
                ac[...] = jnp.zeros_like(ac)

        @pl.when(i >= j)
        def _():
            rows = pl.ds(pl.multiple_of(i * tq, tq), tq)
            cols = pl.ds(pl.multiple_of(j * tk, tk), tk)
            k2v = k2r[...].astype(bf16) if has2 else None
            for e in range(hp):
                ln = slice(e * LANE, (e + 1) * LANE)
                q1v, k1v, vv, dov = (r[:, ln].astype(bf16) for r in (q1r, k1r, vr, dor))
                q2v = q2r[:, ln].astype(bf16) if has2 else None
                s = _logits_t(k1v, q1v, k2v, q2v, ckr[e][:, 0:1] if hasb else None, scale, j * tk, i * tq)
                p = jnp.exp(s - lser[e, 0])
                dp = _dg(vv, dov, 1, 1)
                dsr = p * (dp - dlr[e, 0])
                ds = (dsr * scale).astype(bf16)
                av[:, ln] += _dg(p, dov, 1, 0)
                ak[:, ln] += _dg(ds, q1v, 1, 0)
                dq1r[rows, ln] += _dg(ds, k1v, 0, 0)
                if has2:
                    dq2r[rows, ln] += _dg(ds, k2v, 0, 0)
                    dk2r[cols, :] += _dg(ds, q2v, 1, 0)
                if hasb:
                    drowr[e, pl.ds(i, 1), :] += jnp.sum(dsr, axis=0, keepdims=True)
                    ac[e] -= jnp.broadcast_to(jnp.sum(dsr, axis=1, keepdims=True), (tk, LANE))

        @pl.when(i == nq - 1)
        def _():
            dk1r[...] = ak[...]
            dvr[...] = av[...]
            if hasb:
                for e in range(hp):
                    dckr[e] = _row_sums_as_row(ac[e], first_lane_only=True)

    def qmap(off):
        return lambda h, j, i: (jnp.maximum(i, j), off // hp + h)

    def kmap(off):
        return lambda h, j, i: (j, off // hp + h)

    in_specs = [pl.BlockSpec((tq, wd), qmap(q1o)), pl.BlockSpec((tk, wd), kmap(k1o)),
                pl.BlockSpec((tk, wd), kmap(vo))]
    args = [q1, k1, v]
    if has2:
        in_specs += [pl.BlockSpec((tq, wd), qmap(0)), pl.BlockSpec((tk, LANE), lambda h, j, i: (j, 0))]
        args += [q2, k2]
    if hasb:
        in_specs += [pl.BlockSpec((hp, tk, LANE), lambda h, j, i: (h, j, 0))]
        args += [ck]
    row3 = pl.BlockSpec((hp, 1, 1, tq), lambda h, j, i: (h, jnp.maximum(i, j), 0, 0))
    in_specs += [pl.BlockSpec((tq, wd), qmap(0)), row3, row3]
    args += [do, lse, delta]
    hd = jax.ShapeDtypeStruct((t, heads * LANE), f32)
    head_cols = pl.BlockSpec((t, wd), lambda h, j, i: (0, h))
    kv_blk = pl.BlockSpec((tk, wd), kmap(0))
    out_specs, out_shape = [head_cols, kv_blk, kv_blk], [hd, hd, hd]
    scratch = [pltpu.VMEM((tk, wd), f32)] * 2
    if has2:
        out_specs += [head_cols, pl.BlockSpec((t, LANE), lambda h, j, i: (0, 0))]
        out_shape += [hd, jax.ShapeDtypeStruct((t, LANE), f32)]
    if hasb:
        out_specs += [pl.BlockSpec((hp, nq, tq), lambda h, j, i: (h, 0, 0)),
                      pl.BlockSpec((hp, 1, tk), lambda h, j, i: (h, 0, j))]
        out_shape += [jax.ShapeDtypeStruct((heads, nq, tq), f32), jax.ShapeDtypeStruct((heads, 1, t), f32)]
        scratch.append(pltpu.VMEM((hp, tk, LANE), f32))
    return _hosted_call(body, side, name=name, grid=(heads // hp, nq, nq), in_specs=in_specs, out_specs=out_specs,
                        out_shape=out_shape, scratch_shapes=scratch, args=args)


def _cumsum_rows(x, *, reverse, name):
    t, w = x.shape
    blk = 128
    nb = t // blk

    def body(x_ref, o_ref):
        r = lax.broadcasted_iota(jnp.int32, (blk, blk), 0)
        c = lax.broadcasted_iota(jnp.int32, (blk, blk), 1)
        tri = ((r <= c) if reverse else (r >= c)).astype(f32)
        carry = jnp.zeros((1, w), f32)
        for b in (reversed(range(nb)) if reverse else range(nb)):
            seg = x_ref[b * blk:(b + 1) * blk, :]
            cs = jnp.dot(tri, seg, precision=HIGHEST, preferred_element_type=f32) + carry
            o_ref[b * blk:(b + 1) * blk, :] = cs
            carry = cs[0:1, :] if reverse else cs[blk - 1:blk, :]

    return pl.pallas_call(body, name=name, out_shape=jax.ShapeDtypeStruct((t, w), f32))(x)


CONV_PAD = 8


def _conv_fn(x, w, b):
    t, taps = x.shape[0], w.shape[0]
    xx = jnp.concatenate([jnp.zeros((CONV_PAD, x.shape[1]), f32), x], axis=0)
    y = b
    for k in range(taps):
        off = CONV_PAD - (taps - 1) + k
        y = y + w[k:k + 1, :] * xx[off:off + t, :]
    return _silu(y)


def _conv_fwd(x, w, b, *, name):
    t, c = x.shape
    tc = LANE
    taps = w.shape[0]

    def body(x_ref, w_ref, b_ref, o_ref):
        o_ref[...] = _conv_fn(x_ref[...], w_ref[...], b_ref[...])

    xs = pl.BlockSpec((t, tc), lambda i: (0, i))
    return pl.pallas_call(
        body, name=name, grid=(c // tc,),
        in_specs=[xs, pl.BlockSpec((taps, tc), lambda i: (0, i)), pl.BlockSpec((1, tc), lambda i: (0, i))],
        out_specs=xs, out_shape=jax.ShapeDtypeStruct((t, c), f32),
        compiler_params=pltpu.CompilerParams(dimension_semantics=("parallel",)),
    )(x, w, b)


def _conv_bwd(x, w, b, ds, *, name):
    t, c = x.shape
    tc = LANE
    taps = w.shape[0]

    def body(x_ref, w_ref, b_ref, ds_ref, dx_ref, dw_ref, db_ref):
        _, vjp = jax.vjp(_conv_fn, x_ref[...], w_ref[...], b_ref[...])
        dx, dw, db = vjp(ds_ref[...])
        dx_ref[...] = dx
        dw_ref[...] = dw
        db_ref[...] = db

    xs = pl.BlockSpec((t, tc), lambda i: (0, i))
    ws = pl.BlockSpec((taps, tc), lambda i: (0, i))
    bs = pl.BlockSpec((1, tc), lambda i: (0, i))
    return pl.pallas_call(
        body, name=name, grid=(c // tc,), in_specs=[xs, ws, bs, xs], out_specs=[xs, ws, bs],
        out_shape=[jax.ShapeDtypeStruct((t, c), f32), jax.ShapeDtypeStruct((taps, c), f32),
                   jax.ShapeDtypeStruct((1, c), f32)],
        compiler_params=pltpu.CompilerParams(dimension_semantics=("parallel",)),
    )(x, w, b, ds)


def _ssd_chunk(x, bm, cm, dtr, st, dtb, alog, dsk, *, e_heads):
    ln, p = x.shape[0], SSD_HEADDIM
    dt = _softplus(dtr + dtb)
    da = dt * (-jnp.exp(alog))
    r = lax.broadcasted_iota(jnp.int32, (ln, ln), 0)
    c = lax.broadcasted_iota(jnp.int32, (ln, ln), 1)
    lower = r >= c
    acs = jnp.dot(lower.astype(f32), da, precision=HIGHEST, preferred_element_type=f32)
    acs_t = acs.T

    def per_head(v):
        return jnp.concatenate([jnp.broadcast_to(v[:, e:e + 1], (v.shape[0], p)) for e in range(e_heads)], axis=1)

    dt_x, acs_x = per_head(dt), per_head(acs)
    last_x = acs_x[ln - 1:ln, :]
    xdt = x * dt_x
    cb = bdot(cm, bm, 1, 1)
    y_off = bdot(cm, st, 1, 0) * jnp.exp(acs_x)
    st_new = st * jnp.exp(last_x) + bdot(bm, xdt * jnp.exp(last_x - acs_x), 0, 0)
    ys = []
    for e in range(e_heads):
        seg = acs[:, e:e + 1] - acs_t[e:e + 1, :]
        dec = jnp.exp(jnp.where(lower, seg, -jnp.inf))
        ys.append(bdot(cb * dec, xdt[:, e * p:(e + 1) * p], 1, 0))
    y = jnp.concatenate(ys, axis=1) + y_off + per_head(dsk) * x
    return y, st_new


def _ssd_specs(t, d_inner, groups):
    ln, n = SSD_CHUNK, SSD_STATE
    gw = d_inner // groups
    nc = t // ln
    return ln, n, gw, nc


def _ssd_fwd(xbc, dtp, dtb, alog, dsk, *, d_inner, groups, name, side=None):
    t = xbc.shape[0]
    ln, n, gw, nc = _ssd_specs(t, d_inner, groups)
    e_heads = gw // SSD_HEADDIM
    boff = d_inner // n

    def body(x_ref, b_ref, c_ref, dt_ref, dtb_ref, alog_ref, dsk_ref, y_ref, save_ref, st_scr):
        c, g = pl.program_id(0), pl.program_id(1)

        @pl.when(c == 0)
        def _():
            st_scr[g] = jnp.zeros((n, gw), f32)

        st = st_scr[g]
        save_ref[0] = st
        y, st_new = _ssd_chunk(x_ref[...], b_ref[...], c_ref[...], dt_ref[...], st,
                               dtb_ref[0], alog_ref[0], dsk_ref[0], e_heads=e_heads)
        y_ref[...] = y
        st_scr[g] = st_new

    par = pl.BlockSpec((1, 1, LANE), lambda c, g: (g, 0, 0))
    return _hosted_call(
        body, side, name=name, grid=(nc, groups),
        in_specs=[pl.BlockSpec((ln, gw), lambda c, g: (c, g)),
                  pl.BlockSpec((ln, n), lambda c, g: (c, boff + g)),
                  pl.BlockSpec((ln, n), lambda c, g: (c, boff + groups + g)),
                  pl.BlockSpec((ln, LANE), lambda c, g: (c, g)), par, par, par],
        out_specs=[pl.BlockSpec((ln, gw), lambda c, g: (c, g)),
                   pl.BlockSpec((1, n, gw), lambda c, g: (c * groups + g, 0, 0))],
        out_shape=[jax.ShapeDtypeStruct((t, d_inner), f32),
                   jax.ShapeDtypeStruct((nc * groups, n, gw), f32)],
        scratch_shapes=[pltpu.VMEM((groups, n, gw), f32)],
        args=(xbc, xbc, xbc, dtp, dtb, alog, dsk))


def _ssd_bwd(xbc, dtp, dtb, alog, dsk, saved, dy, *, d_inner, groups, name, side=None):
    t = xbc.shape[0]
    ln, n, gw, nc = _ssd_specs(t, d_inner, groups)
    e_heads = gw // SSD_HEADDIM
    boff = d_inner // n

    def body(x_ref, b_ref, c_ref, dt_ref, dtb_ref, alog_ref, dsk_ref, save_ref, dy_ref,
             dx_ref, db_ref, dc_ref, ddt_ref, dpar_ref, dst_scr):
        c, g = pl.program_id(0), pl.program_id(1)

        @pl.when(c == 0)
        def _():
            dst_scr[g] = jnp.zeros((n, gw), f32)

        @pl.when((c == 0) & (g == 0))
        def _():
            dpar_ref[...] = jnp.zeros_like(dpar_ref)

        fn = functools.partial(_ssd_chunk, e_heads=e_heads)
        _, vjp = jax.vjp(fn, x_ref[...], b_ref[...], c_ref[...], dt_ref[...], save_ref[0],
                         dtb_ref[0], alog_ref[0], dsk_ref[0])
        gx, gb, gc, gdt, gst, gdtb, galog, gdsk = vjp((dy_ref[...], dst_scr[g]))
        dx_ref[...] = gx
        db_ref[...] = gb
        dc_ref[...] = gc
        ddt_ref[...] = gdt
        dst_scr[g] = gst
        dpar_ref[g, 0:1, :] += gdtb
        dpar_ref[g, 1:2, :] += galog
        dpar_ref[g, 2:3, :] += gdsk

    def rc(c):
        return nc - 1 - c

    par = pl.BlockSpec((1, 1, LANE), lambda c, g: (g, 0, 0))
    xs = pl.BlockSpec((ln, gw), lambda c, g: (rc(c), g))
    ns = pl.BlockSpec((ln, n), lambda c, g: (rc(c), g))
    return _hosted_call(
        body, side, name=name, grid=(nc, groups),
        in_specs=[xs,
                  pl.BlockSpec((ln, n), lambda c, g: (rc(c), boff + g)),
                  pl.BlockSpec((ln, n), lambda c, g: (rc(c), boff + groups + g)),
                  pl.BlockSpec((ln, LANE), lambda c, g: (rc(c), g)), par, par, par,
                  pl.BlockSpec((1, n, gw), lambda c, g: (rc(c) * groups + g, 0, 0)), xs],
        out_specs=[xs, ns, ns, pl.BlockSpec((ln, LANE), lambda c, g: (rc(c), g)),
                   pl.BlockSpec((groups, 8, LANE), lambda c, g: (0, 0, 0))],
        out_shape=[jax.ShapeDtypeStruct((t, d_inner), f32),
                   jax.ShapeDtypeStruct((t, groups * n), f32),
                   jax.ShapeDtypeStruct((t, groups * n), f32),
                   jax.ShapeDtypeStruct((t, groups * LANE), f32),
                   jax.ShapeDtypeStruct((groups, 8, LANE), f32)],
        scratch_shapes=[pltpu.VMEM((groups, n, gw), f32)],
        args=(xbc, xbc, xbc, dtp, dtb, alog, dsk, saved, dy))


S5_TOP = 8


def _cmul(ar, ai, br, bi):
    return ar * br - ai * bi, ar * bi + ai * br


def _s5_scan(scr, base, ln, pw_ref, sp, *, reverse):
    s, k = 1, 0
    while s < ln:
        pr = pw_ref[0, k:k + 1, :sp]
        pi = pw_ref[0, k:k + 1, sp:]
        if reverse:
            pi = -pi
            src, dst = base + s, base
        else:
            src, dst = base, base + s
        ar, ai = scr[src:src + ln - s, :sp], scr[src:src + ln - s, sp:]
        mr, mi = _cmul(ar, ai, pr, pi)
        scr[dst:dst + ln - s, :sp] = scr[dst:dst + ln - s, :sp] + mr
        scr[dst:dst + ln - s, sp:] = scr[dst:dst + ln - s, sp:] + mi
        s *= 2
        k += 1


def _s5_states(u_ref, b_ref, lam_ref, pw_ref, scr, carry_r, carry_i, ln, sp):
    scr[S5_TOP:S5_TOP + ln, :] = _dg(u_ref[...], b_ref[0], 1, 0)
    mr, mi = _cmul(carry_r, carry_i, lam_ref[0][:, :sp], lam_ref[0][:, sp:])
    scr[S5_TOP:S5_TOP + 1, :sp] = scr[S5_TOP:S5_TOP + 1, :sp] + mr
    scr[S5_TOP:S5_TOP + 1, sp:] = scr[S5_TOP:S5_TOP + 1, sp:] + mi
    _s5_scan(scr, S5_TOP, ln, pw_ref, sp, reverse=False)


def _s5_dims(t, u_width):
    cbw = S5_BLOCK_GROUPS * S5_GROUP
    return S5_CHUNK, cbw, u_width // cbw, t // S5_CHUNK


def _s5_fwd(u, lamv, pw, bmat, cmat, *, name):
    t, w = u.shape
    ln, cbw, nb, nc = _s5_dims(t, w)
    sp2 = lamv.shape[2]
    sp = sp2 // 2

    def body(u_ref, lam_ref, pw_ref, b_ref, c_ref, ys_ref, xp_ref, st_ref, scr, carry):
        @pl.when(pl.program_id(1) == 0)
        def _():
            carry[...] = jnp.zeros_like(carry)

        xp_ref[0] = carry[0:1, :]
        _s5_states(u_ref, b_ref, lam_ref, pw_ref, scr, carry[0:1, :sp], carry[0:1, sp:], ln, sp)
        carry[0:1, :] = scr[S5_TOP + ln - 1:S5_TOP + ln, :]
        states = scr[S5_TOP:S5_TOP + ln, :]
        st_ref[0] = states
        ys_ref[...] = _dg(states, c_ref[0], 1, 0)

    def blk(shape):
        return pl.BlockSpec((1,) + shape, lambda j, c: (j, 0, 0))

    return pl.pallas_call(
        body, name=name, grid=(nb, nc),
        in_specs=[pl.BlockSpec((ln, cbw), lambda j, c: (c, j)), blk((1, sp2)), blk((8, sp2)),
                  blk((cbw, sp2)), blk((sp2, cbw))],
        out_specs=[pl.BlockSpec((ln, cbw), lambda j, c: (c, j)),
                   pl.BlockSpec((1, 1, sp2), lambda j, c: (j * nc + c, 0, 0)),
                   pl.BlockSpec((1, ln, sp2), lambda j, c: (j, c, 0))],
        out_shape=[jax.ShapeDtypeStruct((t, w), f32), jax.ShapeDtypeStruct((nb * nc, 1, sp2), f32),
                   jax.ShapeDtypeStruct((nb, t, sp2), f32)],
        scratch_shapes=[pltpu.VMEM((S5_TOP + ln, sp2), f32), pltpu.VMEM((8, sp2), f32)],
        compiler_params=pltpu.CompilerParams(dimension_semantics=("parallel", "arbitrary")),
    )(u, lamv, pw, bmat, cmat)


def _s5_bwd(u, dy, du_skip, xp, states, lamv, pw, bmat, cmat, *, name):
    t, w = u.shape
    ln, cbw, nb, nc = _s5_dims(t, w)
    sp2 = lamv.shape[2]
    sp = sp2 // 2

    def body(u_ref, dy_ref, dus_ref, xp_ref, st_ref, lam_ref, pw_ref, b_ref, c_ref,
             du_ref, db_ref, dc_ref, dl_ref, scr, gsc, gcarry):
        @pl.when(pl.program_id(1) == 0)
        def _():
            gcarry[...] = jnp.zeros_like(gcarry)
            db_ref[...] = jnp.zeros_like(db_ref)
            dc_ref[...] = jnp.zeros_like(dc_ref)
            dl_ref[...] = jnp.zeros_like(dl_ref)

        lr, li = lam_ref[0][:, :sp], lam_ref[0][:, sp:]
        scr[S5_TOP - 1:S5_TOP, :] = xp_ref[0]
        scr[S5_TOP:S5_TOP + ln, :] = st_ref[0]
        dy = dy_ref[...]
        gsc[0:ln, :] = _dg(dy, c_ref[0], 1, 1)
        mr, mi = _cmul(gcarry[0:1, :sp], gcarry[0:1, sp:], lr, -li)
        gsc[ln - 1:ln, :sp] = gsc[ln - 1:ln, :sp] + mr
        gsc[ln - 1:ln, sp:] = gsc[ln - 1:ln, sp:] + mi
        _s5_scan(gsc, 0, ln, pw_ref, sp, reverse=True)
        gcarry[0:1, :] = gsc[0:1, :]
        g = gsc[0:ln, :]
        du_ref[...] = dus_ref[...] + _dg(g, b_ref[0], 1, 1)
        db_ref[0] += _dg(u_ref[...], g, 0, 0)
        dc_ref[0] += _dg(scr[S5_TOP:S5_TOP + ln, :], dy, 0, 0)
        pr, pi = scr[S5_TOP - 1:S5_TOP - 1 + ln, :sp], scr[S5_TOP - 1:S5_TOP - 1 + ln, sp:]
        gr, gi = g[:, :sp], g[:, sp:]
        dl_ref[0, 0:1, :sp] += jnp.sum(pr * gr + pi * gi, axis=0, keepdims=True)
        dl_ref[0, 0:1, sp:] += jnp.sum(pr * gi - pi * gr, axis=0, keepdims=True)

    def rc(c):
        return nc - 1 - c

    def blk(shape):
        return pl.BlockSpec((1,) + shape, lambda j, c: (j, 0, 0))

    row = pl.BlockSpec((ln, cbw), lambda j, c: (rc(c), j))
    return pl.pallas_call(
        body, name=name, grid=(nb, nc),
        in_specs=[row, row, row, pl.BlockSpec((1, 1, sp2), lambda j, c: (j * nc + rc(c), 0, 0)),
                  pl.BlockSpec((1, ln, sp2), lambda j, c: (j, rc(c), 0)),
                  blk((1, sp2)), blk((8, sp2)), blk((cbw, sp2)), blk((sp2, cbw))],
        out_specs=[row, blk((cbw, sp2)), blk((sp2, cbw)), blk((8, sp2))],
        out_shape=[jax.ShapeDtypeStruct((t, w), f32), jax.ShapeDtypeStruct((nb, cbw, sp2), f32),
                   jax.ShapeDtypeStruct((nb, sp2, cbw), f32), jax.ShapeDtypeStruct((nb, 8, sp2), f32)],
        scratch_shapes=[pltpu.VMEM((S5_TOP + ln, sp2), f32), pltpu.VMEM((ln, sp2), f32),
                        pltpu.VMEM((8, sp2), f32)],
        compiler_params=pltpu.CompilerParams(dimension_semantics=("parallel", "arbitrary")),
    )(u, dy, du_skip, xp, states, lamv, pw, bmat, cmat)


def _s5_discretize(lre, lim, log_step, bre, bim, cre, cim):
    g, p = lre.shape
    gb = S5_BLOCK_GROUPS
    nb = g // gb
    lam = lax.complex(lre, lim)
    lam_bar = jnp.exp(lam * jnp.exp(log_step)[:, None])
    b_bar = ((lam_bar - 1.0) / lam)[..., None] * lax.complex(bre, bim)
    lb = lam_bar.reshape(nb, 1, gb * p)
    lamv = jnp.concatenate([jnp.real(lb), jnp.imag(lb)], axis=-1)
    eye = jnp.eye(gb, dtype=f32)
    bb = b_bar.reshape(nb, gb, p, S5_GROUP)

    def bdiag(v):
        return jnp.einsum('jgpi,gh->jgihp', v, eye).reshape(nb, gb * S5_GROUP, gb * p)

    bmat = jnp.concatenate([bdiag(jnp.real(bb)), bdiag(jnp.imag(bb))], axis=-1)

    def cdiag(v):
        return jnp.einsum('jgip,gh->jhpgi', v, eye).reshape(nb, gb * p, gb * S5_GROUP)

    cmat = jnp.concatenate([cdiag(cre.reshape(nb, gb, S5_GROUP, p)),
                            -cdiag(cim.reshape(nb, gb, S5_GROUP, p))], axis=1)
    return lamv, bmat, cmat


def _s5_powers(lamv):
    sp = lamv.shape[2] // 2
    pr, pi = lamv[:, :, :sp], lamv[:, :, sp:]
    rows = []
    for _ in range(8):
        rows.append(jnp.concatenate([pr, pi], axis=-1))
        pr, pi = pr * pr - pi * pi, 2.0 * pr * pi
    return lax.stop_gradient(jnp.concatenate(rows, axis=1))


def _ln_fwd(h, out, p, tag):
    return _rw_fwd(_ln_fn, [h, out], [p["ln_g"], p["ln_b"]], [h.shape[1]], name=tag + "_ln")[0]


def _ln_bwd(h, out, p, dh2, tag):
    return _rw_bwd(_ln_fn, [h, out], [p["ln_g"], p["ln_b"]], [dh2], name=tag + "_ln_bwd")


def _ssd_layer_fwd(h, p, side=None):
    di, groups = p["wz"].shape[1], p["dtb"].shape[0]
    z = _mm(h, p["wz"], name="ssd_z")
    xr = _mm(h, p["wxbc"], name="ssd_xbc")
    dtp = _mm(h, p["wdt"], name="ssd_dt")
    xbc = _conv_fwd(xr, p["conv_w"], p["conv_b"], name="ssd_conv")
    (y, states), got = _ssd_fwd(xbc, dtp, p["dtb"], p["alog"], p["dsk"], d_inner=di, groups=groups,
                                name="ssd_scan", side=side)
    yn = _rw_fwd(_make_ssd_post_fn(groups), [y, z], [p["norm_w"]], [di], name="ssd_post")[0]
    out = _mm(yn, p["wout"], name="ssd_out")
    return _ln_fwd(h, out, p, "ssd"), (h, z, xr, dtp, xbc, states, y, yn, out), got


def _ssd_layer_bwd(saved, p, dh2, side=None):
    h, z, xr, dtp, xbc, states, y, yn, out = saved
    di, groups = p["wz"].shape[1], p["dtb"].shape[0]
    dh, dout, dg, db = _ln_bwd(h, out, p, dh2, "ssd")
    dyn = _mm(dout, p["wout"], tb=True, name="ssd_out_dx")
    g = {"ln_g": dg, "ln_b": db, "wout": _mm(yn, dout, ta=True, out_dtype=bf16, name="ssd_out_dw")}
    if callable(side):
        side = side(g["wout"])
    dy, dz, g["norm_w"] = _rw_bwd(_make_ssd_post_fn(groups), [y, z], [p["norm_w"]], [dyn], name="ssd_post_bwd")
    (dx, dbm, dcm, ddt, dpar), got = _ssd_bwd(xbc, dtp, p["dtb"], p["alog"], p["dsk"], states, dy,
                                              d_inner=di, groups=groups, name="ssd_scan_bwd", side=side)
    g["dtb"], g["alog"], g["dsk"] = dpar[:, 0:1, :], dpar[:, 1:2, :], dpar[:, 2:3, :]
    dxr, g["conv_w"], g["conv_b"] = _conv_bwd(xr, p["conv_w"], p["conv_b"],
                                               jnp.concatenate([dx, dbm, dcm], axis=1), name="ssd_conv_bwd")
    dh = _mm(dz, p["wz"], tb=True, add=dh, name="ssd_z_dx")
    dh = _mm(dxr, p["wxbc"], tb=True, add=dh, name="ssd_xbc_dx")
    dh = _mm(ddt, p["wdt"], tb=True, add=dh, name="ssd_dt_dx")
    g["wz"] = _mm(h, dz, ta=True, name="ssd_z_dw")
    g["wxbc"] = _mm(h, dxr, ta=True, name="ssd_xbc_dw")
    g["wdt"] = _mm(h, ddt, ta=True, name="ssd_dt_dw")
    return dh, g, got


def _fox_layer_fwd(h, p, side=None):
    w = p["wout"].shape[0]
    heads = w // FOX_HEAD_DIM
    t = h.shape[0]
    qkvz = _mm(h, p["wqkvz"], name="fox_qkvz")
    fr = _mm(h, p["wf"], name="fox_f")
    logf = _rw_fwd(_logf_fn, [fr], [p["fb"]], [LANE], name="fox_logf")[0]
    cum = _cumsum_rows(logf, reverse=False, name="fox_cumsum")
    ck = _lane_bcast(cum[:, :heads].T)
    (o, lse), got = _attn_fwd(qkvz, 0, qkvz, heads, qkvz, 2 * heads, None, None, ck,
                              heads=heads, scale=FOX_HEAD_DIM ** -0.5, name="fox_attn", side=side)
    yg = _rw_fwd(_gate_fn, [o, (qkvz, w, 3)], [], [w], name="fox_gate")[0]
    out = _mm(yg, p["wout"], name="fox_out")
    return _ln_fwd(h, out, p, "fox"), (h, qkvz, fr, ck, o, lse, yg, out), got


def _fox_layer_bwd(saved, p, dh2, side=None):
    h, qkvz, fr, ck, o, lse, yg, out = saved
    w = p["wout"].shape[0]
    heads = w // FOX_HEAD_DIM
    t = h.shape[0]
    dh, dout, dg, db = _ln_bwd(h, out, p, dh2, "fox")
    dyg = _mm(dout, p["wout"], tb=True, name="fox_out_dx")
    g = {"ln_g": dg, "ln_b": db, "wout": _mm(yg, dout, ta=True, out_dtype=bf16, name="fox_out_dw")}
    do, dz = _rw_bwd(_gate_fn, [o, (qkvz, w, 3)], [], [dyg], name="fox_gate_bwd")
    delta = _attn_delta(do, o, heads=heads, name="fox_delta")
    (dq, dk, dv, drow, dck), got = _attn_bwd(qkvz, 0, qkvz, heads, qkvz, 2 * heads, None, None, ck, do, lse, delta,
                                             heads=heads, scale=FOX_HEAD_DIM ** -0.5, name="fox_attn_bwd", side=side)
    dqkvz = jnp.concatenate([dq, dk, dv, dz], axis=1)
    dcum = jnp.pad((drow.reshape(heads, t) + dck.reshape(heads, t)).T, ((0, 0), (0, LANE - heads)))
    dlogf = _cumsum_rows(dcum, reverse=True, name="fox_cumsum_bwd")
    dfr, g["fb"] = _rw_bwd(_logf_fn, [fr], [p["fb"]], [dlogf], name="fox_logf_bwd")
    dh = _mm(dqkvz, p["wqkvz"], tb=True, add=dh, name="fox_qkvz_dx")
    dh = _mm(dfr, p["wf"], tb=True, add=dh, name="fox_f_dx")
    g["wqkvz"] = _mm(h, dqkvz, ta=True, name="fox_qkvz_dw")
    g["wf"] = _mm(h, dfr, ta=True, name="fox_f_dw")
    return dh, g, got


def _mla_layer_fwd(h, p, cos, sin):
    rank = p["qn"].shape[1]
    w = p["wout"].shape[0]
    heads = w // MLA_V
    lat = _mm(h, p["wlat"], name="mla_lat")
    kpr = _mm(h, p["wkpe"], name="mla_kpe")
    z = _mm(h, p["wz"], name="mla_z")
    qnl, kvnl = _rw_fwd(_make_mla_norm_fn(rank), [lat], [p["qn"], p["kvn"]], [rank, rank], name="mla_norm")
    qno = _mm(qnl, p["wqn"], name="mla_qn")
    qpr = _mm(qnl, p["wqp"], name="mla_qp")
    kno = _mm(kvnl, p["wkn"], name="mla_kn")
    v = _mm(kvnl, p["wv"], name="mla_v")
    qp = _rw_fwd(_make_rope_fn(heads), [qpr, cos, sin], [], [heads * LANE], name="mla_rope_q")[0]
    kp = _rw_fwd(_make_rope_fn(1), [kpr, cos, sin], [], [LANE], name="mla_rope_k")[0]
    scale = (MLA_NOPE + MLA_ROPE) ** -0.5
    (o, lse), _ = _attn_fwd(qno, 0, kno, 0, v, 0, qp, kp, None, heads=heads, scale=scale, name="mla_attn")
    yg = _rw_fwd(_gate_fn, [o, z], [], [w], name="mla_gate")[0]
    out = _mm(yg, p["wout"], name="mla_out")
    return _ln_fwd(h, out, p, "mla"), (h, lat, kpr, z, qnl, kvnl, qno, qpr, kno, v, qp, kp, o, lse, yg, out)


def _mla_layer_bwd(saved, p, dh2, cos, sin, side=None):
    h, lat, kpr, z, qnl, kvnl, qno, qpr, kno, v, qp, kp, o, lse, yg, out = saved
    rank = p["qn"].shape[1]
    w = p["wout"].shape[0]
    heads = w // MLA_V
    dh, dout, dg, db = _ln_bwd(h, out, p, dh2, "mla")
    dyg = _mm(dout, p["wout"], tb=True, name="mla_out_dx")
    g = {"ln_g": dg, "ln_b": db, "wout": _mm(yg, dout, ta=True, out_dtype=bf16, name="mla_out_dw")}
    do, dz = _rw_bwd(_gate_fn, [o, z], [], [dyg], name="mla_gate_bwd")
    delta = _attn_delta(do, o, heads=heads, name="mla_delta")
    (dqno, dkno, dv, dqp, dkp), got = _attn_bwd(qno, 0, kno, 0, v, 0, qp, kp, None, do, lse, delta, heads=heads,
                                                scale=(MLA_NOPE + MLA_ROPE) ** -0.5, name="mla_attn_bwd", side=side)
    (dqpr,) = _rw_bwd(_make_rope_fn(heads), [qpr, cos, sin], [], [dqp], n_const=2, name="mla_rope_q_bwd")
    (dkpr,) = _rw_bwd(_make_rope_fn(1), [kpr, cos, sin], [], [dkp], n_const=2, name="mla_rope_k_bwd")
    dqnl = _mm(dqno, p["wqn"], tb=True, name="mla_qn_dx")
    dqnl = _mm(dqpr, p["wqp"], tb=True, add=dqnl, name="mla_qp_dx")
    dkvnl = _mm(dkno, p["wkn"], tb=True, name="mla_kn_dx")
    dkvnl = _mm(dv, p["wv"], tb=True, add=dkvnl, name="mla_v_dx")
    g["wqn"] = _mm(qnl, dqno, ta=True, name="mla_qn_dw")
    g["wqp"] = _mm(qnl, dqpr, ta=True, name="mla_qp_dw")
    g["wkn"] = _mm(kvnl, dkno, ta=True, name="mla_kn_dw")
    g["wv"] = _mm(kvnl, dv, ta=True, name="mla_v_dw")
    dlat, g["qn"], g["kvn"] = _rw_bwd(_make_mla_norm_fn(rank), [lat], [p["qn"], p["kvn"]], [dqnl, dkvnl],
                                     name="mla_norm_bwd")
    dh = _mm(dlat, p["wlat"], tb=True, add=dh, name="mla_lat_dx")
    dh = _mm(dkpr, p["wkpe"], tb=True, add=dh, name="mla_kpe_dx")
    dh = _mm(dz, p["wz"], tb=True, add=dh, name="mla_z_dx")
    g["wlat"] = _mm(h, dlat, ta=True, name="mla_lat_dw")
    g["wkpe"] = _mm(h, dkpr, ta=True, name="mla_kpe_dw")
    g["wz"] = _mm(h, dz, ta=True, name="mla_z_dw")
    return dh, g, got


def _s5_layer_fwd(h, p):
    w = p["wout"].shape[0]
    u = _mm(h, p["wu"], name="s5_u")
    z = _mm(h, p["wz"], name="s5_z")
    ys, xp, states = _s5_fwd(u, p["lamv"], p["pw"], p["bmat"], p["cmat"], name="s5_scan")
    y1 = _rw_fwd(_s5_act_fn, [ys, u], [p["d"]], [w], name="s5_act")[0]
    gp = _mm(y1, p["wglu"], name="s5_glu")
    y2 = _rw_fwd(_s5_glu_fn, [y1, gp, z], [p["bglu"]], [w], name="s5_gate")[0]
    out = _mm(y2, p["wout"], name="s5_out")
    return _ln_fwd(h, out, p, "s5"), (h, u, z, ys, xp, states, y1, gp, y2, out)


def _s5_layer_bwd(saved, p, dh2):
    h, u, z, ys, xp, states, y1, gp, y2, out = saved
    dh, dout, dg, db = _ln_bwd(h, out, p, dh2, "s5")
    dy2 = _mm(dout, p["wout"], tb=True, name="s5_out_dx")
    g = {"ln_g": dg, "ln_b": db, "wout": _mm(y2, dout, ta=True, out_dtype=bf16, name="s5_out_dw")}
    dy1, dgp, dz, g["bglu"] = _rw_bwd(_s5_glu_fn, [y1, gp, z], [p["bglu"]], [dy2], name="s5_gate_bwd")
    dy1 = _mm(dgp, p["wglu"], tb=True, add=dy1, name="s5_glu_dx")
    g["wglu"] = _mm(y1, dgp, ta=True, out_dtype=bf16, name="s5_glu_dw")
    dys, du, g["d"] = _rw_bwd(_s5_act_fn, [ys, u], [p["d"]], [dy1], name="s5_act_bwd")
    du, g["bmat"], g["cmat"], dlam = _s5_bwd(u, dys, du, xp, states, p["lamv"], p["pw"], p["bmat"], p["cmat"],
                                             name="s5_scan_bwd")
    g["lamv"] = dlam[:, 0:1, :]
    dh = _mm(du, p["wu"], tb=True, add=dh, name="s5_u_dx")
    dh = _mm(dz, p["wz"], tb=True, add=dh, name="s5_z_dx")
    g["wu"] = _mm(h, du, ta=True, name="s5_u_dw")
    g["wz"] = _mm(h, dz, ta=True, name="s5_z_dw")
    return dh, g


def _pad_last(a, to):
    return jnp.pad(a, [(0, 0)] * (a.ndim - 1) + [(0, to - a.shape[-1])])


def _row(v):
    return v.reshape(1, -1)


def _prep_ssd(w):
    di, cd, hh = w["ssd_w_out"].shape[0], w["ssd_conv_b"].shape[0], w["ssd_dt_bias"].shape[0]
    groups = (cd - di) // (2 * SSD_STATE)
    e = hh // groups

    def perm(v):
        lead = v.shape[:-1]
        return _pad_last(v.reshape(lead + (groups, e)), LANE).reshape(lead + (groups * LANE,))

    w_in = w["ssd_w_in"]
    return dict(
        wz=w_in[:, :di], wxbc=w_in[:, di:di + cd], wdt=perm(w_in[:, di + cd:]), wout=w["ssd_w_out"],
        conv_w=w["ssd_conv_w"], conv_b=_row(w["ssd_conv_b"]),
        dtb=perm(w["ssd_dt_bias"]).reshape(groups, 1, LANE), alog=perm(w["ssd_a_log"]).reshape(groups, 1, LANE),
        dsk=perm(w["ssd_d"]).reshape(groups, 1, LANE), norm_w=_row(w["ssd_norm_w"]),
        ln_g=_row(w["ln_g"][0]), ln_b=_row(w["ln_b"][0]))


def _prep_fox(w):
    wd = w["fox_w_out"].shape[0]
    w_in = w["fox_w_in"]
    return dict(wqkvz=w_in[:, :4 * wd], wf=_pad_last(w_in[:, 4 * wd:], LANE), wout=w["fox_w_out"],
                fb=_pad_last(_row(w["fox_f_bias"]), LANE), ln_g=_row(w["ln_g"][1]), ln_b=_row(w["ln_b"][1]))


def _prep_mla(w):
    rank = w["mla_q_norm"].shape[0]
    wd = w["mla_w_out"].shape[0]
    heads = wd // MLA_V
    w_in = w["mla_w_in"]
    qu = w["mla_w_q_up"].reshape(rank, heads, MLA_NOPE + MLA_ROPE)
    kvu = w["mla_w_kv_up"].reshape(w["mla_w_kv_up"].shape[0], heads, MLA_NOPE + MLA_V)
    return dict(
        wlat=w_in[:, :2 * rank], wkpe=_pad_last(w_in[:, 2 * rank:2 * rank + MLA_ROPE], LANE),
        wz=w_in[:, 2 * rank + MLA_ROPE:],
        wqn=qu[:, :, :MLA_NOPE].reshape(rank, heads * LANE),
        wqp=_pad_last(qu[:, :, MLA_NOPE:], LANE).reshape(rank, heads * LANE),
        wkn=kvu[:, :, :MLA_NOPE].reshape(-1, heads * LANE), wv=kvu[:, :, MLA_NOPE:].reshape(-1, heads * LANE),
        wout=w["mla_w_out"], qn=_row(w["mla_q_norm"]), kvn=_row(w["mla_kv_norm"]),
        ln_g=_row(w["ln_g"][2]), ln_b=_row(w["ln_b"][2]))


def _prep_s5(w):
    wd = w["s5_w_out"].shape[0]
    w_in = w["s5_w_in"]
    lamv, bmat, cmat = _s5_discretize(w["s5_lambda_re"], w["s5_lambda_im"], w["s5_log_step"],
                                      w["s5_b_re"], w["s5_b_im"], w["s5_c_re"], w["s5_c_im"])
    return dict(wu=w_in[:, :wd], wz=w_in[:, wd:], wglu=w["s5_w_glu"], wout=w["s5_w_out"],
                d=_row(w["s5_d"]), bglu=_row(w["s5_b_glu"]), lamv=lamv, bmat=bmat, cmat=cmat,
                ln_g=_row(w["ln_g"][3]), ln_b=_row(w["ln_b"][3]))


def _rope_tables(positions):
    inv_freq = ROPE_BASE ** (-jnp.arange(0, MLA_ROPE, 2, dtype=f32) / MLA_ROPE)
    ang = positions.astype(f32).reshape(-1, 1) * inv_freq
    return jnp.cos(ang), jnp.sin(ang)


def _heads3(v):
    return v.reshape(v.shape[0], -1, LANE)


def _flat2(v):
    return v.reshape(v.shape[0], -1)


def _natural_grads_ssd(g, e_heads):
    return {"ssd_w_in": jnp.concatenate([g["wz"], g["wxbc"], _flat2(_heads3(g["wdt"])[:, :, :e_heads])], axis=1),
            "ssd_w_out": g["wout"]}


def _natural_grads_fox(g, heads):
    return {"fox_w_in": jnp.concatenate([g["wqkvz"], g["wf"][:, :heads]], axis=1), "fox_w_out": g["wout"]}


def _natural_grads_mla(g):
    return {"mla_w_in": jnp.concatenate([g["wlat"], g["wkpe"][:, :MLA_ROPE], g["wz"]], axis=1),
            "mla_w_q_up": _flat2(jnp.concatenate([_heads3(g["wqn"]), _heads3(g["wqp"])[:, :, :MLA_ROPE]], axis=2)),
            "mla_w_kv_up": _flat2(jnp.concatenate([_heads3(g["wkn"]), _heads3(g["wv"])], axis=2)),
            "mla_w_out": g["wout"]}


def _natural_grads_s5(g):
    return {"s5_w_in": jnp.concatenate([g["wu"], g["wz"]], axis=1), "s5_w_glu": g["wglu"], "s5_w_out": g["wout"]}


def _sequence_step(x, positions, tgt, small, big_of_layer, gather_sides=(None, None), exchange_side=None):
    cos, sin = _rope_tables(positions)
    exchange_side = exchange_side or (lambda nat: None)
    vjps = []

    def prepared(prep, big):
        p, vjp = jax.vjp(lambda s: prep({**big, **s}), small)
        vjps.append(vjp)
        return p

    p0 = prepared(_prep_ssd, big_of_layer[0](None))
    h1, s0, got1 = _ssd_layer_fwd(x, p0, side=gather_sides[0])
    p1 = prepared(_prep_fox, big_of_layer[1](got1))
    h2, s1, got23 = _fox_layer_fwd(h1, p1, side=gather_sides[1])
    p2 = prepared(_prep_mla, big_of_layer[2](got23))
    p3 = prepared(_prep_s5, big_of_layer[3](got23))
    p3["pw"] = _s5_powers(p3["lamv"])
    h3, s2 = _mla_layer_fwd(h2, p2, cos, sin)
    h4, s3 = _s5_layer_fwd(h3, p3)
    dh, loss = _loss_head(h4, tgt)
    dh, g3 = _s5_layer_bwd(s3, p3, dh)
    nat3 = _natural_grads_s5(g3)
    dh, g2, parts3 = _mla_layer_bwd(s2, p2, dh, cos, sin, side=exchange_side(nat3))
    nat2 = _natural_grads_mla(g2)
    dh, g1, parts2 = _fox_layer_bwd(s1, p1, dh, side=exchange_side(nat2))
    nat1 = _natural_grads_fox(g1, small["fox_f_bias"].shape[0])
    dh, g0, parts1 = _ssd_layer_bwd(s0, p0, dh, side=lambda gw: exchange_side({**nat1, "ssd_w_out": gw}))
    nat0 = _natural_grads_ssd(g0, small["ssd_dt_bias"].shape[0] // p0["dtb"].shape[0])
    g_small = None
    for p, vjp, g in zip((p0, p1, p2, p3), vjps, (g0, g1, g2, g3)):
        (gi,) = vjp({k: g[k].astype(p[k].dtype) for k in p if k != "pw"})
        g_small = gi if g_small is None else jax.tree.map(jnp.add, g_small, gi)
    return loss, dh, g_small, (nat0, nat1, nat2, nat3), (parts1, parts2, parts3)


FLAT_COLS = 1024
FLAT_ROW_ALIGN = 128


def _pack(arrs):
    flat = jnp.concatenate([a.reshape(-1) for a in arrs])
    unit = FLAT_COLS * FLAT_ROW_ALIGN
    return jnp.pad(flat, (0, -flat.shape[0] % unit)).reshape(-1, FLAT_COLS)


def _unpack(flat2d, shapes):
    flat = flat2d.reshape(-1)
    out, off = [], 0
    for s in shapes:
        sz = math.prod(s)
        out.append(flat[off:off + sz].reshape(s))
        off += sz
    return out


def _unpack_gathered(gathered, shapes, axes):
    flat = gathered.reshape(N_DEV, -1)
    out, off = [], 0
    for s, ax in zip(shapes, axes):
        sz = math.prod(s)
        seg = flat[:, off:off + sz].reshape((N_DEV,) + tuple(s))
        out.append(jnp.concatenate([seg[d] for d in range(N_DEV)], axis=ax)[0])
        off += sz
    return out


MESH = pl.DeviceIdType.MESH
HBM_SPEC = pl.BlockSpec(memory_space=pl.ANY)


def _comm_scratch(n):
    return [pltpu.SemaphoreType.DMA((n, 7)), pltpu.SemaphoreType.DMA((n, 7)), pltpu.SemaphoreType.DMA((n,))]


class _Gather:
    def __init__(self, xs):
        self.ins = list(xs)
        self.out_shape = [jax.ShapeDtypeStruct((N_DEV,) + v.shape, v.dtype) for v in xs]

    def _plan(self, x_refs, out_refs, sems):
        send_sems, recv_sems, local_sems = sems
        x, y, cc = lax.axis_index("x"), lax.axis_index("y"), lax.axis_index("c")
        me, sibling = (x, y, cc), (x, y, 1 - cc)
        chips = [(1 - x, y), (x, 1 - y), (1 - x, 1 - y)]

        def rows(w, px, py, pc):
            return out_refs[w].at[4 * px + 2 * py + pc]

        def copy(w, k, block, to, src=None):
            return pltpu.make_async_remote_copy(
                src_ref=rows(w, *block) if src is None else src, dst_ref=rows(w, *block),
                send_sem=send_sems.at[w, k], recv_sem=recv_sems.at[w, k], device_id=to, device_id_type=MESH)

        n = len(x_refs)
        mine = [pltpu.make_async_copy(x_refs[w], rows(w, *me), local_sems.at[w]) for w in range(n)]
        first = []
        for w in range(n):
            first.append(copy(w, 0, me, sibling, src=x_refs[w]))
            first += [copy(w, 1 + j, me, (*chip, cc), src=x_refs[w]) for j, chip in enumerate(chips)]
        return n, me, sibling, chips, cc, copy, mine, first

    def start(self, x_refs, out_refs, sems):
        *_, mine, first = self._plan(x_refs, out_refs, sems)
        for cp in mine + first:
            cp.start()

    def finish(self, x_refs, out_refs, sems):
        n, me, sibling, chips, cc, copy, mine, first = self._plan(x_refs, out_refs, sems)
        passed = []
        for w in range(n):
            for j, chip in enumerate(chips):
                copy(w, 1 + j, (*chip, cc), me).wait_recv()
                cp = copy(w, 4 + j, (*chip, cc), sibling)
                cp.start()
                passed.append(cp)
        for w in range(n):
            copy(w, 0, sibling, me).wait_recv()
            for j, chip in enumerate(chips):
                copy(w, 4 + j, (*chip, 1 - cc), me).wait_recv()
        for cp in first + passed:
            cp.wait_send()
        for cp in mine:
            cp.wait()


class _Exchange:
    def __init__(self, gs):
        self.ins = list(gs)
        self.out_shape = [jax.ShapeDtypeStruct(v.shape, v.dtype) for v in gs]

    def _plan(self, g_refs, out_refs, sems):
        send_sems, recv_sems, local_sems = sems
        x, y, cc = lax.axis_index("x"), lax.axis_index("y"), lax.axis_index("c")
        me = 4 * x + 2 * y + cc
        peers = []
        for k in range(1, N_DEV):
            px = 1 - x if (k >> 2) & 1 else x
            py = 1 - y if (k >> 1) & 1 else y
            pc = 1 - cc if k & 1 else cc
            peers.append((k, (px, py, pc), 4 * px + 2 * py + pc))

        def copy(w, k, peer, slot_src, slot_dst):
            return pltpu.make_async_remote_copy(
                src_ref=g_refs[w].at[slot_src], dst_ref=out_refs[w].at[slot_dst],
                send_sem=send_sems.at[w, k - 1], recv_sem=recv_sems.at[w, k - 1], device_id=peer, device_id_type=MESH)

        n = len(g_refs)
        mine = [pltpu.make_async_copy(g_refs[w].at[me], out_refs[w].at[me], local_sems.at[w]) for w in range(n)]
        sends = [copy(w, k, peer, idx, me) for w in range(n) for k, peer, idx in peers]
        return n, peers, copy, mine, sends

    def start(self, g_refs, out_refs, sems):
        *_, mine, sends = self._plan(g_refs, out_refs, sems)
        for cp in mine + sends:
            cp.start()

    def finish(self, g_refs, out_refs, sems):
        n, peers, copy, mine, sends = self._plan(g_refs, out_refs, sems)
        for w in range(n):
            for k, peer, idx in peers:
                copy(w, k, peer, idx, idx).wait_recv()
        for cp in sends:
            cp.wait_send()
        for cp in mine:
            cp.wait()


def _run_comm(side, *, name):
    n = len(side.ins)

    def body(*refs):
        ins, outs, sems = refs[:n], refs[n:2 * n], refs[2 * n:]
        side.start(ins, outs, sems)
        side.finish(ins, outs, sems)

    return pl.pallas_call(
        body, name=name, out_shape=side.out_shape,
        in_specs=[HBM_SPEC] * n, out_specs=[HBM_SPEC] * n, scratch_shapes=_comm_scratch(n),
    )(*side.ins)


def _hosted_call(body, side, *, name, grid, in_specs, out_specs, out_shape, scratch_shapes, args):
    out_specs, out_shape = list(out_specs), list(out_shape)
    if side is None:
        res = pl.pallas_call(
            body, name=name, grid=grid, in_specs=in_specs, out_specs=out_specs, out_shape=out_shape,
            scratch_shapes=scratch_shapes,
            compiler_params=pltpu.CompilerParams(dimension_semantics=("arbitrary",) * len(grid)))(*args)
        return res, None
    n_in, n_out, n_scr, ns = len(in_specs), len(out_specs), len(scratch_shapes), len(side.ins)

    def wrapped(*refs):
        ins, s_ins = refs[:n_in], refs[n_in:n_in + ns]
        outs = refs[n_in + ns:n_in + ns + n_out]
        s_outs = refs[n_in + ns + n_out:n_in + 2 * ns + n_out]
        scr = refs[n_in + 2 * ns + n_out:n_in + 2 * ns + n_out + n_scr]
        sems = refs[n_in + 2 * ns + n_out + n_scr:]
        ids = [pl.program_id(ax) for ax in range(len(grid))]
        first = functools.reduce(lambda p, q: p & q, [i == 0 for i in ids])
        last = functools.reduce(lambda p, q: p & q, [i == g - 1 for i, g in zip(ids, grid)])

        @pl.when(first)
        def _():
            side.start(s_ins, s_outs, sems)

        body(*ins, *outs, *scr)

        @pl.when(last)
        def _():
            side.finish(s_ins, s_outs, sems)

    res = pl.pallas_call(
        wrapped, name=name, grid=grid, in_specs=list(in_specs) + [HBM_SPEC] * ns,
        out_specs=out_specs + [HBM_SPEC] * ns, out_shape=out_shape + side.out_shape,
        scratch_shapes=list(scratch_shapes) + _comm_scratch(ns),
        compiler_params=pltpu.CompilerParams(dimension_semantics=("arbitrary",) * len(grid)))(*args, *side.ins)
    return res[:n_out], res[n_out:]


BIG = (("ssd_w_in", 2), ("ssd_w_out", 1), ("fox_w_in", 2), ("fox_w_out", 1), ("mla_w_in", 2),
       ("mla_w_q_up", 2), ("mla_w_kv_up", 2), ("mla_w_out", 1), ("s5_w_in", 2), ("s5_w_glu", 1), ("s5_w_out", 1))
LAYER_BIG = (("ssd_w_in", "ssd_w_out"), ("fox_w_in", "fox_w_out"),
             ("mla_w_in", "mla_w_q_up", "mla_w_kv_up", "mla_w_out"), ("s5_w_in", "s5_w_glu", "s5_w_out"))
SMALL = (("ssd_conv_w", 2), ("mla_q_norm", 1), ("mla_kv_norm", 1), ("s5_d", 1), ("s5_b_glu", 1))
REPLICATED = ("ln_g", "ln_b", "ssd_conv_b", "ssd_dt_bias", "ssd_a_log", "ssd_d", "ssd_norm_w", "fox_f_bias",
              "s5_lambda_re", "s5_lambda_im", "s5_log_step", "s5_b_re", "s5_b_im", "s5_c_re", "s5_c_im")
WEIGHT_ORDER = ("ln_g", "ln_b", "ssd_w_in", "ssd_conv_w", "ssd_conv_b", "ssd_dt_bias", "ssd_a_log", "ssd_d",
                "ssd_norm_w", "ssd_w_out", "fox_w_in", "fox_f_bias", "fox_w_out", "mla_w_in", "mla_q_norm",
                "mla_kv_norm", "mla_w_q_up", "mla_w_kv_up", "mla_w_out", "s5_w_in", "s5_lambda_re", "s5_lambda_im",
                "s5_log_step", "s5_b_re", "s5_b_im", "s5_c_re", "s5_c_im", "s5_d", "s5_w_glu", "s5_b_glu", "s5_w_out")


def _sum_adamw(parts, w, m, v, *, name):
    r, c = w.shape
    row_bytes = 2 * c * (N_DEV * parts.dtype.itemsize + 7 * 4)
    tr = _pick(r, [t for t in (256, 128, 64, 32, 16) if t * row_bytes <= VMEM_BLOCK_BUDGET])

    def body(p_ref, w_ref, m_ref, v_ref, g_ref, d_ref, m2_ref, v2_ref):
        gv = p_ref[0].astype(f32)
        for s in range(1, N_DEV):
            gv = gv + p_ref[s].astype(f32)
        g_ref[...] = gv
        m2 = ADAM_B1 * m_ref[...] + (1.0 - ADAM_B1) * gv
        v2 = ADAM_B2 * v_ref[...] + (1.0 - ADAM_B2) * jnp.square(gv)
        m_hat = m2 / (1.0 - ADAM_B1 ** ADAM_STEP)
        v_hat = v2 / (1.0 - ADAM_B2 ** ADAM_STEP)
        d_ref[...] = -ADAM_LR * (m_hat / (jnp.sqrt(v_hat) + ADAM_EPS) + ADAM_WD * w_ref[...])
        m2_ref[...] = m2
        v2_ref[...] = v2

    spec = pl.BlockSpec((tr, c), lambda i: (i, 0))
    shp = jax.ShapeDtypeStruct((r, c), f32)
    return pl.pallas_call(
        body, name=name, grid=(r // tr,),
        in_specs=[pl.BlockSpec((N_DEV, tr, c), lambda i: (0, i, 0))] + [spec] * 3,
        out_specs=[spec] * 4, out_shape=[shp] * 4,
        compiler_params=pltpu.CompilerParams(dimension_semantics=("parallel",)),
    )(parts, w, m, v)


def _train_step(a):
    small_names, small_axes = [n for n, _ in SMALL], [ax for _, ax in SMALL]
    small_shapes = [a[n].shape for n in small_names]
    rep_shapes = [a[n].shape for n in REPLICATED]

    big_axis = dict(BIG)

    def shards(names):
        return [a[n][0].astype(bf16) for n in names]

    def natural(names, gathered):
        w = {}
        for n, g in zip(names, gathered):
            if big_axis[n] == 1:
                w[n] = g.reshape(-1, g.shape[2])
            else:
                w[n] = jnp.transpose(g, (1, 0, 2)).reshape(g.shape[1], -1)
        return w

    def to_owner(nat):
        blocks = []
        for n, g in nat.items():
            if big_axis[n] == 1:
                g = g.reshape(N_DEV, -1, g.shape[1])
            else:
                g = jnp.transpose(g.reshape(g.shape[0], N_DEV, -1), (1, 0, 2))
            blocks.append(g.astype(bf16))
        return blocks

    small_flat = _pack([a[n] for n in small_names])
    got0 = _run_comm(_Gather(shards(LAYER_BIG[0]) + [small_flat]), name="gather_ssd_weights")
    small = dict(zip(small_names, _unpack_gathered(got0[-1], small_shapes, small_axes)))
    for n in REPLICATED:
        small[n] = a[n] if n in ("ln_g", "ln_b") else a[n][0]
    rest_names = LAYER_BIG[2] + LAYER_BIG[3]
    big_of_layer = (lambda got: natural(LAYER_BIG[0], got0[:-1]),
                    lambda got: natural(LAYER_BIG[1], got),
                    lambda got: natural(LAYER_BIG[2], got[:len(LAYER_BIG[2])]),
                    lambda got: natural(LAYER_BIG[3], got[len(LAYER_BIG[2]):]))
    loss, grad_x, g_small, nats, hosted_parts = _sequence_step(
        a["x"][0], a["positions"][0], a["loss_target"][0], small, big_of_layer,
        gather_sides=(_Gather(shards(LAYER_BIG[1])), _Gather(shards(rest_names))),
        exchange_side=lambda nat: _Exchange(to_owner(nat)))
    small_to_owner = jax.linear_transpose(
        lambda s: _unpack_gathered(s, small_shapes, small_axes), jax.ShapeDtypeStruct(got0[-1].shape, f32))
    (g_small_blocks,) = small_to_owner([g_small[n] for n in small_names])
    parts0 = _run_comm(_Exchange(to_owner({"ssd_w_in": nats[0]["ssd_w_in"]}) + [g_small_blocks]),
                       name="exchange_ssd_gradients")
    parts_of = {"ssd_w_in": parts0[0]}
    for names, parts in zip((LAYER_BIG[1] + ("ssd_w_out",), LAYER_BIG[2], LAYER_BIG[3]), hosted_parts):
        parts_of.update(zip(names, parts, strict=True))

    out = {"grad": {}, "delta": {}, "new_m": {}, "new_v": {}}
    kinds = ("grad", "delta", "new_m", "new_v")
    for n, _ in BIG:
        res = _sum_adamw(parts_of[n], a[n][0], a["m_" + n][0], a["v_" + n][0], name="adamw_" + n)
        for kind, r in zip(kinds, res):
            out[kind][n] = r[None]
    upd_small = _sum_adamw(parts0[-1], small_flat, _pack([a["m_" + n] for n in small_names]),
                           _pack([a["v_" + n] for n in small_names]), name="adamw_small_sharded")
    for kind, r in zip(kinds, upd_small):
        out[kind].update(zip(small_names, _unpack(r, small_shapes)))
    g_rep_local = _pack([g_small[n] if n in ("ln_g", "ln_b") else g_small[n][None] for n in REPLICATED])
    (g_rep_all,) = _run_comm(_Gather([g_rep_local]), name="gather_replicated_gradients")
    upd_rep = _sum_adamw(g_rep_all, _pack([a[n] for n in REPLICATED]), _pack([a["m_" + n] for n in REPLICATED]),
                         _pack([a["v_" + n] for n in REPLICATED]), name="adamw_replicated")
    for kind, r in zip(kinds, upd_rep):
        out[kind].update(zip(REPLICATED, _unpack(r, rep_shapes)))
    loss_total = lax.psum(loss[0, 0], ("x", "y", "c"))
    return (loss_total, grad_x[None],
            *[out[kind][n] for kind in ("grad", "delta", "new_m", "new_v") for n in WEIGHT_ORDER])


def kernel(x, positions, ln_g, ln_b, ssd_w_in, ssd_conv_w, ssd_conv_b, ssd_dt_bias, ssd_a_log, ssd_d, ssd_norm_w, ssd_w_out, fox_w_in, fox_f_bias, fox_w_out, mla_w_in, mla_q_norm, mla_kv_norm, mla_w_q_up, mla_w_kv_up, mla_w_out, s5_w_in, s5_lambda_re, s5_lambda_im, s5_log_step, s5_b_re, s5_b_im, s5_c_re, s5_c_im, s5_d, s5_w_glu, s5_b_glu, s5_w_out, loss_target, m_ln_g, m_ln_b, m_ssd_w_in, m_ssd_conv_w, m_ssd_conv_b, m_ssd_dt_bias, m_ssd_a_log, m_ssd_d, m_ssd_norm_w, m_ssd_w_out, m_fox_w_in, m_fox_f_bias, m_fox_w_out, m_mla_w_in, m_mla_q_norm, m_mla_kv_norm, m_mla_w_q_up, m_mla_w_kv_up, m_mla_w_out, m_s5_w_in, m_s5_lambda_re, m_s5_lambda_im, m_s5_log_step, m_s5_b_re, m_s5_b_im, m_s5_c_re, m_s5_c_im, m_s5_d, m_s5_w_glu, m_s5_b_glu, m_s5_w_out, v_ln_g, v_ln_b, v_ssd_w_in, v_ssd_conv_w, v_ssd_conv_b, v_ssd_dt_bias, v_ssd_a_log, v_ssd_d, v_ssd_norm_w, v_ssd_w_out, v_fox_w_in, v_fox_f_bias, v_fox_w_out, v_mla_w_in, v_mla_q_norm, v_mla_kv_norm, v_mla_w_q_up, v_mla_w_kv_up, v_mla_w_out, v_s5_w_in, v_s5_lambda_re, v_s5_lambda_im, v_s5_log_step, v_s5_b_re, v_s5_b_im, v_s5_c_re, v_s5_c_im, v_s5_d, v_s5_w_glu, v_s5_b_glu, v_s5_w_out):
    return _train_step(dict(locals()))
```

```python
import functools
import math

import jax
import jax.numpy as jnp
from jax import lax
from jax.experimental import pallas as pl
from jax.experimental.pallas import tpu as pltpu

f32 = jnp.float32
bf16 = jnp.bfloat16

N_DEV = 8
DEPTH = 4
ALPHA = (2.0 * DEPTH) ** 0.25
LN_EPS = 1e-5
RMS_EPS = 1e-6
LANE = 128

SSD_STATE = 128
SSD_HEADDIM = 64
SSD_CHUNK = 128
FOX_HEAD_DIM = 128
MLA_NOPE = 128
MLA_ROPE = 64
MLA_V = 128
ROPE_BASE = 10000.0
S5_GROUP = 16
S5_BLOCK_GROUPS = 8
S5_CHUNK = 128

ADAM_LR = 0.001
ADAM_B1 = 0.9
ADAM_B2 = 0.999
ADAM_EPS = 1e-08
ADAM_WD = 0.01
ADAM_STEP = 10

VMEM_BLOCK_BUDGET = 20 * 1024 * 1024
MM_FULL_K = 2048
HIGHEST = lax.Precision.HIGHEST


def _pick(n, cands):
    for c in cands:
        if n % c == 0:
            return c
    return n


def _dg(a, b, ca, cb):
    return lax.dot_general(a.astype(bf16), b.astype(bf16), (((ca,), (cb,)), ((), ())),
                           preferred_element_type=f32)


@functools.partial(jax.custom_vjp, nondiff_argnums=(2, 3))
def bdot(a, b, ca, cb):
    return _dg(a, b, ca, cb)


def _bdot_fwd(a, b, ca, cb):
    return _dg(a, b, ca, cb), (a, b)


def _bdot_bwd(ca, cb, res, g):
    a, b = res
    nb = 1 - cb
    ma = 1 - ca
    da = _dg(g, b, 1, nb) if ca == 1 else _dg(b, g, nb, 1)
    db = _dg(a, g, ma, 0) if cb == 0 else _dg(g, a, 0, ma)
    return da, db


bdot.defvjp(_bdot_fwd, _bdot_bwd)


def _silu(x):
    return x * jax.nn.sigmoid(x)


def _softplus(x):
    return jnp.maximum(x, 0.0) + jnp.log(1.0 + jnp.exp(-jnp.abs(x)))


def _mm(a, b, *, ta=False, tb=False, add=None, out_dtype=f32, name, side=None):
    m, k = (a.shape[1], a.shape[0]) if ta else a.shape
    n = b.shape[0] if tb else b.shape[1]
    kb = b.shape[1] if tb else b.shape[0]
    assert k == kb, (name, a.shape, b.shape)
    if k <= MM_FULL_K:
        tm = _pick(m, (512, 256, 128))
        tn = _pick(n, (1024, 512, 384, 256, 128))
        tk = k
    else:
        tm = _pick(m, (1024, 512, 256, 128))
        tn = _pick(n, (1024, 512, 384, 256, 128))
        tk = _pick(k, (512, 256, 128))
    nk = k // tk
    has_add = add is not None

    def body(*refs):
        if has_add:
            a_ref, b_ref, add_ref, o_ref, acc = refs
        else:
            a_ref, b_ref, o_ref, acc = refs
        kk = pl.program_id(2)

        def prod():
            return _dg(a_ref[...], b_ref[...], 0 if ta else 1, 1 if tb else 0)

        def finish(r):
            o_ref[...] = (r + add_ref[...] if has_add else r).astype(o_ref.dtype)

        if nk == 1:
            finish(prod())
            return

        @pl.when(kk == 0)
        def _():
            acc[...] = prod()

        @pl.when((kk > 0) & (kk < nk - 1))
        def _():
            acc[...] += prod()

        @pl.when(kk == nk - 1)
        def _():
            finish(acc[...] + prod())

    a_spec = (pl.BlockSpec((tk, tm), lambda i, j, kk: (kk, i)) if ta
              else pl.BlockSpec((tm, tk), lambda i, j, kk: (i, kk)))
    b_spec = (pl.BlockSpec((tn, tk), lambda i, j, kk: (j, kk)) if tb
              else pl.BlockSpec((tk, tn), lambda i, j, kk: (kk, j)))
    o_spec = pl.BlockSpec((tm, tn), lambda i, j, kk: (i, j))
    in_specs = [a_spec, b_spec] + ([o_spec] if has_add else [])
    args = (a, b) + ((add,) if has_add else ())
    if side is not None:
        (out,), got = _hosted_call(
            body, side, name=name, grid=(m // tm, n // tn, nk), in_specs=in_specs, out_specs=[o_spec],
            out_shape=[jax.ShapeDtypeStruct((m, n), out_dtype)], scratch_shapes=[pltpu.VMEM((tm, tn), f32)],
            args=args)
        return out, got
    return pl.pallas_call(
        body, name=name, grid=(m // tm, n // tn, nk),
        in_specs=in_specs, out_specs=o_spec,
        out_shape=jax.ShapeDtypeStruct((m, n), out_dtype),
        scratch_shapes=[pltpu.VMEM((tm, tn), f32)],
        compiler_params=pltpu.CompilerParams(dimension_semantics=("parallel", "parallel", "arbitrary")),
    )(*args)


def _row_operand(op):
    if isinstance(op, tuple):
        arr, w, cb = op
    else:
        arr, w, cb = op, op.shape[1], 0
    return arr, w, cb


def _rw_tm(t, widths):
    per_row = 2 * 4 * sum(widths)
    tm = 512
    while tm > 8 and (tm * per_row > VMEM_BLOCK_BUDGET or t % tm):
        tm //= 2
    return tm


def _rw_specs(ops, tm):
    specs, arrs = [], []
    for op in ops:
        arr, w, cb = _row_operand(op)
        specs.append(pl.BlockSpec((tm, w), functools.partial(lambda i, cb: (i, cb), cb=cb)))
        arrs.append(arr)
    return specs, arrs


def _param_spec(p):
    nd = p.ndim
    return pl.BlockSpec(p.shape, lambda i: (0,) * nd)


def _rw_fwd(fn, rows, params, out_widths, *, name):
    t = _row_operand(rows[0])[0].shape[0]
    tm = _rw_tm(t, [_row_operand(r)[1] for r in rows] + list(out_widths))
    nr, npar = len(rows), len(params)

    def body(*refs):
        vals = [r[...] for r in refs[:nr + npar]]
        outs = fn(*vals)
        for o_ref, v in zip(refs[nr + npar:], outs):
            o_ref[...] = v

    rspecs, rarrs = _rw_specs(rows, tm)
    return pl.pallas_call(
        body, name=name, grid=(t // tm,),
        in_specs=rspecs + [_param_spec(p) for p in params],
        out_specs=[pl.BlockSpec((tm, w), lambda i: (i, 0)) for w in out_widths],
        out_shape=[jax.ShapeDtypeStruct((t, w), f32) for w in out_widths],
        compiler_params=pltpu.CompilerParams(dimension_semantics=("parallel",)),
    )(*rarrs, *params)


def _rw_bwd(fn, rows, params, cots, *, name, n_const=0):
    t = _row_operand(rows[0])[0].shape[0]
    nr, npar, nc = len(rows), len(params), len(cots)
    nd = nr - n_const
    widths = [_row_operand(r)[1] for r in rows]
    tm = _rw_tm(t, widths + widths[:nd] + [c.shape[1] for c in cots])

    def body(*refs):
        rv = [r[...] for r in refs[:nr]]
        pv = [r[...] for r in refs[nr:nr + npar]]
        cv = [r[...] for r in refs[nr + npar:nr + npar + nc]]
        outs = refs[nr + npar + nc:]
        consts = rv[nd:]

        def f(*diff):
            return fn(*diff[:nd], *consts, *diff[nd:])

        _, vjp = jax.vjp(f, *rv[:nd], *pv)
        g = vjp(tuple(cv))
        for o_ref, v in zip(outs[:nd], g[:nd]):
            o_ref[...] = v
        if npar:
            @pl.when(pl.program_id(0) == 0)
            def _():
                for o_ref in outs[nd:]:
                    o_ref[...] = jnp.zeros_like(o_ref)

            for o_ref, v in zip(outs[nd:], g[nd:]):
                o_ref[...] += v

    rspecs, rarrs = _rw_specs(rows, tm)
    cspecs = [pl.BlockSpec((tm, c.shape[1]), lambda i: (i, 0)) for c in cots]
    return pl.pallas_call(
        body, name=name, grid=(t // tm,),
        in_specs=rspecs + [_param_spec(p) for p in params] + cspecs,
        out_specs=[pl.BlockSpec((tm, w), lambda i: (i, 0)) for w in widths[:nd]]
        + [_param_spec(p) for p in params],
        out_shape=[jax.ShapeDtypeStruct((t, w), f32) for w in widths[:nd]]
        + [jax.ShapeDtypeStruct(p.shape, f32) for p in params],
        compiler_params=pltpu.CompilerParams(dimension_semantics=("arbitrary",)),
    )(*rarrs, *params, *cots)


def _ln_fn(h, o, g, b):
    x = ALPHA * h + o
    mu = jnp.mean(x, -1, keepdims=True)
    var = jnp.mean(jnp.square(x - mu), -1, keepdims=True)
    return ((x - mu) * lax.rsqrt(var + LN_EPS) * g + b,)


def _gate_fn(o, z):
    return (o * _silu(z),)


def _make_ssd_post_fn(groups):
    def fn(y, z, nw):
        yz = y * _silu(z)
        gw = yz.shape[1] // groups
        outs = []
        for g in range(groups):
            blk = yz[:, g * gw:(g + 1) * gw]
            outs.append(blk * lax.rsqrt(jnp.mean(jnp.square(blk), -1, keepdims=True) + RMS_EPS))
        return (jnp.concatenate(outs, axis=1) * nw,)
    return fn


def _logf_fn(f, b):
    return (jax.nn.log_sigmoid(f + b),)


def _make_mla_norm_fn(rank):
    def fn(lat, qn, kvn):
        def rms(x, w):
            return x * lax.rsqrt(jnp.mean(jnp.square(x), -1, keepdims=True) + RMS_EPS) * w
        return rms(lat[:, :rank], qn), rms(lat[:, rank:], kvn)
    return fn


def _rope_block(xb, cos, sin):
    half = MLA_ROPE // 2
    x1, x2 = xb[:, :half], xb[:, half:MLA_ROPE]
    return jnp.concatenate([x1 * cos - x2 * sin, x1 * sin + x2 * cos, xb[:, MLA_ROPE:]], axis=1)


def _make_rope_fn(heads):
    def fn(x, cos, sin):
        return (jnp.concatenate([_rope_block(x[:, h * LANE:(h + 1) * LANE], cos, sin)
                                 for h in range(heads)], axis=1),)
    return fn


def _s5_act_fn(ys, u, d):
    return (jax.nn.gelu(ys + d * u),)


def _s5_glu_fn(y1, gp, z, bg):
    return (y1 * jax.nn.sigmoid(gp + bg) * _silu(z),)


def _loss_head(y, tgt):
    t, d = y.shape
    tm = _rw_tm(t, [d, d, d])

    def body(y_ref, t_ref, dy_ref, l_ref):
        @pl.when(pl.program_id(0) == 0)
        def _():
            l_ref[...] = jnp.zeros_like(l_ref)

        e = y_ref[...] - t_ref[...]
        dy_ref[...] = e * (1.0 / d)
        l_ref[...] += 0.5 * jnp.sum(jnp.mean(jnp.square(e), axis=-1, keepdims=True), axis=0, keepdims=True)

    spec = pl.BlockSpec((tm, d), lambda i: (i, 0))
    return pl.pallas_call(
        body, name="loss_head", grid=(t // tm,), in_specs=[spec, spec],
        out_specs=[spec, pl.BlockSpec((1, 1), lambda i: (0, 0))],
        out_shape=[jax.ShapeDtypeStruct((t, d), f32), jax.ShapeDtypeStruct((1, 1), f32)],
        compiler_params=pltpu.CompilerParams(dimension_semantics=("arbitrary",)),
    )(y, tgt)


def _attn_tile(t):
    return _pick(t, (512, 256, 128))


ATTN_HEADS_PER_STEP = 4
ATTN_BWD_HEADS_PER_STEP = 4
SSD_GROUPS_PER_STEP = 1
S5_BLOCKS_PER_STEP = 2


def _logits_t(k1, q1, k2, q2, ck_col, scale, key0, query0):
    s = _dg(k1, q1, 1, 1)
    if q2 is not None:
        s = s + _dg(k2, q2, 1, 1)
    s = s * scale
    if ck_col is not None:
        s = s - ck_col
    key = key0 + lax.broadcasted_iota(jnp.int32, s.shape, 0)
    query = query0 + lax.broadcasted_iota(jnp.int32, s.shape, 1)
    return jnp.where(key <= query, s, -jnp.inf)


def _lane_bcast(v):
    return jnp.broadcast_to(v[:, :, None], v.shape + (LANE,))


def _attn_fwd(q1, q1o, k1, k1o, v, vo, q2, k2, ck, *, heads, scale, name, side=None):
    t = q1.shape[0]
    tq = tk = _attn_tile(t)
    nq = t // tq
    has2, hasb = q2 is not None, ck is not None
    hp = ATTN_HEADS_PER_STEP
    wd = hp * LANE
    assert heads % hp == 0 and q1o % hp == 0 and k1o % hp == 0 and vo % hp == 0

    def body(*refs):
        it = iter(refs)
        q1r, k1r, vr = next(it), next(it), next(it)
        q2r, k2r = (next(it), next(it)) if has2 else (None, None)
        ckr = next(it) if hasb else None
        o_r, lse_r, m_s, l_s, acc = next(it), next(it), next(it), next(it), next(it)
        i, j = pl.program_id(1), pl.program_id(2)

        @pl.when(j == 0)
        def _():
            m_s[...] = jnp.full_like(m_s, -jnp.inf)
            l_s[...] = jnp.zeros_like(l_s)
            acc[...] = jnp.zeros_like(acc)

        @pl.when(j <= i)
        def _():
            k2v = k2r[...].astype(bf16) if has2 else None
            for e in range(hp):
                ln = slice(e * LANE, (e + 1) * LANE)
                q1v, k1v, vv = (r[:, ln].astype(bf16) for r in (q1r, k1r, vr))
                q2v = q2r[:, ln].astype(bf16) if has2 else None
                s = _logits_t(k1v, q1v, k2v, q2v, ckr[e][:, 0:1] if hasb else None, scale, j * tk, i * tq)
                m_prev = m_s[e]
                m_new = jnp.maximum(m_prev, jnp.max(s, axis=0, keepdims=True))
                p = jnp.exp(s - m_new)
                a = jnp.exp(m_prev - m_new)
                l_s[e] = a * l_s[e] + jnp.sum(p, axis=0, keepdims=True)
                acc[e] = a * acc[e] + _dg(vv, p, 0, 0)
                m_s[e] = m_new

        @pl.when(j == i)
        def _():
            for e in range(hp):
                o_r[:, e * LANE:(e + 1) * LANE] = (acc[e] / l_s[e]).T
                lse_r[e, 0] = m_s[e] + jnp.log(l_s[e])

    def qmap(off):
        return lambda h, i, j: (i, off // hp + h)

    def kmap(off):
        return lambda h, i, j: (jnp.minimum(j, i), off // hp + h)

    in_specs = [pl.BlockSpec((tq, wd), qmap(q1o)), pl.BlockSpec((tk, wd), kmap(k1o)),
                pl.BlockSpec((tk, wd), kmap(vo))]
    args = [q1, k1, v]
    if has2:
        in_specs += [pl.BlockSpec((tq, wd), qmap(0)),
                     pl.BlockSpec((tk, LANE), lambda h, i, j: (jnp.minimum(j, i), 0))]
        args += [q2, k2]
    if hasb:
        in_specs += [pl.BlockSpec((hp, tk, LANE), lambda h, i, j: (h, jnp.minimum(j, i), 0))]
        args += [ck]
    return _hosted_call(
        body, side, name=name, grid=(heads // hp, nq, nq), in_specs=in_specs,
        out_specs=[pl.BlockSpec((tq, wd), lambda h, i, j: (i, h)),
                   pl.BlockSpec((hp, 1, 1, tq), lambda h, i, j: (h, i, 0, 0))],
        out_shape=[jax.ShapeDtypeStruct((t, heads * LANE), f32),
                   jax.ShapeDtypeStruct((heads, nq, 1, tq), f32)],
        scratch_shapes=[pltpu.VMEM((hp, 1, tq), f32), pltpu.VMEM((hp, 1, tq), f32),
                        pltpu.VMEM((hp, LANE, tq), f32)],
        args=args)


def _row_sums_as_row(x, first_lane_only=False):
    sel = jnp.ones((8, x.shape[1]), f32)
    if first_lane_only:
        sel = (lax.broadcasted_iota(jnp.int32, sel.shape, 1) == 0).astype(f32)
    return lax.dot_general(sel, x, (((1,), (1,)), ((), ())), precision=HIGHEST, preferred_element_type=f32)[0:1]


def _attn_delta(do, o, *, heads, name):
    t = do.shape[0]
    tq = _attn_tile(t)
    hp = ATTN_HEADS_PER_STEP
    assert heads % hp == 0

    def body(do_ref, o_ref, d_ref):
        for e in range(hp):
            ln = slice(e * LANE, (e + 1) * LANE)
            d_ref[e, 0] = _row_sums_as_row(do_ref[:, ln] * o_ref[:, ln])

    spec = pl.BlockSpec((tq, hp * LANE), lambda i, h: (i, h))
    return pl.pallas_call(
        body, name=name, grid=(t // tq, heads // hp), in_specs=[spec, spec],
        out_specs=pl.BlockSpec((hp, 1, 1, tq), lambda i, h: (h, i, 0, 0)),
        out_shape=jax.ShapeDtypeStruct((heads, t // tq, 1, tq), f32),
        compiler_params=pltpu.CompilerParams(dimension_semantics=("parallel", "parallel")),
    )(do, o)


def _attn_bwd(q1, q1o, k1, k1o, v, vo, q2, k2, ck, do, lse, delta, *, heads, scale, name, side=None):
    t = q1.shape[0]
    tq = tk = _attn_tile(t)
    nq = t // tq
    has2, hasb = q2 is not None, ck is not None
    hp = ATTN_BWD_HEADS_PER_STEP // 2 if has2 else ATTN_BWD_HEADS_PER_STEP
    wd = hp * LANE
    assert heads % hp == 0 and q1o % hp == 0 and k1o % hp == 0 and vo % hp == 0

    def body(*refs):
        it = iter(refs)
        q1r, k1r, vr = next(it), next(it), next(it)
        q2r, k2r = (next(it), next(it)) if has2 else (None, None)
        ckr = next(it) if hasb else None
        dor, lser, dlr = next(it), next(it), next(it)
        dq1r, dk1r, dvr = next(it), next(it), next(it)
        dq2r, dk2r = (next(it), next(it)) if has2 else (None, None)
        drowr, dckr = (next(it), next(it)) if hasb else (None, None)
        ak, av = next(it), next(it)
        ac = next(it) if hasb else None
        h, j, i = pl.program_id(0), pl.program_id(1), pl.program_id(2)

        @pl.when((j == 0) & (i == 0))
        def _():
            dq1r[...] = jnp.zeros_like(dq1r)
            if has2:
                dq2r[...] = jnp.zeros_like(dq2r)
            if hasb:
                drowr[...] = jnp.zeros_like(drowr)

        if has2:
            @pl.when((h == 0) & (j == 0) & (i == 0))
            def _():
                dk2r[...] = jnp.zeros_like(dk2r)

        @pl.when(i == j)
        def _():
            ak[...] = jnp.zeros_like(ak)
            av[...] = jnp.zeros_like(av)
            if hasb:
                ac[...] = jnp.zeros_like(ac)

        @pl.when(i >= j)
        def _():
            rows = pl.ds(pl.multiple_of(i * tq, tq), tq)
            cols = pl.ds(pl.multiple_of(j * tk, tk), tk)
            k2v = k2r[...].astype(bf16) if has2 else None
            for e in range(hp):
                ln = slice(e * LANE, (e + 1) * LANE)
                q1v, k1v, vv, dov = (r[:, ln].astype(bf16) for r in (q1r, k1r, vr, dor))
                q2v = q2r[:, ln].astype(bf16) if has2 else None
                s = _logits_t(k1v, q1v, k2v, q2v, ckr[e][:, 0:1] if hasb else None, scale, j * tk, i * tq)
                p = jnp.exp(s - lser[e, 0])
                dp = _dg(vv, dov, 1, 1)
                dsr = p * (dp - dlr[e, 0])
                ds = (dsr * scale).astype(bf16)
                av[:, ln] += _dg(p, dov, 1, 0)
                ak[:, ln] += _dg(ds, q1v, 1, 0)
                dq1r[rows, ln] += _dg(ds, k1v, 0, 0)
                if has2:
                    dq2r[rows, ln] += _dg(ds, k2v, 0, 0)
                    dk2r[cols, :] += _dg(ds, q2v, 1, 0)
                if hasb:
                    drowr[e, pl.ds(i, 1), :] += jnp.sum(dsr, axis=0, keepdims=True)
                    ac[e] -= jnp.broadcast_to(jnp.sum(dsr, axis=1, keepdims=True), (tk, LANE))

        @pl.when(i == nq - 1)
        def _():
            dk1r[...] = ak[...]
            dvr[...] = av[...]
            if hasb:
                for e in range(hp):
                    dckr[e] = _row_sums_as_row(ac[e], first_lane_only=True)

    def qmap(off):
        return lambda h, j, i: (jnp.maximum(i, j), off // hp + h)

    def kmap(off):
        return lambda h, j, i: (j, off // hp + h)

    in_specs = [pl.BlockSpec((tq, wd), qmap(q1o)), pl.BlockSpec((tk, wd), kmap(k1o)),
                pl.BlockSpec((tk, wd), kmap(vo))]
    args = [q1, k1, v]
    if has2:
        in_specs += [pl.BlockSpec((tq, wd), qmap(0)), pl.BlockSpec((tk, LANE), lambda h, j, i: (j, 0))]
        args += [q2, k2]
    if hasb:
        in_specs += [pl.BlockSpec((hp, tk, LANE), lambda h, j, i: (h, j, 0))]
        args += [ck]
    row3 = pl.BlockSpec((hp, 1, 1, tq), lambda h, j, i: (h, jnp.maximum(i, j), 0, 0))
    in_specs += [pl.BlockSpec((tq, wd), qmap(0)), row3, row3]
    args += [do, lse, delta]
    hd = jax.ShapeDtypeStruct((t, heads * LANE), f32)
    head_cols = pl.BlockSpec((t, wd), lambda h, j, i: (0, h))
    kv_blk = pl.BlockSpec((tk, wd), kmap(0))
    out_specs, out_shape = [head_cols, kv_blk, kv_blk], [hd, hd, hd]
    scratch = [pltpu.VMEM((tk, wd), f32)] * 2
    if has2:
        out_specs += [head_cols, pl.BlockSpec((t, LANE), lambda h, j, i: (0, 0))]
        out_shape += [hd, jax.ShapeDtypeStruct((t, LANE), f32)]
    if hasb:
        out_specs += [pl.BlockSpec((hp, nq, tq), lambda h, j, i: (h, 0, 0)),
                      pl.BlockSpec((hp, 1, tk), lambda h, j, i: (h, 0, j))]
        out_shape += [jax.ShapeDtypeStruct((heads, nq, tq), f32), jax.ShapeDtypeStruct((heads, 1, t), f32)]
        scratch.append(pltpu.VMEM((hp, tk, LANE), f32))
    return _hosted_call(body, side, name=name, grid=(heads // hp, nq, nq), in_specs=in_specs, out_specs=out_specs,
                        out_shape=out_shape, scratch_shapes=scratch, args=args)


def _cumsum_rows(x, *, reverse, name):
    t, w = x.shape
    blk = 128
    nb = t // blk

    def body(x_ref, o_ref):
        r = lax.broadcasted_iota(jnp.int32, (blk, blk), 0)
        c = lax.broadcasted_iota(jnp.int32, (blk, blk), 1)
        tri = ((r <= c) if reverse else (r >= c)).astype(f32)
        carry = jnp.zeros((1, w), f32)
        for b in (reversed(range(nb)) if reverse else range(nb)):
            seg = x_ref[b * blk:(b + 1) * blk, :]
            cs = jnp.dot(tri, seg, precision=HIGHEST, preferred_element_type=f32) + carry
            o_ref[b * blk:(b + 1) * blk, :] = cs
            carry = cs[0:1, :] if reverse else cs[blk - 1:blk, :]

    return pl.pallas_call(body, name=name, out_shape=jax.ShapeDtypeStruct((t, w), f32))(x)


CONV_PAD = 8


def _conv_fn(x, w, b):
    t, taps = x.shape[0], w.shape[0]
    xx = jnp.concatenate([jnp.zeros((CONV_PAD, x.shape[1]), f32), x], axis=0)
    y = b
    for k in range(taps):
        off = CONV_PAD - (taps - 1) + k
        y = y + w[k:k + 1, :] * xx[off:off + t, :]
    return _silu(y)


def _conv_fwd(x, w, b, *, name):
    t, c = x.shape
    tc = LANE
    taps = w.shape[0]

    def body(x_ref, w_ref, b_ref, o_ref):
        o_ref[...] = _conv_fn(x_ref[...], w_ref[...], b_ref[...])

    xs = pl.BlockSpec((t, tc), lambda i: (0, i))
    return pl.pallas_call(
        body, name=name, grid=(c // tc,),
        in_specs=[xs, pl.BlockSpec((taps, tc), lambda i: (0, i)), pl.BlockSpec((1, tc), lambda i: (0, i))],
        out_specs=xs, out_shape=jax.ShapeDtypeStruct((t, c), f32),
        compiler_params=pltpu.CompilerParams(dimension_semantics=("parallel",)),
    )(x, w, b)


def _conv_bwd(x, w, b, ds, *, name):
    t, c = x.shape
    tc = LANE
    taps = w.shape[0]

    def body(x_ref, w_ref, b_ref, ds_ref, dx_ref, dw_ref, db_ref):
        _, vjp = jax.vjp(_conv_fn, x_ref[...], w_ref[...], b_ref[...])
        dx, dw, db = vjp(ds_ref[...])
        dx_ref[...] = dx
        dw_ref[...] = dw
        db_ref[...] = db

    xs = pl.BlockSpec((t, tc), lambda i: (0, i))
    ws = pl.BlockSpec((taps, tc), lambda i: (0, i))
    bs = pl.BlockSpec((1, tc), lambda i: (0, i))
    return pl.pallas_call(
        body, name=name, grid=(c // tc,), in_specs=[xs, ws, bs, xs], out_specs=[xs, ws, bs],
        out_shape=[jax.ShapeDtypeStruct((t, c), f32), jax.ShapeDtypeStruct((taps, c), f32),
                   jax.ShapeDtypeStruct((1, c), f32)],
        compiler_params=pltpu.CompilerParams(dimension_semantics=("parallel",)),
    )(x, w, b, ds)


def _ssd_chunk(x, bm, cm, dtr, st, dtb, alog, dsk, *, e_heads):
    ln, p = x.shape[0], SSD_HEADDIM
    dt = _softplus(dtr + dtb)
    da = dt * (-jnp.exp(alog))
    r = lax.broadcasted_iota(jnp.int32, (ln, ln), 0)
    c = lax.broadcasted_iota(jnp.int32, (ln, ln), 1)
    lower = r >= c
    acs = jnp.dot(lower.astype(f32), da, precision=HIGHEST, preferred_element_type=f32)
    acs_t = acs.T

    def per_head(v):
        return jnp.concatenate([jnp.broadcast_to(v[:, e:e + 1], (v.shape[0], p)) for e in range(e_heads)], axis=1)

    dt_x, acs_x = per_head(dt), per_head(acs)
    last_x = acs_x[ln - 1:ln, :]
    xdt = x * dt_x
    cb = bdot(cm, bm, 1, 1)
    y_off = bdot(cm, st, 1, 0) * jnp.exp(acs_x)
    st_new = st * jnp.exp(last_x) + bdot(bm, xdt * jnp.exp(last_x - acs_x), 0, 0)
    ys = []
    for e in range(e_heads):
        seg = acs[:, e:e + 1] - acs_t[e:e + 1, :]
        dec = jnp.exp(jnp.where(lower, seg, -jnp.inf))
        ys.append(bdot(cb * dec, xdt[:, e * p:(e + 1) * p], 1, 0))
    y = jnp.concatenate(ys, axis=1) + y_off + per_head(dsk) * x
    return y, st_new


def _ssd_specs(t, d_inner, groups):
    ln, n = SSD_CHUNK, SSD_STATE
    gw = d_inner // groups
    nc = t // ln
    return ln, n, gw, nc


def _ssd_fwd(xbc, dtp, dtb, alog, dsk, *, d_inner, groups, name, side=None):
    t = xbc.shape[0]
    ln, n, gw, nc = _ssd_specs(t, d_inner, groups)
    e_heads = gw // SSD_HEADDIM
    gp = SSD_GROUPS_PER_STEP
    assert groups % gp == 0 and (d_inner // n) % gp == 0
    boff = d_inner // n // gp
    ng = groups // gp

    def body(x_ref, b_ref, c_ref, dt_ref, dtb_ref, alog_ref, dsk_ref, y_ref, save_ref, st_scr):
        c, g = pl.program_id(0), pl.program_id(1)
        for e in range(gp):
            gi = g * gp + e

            @pl.when(c == 0)
            def _():
                st_scr[gi] = jnp.zeros((n, gw), f32)

            st = st_scr[gi]
            save_ref[e] = st
            y, st_new = _ssd_chunk(x_ref[:, e * gw:(e + 1) * gw], b_ref[:, e * n:(e + 1) * n],
                                   c_ref[:, e * n:(e + 1) * n], dt_ref[:, e * LANE:(e + 1) * LANE], st,
                                   dtb_ref[e], alog_ref[e], dsk_ref[e], e_heads=e_heads)
            y_ref[:, e * gw:(e + 1) * gw] = y
            st_scr[gi] = st_new

    par = pl.BlockSpec((gp, 1, LANE), lambda c, g: (g, 0, 0))
    return _hosted_call(
        body, side, name=name, grid=(nc, ng),
        in_specs=[pl.BlockSpec((ln, gp * gw), lambda c, g: (c, g)),
                  pl.BlockSpec((ln, gp * n), lambda c, g: (c, boff + g)),
                  pl.BlockSpec((ln, gp * n), lambda c, g: (c, boff + ng + g)),
                  pl.BlockSpec((ln, gp * LANE), lambda c, g: (c, g)), par, par, par],
        out_specs=[pl.BlockSpec((ln, gp * gw), lambda c, g: (c, g)),
                   pl.BlockSpec((gp, n, gw), lambda c, g: (c * ng + g, 0, 0))],
        out_shape=[jax.ShapeDtypeStruct((t, d_inner), f32),
                   jax.ShapeDtypeStruct((nc * groups, n, gw), f32)],
        scratch_shapes=[pltpu.VMEM((groups, n, gw), f32)],
        args=(xbc, xbc, xbc, dtp, dtb, alog, dsk))


def _ssd_bwd(xbc, dtp, dtb, alog, dsk, saved, dy, *, d_inner, groups, name, side=None):
    t = xbc.shape[0]
    ln, n, gw, nc = _ssd_specs(t, d_inner, groups)
    e_heads = gw // SSD_HEADDIM
    gp = SSD_GROUPS_PER_STEP
    assert groups % gp == 0 and (d_inner // n) % gp == 0
    boff = d_inner // n // gp
    ng = groups // gp

    def body(x_ref, b_ref, c_ref, dt_ref, dtb_ref, alog_ref, dsk_ref, save_ref, dy_ref,
             dx_ref, db_ref, dc_ref, ddt_ref, dpar_ref, dst_scr):
        c, g = pl.program_id(0), pl.program_id(1)

        @pl.when((c == 0) & (g == 0))
        def _():
            dpar_ref[...] = jnp.zeros_like(dpar_ref)

        fn = functools.partial(_ssd_chunk, e_heads=e_heads)
        for e in range(gp):
            gi = g * gp + e
            xl, nl, dl = (slice(e * w, (e + 1) * w) for w in (gw, n, LANE))

            @pl.when(c == 0)
            def _():
                dst_scr[gi] = jnp.zeros((n, gw), f32)

            _, vjp = jax.vjp(fn, x_ref[:, xl], b_ref[:, nl], c_ref[:, nl], dt_ref[:, dl], save_ref[e],
                             dtb_ref[e], alog_ref[e], dsk_ref[e])
            gx, gb, gc, gdt, gst, gdtb, galog, gdsk = vjp((dy_ref[:, xl], dst_scr[gi]))
            dx_ref[:, xl] = gx
            db_ref[:, nl] = gb
            dc_ref[:, nl] = gc
            ddt_ref[:, dl] = gdt
            dst_scr[gi] = gst
            dpar_ref[gi, 0:1, :] += gdtb
            dpar_ref[gi, 1:2, :] += galog
            dpar_ref[gi, 2:3, :] += gdsk

    def rc(c):
        return nc - 1 - c

    par = pl.BlockSpec((gp, 1, LANE), lambda c, g: (g, 0, 0))
    xs = pl.BlockSpec((ln, gp * gw), lambda c, g: (rc(c), g))
    ns = pl.BlockSpec((ln, gp * n), lambda c, g: (rc(c), g))
    return _hosted_call(
        body, side, name=name, grid=(nc, ng),
        in_specs=[xs,
                  pl.BlockSpec((ln, gp * n), lambda c, g: (rc(c), boff + g)),
                  pl.BlockSpec((ln, gp * n), lambda c, g: (rc(c), boff + ng + g)),
                  pl.BlockSpec((ln, gp * LANE), lambda c, g: (rc(c), g)), par, par, par,
                  pl.BlockSpec((gp, n, gw), lambda c, g: (rc(c) * ng + g, 0, 0)), xs],
        out_specs=[xs, ns, ns, pl.BlockSpec((ln, gp * LANE), lambda c, g: (rc(c), g)),
                   pl.BlockSpec((groups, 8, LANE), lambda c, g: (0, 0, 0))],
        out_shape=[jax.ShapeDtypeStruct((t, d_inner), f32),
                   jax.ShapeDtypeStruct((t, groups * n), f32),
                   jax.ShapeDtypeStruct((t, groups * n), f32),
                   jax.ShapeDtypeStruct((t, groups * LANE), f32),
                   jax.ShapeDtypeStruct((groups, 8, LANE), f32)],
        scratch_shapes=[pltpu.VMEM((groups, n, gw), f32)],
        args=(xbc, xbc, xbc, dtp, dtb, alog, dsk, saved, dy))


S5_TOP = 8


def _cmul(ar, ai, br, bi):
    return ar * br - ai * bi, ar * bi + ai * br


def _s5_scan(scr, base, ln, pw_ref, sp, *, reverse):
    s, k = 1, 0
    while s < ln:
        pr = pw_ref[0, k:k + 1, :sp]
        pi = pw_ref[0, k:k + 1, sp:]
        if reverse:
            pi = -pi
            src, dst = base + s, base
        else:
            src, dst = base, base + s
        ar, ai = scr[src:src + ln - s, :sp], scr[src:src + ln - s, sp:]
        mr, mi = _cmul(ar, ai, pr, pi)
        scr[dst:dst + ln - s, :sp] = scr[dst:dst + ln - s, :sp] + mr
        scr[dst:dst + ln - s, sp:] = scr[dst:dst + ln - s, sp:] + mi
        s *= 2
        k += 1


def _s5_states(u_ref, b_ref, lam_ref, pw_ref, scr, carry_r, carry_i, ln, sp):
    scr[S5_TOP:S5_TOP + ln, :] = _dg(u_ref[...], b_ref[0], 1, 0)
    mr, mi = _cmul(carry_r, carry_i, lam_ref[0][:, :sp], lam_ref[0][:, sp:])
    scr[S5_TOP:S5_TOP + 1, :sp] = scr[S5_TOP:S5_TOP + 1, :sp] + mr
    scr[S5_TOP:S5_TOP + 1, sp:] = scr[S5_TOP:S5_TOP + 1, sp:] + mi
    _s5_scan(scr, S5_TOP, ln, pw_ref, sp, reverse=False)


def _s5_dims(t, u_width):
    cbw = S5_BLOCK_GROUPS * S5_GROUP
    return S5_CHUNK, cbw, u_width // cbw, t // S5_CHUNK


def _s5_fwd(u, lamv, pw, bmat, cmat, *, name):
    t, w = u.shape
    ln, cbw, nb, nc = _s5_dims(t, w)
    sp2 = lamv.shape[2]
    sp = sp2 // 2
    bp = S5_BLOCKS_PER_STEP
    assert nb % bp == 0

    def one_block(u_ref, lam_ref, pw_ref, b_ref, c_ref, ys_ref, xp_ref, st_ref, scr, carry):
        @pl.when(pl.program_id(1) == 0)
        def _():
            carry[...] = jnp.zeros_like(carry)

        xp_ref[0] = carry[0:1, :]
        _s5_states(u_ref, b_ref, lam_ref, pw_ref, scr, carry[0:1, :sp], carry[0:1, sp:], ln, sp)
        carry[0:1, :] = scr[S5_TOP + ln - 1:S5_TOP + ln, :]
        states = scr[S5_TOP:S5_TOP + ln, :]
        st_ref[0] = states
        ys_ref[...] = _dg(states, c_ref[0], 1, 0)

    def body(u_ref, lam_ref, pw_ref, b_ref, c_ref, ys_ref, xp_ref, st_ref, scr, carry):
        for e in range(bp):
            one, cols = pl.ds(e, 1), pl.ds(e * cbw, cbw)
            one_block(u_ref.at[:, cols], lam_ref.at[one], pw_ref.at[one], b_ref.at[one], c_ref.at[one],
                      ys_ref.at[:, cols], xp_ref.at[0, one], st_ref.at[one], scr.at[e], carry.at[e])

    def blk(shape):
        return pl.BlockSpec((bp,) + shape, lambda j, c: (j, 0, 0))

    return pl.pallas_call(
        body, name=name, grid=(nb // bp, nc),
        in_specs=[pl.BlockSpec((ln, bp * cbw), lambda j, c: (c, j)), blk((1, sp2)), blk((8, sp2)),
                  blk((cbw, sp2)), blk((sp2, cbw))],
        out_specs=[pl.BlockSpec((ln, bp * cbw), lambda j, c: (c, j)),
                   pl.BlockSpec((1, bp, 1, sp2), lambda j, c: (c, j, 0, 0)),
                   pl.BlockSpec((bp, ln, sp2), lambda j, c: (j, c, 0))],
        out_shape=[jax.ShapeDtypeStruct((t, w), f32), jax.ShapeDtypeStruct((nc, nb, 1, sp2), f32),
                   jax.ShapeDtypeStruct((nb, t, sp2), f32)],
        scratch_shapes=[pltpu.VMEM((bp, S5_TOP + ln, sp2), f32), pltpu.VMEM((bp, 8, sp2), f32)],
        compiler_params=pltpu.CompilerParams(dimension_semantics=("parallel", "arbitrary")),
    )(u, lamv, pw, bmat, cmat)


def _s5_bwd(u, dy, du_skip, xp, states, lamv, pw, bmat, cmat, *, name):
    t, w = u.shape
    ln, cbw, nb, nc = _s5_dims(t, w)
    sp2 = lamv.shape[2]
    sp = sp2 // 2
    bp = S5_BLOCKS_PER_STEP
    assert nb % bp == 0

    def body(u_ref, dy_ref, dus_ref, xp_ref, st_ref, lam_ref, pw_ref, b_ref, c_ref,
             du_ref, db_ref, dc_ref, dl_ref, scr, gsc, gcarry):
        for e in range(bp):
            one, cols = pl.ds(e, 1), pl.ds(e * cbw, cbw)
            one_block(u_ref.at[:, cols], dy_ref.at[:, cols], dus_ref.at[:, cols], xp_ref.at[0, one],
                      st_ref.at[one], lam_ref.at[one], pw_ref.at[one], b_ref.at[one], c_ref.at[one],
                      du_ref.at[:, cols], db_ref.at[one], dc_ref.at[one], dl_ref.at[one],
                      scr.at[e], gsc.at[e], gcarry.at[e])

    def one_block(u_ref, dy_ref, dus_ref, xp_ref, st_ref, lam_ref, pw_ref, b_ref, c_ref,
                  du_ref, db_ref, dc_ref, dl_ref, scr, gsc, gcarry):
        @pl.when(pl.program_id(1) == 0)
        def _():
            gcarry[...] = jnp.zeros_like(gcarry)
            db_ref[...] = jnp.zeros_like(db_ref)
            dc_ref[...] = jnp.zeros_like(dc_ref)
            dl_ref[...] = jnp.zeros_like(dl_ref)

        lr, li = lam_ref[0][:, :sp], lam_ref[0][:, sp:]
        scr[S5_TOP - 1:S5_TOP, :] = xp_ref[0]
        scr[S5_TOP:S5_TOP + ln, :] = st_ref[0]
        dy = dy_ref[...]
        gsc[0:ln, :] = _dg(dy, c_ref[0], 1, 1)
        mr, mi = _cmul(gcarry[0:1, :sp], gcarry[0:1, sp:], lr, -li)
        gsc[ln - 1:ln, :sp] = gsc[ln - 1:ln, :sp] + mr
        gsc[ln - 1:ln, sp:] = gsc[ln - 1:ln, sp:] + mi
        _s5_scan(gsc, 0, ln, pw_ref, sp, reverse=True)
        gcarry[0:1, :] = gsc[0:1, :]
        g = gsc[0:ln, :]
        du_ref[...] = dus_ref[...] + _dg(g, b_ref[0], 1, 1)
        db_ref[0] += _dg(u_ref[...], g, 0, 0)
        dc_ref[0] += _dg(scr[S5_TOP:S5_TOP + ln, :], dy, 0, 0)
        pr, pi = scr[S5_TOP - 1:S5_TOP - 1 + ln, :sp], scr[S5_TOP - 1:S5_TOP - 1 + ln, sp:]
        gr, gi = g[:, :sp], g[:, sp:]
        dl_ref[0, 0:1, :sp] += jnp.sum(pr * gr + pi * gi, axis=0, keepdims=True)
        dl_ref[0, 0:1, sp:] += jnp.sum(pr * gi - pi * gr, axis=0, keepdims=True)

    def rc(c):
        return nc - 1 - c

    def blk(shape):
        return pl.BlockSpec((bp,) + shape, lambda j, c: (j, 0, 0))

    row = pl.BlockSpec((ln, bp * cbw), lambda j, c: (rc(c), j))
    return pl.pallas_call(
        body, name=name, grid=(nb // bp, nc),
        in_specs=[row, row, row, pl.BlockSpec((1, bp, 1, sp2), lambda j, c: (rc(c), j, 0, 0)),
                  pl.BlockSpec((bp, ln, sp2), lambda j, c: (j, rc(c), 0)),
                  blk((1, sp2)), blk((8, sp2)), blk((cbw, sp2)), blk((sp2, cbw))],
        out_specs=[row, blk((cbw, sp2)), blk((sp2, cbw)), blk((8, sp2))],
        out_shape=[jax.ShapeDtypeStruct((t, w), f32), jax.ShapeDtypeStruct((nb, cbw, sp2), f32),
                   jax.ShapeDtypeStruct((nb, sp2, cbw), f32), jax.ShapeDtypeStruct((nb, 8, sp2), f32)],
        scratch_shapes=[pltpu.VMEM((bp, S5_TOP + ln, sp2), f32), pltpu.VMEM((bp, ln, sp2), f32),
                        pltpu.VMEM((bp, 8, sp2), f32)],
        compiler_params=pltpu.CompilerParams(dimension_semantics=("parallel", "arbitrary")),
    )(u, dy, du_skip, xp, states, lamv, pw, bmat, cmat)


def _s5_discretize(lre, lim, log_step, bre, bim, cre, cim):
    g, p = lre.shape
    gb = S5_BLOCK_GROUPS
    nb = g // gb
    lam = lax.complex(lre, lim)
    lam_bar = jnp.exp(lam * jnp.exp(log_step)[:, None])
    b_bar = ((lam_bar - 1.0) / lam)[..., None] * lax.complex(bre, bim)
    lb = lam_bar.reshape(nb, 1, gb * p)
    lamv = jnp.concatenate([jnp.real(lb), jnp.imag(lb)], axis=-1)
    eye = jnp.eye(gb, dtype=f32)
    bb = b_bar.reshape(nb, gb, p, S5_GROUP)

    def bdiag(v):
        return jnp.einsum('jgpi,gh->jgihp', v, eye).reshape(nb, gb * S5_GROUP, gb * p)

    bmat = jnp.concatenate([bdiag(jnp.real(bb)), bdiag(jnp.imag(bb))], axis=-1)

    def cdiag(v):
        return jnp.einsum('jgip,gh->jhpgi', v, eye).reshape(nb, gb * p, gb * S5_GROUP)

    cmat = jnp.concatenate([cdiag(cre.reshape(nb, gb, S5_GROUP, p)),
                            -cdiag(cim.reshape(nb, gb, S5_GROUP, p))], axis=1)
    return lamv, bmat, cmat


def _s5_powers(lamv):
    sp = lamv.shape[2] // 2
    pr, pi = lamv[:, :, :sp], lamv[:, :, sp:]
    rows = []
    for _ in range(8):
        rows.append(jnp.concatenate([pr, pi], axis=-1))
        pr, pi = pr * pr - pi * pi, 2.0 * pr * pi
    return lax.stop_gradient(jnp.concatenate(rows, axis=1))


def _ln_fwd(h, out, p, tag):
    return _rw_fwd(_ln_fn, [h, out], [p["ln_g"], p["ln_b"]], [h.shape[1]], name=tag + "_ln")[0]


def _ln_bwd(h, out, p, dh2, tag):
    return _rw_bwd(_ln_fn, [h, out], [p["ln_g"], p["ln_b"]], [dh2], name=tag + "_ln_bwd")


def _ssd_layer_fwd(h, p, side=None):
    di, groups = p["wz"].shape[1], p["dtb"].shape[0]
    z = _mm(h, p["wz"], name="ssd_z")
    xr = _mm(h, p["wxbc"], name="ssd_xbc")
    dtp = _mm(h, p["wdt"], name="ssd_dt")
    xbc = _conv_fwd(xr, p["conv_w"], p["conv_b"], name="ssd_conv")
    (y, states), got = _ssd_fwd(xbc, dtp, p["dtb"], p["alog"], p["dsk"], d_inner=di, groups=groups,
                                name="ssd_scan", side=side)
    yn = _rw_fwd(_make_ssd_post_fn(groups), [y, z], [p["norm_w"]], [di], name="ssd_post")[0]
    out = _mm(yn, p["wout"], name="ssd_out")
    return _ln_fwd(h, out, p, "ssd"), (h, z, xr, dtp, xbc, states, y, yn, out), got


def _ssd_layer_bwd(saved, p, dh2, side=None, in_sides=None):
    h, z, xr, dtp, xbc, states, y, yn, out = saved
    di, groups = p["wz"].shape[1], p["dtb"].shape[0]
    dh, dout, dg, db = _ln_bwd(h, out, p, dh2, "ssd")
    dyn = _mm(dout, p["wout"], tb=True, name="ssd_out_dx")
    g = {"ln_g": dg, "ln_b": db, "wout": _mm(yn, dout, ta=True, out_dtype=bf16, name="ssd_out_dw")}
    if callable(side):
        side = side(g["wout"])
    dy, dz, g["norm_w"] = _rw_bwd(_make_ssd_post_fn(groups), [y, z], [p["norm_w"]], [dyn], name="ssd_post_bwd")
    (dx, dbm, dcm, ddt, dpar), got = _ssd_bwd(xbc, dtp, p["dtb"], p["alog"], p["dsk"], states, dy,
                                              d_inner=di, groups=groups, name="ssd_scan_bwd", side=side)
    g["dtb"], g["alog"], g["dsk"] = dpar[:, 0:1, :], dpar[:, 1:2, :], dpar[:, 2:3, :]
    dxr, g["conv_w"], g["conv_b"] = _conv_bwd(xr, p["conv_w"], p["conv_b"],
                                               jnp.concatenate([dx, dbm, dcm], axis=1), name="ssd_conv_bwd")
    g["wz"] = _mm(h, dz, ta=True, name="ssd_z_dw")
    g["wxbc"] = _mm(h, dxr, ta=True, name="ssd_xbc_dw")
    g["wdt"] = _mm(h, ddt, ta=True, name="ssd_dt_dw")
    if in_sides is None:
        dh = _mm(dxr, p["wxbc"], tb=True, add=dh, name="ssd_xbc_dx")
        dh = _mm(dz, p["wz"], tb=True, add=dh, name="ssd_z_dx")
        got_in = (None, None)
    else:
        side_a, side_b = in_sides(g)
        dh, got_a = _mm(dxr, p["wxbc"], tb=True, add=dh, name="ssd_xbc_dx", side=side_a)
        dh, got_b = _mm(dz, p["wz"], tb=True, add=dh, name="ssd_z_dx", side=side_b)
        got_in = (got_a, got_b)
    dh = _mm(ddt, p["wdt"], tb=True, add=dh, name="ssd_dt_dx")
    return dh, g, got, got_in


def _fox_layer_fwd(h, p, side=None):
    w = p["wout"].shape[0]
    heads = w // FOX_HEAD_DIM
    t = h.shape[0]
    qkvz = _mm(h, p["wqkvz"], name="fox_qkvz")
    fr = _mm(h, p["wf"], name="fox_f")
    logf = _rw_fwd(_logf_fn, [fr], [p["fb"]], [LANE], name="fox_logf")[0]
    cum = _cumsum_rows(logf, reverse=False, name="fox_cumsum")
    ck = _lane_bcast(cum[:, :heads].T)
    (o, lse), got = _attn_fwd(qkvz, 0, qkvz, heads, qkvz, 2 * heads, None, None, ck,
                              heads=heads, scale=FOX_HEAD_DIM ** -0.5, name="fox_attn", side=side)
    yg = _rw_fwd(_gate_fn, [o, (qkvz, w, 3)], [], [w], name="fox_gate")[0]
    out = _mm(yg, p["wout"], name="fox_out")
    return _ln_fwd(h, out, p, "fox"), (h, qkvz, fr, ck, o, lse, yg, out), got


def _fox_layer_bwd(saved, p, dh2, side=None):
    h, qkvz, fr, ck, o, lse, yg, out = saved
    w = p["wout"].shape[0]
    heads = w // FOX_HEAD_DIM
    t = h.shape[0]
    dh, dout, dg, db = _ln_bwd(h, out, p, dh2, "fox")
    dyg = _mm(dout, p["wout"], tb=True, name="fox_out_dx")
    g = {"ln_g": dg, "ln_b": db, "wout": _mm(yg, dout, ta=True, out_dtype=bf16, name="fox_out_dw")}
    do, dz = _rw_bwd(_gate_fn, [o, (qkvz, w, 3)], [], [dyg], name="fox_gate_bwd")
    delta = _attn_delta(do, o, heads=heads, name="fox_delta")
    (dq, dk, dv, drow, dck), got = _attn_bwd(qkvz, 0, qkvz, heads, qkvz, 2 * heads, None, None, ck, do, lse, delta,
                                             heads=heads, scale=FOX_HEAD_DIM ** -0.5, name="fox_attn_bwd", side=side)
    dqkvz = jnp.concatenate([dq, dk, dv, dz], axis=1)
    dcum = jnp.pad((drow.reshape(heads, t) + dck.reshape(heads, t)).T, ((0, 0), (0, LANE - heads)))
    dlogf = _cumsum_rows(dcum, reverse=True, name="fox_cumsum_bwd")
    dfr, g["fb"] = _rw_bwd(_logf_fn, [fr], [p["fb"]], [dlogf], name="fox_logf_bwd")
    dh = _mm(dqkvz, p["wqkvz"], tb=True, add=dh, name="fox_qkvz_dx")
    dh = _mm(dfr, p["wf"], tb=True, add=dh, name="fox_f_dx")
    g["wqkvz"] = _mm(h, dqkvz, ta=True, name="fox_qkvz_dw")
    g["wf"] = _mm(h, dfr, ta=True, name="fox_f_dw")
    return dh, g, got


def _mla_layer_fwd(h, p, cos, sin):
    rank = p["qn"].shape[1]
    w = p["wout"].shape[0]
    heads = w // MLA_V
    lat = _mm(h, p["wlat"], name="mla_lat")
    kpr = _mm(h, p["wkpe"], name="mla_kpe")
    z = _mm(h, p["wz"], name="mla_z")
    qnl, kvnl = _rw_fwd(_make_mla_norm_fn(rank), [lat], [p["qn"], p["kvn"]], [rank, rank], name="mla_norm")
    qno = _mm(qnl, p["wqn"], name="mla_qn")
    qpr = _mm(qnl, p["wqp"], name="mla_qp")
    kno = _mm(kvnl, p["wkn"], name="mla_kn")
    v = _mm(kvnl, p["wv"], name="mla_v")
    qp = _rw_fwd(_make_rope_fn(heads), [qpr, cos, sin], [], [heads * LANE], name="mla_rope_q")[0]
    kp = _rw_fwd(_make_rope_fn(1), [kpr, cos, sin], [], [LANE], name="mla_rope_k")[0]
    scale = (MLA_NOPE + MLA_ROPE) ** -0.5
    (o, lse), _ = _attn_fwd(qno, 0, kno, 0, v, 0, qp, kp, None, heads=heads, scale=scale, name="mla_attn")
    yg = _rw_fwd(_gate_fn, [o, z], [], [w], name="mla_gate")[0]
    out = _mm(yg, p["wout"], name="mla_out")
    return _ln_fwd(h, out, p, "mla"), (h, lat, kpr, z, qnl, kvnl, qno, qpr, kno, v, qp, kp, o, lse, yg, out)


def _mla_layer_bwd(saved, p, dh2, cos, sin, side=None):
    h, lat, kpr, z, qnl, kvnl, qno, qpr, kno, v, qp, kp, o, lse, yg, out = saved
    rank = p["qn"].shape[1]
    w = p["wout"].shape[0]
    heads = w // MLA_V
    dh, dout, dg, db = _ln_bwd(h, out, p, dh2, "mla")
    dyg = _mm(dout, p["wout"], tb=True, name="mla_out_dx")
    g = {"ln_g": dg, "ln_b": db, "wout": _mm(yg, dout, ta=True, out_dtype=bf16, name="mla_out_dw")}
    do, dz = _rw_bwd(_gate_fn, [o, z], [], [dyg], name="mla_gate_bwd")
    delta = _attn_delta(do, o, heads=heads, name="mla_delta")
    (dqno, dkno, dv, dqp, dkp), got = _attn_bwd(qno, 0, kno, 0, v, 0, qp, kp, None, do, lse, delta, heads=heads,
                                                scale=(MLA_NOPE + MLA_ROPE) ** -0.5, name="mla_attn_bwd", side=side)
    (dqpr,) = _rw_bwd(_make_rope_fn(heads), [qpr, cos, sin], [], [dqp], n_const=2, name="mla_rope_q_bwd")
    (dkpr,) = _rw_bwd(_make_rope_fn(1), [kpr, cos, sin], [], [dkp], n_const=2, name="mla_rope_k_bwd")
    dqnl = _mm(dqno, p["wqn"], tb=True, name="mla_qn_dx")
    dqnl = _mm(dqpr, p["wqp"], tb=True, add=dqnl, name="mla_qp_dx")
    dkvnl = _mm(dkno, p["wkn"], tb=True, name="mla_kn_dx")
    dkvnl = _mm(dv, p["wv"], tb=True, add=dkvnl, name="mla_v_dx")
    g["wqn"] = _mm(qnl, dqno, ta=True, name="mla_qn_dw")
    g["wqp"] = _mm(qnl, dqpr, ta=True, name="mla_qp_dw")
    g["wkn"] = _mm(kvnl, dkno, ta=True, name="mla_kn_dw")
    g["wv"] = _mm(kvnl, dv, ta=True, name="mla_v_dw")
    dlat, g["qn"], g["kvn"] = _rw_bwd(_make_mla_norm_fn(rank), [lat], [p["qn"], p["kvn"]], [dqnl, dkvnl],
                                     name="mla_norm_bwd")
    dh = _mm(dlat, p["wlat"], tb=True, add=dh, name="mla_lat_dx")
    dh = _mm(dkpr, p["wkpe"], tb=True, add=dh, name="mla_kpe_dx")
    dh = _mm(dz, p["wz"], tb=True, add=dh, name="mla_z_dx")
    g["wlat"] = _mm(h, dlat, ta=True, name="mla_lat_dw")
    g["wkpe"] = _mm(h, dkpr, ta=True, name="mla_kpe_dw")
    g["wz"] = _mm(h, dz, ta=True, name="mla_z_dw")
    return dh, g, got


def _s5_layer_fwd(h, p):
    w = p["wout"].shape[0]
    u = _mm(h, p["wu"], name="s5_u")
    z = _mm(h, p["wz"], name="s5_z")
    ys, xp, states = _s5_fwd(u, p["lamv"], p["pw"], p["bmat"], p["cmat"], name="s5_scan")
    y1 = _rw_fwd(_s5_act_fn, [ys, u], [p["d"]], [w], name="s5_act")[0]
    gp = _mm(y1, p["wglu"], name="s5_glu")
    y2 = _rw_fwd(_s5_glu_fn, [y1, gp, z], [p["bglu"]], [w], name="s5_gate")[0]
    out = _mm(y2, p["wout"], name="s5_out")
    return _ln_fwd(h, out, p, "s5"), (h, u, z, ys, xp, states, y1, gp, y2, out)


def _s5_layer_bwd(saved, p, dh2):
    h, u, z, ys, xp, states, y1, gp, y2, out = saved
    dh, dout, dg, db = _ln_bwd(h, out, p, dh2, "s5")
    dy2 = _mm(dout, p["wout"], tb=True, name="s5_out_dx")
    g = {"ln_g": dg, "ln_b": db, "wout": _mm(y2, dout, ta=True, out_dtype=bf16, name="s5_out_dw")}
    dy1, dgp, dz, g["bglu"] = _rw_bwd(_s5_glu_fn, [y1, gp, z], [p["bglu"]], [dy2], name="s5_gate_bwd")
    dy1 = _mm(dgp, p["wglu"], tb=True, add=dy1, name="s5_glu_dx")
    g["wglu"] = _mm(y1, dgp, ta=True, out_dtype=bf16, name="s5_glu_dw")
    dys, du, g["d"] = _rw_bwd(_s5_act_fn, [ys, u], [p["d"]], [dy1], name="s5_act_bwd")
    du, g["bmat"], g["cmat"], dlam = _s5_bwd(u, dys, du, xp, states, p["lamv"], p["pw"], p["bmat"], p["cmat"],
                                             name="s5_scan_bwd")
    g["lamv"] = dlam[:, 0:1, :]
    dh = _mm(du, p["wu"], tb=True, add=dh, name="s5_u_dx")
    dh = _mm(dz, p["wz"], tb=True, add=dh, name="s5_z_dx")
    g["wu"] = _mm(h, du, ta=True, name="s5_u_dw")
    g["wz"] = _mm(h, dz, ta=True, name="s5_z_dw")
    return dh, g


def _pad_last(a, to):
    return jnp.pad(a, [(0, 0)] * (a.ndim - 1) + [(0, to - a.shape[-1])])


def _row(v):
    return v.reshape(1, -1)


def _prep_ssd(w):
    di, cd, hh = w["ssd_w_out"].shape[0], w["ssd_conv_b"].shape[0], w["ssd_dt_bias"].shape[0]
    groups = (cd - di) // (2 * SSD_STATE)
    e = hh // groups

    def perm(v):
        lead = v.shape[:-1]
        return _pad_last(v.reshape(lead + (groups, e)), LANE).reshape(lead + (groups * LANE,))

    w_in = w["ssd_w_in"]
    return dict(
        wz=w_in[:, :di], wxbc=w_in[:, di:di + cd], wdt=perm(w_in[:, di + cd:]), wout=w["ssd_w_out"],
        conv_w=w["ssd_conv_w"], conv_b=_row(w["ssd_conv_b"]),
        dtb=perm(w["ssd_dt_bias"]).reshape(groups, 1, LANE), alog=perm(w["ssd_a_log"]).reshape(groups, 1, LANE),
        dsk=perm(w["ssd_d"]).reshape(groups, 1, LANE), norm_w=_row(w["ssd_norm_w"]),
        ln_g=_row(w["ln_g"][0]), ln_b=_row(w["ln_b"][0]))


def _prep_fox(w):
    wd = w["fox_w_out"].shape[0]
    w_in = w["fox_w_in"]
    return dict(wqkvz=w_in[:, :4 * wd], wf=_pad_last(w_in[:, 4 * wd:], LANE), wout=w["fox_w_out"],
                fb=_pad_last(_row(w["fox_f_bias"]), LANE), ln_g=_row(w["ln_g"][1]), ln_b=_row(w["ln_b"][1]))


def _prep_mla(w):
    rank = w["mla_q_norm"].shape[0]
    wd = w["mla_w_out"].shape[0]
    heads = wd // MLA_V
    w_in = w["mla_w_in"]
    qu = w["mla_w_q_up"].reshape(rank, heads, MLA_NOPE + MLA_ROPE)
    kvu = w["mla_w_kv_up"].reshape(w["mla_w_kv_up"].shape[0], heads, MLA_NOPE + MLA_V)
    return dict(
        wlat=w_in[:, :2 * rank], wkpe=_pad_last(w_in[:, 2 * rank:2 * rank + MLA_ROPE], LANE),
        wz=w_in[:, 2 * rank + MLA_ROPE:],
        wqn=qu[:, :, :MLA_NOPE].reshape(rank, heads * LANE),
        wqp=_pad_last(qu[:, :, MLA_NOPE:], LANE).reshape(rank, heads * LANE),
        wkn=kvu[:, :, :MLA_NOPE].reshape(-1, heads * LANE), wv=kvu[:, :, MLA_NOPE:].reshape(-1, heads * LANE),
        wout=w["mla_w_out"], qn=_row(w["mla_q_norm"]), kvn=_row(w["mla_kv_norm"]),
        ln_g=_row(w["ln_g"][2]), ln_b=_row(w["ln_b"][2]))


def _prep_s5(w):
    wd = w["s5_w_out"].shape[0]
    w_in = w["s5_w_in"]
    lamv, bmat, cmat = _s5_discretize(w["s5_lambda_re"], w["s5_lambda_im"], w["s5_log_step"],
                                      w["s5_b_re"], w["s5_b_im"], w["s5_c_re"], w["s5_c_im"])
    return dict(wu=w_in[:, :wd], wz=w_in[:, wd:], wglu=w["s5_w_glu"], wout=w["s5_w_out"],
                d=_row(w["s5_d"]), bglu=_row(w["s5_b_glu"]), lamv=lamv, bmat=bmat, cmat=cmat,
                ln_g=_row(w["ln_g"][3]), ln_b=_row(w["ln_b"][3]))


def _rope_tables(positions):
    inv_freq = ROPE_BASE ** (-jnp.arange(0, MLA_ROPE, 2, dtype=f32) / MLA_ROPE)
    ang = positions.astype(f32).reshape(-1, 1) * inv_freq
    return jnp.cos(ang), jnp.sin(ang)


def _heads3(v):
    return v.reshape(v.shape[0], -1, LANE)


def _flat2(v):
    return v.reshape(v.shape[0], -1)


def _natural_grads_ssd(g, e_heads):
    return {"ssd_w_in": jnp.concatenate([g["wz"], g["wxbc"], _flat2(_heads3(g["wdt"])[:, :, :e_heads])], axis=1),
            "ssd_w_out": g["wout"]}


def _natural_grads_fox(g, heads):
    return {"fox_w_in": jnp.concatenate([g["wqkvz"], g["wf"][:, :heads]], axis=1), "fox_w_out": g["wout"]}


def _natural_grads_mla(g):
    return {"mla_w_in": jnp.concatenate([g["wlat"], g["wkpe"][:, :MLA_ROPE], g["wz"]], axis=1),
            "mla_w_q_up": _flat2(jnp.concatenate([_heads3(g["wqn"]), _heads3(g["wqp"])[:, :, :MLA_ROPE]], axis=2)),
            "mla_w_kv_up": _flat2(jnp.concatenate([_heads3(g["wkn"]), _heads3(g["wv"])], axis=2)),
            "mla_w_out": g["wout"]}


def _natural_grads_s5(g):
    return {"s5_w_in": jnp.concatenate([g["wu"], g["wz"]], axis=1), "s5_w_glu": g["wglu"], "s5_w_out": g["wout"]}


def _sequence_step(x, positions, tgt, small, big_of_layer, gather_sides=(None, None), exchange_side=None,
                   ssd_in_sides=None):
    cos, sin = _rope_tables(positions)
    exchange_side = exchange_side or (lambda nat: None)
    vjps = []

    def prepared(prep, big):
        p, vjp = jax.vjp(lambda s: prep({**big, **s}), small)
        vjps.append(vjp)
        return p

    p0 = prepared(_prep_ssd, big_of_layer[0](None))
    h1, s0, got1 = _ssd_layer_fwd(x, p0, side=gather_sides[0])
    p1 = prepared(_prep_fox, big_of_layer[1](got1))
    h2, s1, got23 = _fox_layer_fwd(h1, p1, side=gather_sides[1])
    p2 = prepared(_prep_mla, big_of_layer[2](got23))
    p3 = prepared(_prep_s5, big_of_layer[3](got23))
    p3["pw"] = _s5_powers(p3["lamv"])
    h3, s2 = _mla_layer_fwd(h2, p2, cos, sin)
    h4, s3 = _s5_layer_fwd(h3, p3)
    dh, loss = _loss_head(h4, tgt)
    dh, g3 = _s5_layer_bwd(s3, p3, dh)
    nat3 = _natural_grads_s5(g3)
    dh, g2, parts3 = _mla_layer_bwd(s2, p2, dh, cos, sin, side=exchange_side(nat3))
    nat2 = _natural_grads_mla(g2)
    dh, g1, parts2 = _fox_layer_bwd(s1, p1, dh, side=exchange_side(nat2))
    nat1 = _natural_grads_fox(g1, small["fox_f_bias"].shape[0])
    e_heads = small["ssd_dt_bias"].shape[0] // p0["dtb"].shape[0]
    dh, g0, parts1, parts_in = _ssd_layer_bwd(
        s0, p0, dh, side=lambda gw: exchange_side({**nat1, "ssd_w_out": gw}),
        in_sides=None if ssd_in_sides is None else
        lambda g: ssd_in_sides(_natural_grads_ssd(g, e_heads)["ssd_w_in"]))
    nat0 = _natural_grads_ssd(g0, e_heads)
    g_small = None
    for p, vjp, g in zip((p0, p1, p2, p3), vjps, (g0, g1, g2, g3)):
        (gi,) = vjp({k: g[k].astype(p[k].dtype) for k in p if k != "pw"})
        g_small = gi if g_small is None else jax.tree.map(jnp.add, g_small, gi)
    return loss, dh, g_small, (nat0, nat1, nat2, nat3), (parts1, parts2, parts3, parts_in)


FLAT_COLS = 1024
FLAT_ROW_ALIGN = 128


def _pack(arrs):
    flat = jnp.concatenate([a.reshape(-1) for a in arrs])
    unit = FLAT_COLS * FLAT_ROW_ALIGN
    return jnp.pad(flat, (0, -flat.shape[0] % unit)).reshape(-1, FLAT_COLS)


def _unpack(flat2d, shapes):
    flat = flat2d.reshape(-1)
    out, off = [], 0
    for s in shapes:
        sz = math.prod(s)
        out.append(flat[off:off + sz].reshape(s))
        off += sz
    return out


def _unpack_gathered(gathered, shapes, axes):
    flat = gathered.reshape(N_DEV, -1)
    out, off = [], 0
    for s, ax in zip(shapes, axes):
        sz = math.prod(s)
        seg = flat[:, off:off + sz].reshape((N_DEV,) + tuple(s))
        out.append(jnp.concatenate([seg[d] for d in range(N_DEV)], axis=ax)[0])
        off += sz
    return out


MESH = pl.DeviceIdType.MESH
HBM_SPEC = pl.BlockSpec(memory_space=pl.ANY)


def _comm_scratch(n):
    return [pltpu.SemaphoreType.DMA((n, 7)), pltpu.SemaphoreType.DMA((n, 7)), pltpu.SemaphoreType.DMA((n,))]


class _Gather:
    def __init__(self, xs):
        self.ins = list(xs)
        self.out_shape = [jax.ShapeDtypeStruct((N_DEV,) + v.shape, v.dtype) for v in xs]

    def _plan(self, x_refs, out_refs, sems):
        send_sems, recv_sems, local_sems = sems
        x, y, cc = lax.axis_index("x"), lax.axis_index("y"), lax.axis_index("c")
        me, sibling = (x, y, cc), (x, y, 1 - cc)
        chips = [(1 - x, y), (x, 1 - y), (1 - x, 1 - y)]

        def rows(w, px, py, pc):
            return out_refs[w].at[4 * px + 2 * py + pc]

        def copy(w, k, block, to, src=None):
            return pltpu.make_async_remote_copy(
                src_ref=rows(w, *block) if src is None else src, dst_ref=rows(w, *block),
                send_sem=send_sems.at[w, k], recv_sem=recv_sems.at[w, k], device_id=to, device_id_type=MESH)

        n = len(x_refs)
        mine = [pltpu.make_async_copy(x_refs[w], rows(w, *me), local_sems.at[w]) for w in range(n)]
        first = []
        for w in range(n):
            first.append(copy(w, 0, me, sibling, src=x_refs[w]))
            first += [copy(w, 1 + j, me, (*chip, cc), src=x_refs[w]) for j, chip in enumerate(chips)]
        return n, me, sibling, chips, cc, copy, mine, first

    def start(self, x_refs, out_refs, sems):
        *_, mine, first = self._plan(x_refs, out_refs, sems)
        for cp in mine + first:
            cp.start()

    def finish(self, x_refs, out_refs, sems):
        n, me, sibling, chips, cc, copy, mine, first = self._plan(x_refs, out_refs, sems)
        passed = []
        for w in range(n):
            for j, chip in enumerate(chips):
                copy(w, 1 + j, (*chip, cc), me).wait_recv()
                cp = copy(w, 4 + j, (*chip, cc), sibling)
                cp.start()
                passed.append(cp)
        for w in range(n):
            copy(w, 0, sibling, me).wait_recv()
            for j, chip in enumerate(chips):
                copy(w, 4 + j, (*chip, 1 - cc), me).wait_recv()
        for cp in first + passed:
            cp.wait_send()
        for cp in mine:
            cp.wait()


class _Exchange:
    def __init__(self, gs):
        self.ins = list(gs)
        self.out_shape = [jax.ShapeDtypeStruct(v.shape, v.dtype) for v in gs]

    def _plan(self, g_refs, out_refs, sems):
        send_sems, recv_sems, local_sems = sems
        x, y, cc = lax.axis_index("x"), lax.axis_index("y"), lax.axis_index("c")
        me = 4 * x + 2 * y + cc
        peers = []
        for k in range(1, N_DEV):
            px = 1 - x if (k >> 2) & 1 else x
            py = 1 - y if (k >> 1) & 1 else y
            pc = 1 - cc if k & 1 else cc
            peers.append((k, (px, py, pc), 4 * px + 2 * py + pc))

        def copy(w, k, peer, slot_src, slot_dst):
            return pltpu.make_async_remote_copy(
                src_ref=g_refs[w].at[slot_src], dst_ref=out_refs[w].at[slot_dst],
                send_sem=send_sems.at[w, k - 1], recv_sem=recv_sems.at[w, k - 1], device_id=peer, device_id_type=MESH)

        n = len(g_refs)
        mine = [pltpu.make_async_copy(g_refs[w].at[me], out_refs[w].at[me], local_sems.at[w]) for w in range(n)]
        sends = [copy(w, k, peer, idx, me) for w in range(n) for k, peer, idx in peers]
        return n, peers, copy, mine, sends

    def start(self, g_refs, out_refs, sems):
        *_, mine, sends = self._plan(g_refs, out_refs, sems)
        for cp in mine + sends:
            cp.start()

    def finish(self, g_refs, out_refs, sems):
        n, peers, copy, mine, sends = self._plan(g_refs, out_refs, sems)
        for w in range(n):
            for k, peer, idx in peers:
                copy(w, k, peer, idx, idx).wait_recv()
        for cp in sends:
            cp.wait_send()
        for cp in mine:
            cp.wait()


class _Both:
    def __init__(self, first, second):
        self.parts = (first, second)
        self.ins = first.ins + second.ins
        self.out_shape = first.out_shape + second.out_shape

    def _split(self, ins, outs, sems):
        n0 = len(self.parts[0].ins)
        n1 = len(self.parts[1].ins)
        for side, lo, n in ((self.parts[0], 0, n0), (self.parts[1], n0, n1)):
            yield side, ins[lo:lo + n], outs[lo:lo + n], tuple(s.at[pl.ds(lo, n)] for s in sems)

    def start(self, ins, outs, sems):
        for side, i, o, s in self._split(ins, outs, sems):
            side.start(i, o, s)

    def finish(self, ins, outs, sems):
        for side, i, o, s in self._split(ins, outs, sems):
            side.finish(i, o, s)


def _run_comm(side, *, name):
    n = len(side.ins)

    def body(*refs):
        ins, outs, sems = refs[:n], refs[n:2 * n], refs[2 * n:]
        side.start(ins, outs, sems)
        side.finish(ins, outs, sems)

    return pl.pallas_call(
        body, name=name, out_shape=side.out_shape,
        in_specs=[HBM_SPEC] * n, out_specs=[HBM_SPEC] * n, scratch_shapes=_comm_scratch(n),
    )(*side.ins)


def _hosted_call(body, side, *, name, grid, in_specs, out_specs, out_shape, scratch_shapes, args):
    out_specs, out_shape = list(out_specs), list(out_shape)
    if side is None:
        res = pl.pallas_call(
            body, name=name, grid=grid, in_specs=in_specs, out_specs=out_specs, out_shape=out_shape,
            scratch_shapes=scratch_shapes,
            compiler_params=pltpu.CompilerParams(dimension_semantics=("arbitrary",) * len(grid)))(*args)
        return res, None
    n_in, n_out, n_scr, ns = len(in_specs), len(out_specs), len(scratch_shapes), len(side.ins)

    def wrapped(*refs):
        ins, s_ins = refs[:n_in], refs[n_in:n_in + ns]
        outs = refs[n_in + ns:n_in + ns + n_out]
        s_outs = refs[n_in + ns + n_out:n_in + 2 * ns + n_out]
        scr = refs[n_in + 2 * ns + n_out:n_in + 2 * ns + n_out + n_scr]
        sems = refs[n_in + 2 * ns + n_out + n_scr:]
        ids = [pl.program_id(ax) for ax in range(len(grid))]
        first = functools.reduce(lambda p, q: p & q, [i == 0 for i in ids])
        last = functools.reduce(lambda p, q: p & q, [i == g - 1 for i, g in zip(ids, grid)])

        @pl.when(first)
        def _():
            side.start(s_ins, s_outs, sems)

        body(*ins, *outs, *scr)

        @pl.when(last)
        def _():
            side.finish(s_ins, s_outs, sems)

    res = pl.pallas_call(
        wrapped, name=name, grid=grid, in_specs=list(in_specs) + [HBM_SPEC] * ns,
        out_specs=out_specs + [HBM_SPEC] * ns, out_shape=out_shape + side.out_shape,
        scratch_shapes=list(scratch_shapes) + _comm_scratch(ns),
        compiler_params=pltpu.CompilerParams(dimension_semantics=("arbitrary",) * len(grid)))(*args, *side.ins)
    return res[:n_out], res[n_out:]


BIG = (("ssd_w_in", 2), ("ssd_w_out", 1), ("fox_w_in", 2), ("fox_w_out", 1), ("mla_w_in", 2),
       ("mla_w_q_up", 2), ("mla_w_kv_up", 2), ("mla_w_out", 1), ("s5_w_in", 2), ("s5_w_glu", 1), ("s5_w_out", 1))
LAYER_BIG = (("ssd_w_in", "ssd_w_out"), ("fox_w_in", "fox_w_out"),
             ("mla_w_in", "mla_w_q_up", "mla_w_kv_up", "mla_w_out"), ("s5_w_in", "s5_w_glu", "s5_w_out"))
SMALL = (("ssd_conv_w", 2), ("mla_q_norm", 1), ("mla_kv_norm", 1), ("s5_d", 1), ("s5_b_glu", 1))
REPLICATED = ("ln_g", "ln_b", "ssd_conv_b", "ssd_dt_bias", "ssd_a_log", "ssd_d", "ssd_norm_w", "fox_f_bias",
              "s5_lambda_re", "s5_lambda_im", "s5_log_step", "s5_b_re", "s5_b_im", "s5_c_re", "s5_c_im")
WEIGHT_ORDER = ("ln_g", "ln_b", "ssd_w_in", "ssd_conv_w", "ssd_conv_b", "ssd_dt_bias", "ssd_a_log", "ssd_d",
                "ssd_norm_w", "ssd_w_out", "fox_w_in", "fox_f_bias", "fox_w_out", "mla_w_in", "mla_q_norm",
                "mla_kv_norm", "mla_w_q_up", "mla_w_kv_up", "mla_w_out", "s5_w_in", "s5_lambda_re", "s5_lambda_im",
                "s5_log_step", "s5_b_re", "s5_b_im", "s5_c_re", "s5_c_im", "s5_d", "s5_w_glu", "s5_b_glu", "s5_w_out")


def _sum_adamw(parts, w, m, v, *, name):
    r, c = w.shape
    row_bytes = 2 * c * (N_DEV * parts.dtype.itemsize + 7 * 4)
    tr = _pick(r, [t for t in (256, 128, 64, 32, 16) if t * row_bytes <= VMEM_BLOCK_BUDGET])

    def body(p_ref, w_ref, m_ref, v_ref, g_ref, d_ref, m2_ref, v2_ref):
        gv = p_ref[0].astype(f32)
        for s in range(1, N_DEV):
            gv = gv + p_ref[s].astype(f32)
        g_ref[...] = gv
        m2 = ADAM_B1 * m_ref[...] + (1.0 - ADAM_B1) * gv
        v2 = ADAM_B2 * v_ref[...] + (1.0 - ADAM_B2) * jnp.square(gv)
        m_hat = m2 / (1.0 - ADAM_B1 ** ADAM_STEP)
        v_hat = v2 / (1.0 - ADAM_B2 ** ADAM_STEP)
        d_ref[...] = -ADAM_LR * (m_hat / (jnp.sqrt(v_hat) + ADAM_EPS) + ADAM_WD * w_ref[...])
        m2_ref[...] = m2
        v2_ref[...] = v2

    spec = pl.BlockSpec((tr, c), lambda i: (i, 0))
    shp = jax.ShapeDtypeStruct((r, c), f32)
    return pl.pallas_call(
        body, name=name, grid=(r // tr,),
        in_specs=[pl.BlockSpec((N_DEV, tr, c), lambda i: (0, i, 0))] + [spec] * 3,
        out_specs=[spec] * 4, out_shape=[shp] * 4,
        compiler_params=pltpu.CompilerParams(dimension_semantics=("parallel",)),
    )(parts, w, m, v)


def _train_step(a):
    small_names, small_axes = [n for n, _ in SMALL], [ax for _, ax in SMALL]
    small_shapes = [a[n].shape for n in small_names]
    rep_shapes = [a[n].shape for n in REPLICATED]

    big_axis = dict(BIG)

    def shards(names):
        return [a[n][0].astype(bf16) for n in names]

    def natural(names, gathered):
        w = {}
        for n, g in zip(names, gathered):
            if big_axis[n] == 1:
                w[n] = g.reshape(-1, g.shape[2])
            else:
                w[n] = jnp.transpose(g, (1, 0, 2)).reshape(g.shape[1], -1)
        return w

    def to_owner(nat):
        blocks = []
        for n, g in nat.items():
            if big_axis[n] == 1:
                g = g.reshape(N_DEV, -1, g.shape[1])
            else:
                g = jnp.transpose(g.reshape(g.shape[0], N_DEV, -1), (1, 0, 2))
            blocks.append(g.astype(bf16))
        return blocks

    small_flat = _pack([a[n] for n in small_names])
    got0 = _run_comm(_Gather(shards(LAYER_BIG[0]) + [small_flat]), name="gather_ssd_weights")
    small = dict(zip(small_names, _unpack_gathered(got0[-1], small_shapes, small_axes)))
    for n in REPLICATED:
        small[n] = a[n] if n in ("ln_g", "ln_b") else a[n][0]
    rest_names = LAYER_BIG[2] + LAYER_BIG[3]
    big_of_layer = (lambda got: natural(LAYER_BIG[0], got0[:-1]),
                    lambda got: natural(LAYER_BIG[1], got),
                    lambda got: natural(LAYER_BIG[2], got[:len(LAYER_BIG[2])]),
                    lambda got: natural(LAYER_BIG[3], got[len(LAYER_BIG[2]):]))
    def ssd_in_sides(nat_w_in):
        (blocks,) = to_owner({"ssd_w_in": nat_w_in})
        half = blocks.shape[1] // 2
        return _Exchange([blocks[:, :half]]), _Exchange([blocks[:, half:]])

    loss, grad_x, g_small, nats, hosted_parts = _sequence_step(
        a["x"][0], a["positions"][0], a["loss_target"][0], small, big_of_layer,
        gather_sides=(_Gather(shards(LAYER_BIG[1])), _Gather(shards(rest_names))),
        exchange_side=lambda nat: _Exchange(to_owner(nat)), ssd_in_sides=ssd_in_sides)
    parts_in = hosted_parts[3]
    parts_of = {"ssd_w_in": jnp.concatenate([parts_in[0][0], parts_in[1][0]], axis=1)}
    for names, parts in zip((LAYER_BIG[1] + ("ssd_w_out",), LAYER_BIG[2], LAYER_BIG[3]), hosted_parts[:3]):
        parts_of.update(zip(names, parts, strict=True))
    small_to_owner = jax.linear_transpose(
        lambda s: _unpack_gathered(s, small_shapes, small_axes), jax.ShapeDtypeStruct(got0[-1].shape, f32))
    (g_small_blocks,) = small_to_owner([g_small[n] for n in small_names])
    g_rep_local = _pack([g_small[n] if n in ("ln_g", "ln_b") else g_small[n][None] for n in REPLICATED])
    small_parts, g_rep_all = _run_comm(_Both(_Exchange([g_small_blocks]), _Gather([g_rep_local])),
                                       name="exchange_small_gradients")

    out = {"grad": {}, "delta": {}, "new_m": {}, "new_v": {}}
    kinds = ("grad", "delta", "new_m", "new_v")
    for n, _ in BIG:
        res = _sum_adamw(parts_of[n], a[n][0], a["m_" + n][0], a["v_" + n][0], name="adamw_" + n)
        for kind, r in zip(kinds, res):
            out[kind][n] = r[None]
    upd_small = _sum_adamw(small_parts, small_flat, _pack([a["m_" + n] for n in small_names]),
                           _pack([a["v_" + n] for n in small_names]), name="adamw_small_sharded")
    for kind, r in zip(kinds, upd_small):
        out[kind].update(zip(small_names, _unpack(r, small_shapes)))
    upd_rep = _sum_adamw(g_rep_all, _pack([a[n] for n in REPLICATED]), _pack([a["m_" + n] for n in REPLICATED]),
                         _pack([a["v_" + n] for n in REPLICATED]), name="adamw_replicated")
    for kind, r in zip(kinds, upd_rep):
        out[kind].update(zip(REPLICATED, _unpack(r, rep_shapes)))
    loss_total = lax.psum(loss[0, 0], ("x", "y", "c"))
    return (loss_total, grad_x[None],
            *[out[kind][n] for kind in ("grad", "delta", "new_m", "new_v") for n in WEIGHT_ORDER])


def kernel(x, positions, ln_g, ln_b, ssd_w_in, ssd_conv_w, ssd_conv_b, ssd_dt_bias, ssd_a_log, ssd_d, ssd_norm_w, ssd_w_out, fox_w_in, fox_f_bias, fox_w_out, mla_w_in, mla_q_norm, mla_kv_norm, mla_w_q_up, mla_w_kv_up, mla_w_out, s5_w_in, s5_lambda_re, s5_lambda_im, s5_log_step, s5_b_re, s5_b_im, s5_c_re, s5_c_im, s5_d, s5_w_glu, s5_b_glu, s5_w_out, loss_target, m_ln_g, m_ln_b, m_ssd_w_in, m_ssd_conv_w, m_ssd_conv_b, m_ssd_dt_bias, m_ssd_a_log, m_ssd_d, m_ssd_norm_w, m_ssd_w_out, m_fox_w_in, m_fox_f_bias, m_fox_w_out, m_mla_w_in, m_mla_q_norm, m_mla_kv_norm, m_mla_w_q_up, m_mla_w_kv_up, m_mla_w_out, m_s5_w_in, m_s5_lambda_re, m_s5_lambda_im, m_s5_log_step, m_s5_b_re, m_s5_b_im, m_s5_c_re, m_s5_c_im, m_s5_d, m_s5_w_glu, m_s5_b_glu, m_s5_w_out, v_ln_g, v_ln_b, v_ssd_w_in, v_ssd_conv_w, v_ssd_conv_b, v_ssd_dt_bias, v_ssd_a_log, v_ssd_d, v_ssd_norm_w, v_ssd_w_out, v_fox_w_in, v_fox_f_bias, v_fox_w_out, v_mla_w_in, v_mla_q_norm, v_mla_kv_norm, v_mla_w_q_up, v_mla_w_kv_up, v_mla_w_out, v_s5_w_in, v_s5_lambda_re, v_s5_lambda_im, v_s5_log_step, v_s5_b_re, v_s5_b_im, v_s5_c_re, v_s5_c_im, v_s5_d, v_s5_w_glu, v_s5_b_glu, v_s5_w_out):
    return _train_step(dict(locals()))
```

```python
import functools
import math

import jax
import jax.numpy as jnp
from jax import lax
from jax.experimental import pallas as pl
from jax.experimental.pallas import tpu as pltpu

f32 = jnp.float32
bf16 = jnp.bfloat16

N_DEV = 8
DEPTH = 4
ALPHA = (2.0 * DEPTH) ** 0.25
LN_EPS = 1e-5
RMS_EPS = 1e-6
LANE = 128

SSD_STATE = 128
SSD_HEADDIM = 64
SSD_CHUNK = 128
FOX_HEAD_DIM = 128
MLA_NOPE = 128
MLA_ROPE = 64
MLA_V = 128
ROPE_BASE = 10000.0
S5_GROUP = 16
S5_BLOCK_GROUPS = 8
S5_CHUNK = 128

ADAM_LR = 0.001
ADAM_B1 = 0.9
ADAM_B2 = 0.999
ADAM_EPS = 1e-08
ADAM_WD = 0.01
ADAM_STEP = 10

VMEM_BLOCK_BUDGET = 20 * 1024 * 1024
MM_FULL_K = 2048
HIGHEST = lax.Precision.HIGHEST


def _pick(n, cands):
    for c in cands:
        if n % c == 0:
            return c
    return n


def _dg(a, b, ca, cb):
    return lax.dot_general(a.astype(bf16), b.astype(bf16), (((ca,), (cb,)), ((), ())),
                           preferred_element_type=f32)


@functools.partial(jax.custom_vjp, nondiff_argnums=(2, 3))
def bdot(a, b, ca, cb):
    return _dg(a, b, ca, cb)


def _bdot_fwd(a, b, ca, cb):
    return _dg(a, b, ca, cb), (a, b)


def _bdot_bwd(ca, cb, res, g):
    a, b = res
    nb = 1 - cb
    ma = 1 - ca
    da = _dg(g, b, 1, nb) if ca == 1 else _dg(b, g, nb, 1)
    db = _dg(a, g, ma, 0) if cb == 0 else _dg(g, a, 0, ma)
    return da, db


bdot.defvjp(_bdot_fwd, _bdot_bwd)


def _silu(x):
    return x * jax.nn.sigmoid(x)


def _softplus(x):
    return jnp.maximum(x, 0.0) + jnp.log(1.0 + jnp.exp(-jnp.abs(x)))


def _mm(a, b, *, ta=False, tb=False, add=None, out_dtype=f32, name, side=None):
    m, k = (a.shape[1], a.shape[0]) if ta else a.shape
    n = b.shape[0] if tb else b.shape[1]
    kb = b.shape[1] if tb else b.shape[0]
    assert k == kb, (name, a.shape, b.shape)
    if k <= MM_FULL_K:
        tm = _pick(m, (512, 256, 128))
        tn = _pick(n, (1024, 512, 384, 256, 128))
        tk = k
    else:
        tm = _pick(m, (1024, 512, 256, 128))
        tn = _pick(n, (1024, 512, 384, 256, 128))
        tk = _pick(k, (1024, 512, 256, 128))
    nk = k // tk
    has_add = add is not None

    def body(*refs):
        if has_add:
            a_ref, b_ref, add_ref, o_ref, acc = refs
        else:
            a_ref, b_ref, o_ref, acc = refs
        kk = pl.program_id(2)

        def prod():
            return _dg(a_ref[...], b_ref[...], 0 if ta else 1, 1 if tb else 0)

        def finish(r):
            o_ref[...] = (r + add_ref[...] if has_add else r).astype(o_ref.dtype)

        if nk == 1:
            finish(prod())
            return

        @pl.when(kk == 0)
        def _():
            acc[...] = prod()

        @pl.when((kk > 0) & (kk < nk - 1))
        def _():
            acc[...] += prod()

        @pl.when(kk == nk - 1)
        def _():
            finish(acc[...] + prod())

    a_spec = (pl.BlockSpec((tk, tm), lambda i, j, kk: (kk, i)) if ta
              else pl.BlockSpec((tm, tk), lambda i, j, kk: (i, kk)))
    b_spec = (pl.BlockSpec((tn, tk), lambda i, j, kk: (j, kk)) if tb
              else pl.BlockSpec((tk, tn), lambda i, j, kk: (kk, j)))
    o_spec = pl.BlockSpec((tm, tn), lambda i, j, kk: (i, j))
    in_specs = [a_spec, b_spec] + ([o_spec] if has_add else [])
    args = (a, b) + ((add,) if has_add else ())
    if side is not None:
        (out,), got = _hosted_call(
            body, side, name=name, grid=(m // tm, n // tn, nk), in_specs=in_specs, out_specs=[o_spec],
            out_shape=[jax.ShapeDtypeStruct((m, n), out_dtype)], scratch_shapes=[pltpu.VMEM((tm, tn), f32)],
            args=args)
        return out, got
    return pl.pallas_call(
        body, name=name, grid=(m // tm, n // tn, nk),
        in_specs=in_specs, out_specs=o_spec,
        out_shape=jax.ShapeDtypeStruct((m, n), out_dtype),
        scratch_shapes=[pltpu.VMEM((tm, tn), f32)],
        compiler_params=pltpu.CompilerParams(dimension_semantics=("parallel", "parallel", "arbitrary")),
    )(*args)


def _row_operand(op):
    if isinstance(op, tuple):
        arr, w, cb = op
    else:
        arr, w, cb = op, op.shape[1], 0
    return arr, w, cb


def _rw_tm(t, widths):
    per_row = 2 * 4 * sum(widths)
    tm = 512
    while tm > 8 and (tm * per_row > VMEM_BLOCK_BUDGET or t % tm):
        tm //= 2
    return tm


def _rw_specs(ops, tm):
    specs, arrs = [], []
    for op in ops:
        arr, w, cb = _row_operand(op)
        specs.append(pl.BlockSpec((tm, w), functools.partial(lambda i, cb: (i, cb), cb=cb)))
        arrs.append(arr)
    return specs, arrs


def _param_spec(p):
    nd = p.ndim
    return pl.BlockSpec(p.shape, lambda i: (0,) * nd)


def _rw_fwd(fn, rows, params, out_widths, *, name):
    t = _row_operand(rows[0])[0].shape[0]
    tm = _rw_tm(t, [_row_operand(r)[1] for r in rows] + list(out_widths))
    nr, npar = len(rows), len(params)

    def body(*refs):
        vals = [r[...] for r in refs[:nr + npar]]
        outs = fn(*vals)
        for o_ref, v in zip(refs[nr + npar:], outs):
            o_ref[...] = v

    rspecs, rarrs = _rw_specs(rows, tm)
    return pl.pallas_call(
        body, name=name, grid=(t // tm,),
        in_specs=rspecs + [_param_spec(p) for p in params],
        out_specs=[pl.BlockSpec((tm, w), lambda i: (i, 0)) for w in out_widths],
        out_shape=[jax.ShapeDtypeStruct((t, w), f32) for w in out_widths],
        compiler_params=pltpu.CompilerParams(dimension_semantics=("parallel",)),
    )(*rarrs, *params)


def _rw_bwd(fn, rows, params, cots, *, name, n_const=0):
    t = _row_operand(rows[0])[0].shape[0]
    nr, npar, nc = len(rows), len(params), len(cots)
    nd = nr - n_const
    widths = [_row_operand(r)[1] for r in rows]
    tm = _rw_tm(t, widths + widths[:nd] + [c.shape[1] for c in cots])

    def body(*refs):
        rv = [r[...] for r in refs[:nr]]
        pv = [r[...] for r in refs[nr:nr + npar]]
        cv = [r[...] for r in refs[nr + npar:nr + npar + nc]]
        outs = refs[nr + npar + nc:]
        consts = rv[nd:]

        def f(*diff):
            return fn(*diff[:nd], *consts, *diff[nd:])

        _, vjp = jax.vjp(f, *rv[:nd], *pv)
        g = vjp(tuple(cv))
        for o_ref, v in zip(outs[:nd], g[:nd]):
            o_ref[...] = v
        if npar:
            @pl.when(pl.program_id(0) == 0)
            def _():
                for o_ref in outs[nd:]:
                    o_ref[...] = jnp.zeros_like(o_ref)

            for o_ref, v in zip(outs[nd:], g[nd:]):
                o_ref[...] += v

    rspecs, rarrs = _rw_specs(rows, tm)
    cspecs = [pl.BlockSpec((tm, c.shape[1]), lambda i: (i, 0)) for c in cots]
    return pl.pallas_call(
        body, name=name, grid=(t // tm,),
        in_specs=rspecs + [_param_spec(p) for p in params] + cspecs,
        out_specs=[pl.BlockSpec((tm, w), lambda i: (i, 0)) for w in widths[:nd]]
        + [_param_spec(p) for p in params],
        out_shape=[jax.ShapeDtypeStruct((t, w), f32) for w in widths[:nd]]
        + [jax.ShapeDtypeStruct(p.shape, f32) for p in params],
        compiler_params=pltpu.CompilerParams(dimension_semantics=("arbitrary",)),
    )(*rarrs, *params, *cots)


def _ln_fn(h, o, g, b):
    x = ALPHA * h + o
    mu = jnp.mean(x, -1, keepdims=True)
    var = jnp.mean(jnp.square(x - mu), -1, keepdims=True)
    return ((x - mu) * lax.rsqrt(var + LN_EPS) * g + b,)


def _gate_fn(o, z):
    return (o * _silu(z),)


def _make_ssd_post_fn(groups):
    def fn(y, z, nw):
        yz = y * _silu(z)
        gw = yz.shape[1] // groups
        outs = []
        for g in range(groups):
            blk = yz[:, g * gw:(g + 1) * gw]
            outs.append(blk * lax.rsqrt(jnp.mean(jnp.square(blk), -1, keepdims=True) + RMS_EPS))
        return (jnp.concatenate(outs, axis=1) * nw,)
    return fn


def _logf_fn(f, b):
    return (jax.nn.log_sigmoid(f + b),)


def _make_mla_norm_fn(rank):
    def fn(lat, qn, kvn):
        def rms(x, w):
            return x * lax.rsqrt(jnp.mean(jnp.square(x), -1, keepdims=True) + RMS_EPS) * w
        return rms(lat[:, :rank], qn), rms(lat[:, rank:], kvn)
    return fn


def _rope_block(xb, cos, sin):
    half = MLA_ROPE // 2
    x1, x2 = xb[:, :half], xb[:, half:MLA_ROPE]
    return jnp.concatenate([x1 * cos - x2 * sin, x1 * sin + x2 * cos, xb[:, MLA_ROPE:]], axis=1)


def _make_rope_fn(heads):
    def fn(x, cos, sin):
        return (jnp.concatenate([_rope_block(x[:, h * LANE:(h + 1) * LANE], cos, sin)
                                 for h in range(heads)], axis=1),)
    return fn


def _s5_act_fn(ys, u, d):
    return (jax.nn.gelu(ys + d * u),)


def _s5_glu_fn(y1, gp, z, bg):
    return (y1 * jax.nn.sigmoid(gp + bg) * _silu(z),)


def _loss_head(y, tgt):
    t, d = y.shape
    tm = _rw_tm(t, [d, d, d])

    def body(y_ref, t_ref, dy_ref, l_ref):
        @pl.when(pl.program_id(0) == 0)
        def _():
            l_ref[...] = jnp.zeros_like(l_ref)

        e = y_ref[...] - t_ref[...]
        dy_ref[...] = e * (1.0 / d)
        l_ref[...] += 0.5 * jnp.sum(jnp.mean(jnp.square(e), axis=-1, keepdims=True), axis=0, keepdims=True)

    spec = pl.BlockSpec((tm, d), lambda i: (i, 0))
    return pl.pallas_call(
        body, name="loss_head", grid=(t // tm,), in_specs=[spec, spec],
        out_specs=[spec, pl.BlockSpec((1, 1), lambda i: (0, 0))],
        out_shape=[jax.ShapeDtypeStruct((t, d), f32), jax.ShapeDtypeStruct((1, 1), f32)],
        compiler_params=pltpu.CompilerParams(dimension_semantics=("arbitrary",)),
    )(y, tgt)


def _attn_tile(t):
    return _pick(t, (512, 256, 128))


ATTN_HEADS_PER_STEP = 4
ATTN_BWD_HEADS_PER_STEP = 4
SSD_GROUPS_PER_STEP = 1
S5_BLOCKS_PER_STEP = 2


def _logits_t(k1, q1, k2, q2, ck_col, scale, key0, query0):
    s = _dg(k1, q1, 1, 1)
    if q2 is not None:
        s = s + _dg(k2, q2, 1, 1)
    s = s * scale
    if ck_col is not None:
        s = s - ck_col
    key = key0 + lax.broadcasted_iota(jnp.int32, s.shape, 0)
    query = query0 + lax.broadcasted_iota(jnp.int32, s.shape, 1)
    return jnp.where(key <= query, s, -jnp.inf)


def _lane_bcast(v):
    return jnp.broadcast_to(v[:, :, None], v.shape + (LANE,))


def _attn_fwd(q1, q1o, k1, k1o, v, vo, q2, k2, ck, *, heads, scale, name, side=None):
    t = q1.shape[0]
    tq = tk = _attn_tile(t)
    nq = t // tq
    has2, hasb = q2 is not None, ck is not None
    hp = ATTN_HEADS_PER_STEP
    wd = hp * LANE
    assert heads % hp == 0 and q1o % hp == 0 and k1o % hp == 0 and vo % hp == 0

    def body(*refs):
        it = iter(refs)
        q1r, k1r, vr = next(it), next(it), next(it)
        q2r, k2r = (next(it), next(it)) if has2 else (None, None)
        ckr = next(it) if hasb else None
        o_r, lse_r, m_s, l_s, acc = next(it), next(it), next(it), next(it), next(it)
        i, j = pl.program_id(1), pl.program_id(2)

        @pl.when(j == 0)
        def _():
            m_s[...] = jnp.full_like(m_s, -jnp.inf)
            l_s[...] = jnp.zeros_like(l_s)
            acc[...] = jnp.zeros_like(acc)

        @pl.when(j <= i)
        def _():
            k2v = k2r[...].astype(bf16) if has2 else None
            for e in range(hp):
                ln = slice(e * LANE, (e + 1) * LANE)
                q1v, k1v, vv = (r[:, ln].astype(bf16) for r in (q1r, k1r, vr))
                q2v = q2r[:, ln].astype(bf16) if has2 else None
                s = _logits_t(k1v, q1v, k2v, q2v, ckr[e][:, 0:1] if hasb else None, scale, j * tk, i * tq)
                m_prev = m_s[e]
                m_new = jnp.maximum(m_prev, jnp.max(s, axis=0, keepdims=True))
                p = jnp.exp(s - m_new)
                a = jnp.exp(m_prev - m_new)
                l_s[e] = a * l_s[e] + jnp.sum(p, axis=0, keepdims=True)
                acc[e] = a * acc[e] + _dg(vv, p, 0, 0)
                m_s[e] = m_new

        @pl.when(j == i)
        def _():
            for e in range(hp):
                o_r[:, e * LANE:(e + 1) * LANE] = (acc[e] / l_s[e]).T
                lse_r[e, 0] = m_s[e] + jnp.log(l_s[e])

    def qmap(off):
        return lambda h, i, j: (i, off // hp + h)

    def kmap(off):
        return lambda h, i, j: (jnp.minimum(j, i), off // hp + h)

    in_specs = [pl.BlockSpec((tq, wd), qmap(q1o)), pl.BlockSpec((tk, wd), kmap(k1o)),
                pl.BlockSpec((tk, wd), kmap(vo))]
    args = [q1, k1, v]
    if has2:
        in_specs += [pl.BlockSpec((tq, wd), qmap(0)),
                     pl.BlockSpec((tk, LANE), lambda h, i, j: (jnp.minimum(j, i), 0))]
        args += [q2, k2]
    if hasb:
        in_specs += [pl.BlockSpec((hp, tk, LANE), lambda h, i, j: (h, jnp.minimum(j, i), 0))]
        args += [ck]
    return _hosted_call(
        body, side, name=name, grid=(heads // hp, nq, nq), in_specs=in_specs,
        out_specs=[pl.BlockSpec((tq, wd), lambda h, i, j: (i, h)),
                   pl.BlockSpec((hp, 1, 1, tq), lambda h, i, j: (h, i, 0, 0))],
        out_shape=[jax.ShapeDtypeStruct((t, heads * LANE), f32),
                   jax.ShapeDtypeStruct((heads, nq, 1, tq), f32)],
        scratch_shapes=[pltpu.VMEM((hp, 1, tq), f32), pltpu.VMEM((hp, 1, tq), f32),
                        pltpu.VMEM((hp, LANE, tq), f32)],
        args=args)


def _row_sums_as_row(x, first_lane_only=False):
    sel = jnp.ones((8, x.shape[1]), f32)
    if first_lane_only:
        sel = (lax.broadcasted_iota(jnp.int32, sel.shape, 1) == 0).astype(f32)
    return lax.dot_general(sel, x, (((1,), (1,)), ((), ())), precision=HIGHEST, preferred_element_type=f32)[0:1]


def _attn_delta(do, o, *, heads, name):
    t = do.shape[0]
    tq = _attn_tile(t)
    hp = ATTN_HEADS_PER_STEP
    assert heads % hp == 0

    def body(do_ref, o_ref, d_ref):
        for e in range(hp):
            ln = slice(e * LANE, (e + 1) * LANE)
            d_ref[e, 0] = _row_sums_as_row(do_ref[:, ln] * o_ref[:, ln])

    spec = pl.BlockSpec((tq, hp * LANE), lambda i, h: (i, h))
    return pl.pallas_call(
        body, name=name, grid=(t // tq, heads // hp), in_specs=[spec, spec],
        out_specs=pl.BlockSpec((hp, 1, 1, tq), lambda i, h: (h, i, 0, 0)),
        out_shape=jax.ShapeDtypeStruct((heads, t // tq, 1, tq), f32),
        compiler_params=pltpu.CompilerParams(dimension_semantics=("parallel", "parallel")),
    )(do, o)


def _attn_bwd(q1, q1o, k1, k1o, v, vo, q2, k2, ck, do, lse, delta, *, heads, scale, name, side=None):
    t = q1.shape[0]
    tq = tk = _attn_tile(t)
    nq = t // tq
    has2, hasb = q2 is not None, ck is not None
    hp = ATTN_BWD_HEADS_PER_STEP // 2 if has2 else ATTN_BWD_HEADS_PER_STEP
    wd = hp * LANE
    assert heads % hp == 0 and q1o % hp == 0 and k1o % hp == 0 and vo % hp == 0

    def body(*refs):
        it = iter(refs)
        q1r, k1r, vr = next(it), next(it), next(it)
        q2r, k2r = (next(it), next(it)) if has2 else (None, None)
        ckr = next(it) if hasb else None
        dor, lser, dlr = next(it), next(it), next(it)
        dq1r, dk1r, dvr = next(it), next(it), next(it)
        dq2r, dk2r = (next(it), next(it)) if has2 else (None, None)
        drowr, dckr = (next(it), next(it)) if hasb else (None, None)
        ak, av = next(it), next(it)
        ac = next(it) if hasb else None
        h, j, i = pl.program_id(0), pl.program_id(1), pl.program_id(2)

        @pl.when((j == 0) & (i == 0))
        def _():
            dq1r[...] = jnp.zeros_like(dq1r)
            if has2:
                dq2r[...] = jnp.zeros_like(dq2r)
            if hasb:
                drowr[...] = jnp.zeros_like(drowr)

        if has2:
            @pl.when((h == 0) & (j == 0) & (i == 0))
            def _():
                dk2r[...] = jnp.zeros_like(dk2r)

        @pl.when(i == j)
        def _():
            ak[...] = jnp.zeros_like(ak)
            av[...] = jnp.zeros_like(av)
            if hasb:
                ac[...] = jnp.zeros_like(ac)

        @pl.when(i >= j)
        def _():
            rows = pl.ds(pl.multiple_of(i * tq, tq), tq)
            cols = pl.ds(pl.multiple_of(j * tk, tk), tk)
            k2v = k2r[...].astype(bf16) if has2 else None
            for e in range(hp):
                ln = slice(e * LANE, (e + 1) * LANE)
                q1v, k1v, vv, dov = (r[:, ln].astype(bf16) for r in (q1r, k1r, vr, dor))
                q2v = q2r[:, ln].astype(bf16) if has2 else None
                s = _logits_t(k1v, q1v, k2v, q2v, ckr[e][:, 0:1] if hasb else None, scale, j * tk, i * tq)
                p = jnp.exp(s - lser[e, 0])
                dp = _dg(vv, dov, 1, 1)
                dsr = p * (dp - dlr[e, 0])
                ds = (dsr * scale).astype(bf16)
                av[:, ln] += _dg(p, dov, 1, 0)
                ak[:, ln] += _dg(ds, q1v, 1, 0)
                dq1r[rows, ln] += _dg(ds, k1v, 0, 0)
                if has2:
                    dq2r[rows, ln] += _dg(ds, k2v, 0, 0)
                    dk2r[cols, :] += _dg(ds, q2v, 1, 0)
                if hasb:
                    drowr[e, pl.ds(i, 1), :] += jnp.sum(dsr, axis=0, keepdims=True)
                    ac[e] -= jnp.broadcast_to(jnp.sum(dsr, axis=1, keepdims=True), (tk, LANE))

        @pl.when(i == nq - 1)
        def _():
            dk1r[...] = ak[...]
            dvr[...] = av[...]
            if hasb:
                for e in range(hp):
                    dckr[e] = _row_sums_as_row(ac[e], first_lane_only=True)

    def qmap(off):
        return lambda h, j, i: (jnp.maximum(i, j), off // hp + h)

    def kmap(off):
        return lambda h, j, i: (j, off // hp + h)

    in_specs = [pl.BlockSpec((tq, wd), qmap(q1o)), pl.BlockSpec((tk, wd), kmap(k1o)),
                pl.BlockSpec((tk, wd), kmap(vo))]
    args = [q1, k1, v]
    if has2:
        in_specs += [pl.BlockSpec((tq, wd), qmap(0)), pl.BlockSpec((tk, LANE), lambda h, j, i: (j, 0))]
        args += [q2, k2]
    if hasb:
        in_specs += [pl.BlockSpec((hp, tk, LANE), lambda h, j, i: (h, j, 0))]
        args += [ck]
    row3 = pl.BlockSpec((hp, 1, 1, tq), lambda h, j, i: (h, jnp.maximum(i, j), 0, 0))
    in_specs += [pl.BlockSpec((tq, wd), qmap(0)), row3, row3]
    args += [do, lse, delta]
    hd = jax.ShapeDtypeStruct((t, heads * LANE), f32)
    head_cols = pl.BlockSpec((t, wd), lambda h, j, i: (0, h))
    kv_blk = pl.BlockSpec((tk, wd), kmap(0))
    out_specs, out_shape = [head_cols, kv_blk, kv_blk], [hd, hd, hd]
    scratch = [pltpu.VMEM((tk, wd), f32)] * 2
    if has2:
        out_specs += [head_cols, pl.BlockSpec((t, LANE), lambda h, j, i: (0, 0))]
        out_shape += [hd, jax.ShapeDtypeStruct((t, LANE), f32)]
    if hasb:
        out_specs += [pl.BlockSpec((hp, nq, tq), lambda h, j, i: (h, 0, 0)),
                      pl.BlockSpec((hp, 1, tk), lambda h, j, i: (h, 0, j))]
        out_shape += [jax.ShapeDtypeStruct((heads, nq, tq), f32), jax.ShapeDtypeStruct((heads, 1, t), f32)]
        scratch.append(pltpu.VMEM((hp, tk, LANE), f32))
    return _hosted_call(body, side, name=name, grid=(heads // hp, nq, nq), in_specs=in_specs, out_specs=out_specs,
                        out_shape=out_shape, scratch_shapes=scratch, args=args)


def _cumsum_rows(x, *, reverse, name):
    t, w = x.shape
    blk = 128
    nb = t // blk

    def body(x_ref, o_ref):
        r = lax.broadcasted_iota(jnp.int32, (blk, blk), 0)
        c = lax.broadcasted_iota(jnp.int32, (blk, blk), 1)
        tri = ((r <= c) if reverse else (r >= c)).astype(f32)
        carry = jnp.zeros((1, w), f32)
        for b in (reversed(range(nb)) if reverse else range(nb)):
            seg = x_ref[b * blk:(b + 1) * blk, :]
            cs = jnp.dot(tri, seg, precision=HIGHEST, preferred_element_type=f32) + carry
            o_ref[b * blk:(b + 1) * blk, :] = cs
            carry = cs[0:1, :] if reverse else cs[blk - 1:blk, :]

    return pl.pallas_call(body, name=name, out_shape=jax.ShapeDtypeStruct((t, w), f32))(x)


CONV_PAD = 8


def _conv_fn(x, w, b):
    t, taps = x.shape[0], w.shape[0]
    xx = jnp.concatenate([jnp.zeros((CONV_PAD, x.shape[1]), f32), x], axis=0)
    y = b
    for k in range(taps):
        off = CONV_PAD - (taps - 1) + k
        y = y + w[k:k + 1, :] * xx[off:off + t, :]
    return _silu(y)


def _conv_fwd(x, w, b, *, name):
    t, c = x.shape
    tc = LANE
    taps = w.shape[0]

    def body(x_ref, w_ref, b_ref, o_ref):
        o_ref[...] = _conv_fn(x_ref[...], w_ref[...], b_ref[...])

    xs = pl.BlockSpec((t, tc), lambda i: (0, i))
    return pl.pallas_call(
        body, name=name, grid=(c // tc,),
        in_specs=[xs, pl.BlockSpec((taps, tc), lambda i: (0, i)), pl.BlockSpec((1, tc), lambda i: (0, i))],
        out_specs=xs, out_shape=jax.ShapeDtypeStruct((t, c), f32),
        compiler_params=pltpu.CompilerParams(dimension_semantics=("parallel",)),
    )(x, w, b)


def _conv_bwd(x, w, b, ds, *, name):
    t, c = x.shape
    tc = LANE
    taps = w.shape[0]

    def body(x_ref, w_ref, b_ref, ds_ref, dx_ref, dw_ref, db_ref):
        _, vjp = jax.vjp(_conv_fn, x_ref[...], w_ref[...], b_ref[...])
        dx, dw, db = vjp(ds_ref[...])
        dx_ref[...] = dx
        dw_ref[...] = dw
        db_ref[...] = db

    xs = pl.BlockSpec((t, tc), lambda i: (0, i))
    ws = pl.BlockSpec((taps, tc), lambda i: (0, i))
    bs = pl.BlockSpec((1, tc), lambda i: (0, i))
    return pl.pallas_call(
        body, name=name, grid=(c // tc,), in_specs=[xs, ws, bs, xs], out_specs=[xs, ws, bs],
        out_shape=[jax.ShapeDtypeStruct((t, c), f32), jax.ShapeDtypeStruct((taps, c), f32),
                   jax.ShapeDtypeStruct((1, c), f32)],
        compiler_params=pltpu.CompilerParams(dimension_semantics=("parallel",)),
    )(x, w, b, ds)


def _ssd_chunk(x, bm, cm, dtr, st, dtb, alog, dsk, *, e_heads):
    ln, p = x.shape[0], SSD_HEADDIM
    dt = _softplus(dtr + dtb)
    da = dt * (-jnp.exp(alog))
    r = lax.broadcasted_iota(jnp.int32, (ln, ln), 0)
    c = lax.broadcasted_iota(jnp.int32, (ln, ln), 1)
    lower = r >= c
    acs = jnp.dot(lower.astype(f32), da, precision=HIGHEST, preferred_element_type=f32)
    acs_t = acs.T

    def per_head(v):
        return jnp.concatenate([jnp.broadcast_to(v[:, e:e + 1], (v.shape[0], p)) for e in range(e_heads)], axis=1)

    dt_x, acs_x = per_head(dt), per_head(acs)
    last_x = acs_x[ln - 1:ln, :]
    xdt = x * dt_x
    cb = bdot(cm, bm, 1, 1)
    y_off = bdot(cm, st, 1, 0) * jnp.exp(acs_x)
    st_new = st * jnp.exp(last_x) + bdot(bm, xdt * jnp.exp(last_x - acs_x), 0, 0)
    ys = []
    for e in range(e_heads):
        seg = acs[:, e:e + 1] - acs_t[e:e + 1, :]
        dec = jnp.exp(jnp.where(lower, seg, -jnp.inf))
        ys.append(bdot(cb * dec, xdt[:, e * p:(e + 1) * p], 1, 0))
    y = jnp.concatenate(ys, axis=1) + y_off + per_head(dsk) * x
    return y, st_new


def _ssd_specs(t, d_inner, groups):
    ln, n = SSD_CHUNK, SSD_STATE
    gw = d_inner // groups
    nc = t // ln
    return ln, n, gw, nc


def _ssd_fwd(xbc, dtp, dtb, alog, dsk, *, d_inner, groups, name, side=None):
    t = xbc.shape[0]
    ln, n, gw, nc = _ssd_specs(t, d_inner, groups)
    e_heads = gw // SSD_HEADDIM
    gp = SSD_GROUPS_PER_STEP
    assert groups % gp == 0 and (d_inner // n) % gp == 0
    boff = d_inner // n // gp
    ng = groups // gp

    def body(x_ref, b_ref, c_ref, dt_ref, dtb_ref, alog_ref, dsk_ref, y_ref, save_ref, st_scr):
        c, g = pl.program_id(0), pl.program_id(1)
        for e in range(gp):
            gi = g * gp + e

            @pl.when(c == 0)
            def _():
                st_scr[gi] = jnp.zeros((n, gw), f32)

            st = st_scr[gi]
            save_ref[e] = st
            y, st_new = _ssd_chunk(x_ref[:, e * gw:(e + 1) * gw], b_ref[:, e * n:(e + 1) * n],
                                   c_ref[:, e * n:(e + 1) * n], dt_ref[:, e * LANE:(e + 1) * LANE], st,
                                   dtb_ref[e], alog_ref[e], dsk_ref[e], e_heads=e_heads)
            y_ref[:, e * gw:(e + 1) * gw] = y
            st_scr[gi] = st_new

    par = pl.BlockSpec((gp, 1, LANE), lambda c, g: (g, 0, 0))
    return _hosted_call(
        body, side, name=name, grid=(nc, ng),
        in_specs=[pl.BlockSpec((ln, gp * gw), lambda c, g: (c, g)),
                  pl.BlockSpec((ln, gp * n), lambda c, g: (c, boff + g)),
                  pl.BlockSpec((ln, gp * n), lambda c, g: (c, boff + ng + g)),
                  pl.BlockSpec((ln, gp * LANE), lambda c, g: (c, g)), par, par, par],
        out_specs=[pl.BlockSpec((ln, gp * gw), lambda c, g: (c, g)),
                   pl.BlockSpec((gp, n, gw), lambda c, g: (c * ng + g, 0, 0))],
        out_shape=[jax.ShapeDtypeStruct((t, d_inner), f32),
                   jax.ShapeDtypeStruct((nc * groups, n, gw), f32)],
        scratch_shapes=[pltpu.VMEM((groups, n, gw), f32)],
        args=(xbc, xbc, xbc, dtp, dtb, alog, dsk))


def _ssd_bwd(xbc, dtp, dtb, alog, dsk, saved, dy, *, d_inner, groups, name, side=None):
    t = xbc.shape[0]
    ln, n, gw, nc = _ssd_specs(t, d_inner, groups)
    e_heads = gw // SSD_HEADDIM
    gp = SSD_GROUPS_PER_STEP
    assert groups % gp == 0 and (d_inner // n) % gp == 0
    boff = d_inner // n // gp
    ng = groups // gp

    def body(x_ref, b_ref, c_ref, dt_ref, dtb_ref, alog_ref, dsk_ref, save_ref, dy_ref,
             dx_ref, db_ref, dc_ref, ddt_ref, dpar_ref, dst_scr):
        c, g = pl.program_id(0), pl.program_id(1)

        @pl.when((c == 0) & (g == 0))
        def _():
            dpar_ref[...] = jnp.zeros_like(dpar_ref)

        fn = functools.partial(_ssd_chunk, e_heads=e_heads)
        for e in range(gp):
            gi = g * gp + e
            xl, nl, dl = (slice(e * w, (e + 1) * w) for w in (gw, n, LANE))

            @pl.when(c == 0)
            def _():
                dst_scr[gi] = jnp.zeros((n, gw), f32)

            _, vjp = jax.vjp(fn, x_ref[:, xl], b_ref[:, nl], c_ref[:, nl], dt_ref[:, dl], save_ref[e],
                             dtb_ref[e], alog_ref[e], dsk_ref[e])
            gx, gb, gc, gdt, gst, gdtb, galog, gdsk = vjp((dy_ref[:, xl], dst_scr[gi]))
            dx_ref[:, xl] = gx
            db_ref[:, nl] = gb
            dc_ref[:, nl] = gc
            ddt_ref[:, dl] = gdt
            dst_scr[gi] = gst
            dpar_ref[gi, 0:1, :] += gdtb
            dpar_ref[gi, 1:2, :] += galog
            dpar_ref[gi, 2:3, :] += gdsk

    def rc(c):
        return nc - 1 - c

    par = pl.BlockSpec((gp, 1, LANE), lambda c, g: (g, 0, 0))
    xs = pl.BlockSpec((ln, gp * gw), lambda c, g: (rc(c), g))
    ns = pl.BlockSpec((ln, gp * n), lambda c, g: (rc(c), g))
    return _hosted_call(
        body, side, name=name, grid=(nc, ng),
        in_specs=[xs,
                  pl.BlockSpec((ln, gp * n), lambda c, g: (rc(c), boff + g)),
                  pl.BlockSpec((ln, gp * n), lambda c, g: (rc(c), boff + ng + g)),
                  pl.BlockSpec((ln, gp * LANE), lambda c, g: (rc(c), g)), par, par, par,
                  pl.BlockSpec((gp, n, gw), lambda c, g: (rc(c) * ng + g, 0, 0)), xs],
        out_specs=[xs, ns, ns, pl.BlockSpec((ln, gp * LANE), lambda c, g: (rc(c), g)),
                   pl.BlockSpec((groups, 8, LANE), lambda c, g: (0, 0, 0))],
        out_shape=[jax.ShapeDtypeStruct((t, d_inner), f32),
                   jax.ShapeDtypeStruct((t, groups * n), f32),
                   jax.ShapeDtypeStruct((t, groups * n), f32),
                   jax.ShapeDtypeStruct((t, groups * LANE), f32),
                   jax.ShapeDtypeStruct((groups, 8, LANE), f32)],
        scratch_shapes=[pltpu.VMEM((groups, n, gw), f32)],
        args=(xbc, xbc, xbc, dtp, dtb, alog, dsk, saved, dy))


S5_TOP = 8


def _cmul(ar, ai, br, bi):
    return ar * br - ai * bi, ar * bi + ai * br


def _s5_scan(scr, base, ln, pw_ref, sp, *, reverse):
    s, k = 1, 0
    while s < ln:
        pr = pw_ref[0, k:k + 1, :sp]
        pi = pw_ref[0, k:k + 1, sp:]
        if reverse:
            pi = -pi
            src, dst = base + s, base
        else:
            src, dst = base, base + s
        ar, ai = scr[src:src + ln - s, :sp], scr[src:src + ln - s, sp:]
        mr, mi = _cmul(ar, ai, pr, pi)
        scr[dst:dst + ln - s, :sp] = scr[dst:dst + ln - s, :sp] + mr
        scr[dst:dst + ln - s, sp:] = scr[dst:dst + ln - s, sp:] + mi
        s *= 2
        k += 1


def _s5_states(u_ref, b_ref, lam_ref, pw_ref, scr, carry_r, carry_i, ln, sp):
    scr[S5_TOP:S5_TOP + ln, :] = _dg(u_ref[...], b_ref[0], 1, 0)
    mr, mi = _cmul(carry_r, carry_i, lam_ref[0][:, :sp], lam_ref[0][:, sp:])
    scr[S5_TOP:S5_TOP + 1, :sp] = scr[S5_TOP:S5_TOP + 1, :sp] + mr
    scr[S5_TOP:S5_TOP + 1, sp:] = scr[S5_TOP:S5_TOP + 1, sp:] + mi
    _s5_scan(scr, S5_TOP, ln, pw_ref, sp, reverse=False)


def _s5_dims(t, u_width):
    cbw = S5_BLOCK_GROUPS * S5_GROUP
    return S5_CHUNK, cbw, u_width // cbw, t // S5_CHUNK


def _s5_fwd(u, lamv, pw, bmat, cmat, *, name):
    t, w = u.shape
    ln, cbw, nb, nc = _s5_dims(t, w)
    sp2 = lamv.shape[2]
    sp = sp2 // 2
    bp = S5_BLOCKS_PER_STEP
    assert nb % bp == 0

    def one_block(u_ref, lam_ref, pw_ref, b_ref, c_ref, ys_ref, xp_ref, st_ref, scr, carry):
        @pl.when(pl.program_id(1) == 0)
        def _():
            carry[...] = jnp.zeros_like(carry)

        xp_ref[0] = carry[0:1, :]
        _s5_states(u_ref, b_ref, lam_ref, pw_ref, scr, carry[0:1, :sp], carry[0:1, sp:], ln, sp)
        carry[0:1, :] = scr[S5_TOP + ln - 1:S5_TOP + ln, :]
        states = scr[S5_TOP:S5_TOP + ln, :]
        st_ref[0] = states
        ys_ref[...] = _dg(states, c_ref[0], 1, 0)

    def body(u_ref, lam_ref, pw_ref, b_ref, c_ref, ys_ref, xp_ref, st_ref, scr, carry):
        for e in range(bp):
            one, cols = pl.ds(e, 1), pl.ds(e * cbw, cbw)
            one_block(u_ref.at[:, cols], lam_ref.at[one], pw_ref.at[one], b_ref.at[one], c_ref.at[one],
                      ys_ref.at[:, cols], xp_ref.at[0, one], st_ref.at[one], scr.at[e], carry.at[e])

    def blk(shape):
        return pl.BlockSpec((bp,) + shape, lambda j, c: (j, 0, 0))

    return pl.pallas_call(
        body, name=name, grid=(nb // bp, nc),
        in_specs=[pl.BlockSpec((ln, bp * cbw), lambda j, c: (c, j)), blk((1, sp2)), blk((8, sp2)),
                  blk((cbw, sp2)), blk((sp2, cbw))],
        out_specs=[pl.BlockSpec((ln, bp * cbw), lambda j, c: (c, j)),
                   pl.BlockSpec((1, bp, 1, sp2), lambda j, c: (c, j, 0, 0)),
                   pl.BlockSpec((bp, ln, sp2), lambda j, c: (j, c, 0))],
        out_shape=[jax.ShapeDtypeStruct((t, w), f32), jax.ShapeDtypeStruct((nc, nb, 1, sp2), f32),
                   jax.ShapeDtypeStruct((nb, t, sp2), f32)],
        scratch_shapes=[pltpu.VMEM((bp, S5_TOP + ln, sp2), f32), pltpu.VMEM((bp, 8, sp2), f32)],
        compiler_params=pltpu.CompilerParams(dimension_semantics=("parallel", "arbitrary")),
    )(u, lamv, pw, bmat, cmat)


def _s5_bwd(u, dy, du_skip, xp, states, lamv, pw, bmat, cmat, *, name):
    t, w = u.shape
    ln, cbw, nb, nc = _s5_dims(t, w)
    sp2 = lamv.shape[2]
    sp = sp2 // 2
    bp = S5_BLOCKS_PER_STEP
    assert nb % bp == 0

    def body(u_ref, dy_ref, dus_ref, xp_ref, st_ref, lam_ref, pw_ref, b_ref, c_ref,
             du_ref, db_ref, dc_ref, dl_ref, scr, gsc, gcarry):
        for e in range(bp):
            one, cols = pl.ds(e, 1), pl.ds(e * cbw, cbw)
            one_block(u_ref.at[:, cols], dy_ref.at[:, cols], dus_ref.at[:, cols], xp_ref.at[0, one],
                      st_ref.at[one], lam_ref.at[one], pw_ref.at[one], b_ref.at[one], c_ref.at[one],
                      du_ref.at[:, cols], db_ref.at[one], dc_ref.at[one], dl_ref.at[one],
                      scr.at[e], gsc.at[e], gcarry.at[e])

    def one_block(u_ref, dy_ref, dus_ref, xp_ref, st_ref, lam_ref, pw_ref, b_ref, c_ref,
                  du_ref, db_ref, dc_ref, dl_ref, scr, gsc, gcarry):
        @pl.when(pl.program_id(1) == 0)
        def _():
            gcarry[...] = jnp.zeros_like(gcarry)
            db_ref[...] = jnp.zeros_like(db_ref)
            dc_ref[...] = jnp.zeros_like(dc_ref)
            dl_ref[...] = jnp.zeros_like(dl_ref)

        lr, li = lam_ref[0][:, :sp], lam_ref[0][:, sp:]
        scr[S5_TOP - 1:S5_TOP, :] = xp_ref[0]
        scr[S5_TOP:S5_TOP + ln, :] = st_ref[0]
        dy = dy_ref[...]
        gsc[0:ln, :] = _dg(dy, c_ref[0], 1, 1)
        mr, mi = _cmul(gcarry[0:1, :sp], gcarry[0:1, sp:], lr, -li)
        gsc[ln - 1:ln, :sp] = gsc[ln - 1:ln, :sp] + mr
        gsc[ln - 1:ln, sp:] = gsc[ln - 1:ln, sp:] + mi
        _s5_scan(gsc, 0, ln, pw_ref, sp, reverse=True)
        gcarry[0:1, :] = gsc[0:1, :]
        g = gsc[0:ln, :]
        du_ref[...] = dus_ref[...] + _dg(g, b_ref[0], 1, 1)
        db_ref[0] += _dg(u_ref[...], g, 0, 0)
        dc_ref[0] += _dg(scr[S5_TOP:S5_TOP + ln, :], dy, 0, 0)
        pr, pi = scr[S5_TOP - 1:S5_TOP - 1 + ln, :sp], scr[S5_TOP - 1:S5_TOP - 1 + ln, sp:]
        gr, gi = g[:, :sp], g[:, sp:]
        dl_ref[0, 0:1, :sp] += jnp.sum(pr * gr + pi * gi, axis=0, keepdims=True)
        dl_ref[0, 0:1, sp:] += jnp.sum(pr * gi - pi * gr, axis=0, keepdims=True)

    def rc(c):
        return nc - 1 - c

    def blk(shape):
        return pl.BlockSpec((bp,) + shape, lambda j, c: (j, 0, 0))

    row = pl.BlockSpec((ln, bp * cbw), lambda j, c: (rc(c), j))
    return pl.pallas_call(
        body, name=name, grid=(nb // bp, nc),
        in_specs=[row, row, row, pl.BlockSpec((1, bp, 1, sp2), lambda j, c: (rc(c), j, 0, 0)),
                  pl.BlockSpec((bp, ln, sp2), lambda j, c: (j, rc(c), 0)),
                  blk((1, sp2)), blk((8, sp2)), blk((cbw, sp2)), blk((sp2, cbw))],
        out_specs=[row, blk((cbw, sp2)), blk((sp2, cbw)), blk((8, sp2))],
        out_shape=[jax.ShapeDtypeStruct((t, w), f32), jax.ShapeDtypeStruct((nb, cbw, sp2), f32),
                   jax.ShapeDtypeStruct((nb, sp2, cbw), f32), jax.ShapeDtypeStruct((nb, 8, sp2), f32)],
        scratch_shapes=[pltpu.VMEM((bp, S5_TOP + ln, sp2), f32), pltpu.VMEM((bp, ln, sp2), f32),
                        pltpu.VMEM((bp, 8, sp2), f32)],
        compiler_params=pltpu.CompilerParams(dimension_semantics=("parallel", "arbitrary")),
    )(u, dy, du_skip, xp, states, lamv, pw, bmat, cmat)


def _s5_discretize(lre, lim, log_step, bre, bim, cre, cim):
    g, p = lre.shape
    gb = S5_BLOCK_GROUPS
    nb = g // gb
    lam = lax.complex(lre, lim)
    lam_bar = jnp.exp(lam * jnp.exp(log_step)[:, None])
    b_bar = ((lam_bar - 1.0) / lam)[..., None] * lax.complex(bre, bim)
    lb = lam_bar.reshape(nb, 1, gb * p)
    lamv = jnp.concatenate([jnp.real(lb), jnp.imag(lb)], axis=-1)
    eye = jnp.eye(gb, dtype=f32)
    bb = b_bar.reshape(nb, gb, p, S5_GROUP)

    def bdiag(v):
        return jnp.einsum('jgpi,gh->jgihp', v, eye).reshape(nb, gb * S5_GROUP, gb * p)

    bmat = jnp.concatenate([bdiag(jnp.real(bb)), bdiag(jnp.imag(bb))], axis=-1)

    def cdiag(v):
        return jnp.einsum('jgip,gh->jhpgi', v, eye).reshape(nb, gb * p, gb * S5_GROUP)

    cmat = jnp.concatenate([cdiag(cre.reshape(nb, gb, S5_GROUP, p)),
                            -cdiag(cim.reshape(nb, gb, S5_GROUP, p))], axis=1)
    return lamv, bmat, cmat


def _s5_powers(lamv):
    sp = lamv.shape[2] // 2
    pr, pi = lamv[:, :, :sp], lamv[:, :, sp:]
    rows = []
    for _ in range(8):
        rows.append(jnp.concatenate([pr, pi], axis=-1))
        pr, pi = pr * pr - pi * pi, 2.0 * pr * pi
    return lax.stop_gradient(jnp.concatenate(rows, axis=1))


def _ln_fwd(h, out, p, tag):
    return _rw_fwd(_ln_fn, [h, out], [p["ln_g"], p["ln_b"]], [h.shape[1]], name=tag + "_ln")[0]


def _ln_bwd(h, out, p, dh2, tag):
    return _rw_bwd(_ln_fn, [h, out], [p["ln_g"], p["ln_b"]], [dh2], name=tag + "_ln_bwd")


def _ssd_layer_fwd(h, p, wout_of, side=None):
    di, groups = p["wz"].shape[1], p["dtb"].shape[0]
    z = _mm(h, p["wz"], name="ssd_z")
    xr = _mm(h, p["wxbc"], name="ssd_xbc")
    dtp = _mm(h, p["wdt"], name="ssd_dt")
    xbc = _conv_fwd(xr, p["conv_w"], p["conv_b"], name="ssd_conv")
    (y, states), got = _ssd_fwd(xbc, dtp, p["dtb"], p["alog"], p["dsk"], d_inner=di, groups=groups,
                                name="ssd_scan", side=side)
    p["wout"] = wout_of(got)
    yn = _rw_fwd(_make_ssd_post_fn(groups), [y, z], [p["norm_w"]], [di], name="ssd_post")[0]
    out = _mm(yn, p["wout"], name="ssd_out")
    return _ln_fwd(h, out, p, "ssd"), (h, z, xr, dtp, xbc, states, y, yn, out), got


def _ssd_layer_bwd(saved, p, dh2, side=None, in_sides=None):
    h, z, xr, dtp, xbc, states, y, yn, out = saved
    di, groups = p["wz"].shape[1], p["dtb"].shape[0]
    dh, dout, dg, db = _ln_bwd(h, out, p, dh2, "ssd")
    dyn = _mm(dout, p["wout"], tb=True, name="ssd_out_dx")
    g = {"ln_g": dg, "ln_b": db, "wout": _mm(yn, dout, ta=True, out_dtype=bf16, name="ssd_out_dw")}
    if callable(side):
        side = side(g["wout"])
    dy, dz, g["norm_w"] = _rw_bwd(_make_ssd_post_fn(groups), [y, z], [p["norm_w"]], [dyn], name="ssd_post_bwd")
    (dx, dbm, dcm, ddt, dpar), got = _ssd_bwd(xbc, dtp, p["dtb"], p["alog"], p["dsk"], states, dy,
                                              d_inner=di, groups=groups, name="ssd_scan_bwd", side=side)
    g["dtb"], g["alog"], g["dsk"] = dpar[:, 0:1, :], dpar[:, 1:2, :], dpar[:, 2:3, :]
    dxr, g["conv_w"], g["conv_b"] = _conv_bwd(xr, p["conv_w"], p["conv_b"],
                                               jnp.concatenate([dx, dbm, dcm], axis=1), name="ssd_conv_bwd")
    g["wz"] = _mm(h, dz, ta=True, name="ssd_z_dw")
    g["wxbc"] = _mm(h, dxr, ta=True, name="ssd_xbc_dw")
    g["wdt"] = _mm(h, ddt, ta=True, name="ssd_dt_dw")
    if in_sides is None:
        dh = _mm(dxr, p["wxbc"], tb=True, add=dh, name="ssd_xbc_dx")
        dh = _mm(dz, p["wz"], tb=True, add=dh, name="ssd_z_dx")
        got_in = (None, None)
    else:
        side_a, side_b = in_sides(g)
        dh, got_a = _mm(dxr, p["wxbc"], tb=True, add=dh, name="ssd_xbc_dx", side=side_a)
        dh, got_b = _mm(dz, p["wz"], tb=True, add=dh, name="ssd_z_dx", side=side_b)
        got_in = (got_a, got_b)
    dh = _mm(ddt, p["wdt"], tb=True, add=dh, name="ssd_dt_dx")
    return dh, g, got, got_in


def _fox_layer_fwd(h, p, side=None):
    w = p["wout"].shape[0]
    heads = w // FOX_HEAD_DIM
    t = h.shape[0]
    qkvz = _mm(h, p["wqkvz"], name="fox_qkvz")
    fr = _mm(h, p["wf"], name="fox_f")
    logf = _rw_fwd(_logf_fn, [fr], [p["fb"]], [LANE], name="fox_logf")[0]
    cum = _cumsum_rows(logf, reverse=False, name="fox_cumsum")
    ck = _lane_bcast(cum[:, :heads].T)
    (o, lse), got = _attn_fwd(qkvz, 0, qkvz, heads, qkvz, 2 * heads, None, None, ck,
                              heads=heads, scale=FOX_HEAD_DIM ** -0.5, name="fox_attn", side=side)
    yg = _rw_fwd(_gate_fn, [o, (qkvz, w, 3)], [], [w], name="fox_gate")[0]
    out = _mm(yg, p["wout"], name="fox_out")
    return _ln_fwd(h, out, p, "fox"), (h, qkvz, fr, ck, o, lse, yg, out), got


def _fox_layer_bwd(saved, p, dh2, side=None):
    h, qkvz, fr, ck, o, lse, yg, out = saved
    w = p["wout"].shape[0]
    heads = w // FOX_HEAD_DIM
    t = h.shape[0]
    dh, dout, dg, db = _ln_bwd(h, out, p, dh2, "fox")
    dyg = _mm(dout, p["wout"], tb=True, name="fox_out_dx")
    g = {"ln_g": dg, "ln_b": db, "wout": _mm(yg, dout, ta=True, out_dtype=bf16, name="fox_out_dw")}
    do, dz = _rw_bwd(_gate_fn, [o, (qkvz, w, 3)], [], [dyg], name="fox_gate_bwd")
    delta = _attn_delta(do, o, heads=heads, name="fox_delta")
    (dq, dk, dv, drow, dck), got = _attn_bwd(qkvz, 0, qkvz, heads, qkvz, 2 * heads, None, None, ck, do, lse, delta,
                                             heads=heads, scale=FOX_HEAD_DIM ** -0.5, name="fox_attn_bwd", side=side)
    dqkvz = jnp.concatenate([dq, dk, dv, dz], axis=1)
    dcum = jnp.pad((drow.reshape(heads, t) + dck.reshape(heads, t)).T, ((0, 0), (0, LANE - heads)))
    dlogf = _cumsum_rows(dcum, reverse=True, name="fox_cumsum_bwd")
    dfr, g["fb"] = _rw_bwd(_logf_fn, [fr], [p["fb"]], [dlogf], name="fox_logf_bwd")
    dh = _mm(dqkvz, p["wqkvz"], tb=True, add=dh, name="fox_qkvz_dx")
    dh = _mm(dfr, p["wf"], tb=True, add=dh, name="fox_f_dx")
    g["wqkvz"] = _mm(h, dqkvz, ta=True, name="fox_qkvz_dw")
    g["wf"] = _mm(h, dfr, ta=True, name="fox_f_dw")
    return dh, g, got


def _mla_layer_fwd(h, p, cos, sin):
    rank = p["qn"].shape[1]
    w = p["wout"].shape[0]
    heads = w // MLA_V
    lat = _mm(h, p["wlat"], name="mla_lat")
    kpr = _mm(h, p["wkpe"], name="mla_kpe")
    z = _mm(h, p["wz"], name="mla_z")
    qnl, kvnl = _rw_fwd(_make_mla_norm_fn(rank), [lat], [p["qn"], p["kvn"]], [rank, rank], name="mla_norm")
    qno = _mm(qnl, p["wqn"], name="mla_qn")
    qpr = _mm(qnl, p["wqp"], name="mla_qp")
    kno = _mm(kvnl, p["wkn"], name="mla_kn")
    v = _mm(kvnl, p["wv"], name="mla_v")
    qp = _rw_fwd(_make_rope_fn(heads), [qpr, cos, sin], [], [heads * LANE], name="mla_rope_q")[0]
    kp = _rw_fwd(_make_rope_fn(1), [kpr, cos, sin], [], [LANE], name="mla_rope_k")[0]
    scale = (MLA_NOPE + MLA_ROPE) ** -0.5
    (o, lse), _ = _attn_fwd(qno, 0, kno, 0, v, 0, qp, kp, None, heads=heads, scale=scale, name="mla_attn")
    yg = _rw_fwd(_gate_fn, [o, z], [], [w], name="mla_gate")[0]
    out = _mm(yg, p["wout"], name="mla_out")
    return _ln_fwd(h, out, p, "mla"), (h, lat, kpr, z, qnl, kvnl, qno, qpr, kno, v, qp, kp, o, lse, yg, out)


def _mla_layer_bwd(saved, p, dh2, cos, sin, side=None):
    h, lat, kpr, z, qnl, kvnl, qno, qpr, kno, v, qp, kp, o, lse, yg, out = saved
    rank = p["qn"].shape[1]
    w = p["wout"].shape[0]
    heads = w // MLA_V
    dh, dout, dg, db = _ln_bwd(h, out, p, dh2, "mla")
    dyg = _mm(dout, p["wout"], tb=True, name="mla_out_dx")
    g = {"ln_g": dg, "ln_b": db, "wout": _mm(yg, dout, ta=True, out_dtype=bf16, name="mla_out_dw")}
    do, dz = _rw_bwd(_gate_fn, [o, z], [], [dyg], name="mla_gate_bwd")
    delta = _attn_delta(do, o, heads=heads, name="mla_delta")
    (dqno, dkno, dv, dqp, dkp), got = _attn_bwd(qno, 0, kno, 0, v, 0, qp, kp, None, do, lse, delta, heads=heads,
                                                scale=(MLA_NOPE + MLA_ROPE) ** -0.5, name="mla_attn_bwd", side=side)
    (dqpr,) = _rw_bwd(_make_rope_fn(heads), [qpr, cos, sin], [], [dqp], n_const=2, name="mla_rope_q_bwd")
    (dkpr,) = _rw_bwd(_make_rope_fn(1), [kpr, cos, sin], [], [dkp], n_const=2, name="mla_rope_k_bwd")
    dqnl = _mm(dqno, p["wqn"], tb=True, name="mla_qn_dx")
    dqnl = _mm(dqpr, p["wqp"], tb=True, add=dqnl, name="mla_qp_dx")
    dkvnl = _mm(dkno, p["wkn"], tb=True, name="mla_kn_dx")
    dkvnl = _mm(dv, p["wv"], tb=True, add=dkvnl, name="mla_v_dx")
    g["wqn"] = _mm(qnl, dqno, ta=True, name="mla_qn_dw")
    g["wqp"] = _mm(qnl, dqpr, ta=True, name="mla_qp_dw")
    g["wkn"] = _mm(kvnl, dkno, ta=True, name="mla_kn_dw")
    g["wv"] = _mm(kvnl, dv, ta=True, name="mla_v_dw")
    dlat, g["qn"], g["kvn"] = _rw_bwd(_make_mla_norm_fn(rank), [lat], [p["qn"], p["kvn"]], [dqnl, dkvnl],
                                     name="mla_norm_bwd")
    dh = _mm(dlat, p["wlat"], tb=True, add=dh, name="mla_lat_dx")
    dh = _mm(dkpr, p["wkpe"], tb=True, add=dh, name="mla_kpe_dx")
    dh = _mm(dz, p["wz"], tb=True, add=dh, name="mla_z_dx")
    g["wlat"] = _mm(h, dlat, ta=True, name="mla_lat_dw")
    g["wkpe"] = _mm(h, dkpr, ta=True, name="mla_kpe_dw")
    g["wz"] = _mm(h, dz, ta=True, name="mla_z_dw")
    return dh, g, got


def _s5_layer_fwd(h, p):
    w = p["wout"].shape[0]
    u = _mm(h, p["wu"], name="s5_u")
    z = _mm(h, p["wz"], name="s5_z")
    ys, xp, states = _s5_fwd(u, p["lamv"], p["pw"], p["bmat"], p["cmat"], name="s5_scan")
    y1 = _rw_fwd(_s5_act_fn, [ys, u], [p["d"]], [w], name="s5_act")[0]
    gp = _mm(y1, p["wglu"], name="s5_glu")
    y2 = _rw_fwd(_s5_glu_fn, [y1, gp, z], [p["bglu"]], [w], name="s5_gate")[0]
    out = _mm(y2, p["wout"], name="s5_out")
    return _ln_fwd(h, out, p, "s5"), (h, u, z, ys, xp, states, y1, gp, y2, out)


def _s5_layer_bwd(saved, p, dh2):
    h, u, z, ys, xp, states, y1, gp, y2, out = saved
    dh, dout, dg, db = _ln_bwd(h, out, p, dh2, "s5")
    dy2 = _mm(dout, p["wout"], tb=True, name="s5_out_dx")
    g = {"ln_g": dg, "ln_b": db, "wout": _mm(y2, dout, ta=True, out_dtype=bf16, name="s5_out_dw")}
    dy1, dgp, dz, g["bglu"] = _rw_bwd(_s5_glu_fn, [y1, gp, z], [p["bglu"]], [dy2], name="s5_gate_bwd")
    dy1 = _mm(dgp, p["wglu"], tb=True, add=dy1, name="s5_glu_dx")
    g["wglu"] = _mm(y1, dgp, ta=True, out_dtype=bf16, name="s5_glu_dw")
    dys, du, g["d"] = _rw_bwd(_s5_act_fn, [ys, u], [p["d"]], [dy1], name="s5_act_bwd")
    du, g["bmat"], g["cmat"], dlam = _s5_bwd(u, dys, du, xp, states, p["lamv"], p["pw"], p["bmat"], p["cmat"],
                                             name="s5_scan_bwd")
    g["lamv"] = dlam[:, 0:1, :]
    dh = _mm(du, p["wu"], tb=True, add=dh, name="s5_u_dx")
    dh = _mm(dz, p["wz"], tb=True, add=dh, name="s5_z_dx")
    g["wu"] = _mm(h, du, ta=True, name="s5_u_dw")
    g["wz"] = _mm(h, dz, ta=True, name="s5_z_dw")
    return dh, g


def _pad_last(a, to):
    return jnp.pad(a, [(0, 0)] * (a.ndim - 1) + [(0, to - a.shape[-1])])


def _row(v):
    return v.reshape(1, -1)


def _prep_ssd(w):
    di, cd, hh = w["ssd_norm_w"].shape[0], w["ssd_conv_b"].shape[0], w["ssd_dt_bias"].shape[0]
    groups = (cd - di) // (2 * SSD_STATE)
    e = hh // groups

    def perm(v):
        lead = v.shape[:-1]
        return _pad_last(v.reshape(lead + (groups, e)), LANE).reshape(lead + (groups * LANE,))

    w_in = w["ssd_w_in"]
    return dict(
        wz=w_in[:, :di], wxbc=w_in[:, di:di + cd], wdt=perm(w_in[:, di + cd:]),
        conv_w=w["ssd_conv_w"], conv_b=_row(w["ssd_conv_b"]),
        dtb=perm(w["ssd_dt_bias"]).reshape(groups, 1, LANE), alog=perm(w["ssd_a_log"]).reshape(groups, 1, LANE),
        dsk=perm(w["ssd_d"]).reshape(groups, 1, LANE), norm_w=_row(w["ssd_norm_w"]),
        ln_g=_row(w["ln_g"][0]), ln_b=_row(w["ln_b"][0]))


def _prep_fox(w):
    wd = w["fox_w_out"].shape[0]
    w_in = w["fox_w_in"]
    return dict(wqkvz=w_in[:, :4 * wd], wf=_pad_last(w_in[:, 4 * wd:], LANE), wout=w["fox_w_out"],
                fb=_pad_last(_row(w["fox_f_bias"]), LANE), ln_g=_row(w["ln_g"][1]), ln_b=_row(w["ln_b"][1]))


def _prep_mla(w):
    rank = w["mla_q_norm"].shape[0]
    wd = w["mla_w_out"].shape[0]
    heads = wd // MLA_V
    w_in = w["mla_w_in"]
    qu = w["mla_w_q_up"].reshape(rank, heads, MLA_NOPE + MLA_ROPE)
    kvu = w["mla_w_kv_up"].reshape(w["mla_w_kv_up"].shape[0], heads, MLA_NOPE + MLA_V)
    return dict(
        wlat=w_in[:, :2 * rank], wkpe=_pad_last(w_in[:, 2 * rank:2 * rank + MLA_ROPE], LANE),
        wz=w_in[:, 2 * rank + MLA_ROPE:],
        wqn=qu[:, :, :MLA_NOPE].reshape(rank, heads * LANE),
        wqp=_pad_last(qu[:, :, MLA_NOPE:], LANE).reshape(rank, heads * LANE),
        wkn=kvu[:, :, :MLA_NOPE].reshape(-1, heads * LANE), wv=kvu[:, :, MLA_NOPE:].reshape(-1, heads * LANE),
        wout=w["mla_w_out"], qn=_row(w["mla_q_norm"]), kvn=_row(w["mla_kv_norm"]),
        ln_g=_row(w["ln_g"][2]), ln_b=_row(w["ln_b"][2]))


def _prep_s5(w):
    wd = w["s5_w_out"].shape[0]
    w_in = w["s5_w_in"]
    lamv, bmat, cmat = _s5_discretize(w["s5_lambda_re"], w["s5_lambda_im"], w["s5_log_step"],
                                      w["s5_b_re"], w["s5_b_im"], w["s5_c_re"], w["s5_c_im"])
    return dict(wu=w_in[:, :wd], wz=w_in[:, wd:], wglu=w["s5_w_glu"], wout=w["s5_w_out"],
                d=_row(w["s5_d"]), bglu=_row(w["s5_b_glu"]), lamv=lamv, bmat=bmat, cmat=cmat,
                ln_g=_row(w["ln_g"][3]), ln_b=_row(w["ln_b"][3]))


def _rope_tables(positions):
    inv_freq = ROPE_BASE ** (-jnp.arange(0, MLA_ROPE, 2, dtype=f32) / MLA_ROPE)
    ang = positions.astype(f32).reshape(-1, 1) * inv_freq
    return jnp.cos(ang), jnp.sin(ang)


def _heads3(v):
    return v.reshape(v.shape[0], -1, LANE)


def _flat2(v):
    return v.reshape(v.shape[0], -1)


def _natural_grads_ssd(g, e_heads):
    return {"ssd_w_in": jnp.concatenate([g["wz"], g["wxbc"], _flat2(_heads3(g["wdt"])[:, :, :e_heads])], axis=1),
            "ssd_w_out": g["wout"]}


def _natural_grads_fox(g, heads):
    return {"fox_w_in": jnp.concatenate([g["wqkvz"], g["wf"][:, :heads]], axis=1), "fox_w_out": g["wout"]}


def _natural_grads_mla(g):
    return {"mla_w_in": jnp.concatenate([g["wlat"], g["wkpe"][:, :MLA_ROPE], g["wz"]], axis=1),
            "mla_w_q_up": _flat2(jnp.concatenate([_heads3(g["wqn"]), _heads3(g["wqp"])[:, :, :MLA_ROPE]], axis=2)),
            "mla_w_kv_up": _flat2(jnp.concatenate([_heads3(g["wkn"]), _heads3(g["wv"])], axis=2)),
            "mla_w_out": g["wout"]}


def _natural_grads_s5(g):
    return {"s5_w_in": jnp.concatenate([g["wu"], g["wz"]], axis=1), "s5_w_glu": g["wglu"], "s5_w_out": g["wout"]}


def _sequence_step(x, positions, tgt, small, big_of_layer, gather_sides=(None, None), exchange_side=None,
                   ssd_in_sides=None, ssd_wout_of=None):
    cos, sin = _rope_tables(positions)
    exchange_side = exchange_side or (lambda nat: None)
    vjps = []

    def prepared(prep, big):
        p, vjp = jax.vjp(lambda s: prep({**big, **s}), small)
        vjps.append((vjp, tuple(p)))
        return p

    big0 = big_of_layer[0](None)
    p0 = prepared(_prep_ssd, big0)
    wout_of = (lambda got: big0["ssd_w_out"]) if "ssd_w_out" in big0 else ssd_wout_of
    h1, s0, got1 = _ssd_layer_fwd(x, p0, wout_of, side=gather_sides[0])
    p1 = prepared(_prep_fox, big_of_layer[1](got1))
    h2, s1, got23 = _fox_layer_fwd(h1, p1, side=gather_sides[1])
    p2 = prepared(_prep_mla, big_of_layer[2](got23))
    p3 = prepared(_prep_s5, big_of_layer[3](got23))
    p3["pw"] = _s5_powers(p3["lamv"])
    h3, s2 = _mla_layer_fwd(h2, p2, cos, sin)
    h4, s3 = _s5_layer_fwd(h3, p3)
    dh, loss = _loss_head(h4, tgt)
    dh, g3 = _s5_layer_bwd(s3, p3, dh)
    nat3 = _natural_grads_s5(g3)
    dh, g2, parts3 = _mla_layer_bwd(s2, p2, dh, cos, sin, side=exchange_side(nat3))
    nat2 = _natural_grads_mla(g2)
    dh, g1, parts2 = _fox_layer_bwd(s1, p1, dh, side=exchange_side(nat2))
    nat1 = _natural_grads_fox(g1, small["fox_f_bias"].shape[0])
    e_heads = small["ssd_dt_bias"].shape[0] // p0["dtb"].shape[0]
    dh, g0, parts1, parts_in = _ssd_layer_bwd(
        s0, p0, dh, side=lambda gw: exchange_side({**nat1, "ssd_w_out": gw}),
        in_sides=None if ssd_in_sides is None else
        lambda g: ssd_in_sides(_natural_grads_ssd(g, e_heads)["ssd_w_in"]))
    nat0 = _natural_grads_ssd(g0, e_heads)
    g_small = None
    for p, (vjp, keys), g in zip((p0, p1, p2, p3), vjps, (g0, g1, g2, g3)):
        (gi,) = vjp({k: g[k].astype(p[k].dtype) for k in keys})
        g_small = gi if g_small is None else jax.tree.map(jnp.add, g_small, gi)
    return loss, dh, g_small, (nat0, nat1, nat2, nat3), (parts1, parts2, parts3, parts_in)


FLAT_COLS = 1024
FLAT_ROW_ALIGN = 128


def _pack(arrs):
    flat = jnp.concatenate([a.reshape(-1) for a in arrs])
    unit = FLAT_COLS * FLAT_ROW_ALIGN
    return jnp.pad(flat, (0, -flat.shape[0] % unit)).reshape(-1, FLAT_COLS)


def _unpack(flat2d, shapes):
    flat = flat2d.reshape(-1)
    out, off = [], 0
    for s in shapes:
        sz = math.prod(s)
        out.append(flat[off:off + sz].reshape(s))
        off += sz
    return out


def _unpack_gathered(gathered, shapes, axes):
    flat = gathered.reshape(N_DEV, -1)
    out, off = [], 0
    for s, ax in zip(shapes, axes):
        sz = math.prod(s)
        seg = flat[:, off:off + sz].reshape((N_DEV,) + tuple(s))
        out.append(jnp.concatenate([seg[d] for d in range(N_DEV)], axis=ax)[0])
        off += sz
    return out


MESH = pl.DeviceIdType.MESH
HBM_SPEC = pl.BlockSpec(memory_space=pl.ANY)


def _comm_scratch(n):
    return [pltpu.SemaphoreType.DMA((n, 7)), pltpu.SemaphoreType.DMA((n, 7)), pltpu.SemaphoreType.DMA((n,))]


class _Gather:
    def __init__(self, xs):
        self.ins = list(xs)
        self.out_shape = [jax.ShapeDtypeStruct((N_DEV,) + v.shape, v.dtype) for v in xs]

    def _plan(self, x_refs, out_refs, sems):
        send_sems, recv_sems, local_sems = sems
        x, y, cc = lax.axis_index("x"), lax.axis_index("y"), lax.axis_index("c")
        me, sibling = (x, y, cc), (x, y, 1 - cc)
        chips = [(1 - x, y), (x, 1 - y), (1 - x, 1 - y)]

        def rows(w, px, py, pc):
            return out_refs[w].at[4 * px + 2 * py + pc]

        def copy(w, k, block, to, src=None):
            return pltpu.make_async_remote_copy(
                src_ref=rows(w, *block) if src is None else src, dst_ref=rows(w, *block),
                send_sem=send_sems.at[w, k], recv_sem=recv_sems.at[w, k], device_id=to, device_id_type=MESH)

        n = len(x_refs)
        mine = [pltpu.make_async_copy(x_refs[w], rows(w, *me), local_sems.at[w]) for w in range(n)]
        first = []
        for w in range(n):
            first.append(copy(w, 0, me, sibling, src=x_refs[w]))
            first += [copy(w, 1 + j, me, (*chip, cc), src=x_refs[w]) for j, chip in enumerate(chips)]
        return n, me, sibling, chips, cc, copy, mine, first

    def start(self, x_refs, out_refs, sems):
        *_, mine, first = self._plan(x_refs, out_refs, sems)
        for cp in mine + first:
            cp.start()

    def finish(self, x_refs, out_refs, sems):
        n, me, sibling, chips, cc, copy, mine, first = self._plan(x_refs, out_refs, sems)
        passed = []
        for w in range(n):
            for j, chip in enumerate(chips):
                copy(w, 1 + j, (*chip, cc), me).wait_recv()
                cp = copy(w, 4 + j, (*chip, cc), sibling)
                cp.start()
                passed.append(cp)
        for w in range(n):
            copy(w, 0, sibling, me).wait_recv()
            for j, chip in enumerate(chips):
                copy(w, 4 + j, (*chip, 1 - cc), me).wait_recv()
        for cp in first + passed:
            cp.wait_send()
        for cp in mine:
            cp.wait()


class _Exchange:
    def __init__(self, gs):
        self.ins = list(gs)
        self.out_shape = [jax.ShapeDtypeStruct(v.shape, v.dtype) for v in gs]

    def _plan(self, g_refs, out_refs, sems):
        send_sems, recv_sems, local_sems = sems
        x, y, cc = lax.axis_index("x"), lax.axis_index("y"), lax.axis_index("c")
        me = 4 * x + 2 * y + cc
        peers = []
        for k in range(1, N_DEV):
            px = 1 - x if (k >> 2) & 1 else x
            py = 1 - y if (k >> 1) & 1 else y
            pc = 1 - cc if k & 1 else cc
            peers.append((k, (px, py, pc), 4 * px + 2 * py + pc))

        def copy(w, k, peer, slot_src, slot_dst):
            return pltpu.make_async_remote_copy(
                src_ref=g_refs[w].at[slot_src], dst_ref=out_refs[w].at[slot_dst],
                send_sem=send_sems.at[w, k - 1], recv_sem=recv_sems.at[w, k - 1], device_id=peer, device_id_type=MESH)

        n = len(g_refs)
        mine = [pltpu.make_async_copy(g_refs[w].at[me], out_refs[w].at[me], local_sems.at[w]) for w in range(n)]
        sends = [copy(w, k, peer, idx, me) for w in range(n) for k, peer, idx in peers]
        return n, peers, copy, mine, sends

    def start(self, g_refs, out_refs, sems):
        *_, mine, sends = self._plan(g_refs, out_refs, sems)
        for cp in mine + sends:
            cp.start()

    def finish(self, g_refs, out_refs, sems):
        n, peers, copy, mine, sends = self._plan(g_refs, out_refs, sems)
        for w in range(n):
            for k, peer, idx in peers:
                copy(w, k, peer, idx, idx).wait_recv()
        for cp in sends:
            cp.wait_send()
        for cp in mine:
            cp.wait()


class _Both:
    def __init__(self, first, second):
        self.parts = (first, second)
        self.ins = first.ins + second.ins
        self.out_shape = first.out_shape + second.out_shape

    def _split(self, ins, outs, sems):
        n0 = len(self.parts[0].ins)
        n1 = len(self.parts[1].ins)
        for side, lo, n in ((self.parts[0], 0, n0), (self.parts[1], n0, n1)):
            yield side, ins[lo:lo + n], outs[lo:lo + n], tuple(s.at[pl.ds(lo, n)] for s in sems)

    def start(self, ins, outs, sems):
        for side, i, o, s in self._split(ins, outs, sems):
            side.start(i, o, s)

    def finish(self, ins, outs, sems):
        for side, i, o, s in self._split(ins, outs, sems):
            side.finish(i, o, s)


def _run_comm(side, *, name):
    n = len(side.ins)

    def body(*refs):
        ins, outs, sems = refs[:n], refs[n:2 * n], refs[2 * n:]
        side.start(ins, outs, sems)
        side.finish(ins, outs, sems)

    return pl.pallas_call(
        body, name=name, out_shape=side.out_shape,
        in_specs=[HBM_SPEC] * n, out_specs=[HBM_SPEC] * n, scratch_shapes=_comm_scratch(n),
    )(*side.ins)


def _hosted_call(body, side, *, name, grid, in_specs, out_specs, out_shape, scratch_shapes, args):
    out_specs, out_shape = list(out_specs), list(out_shape)
    if side is None:
        res = pl.pallas_call(
            body, name=name, grid=grid, in_specs=in_specs, out_specs=out_specs, out_shape=out_shape,
            scratch_shapes=scratch_shapes,
            compiler_params=pltpu.CompilerParams(dimension_semantics=("arbitrary",) * len(grid)))(*args)
        return res, None
    n_in, n_out, n_scr, ns = len(in_specs), len(out_specs), len(scratch_shapes), len(side.ins)

    def wrapped(*refs):
        ins, s_ins = refs[:n_in], refs[n_in:n_in + ns]
        outs = refs[n_in + ns:n_in + ns + n_out]
        s_outs = refs[n_in + ns + n_out:n_in + 2 * ns + n_out]
        scr = refs[n_in + 2 * ns + n_out:n_in + 2 * ns + n_out + n_scr]
        sems = refs[n_in + 2 * ns + n_out + n_scr:]
        ids = [pl.program_id(ax) for ax in range(len(grid))]
        first = functools.reduce(lambda p, q: p & q, [i == 0 for i in ids])
        last = functools.reduce(lambda p, q: p & q, [i == g - 1 for i, g in zip(ids, grid)])

        @pl.when(first)
        def _():
            side.start(s_ins, s_outs, sems)

        body(*ins, *outs, *scr)

        @pl.when(last)
        def _():
            side.finish(s_ins, s_outs, sems)

    res = pl.pallas_call(
        wrapped, name=name, grid=grid, in_specs=list(in_specs) + [HBM_SPEC] * ns,
        out_specs=out_specs + [HBM_SPEC] * ns, out_shape=out_shape + side.out_shape,
        scratch_shapes=list(scratch_shapes) + _comm_scratch(ns),
        compiler_params=pltpu.CompilerParams(dimension_semantics=("arbitrary",) * len(grid)))(*args, *side.ins)
    return res[:n_out], res[n_out:]


BIG = (("ssd_w_in", 2), ("ssd_w_out", 1), ("fox_w_in", 2), ("fox_w_out", 1), ("mla_w_in", 2),
       ("mla_w_q_up", 2), ("mla_w_kv_up", 2), ("mla_w_out", 1), ("s5_w_in", 2), ("s5_w_glu", 1), ("s5_w_out", 1))
LAYER_BIG = (("ssd_w_in", "ssd_w_out"), ("fox_w_in", "fox_w_out"),
             ("mla_w_in", "mla_w_q_up", "mla_w_kv_up", "mla_w_out"), ("s5_w_in", "s5_w_glu", "s5_w_out"))
SMALL = (("ssd_conv_w", 2), ("mla_q_norm", 1), ("mla_kv_norm", 1), ("s5_d", 1), ("s5_b_glu", 1))
REPLICATED = ("ln_g", "ln_b", "ssd_conv_b", "ssd_dt_bias", "ssd_a_log", "ssd_d", "ssd_norm_w", "fox_f_bias",
              "s5_lambda_re", "s5_lambda_im", "s5_log_step", "s5_b_re", "s5_b_im", "s5_c_re", "s5_c_im")
WEIGHT_ORDER = ("ln_g", "ln_b", "ssd_w_in", "ssd_conv_w", "ssd_conv_b", "ssd_dt_bias", "ssd_a_log", "ssd_d",
                "ssd_norm_w", "ssd_w_out", "fox_w_in", "fox_f_bias", "fox_w_out", "mla_w_in", "mla_q_norm",
                "mla_kv_norm", "mla_w_q_up", "mla_w_kv_up", "mla_w_out", "s5_w_in", "s5_lambda_re", "s5_lambda_im",
                "s5_log_step", "s5_b_re", "s5_b_im", "s5_c_re", "s5_c_im", "s5_d", "s5_w_glu", "s5_b_glu", "s5_w_out")


def _sum_adamw(parts, w, m, v, *, name):
    r, c = w.shape
    row_bytes = 2 * c * (N_DEV * parts.dtype.itemsize + 7 * 4)
    tr = _pick(r, [t for t in (256, 128, 64, 32, 16) if t * row_bytes <= VMEM_BLOCK_BUDGET])

    def body(p_ref, w_ref, m_ref, v_ref, g_ref, d_ref, m2_ref, v2_ref):
        gv = p_ref[0].astype(f32)
        for s in range(1, N_DEV):
            gv = gv + p_ref[s].astype(f32)
        g_ref[...] = gv
        m2 = ADAM_B1 * m_ref[...] + (1.0 - ADAM_B1) * gv
        v2 = ADAM_B2 * v_ref[...] + (1.0 - ADAM_B2) * jnp.square(gv)
        m_hat = m2 / (1.0 - ADAM_B1 ** ADAM_STEP)
        v_hat = v2 / (1.0 - ADAM_B2 ** ADAM_STEP)
        d_ref[...] = -ADAM_LR * (m_hat / (jnp.sqrt(v_hat) + ADAM_EPS) + ADAM_WD * w_ref[...])
        m2_ref[...] = m2
        v2_ref[...] = v2

    spec = pl.BlockSpec((tr, c), lambda i: (i, 0))
    shp = jax.ShapeDtypeStruct((r, c), f32)
    return pl.pallas_call(
        body, name=name, grid=(r // tr,),
        in_specs=[pl.BlockSpec((N_DEV, tr, c), lambda i: (0, i, 0))] + [spec] * 3,
        out_specs=[spec] * 4, out_shape=[shp] * 4,
        compiler_params=pltpu.CompilerParams(dimension_semantics=("parallel",)),
    )(parts, w, m, v)


def _train_step(a):
    small_names, small_axes = [n for n, _ in SMALL], [ax for _, ax in SMALL]
    small_shapes = [a[n].shape for n in small_names]
    rep_shapes = [a[n].shape for n in REPLICATED]

    big_axis = dict(BIG)

    def shards(names):
        return [a[n][0].astype(bf16) for n in names]

    def natural(names, gathered):
        w = {}
        for n, g in zip(names, gathered):
            if big_axis[n] == 1:
                w[n] = g.reshape(-1, g.shape[2])
            else:
                w[n] = jnp.transpose(g, (1, 0, 2)).reshape(g.shape[1], -1)
        return w

    def to_owner(nat):
        blocks = []
        for n, g in nat.items():
            if big_axis[n] == 1:
                g = g.reshape(N_DEV, -1, g.shape[1])
            else:
                g = jnp.transpose(g.reshape(g.shape[0], N_DEV, -1), (1, 0, 2))
            blocks.append(g.astype(bf16))
        return blocks

    small_flat = _pack([a[n] for n in small_names])
    got0 = _run_comm(_Gather(shards(("ssd_w_in",)) + [small_flat]), name="gather_ssd_weights")
    small = dict(zip(small_names, _unpack_gathered(got0[-1], small_shapes, small_axes)))
    for n in REPLICATED:
        small[n] = a[n] if n in ("ln_g", "ln_b") else a[n][0]
    rest_names = LAYER_BIG[2] + LAYER_BIG[3]
    under_scan = LAYER_BIG[1] + ("ssd_w_out",)
    big_of_layer = (lambda got: natural(("ssd_w_in",), got0[:1]),
                    lambda got: natural(LAYER_BIG[1], got[:len(LAYER_BIG[1])]),
                    lambda got: natural(LAYER_BIG[2], got[:len(LAYER_BIG[2])]),
                    lambda got: natural(LAYER_BIG[3], got[len(LAYER_BIG[2]):]))
    def ssd_in_sides(nat_w_in):
        (blocks,) = to_owner({"ssd_w_in": nat_w_in})
        cut = blocks.shape[1] * 5 // 8
        return _Exchange([blocks[:, :cut]]), _Exchange([blocks[:, cut:]])

    loss, grad_x, g_small, nats, hosted_parts = _sequence_step(
        a["x"][0], a["positions"][0], a["loss_target"][0], small, big_of_layer,
        gather_sides=(_Gather(shards(under_scan)), _Gather(shards(rest_names))),
        exchange_side=lambda nat: _Exchange(to_owner(nat)), ssd_in_sides=ssd_in_sides,
        ssd_wout_of=lambda got: natural(("ssd_w_out",), got[-1:])["ssd_w_out"])
    parts_in = hosted_parts[3]
    parts_of = {"ssd_w_in": jnp.concatenate([parts_in[0][0], parts_in[1][0]], axis=1)}
    for names, parts in zip((LAYER_BIG[1] + ("ssd_w_out",), LAYER_BIG[2], LAYER_BIG[3]), hosted_parts[:3]):
        parts_of.update(zip(names, parts, strict=True))
    small_to_owner = jax.linear_transpose(
        lambda s: _unpack_gathered(s, small_shapes, small_axes), jax.ShapeDtypeStruct(got0[-1].shape, f32))
    (g_small_blocks,) = small_to_owner([g_small[n] for n in small_names])
    g_rep_local = _pack([g_small[n] if n in ("ln_g", "ln_b") else g_small[n][None] for n in REPLICATED])
    small_parts, g_rep_all = _run_comm(_Both(_Exchange([g_small_blocks]), _Gather([g_rep_local])),
                                       name="exchange_small_gradients")

    out = {"grad": {}, "delta": {}, "new_m": {}, "new_v": {}}
    kinds = ("grad", "delta", "new_m", "new_v")
    for n, _ in BIG:
        res = _sum_adamw(parts_of[n], a[n][0], a["m_" + n][0], a["v_" + n][0], name="adamw_" + n)
        for kind, r in zip(kinds, res):
            out[kind][n] = r[None]
    upd_small = _sum_adamw(small_parts, small_flat, _pack([a["m_" + n] for n in small_names]),
                           _pack([a["v_" + n] for n in small_names]), name="adamw_small_sharded")
    for kind, r in zip(kinds, upd_small):
        out[kind].update(zip(small_names, _unpack(r, small_shapes)))
    upd_rep = _sum_adamw(g_rep_all, _pack([a[n] for n in REPLICATED]), _pack([a["m_" + n] for n in REPLICATED]),
                         _pack([a["v_" + n] for n in REPLICATED]), name="adamw_replicated")
    for kind, r in zip(kinds, upd_rep):
        out[kind].update(zip(REPLICATED, _unpack(r, rep_shapes)))
    loss_total = lax.psum(loss[0, 0], ("x", "y", "c"))
    return (loss_total, grad_x[None],
            *[out[kind][n] for kind in ("grad", "delta", "new_m", "new_v") for n in WEIGHT_ORDER])


def kernel(x, positions, ln_g, ln_b, ssd_w_in, ssd_conv_w, ssd_conv_b, ssd_dt_bias, ssd_a_log, ssd_d, ssd_norm_w, ssd_w_out, fox_w_in, fox_f_bias, fox_w_out, mla_w_in, mla_q_norm, mla_kv_norm, mla_w_q_up, mla_w_kv_up, mla_w_out, s5_w_in, s5_lambda_re, s5_lambda_im, s5_log_step, s5_b_re, s5_b_im, s5_c_re, s5_c_im, s5_d, s5_w_glu, s5_b_glu, s5_w_out, loss_target, m_ln_g, m_ln_b, m_ssd_w_in, m_ssd_conv_w, m_ssd_conv_b, m_ssd_dt_bias, m_ssd_a_log, m_ssd_d, m_ssd_norm_w, m_ssd_w_out, m_fox_w_in, m_fox_f_bias, m_fox_w_out, m_mla_w_in, m_mla_q_norm, m_mla_kv_norm, m_mla_w_q_up, m_mla_w_kv_up, m_mla_w_out, m_s5_w_in, m_s5_lambda_re, m_s5_lambda_im, m_s5_log_step, m_s5_b_re, m_s5_b_im, m_s5_c_re, m_s5_c_im, m_s5_d, m_s5_w_glu, m_s5_b_glu, m_s5_w_out, v_ln_g, v_ln_b, v_ssd_w_in, v_ssd_conv_w, v_ssd_conv_b, v_ssd_dt_bias, v_ssd_a_log, v_ssd_d, v_ssd_norm_w, v_ssd_w_out, v_fox_w_in, v_fox_f_bias, v_fox_w_out, v_mla_w_in, v_mla_q_norm, v_mla_kv_norm, v_mla_w_q_up, v_mla_w_kv_up, v_mla_w_out, v_s5_w_in, v_s5_lambda_re, v_s5_lambda_im, v_s5_log_step, v_s5_b_re, v_s5_b_im, v_s5_c_re, v_s5_c_im, v_s5_d, v_s5_w_glu, v_s5_b_glu, v_s5_w_out):
    return _train_step(dict(locals()))
```

```python
import functools
import math

import jax
import jax.numpy as jnp
from jax import lax
from jax.experimental import pallas as pl
from jax.experimental.pallas import tpu as pltpu

f32 = jnp.float32
bf16 = jnp.bfloat16

N_DEV = 8
DEPTH = 4
ALPHA = (2.0 * DEPTH) ** 0.25
LN_EPS = 1e-5
RMS_EPS = 1e-6
LANE = 128

SSD_STATE = 128
SSD_HEADDIM = 64
SSD_CHUNK = 128
FOX_HEAD_DIM = 128
MLA_NOPE = 128
MLA_ROPE = 64
MLA_V = 128
ROPE_BASE = 10000.0
S5_GROUP = 16
S5_BLOCK_GROUPS = 8
S5_CHUNK = 128

ADAM_LR = 0.001
ADAM_B1 = 0.9
ADAM_B2 = 0.999
ADAM_EPS = 1e-08
ADAM_WD = 0.01
ADAM_STEP = 10

VMEM_BLOCK_BUDGET = 20 * 1024 * 1024
MM_FULL_K = 2048
HIGHEST = lax.Precision.HIGHEST


def _pick(n, cands):
    for c in cands:
        if n % c == 0:
            return c
    return n


def _dg(a, b, ca, cb):
    return lax.dot_general(a.astype(bf16), b.astype(bf16), (((ca,), (cb,)), ((), ())),
                           preferred_element_type=f32)


@functools.partial(jax.custom_vjp, nondiff_argnums=(2, 3))
def bdot(a, b, ca, cb):
    return _dg(a, b, ca, cb)


def _bdot_fwd(a, b, ca, cb):
    return _dg(a, b, ca, cb), (a, b)


def _bdot_bwd(ca, cb, res, g):
    a, b = res
    nb = 1 - cb
    ma = 1 - ca
    da = _dg(g, b, 1, nb) if ca == 1 else _dg(b, g, nb, 1)
    db = _dg(a, g, ma, 0) if cb == 0 else _dg(g, a, 0, ma)
    return da, db


bdot.defvjp(_bdot_fwd, _bdot_bwd)


def _silu(x):
    return x * jax.nn.sigmoid(x)


def _softplus(x):
    return jnp.maximum(x, 0.0) + jnp.log(1.0 + jnp.exp(-jnp.abs(x)))


def _mm(a, b, *, ta=False, tb=False, add=None, out_dtype=f32, name, side=None):
    m, k = (a.shape[1], a.shape[0]) if ta else a.shape
    n = b.shape[0] if tb else b.shape[1]
    kb = b.shape[1] if tb else b.shape[0]
    assert k == kb, (name, a.shape, b.shape)
    if k <= MM_FULL_K:
        tm = _pick(m, (512, 256, 128))
        tn = _pick(n, (1024, 512, 384, 256, 128))
        tk = k
    else:
        tm = _pick(m, (1024, 512, 256, 128))
        tn = _pick(n, (1024, 512, 384, 256, 128))
        tk = _pick(k, (1024, 512, 256, 128))
    nk = k // tk
    has_add = add is not None

    def body(*refs):
        if has_add:
            a_ref, b_ref, add_ref, o_ref, acc = refs
        else:
            a_ref, b_ref, o_ref, acc = refs
        kk = pl.program_id(2)

        def prod():
            return _dg(a_ref[...], b_ref[...], 0 if ta else 1, 1 if tb else 0)

        def finish(r):
            o_ref[...] = (r + add_ref[...] if has_add else r).astype(o_ref.dtype)

        if nk == 1:
            finish(prod())
            return

        @pl.when(kk == 0)
        def _():
            acc[...] = prod()

        @pl.when((kk > 0) & (kk < nk - 1))
        def _():
            acc[...] += prod()

        @pl.when(kk == nk - 1)
        def _():
            finish(acc[...] + prod())

    a_spec = (pl.BlockSpec((tk, tm), lambda i, j, kk: (kk, i)) if ta
              else pl.BlockSpec((tm, tk), lambda i, j, kk: (i, kk)))
    b_spec = (pl.BlockSpec((tn, tk), lambda i, j, kk: (j, kk)) if tb
              else pl.BlockSpec((tk, tn), lambda i, j, kk: (kk, j)))
    o_spec = pl.BlockSpec((tm, tn), lambda i, j, kk: (i, j))
    in_specs = [a_spec, b_spec] + ([o_spec] if has_add else [])
    args = (a, b) + ((add,) if has_add else ())
    if side is not None:
        (out,), got = _hosted_call(
            body, side, name=name, grid=(m // tm, n // tn, nk), in_specs=in_specs, out_specs=[o_spec],
            out_shape=[jax.ShapeDtypeStruct((m, n), out_dtype)], scratch_shapes=[pltpu.VMEM((tm, tn), f32)],
            args=args)
        return out, got
    return pl.pallas_call(
        body, name=name, grid=(m // tm, n // tn, nk),
        in_specs=in_specs, out_specs=o_spec,
        out_shape=jax.ShapeDtypeStruct((m, n), out_dtype),
        scratch_shapes=[pltpu.VMEM((tm, tn), f32)],
        compiler_params=pltpu.CompilerParams(dimension_semantics=("parallel", "parallel", "arbitrary")),
    )(*args)


def _row_operand(op):
    if isinstance(op, tuple):
        arr, w, cb = op
    else:
        arr, w, cb = op, op.shape[1], 0
    return arr, w, cb


def _rw_tm(t, widths):
    per_row = 2 * 4 * sum(widths)
    tm = 512
    while tm > 8 and (tm * per_row > VMEM_BLOCK_BUDGET or t % tm):
        tm //= 2
    return tm


def _rw_specs(ops, tm):
    specs, arrs = [], []
    for op in ops:
        arr, w, cb = _row_operand(op)
        specs.append(pl.BlockSpec((tm, w), functools.partial(lambda i, cb: (i, cb), cb=cb)))
        arrs.append(arr)
    return specs, arrs


def _param_spec(p):
    nd = p.ndim
    return pl.BlockSpec(p.shape, lambda i: (0,) * nd)


def _rw_fwd(fn, rows, params, out_widths, *, name):
    t = _row_operand(rows[0])[0].shape[0]
    tm = _rw_tm(t, [_row_operand(r)[1] for r in rows] + list(out_widths))
    nr, npar = len(rows), len(params)

    def body(*refs):
        vals = [r[...] for r in refs[:nr + npar]]
        outs = fn(*vals)
        for o_ref, v in zip(refs[nr + npar:], outs):
            o_ref[...] = v

    rspecs, rarrs = _rw_specs(rows, tm)
    return pl.pallas_call(
        body, name=name, grid=(t // tm,),
        in_specs=rspecs + [_param_spec(p) for p in params],
        out_specs=[pl.BlockSpec((tm, w), lambda i: (i, 0)) for w in out_widths],
        out_shape=[jax.ShapeDtypeStruct((t, w), f32) for w in out_widths],
        compiler_params=pltpu.CompilerParams(dimension_semantics=("parallel",)),
    )(*rarrs, *params)


def _rw_bwd(fn, rows, params, cots, *, name, n_const=0):
    t = _row_operand(rows[0])[0].shape[0]
    nr, npar, nc = len(rows), len(params), len(cots)
    nd = nr - n_const
    widths = [_row_operand(r)[1] for r in rows]
    tm = _rw_tm(t, widths + widths[:nd] + [c.shape[1] for c in cots])

    def body(*refs):
        rv = [r[...] for r in refs[:nr]]
        pv = [r[...] for r in refs[nr:nr + npar]]
        cv = [r[...] for r in refs[nr + npar:nr + npar + nc]]
        outs = refs[nr + npar + nc:]
        consts = rv[nd:]

        def f(*diff):
            return fn(*diff[:nd], *consts, *diff[nd:])

        _, vjp = jax.vjp(f, *rv[:nd], *pv)
        g = vjp(tuple(cv))
        for o_ref, v in zip(outs[:nd], g[:nd]):
            o_ref[...] = v
        if npar:
            @pl.when(pl.program_id(0) == 0)
            def _():
                for o_ref in outs[nd:]:
                    o_ref[...] = jnp.zeros_like(o_ref)

            for o_ref, v in zip(outs[nd:], g[nd:]):
                o_ref[...] += v

    rspecs, rarrs = _rw_specs(rows, tm)
    cspecs = [pl.BlockSpec((tm, c.shape[1]), lambda i: (i, 0)) for c in cots]
    return pl.pallas_call(
        body, name=name, grid=(t // tm,),
        in_specs=rspecs + [_param_spec(p) for p in params] + cspecs,
        out_specs=[pl.BlockSpec((tm, w), lambda i: (i, 0)) for w in widths[:nd]]
        + [_param_spec(p) for p in params],
        out_shape=[jax.ShapeDtypeStruct((t, w), f32) for w in widths[:nd]]
        + [jax.ShapeDtypeStruct(p.shape, f32) for p in params],
        compiler_params=pltpu.CompilerParams(dimension_semantics=("arbitrary",)),
    )(*rarrs, *params, *cots)


def _ln_fn(h, o, g, b):
    x = ALPHA * h + o
    mu = jnp.mean(x, -1, keepdims=True)
    var = jnp.mean(jnp.square(x - mu), -1, keepdims=True)
    return ((x - mu) * lax.rsqrt(var + LN_EPS) * g + b,)


def _gate_fn(o, z):
    return (o * _silu(z),)


def _make_ssd_post_fn(groups):
    def fn(y, z, nw):
        yz = y * _silu(z)
        gw = yz.shape[1] // groups
        outs = []
        for g in range(groups):
            blk = yz[:, g * gw:(g + 1) * gw]
            outs.append(blk * lax.rsqrt(jnp.mean(jnp.square(blk), -1, keepdims=True) + RMS_EPS))
        return (jnp.concatenate(outs, axis=1) * nw,)
    return fn


def _logf_fn(f, b):
    return (jax.nn.log_sigmoid(f + b),)


def _make_mla_norm_fn(rank):
    def fn(lat, qn, kvn):
        def rms(x, w):
            return x * lax.rsqrt(jnp.mean(jnp.square(x), -1, keepdims=True) + RMS_EPS) * w
        return rms(lat[:, :rank], qn), rms(lat[:, rank:], kvn)
    return fn


def _rope_block(xb, cos, sin):
    half = MLA_ROPE // 2
    x1, x2 = xb[:, :half], xb[:, half:MLA_ROPE]
    return jnp.concatenate([x1 * cos - x2 * sin, x1 * sin + x2 * cos, xb[:, MLA_ROPE:]], axis=1)


def _make_rope_fn(heads):
    def fn(x, cos, sin):
        return (jnp.concatenate([_rope_block(x[:, h * LANE:(h + 1) * LANE], cos, sin)
                                 for h in range(heads)], axis=1),)
    return fn


def _s5_act_fn(ys, u, d):
    return (jax.nn.gelu(ys + d * u),)


def _s5_glu_fn(y1, gp, z, bg):
    return (y1 * jax.nn.sigmoid(gp + bg) * _silu(z),)


def _loss_head(y, tgt):
    t, d = y.shape
    tm = _rw_tm(t, [d, d, d])

    def body(y_ref, t_ref, dy_ref, l_ref):
        @pl.when(pl.program_id(0) == 0)
        def _():
            l_ref[...] = jnp.zeros_like(l_ref)

        e = y_ref[...] - t_ref[...]
        dy_ref[...] = e * (1.0 / d)
        l_ref[...] += 0.5 * jnp.sum(jnp.mean(jnp.square(e), axis=-1, keepdims=True), axis=0, keepdims=True)

    spec = pl.BlockSpec((tm, d), lambda i: (i, 0))
    return pl.pallas_call(
        body, name="loss_head", grid=(t // tm,), in_specs=[spec, spec],
        out_specs=[spec, pl.BlockSpec((1, 1), lambda i: (0, 0))],
        out_shape=[jax.ShapeDtypeStruct((t, d), f32), jax.ShapeDtypeStruct((1, 1), f32)],
        compiler_params=pltpu.CompilerParams(dimension_semantics=("arbitrary",)),
    )(y, tgt)


def _attn_tile(t):
    return _pick(t, (512, 256, 128))


ATTN_HEADS_PER_STEP = 8
ATTN_BWD_HEADS_PER_STEP = 4
SSD_GROUPS_PER_STEP = 1
S5_BLOCKS_PER_STEP = 2


def _logits_t(k1, q1, k2, q2, ck_col, scale, key0, query0):
    s = _dg(k1, q1, 1, 1)
    if q2 is not None:
        s = s + _dg(k2, q2, 1, 1)
    s = s * scale
    if ck_col is not None:
        s = s - ck_col
    key = key0 + lax.broadcasted_iota(jnp.int32, s.shape, 0)
    query = query0 + lax.broadcasted_iota(jnp.int32, s.shape, 1)
    return jnp.where(key <= query, s, -jnp.inf)


def _lane_bcast(v):
    return jnp.broadcast_to(v[:, :, None], v.shape + (LANE,))


def _attn_fwd(q1, q1o, k1, k1o, v, vo, q2, k2, ck, *, heads, scale, name, side=None):
    t = q1.shape[0]
    tq = tk = _attn_tile(t)
    nq = t // tq
    has2, hasb = q2 is not None, ck is not None
    hp = math.gcd(ATTN_HEADS_PER_STEP, heads)
    wd = hp * LANE
    assert heads % hp == 0 and q1o % hp == 0 and k1o % hp == 0 and vo % hp == 0

    def body(*refs):
        it = iter(refs)
        q1r, k1r, vr = next(it), next(it), next(it)
        q2r, k2r = (next(it), next(it)) if has2 else (None, None)
        ckr = next(it) if hasb else None
        o_r, lse_r, m_s, l_s, acc = next(it), next(it), next(it), next(it), next(it)
        i, j = pl.program_id(1), pl.program_id(2)

        @pl.when(j == 0)
        def _():
            m_s[...] = jnp.full_like(m_s, -jnp.inf)
            l_s[...] = jnp.zeros_like(l_s)
            acc[...] = jnp.zeros_like(acc)

        @pl.when(j <= i)
        def _():
            k2v = k2r[...].astype(bf16) if has2 else None
            for e in range(hp):
                ln = slice(e * LANE, (e + 1) * LANE)
                q1v, k1v, vv = (r[:, ln].astype(bf16) for r in (q1r, k1r, vr))
                q2v = q2r[:, ln].astype(bf16) if has2 else None
                s = _logits_t(k1v, q1v, k2v, q2v, ckr[e][:, 0:1] if hasb else None, scale, j * tk, i * tq)
                m_prev = m_s[e]
                m_new = jnp.maximum(m_prev, jnp.max(s, axis=0, keepdims=True))
                p = jnp.exp(s - m_new)
                a = jnp.exp(m_prev - m_new)
                l_s[e] = a * l_s[e] + jnp.sum(p, axis=0, keepdims=True)
                acc[e] = a * acc[e] + _dg(vv, p, 0, 0)
                m_s[e] = m_new

        @pl.when(j == i)
        def _():
            for e in range(hp):
                o_r[:, e * LANE:(e + 1) * LANE] = (acc[e] / l_s[e]).T
                lse_r[e, 0] = m_s[e] + jnp.log(l_s[e])

    def qmap(off):
        return lambda h, i, j: (i, off // hp + h)

    def kmap(off):
        return lambda h, i, j: (jnp.minimum(j, i), off // hp + h)

    in_specs = [pl.BlockSpec((tq, wd), qmap(q1o)), pl.BlockSpec((tk, wd), kmap(k1o)),
                pl.BlockSpec((tk, wd), kmap(vo))]
    args = [q1, k1, v]
    if has2:
        in_specs += [pl.BlockSpec((tq, wd), qmap(0)),
                     pl.BlockSpec((tk, LANE), lambda h, i, j: (jnp.minimum(j, i), 0))]
        args += [q2, k2]
    if hasb:
        in_specs += [pl.BlockSpec((hp, tk, LANE), lambda h, i, j: (h, jnp.minimum(j, i), 0))]
        args += [ck]
    return _hosted_call(
        body, side, name=name, grid=(heads // hp, nq, nq), in_specs=in_specs,
        out_specs=[pl.BlockSpec((tq, wd), lambda h, i, j: (i, h)),
                   pl.BlockSpec((hp, 1, 1, tq), lambda h, i, j: (h, i, 0, 0))],
        out_shape=[jax.ShapeDtypeStruct((t, heads * LANE), f32),
                   jax.ShapeDtypeStruct((heads, nq, 1, tq), f32)],
        scratch_shapes=[pltpu.VMEM((hp, 1, tq), f32), pltpu.VMEM((hp, 1, tq), f32),
                        pltpu.VMEM((hp, LANE, tq), f32)],
        args=args)


def _row_sums_as_row(x, first_lane_only=False):
    sel = jnp.ones((8, x.shape[1]), f32)
    if first_lane_only:
        sel = (lax.broadcasted_iota(jnp.int32, sel.shape, 1) == 0).astype(f32)
    return lax.dot_general(sel, x, (((1,), (1,)), ((), ())), precision=HIGHEST, preferred_element_type=f32)[0:1]


def _attn_delta(do, o, *, heads, name):
    t = do.shape[0]
    tq = _attn_tile(t)
    hp = math.gcd(ATTN_HEADS_PER_STEP, heads)

    def body(do_ref, o_ref, d_ref):
        for e in range(hp):
            ln = slice(e * LANE, (e + 1) * LANE)
            d_ref[e, 0] = _row_sums_as_row(do_ref[:, ln] * o_ref[:, ln])

    spec = pl.BlockSpec((tq, hp * LANE), lambda i, h: (i, h))
    return pl.pallas_call(
        body, name=name, grid=(t // tq, heads // hp), in_specs=[spec, spec],
        out_specs=pl.BlockSpec((hp, 1, 1, tq), lambda i, h: (h, i, 0, 0)),
        out_shape=jax.ShapeDtypeStruct((heads, t // tq, 1, tq), f32),
        compiler_params=pltpu.CompilerParams(dimension_semantics=("parallel", "parallel")),
    )(do, o)


def _attn_bwd(q1, q1o, k1, k1o, v, vo, q2, k2, ck, do, lse, delta, *, heads, scale, name, side=None):
    t = q1.shape[0]
    tq = tk = _attn_tile(t)
    nq = t // tq
    has2, hasb = q2 is not None, ck is not None
    hp = ATTN_BWD_HEADS_PER_STEP // 2 if has2 else ATTN_BWD_HEADS_PER_STEP
    wd = hp * LANE
    assert heads % hp == 0 and q1o % hp == 0 and k1o % hp == 0 and vo % hp == 0

    def body(*refs):
        it = iter(refs)
        q1r, k1r, vr = next(it), next(it), next(it)
        q2r, k2r = (next(it), next(it)) if has2 else (None, None)
        ckr = next(it) if hasb else None
        dor, lser, dlr = next(it), next(it), next(it)
        dq1r, dk1r, dvr = next(it), next(it), next(it)
        dq2r, dk2r = (next(it), next(it)) if has2 else (None, None)
        drowr, dckr = (next(it), next(it)) if hasb else (None, None)
        ak, av = next(it), next(it)
        ac = next(it) if hasb else None
        h, j, i = pl.program_id(0), pl.program_id(1), pl.program_id(2)

        @pl.when((j == 0) & (i == 0))
        def _():
            dq1r[...] = jnp.zeros_like(dq1r)
            if has2:
                dq2r[...] = jnp.zeros_like(dq2r)
            if hasb:
                drowr[...] = jnp.zeros_like(drowr)

        if has2:
            @pl.when((h == 0) & (j == 0) & (i == 0))
            def _():
                dk2r[...] = jnp.zeros_like(dk2r)

        @pl.when(i == j)
        def _():
            ak[...] = jnp.zeros_like(ak)
            av[...] = jnp.zeros_like(av)
            if hasb:
                ac[...] = jnp.zeros_like(ac)

        @pl.when(i >= j)
        def _():
            rows = pl.ds(pl.multiple_of(i * tq, tq), tq)
            cols = pl.ds(pl.multiple_of(j * tk, tk), tk)
            k2v = k2r[...].astype(bf16) if has2 else None
            for e in range(hp):
                ln = slice(e * LANE, (e + 1) * LANE)
                q1v, k1v, vv, dov = (r[:, ln].astype(bf16) for r in (q1r, k1r, vr, dor))
                q2v = q2r[:, ln].astype(bf16) if has2 else None
                s = _logits_t(k1v, q1v, k2v, q2v, ckr[e][:, 0:1] if hasb else None, scale, j * tk, i * tq)
                p = jnp.exp(s - lser[e, 0])
                dp = _dg(vv, dov, 1, 1)
                dsr = p * (dp - dlr[e, 0])
                ds = (dsr * scale).astype(bf16)
                av[:, ln] += _dg(p, dov, 1, 0)
                ak[:, ln] += _dg(ds, q1v, 1, 0)
                dq1r[rows, ln] += _dg(ds, k1v, 0, 0)
                if has2:
                    dq2r[rows, ln] += _dg(ds, k2v, 0, 0)
                    dk2r[cols, :] += _dg(ds, q2v, 1, 0)
                if hasb:
                    drowr[e, pl.ds(i, 1), :] += jnp.sum(dsr, axis=0, keepdims=True)
                    ac[e] -= jnp.broadcast_to(jnp.sum(dsr, axis=1, keepdims=True), (tk, LANE))

        @pl.when(i == nq - 1)
        def _():
            dk1r[...] = ak[...]
            dvr[...] = av[...]
            if hasb:
                for e in range(hp):
                    dckr[e] = _row_sums_as_row(ac[e], first_lane_only=True)

    def qmap(off):
        return lambda h, j, i: (jnp.maximum(i, j), off // hp + h)

    def kmap(off):
        return lambda h, j, i: (j, off // hp + h)

    in_specs = [pl.BlockSpec((tq, wd), qmap(q1o)), pl.BlockSpec((tk, wd), kmap(k1o)),
                pl.BlockSpec((tk, wd), kmap(vo))]
    args = [q1, k1, v]
    if has2:
        in_specs += [pl.BlockSpec((tq, wd), qmap(0)), pl.BlockSpec((tk, LANE), lambda h, j, i: (j, 0))]
        args += [q2, k2]
    if hasb:
        in_specs += [pl.BlockSpec((hp, tk, LANE), lambda h, j, i: (h, j, 0))]
        args += [ck]
    row3 = pl.BlockSpec((hp, 1, 1, tq), lambda h, j, i: (h, jnp.maximum(i, j), 0, 0))
    in_specs += [pl.BlockSpec((tq, wd), qmap(0)), row3, row3]
    args += [do, lse, delta]
    hd = jax.ShapeDtypeStruct((t, heads * LANE), f32)
    head_cols = pl.BlockSpec((t, wd), lambda h, j, i: (0, h))
    kv_blk = pl.BlockSpec((tk, wd), kmap(0))
    out_specs, out_shape = [head_cols, kv_blk, kv_blk], [hd, hd, hd]
    scratch = [pltpu.VMEM((tk, wd), f32)] * 2
    if has2:
        out_specs += [head_cols, pl.BlockSpec((t, LANE), lambda h, j, i: (0, 0))]
        out_shape += [hd, jax.ShapeDtypeStruct((t, LANE), f32)]
    if hasb:
        out_specs += [pl.BlockSpec((hp, nq, tq), lambda h, j, i: (h, 0, 0)),
                      pl.BlockSpec((hp, 1, tk), lambda h, j, i: (h, 0, j))]
        out_shape += [jax.ShapeDtypeStruct((heads, nq, tq), f32), jax.ShapeDtypeStruct((heads, 1, t), f32)]
        scratch.append(pltpu.VMEM((hp, tk, LANE), f32))
    return _hosted_call(body, side, name=name, grid=(heads // hp, nq, nq), in_specs=in_specs, out_specs=out_specs,
                        out_shape=out_shape, scratch_shapes=scratch, args=args)


def _cumsum_rows(x, *, reverse, name):
    t, w = x.shape
    blk = 128
    nb = t // blk

    def body(x_ref, o_ref):
        r = lax.broadcasted_iota(jnp.int32, (blk, blk), 0)
        c = lax.broadcasted_iota(jnp.int32, (blk, blk), 1)
        tri = ((r <= c) if reverse else (r >= c)).astype(f32)
        carry = jnp.zeros((1, w), f32)
        for b in (reversed(range(nb)) if reverse else range(nb)):
            seg = x_ref[b * blk:(b + 1) * blk, :]
            cs = jnp.dot(tri, seg, precision=HIGHEST, preferred_element_type=f32) + carry
            o_ref[b * blk:(b + 1) * blk, :] = cs
            carry = cs[0:1, :] if reverse else cs[blk - 1:blk, :]

    return pl.pallas_call(body, name=name, out_shape=jax.ShapeDtypeStruct((t, w), f32))(x)


CONV_PAD = 8


def _conv_fn(x, w, b):
    t, taps = x.shape[0], w.shape[0]
    xx = jnp.concatenate([jnp.zeros((CONV_PAD, x.shape[1]), f32), x], axis=0)
    y = b
    for k in range(taps):
        off = CONV_PAD - (taps - 1) + k
        y = y + w[k:k + 1, :] * xx[off:off + t, :]
    return _silu(y)


def _conv_fwd(x, w, b, *, name):
    t, c = x.shape
    tc = LANE
    taps = w.shape[0]

    def body(x_ref, w_ref, b_ref, o_ref):
        o_ref[...] = _conv_fn(x_ref[...], w_ref[...], b_ref[...])

    xs = pl.BlockSpec((t, tc), lambda i: (0, i))
    return pl.pallas_call(
        body, name=name, grid=(c // tc,),
        in_specs=[xs, pl.BlockSpec((taps, tc), lambda i: (0, i)), pl.BlockSpec((1, tc), lambda i: (0, i))],
        out_specs=xs, out_shape=jax.ShapeDtypeStruct((t, c), f32),
        compiler_params=pltpu.CompilerParams(dimension_semantics=("parallel",)),
    )(x, w, b)


def _conv_bwd(x, w, b, ds, *, name):
    t, c = x.shape
    tc = LANE
    taps = w.shape[0]

    rb = _pick(t, (512, 256, 128))
    assert taps - 1 <= CONV_PAD

    def body(x_ref, w_ref, b_ref, ds_ref, dx_ref, dw_ref, db_ref, xpad, dypad):
        xpad[0:CONV_PAD, :] = jnp.zeros((CONV_PAD, tc), f32)
        xpad[CONV_PAD:CONV_PAD + t, :] = x_ref[...]
        dypad[t:t + CONV_PAD, :] = jnp.zeros((CONV_PAD, tc), f32)
        wv, bv = w_ref[...], b_ref[...]
        dw = [jnp.zeros((1, tc), f32) for _ in range(taps)]
        db = jnp.zeros((1, tc), f32)
        for r0 in range(0, t, rb):
            first = r0 + CONV_PAD - (taps - 1)
            xsh = [xpad[first + k:first + k + rb, :] for k in range(taps)]
            y = bv
            for k in range(taps):
                y = y + wv[k:k + 1, :] * xsh[k]
            sig = jax.nn.sigmoid(y)
            dy = ds_ref[r0:r0 + rb, :] * (sig * (1.0 + y * (1.0 - sig)))
            dypad[r0:r0 + rb, :] = dy
            for k in range(taps):
                dw[k] = dw[k] + jnp.sum(dy * xsh[k], axis=0, keepdims=True)
            db = db + jnp.sum(dy, axis=0, keepdims=True)
        for r0 in range(0, t, rb):
            dx = jnp.zeros((rb, tc), f32)
            for k in range(taps):
                first = r0 + (taps - 1) - k
                dx = dx + wv[k:k + 1, :] * dypad[first:first + rb, :]
            dx_ref[r0:r0 + rb, :] = dx
        dw_ref[...] = jnp.concatenate(dw, axis=0)
        db_ref[...] = db

    xs = pl.BlockSpec((t, tc), lambda i: (0, i))
    ws = pl.BlockSpec((taps, tc), lambda i: (0, i))
    bs = pl.BlockSpec((1, tc), lambda i: (0, i))
    return pl.pallas_call(
        body, name=name, grid=(c // tc,), in_specs=[xs, ws, bs, xs], out_specs=[xs, ws, bs],
        out_shape=[jax.ShapeDtypeStruct((t, c), f32), jax.ShapeDtypeStruct((taps, c), f32),
                   jax.ShapeDtypeStruct((1, c), f32)],
        scratch_shapes=[pltpu.VMEM((t + CONV_PAD, tc), f32), pltpu.VMEM((t + CONV_PAD, tc), f32)],
        compiler_params=pltpu.CompilerParams(dimension_semantics=("parallel",)),
    )(x, w, b, ds)


def _ssd_chunk(x, bm, cm, dtr, st, dtb, alog, dsk, *, e_heads):
    ln, p = x.shape[0], SSD_HEADDIM
    dt = _softplus(dtr + dtb)
    da = dt * (-jnp.exp(alog))
    r = lax.broadcasted_iota(jnp.int32, (ln, ln), 0)
    c = lax.broadcasted_iota(jnp.int32, (ln, ln), 1)
    lower = r >= c
    acs = jnp.dot(lower.astype(f32), da, precision=HIGHEST, preferred_element_type=f32)
    acs_t = acs.T

    def per_head(v):
        return jnp.concatenate([jnp.broadcast_to(v[:, e:e + 1], (v.shape[0], p)) for e in range(e_heads)], axis=1)

    dt_x, acs_x = per_head(dt), per_head(acs)
    last_x = acs_x[ln - 1:ln, :]
    xdt = x * dt_x
    cb = bdot(cm, bm, 1, 1)
    y_off = bdot(cm, st, 1, 0) * jnp.exp(acs_x)
    st_new = st * jnp.exp(last_x) + bdot(bm, xdt * jnp.exp(last_x - acs_x), 0, 0)
    ys = []
    for e in range(e_heads):
        seg = acs[:, e:e + 1] - acs_t[e:e + 1, :]
        dec = jnp.exp(jnp.where(lower, seg, -jnp.inf))
        ys.append(bdot(cb * dec, xdt[:, e * p:(e + 1) * p], 1, 0))
    y = jnp.concatenate(ys, axis=1) + y_off + per_head(dsk) * x
    return y, st_new


def _ssd_specs(t, d_inner, groups):
    ln, n = SSD_CHUNK, SSD_STATE
    gw = d_inner // groups
    nc = t // ln
    return ln, n, gw, nc


def _ssd_fwd(xbc, dtp, dtb, alog, dsk, *, d_inner, groups, name, side=None):
    t = xbc.shape[0]
    ln, n, gw, nc = _ssd_specs(t, d_inner, groups)
    e_heads = gw // SSD_HEADDIM
    gp = SSD_GROUPS_PER_STEP
    assert groups % gp == 0 and (d_inner // n) % gp == 0
    boff = d_inner // n // gp
    ng = groups // gp

    def body(x_ref, b_ref, c_ref, dt_ref, dtb_ref, alog_ref, dsk_ref, y_ref, save_ref, st_scr):
        c, g = pl.program_id(0), pl.program_id(1)
        for e in range(gp):
            gi = g * gp + e

            @pl.when(c == 0)
            def _():
                st_scr[gi] = jnp.zeros((n, gw), f32)

            st = st_scr[gi]
            save_ref[e] = st
            y, st_new = _ssd_chunk(x_ref[:, e * gw:(e + 1) * gw], b_ref[:, e * n:(e + 1) * n],
                                   c_ref[:, e * n:(e + 1) * n], dt_ref[:, e * LANE:(e + 1) * LANE], st,
                                   dtb_ref[e], alog_ref[e], dsk_ref[e], e_heads=e_heads)
            y_ref[:, e * gw:(e + 1) * gw] = y
            st_scr[gi] = st_new

    par = pl.BlockSpec((gp, 1, LANE), lambda c, g: (g, 0, 0))
    return _hosted_call(
        body, side, name=name, grid=(nc, ng),
        in_specs=[pl.BlockSpec((ln, gp * gw), lambda c, g: (c, g)),
                  pl.BlockSpec((ln, gp * n), lambda c, g: (c, boff + g)),
                  pl.BlockSpec((ln, gp * n), lambda c, g: (c, boff + ng + g)),
                  pl.BlockSpec((ln, gp * LANE), lambda c, g: (c, g)), par, par, par],
        out_specs=[pl.BlockSpec((ln, gp * gw), lambda c, g: (c, g)),
                   pl.BlockSpec((gp, n, gw), lambda c, g: (c * ng + g, 0, 0))],
        out_shape=[jax.ShapeDtypeStruct((t, d_inner), f32),
                   jax.ShapeDtypeStruct((nc * groups, n, gw), f32)],
        scratch_shapes=[pltpu.VMEM((groups, n, gw), f32)],
        args=(xbc, xbc, xbc, dtp, dtb, alog, dsk))


def _ssd_bwd(xbc, dtp, dtb, alog, dsk, saved, dy, *, d_inner, groups, name, side=None):
    t = xbc.shape[0]
    ln, n, gw, nc = _ssd_specs(t, d_inner, groups)
    e_heads = gw // SSD_HEADDIM
    gp = SSD_GROUPS_PER_STEP
    assert groups % gp == 0 and (d_inner // n) % gp == 0
    boff = d_inner // n // gp
    ng = groups // gp

    def body(x_ref, b_ref, c_ref, dt_ref, dtb_ref, alog_ref, dsk_ref, save_ref, dy_ref,
             dx_ref, db_ref, dc_ref, ddt_ref, dpar_ref, dst_scr):
        c, g = pl.program_id(0), pl.program_id(1)

        @pl.when((c == 0) & (g == 0))
        def _():
            dpar_ref[...] = jnp.zeros_like(dpar_ref)

        fn = functools.partial(_ssd_chunk, e_heads=e_heads)
        for e in range(gp):
            gi = g * gp + e
            xl, nl, dl = (slice(e * w, (e + 1) * w) for w in (gw, n, LANE))

            @pl.when(c == 0)
            def _():
                dst_scr[gi] = jnp.zeros((n, gw), f32)

            _, vjp = jax.vjp(fn, x_ref[:, xl], b_ref[:, nl], c_ref[:, nl], dt_ref[:, dl], save_ref[e],
                             dtb_ref[e], alog_ref[e], dsk_ref[e])
            gx, gb, gc, gdt, gst, gdtb, galog, gdsk = vjp((dy_ref[:, xl], dst_scr[gi]))
            dx_ref[:, xl] = gx
            db_ref[:, nl] = gb
            dc_ref[:, nl] = gc
            ddt_ref[:, dl] = gdt
            dst_scr[gi] = gst
            dpar_ref[gi, 0:1, :] += gdtb
            dpar_ref[gi, 1:2, :] += galog
            dpar_ref[gi, 2:3, :] += gdsk

    def rc(c):
        return nc - 1 - c

    par = pl.BlockSpec((gp, 1, LANE), lambda c, g: (g, 0, 0))
    xs = pl.BlockSpec((ln, gp * gw), lambda c, g: (rc(c), g))
    ns = pl.BlockSpec((ln, gp * n), lambda c, g: (rc(c), g))
    return _hosted_call(
        body, side, name=name, grid=(nc, ng),
        in_specs=[xs,
                  pl.BlockSpec((ln, gp * n), lambda c, g: (rc(c), boff + g)),
                  pl.BlockSpec((ln, gp * n), lambda c, g: (rc(c), boff + ng + g)),
                  pl.BlockSpec((ln, gp * LANE), lambda c, g: (rc(c), g)), par, par, par,
                  pl.BlockSpec((gp, n, gw), lambda c, g: (rc(c) * ng + g, 0, 0)), xs],
        out_specs=[xs, ns, ns, pl.BlockSpec((ln, gp * LANE), lambda c, g: (rc(c), g)),
                   pl.BlockSpec((groups, 8, LANE), lambda c, g: (0, 0, 0))],
        out_shape=[jax.ShapeDtypeStruct((t, d_inner), f32),
                   jax.ShapeDtypeStruct((t, groups * n), f32),
                   jax.ShapeDtypeStruct((t, groups * n), f32),
                   jax.ShapeDtypeStruct((t, groups * LANE), f32),
                   jax.ShapeDtypeStruct((groups, 8, LANE), f32)],
        scratch_shapes=[pltpu.VMEM((groups, n, gw), f32)],
        args=(xbc, xbc, xbc, dtp, dtb, alog, dsk, saved, dy))


S5_TOP = 8


def _cmul(ar, ai, br, bi):
    return ar * br - ai * bi, ar * bi + ai * br


def _s5_scan(scr, base, ln, pw_ref, sp, *, reverse):
    s, k = 1, 0
    while s < ln:
        pr = pw_ref[0, k:k + 1, :sp]
        pi = pw_ref[0, k:k + 1, sp:]
        if reverse:
            pi = -pi
            src, dst = base + s, base
        else:
            src, dst = base, base + s
        ar, ai = scr[src:src + ln - s, :sp], scr[src:src + ln - s, sp:]
        mr, mi = _cmul(ar, ai, pr, pi)
        scr[dst:dst + ln - s, :sp] = scr[dst:dst + ln - s, :sp] + mr
        scr[dst:dst + ln - s, sp:] = scr[dst:dst + ln - s, sp:] + mi
        s *= 2
        k += 1


def _s5_states(u_ref, b_ref, lam_ref, pw_ref, scr, carry_r, carry_i, ln, sp):
    scr[S5_TOP:S5_TOP + ln, :] = _dg(u_ref[...], b_ref[0], 1, 0)
    mr, mi = _cmul(carry_r, carry_i, lam_ref[0][:, :sp], lam_ref[0][:, sp:])
    scr[S5_TOP:S5_TOP + 1, :sp] = scr[S5_TOP:S5_TOP + 1, :sp] + mr
    scr[S5_TOP:S5_TOP + 1, sp:] = scr[S5_TOP:S5_TOP + 1, sp:] + mi
    _s5_scan(scr, S5_TOP, ln, pw_ref, sp, reverse=False)


def _s5_dims(t, u_width):
    cbw = S5_BLOCK_GROUPS * S5_GROUP
    return S5_CHUNK, cbw, u_width // cbw, t // S5_CHUNK


def _s5_fwd(u, lamv, pw, bmat, cmat, *, name):
    t, w = u.shape
    ln, cbw, nb, nc = _s5_dims(t, w)
    sp2 = lamv.shape[2]
    sp = sp2 // 2
    bp = S5_BLOCKS_PER_STEP
    assert nb % bp == 0

    def one_block(u_ref, lam_ref, pw_ref, b_ref, c_ref, ys_ref, xp_ref, st_ref, scr, carry):
        @pl.when(pl.program_id(1) == 0)
        def _():
            carry[...] = jnp.zeros_like(carry)

        xp_ref[0] = carry[0:1, :]
        _s5_states(u_ref, b_ref, lam_ref, pw_ref, scr, carry[0:1, :sp], carry[0:1, sp:], ln, sp)
        carry[0:1, :] = scr[S5_TOP + ln - 1:S5_TOP + ln, :]
        states = scr[S5_TOP:S5_TOP + ln, :]
        st_ref[0] = states
        ys_ref[...] = _dg(states, c_ref[0], 1, 0)

    def body(u_ref, lam_ref, pw_ref, b_ref, c_ref, ys_ref, xp_ref, st_ref, scr, carry):
        for e in range(bp):
            one, cols = pl.ds(e, 1), pl.ds(e * cbw, cbw)
            one_block(u_ref.at[:, cols], lam_ref.at[one], pw_ref.at[one], b_ref.at[one], c_ref.at[one],
                      ys_ref.at[:, cols], xp_ref.at[0, one], st_ref.at[one], scr.at[e], carry.at[e])

    def blk(shape):
        return pl.BlockSpec((bp,) + shape, lambda j, c: (j, 0, 0))

    return pl.pallas_call(
        body, name=name, grid=(nb // bp, nc),
        in_specs=[pl.BlockSpec((ln, bp * cbw), lambda j, c: (c, j)), blk((1, sp2)), blk((8, sp2)),
                  blk((cbw, sp2)), blk((sp2, cbw))],
        out_specs=[pl.BlockSpec((ln, bp * cbw), lambda j, c: (c, j)),
                   pl.BlockSpec((1, bp, 1, sp2), lambda j, c: (c, j, 0, 0)),
                   pl.BlockSpec((bp, ln, sp2), lambda j, c: (j, c, 0))],
        out_shape=[jax.ShapeDtypeStruct((t, w), f32), jax.ShapeDtypeStruct((nc, nb, 1, sp2), f32),
                   jax.ShapeDtypeStruct((nb, t, sp2), f32)],
        scratch_shapes=[pltpu.VMEM((bp, S5_TOP + ln, sp2), f32), pltpu.VMEM((bp, 8, sp2), f32)],
        compiler_params=pltpu.CompilerParams(dimension_semantics=("parallel", "arbitrary")),
    )(u, lamv, pw, bmat, cmat)


def _s5_bwd(u, dy, du_skip, xp, states, lamv, pw, bmat, cmat, *, name):
    t, w = u.shape
    ln, cbw, nb, nc = _s5_dims(t, w)
    sp2 = lamv.shape[2]
    sp = sp2 // 2
    bp = S5_BLOCKS_PER_STEP
    assert nb % bp == 0

    def body(u_ref, dy_ref, dus_ref, xp_ref, st_ref, lam_ref, pw_ref, b_ref, c_ref,
             du_ref, db_ref, dc_ref, dl_ref, scr, gsc, gcarry):
        for e in range(bp):
            one, cols = pl.ds(e, 1), pl.ds(e * cbw, cbw)
            one_block(u_ref.at[:, cols], dy_ref.at[:, cols], dus_ref.at[:, cols], xp_ref.at[0, one],
                      st_ref.at[one], lam_ref.at[one], pw_ref.at[one], b_ref.at[one], c_ref.at[one],
                      du_ref.at[:, cols], db_ref.at[one], dc_ref.at[one], dl_ref.at[one],
                      scr.at[e], gsc.at[e], gcarry.at[e])

    def one_block(u_ref, dy_ref, dus_ref, xp_ref, st_ref, lam_ref, pw_ref, b_ref, c_ref,
                  du_ref, db_ref, dc_ref, dl_ref, scr, gsc, gcarry):
        @pl.when(pl.program_id(1) == 0)
        def _():
            gcarry[...] = jnp.zeros_like(gcarry)
            db_ref[...] = jnp.zeros_like(db_ref)
            dc_ref[...] = jnp.zeros_like(dc_ref)
            dl_ref[...] = jnp.zeros_like(dl_ref)

        lr, li = lam_ref[0][:, :sp], lam_ref[0][:, sp:]
        scr[S5_TOP - 1:S5_TOP, :] = xp_ref[0]
        scr[S5_TOP:S5_TOP + ln, :] = st_ref[0]
        dy = dy_ref[...]
        gsc[0:ln, :] = _dg(dy, c_ref[0], 1, 1)
        mr, mi = _cmul(gcarry[0:1, :sp], gcarry[0:1, sp:], lr, -li)
        gsc[ln - 1:ln, :sp] = gsc[ln - 1:ln, :sp] + mr
        gsc[ln - 1:ln, sp:] = gsc[ln - 1:ln, sp:] + mi
        _s5_scan(gsc, 0, ln, pw_ref, sp, reverse=True)
        gcarry[0:1, :] = gsc[0:1, :]
        g = gsc[0:ln, :]
        du_ref[...] = dus_ref[...] + _dg(g, b_ref[0], 1, 1)
        db_ref[0] += _dg(u_ref[...], g, 0, 0)
        dc_ref[0] += _dg(scr[S5_TOP:S5_TOP + ln, :], dy, 0, 0)
        pr, pi = scr[S5_TOP - 1:S5_TOP - 1 + ln, :sp], scr[S5_TOP - 1:S5_TOP - 1 + ln, sp:]
        gr, gi = g[:, :sp], g[:, sp:]
        dl_ref[0, 0:1, :sp] += jnp.sum(pr * gr + pi * gi, axis=0, keepdims=True)
        dl_ref[0, 0:1, sp:] += jnp.sum(pr * gi - pi * gr, axis=0, keepdims=True)

    def rc(c):
        return nc - 1 - c

    def blk(shape):
        return pl.BlockSpec((bp,) + shape, lambda j, c: (j, 0, 0))

    row = pl.BlockSpec((ln, bp * cbw), lambda j, c: (rc(c), j))
    return pl.pallas_call(
        body, name=name, grid=(nb // bp, nc),
        in_specs=[row, row, row, pl.BlockSpec((1, bp, 1, sp2), lambda j, c: (rc(c), j, 0, 0)),
                  pl.BlockSpec((bp, ln, sp2), lambda j, c: (j, rc(c), 0)),
                  blk((1, sp2)), blk((8, sp2)), blk((cbw, sp2)), blk((sp2, cbw))],
        out_specs=[row, blk((cbw, sp2)), blk((sp2, cbw)), blk((8, sp2))],
        out_shape=[jax.ShapeDtypeStruct((t, w), f32), jax.ShapeDtypeStruct((nb, cbw, sp2), f32),
                   jax.ShapeDtypeStruct((nb, sp2, cbw), f32), jax.ShapeDtypeStruct((nb, 8, sp2), f32)],
        scratch_shapes=[pltpu.VMEM((bp, S5_TOP + ln, sp2), f32), pltpu.VMEM((bp, ln, sp2), f32),
                        pltpu.VMEM((bp, 8, sp2), f32)],
        compiler_params=pltpu.CompilerParams(dimension_semantics=("parallel", "arbitrary")),
    )(u, dy, du_skip, xp, states, lamv, pw, bmat, cmat)


def _s5_discretize(lre, lim, log_step, bre, bim, cre, cim):
    g, p = lre.shape
    gb = S5_BLOCK_GROUPS
    nb = g // gb
    lam = lax.complex(lre, lim)
    lam_bar = jnp.exp(lam * jnp.exp(log_step)[:, None])
    b_bar = ((lam_bar - 1.0) / lam)[..., None] * lax.complex(bre, bim)
    lb = lam_bar.reshape(nb, 1, gb * p)
    lamv = jnp.concatenate([jnp.real(lb), jnp.imag(lb)], axis=-1)
    eye = jnp.eye(gb, dtype=f32)
    bb = b_bar.reshape(nb, gb, p, S5_GROUP)

    def bdiag(v):
        return jnp.einsum('jgpi,gh->jgihp', v, eye).reshape(nb, gb * S5_GROUP, gb * p)

    bmat = jnp.concatenate([bdiag(jnp.real(bb)), bdiag(jnp.imag(bb))], axis=-1)

    def cdiag(v):
        return jnp.einsum('jgip,gh->jhpgi', v, eye).reshape(nb, gb * p, gb * S5_GROUP)

    cmat = jnp.concatenate([cdiag(cre.reshape(nb, gb, S5_GROUP, p)),
                            -cdiag(cim.reshape(nb, gb, S5_GROUP, p))], axis=1)
    return lamv, bmat, cmat


def _s5_powers(lamv):
    sp = lamv.shape[2] // 2
    pr, pi = lamv[:, :, :sp], lamv[:, :, sp:]
    rows = []
    for _ in range(8):
        rows.append(jnp.concatenate([pr, pi], axis=-1))
        pr, pi = pr * pr - pi * pi, 2.0 * pr * pi
    return lax.stop_gradient(jnp.concatenate(rows, axis=1))


def _ln_fwd(h, out, p, tag):
    return _rw_fwd(_ln_fn, [h, out], [p["ln_g"], p["ln_b"]], [h.shape[1]], name=tag + "_ln")[0]


def _ln_bwd(h, out, p, dh2, tag):
    return _rw_bwd(_ln_fn, [h, out], [p["ln_g"], p["ln_b"]], [dh2], name=tag + "_ln_bwd")


def _ssd_layer_fwd(h, p, wout_of, side=None):
    di, groups = p["wz"].shape[1], p["dtb"].shape[0]
    z = _mm(h, p["wz"], name="ssd_z")
    xr = _mm(h, p["wxbc"], name="ssd_xbc")
    dtp = _mm(h, p["wdt"], name="ssd_dt")
    xbc = _conv_fwd(xr, p["conv_w"], p["conv_b"], name="ssd_conv")
    (y, states), got = _ssd_fwd(xbc, dtp, p["dtb"], p["alog"], p["dsk"], d_inner=di, groups=groups,
                                name="ssd_scan", side=side)
    p["wout"] = wout_of(got)
    yn = _rw_fwd(_make_ssd_post_fn(groups), [y, z], [p["norm_w"]], [di], name="ssd_post")[0]
    out = _mm(yn, p["wout"], name="ssd_out")
    return _ln_fwd(h, out, p, "ssd"), (h, z, xr, dtp, xbc, states, y, yn, out), got


def _ssd_layer_bwd(saved, p, dh2, side=None, in_sides=None):
    h, z, xr, dtp, xbc, states, y, yn, out = saved
    di, groups = p["wz"].shape[1], p["dtb"].shape[0]
    dh, dout, dg, db = _ln_bwd(h, out, p, dh2, "ssd")
    dyn = _mm(dout, p["wout"], tb=True, name="ssd_out_dx")
    g = {"ln_g": dg, "ln_b": db, "wout": _mm(yn, dout, ta=True, out_dtype=bf16, name="ssd_out_dw")}
    if callable(side):
        side = side(g["wout"])
    dy, dz, g["norm_w"] = _rw_bwd(_make_ssd_post_fn(groups), [y, z], [p["norm_w"]], [dyn], name="ssd_post_bwd")
    (dx, dbm, dcm, ddt, dpar), got = _ssd_bwd(xbc, dtp, p["dtb"], p["alog"], p["dsk"], states, dy,
                                              d_inner=di, groups=groups, name="ssd_scan_bwd", side=side)
    g["dtb"], g["alog"], g["dsk"] = dpar[:, 0:1, :], dpar[:, 1:2, :], dpar[:, 2:3, :]
    dxr, g["conv_w"], g["conv_b"] = _conv_bwd(xr, p["conv_w"], p["conv_b"],
                                               jnp.concatenate([dx, dbm, dcm], axis=1), name="ssd_conv_bwd")
    g["wz"] = _mm(h, dz, ta=True, name="ssd_z_dw")
    g["wxbc"] = _mm(h, dxr, ta=True, name="ssd_xbc_dw")
    g["wdt"] = _mm(h, ddt, ta=True, name="ssd_dt_dw")
    if in_sides is None:
        dh = _mm(dxr, p["wxbc"], tb=True, add=dh, name="ssd_xbc_dx")
        dh = _mm(dz, p["wz"], tb=True, add=dh, name="ssd_z_dx")
        got_in = (None, None)
    else:
        side_a, side_b = in_sides(g)
        dh, got_a = _mm(dxr, p["wxbc"], tb=True, add=dh, name="ssd_xbc_dx", side=side_a)
        dh, got_b = _mm(dz, p["wz"], tb=True, add=dh, name="ssd_z_dx", side=side_b)
        got_in = (got_a, got_b)
    dh = _mm(ddt, p["wdt"], tb=True, add=dh, name="ssd_dt_dx")
    return dh, g, got, got_in


def _fox_layer_fwd(h, p, side=None):
    w = p["wout"].shape[0]
    heads = w // FOX_HEAD_DIM
    t = h.shape[0]
    qkvz = _mm(h, p["wqkvz"], name="fox_qkvz")
    fr = _mm(h, p["wf"], name="fox_f")
    logf = _rw_fwd(_logf_fn, [fr], [p["fb"]], [LANE], name="fox_logf")[0]
    cum = _cumsum_rows(logf, reverse=False, name="fox_cumsum")
    ck = _lane_bcast(cum[:, :heads].T)
    (o, lse), got = _attn_fwd(qkvz, 0, qkvz, heads, qkvz, 2 * heads, None, None, ck,
                              heads=heads, scale=FOX_HEAD_DIM ** -0.5, name="fox_attn", side=side)
    yg = _rw_fwd(_gate_fn, [o, (qkvz, w, 3)], [], [w], name="fox_gate")[0]
    out = _mm(yg, p["wout"], name="fox_out")
    return _ln_fwd(h, out, p, "fox"), (h, qkvz, fr, ck, o, lse, yg, out), got


def _fox_layer_bwd(saved, p, dh2, side=None):
    h, qkvz, fr, ck, o, lse, yg, out = saved
    w = p["wout"].shape[0]
    heads = w // FOX_HEAD_DIM
    t = h.shape[0]
    dh, dout, dg, db = _ln_bwd(h, out, p, dh2, "fox")
    dyg = _mm(dout, p["wout"], tb=True, name="fox_out_dx")
    g = {"ln_g": dg, "ln_b": db, "wout": _mm(yg, dout, ta=True, out_dtype=bf16, name="fox_out_dw")}
    do, dz = _rw_bwd(_gate_fn, [o, (qkvz, w, 3)], [], [dyg], name="fox_gate_bwd")
    delta = _attn_delta(do, o, heads=heads, name="fox_delta")
    (dq, dk, dv, drow, dck), got = _attn_bwd(qkvz, 0, qkvz, heads, qkvz, 2 * heads, None, None, ck, do, lse, delta,
                                             heads=heads, scale=FOX_HEAD_DIM ** -0.5, name="fox_attn_bwd", side=side)
    dqkvz = jnp.concatenate([dq, dk, dv, dz], axis=1)
    dcum = jnp.pad((drow.reshape(heads, t) + dck.reshape(heads, t)).T, ((0, 0), (0, LANE - heads)))
    dlogf = _cumsum_rows(dcum, reverse=True, name="fox_cumsum_bwd")
    dfr, g["fb"] = _rw_bwd(_logf_fn, [fr], [p["fb"]], [dlogf], name="fox_logf_bwd")
    dh = _mm(dqkvz, p["wqkvz"], tb=True, add=dh, name="fox_qkvz_dx")
    dh = _mm(dfr, p["wf"], tb=True, add=dh, name="fox_f_dx")
    g["wqkvz"] = _mm(h, dqkvz, ta=True, name="fox_qkvz_dw")
    g["wf"] = _mm(h, dfr, ta=True, name="fox_f_dw")
    return dh, g, got


def _mla_layer_fwd(h, p, cos, sin):
    rank = p["qn"].shape[1]
    w = p["wout"].shape[0]
    heads = w // MLA_V
    lat = _mm(h, p["wlat"], name="mla_lat")
    kpr = _mm(h, p["wkpe"], name="mla_kpe")
    z = _mm(h, p["wz"], name="mla_z")
    qnl, kvnl = _rw_fwd(_make_mla_norm_fn(rank), [lat], [p["qn"], p["kvn"]], [rank, rank], name="mla_norm")
    qno = _mm(qnl, p["wqn"], name="mla_qn")
    qpr = _mm(qnl, p["wqp"], name="mla_qp")
    kno = _mm(kvnl, p["wkn"], name="mla_kn")
    v = _mm(kvnl, p["wv"], name="mla_v")
    qp = _rw_fwd(_make_rope_fn(heads), [qpr, cos, sin], [], [heads * LANE], name="mla_rope_q")[0]
    kp = _rw_fwd(_make_rope_fn(1), [kpr, cos, sin], [], [LANE], name="mla_rope_k")[0]
    scale = (MLA_NOPE + MLA_ROPE) ** -0.5
    (o, lse), _ = _attn_fwd(qno, 0, kno, 0, v, 0, qp, kp, None, heads=heads, scale=scale, name="mla_attn")
    yg = _rw_fwd(_gate_fn, [o, z], [], [w], name="mla_gate")[0]
    out = _mm(yg, p["wout"], name="mla_out")
    return _ln_fwd(h, out, p, "mla"), (h, lat, kpr, z, qnl, kvnl, qno, qpr, kno, v, qp, kp, o, lse, yg, out)


def _mla_layer_bwd(saved, p, dh2, cos, sin, side=None):
    h, lat, kpr, z, qnl, kvnl, qno, qpr, kno, v, qp, kp, o, lse, yg, out = saved
    rank = p["qn"].shape[1]
    w = p["wout"].shape[0]
    heads = w // MLA_V
    dh, dout, dg, db = _ln_bwd(h, out, p, dh2, "mla")
    dyg = _mm(dout, p["wout"], tb=True, name="mla_out_dx")
    g = {"ln_g": dg, "ln_b": db, "wout": _mm(yg, dout, ta=True, out_dtype=bf16, name="mla_out_dw")}
    do, dz = _rw_bwd(_gate_fn, [o, z], [], [dyg], name="mla_gate_bwd")
    delta = _attn_delta(do, o, heads=heads, name="mla_delta")
    (dqno, dkno, dv, dqp, dkp), got = _attn_bwd(qno, 0, kno, 0, v, 0, qp, kp, None, do, lse, delta, heads=heads,
                                                scale=(MLA_NOPE + MLA_ROPE) ** -0.5, name="mla_attn_bwd", side=side)
    (dqpr,) = _rw_bwd(_make_rope_fn(heads), [qpr, cos, sin], [], [dqp], n_const=2, name="mla_rope_q_bwd")
    (dkpr,) = _rw_bwd(_make_rope_fn(1), [kpr, cos, sin], [], [dkp], n_const=2, name="mla_rope_k_bwd")
    dqnl = _mm(dqno, p["wqn"], tb=True, name="mla_qn_dx")
    dqnl = _mm(dqpr, p["wqp"], tb=True, add=dqnl, name="mla_qp_dx")
    dkvnl = _mm(dkno, p["wkn"], tb=True, name="mla_kn_dx")
    dkvnl = _mm(dv, p["wv"], tb=True, add=dkvnl, name="mla_v_dx")
    g["wqn"] = _mm(qnl, dqno, ta=True, name="mla_qn_dw")
    g["wqp"] = _mm(qnl, dqpr, ta=True, name="mla_qp_dw")
    g["wkn"] = _mm(kvnl, dkno, ta=True, name="mla_kn_dw")
    g["wv"] = _mm(kvnl, dv, ta=True, name="mla_v_dw")
    dlat, g["qn"], g["kvn"] = _rw_bwd(_make_mla_norm_fn(rank), [lat], [p["qn"], p["kvn"]], [dqnl, dkvnl],
                                     name="mla_norm_bwd")
    dh = _mm(dlat, p["wlat"], tb=True, add=dh, name="mla_lat_dx")
    dh = _mm(dkpr, p["wkpe"], tb=True, add=dh, name="mla_kpe_dx")
    dh = _mm(dz, p["wz"], tb=True, add=dh, name="mla_z_dx")
    g["wlat"] = _mm(h, dlat, ta=True, name="mla_lat_dw")
    g["wkpe"] = _mm(h, dkpr, ta=True, name="mla_kpe_dw")
    g["wz"] = _mm(h, dz, ta=True, name="mla_z_dw")
    return dh, g, got


def _s5_layer_fwd(h, p):
    w = p["wout"].shape[0]
    u = _mm(h, p["wu"], name="s5_u")
    z = _mm(h, p["wz"], name="s5_z")
    ys, xp, states = _s5_fwd(u, p["lamv"], p["pw"], p["bmat"], p["cmat"], name="s5_scan")
    y1 = _rw_fwd(_s5_act_fn, [ys, u], [p["d"]], [w], name="s5_act")[0]
    gp = _mm(y1, p["wglu"], name="s5_glu")
    y2 = _rw_fwd(_s5_glu_fn, [y1, gp, z], [p["bglu"]], [w], name="s5_gate")[0]
    out = _mm(y2, p["wout"], name="s5_out")
    return _ln_fwd(h, out, p, "s5"), (h, u, z, ys, xp, states, y1, gp, y2, out)


def _s5_layer_bwd(saved, p, dh2):
    h, u, z, ys, xp, states, y1, gp, y2, out = saved
    dh, dout, dg, db = _ln_bwd(h, out, p, dh2, "s5")
    dy2 = _mm(dout, p["wout"], tb=True, name="s5_out_dx")
    g = {"ln_g": dg, "ln_b": db, "wout": _mm(y2, dout, ta=True, out_dtype=bf16, name="s5_out_dw")}
    dy1, dgp, dz, g["bglu"] = _rw_bwd(_s5_glu_fn, [y1, gp, z], [p["bglu"]], [dy2], name="s5_gate_bwd")
    dy1 = _mm(dgp, p["wglu"], tb=True, add=dy1, name="s5_glu_dx")
    g["wglu"] = _mm(y1, dgp, ta=True, out_dtype=bf16, name="s5_glu_dw")
    dys, du, g["d"] = _rw_bwd(_s5_act_fn, [ys, u], [p["d"]], [dy1], name="s5_act_bwd")
    du, g["bmat"], g["cmat"], dlam = _s5_bwd(u, dys, du, xp, states, p["lamv"], p["pw"], p["bmat"], p["cmat"],
                                             name="s5_scan_bwd")
    g["lamv"] = dlam[:, 0:1, :]
    dh = _mm(du, p["wu"], tb=True, add=dh, name="s5_u_dx")
    dh = _mm(dz, p["wz"], tb=True, add=dh, name="s5_z_dx")
    g["wu"] = _mm(h, du, ta=True, name="s5_u_dw")
    g["wz"] = _mm(h, dz, ta=True, name="s5_z_dw")
    return dh, g


def _pad_last(a, to):
    return jnp.pad(a, [(0, 0)] * (a.ndim - 1) + [(0, to - a.shape[-1])])


def _row(v):
    return v.reshape(1, -1)


def _prep_ssd(w):
    di, cd, hh = w["ssd_norm_w"].shape[0], w["ssd_conv_b"].shape[0], w["ssd_dt_bias"].shape[0]
    groups = (cd - di) // (2 * SSD_STATE)
    e = hh // groups

    def perm(v):
        lead = v.shape[:-1]
        return _pad_last(v.reshape(lead + (groups, e)), LANE).reshape(lead + (groups * LANE,))

    w_in = w["ssd_w_in"]
    return dict(
        wz=w_in[:, :di], wxbc=w_in[:, di:di + cd], wdt=perm(w_in[:, di + cd:]),
        conv_w=w["ssd_conv_w"], conv_b=_row(w["ssd_conv_b"]),
        dtb=perm(w["ssd_dt_bias"]).reshape(groups, 1, LANE), alog=perm(w["ssd_a_log"]).reshape(groups, 1, LANE),
        dsk=perm(w["ssd_d"]).reshape(groups, 1, LANE), norm_w=_row(w["ssd_norm_w"]),
        ln_g=_row(w["ln_g"][0]), ln_b=_row(w["ln_b"][0]))


def _prep_fox(w):
    wd = w["fox_w_out"].shape[0]
    w_in = w["fox_w_in"]
    return dict(wqkvz=w_in[:, :4 * wd], wf=_pad_last(w_in[:, 4 * wd:], LANE), wout=w["fox_w_out"],
                fb=_pad_last(_row(w["fox_f_bias"]), LANE), ln_g=_row(w["ln_g"][1]), ln_b=_row(w["ln_b"][1]))


def _prep_mla(w):
    rank = w["mla_q_norm"].shape[0]
    wd = w["mla_w_out"].shape[0]
    heads = wd // MLA_V
    w_in = w["mla_w_in"]
    qu = w["mla_w_q_up"].reshape(rank, heads, MLA_NOPE + MLA_ROPE)
    kvu = w["mla_w_kv_up"].reshape(w["mla_w_kv_up"].shape[0], heads, MLA_NOPE + MLA_V)
    return dict(
        wlat=w_in[:, :2 * rank], wkpe=_pad_last(w_in[:, 2 * rank:2 * rank + MLA_ROPE], LANE),
        wz=w_in[:, 2 * rank + MLA_ROPE:],
        wqn=qu[:, :, :MLA_NOPE].reshape(rank, heads * LANE),
        wqp=_pad_last(qu[:, :, MLA_NOPE:], LANE).reshape(rank, heads * LANE),
        wkn=kvu[:, :, :MLA_NOPE].reshape(-1, heads * LANE), wv=kvu[:, :, MLA_NOPE:].reshape(-1, heads * LANE),
        wout=w["mla_w_out"], qn=_row(w["mla_q_norm"]), kvn=_row(w["mla_kv_norm"]),
        ln_g=_row(w["ln_g"][2]), ln_b=_row(w["ln_b"][2]))


def _prep_s5(w):
    wd = w["s5_w_out"].shape[0]
    w_in = w["s5_w_in"]
    lamv, bmat, cmat = _s5_discretize(w["s5_lambda_re"], w["s5_lambda_im"], w["s5_log_step"],
                                      w["s5_b_re"], w["s5_b_im"], w["s5_c_re"], w["s5_c_im"])
    return dict(wu=w_in[:, :wd], wz=w_in[:, wd:], wglu=w["s5_w_glu"], wout=w["s5_w_out"],
                d=_row(w["s5_d"]), bglu=_row(w["s5_b_glu"]), lamv=lamv, bmat=bmat, cmat=cmat,
                ln_g=_row(w["ln_g"][3]), ln_b=_row(w["ln_b"][3]))


def _rope_tables(positions):
    inv_freq = ROPE_BASE ** (-jnp.arange(0, MLA_ROPE, 2, dtype=f32) / MLA_ROPE)
    ang = positions.astype(f32).reshape(-1, 1) * inv_freq
    return jnp.cos(ang), jnp.sin(ang)


def _heads3(v):
    return v.reshape(v.shape[0], -1, LANE)


def _flat2(v):
    return v.reshape(v.shape[0], -1)


def _natural_grads_ssd(g, e_heads):
    return {"ssd_w_in": jnp.concatenate([g["wz"], g["wxbc"], _flat2(_heads3(g["wdt"])[:, :, :e_heads])], axis=1),
            "ssd_w_out": g["wout"]}


def _natural_grads_fox(g, heads):
    return {"fox_w_in": jnp.concatenate([g["wqkvz"], g["wf"][:, :heads]], axis=1), "fox_w_out": g["wout"]}


def _natural_grads_mla(g):
    return {"mla_w_in": jnp.concatenate([g["wlat"], g["wkpe"][:, :MLA_ROPE], g["wz"]], axis=1),
            "mla_w_q_up": _flat2(jnp.concatenate([_heads3(g["wqn"]), _heads3(g["wqp"])[:, :, :MLA_ROPE]], axis=2)),
            "mla_w_kv_up": _flat2(jnp.concatenate([_heads3(g["wkn"]), _heads3(g["wv"])], axis=2)),
            "mla_w_out": g["wout"]}


def _natural_grads_s5(g):
    return {"s5_w_in": jnp.concatenate([g["wu"], g["wz"]], axis=1), "s5_w_glu": g["wglu"], "s5_w_out": g["wout"]}


def _sequence_step(x, positions, tgt, small, big_of_layer, gather_sides=(None, None), exchange_side=None,
                   ssd_in_sides=None, ssd_wout_of=None):
    cos, sin = _rope_tables(positions)
    exchange_side = exchange_side or (lambda nat: None)
    vjps = []

    def prepared(prep, big):
        p, vjp = jax.vjp(lambda s: prep({**big, **s}), small)
        vjps.append((vjp, tuple(p)))
        return p

    big0 = big_of_layer[0](None)
    p0 = prepared(_prep_ssd, big0)
    wout_of = (lambda got: big0["ssd_w_out"]) if "ssd_w_out" in big0 else ssd_wout_of
    h1, s0, got1 = _ssd_layer_fwd(x, p0, wout_of, side=gather_sides[0])
    p1 = prepared(_prep_fox, big_of_layer[1](got1))
    h2, s1, got23 = _fox_layer_fwd(h1, p1, side=gather_sides[1])
    p2 = prepared(_prep_mla, big_of_layer[2](got23))
    p3 = prepared(_prep_s5, big_of_layer[3](got23))
    p3["pw"] = _s5_powers(p3["lamv"])
    h3, s2 = _mla_layer_fwd(h2, p2, cos, sin)
    h4, s3 = _s5_layer_fwd(h3, p3)
    dh, loss = _loss_head(h4, tgt)
    dh, g3 = _s5_layer_bwd(s3, p3, dh)
    nat3 = _natural_grads_s5(g3)
    dh, g2, parts3 = _mla_layer_bwd(s2, p2, dh, cos, sin, side=exchange_side(nat3))
    nat2 = _natural_grads_mla(g2)
    dh, g1, parts2 = _fox_layer_bwd(s1, p1, dh, side=exchange_side(nat2))
    nat1 = _natural_grads_fox(g1, small["fox_f_bias"].shape[0])
    e_heads = small["ssd_dt_bias"].shape[0] // p0["dtb"].shape[0]
    dh, g0, parts1, parts_in = _ssd_layer_bwd(
        s0, p0, dh, side=lambda gw: exchange_side({**nat1, "ssd_w_out": gw}),
        in_sides=None if ssd_in_sides is None else
        lambda g: ssd_in_sides(_natural_grads_ssd(g, e_heads)["ssd_w_in"]))
    nat0 = _natural_grads_ssd(g0, e_heads)
    g_small = None
    for p, (vjp, keys), g in zip((p0, p1, p2, p3), vjps, (g0, g1, g2, g3)):
        (gi,) = vjp({k: g[k].astype(p[k].dtype) for k in keys})
        g_small = gi if g_small is None else jax.tree.map(jnp.add, g_small, gi)
    return loss, dh, g_small, (nat0, nat1, nat2, nat3), (parts1, parts2, parts3, parts_in)


FLAT_COLS = 1024
FLAT_ROW_ALIGN = 128


def _pack(arrs):
    flat = jnp.concatenate([a.reshape(-1) for a in arrs])
    unit = FLAT_COLS * FLAT_ROW_ALIGN
    return jnp.pad(flat, (0, -flat.shape[0] % unit)).reshape(-1, FLAT_COLS)


def _unpack(flat2d, shapes):
    flat = flat2d.reshape(-1)
    out, off = [], 0
    for s in shapes:
        sz = math.prod(s)
        out.append(flat[off:off + sz].reshape(s))
        off += sz
    return out


def _unpack_gathered(gathered, shapes, axes):
    flat = gathered.reshape(N_DEV, -1)
    out, off = [], 0
    for s, ax in zip(shapes, axes):
        sz = math.prod(s)
        seg = flat[:, off:off + sz].reshape((N_DEV,) + tuple(s))
        out.append(jnp.concatenate([seg[d] for d in range(N_DEV)], axis=ax)[0])
        off += sz
    return out


MESH = pl.DeviceIdType.MESH
HBM_SPEC = pl.BlockSpec(memory_space=pl.ANY)


def _comm_scratch(n):
    return [pltpu.SemaphoreType.DMA((n, 7)), pltpu.SemaphoreType.DMA((n, 7)), pltpu.SemaphoreType.DMA((n,))]


class _Gather:
    def __init__(self, xs):
        self.ins = list(xs)
        self.out_shape = [jax.ShapeDtypeStruct((N_DEV,) + v.shape, v.dtype) for v in xs]

    def _plan(self, x_refs, out_refs, sems):
        send_sems, recv_sems, local_sems = sems
        x, y, cc = lax.axis_index("x"), lax.axis_index("y"), lax.axis_index("c")
        me, sibling = (x, y, cc), (x, y, 1 - cc)
        chips = [(1 - x, y), (x, 1 - y), (1 - x, 1 - y)]

        def rows(w, px, py, pc):
            return out_refs[w].at[4 * px + 2 * py + pc]

        def copy(w, k, block, to, src=None):
            return pltpu.make_async_remote_copy(
                src_ref=rows(w, *block) if src is None else src, dst_ref=rows(w, *block),
                send_sem=send_sems.at[w, k], recv_sem=recv_sems.at[w, k], device_id=to, device_id_type=MESH)

        n = len(x_refs)
        mine = [pltpu.make_async_copy(x_refs[w], rows(w, *me), local_sems.at[w]) for w in range(n)]
        first = []
        for w in range(n):
            first.append(copy(w, 0, me, sibling, src=x_refs[w]))
            first += [copy(w, 1 + j, me, (*chip, cc), src=x_refs[w]) for j, chip in enumerate(chips)]
        return n, me, sibling, chips, cc, copy, mine, first

    def start(self, x_refs, out_refs, sems):
        *_, mine, first = self._plan(x_refs, out_refs, sems)
        for cp in mine + first:
            cp.start()

    def finish(self, x_refs, out_refs, sems):
        n, me, sibling, chips, cc, copy, mine, first = self._plan(x_refs, out_refs, sems)
        passed = []
        for w in range(n):
            for j, chip in enumerate(chips):
                copy(w, 1 + j, (*chip, cc), me).wait_recv()
                cp = copy(w, 4 + j, (*chip, cc), sibling)
                cp.start()
                passed.append(cp)
        for w in range(n):
            copy(w, 0, sibling, me).wait_recv()
            for j, chip in enumerate(chips):
                copy(w, 4 + j, (*chip, 1 - cc), me).wait_recv()
        for cp in first + passed:
            cp.wait_send()
        for cp in mine:
            cp.wait()


class _Exchange:
    def __init__(self, gs):
        self.ins = list(gs)
        self.out_shape = [jax.ShapeDtypeStruct(v.shape, v.dtype) for v in gs]

    def _plan(self, g_refs, out_refs, sems):
        send_sems, recv_sems, local_sems = sems
        x, y, cc = lax.axis_index("x"), lax.axis_index("y"), lax.axis_index("c")
        me = 4 * x + 2 * y + cc
        peers = []
        for k in range(1, N_DEV):
            px = 1 - x if (k >> 2) & 1 else x
            py = 1 - y if (k >> 1) & 1 else y
            pc = 1 - cc if k & 1 else cc
            peers.append((k, (px, py, pc), 4 * px + 2 * py + pc))

        def copy(w, k, peer, slot_src, slot_dst):
            return pltpu.make_async_remote_copy(
                src_ref=g_refs[w].at[slot_src], dst_ref=out_refs[w].at[slot_dst],
                send_sem=send_sems.at[w, k - 1], recv_sem=recv_sems.at[w, k - 1], device_id=peer, device_id_type=MESH)

        n = len(g_refs)
        mine = [pltpu.make_async_copy(g_refs[w].at[me], out_refs[w].at[me], local_sems.at[w]) for w in range(n)]
        sends = [copy(w, k, peer, idx, me) for w in range(n) for k, peer, idx in peers]
        return n, peers, copy, mine, sends

    def start(self, g_refs, out_refs, sems):
        *_, mine, sends = self._plan(g_refs, out_refs, sems)
        for cp in mine + sends:
            cp.start()

    def finish(self, g_refs, out_refs, sems):
        n, peers, copy, mine, sends = self._plan(g_refs, out_refs, sems)
        for w in range(n):
            for k, peer, idx in peers:
                copy(w, k, peer, idx, idx).wait_recv()
        for cp in sends:
            cp.wait_send()
        for cp in mine:
            cp.wait()


class _Both:
    def __init__(self, first, second):
        self.parts = (first, second)
        self.ins = first.ins + second.ins
        self.out_shape = first.out_shape + second.out_shape

    def _split(self, ins, outs, sems):
        n0 = len(self.parts[0].ins)
        n1 = len(self.parts[1].ins)
        for side, lo, n in ((self.parts[0], 0, n0), (self.parts[1], n0, n1)):
            yield side, ins[lo:lo + n], outs[lo:lo + n], tuple(s.at[pl.ds(lo, n)] for s in sems)

    def start(self, ins, outs, sems):
        for side, i, o, s in self._split(ins, outs, sems):
            side.start(i, o, s)

    def finish(self, ins, outs, sems):
        for side, i, o, s in self._split(ins, outs, sems):
            side.finish(i, o, s)


def _run_comm(side, *, name):
    n = len(side.ins)

    def body(*refs):
        ins, outs, sems = refs[:n], refs[n:2 * n], refs[2 * n:]
        side.start(ins, outs, sems)
        side.finish(ins, outs, sems)

    return pl.pallas_call(
        body, name=name, out_shape=side.out_shape,
        in_specs=[HBM_SPEC] * n, out_specs=[HBM_SPEC] * n, scratch_shapes=_comm_scratch(n),
    )(*side.ins)


def _hosted_call(body, side, *, name, grid, in_specs, out_specs, out_shape, scratch_shapes, args):
    out_specs, out_shape = list(out_specs), list(out_shape)
    if side is None:
        res = pl.pallas_call(
            body, name=name, grid=grid, in_specs=in_specs, out_specs=out_specs, out_shape=out_shape,
            scratch_shapes=scratch_shapes,
            compiler_params=pltpu.CompilerParams(dimension_semantics=("arbitrary",) * len(grid)))(*args)
        return res, None
    n_in, n_out, n_scr, ns = len(in_specs), len(out_specs), len(scratch_shapes), len(side.ins)

    def wrapped(*refs):
        ins, s_ins = refs[:n_in], refs[n_in:n_in + ns]
        outs = refs[n_in + ns:n_in + ns + n_out]
        s_outs = refs[n_in + ns + n_out:n_in + 2 * ns + n_out]
        scr = refs[n_in + 2 * ns + n_out:n_in + 2 * ns + n_out + n_scr]
        sems = refs[n_in + 2 * ns + n_out + n_scr:]
        ids = [pl.program_id(ax) for ax in range(len(grid))]
        first = functools.reduce(lambda p, q: p & q, [i == 0 for i in ids])
        last = functools.reduce(lambda p, q: p & q, [i == g - 1 for i, g in zip(ids, grid)])

        @pl.when(first)
        def _():
            side.start(s_ins, s_outs, sems)

        body(*ins, *outs, *scr)

        @pl.when(last)
        def _():
            side.finish(s_ins, s_outs, sems)

    res = pl.pallas_call(
        wrapped, name=name, grid=grid, in_specs=list(in_specs) + [HBM_SPEC] * ns,
        out_specs=out_specs + [HBM_SPEC] * ns, out_shape=out_shape + side.out_shape,
        scratch_shapes=list(scratch_shapes) + _comm_scratch(ns),
        compiler_params=pltpu.CompilerParams(dimension_semantics=("arbitrary",) * len(grid)))(*args, *side.ins)
    return res[:n_out], res[n_out:]


BIG = (("ssd_w_in", 2), ("ssd_w_out", 1), ("fox_w_in", 2), ("fox_w_out", 1), ("mla_w_in", 2),
       ("mla_w_q_up", 2), ("mla_w_kv_up", 2), ("mla_w_out", 1), ("s5_w_in", 2), ("s5_w_glu", 1), ("s5_w_out", 1))
LAYER_BIG = (("ssd_w_in", "ssd_w_out"), ("fox_w_in", "fox_w_out"),
             ("mla_w_in", "mla_w_q_up", "mla_w_kv_up", "mla_w_out"), ("s5_w_in", "s5_w_glu", "s5_w_out"))
SMALL = (("ssd_conv_w", 2), ("mla_q_norm", 1), ("mla_kv_norm", 1), ("s5_d", 1), ("s5_b_glu", 1))
REPLICATED = ("ln_g", "ln_b", "ssd_conv_b", "ssd_dt_bias", "ssd_a_log", "ssd_d", "ssd_norm_w", "fox_f_bias",
              "s5_lambda_re", "s5_lambda_im", "s5_log_step", "s5_b_re", "s5_b_im", "s5_c_re", "s5_c_im")
WEIGHT_ORDER = ("ln_g", "ln_b", "ssd_w_in", "ssd_conv_w", "ssd_conv_b", "ssd_dt_bias", "ssd_a_log", "ssd_d",
                "ssd_norm_w", "ssd_w_out", "fox_w_in", "fox_f_bias", "fox_w_out", "mla_w_in", "mla_q_norm",
                "mla_kv_norm", "mla_w_q_up", "mla_w_kv_up", "mla_w_out", "s5_w_in", "s5_lambda_re", "s5_lambda_im",
                "s5_log_step", "s5_b_re", "s5_b_im", "s5_c_re", "s5_c_im", "s5_d", "s5_w_glu", "s5_b_glu", "s5_w_out")


def _sum_adamw(parts, w, m, v, *, name):
    r, c = w.shape
    row_bytes = 2 * c * (N_DEV * parts.dtype.itemsize + 7 * 4)
    tr = _pick(r, [t for t in (256, 128, 64, 32, 16) if t * row_bytes <= VMEM_BLOCK_BUDGET])

    def body(p_ref, w_ref, m_ref, v_ref, g_ref, d_ref, m2_ref, v2_ref):
        gv = p_ref[0].astype(f32)
        for s in range(1, N_DEV):
            gv = gv + p_ref[s].astype(f32)
        g_ref[...] = gv
        m2 = ADAM_B1 * m_ref[...] + (1.0 - ADAM_B1) * gv
        v2 = ADAM_B2 * v_ref[...] + (1.0 - ADAM_B2) * jnp.square(gv)
        m_hat = m2 / (1.0 - ADAM_B1 ** ADAM_STEP)
        v_hat = v2 / (1.0 - ADAM_B2 ** ADAM_STEP)
        d_ref[...] = -ADAM_LR * (m_hat / (jnp.sqrt(v_hat) + ADAM_EPS) + ADAM_WD * w_ref[...])
        m2_ref[...] = m2
        v2_ref[...] = v2

    spec = pl.BlockSpec((tr, c), lambda i: (i, 0))
    shp = jax.ShapeDtypeStruct((r, c), f32)
    return pl.pallas_call(
        body, name=name, grid=(r // tr,),
        in_specs=[pl.BlockSpec((N_DEV, tr, c), lambda i: (0, i, 0))] + [spec] * 3,
        out_specs=[spec] * 4, out_shape=[shp] * 4,
        compiler_params=pltpu.CompilerParams(dimension_semantics=("parallel",)),
    )(parts, w, m, v)


def _train_step(a):
    small_names, small_axes = [n for n, _ in SMALL], [ax for _, ax in SMALL]
    small_shapes = [a[n].shape for n in small_names]
    rep_shapes = [a[n].shape for n in REPLICATED]

    big_axis = dict(BIG)

    def shards(names):
        return [a[n][0].astype(bf16) for n in names]

    def natural(names, gathered):
        w = {}
        for n, g in zip(names, gathered):
            if big_axis[n] == 1:
                w[n] = g.reshape(-1, g.shape[2])
            else:
                w[n] = jnp.transpose(g, (1, 0, 2)).reshape(g.shape[1], -1)
        return w

    def to_owner(nat):
        blocks = []
        for n, g in nat.items():
            if big_axis[n] == 1:
                g = g.reshape(N_DEV, -1, g.shape[1])
            else:
                g = jnp.transpose(g.reshape(g.shape[0], N_DEV, -1), (1, 0, 2))
            blocks.append(g.astype(bf16))
        return blocks

    small_flat = _pack([a[n] for n in small_names])
    got0 = _run_comm(_Gather(shards(("ssd_w_in",)) + [small_flat]), name="gather_ssd_weights")
    small = dict(zip(small_names, _unpack_gathered(got0[-1], small_shapes, small_axes)))
    for n in REPLICATED:
        small[n] = a[n] if n in ("ln_g", "ln_b") else a[n][0]
    rest_names = LAYER_BIG[2] + LAYER_BIG[3]
    under_scan = LAYER_BIG[1] + ("ssd_w_out",)
    big_of_layer = (lambda got: natural(("ssd_w_in",), got0[:1]),
                    lambda got: natural(LAYER_BIG[1], got[:len(LAYER_BIG[1])]),
                    lambda got: natural(LAYER_BIG[2], got[:len(LAYER_BIG[2])]),
                    lambda got: natural(LAYER_BIG[3], got[len(LAYER_BIG[2]):]))
    def ssd_in_sides(nat_w_in):
        (blocks,) = to_owner({"ssd_w_in": nat_w_in})
        cut = blocks.shape[1] * 5 // 8
        return _Exchange([blocks[:, :cut]]), _Exchange([blocks[:, cut:]])

    loss, grad_x, g_small, nats, hosted_parts = _sequence_step(
        a["x"][0], a["positions"][0], a["loss_target"][0], small, big_of_layer,
        gather_sides=(_Gather(shards(under_scan)), _Gather(shards(rest_names))),
        exchange_side=lambda nat: _Exchange(to_owner(nat)), ssd_in_sides=ssd_in_sides,
        ssd_wout_of=lambda got: natural(("ssd_w_out",), got[-1:])["ssd_w_out"])
    parts_in = hosted_parts[3]
    parts_of = {"ssd_w_in": jnp.concatenate([parts_in[0][0], parts_in[1][0]], axis=1)}
    for names, parts in zip((LAYER_BIG[1] + ("ssd_w_out",), LAYER_BIG[2], LAYER_BIG[3]), hosted_parts[:3]):
        parts_of.update(zip(names, parts, strict=True))
    small_to_owner = jax.linear_transpose(
        lambda s: _unpack_gathered(s, small_shapes, small_axes), jax.ShapeDtypeStruct(got0[-1].shape, f32))
    (g_small_blocks,) = small_to_owner([g_small[n] for n in small_names])
    g_rep_local = _pack([g_small[n] if n in ("ln_g", "ln_b") else g_small[n][None] for n in REPLICATED])
    small_parts, g_rep_all = _run_comm(_Both(_Exchange([g_small_blocks]), _Gather([g_rep_local])),
                                       name="exchange_small_gradients")

    out = {"grad": {}, "delta": {}, "new_m": {}, "new_v": {}}
    kinds = ("grad", "delta", "new_m", "new_v")
    for n, _ in BIG:
        res = _sum_adamw(parts_of[n], a[n][0], a["m_" + n][0], a["v_" + n][0], name="adamw_" + n)
        for kind, r in zip(kinds, res):
            out[kind][n] = r[None]
    upd_small = _sum_adamw(small_parts, small_flat, _pack([a["m_" + n] for n in small_names]),
                           _pack([a["v_" + n] for n in small_names]), name="adamw_small_sharded")
    for kind, r in zip(kinds, upd_small):
        out[kind].update(zip(small_names, _unpack(r, small_shapes)))
    upd_rep = _sum_adamw(g_rep_all, _pack([a[n] for n in REPLICATED]), _pack([a["m_" + n] for n in REPLICATED]),
                         _pack([a["v_" + n] for n in REPLICATED]), name="adamw_replicated")
    for kind, r in zip(kinds, upd_rep):
        out[kind].update(zip(REPLICATED, _unpack(r, rep_shapes)))
    loss_total = lax.psum(loss[0, 0], ("x", "y", "c"))
    return (loss_total, grad_x[None],
            *[out[kind][n] for kind in ("grad", "delta", "new_m", "new_v") for n in WEIGHT_ORDER])


def kernel(x, positions, ln_g, ln_b, ssd_w_in, ssd_conv_w, ssd_conv_b, ssd_dt_bias, ssd_a_log, ssd_d, ssd_norm_w, ssd_w_out, fox_w_in, fox_f_bias, fox_w_out, mla_w_in, mla_q_norm, mla_kv_norm, mla_w_q_up, mla_w_kv_up, mla_w_out, s5_w_in, s5_lambda_re, s5_lambda_im, s5_log_step, s5_b_re, s5_b_im, s5_c_re, s5_c_im, s5_d, s5_w_glu, s5_b_glu, s5_w_out, loss_target, m_ln_g, m_ln_b, m_ssd_w_in, m_ssd_conv_w, m_ssd_conv_b, m_ssd_dt_bias, m_ssd_a_log, m_ssd_d, m_ssd_norm_w, m_ssd_w_out, m_fox_w_in, m_fox_f_bias, m_fox_w_out, m_mla_w_in, m_mla_q_norm, m_mla_kv_norm, m_mla_w_q_up, m_mla_w_kv_up, m_mla_w_out, m_s5_w_in, m_s5_lambda_re, m_s5_lambda_im, m_s5_log_step, m_s5_b_re, m_s5_b_im, m_s5_c_re, m_s5_c_im, m_s5_d, m_s5_w_glu, m_s5_b_glu, m_s5_w_out, v_ln_g, v_ln_b, v_ssd_w_in, v_ssd_conv_w, v_ssd_conv_b, v_ssd_dt_bias, v_ssd_a_log, v_ssd_d, v_ssd_norm_w, v_ssd_w_out, v_fox_w_in, v_fox_f_bias, v_fox_w_out, v_mla_w_in, v_mla_q_norm, v_mla_kv_norm, v_mla_w_q_up, v_mla_w_kv_up, v_mla_w_out, v_s5_w_in, v_s5_lambda_re, v_s5_lambda_im, v_s5_log_step, v_s5_b_re, v_s5_b_im, v_s5_c_re, v_s5_c_im, v_s5_d, v_s5_w_glu, v_s5_b_glu, v_s5_w_out):
    return _train_step(dict(locals()))
```

```python
import functools
import math

import jax
import jax.numpy as jnp
from jax import lax
from jax.experimental import pallas as pl
from jax.experimental.pallas import tpu as pltpu

f32 = jnp.float32
bf16 = jnp.bfloat16

N_DEV = 8
DEPTH = 4
ALPHA = (2.0 * DEPTH) ** 0.25
LN_EPS = 1e-5
RMS_EPS = 1e-6
LANE = 128

SSD_STATE = 128
SSD_HEADDIM = 64
SSD_CHUNK = 128
FOX_HEAD_DIM = 128
MLA_NOPE = 128
MLA_ROPE = 64
MLA_V = 128
ROPE_BASE = 10000.0
S5_GROUP = 16
S5_BLOCK_GROUPS = 8
S5_CHUNK = 128

ADAM_LR = 0.001
ADAM_B1 = 0.9
ADAM_B2 = 0.999
ADAM_EPS = 1e-08
ADAM_WD = 0.01
ADAM_STEP = 10

VMEM_BLOCK_BUDGET = 20 * 1024 * 1024
MM_FULL_K = 2048
HIGHEST = lax.Precision.HIGHEST


def _pick(n, cands):
    for c in cands:
        if n % c == 0:
            return c
    return n


def _dg(a, b, ca, cb):
    return lax.dot_general(a.astype(bf16), b.astype(bf16), (((ca,), (cb,)), ((), ())),
                           preferred_element_type=f32)


@functools.partial(jax.custom_vjp, nondiff_argnums=(2, 3))
def bdot(a, b, ca, cb):
    return _dg(a, b, ca, cb)


def _bdot_fwd(a, b, ca, cb):
    return _dg(a, b, ca, cb), (a, b)


def _bdot_bwd(ca, cb, res, g):
    a, b = res
    nb = 1 - cb
    ma = 1 - ca
    da = _dg(g, b, 1, nb) if ca == 1 else _dg(b, g, nb, 1)
    db = _dg(a, g, ma, 0) if cb == 0 else _dg(g, a, 0, ma)
    return da, db


bdot.defvjp(_bdot_fwd, _bdot_bwd)


def _silu(x):
    return x * jax.nn.sigmoid(x)


def _softplus(x):
    return jnp.maximum(x, 0.0) + jnp.log(1.0 + jnp.exp(-jnp.abs(x)))


def _mm(a, b, *, ta=False, tb=False, add=None, out_dtype=f32, name, side=None):
    m, k = (a.shape[1], a.shape[0]) if ta else a.shape
    n = b.shape[0] if tb else b.shape[1]
    kb = b.shape[1] if tb else b.shape[0]
    assert k == kb, (name, a.shape, b.shape)
    if k <= MM_FULL_K:
        tm = _pick(m, (512, 256, 128))
        tn = _pick(n, (1024, 512, 384, 256, 128))
        tk = k
    else:
        tm = _pick(m, (1024, 512, 256, 128))
        tn = _pick(n, (1024, 512, 384, 256, 128))
        tk = _pick(k, (1024, 512, 256, 128))
    nk = k // tk
    has_add = add is not None

    def body(*refs):
        if has_add:
            a_ref, b_ref, add_ref, o_ref, acc = refs
        else:
            a_ref, b_ref, o_ref, acc = refs
        kk = pl.program_id(2)

        def prod():
            return _dg(a_ref[...], b_ref[...], 0 if ta else 1, 1 if tb else 0)

        def finish(r):
            o_ref[...] = (r + add_ref[...] if has_add else r).astype(o_ref.dtype)

        if nk == 1:
            finish(prod())
            return

        @pl.when(kk == 0)
        def _():
            acc[...] = prod()

        @pl.when((kk > 0) & (kk < nk - 1))
        def _():
            acc[...] += prod()

        @pl.when(kk == nk - 1)
        def _():
            finish(acc[...] + prod())

    a_spec = (pl.BlockSpec((tk, tm), lambda i, j, kk: (kk, i)) if ta
              else pl.BlockSpec((tm, tk), lambda i, j, kk: (i, kk)))
    b_spec = (pl.BlockSpec((tn, tk), lambda i, j, kk: (j, kk)) if tb
              else pl.BlockSpec((tk, tn), lambda i, j, kk: (kk, j)))
    o_spec = pl.BlockSpec((tm, tn), lambda i, j, kk: (i, j))
    in_specs = [a_spec, b_spec] + ([o_spec] if has_add else [])
    args = (a, b) + ((add,) if has_add else ())
    if side is not None:
        (out,), got = _hosted_call(
            body, side, name=name, grid=(m // tm, n // tn, nk), in_specs=in_specs, out_specs=[o_spec],
            out_shape=[jax.ShapeDtypeStruct((m, n), out_dtype)], scratch_shapes=[pltpu.VMEM((tm, tn), f32)],
            args=args)
        return out, got
    return pl.pallas_call(
        body, name=name, grid=(m // tm, n // tn, nk),
        in_specs=in_specs, out_specs=o_spec,
        out_shape=jax.ShapeDtypeStruct((m, n), out_dtype),
        scratch_shapes=[pltpu.VMEM((tm, tn), f32)],
        compiler_params=pltpu.CompilerParams(dimension_semantics=("parallel", "parallel", "arbitrary")),
    )(*args)


def _row_operand(op):
    if isinstance(op, tuple):
        arr, w, cb = op
    else:
        arr, w, cb = op, op.shape[1], 0
    return arr, w, cb


def _rw_tm(t, widths):
    per_row = 2 * 4 * sum(widths)
    tm = 512
    while tm > 8 and (tm * per_row > VMEM_BLOCK_BUDGET or t % tm):
        tm //= 2
    return tm


def _rw_specs(ops, tm):
    specs, arrs = [], []
    for op in ops:
        arr, w, cb = _row_operand(op)
        specs.append(pl.BlockSpec((tm, w), functools.partial(lambda i, cb: (i, cb), cb=cb)))
        arrs.append(arr)
    return specs, arrs


def _param_spec(p):
    nd = p.ndim
    return pl.BlockSpec(p.shape, lambda i: (0,) * nd)


def _rw_fwd(fn, rows, params, out_widths, *, name):
    t = _row_operand(rows[0])[0].shape[0]
    tm = _rw_tm(t, [_row_operand(r)[1] for r in rows] + list(out_widths))
    nr, npar = len(rows), len(params)

    def body(*refs):
        vals = [r[...] for r in refs[:nr + npar]]
        outs = fn(*vals)
        for o_ref, v in zip(refs[nr + npar:], outs):
            o_ref[...] = v

    rspecs, rarrs = _rw_specs(rows, tm)
    return pl.pallas_call(
        body, name=name, grid=(t // tm,),
        in_specs=rspecs + [_param_spec(p) for p in params],
        out_specs=[pl.BlockSpec((tm, w), lambda i: (i, 0)) for w in out_widths],
        out_shape=[jax.ShapeDtypeStruct((t, w), f32) for w in out_widths],
        compiler_params=pltpu.CompilerParams(dimension_semantics=("parallel",)),
    )(*rarrs, *params)


def _rw_bwd(fn, rows, params, cots, *, name, n_const=0):
    t = _row_operand(rows[0])[0].shape[0]
    nr, npar, nc = len(rows), len(params), len(cots)
    nd = nr - n_const
    widths = [_row_operand(r)[1] for r in rows]
    tm = _rw_tm(t, widths + widths[:nd] + [c.shape[1] for c in cots])

    def body(*refs):
        rv = [r[...] for r in refs[:nr]]
        pv = [r[...] for r in refs[nr:nr + npar]]
        cv = [r[...] for r in refs[nr + npar:nr + npar + nc]]
        outs = refs[nr + npar + nc:]
        consts = rv[nd:]

        def f(*diff):
            return fn(*diff[:nd], *consts, *diff[nd:])

        _, vjp = jax.vjp(f, *rv[:nd], *pv)
        g = vjp(tuple(cv))
        for o_ref, v in zip(outs[:nd], g[:nd]):
            o_ref[...] = v
        if npar:
            @pl.when(pl.program_id(0) == 0)
            def _():
                for o_ref in outs[nd:]:
                    o_ref[...] = jnp.zeros_like(o_ref)

            for o_ref, v in zip(outs[nd:], g[nd:]):
                o_ref[...] += v

    rspecs, rarrs = _rw_specs(rows, tm)
    cspecs = [pl.BlockSpec((tm, c.shape[1]), lambda i: (i, 0)) for c in cots]
    return pl.pallas_call(
        body, name=name, grid=(t // tm,),
        in_specs=rspecs + [_param_spec(p) for p in params] + cspecs,
        out_specs=[pl.BlockSpec((tm, w), lambda i: (i, 0)) for w in widths[:nd]]
        + [_param_spec(p) for p in params],
        out_shape=[jax.ShapeDtypeStruct((t, w), f32) for w in widths[:nd]]
        + [jax.ShapeDtypeStruct(p.shape, f32) for p in params],
        compiler_params=pltpu.CompilerParams(dimension_semantics=("arbitrary",)),
    )(*rarrs, *params, *cots)


def _ln_fn(h, o, g, b):
    x = ALPHA * h + o
    mu = jnp.mean(x, -1, keepdims=True)
    var = jnp.mean(jnp.square(x - mu), -1, keepdims=True)
    return ((x - mu) * lax.rsqrt(var + LN_EPS) * g + b,)


def _gate_fn(o, z):
    return (o * _silu(z),)


def _make_ssd_post_fn(groups):
    def fn(y, z, nw):
        yz = y * _silu(z)
        gw = yz.shape[1] // groups
        outs = []
        for g in range(groups):
            blk = yz[:, g * gw:(g + 1) * gw]
            outs.append(blk * lax.rsqrt(jnp.mean(jnp.square(blk), -1, keepdims=True) + RMS_EPS))
        return (jnp.concatenate(outs, axis=1) * nw,)
    return fn


def _logf_fn(f, b):
    return (jax.nn.log_sigmoid(f + b),)


def _make_mla_norm_fn(rank):
    def fn(lat, qn, kvn):
        def rms(x, w):
            return x * lax.rsqrt(jnp.mean(jnp.square(x), -1, keepdims=True) + RMS_EPS) * w
        return rms(lat[:, :rank], qn), rms(lat[:, rank:], kvn)
    return fn


def _rope_block(xb, cos, sin):
    half = MLA_ROPE // 2
    x1, x2 = xb[:, :half], xb[:, half:MLA_ROPE]
    return jnp.concatenate([x1 * cos - x2 * sin, x1 * sin + x2 * cos, xb[:, MLA_ROPE:]], axis=1)


def _make_rope_fn(heads):
    def fn(x, cos, sin):
        return (jnp.concatenate([_rope_block(x[:, h * LANE:(h + 1) * LANE], cos, sin)
                                 for h in range(heads)], axis=1),)
    return fn


def _s5_act_fn(ys, u, d):
    return (jax.nn.gelu(ys + d * u),)


def _s5_glu_fn(y1, gp, z, bg):
    return (y1 * jax.nn.sigmoid(gp + bg) * _silu(z),)


def _loss_head(y, tgt):
    t, d = y.shape
    tm = _rw_tm(t, [d, d, d])

    def body(y_ref, t_ref, dy_ref, l_ref):
        @pl.when(pl.program_id(0) == 0)
        def _():
            l_ref[...] = jnp.zeros_like(l_ref)

        e = y_ref[...] - t_ref[...]
        dy_ref[...] = e * (1.0 / d)
        l_ref[...] += 0.5 * jnp.sum(jnp.mean(jnp.square(e), axis=-1, keepdims=True), axis=0, keepdims=True)

    spec = pl.BlockSpec((tm, d), lambda i: (i, 0))
    return pl.pallas_call(
        body, name="loss_head", grid=(t // tm,), in_specs=[spec, spec],
        out_specs=[spec, pl.BlockSpec((1, 1), lambda i: (0, 0))],
        out_shape=[jax.ShapeDtypeStruct((t, d), f32), jax.ShapeDtypeStruct((1, 1), f32)],
        compiler_params=pltpu.CompilerParams(dimension_semantics=("arbitrary",)),
    )(y, tgt)


def _attn_tile(t):
    return _pick(t, (512, 256, 128))


ATTN_HEADS_PER_STEP = 8
ATTN_BWD_HEADS_PER_STEP = 4
SSD_GROUPS_PER_STEP = 1
S5_BLOCKS_PER_STEP = 2


def _logits_t(k1, q1, k2, q2, ck_col, scale, key0, query0):
    s = _dg(k1, q1, 1, 1)
    if q2 is not None:
        s = s + _dg(k2, q2, 1, 1)
    s = s * scale
    if ck_col is not None:
        s = s - ck_col
    key = key0 + lax.broadcasted_iota(jnp.int32, s.shape, 0)
    query = query0 + lax.broadcasted_iota(jnp.int32, s.shape, 1)
    return jnp.where(key <= query, s, -jnp.inf)


def _lane_bcast(v):
    return jnp.broadcast_to(v[:, :, None], v.shape + (LANE,))


def _attn_fwd(q1, q1o, k1, k1o, v, vo, q2, k2, ck, *, heads, scale, name, side=None):
    t = q1.shape[0]
    tq = tk = _attn_tile(t)
    nq = t // tq
    has2, hasb = q2 is not None, ck is not None
    hp = math.gcd(ATTN_HEADS_PER_STEP, heads)
    wd = hp * LANE
    assert heads % hp == 0 and q1o % hp == 0 and k1o % hp == 0 and vo % hp == 0

    def body(*refs):
        it = iter(refs)
        q1r, k1r, vr = next(it), next(it), next(it)
        q2r, k2r = (next(it), next(it)) if has2 else (None, None)
        ckr = next(it) if hasb else None
        o_r, lse_r, m_s, l_s, acc = next(it), next(it), next(it), next(it), next(it)
        i, j = pl.program_id(1), pl.program_id(2)

        @pl.when(j == 0)
        def _():
            m_s[...] = jnp.full_like(m_s, -jnp.inf)
            l_s[...] = jnp.zeros_like(l_s)
            acc[...] = jnp.zeros_like(acc)

        @pl.when(j <= i)
        def _():
            k2v = k2r[...].astype(bf16) if has2 else None
            for e in range(hp):
                ln = slice(e * LANE, (e + 1) * LANE)
                q1v, k1v, vv = (r[:, ln].astype(bf16) for r in (q1r, k1r, vr))
                q2v = q2r[:, ln].astype(bf16) if has2 else None
                s = _logits_t(k1v, q1v, k2v, q2v, ckr[e][:, 0:1] if hasb else None, scale, j * tk, i * tq)
                m_prev = m_s[e]
                m_new = jnp.maximum(m_prev, jnp.max(s, axis=0, keepdims=True))
                p = jnp.exp(s - m_new)
                a = jnp.exp(m_prev - m_new)
                l_s[e] = a * l_s[e] + jnp.sum(p, axis=0, keepdims=True)
                acc[e] = a * acc[e] + _dg(vv, p, 0, 0)
                m_s[e] = m_new

        @pl.when(j == i)
        def _():
            for e in range(hp):
                o_r[:, e * LANE:(e + 1) * LANE] = (acc[e] / l_s[e]).T
                lse_r[e, 0] = m_s[e] + jnp.log(l_s[e])

    def qmap(off):
        return lambda h, i, j: (i, off // hp + h)

    def kmap(off):
        return lambda h, i, j: (jnp.minimum(j, i), off // hp + h)

    in_specs = [pl.BlockSpec((tq, wd), qmap(q1o)), pl.BlockSpec((tk, wd), kmap(k1o)),
                pl.BlockSpec((tk, wd), kmap(vo))]
    args = [q1, k1, v]
    if has2:
        in_specs += [pl.BlockSpec((tq, wd), qmap(0)),
                     pl.BlockSpec((tk, LANE), lambda h, i, j: (jnp.minimum(j, i), 0))]
        args += [q2, k2]
    if hasb:
        in_specs += [pl.BlockSpec((hp, tk, LANE), lambda h, i, j: (h, jnp.minimum(j, i), 0))]
        args += [ck]
    return _hosted_call(
        body, side, name=name, grid=(heads // hp, nq, nq), in_specs=in_specs,
        out_specs=[pl.BlockSpec((tq, wd), lambda h, i, j: (i, h)),
                   pl.BlockSpec((hp, 1, 1, tq), lambda h, i, j: (h, i, 0, 0))],
        out_shape=[jax.ShapeDtypeStruct((t, heads * LANE), f32),
                   jax.ShapeDtypeStruct((heads, nq, 1, tq), f32)],
        scratch_shapes=[pltpu.VMEM((hp, 1, tq), f32), pltpu.VMEM((hp, 1, tq), f32),
                        pltpu.VMEM((hp, LANE, tq), f32)],
        args=args)


def _row_sums_as_row(x, first_lane_only=False):
    sel = jnp.ones((8, x.shape[1]), f32)
    if first_lane_only:
        sel = (lax.broadcasted_iota(jnp.int32, sel.shape, 1) == 0).astype(f32)
    return lax.dot_general(sel, x, (((1,), (1,)), ((), ())), precision=HIGHEST, preferred_element_type=f32)[0:1]


def _attn_delta(do, o, *, heads, name):
    t = do.shape[0]
    tq = _attn_tile(t)
    hp = math.gcd(ATTN_HEADS_PER_STEP, heads)

    def body(do_ref, o_ref, d_ref):
        for e in range(hp):
            ln = slice(e * LANE, (e + 1) * LANE)
            d_ref[e, 0] = _row_sums_as_row(do_ref[:, ln] * o_ref[:, ln])

    spec = pl.BlockSpec((tq, hp * LANE), lambda i, h: (i, h))
    return pl.pallas_call(
        body, name=name, grid=(t // tq, heads // hp), in_specs=[spec, spec],
        out_specs=pl.BlockSpec((hp, 1, 1, tq), lambda i, h: (h, i, 0, 0)),
        out_shape=jax.ShapeDtypeStruct((heads, t // tq, 1, tq), f32),
        compiler_params=pltpu.CompilerParams(dimension_semantics=("parallel", "parallel")),
    )(do, o)


def _attn_bwd(q1, q1o, k1, k1o, v, vo, q2, k2, ck, do, lse, delta, *, heads, scale, name, side=None):
    t = q1.shape[0]
    tq = tk = _attn_tile(t)
    nq = t // tq
    has2, hasb = q2 is not None, ck is not None
    hp = ATTN_BWD_HEADS_PER_STEP // 2 if has2 else ATTN_BWD_HEADS_PER_STEP
    wd = hp * LANE
    assert heads % hp == 0 and q1o % hp == 0 and k1o % hp == 0 and vo % hp == 0

    def body(*refs):
        it = iter(refs)
        q1r, k1r, vr = next(it), next(it), next(it)
        q2r, k2r = (next(it), next(it)) if has2 else (None, None)
        ckr = next(it) if hasb else None
        dor, lser, dlr = next(it), next(it), next(it)
        dq1r, dk1r, dvr = next(it), next(it), next(it)
        dq2r, dk2r = (next(it), next(it)) if has2 else (None, None)
        drowr, dckr = (next(it), next(it)) if hasb else (None, None)
        ak, av = next(it), next(it)
        ac = next(it) if hasb else None
        h, j, i = pl.program_id(0), pl.program_id(1), pl.program_id(2)

        @pl.when((j == 0) & (i == 0))
        def _():
            dq1r[...] = jnp.zeros_like(dq1r)
            if has2:
                dq2r[...] = jnp.zeros_like(dq2r)
            if hasb:
                drowr[...] = jnp.zeros_like(drowr)

        if has2:
            @pl.when((h == 0) & (j == 0) & (i == 0))
            def _():
                dk2r[...] = jnp.zeros_like(dk2r)

        @pl.when(i == j)
        def _():
            ak[...] = jnp.zeros_like(ak)
            av[...] = jnp.zeros_like(av)
            if hasb:
                ac[...] = jnp.zeros_like(ac)

        @pl.when(i >= j)
        def _():
            rows = pl.ds(pl.multiple_of(i * tq, tq), tq)
            cols = pl.ds(pl.multiple_of(j * tk, tk), tk)
            k2v = k2r[...].astype(bf16) if has2 else None
            for e in range(hp):
                ln = slice(e * LANE, (e + 1) * LANE)
                q1v, k1v, vv, dov = (r[:, ln].astype(bf16) for r in (q1r, k1r, vr, dor))
                q2v = q2r[:, ln].astype(bf16) if has2 else None
                s = _logits_t(k1v, q1v, k2v, q2v, ckr[e][:, 0:1] if hasb else None, scale, j * tk, i * tq)
                p = jnp.exp(s - lser[e, 0])
                dp = _dg(vv, dov, 1, 1)
                dsr = p * (dp - dlr[e, 0])
                ds = (dsr * scale).astype(bf16)
                av[:, ln] += _dg(p, dov, 1, 0)
                ak[:, ln] += _dg(ds, q1v, 1, 0)
                dq1r[rows, ln] += _dg(ds, k1v, 0, 0)
                if has2:
                    dq2r[rows, ln] += _dg(ds, k2v, 0, 0)
                    dk2r[cols, :] += _dg(ds, q2v, 1, 0)
                if hasb:
                    drowr[e, pl.ds(i, 1), :] += jnp.sum(dsr, axis=0, keepdims=True)
                    ac[e] -= jnp.broadcast_to(jnp.sum(dsr, axis=1, keepdims=True), (tk, LANE))

        @pl.when(i == nq - 1)
        def _():
            dk1r[...] = ak[...]
            dvr[...] = av[...]
            if hasb:
                for e in range(hp):
                    dckr[e] = _row_sums_as_row(ac[e], first_lane_only=True)

    def qmap(off):
        return lambda h, j, i: (jnp.maximum(i, j), off // hp + h)

    def kmap(off):
        return lambda h, j, i: (j, off // hp + h)

    in_specs = [pl.BlockSpec((tq, wd), qmap(q1o)), pl.BlockSpec((tk, wd), kmap(k1o)),
                pl.BlockSpec((tk, wd), kmap(vo))]
    args = [q1, k1, v]
    if has2:
        in_specs += [pl.BlockSpec((tq, wd), qmap(0)), pl.BlockSpec((tk, LANE), lambda h, j, i: (j, 0))]
        args += [q2, k2]
    if hasb:
        in_specs += [pl.BlockSpec((hp, tk, LANE), lambda h, j, i: (h, j, 0))]
        args += [ck]
    row3 = pl.BlockSpec((hp, 1, 1, tq), lambda h, j, i: (h, jnp.maximum(i, j), 0, 0))
    in_specs += [pl.BlockSpec((tq, wd), qmap(0)), row3, row3]
    args += [do, lse, delta]
    hd = jax.ShapeDtypeStruct((t, heads * LANE), f32)
    head_cols = pl.BlockSpec((t, wd), lambda h, j, i: (0, h))
    kv_blk = pl.BlockSpec((tk, wd), kmap(0))
    out_specs, out_shape = [head_cols, kv_blk, kv_blk], [hd, hd, hd]
    scratch = [pltpu.VMEM((tk, wd), f32)] * 2
    if has2:
        out_specs += [head_cols, pl.BlockSpec((t, LANE), lambda h, j, i: (0, 0))]
        out_shape += [hd, jax.ShapeDtypeStruct((t, LANE), f32)]
    if hasb:
        out_specs += [pl.BlockSpec((hp, nq, tq), lambda h, j, i: (h, 0, 0)),
                      pl.BlockSpec((hp, 1, tk), lambda h, j, i: (h, 0, j))]
        out_shape += [jax.ShapeDtypeStruct((heads, nq, tq), f32), jax.ShapeDtypeStruct((heads, 1, t), f32)]
        scratch.append(pltpu.VMEM((hp, tk, LANE), f32))
    return _hosted_call(body, side, name=name, grid=(heads // hp, nq, nq), in_specs=in_specs, out_specs=out_specs,
                        out_shape=out_shape, scratch_shapes=scratch, args=args)


def _cumsum_rows(x, *, reverse, name):
    t, w = x.shape
    blk = 128
    nb = t // blk

    def body(x_ref, o_ref):
        r = lax.broadcasted_iota(jnp.int32, (blk, blk), 0)
        c = lax.broadcasted_iota(jnp.int32, (blk, blk), 1)
        tri = ((r <= c) if reverse else (r >= c)).astype(f32)
        carry = jnp.zeros((1, w), f32)
        for b in (reversed(range(nb)) if reverse else range(nb)):
            seg = x_ref[b * blk:(b + 1) * blk, :]
            cs = jnp.dot(tri, seg, precision=HIGHEST, preferred_element_type=f32) + carry
            o_ref[b * blk:(b + 1) * blk, :] = cs
            carry = cs[0:1, :] if reverse else cs[blk - 1:blk, :]

    return pl.pallas_call(body, name=name, out_shape=jax.ShapeDtypeStruct((t, w), f32))(x)


CONV_PAD = 8


def _conv_fwd(x, w, b, *, name):
    t, c = x.shape
    tc = LANE
    taps = w.shape[0]
    rb = _pick(t, (512, 256, 128))
    assert taps - 1 <= CONV_PAD

    def body(x_ref, w_ref, b_ref, o_ref, xpad):
        xpad[0:CONV_PAD, :] = jnp.zeros((CONV_PAD, tc), f32)
        xpad[CONV_PAD:CONV_PAD + t, :] = x_ref[...]
        wv, bv = w_ref[...], b_ref[...]
        for r0 in range(0, t, rb):
            first = r0 + CONV_PAD - (taps - 1)
            y = bv
            for k in range(taps):
                y = y + wv[k:k + 1, :] * xpad[first + k:first + k + rb, :]
            o_ref[r0:r0 + rb, :] = _silu(y)

    xs = pl.BlockSpec((t, tc), lambda i: (0, i))
    return pl.pallas_call(
        body, name=name, grid=(c // tc,),
        in_specs=[xs, pl.BlockSpec((taps, tc), lambda i: (0, i)), pl.BlockSpec((1, tc), lambda i: (0, i))],
        out_specs=xs, out_shape=jax.ShapeDtypeStruct((t, c), f32),
        scratch_shapes=[pltpu.VMEM((t + CONV_PAD, tc), f32)],
        compiler_params=pltpu.CompilerParams(dimension_semantics=("parallel",)),
    )(x, w, b)


def _conv_bwd(x, w, b, ds, *, name):
    t, c = x.shape
    tc = LANE
    taps = w.shape[0]

    rb = _pick(t, (512, 256, 128))
    assert taps - 1 <= CONV_PAD

    def body(x_ref, w_ref, b_ref, ds_ref, dx_ref, dw_ref, db_ref, xpad, dypad):
        xpad[0:CONV_PAD, :] = jnp.zeros((CONV_PAD, tc), f32)
        xpad[CONV_PAD:CONV_PAD + t, :] = x_ref[...]
        dypad[t:t + CONV_PAD, :] = jnp.zeros((CONV_PAD, tc), f32)
        wv, bv = w_ref[...], b_ref[...]
        dw = [jnp.zeros((1, tc), f32) for _ in range(taps)]
        db = jnp.zeros((1, tc), f32)
        for r0 in range(0, t, rb):
            first = r0 + CONV_PAD - (taps - 1)
            xsh = [xpad[first + k:first + k + rb, :] for k in range(taps)]
            y = bv
            for k in range(taps):
                y = y + wv[k:k + 1, :] * xsh[k]
            sig = jax.nn.sigmoid(y)
            dy = ds_ref[r0:r0 + rb, :] * (sig * (1.0 + y * (1.0 - sig)))
            dypad[r0:r0 + rb, :] = dy
            for k in range(taps):
                dw[k] = dw[k] + jnp.sum(dy * xsh[k], axis=0, keepdims=True)
            db = db + jnp.sum(dy, axis=0, keepdims=True)
        for r0 in range(0, t, rb):
            dx = jnp.zeros((rb, tc), f32)
            for k in range(taps):
                first = r0 + (taps - 1) - k
                dx = dx + wv[k:k + 1, :] * dypad[first:first + rb, :]
            dx_ref[r0:r0 + rb, :] = dx
        dw_ref[...] = jnp.concatenate(dw, axis=0)
        db_ref[...] = db

    xs = pl.BlockSpec((t, tc), lambda i: (0, i))
    ws = pl.BlockSpec((taps, tc), lambda i: (0, i))
    bs = pl.BlockSpec((1, tc), lambda i: (0, i))
    return pl.pallas_call(
        body, name=name, grid=(c // tc,), in_specs=[xs, ws, bs, xs], out_specs=[xs, ws, bs],
        out_shape=[jax.ShapeDtypeStruct((t, c), f32), jax.ShapeDtypeStruct((taps, c), f32),
                   jax.ShapeDtypeStruct((1, c), f32)],
        scratch_shapes=[pltpu.VMEM((t + CONV_PAD, tc), f32), pltpu.VMEM((t + CONV_PAD, tc), f32)],
        compiler_params=pltpu.CompilerParams(dimension_semantics=("parallel",)),
    )(x, w, b, ds)


def _ssd_chunk(x, bm, cm, dtr, st, dtb, alog, dsk, *, e_heads):
    ln, p = x.shape[0], SSD_HEADDIM
    dt = _softplus(dtr + dtb)
    da = dt * (-jnp.exp(alog))
    r = lax.broadcasted_iota(jnp.int32, (ln, ln), 0)
    c = lax.broadcasted_iota(jnp.int32, (ln, ln), 1)
    lower = r >= c
    acs = jnp.dot(lower.astype(f32), da, precision=HIGHEST, preferred_element_type=f32)
    acs_t = acs.T

    def per_head(v):
        return jnp.concatenate([jnp.broadcast_to(v[:, e:e + 1], (v.shape[0], p)) for e in range(e_heads)], axis=1)

    dt_x, acs_x = per_head(dt), per_head(acs)
    last_x = acs_x[ln - 1:ln, :]
    xdt = x * dt_x
    cb = bdot(cm, bm, 1, 1)
    y_off = bdot(cm, st, 1, 0) * jnp.exp(acs_x)
    st_new = st * jnp.exp(last_x) + bdot(bm, xdt * jnp.exp(last_x - acs_x), 0, 0)
    ys = []
    for e in range(e_heads):
        seg = acs[:, e:e + 1] - acs_t[e:e + 1, :]
        dec = jnp.exp(jnp.where(lower, seg, -jnp.inf))
        ys.append(bdot(cb * dec, xdt[:, e * p:(e + 1) * p], 1, 0))
    y = jnp.concatenate(ys, axis=1) + y_off + per_head(dsk) * x
    return y, st_new


def _ssd_specs(t, d_inner, groups):
    ln, n = SSD_CHUNK, SSD_STATE
    gw = d_inner // groups
    nc = t // ln
    return ln, n, gw, nc


def _ssd_fwd(xbc, dtp, dtb, alog, dsk, *, d_inner, groups, name, side=None):
    t = xbc.shape[0]
    ln, n, gw, nc = _ssd_specs(t, d_inner, groups)
    e_heads = gw // SSD_HEADDIM
    gp = SSD_GROUPS_PER_STEP
    assert groups % gp == 0 and (d_inner // n) % gp == 0
    boff = d_inner // n // gp
    ng = groups // gp

    def body(x_ref, b_ref, c_ref, dt_ref, dtb_ref, alog_ref, dsk_ref, y_ref, save_ref, st_scr):
        c, g = pl.program_id(0), pl.program_id(1)
        for e in range(gp):
            gi = g * gp + e

            @pl.when(c == 0)
            def _():
                st_scr[gi] = jnp.zeros((n, gw), f32)

            st = st_scr[gi]
            save_ref[e] = st
            y, st_new = _ssd_chunk(x_ref[:, e * gw:(e + 1) * gw], b_ref[:, e * n:(e + 1) * n],
                                   c_ref[:, e * n:(e + 1) * n], dt_ref[:, e * LANE:(e + 1) * LANE], st,
                                   dtb_ref[e], alog_ref[e], dsk_ref[e], e_heads=e_heads)
            y_ref[:, e * gw:(e + 1) * gw] = y
            st_scr[gi] = st_new

    par = pl.BlockSpec((gp, 1, LANE), lambda c, g: (g, 0, 0))
    return _hosted_call(
        body, side, name=name, grid=(nc, ng),
        in_specs=[pl.BlockSpec((ln, gp * gw), lambda c, g: (c, g)),
                  pl.BlockSpec((ln, gp * n), lambda c, g: (c, boff + g)),
                  pl.BlockSpec((ln, gp * n), lambda c, g: (c, boff + ng + g)),
                  pl.BlockSpec((ln, gp * LANE), lambda c, g: (c, g)), par, par, par],
        out_specs=[pl.BlockSpec((ln, gp * gw), lambda c, g: (c, g)),
                   pl.BlockSpec((gp, n, gw), lambda c, g: (c * ng + g, 0, 0))],
        out_shape=[jax.ShapeDtypeStruct((t, d_inner), f32),
                   jax.ShapeDtypeStruct((nc * groups, n, gw), f32)],
        scratch_shapes=[pltpu.VMEM((groups, n, gw), f32)],
        args=(xbc, xbc, xbc, dtp, dtb, alog, dsk))


def _ssd_bwd(xbc, dtp, dtb, alog, dsk, saved, dy, *, d_inner, groups, name, side=None):
    t = xbc.shape[0]
    ln, n, gw, nc = _ssd_specs(t, d_inner, groups)
    e_heads = gw // SSD_HEADDIM
    gp = SSD_GROUPS_PER_STEP
    assert groups % gp == 0 and (d_inner // n) % gp == 0
    boff = d_inner // n // gp
    ng = groups // gp

    def body(x_ref, b_ref, c_ref, dt_ref, dtb_ref, alog_ref, dsk_ref, save_ref, dy_ref,
             dx_ref, db_ref, dc_ref, ddt_ref, dpar_ref, dst_scr):
        c, g = pl.program_id(0), pl.program_id(1)

        @pl.when((c == 0) & (g == 0))
        def _():
            dpar_ref[...] = jnp.zeros_like(dpar_ref)

        fn = functools.partial(_ssd_chunk, e_heads=e_heads)
        for e in range(gp):
            gi = g * gp + e
            xl, nl, dl = (slice(e * w, (e + 1) * w) for w in (gw, n, LANE))

            @pl.when(c == 0)
            def _():
                dst_scr[gi] = jnp.zeros((n, gw), f32)

            _, vjp = jax.vjp(fn, x_ref[:, xl], b_ref[:, nl], c_ref[:, nl], dt_ref[:, dl], save_ref[e],
                             dtb_ref[e], alog_ref[e], dsk_ref[e])
            gx, gb, gc, gdt, gst, gdtb, galog, gdsk = vjp((dy_ref[:, xl], dst_scr[gi]))
            dx_ref[:, xl] = gx
            db_ref[:, nl] = gb
            dc_ref[:, nl] = gc
            ddt_ref[:, dl] = gdt
            dst_scr[gi] = gst
            dpar_ref[gi, 0:1, :] += gdtb
            dpar_ref[gi, 1:2, :] += galog
            dpar_ref[gi, 2:3, :] += gdsk

    def rc(c):
        return nc - 1 - c

    par = pl.BlockSpec((gp, 1, LANE), lambda c, g: (g, 0, 0))
    xs = pl.BlockSpec((ln, gp * gw), lambda c, g: (rc(c), g))
    ns = pl.BlockSpec((ln, gp * n), lambda c, g: (rc(c), g))
    return _hosted_call(
        body, side, name=name, grid=(nc, ng),
        in_specs=[xs,
                  pl.BlockSpec((ln, gp * n), lambda c, g: (rc(c), boff + g)),
                  pl.BlockSpec((ln, gp * n), lambda c, g: (rc(c), boff + ng + g)),
                  pl.BlockSpec((ln, gp * LANE), lambda c, g: (rc(c), g)), par, par, par,
                  pl.BlockSpec((gp, n, gw), lambda c, g: (rc(c) * ng + g, 0, 0)), xs],
        out_specs=[xs, ns, ns, pl.BlockSpec((ln, gp * LANE), lambda c, g: (rc(c), g)),
                   pl.BlockSpec((groups, 8, LANE), lambda c, g: (0, 0, 0))],
        out_shape=[jax.ShapeDtypeStruct((t, d_inner), f32),
                   jax.ShapeDtypeStruct((t, groups * n), f32),
                   jax.ShapeDtypeStruct((t, groups * n), f32),
                   jax.ShapeDtypeStruct((t, groups * LANE), f32),
                   jax.ShapeDtypeStruct((groups, 8, LANE), f32)],
        scratch_shapes=[pltpu.VMEM((groups, n, gw), f32)],
        args=(xbc, xbc, xbc, dtp, dtb, alog, dsk, saved, dy))


S5_TOP = 8


def _cmul(ar, ai, br, bi):
    return ar * br - ai * bi, ar * bi + ai * br


def _s5_scan(scr, base, ln, pw_ref, sp, *, reverse):
    s, k = 1, 0
    while s < ln:
        pr = pw_ref[0, k:k + 1, :sp]
        pi = pw_ref[0, k:k + 1, sp:]
        if reverse:
            pi = -pi
            src, dst = base + s, base
        else:
            src, dst = base, base + s
        ar, ai = scr[src:src + ln - s, :sp], scr[src:src + ln - s, sp:]
        mr, mi = _cmul(ar, ai, pr, pi)
        scr[dst:dst + ln - s, :sp] = scr[dst:dst + ln - s, :sp] + mr
        scr[dst:dst + ln - s, sp:] = scr[dst:dst + ln - s, sp:] + mi
        s *= 2
        k += 1


def _s5_states(u_ref, b_ref, lam_ref, pw_ref, scr, carry_r, carry_i, ln, sp):
    scr[S5_TOP:S5_TOP + ln, :] = _dg(u_ref[...], b_ref[0], 1, 0)
    mr, mi = _cmul(carry_r, carry_i, lam_ref[0][:, :sp], lam_ref[0][:, sp:])
    scr[S5_TOP:S5_TOP + 1, :sp] = scr[S5_TOP:S5_TOP + 1, :sp] + mr
    scr[S5_TOP:S5_TOP + 1, sp:] = scr[S5_TOP:S5_TOP + 1, sp:] + mi
    _s5_scan(scr, S5_TOP, ln, pw_ref, sp, reverse=False)


def _s5_dims(t, u_width):
    cbw = S5_BLOCK_GROUPS * S5_GROUP
    return S5_CHUNK, cbw, u_width // cbw, t // S5_CHUNK


def _s5_fwd(u, lamv, pw, bmat, cmat, *, name):
    t, w = u.shape
    ln, cbw, nb, nc = _s5_dims(t, w)
    sp2 = lamv.shape[2]
    sp = sp2 // 2
    bp = S5_BLOCKS_PER_STEP
    assert nb % bp == 0

    def one_block(u_ref, lam_ref, pw_ref, b_ref, c_ref, ys_ref, xp_ref, st_ref, scr, carry):
        @pl.when(pl.program_id(1) == 0)
        def _():
            carry[...] = jnp.zeros_like(carry)

        xp_ref[0] = carry[0:1, :]
        _s5_states(u_ref, b_ref, lam_ref, pw_ref, scr, carry[0:1, :sp], carry[0:1, sp:], ln, sp)
        carry[0:1, :] = scr[S5_TOP + ln - 1:S5_TOP + ln, :]
        states = scr[S5_TOP:S5_TOP + ln, :]
        st_ref[0] = states
        ys_ref[...] = _dg(states, c_ref[0], 1, 0)

    def body(u_ref, lam_ref, pw_ref, b_ref, c_ref, ys_ref, xp_ref, st_ref, scr, carry):
        for e in range(bp):
            one, cols = pl.ds(e, 1), pl.ds(e * cbw, cbw)
            one_block(u_ref.at[:, cols], lam_ref.at[one], pw_ref.at[one], b_ref.at[one], c_ref.at[one],
                      ys_ref.at[:, cols], xp_ref.at[0, one], st_ref.at[one], scr.at[e], carry.at[e])

    def blk(shape):
        return pl.BlockSpec((bp,) + shape, lambda j, c: (j, 0, 0))

    return pl.pallas_call(
        body, name=name, grid=(nb // bp, nc),
        in_specs=[pl.BlockSpec((ln, bp * cbw), lambda j, c: (c, j)), blk((1, sp2)), blk((8, sp2)),
                  blk((cbw, sp2)), blk((sp2, cbw))],
        out_specs=[pl.BlockSpec((ln, bp * cbw), lambda j, c: (c, j)),
                   pl.BlockSpec((1, bp, 1, sp2), lambda j, c: (c, j, 0, 0)),
                   pl.BlockSpec((bp, ln, sp2), lambda j, c: (j, c, 0))],
        out_shape=[jax.ShapeDtypeStruct((t, w), f32), jax.ShapeDtypeStruct((nc, nb, 1, sp2), f32),
                   jax.ShapeDtypeStruct((nb, t, sp2), f32)],
        scratch_shapes=[pltpu.VMEM((bp, S5_TOP + ln, sp2), f32), pltpu.VMEM((bp, 8, sp2), f32)],
        compiler_params=pltpu.CompilerParams(dimension_semantics=("parallel", "arbitrary")),
    )(u, lamv, pw, bmat, cmat)


def _s5_bwd(u, dy, du_skip, xp, states, lamv, pw, bmat, cmat, *, name):
    t, w = u.shape
    ln, cbw, nb, nc = _s5_dims(t, w)
    sp2 = lamv.shape[2]
    sp = sp2 // 2
    bp = S5_BLOCKS_PER_STEP
    assert nb % bp == 0

    def body(u_ref, dy_ref, dus_ref, xp_ref, st_ref, lam_ref, pw_ref, b_ref, c_ref,
             du_ref, db_ref, dc_ref, dl_ref, scr, gsc, gcarry):
        for e in range(bp):
            one, cols = pl.ds(e, 1), pl.ds(e * cbw, cbw)
            one_block(u_ref.at[:, cols], dy_ref.at[:, cols], dus_ref.at[:, cols], xp_ref.at[0, one],
                      st_ref.at[one], lam_ref.at[one], pw_ref.at[one], b_ref.at[one], c_ref.at[one],
                      du_ref.at[:, cols], db_ref.at[one], dc_ref.at[one], dl_ref.at[one],
                      scr.at[e], gsc.at[e], gcarry.at[e])

    def one_block(u_ref, dy_ref, dus_ref, xp_ref, st_ref, lam_ref, pw_ref, b_ref, c_ref,
                  du_ref, db_ref, dc_ref, dl_ref, scr, gsc, gcarry):
        @pl.when(pl.program_id(1) == 0)
        def _():
            gcarry[...] = jnp.zeros_like(gcarry)
            db_ref[...] = jnp.zeros_like(db_ref)
            dc_ref[...] = jnp.zeros_like(dc_ref)
            dl_ref[...] = jnp.zeros_like(dl_ref)

        lr, li = lam_ref[0][:, :sp], lam_ref[0][:, sp:]
        scr[S5_TOP - 1:S5_TOP, :] = xp_ref[0]
        scr[S5_TOP:S5_TOP + ln, :] = st_ref[0]
        dy = dy_ref[...]
        gsc[0:ln, :] = _dg(dy, c_ref[0], 1, 1)
        mr, mi = _cmul(gcarry[0:1, :sp], gcarry[0:1, sp:], lr, -li)
        gsc[ln - 1:ln, :sp] = gsc[ln - 1:ln, :sp] + mr
        gsc[ln - 1:ln, sp:] = gsc[ln - 1:ln, sp:] + mi
        _s5_scan(gsc, 0, ln, pw_ref, sp, reverse=True)
        gcarry[0:1, :] = gsc[0:1, :]
        g = gsc[0:ln, :]
        du_ref[...] = dus_ref[...] + _dg(g, b_ref[0], 1, 1)
        db_ref[0] += _dg(u_ref[...], g, 0, 0)
        dc_ref[0] += _dg(scr[S5_TOP:S5_TOP + ln, :], dy, 0, 0)
        pr, pi = scr[S5_TOP - 1:S5_TOP - 1 + ln, :sp], scr[S5_TOP - 1:S5_TOP - 1 + ln, sp:]
        gr, gi = g[:, :sp], g[:, sp:]
        dl_ref[0, 0:1, :sp] += jnp.sum(pr * gr + pi * gi, axis=0, keepdims=True)
        dl_ref[0, 0:1, sp:] += jnp.sum(pr * gi - pi * gr, axis=0, keepdims=True)

    def rc(c):
        return nc - 1 - c

    def blk(shape):
        return pl.BlockSpec((bp,) + shape, lambda j, c: (j, 0, 0))

    row = pl.BlockSpec((ln, bp * cbw), lambda j, c: (rc(c), j))
    return pl.pallas_call(
        body, name=name, grid=(nb // bp, nc),
        in_specs=[row, row, row, pl.BlockSpec((1, bp, 1, sp2), lambda j, c: (rc(c), j, 0, 0)),
                  pl.BlockSpec((bp, ln, sp2), lambda j, c: (j, rc(c), 0)),
                  blk((1, sp2)), blk((8, sp2)), blk((cbw, sp2)), blk((sp2, cbw))],
        out_specs=[row, blk((cbw, sp2)), blk((sp2, cbw)), blk((8, sp2))],
        out_shape=[jax.ShapeDtypeStruct((t, w), f32), jax.ShapeDtypeStruct((nb, cbw, sp2), f32),
                   jax.ShapeDtypeStruct((nb, sp2, cbw), f32), jax.ShapeDtypeStruct((nb, 8, sp2), f32)],
        scratch_shapes=[pltpu.VMEM((bp, S5_TOP + ln, sp2), f32), pltpu.VMEM((bp, ln, sp2), f32),
                        pltpu.VMEM((bp, 8, sp2), f32)],
        compiler_params=pltpu.CompilerParams(dimension_semantics=("parallel", "arbitrary")),
    )(u, dy, du_skip, xp, states, lamv, pw, bmat, cmat)


def _s5_discretize(lre, lim, log_step, bre, bim, cre, cim):
    g, p = lre.shape
    gb = S5_BLOCK_GROUPS
    nb = g // gb
    lam = lax.complex(lre, lim)
    lam_bar = jnp.exp(lam * jnp.exp(log_step)[:, None])
    b_bar = ((lam_bar - 1.0) / lam)[..., None] * lax.complex(bre, bim)
    lb = lam_bar.reshape(nb, 1, gb * p)
    lamv = jnp.concatenate([jnp.real(lb), jnp.imag(lb)], axis=-1)
    eye = jnp.eye(gb, dtype=f32)
    bb = b_bar.reshape(nb, gb, p, S5_GROUP)

    def bdiag(v):
        return jnp.einsum('jgpi,gh->jgihp', v, eye).reshape(nb, gb * S5_GROUP, gb * p)

    bmat = jnp.concatenate([bdiag(jnp.real(bb)), bdiag(jnp.imag(bb))], axis=-1)

    def cdiag(v):
        return jnp.einsum('jgip,gh->jhpgi', v, eye).reshape(nb, gb * p, gb * S5_GROUP)

    cmat = jnp.concatenate([cdiag(cre.reshape(nb, gb, S5_GROUP, p)),
                            -cdiag(cim.reshape(nb, gb, S5_GROUP, p))], axis=1)
    return lamv, bmat, cmat


def _s5_powers(lamv):
    sp = lamv.shape[2] // 2
    pr, pi = lamv[:, :, :sp], lamv[:, :, sp:]
    rows = []
    for _ in range(8):
        rows.append(jnp.concatenate([pr, pi], axis=-1))
        pr, pi = pr * pr - pi * pi, 2.0 * pr * pi
    return lax.stop_gradient(jnp.concatenate(rows, axis=1))


def _ln_fwd(h, out, p, tag):
    return _rw_fwd(_ln_fn, [h, out], [p["ln_g"], p["ln_b"]], [h.shape[1]], name=tag + "_ln")[0]


def _ln_bwd(h, out, p, dh2, tag):
    return _rw_bwd(_ln_fn, [h, out], [p["ln_g"], p["ln_b"]], [dh2], name=tag + "_ln_bwd")


def _ssd_layer_fwd(h, p, wout_of, side=None):
    di, groups = p["wz"].shape[1], p["dtb"].shape[0]
    z = _mm(h, p["wz"], name="ssd_z")
    xr = _mm(h, p["wxbc"], name="ssd_xbc")
    dtp = _mm(h, p["wdt"], name="ssd_dt")
    xbc = _conv_fwd(xr, p["conv_w"], p["conv_b"], name="ssd_conv")
    (y, states), got = _ssd_fwd(xbc, dtp, p["dtb"], p["alog"], p["dsk"], d_inner=di, groups=groups,
                                name="ssd_scan", side=side)
    p["wout"] = wout_of(got)
    yn = _rw_fwd(_make_ssd_post_fn(groups), [y, z], [p["norm_w"]], [di], name="ssd_post")[0]
    out = _mm(yn, p["wout"], name="ssd_out")
    return _ln_fwd(h, out, p, "ssd"), (h, z, xr, dtp, xbc, states, y, yn, out), got


def _ssd_layer_bwd(saved, p, dh2, side=None, in_sides=None):
    h, z, xr, dtp, xbc, states, y, yn, out = saved
    di, groups = p["wz"].shape[1], p["dtb"].shape[0]
    dh, dout, dg, db = _ln_bwd(h, out, p, dh2, "ssd")
    dyn = _mm(dout, p["wout"], tb=True, name="ssd_out_dx")
    g = {"ln_g": dg, "ln_b": db, "wout": _mm(yn, dout, ta=True, out_dtype=bf16, name="ssd_out_dw")}
    if callable(side):
        side = side(g["wout"])
    dy, dz, g["norm_w"] = _rw_bwd(_make_ssd_post_fn(groups), [y, z], [p["norm_w"]], [dyn], name="ssd_post_bwd")
    (dx, dbm, dcm, ddt, dpar), got = _ssd_bwd(xbc, dtp, p["dtb"], p["alog"], p["dsk"], states, dy,
                                              d_inner=di, groups=groups, name="ssd_scan_bwd", side=side)
    g["dtb"], g["alog"], g["dsk"] = dpar[:, 0:1, :], dpar[:, 1:2, :], dpar[:, 2:3, :]
    dxr, g["conv_w"], g["conv_b"] = _conv_bwd(xr, p["conv_w"], p["conv_b"],
                                               jnp.concatenate([dx, dbm, dcm], axis=1), name="ssd_conv_bwd")
    g["wz"] = _mm(h, dz, ta=True, name="ssd_z_dw")
    g["wxbc"] = _mm(h, dxr, ta=True, name="ssd_xbc_dw")
    g["wdt"] = _mm(h, ddt, ta=True, name="ssd_dt_dw")
    if in_sides is None:
        dh = _mm(dxr, p["wxbc"], tb=True, add=dh, name="ssd_xbc_dx")
        dh = _mm(dz, p["wz"], tb=True, add=dh, name="ssd_z_dx")
        got_in = (None, None)
    else:
        side_a, side_b = in_sides(g)
        dh, got_a = _mm(dxr, p["wxbc"], tb=True, add=dh, name="ssd_xbc_dx", side=side_a)
        dh, got_b = _mm(dz, p["wz"], tb=True, add=dh, name="ssd_z_dx", side=side_b)
        got_in = (got_a, got_b)
    dh = _mm(ddt, p["wdt"], tb=True, add=dh, name="ssd_dt_dx")
    return dh, g, got, got_in


def _fox_layer_fwd(h, p, side=None):
    w = p["wout"].shape[0]
    heads = w // FOX_HEAD_DIM
    t = h.shape[0]
    qkvz = _mm(h, p["wqkvz"], name="fox_qkvz")
    fr = _mm(h, p["wf"], name="fox_f")
    logf = _rw_fwd(_logf_fn, [fr], [p["fb"]], [LANE], name="fox_logf")[0]
    cum = _cumsum_rows(logf, reverse=False, name="fox_cumsum")
    ck = _lane_bcast(cum[:, :heads].T)
    (o, lse), got = _attn_fwd(qkvz, 0, qkvz, heads, qkvz, 2 * heads, None, None, ck,
                              heads=heads, scale=FOX_HEAD_DIM ** -0.5, name="fox_attn", side=side)
    yg = _rw_fwd(_gate_fn, [o, (qkvz, w, 3)], [], [w], name="fox_gate")[0]
    out = _mm(yg, p["wout"], name="fox_out")
    return _ln_fwd(h, out, p, "fox"), (h, qkvz, fr, ck, o, lse, yg, out), got


def _fox_layer_bwd(saved, p, dh2, side=None):
    h, qkvz, fr, ck, o, lse, yg, out = saved
    w = p["wout"].shape[0]
    heads = w // FOX_HEAD_DIM
    t = h.shape[0]
    dh, dout, dg, db = _ln_bwd(h, out, p, dh2, "fox")
    dyg = _mm(dout, p["wout"], tb=True, name="fox_out_dx")
    g = {"ln_g": dg, "ln_b": db, "wout": _mm(yg, dout, ta=True, out_dtype=bf16, name="fox_out_dw")}
    do, dz = _rw_bwd(_gate_fn, [o, (qkvz, w, 3)], [], [dyg], name="fox_gate_bwd")
    delta = _attn_delta(do, o, heads=heads, name="fox_delta")
    (dq, dk, dv, drow, dck), got = _attn_bwd(qkvz, 0, qkvz, heads, qkvz, 2 * heads, None, None, ck, do, lse, delta,
                                             heads=heads, scale=FOX_HEAD_DIM ** -0.5, name="fox_attn_bwd", side=side)
    dqkvz = jnp.concatenate([dq, dk, dv, dz], axis=1)
    dcum = jnp.pad((drow.reshape(heads, t) + dck.reshape(heads, t)).T, ((0, 0), (0, LANE - heads)))
    dlogf = _cumsum_rows(dcum, reverse=True, name="fox_cumsum_bwd")
    dfr, g["fb"] = _rw_bwd(_logf_fn, [fr], [p["fb"]], [dlogf], name="fox_logf_bwd")
    dh = _mm(dqkvz, p["wqkvz"], tb=True, add=dh, name="fox_qkvz_dx")
    dh = _mm(dfr, p["wf"], tb=True, add=dh, name="fox_f_dx")
    g["wqkvz"] = _mm(h, dqkvz, ta=True, name="fox_qkvz_dw")
    g["wf"] = _mm(h, dfr, ta=True, name="fox_f_dw")
    return dh, g, got


def _mla_layer_fwd(h, p, cos, sin):
    rank = p["qn"].shape[1]
    w = p["wout"].shape[0]
    heads = w // MLA_V
    lat = _mm(h, p["wlat"], name="mla_lat")
    kpr = _mm(h, p["wkpe"], name="mla_kpe")
    z = _mm(h, p["wz"], name="mla_z")
    qnl, kvnl = _rw_fwd(_make_mla_norm_fn(rank), [lat], [p["qn"], p["kvn"]], [rank, rank], name="mla_norm")
    qno = _mm(qnl, p["wqn"], name="mla_qn")
    qpr = _mm(qnl, p["wqp"], name="mla_qp")
    kno = _mm(kvnl, p["wkn"], name="mla_kn")
    v = _mm(kvnl, p["wv"], name="mla_v")
    qp = _rw_fwd(_make_rope_fn(heads), [qpr, cos, sin], [], [heads * LANE], name="mla_rope_q")[0]
    kp = _rw_fwd(_make_rope_fn(1), [kpr, cos, sin], [], [LANE], name="mla_rope_k")[0]
    scale = (MLA_NOPE + MLA_ROPE) ** -0.5
    (o, lse), _ = _attn_fwd(qno, 0, kno, 0, v, 0, qp, kp, None, heads=heads, scale=scale, name="mla_attn")
    yg = _rw_fwd(_gate_fn, [o, z], [], [w], name="mla_gate")[0]
    out = _mm(yg, p["wout"], name="mla_out")
    return _ln_fwd(h, out, p, "mla"), (h, lat, kpr, z, qnl, kvnl, qno, qpr, kno, v, qp, kp, o, lse, yg, out)


def _mla_layer_bwd(saved, p, dh2, cos, sin, side=None):
    h, lat, kpr, z, qnl, kvnl, qno, qpr, kno, v, qp, kp, o, lse, yg, out = saved
    rank = p["qn"].shape[1]
    w = p["wout"].shape[0]
    heads = w // MLA_V
    dh, dout, dg, db = _ln_bwd(h, out, p, dh2, "mla")
    dyg = _mm(dout, p["wout"], tb=True, name="mla_out_dx")
    g = {"ln_g": dg, "ln_b": db, "wout": _mm(yg, dout, ta=True, out_dtype=bf16, name="mla_out_dw")}
    do, dz = _rw_bwd(_gate_fn, [o, z], [], [dyg], name="mla_gate_bwd")
    delta = _attn_delta(do, o, heads=heads, name="mla_delta")
    (dqno, dkno, dv, dqp, dkp), got = _attn_bwd(qno, 0, kno, 0, v, 0, qp, kp, None, do, lse, delta, heads=heads,
                                                scale=(MLA_NOPE + MLA_ROPE) ** -0.5, name="mla_attn_bwd", side=side)
    (dqpr,) = _rw_bwd(_make_rope_fn(heads), [qpr, cos, sin], [], [dqp], n_const=2, name="mla_rope_q_bwd")
    (dkpr,) = _rw_bwd(_make_rope_fn(1), [kpr, cos, sin], [], [dkp], n_const=2, name="mla_rope_k_bwd")
    dqnl = _mm(dqno, p["wqn"], tb=True, name="mla_qn_dx")
    dqnl = _mm(dqpr, p["wqp"], tb=True, add=dqnl, name="mla_qp_dx")
    dkvnl = _mm(dkno, p["wkn"], tb=True, name="mla_kn_dx")
    dkvnl = _mm(dv, p["wv"], tb=True, add=dkvnl, name="mla_v_dx")
    g["wqn"] = _mm(qnl, dqno, ta=True, name="mla_qn_dw")
    g["wqp"] = _mm(qnl, dqpr, ta=True, name="mla_qp_dw")
    g["wkn"] = _mm(kvnl, dkno, ta=True, name="mla_kn_dw")
    g["wv"] = _mm(kvnl, dv, ta=True, name="mla_v_dw")
    dlat, g["qn"], g["kvn"] = _rw_bwd(_make_mla_norm_fn(rank), [lat], [p["qn"], p["kvn"]], [dqnl, dkvnl],
                                     name="mla_norm_bwd")
    dh = _mm(dlat, p["wlat"], tb=True, add=dh, name="mla_lat_dx")
    dh = _mm(dkpr, p["wkpe"], tb=True, add=dh, name="mla_kpe_dx")
    dh = _mm(dz, p["wz"], tb=True, add=dh, name="mla_z_dx")
    g["wlat"] = _mm(h, dlat, ta=True, name="mla_lat_dw")
    g["wkpe"] = _mm(h, dkpr, ta=True, name="mla_kpe_dw")
    g["wz"] = _mm(h, dz, ta=True, name="mla_z_dw")
    return dh, g, got


def _s5_layer_fwd(h, p):
    w = p["wout"].shape[0]
    u = _mm(h, p["wu"], name="s5_u")
    z = _mm(h, p["wz"], name="s5_z")
    ys, xp, states = _s5_fwd(u, p["lamv"], p["pw"], p["bmat"], p["cmat"], name="s5_scan")
    y1 = _rw_fwd(_s5_act_fn, [ys, u], [p["d"]], [w], name="s5_act")[0]
    gp = _mm(y1, p["wglu"], name="s5_glu")
    y2 = _rw_fwd(_s5_glu_fn, [y1, gp, z], [p["bglu"]], [w], name="s5_gate")[0]
    out = _mm(y2, p["wout"], name="s5_out")
    return _ln_fwd(h, out, p, "s5"), (h, u, z, ys, xp, states, y1, gp, y2, out)


def _s5_layer_bwd(saved, p, dh2):
    h, u, z, ys, xp, states, y1, gp, y2, out = saved
    dh, dout, dg, db = _ln_bwd(h, out, p, dh2, "s5")
    dy2 = _mm(dout, p["wout"], tb=True, name="s5_out_dx")
    g = {"ln_g": dg, "ln_b": db, "wout": _mm(y2, dout, ta=True, out_dtype=bf16, name="s5_out_dw")}
    dy1, dgp, dz, g["bglu"] = _rw_bwd(_s5_glu_fn, [y1, gp, z], [p["bglu"]], [dy2], name="s5_gate_bwd")
    dy1 = _mm(dgp, p["wglu"], tb=True, add=dy1, name="s5_glu_dx")
    g["wglu"] = _mm(y1, dgp, ta=True, out_dtype=bf16, name="s5_glu_dw")
    dys, du, g["d"] = _rw_bwd(_s5_act_fn, [ys, u], [p["d"]], [dy1], name="s5_act_bwd")
    du, g["bmat"], g["cmat"], dlam = _s5_bwd(u, dys, du, xp, states, p["lamv"], p["pw"], p["bmat"], p["cmat"],
                                             name="s5_scan_bwd")
    g["lamv"] = dlam[:, 0:1, :]
    dh = _mm(du, p["wu"], tb=True, add=dh, name="s5_u_dx")
    dh = _mm(dz, p["wz"], tb=True, add=dh, name="s5_z_dx")
    g["wu"] = _mm(h, du, ta=True, name="s5_u_dw")
    g["wz"] = _mm(h, dz, ta=True, name="s5_z_dw")
    return dh, g


def _pad_last(a, to):
    return jnp.pad(a, [(0, 0)] * (a.ndim - 1) + [(0, to - a.shape[-1])])


def _row(v):
    return v.reshape(1, -1)


def _prep_ssd(w):
    di, cd, hh = w["ssd_norm_w"].shape[0], w["ssd_conv_b"].shape[0], w["ssd_dt_bias"].shape[0]
    groups = (cd - di) // (2 * SSD_STATE)
    e = hh // groups

    def perm(v):
        lead = v.shape[:-1]
        return _pad_last(v.reshape(lead + (groups, e)), LANE).reshape(lead + (groups * LANE,))

    w_in = w["ssd_w_in"]
    return dict(
        wz=w_in[:, :di], wxbc=w_in[:, di:di + cd], wdt=perm(w_in[:, di + cd:]),
        conv_w=w["ssd_conv_w"], conv_b=_row(w["ssd_conv_b"]),
        dtb=perm(w["ssd_dt_bias"]).reshape(groups, 1, LANE), alog=perm(w["ssd_a_log"]).reshape(groups, 1, LANE),
        dsk=perm(w["ssd_d"]).reshape(groups, 1, LANE), norm_w=_row(w["ssd_norm_w"]),
        ln_g=_row(w["ln_g"][0]), ln_b=_row(w["ln_b"][0]))


def _prep_fox(w):
    wd = w["fox_w_out"].shape[0]
    w_in = w["fox_w_in"]
    return dict(wqkvz=w_in[:, :4 * wd], wf=_pad_last(w_in[:, 4 * wd:], LANE), wout=w["fox_w_out"],
                fb=_pad_last(_row(w["fox_f_bias"]), LANE), ln_g=_row(w["ln_g"][1]), ln_b=_row(w["ln_b"][1]))


def _prep_mla(w):
    rank = w["mla_q_norm"].shape[0]
    wd = w["mla_w_out"].shape[0]
    heads = wd // MLA_V
    w_in = w["mla_w_in"]
    qu = w["mla_w_q_up"].reshape(rank, heads, MLA_NOPE + MLA_ROPE)
    kvu = w["mla_w_kv_up"].reshape(w["mla_w_kv_up"].shape[0], heads, MLA_NOPE + MLA_V)
    return dict(
        wlat=w_in[:, :2 * rank], wkpe=_pad_last(w_in[:, 2 * rank:2 * rank + MLA_ROPE], LANE),
        wz=w_in[:, 2 * rank + MLA_ROPE:],
        wqn=qu[:, :, :MLA_NOPE].reshape(rank, heads * LANE),
        wqp=_pad_last(qu[:, :, MLA_NOPE:], LANE).reshape(rank, heads * LANE),
        wkn=kvu[:, :, :MLA_NOPE].reshape(-1, heads * LANE), wv=kvu[:, :, MLA_NOPE:].reshape(-1, heads * LANE),
        wout=w["mla_w_out"], qn=_row(w["mla_q_norm"]), kvn=_row(w["mla_kv_norm"]),
        ln_g=_row(w["ln_g"][2]), ln_b=_row(w["ln_b"][2]))


def _prep_s5(w):
    wd = w["s5_w_out"].shape[0]
    w_in = w["s5_w_in"]
    lamv, bmat, cmat = _s5_discretize(w["s5_lambda_re"], w["s5_lambda_im"], w["s5_log_step"],
                                      w["s5_b_re"], w["s5_b_im"], w["s5_c_re"], w["s5_c_im"])
    return dict(wu=w_in[:, :wd], wz=w_in[:, wd:], wglu=w["s5_w_glu"], wout=w["s5_w_out"],
                d=_row(w["s5_d"]), bglu=_row(w["s5_b_glu"]), lamv=lamv, bmat=bmat, cmat=cmat,
                ln_g=_row(w["ln_g"][3]), ln_b=_row(w["ln_b"][3]))


def _rope_tables(positions):
    inv_freq = ROPE_BASE ** (-jnp.arange(0, MLA_ROPE, 2, dtype=f32) / MLA_ROPE)
    ang = positions.astype(f32).reshape(-1, 1) * inv_freq
    return jnp.cos(ang), jnp.sin(ang)


def _heads3(v):
    return v.reshape(v.shape[0], -1, LANE)


def _flat2(v):
    return v.reshape(v.shape[0], -1)


def _natural_grads_ssd(g, e_heads):
    return {"ssd_w_in": jnp.concatenate([g["wz"], g["wxbc"], _flat2(_heads3(g["wdt"])[:, :, :e_heads])], axis=1),
            "ssd_w_out": g["wout"]}


def _natural_grads_fox(g, heads):
    return {"fox_w_in": jnp.concatenate([g["wqkvz"], g["wf"][:, :heads]], axis=1), "fox_w_out": g["wout"]}


def _natural_grads_mla(g):
    return {"mla_w_in": jnp.concatenate([g["wlat"], g["wkpe"][:, :MLA_ROPE], g["wz"]], axis=1),
            "mla_w_q_up": _flat2(jnp.concatenate([_heads3(g["wqn"]), _heads3(g["wqp"])[:, :, :MLA_ROPE]], axis=2)),
            "mla_w_kv_up": _flat2(jnp.concatenate([_heads3(g["wkn"]), _heads3(g["wv"])], axis=2)),
            "mla_w_out": g["wout"]}


def _natural_grads_s5(g):
    return {"s5_w_in": jnp.concatenate([g["wu"], g["wz"]], axis=1), "s5_w_glu": g["wglu"], "s5_w_out": g["wout"]}


def _sequence_step(x, positions, tgt, small, big_of_layer, gather_sides=(None, None), exchange_side=None,
                   ssd_in_sides=None, ssd_wout_of=None):
    cos, sin = _rope_tables(positions)
    exchange_side = exchange_side or (lambda nat: None)
    vjps = []

    def prepared(prep, big):
        p, vjp = jax.vjp(lambda s: prep({**big, **s}), small)
        vjps.append((vjp, tuple(p)))
        return p

    big0 = big_of_layer[0](None)
    p0 = prepared(_prep_ssd, big0)
    wout_of = (lambda got: big0["ssd_w_out"]) if "ssd_w_out" in big0 else ssd_wout_of
    h1, s0, got1 = _ssd_layer_fwd(x, p0, wout_of, side=gather_sides[0])
    p1 = prepared(_prep_fox, big_of_layer[1](got1))
    h2, s1, got23 = _fox_layer_fwd(h1, p1, side=gather_sides[1])
    p2 = prepared(_prep_mla, big_of_layer[2](got23))
    p3 = prepared(_prep_s5, big_of_layer[3](got23))
    p3["pw"] = _s5_powers(p3["lamv"])
    h3, s2 = _mla_layer_fwd(h2, p2, cos, sin)
    h4, s3 = _s5_layer_fwd(h3, p3)
    dh, loss = _loss_head(h4, tgt)
    dh, g3 = _s5_layer_bwd(s3, p3, dh)
    nat3 = _natural_grads_s5(g3)
    dh, g2, parts3 = _mla_layer_bwd(s2, p2, dh, cos, sin, side=exchange_side(nat3))
    nat2 = _natural_grads_mla(g2)
    dh, g1, parts2 = _fox_layer_bwd(s1, p1, dh, side=exchange_side(nat2))
    nat1 = _natural_grads_fox(g1, small["fox_f_bias"].shape[0])
    e_heads = small["ssd_dt_bias"].shape[0] // p0["dtb"].shape[0]
    dh, g0, parts1, parts_in = _ssd_layer_bwd(
        s0, p0, dh, side=lambda gw: exchange_side({**nat1, "ssd_w_out": gw}),
        in_sides=None if ssd_in_sides is None else
        lambda g: ssd_in_sides(_natural_grads_ssd(g, e_heads)["ssd_w_in"]))
    nat0 = _natural_grads_ssd(g0, e_heads)
    g_small = None
    for p, (vjp, keys), g in zip((p0, p1, p2, p3), vjps, (g0, g1, g2, g3)):
        (gi,) = vjp({k: g[k].astype(p[k].dtype) for k in keys})
        g_small = gi if g_small is None else jax.tree.map(jnp.add, g_small, gi)
    return loss, dh, g_small, (nat0, nat1, nat2, nat3), (parts1, parts2, parts3, parts_in)


FLAT_COLS = 1024
FLAT_ROW_ALIGN = 128


def _pack(arrs):
    flat = jnp.concatenate([a.reshape(-1) for a in arrs])
    unit = FLAT_COLS * FLAT_ROW_ALIGN
    return jnp.pad(flat, (0, -flat.shape[0] % unit)).reshape(-1, FLAT_COLS)


def _unpack(flat2d, shapes):
    flat = flat2d.reshape(-1)
    out, off = [], 0
    for s in shapes:
        sz = math.prod(s)
        out.append(flat[off:off + sz].reshape(s))
        off += sz
    return out


def _unpack_gathered(gathered, shapes, axes):
    flat = gathered.reshape(N_DEV, -1)
    out, off = [], 0
    for s, ax in zip(shapes, axes):
        sz = math.prod(s)
        seg = flat[:, off:off + sz].reshape((N_DEV,) + tuple(s))
        out.append(jnp.concatenate([seg[d] for d in range(N_DEV)], axis=ax)[0])
        off += sz
    return out


MESH = pl.DeviceIdType.MESH
HBM_SPEC = pl.BlockSpec(memory_space=pl.ANY)


def _comm_scratch(n):
    return [pltpu.SemaphoreType.DMA((n, 7)), pltpu.SemaphoreType.DMA((n, 7)), pltpu.SemaphoreType.DMA((n,))]


class _Gather:
    def __init__(self, xs):
        self.ins = list(xs)
        self.out_shape = [jax.ShapeDtypeStruct((N_DEV,) + v.shape, v.dtype) for v in xs]

    def _plan(self, x_refs, out_refs, sems):
        send_sems, recv_sems, local_sems = sems
        x, y, cc = lax.axis_index("x"), lax.axis_index("y"), lax.axis_index("c")
        me, sibling = (x, y, cc), (x, y, 1 - cc)
        chips = [(1 - x, y), (x, 1 - y), (1 - x, 1 - y)]

        def rows(w, px, py, pc):
            return out_refs[w].at[4 * px + 2 * py + pc]

        def copy(w, k, block, to, src=None):
            return pltpu.make_async_remote_copy(
                src_ref=rows(w, *block) if src is None else src, dst_ref=rows(w, *block),
                send_sem=send_sems.at[w, k], recv_sem=recv_sems.at[w, k], device_id=to, device_id_type=MESH)

        n = len(x_refs)
        mine = [pltpu.make_async_copy(x_refs[w], rows(w, *me), local_sems.at[w]) for w in range(n)]
        first = []
        for w in range(n):
            first.append(copy(w, 0, me, sibling, src=x_refs[w]))
            first += [copy(w, 1 + j, me, (*chip, cc), src=x_refs[w]) for j, chip in enumerate(chips)]
        return n, me, sibling, chips, cc, copy, mine, first

    def start(self, x_refs, out_refs, sems):
        *_, mine, first = self._plan(x_refs, out_refs, sems)
        for cp in mine + first:
            cp.start()

    def finish(self, x_refs, out_refs, sems):
        n, me, sibling, chips, cc, copy, mine, first = self._plan(x_refs, out_refs, sems)
        passed = []
        for w in range(n):
            for j, chip in enumerate(chips):
                copy(w, 1 + j, (*chip, cc), me).wait_recv()
                cp = copy(w, 4 + j, (*chip, cc), sibling)
                cp.start()
                passed.append(cp)
        for w in range(n):
            copy(w, 0, sibling, me).wait_recv()
            for j, chip in enumerate(chips):
                copy(w, 4 + j, (*chip, 1 - cc), me).wait_recv()
        for cp in first + passed:
            cp.wait_send()
        for cp in mine:
            cp.wait()


class _Exchange:
    def __init__(self, gs):
        self.ins = list(gs)
        self.out_shape = [jax.ShapeDtypeStruct(v.shape, v.dtype) for v in gs]

    def _plan(self, g_refs, out_refs, sems):
        send_sems, recv_sems, local_sems = sems
        x, y, cc = lax.axis_index("x"), lax.axis_index("y"), lax.axis_index("c")
        me = 4 * x + 2 * y + cc
        peers = []
        for k in range(1, N_DEV):
            px = 1 - x if (k >> 2) & 1 else x
            py = 1 - y if (k >> 1) & 1 else y
            pc = 1 - cc if k & 1 else cc
            peers.append((k, (px, py, pc), 4 * px + 2 * py + pc))

        def copy(w, k, peer, slot_src, slot_dst):
            return pltpu.make_async_remote_copy(
                src_ref=g_refs[w].at[slot_src], dst_ref=out_refs[w].at[slot_dst],
                send_sem=send_sems.at[w, k - 1], recv_sem=recv_sems.at[w, k - 1], device_id=peer, device_id_type=MESH)

        n = len(g_refs)
        mine = [pltpu.make_async_copy(g_refs[w].at[me], out_refs[w].at[me], local_sems.at[w]) for w in range(n)]
        sends = [copy(w, k, peer, idx, me) for w in range(n) for k, peer, idx in peers]
        return n, peers, copy, mine, sends

    def start(self, g_refs, out_refs, sems):
        *_, mine, sends = self._plan(g_refs, out_refs, sems)
        for cp in mine + sends:
            cp.start()

    def finish(self, g_refs, out_refs, sems):
        n, peers, copy, mine, sends = self._plan(g_refs, out_refs, sems)
        for w in range(n):
            for k, peer, idx in peers:
                copy(w, k, peer, idx, idx).wait_recv()
        for cp in sends:
            cp.wait_send()
        for cp in mine:
            cp.wait()


class _Both:
    def __init__(self, first, second):
        self.parts = (first, second)
        self.ins = first.ins + second.ins
        self.out_shape = first.out_shape + second.out_shape

    def _split(self, ins, outs, sems):
        n0 = len(self.parts[0].ins)
        n1 = len(self.parts[1].ins)
        for side, lo, n in ((self.parts[0], 0, n0), (self.parts[1], n0, n1)):
            yield side, ins[lo:lo + n], outs[lo:lo + n], tuple(s.at[pl.ds(lo, n)] for s in sems)

    def start(self, ins, outs, sems):
        for side, i, o, s in self._split(ins, outs, sems):
            side.start(i, o, s)

    def finish(self, ins, outs, sems):
        for side, i, o, s in self._split(ins, outs, sems):
            side.finish(i, o, s)


def _run_comm(side, *, name):
    n = len(side.ins)

    def body(*refs):
        ins, outs, sems = refs[:n], refs[n:2 * n], refs[2 * n:]
        side.start(ins, outs, sems)
        side.finish(ins, outs, sems)

    return pl.pallas_call(
        body, name=name, out_shape=side.out_shape,
        in_specs=[HBM_SPEC] * n, out_specs=[HBM_SPEC] * n, scratch_shapes=_comm_scratch(n),
    )(*side.ins)


def _hosted_call(body, side, *, name, grid, in_specs, out_specs, out_shape, scratch_shapes, args):
    out_specs, out_shape = list(out_specs), list(out_shape)
    if side is None:
        res = pl.pallas_call(
            body, name=name, grid=grid, in_specs=in_specs, out_specs=out_specs, out_shape=out_shape,
            scratch_shapes=scratch_shapes,
            compiler_params=pltpu.CompilerParams(dimension_semantics=("arbitrary",) * len(grid)))(*args)
        return res, None
    n_in, n_out, n_scr, ns = len(in_specs), len(out_specs), len(scratch_shapes), len(side.ins)

    def wrapped(*refs):
        ins, s_ins = refs[:n_in], refs[n_in:n_in + ns]
        outs = refs[n_in + ns:n_in + ns + n_out]
        s_outs = refs[n_in + ns + n_out:n_in + 2 * ns + n_out]
        scr = refs[n_in + 2 * ns + n_out:n_in + 2 * ns + n_out + n_scr]
        sems = refs[n_in + 2 * ns + n_out + n_scr:]
        ids = [pl.program_id(ax) for ax in range(len(grid))]
        first = functools.reduce(lambda p, q: p & q, [i == 0 for i in ids])
        last = functools.reduce(lambda p, q: p & q, [i == g - 1 for i, g in zip(ids, grid)])

        @pl.when(first)
        def _():
            side.start(s_ins, s_outs, sems)

        body(*ins, *outs, *scr)

        @pl.when(last)
        def _():
            side.finish(s_ins, s_outs, sems)

    res = pl.pallas_call(
        wrapped, name=name, grid=grid, in_specs=list(in_specs) + [HBM_SPEC] * ns,
        out_specs=out_specs + [HBM_SPEC] * ns, out_shape=out_shape + side.out_shape,
        scratch_shapes=list(scratch_shapes) + _comm_scratch(ns),
        compiler_params=pltpu.CompilerParams(dimension_semantics=("arbitrary",) * len(grid)))(*args, *side.ins)
    return res[:n_out], res[n_out:]


BIG = (("ssd_w_in", 2), ("ssd_w_out", 1), ("fox_w_in", 2), ("fox_w_out", 1), ("mla_w_in", 2),
       ("mla_w_q_up", 2), ("mla_w_kv_up", 2), ("mla_w_out", 1), ("s5_w_in", 2), ("s5_w_glu", 1), ("s5_w_out", 1))
LAYER_BIG = (("ssd_w_in", "ssd_w_out"), ("fox_w_in", "fox_w_out"),
             ("mla_w_in", "mla_w_q_up", "mla_w_kv_up", "mla_w_out"), ("s5_w_in", "s5_w_glu", "s5_w_out"))
SMALL = (("ssd_conv_w", 2), ("mla_q_norm", 1), ("mla_kv_norm", 1), ("s5_d", 1), ("s5_b_glu", 1))
REPLICATED = ("ln_g", "ln_b", "ssd_conv_b", "ssd_dt_bias", "ssd_a_log", "ssd_d", "ssd_norm_w", "fox_f_bias",
              "s5_lambda_re", "s5_lambda_im", "s5_log_step", "s5_b_re", "s5_b_im", "s5_c_re", "s5_c_im")
WEIGHT_ORDER = ("ln_g", "ln_b", "ssd_w_in", "ssd_conv_w", "ssd_conv_b", "ssd_dt_bias", "ssd_a_log", "ssd_d",
                "ssd_norm_w", "ssd_w_out", "fox_w_in", "fox_f_bias", "fox_w_out", "mla_w_in", "mla_q_norm",
                "mla_kv_norm", "mla_w_q_up", "mla_w_kv_up", "mla_w_out", "s5_w_in", "s5_lambda_re", "s5_lambda_im",
                "s5_log_step", "s5_b_re", "s5_b_im", "s5_c_re", "s5_c_im", "s5_d", "s5_w_glu", "s5_b_glu", "s5_w_out")


def _sum_adamw(parts, w, m, v, *, name):
    r, c = w.shape
    row_bytes = 2 * c * (N_DEV * parts.dtype.itemsize + 7 * 4)
    tr = _pick(r, [t for t in (256, 128, 64, 32, 16) if t * row_bytes <= VMEM_BLOCK_BUDGET])

    def body(p_ref, w_ref, m_ref, v_ref, g_ref, d_ref, m2_ref, v2_ref):
        gv = p_ref[0].astype(f32)
        for s in range(1, N_DEV):
            gv = gv + p_ref[s].astype(f32)
        g_ref[...] = gv
        m2 = ADAM_B1 * m_ref[...] + (1.0 - ADAM_B1) * gv
        v2 = ADAM_B2 * v_ref[...] + (1.0 - ADAM_B2) * jnp.square(gv)
        m_hat = m2 / (1.0 - ADAM_B1 ** ADAM_STEP)
        v_hat = v2 / (1.0 - ADAM_B2 ** ADAM_STEP)
        d_ref[...] = -ADAM_LR * (m_hat / (jnp.sqrt(v_hat) + ADAM_EPS) + ADAM_WD * w_ref[...])
        m2_ref[...] = m2
        v2_ref[...] = v2

    spec = pl.BlockSpec((tr, c), lambda i: (i, 0))
    shp = jax.ShapeDtypeStruct((r, c), f32)
    return pl.pallas_call(
        body, name=name, grid=(r // tr,),
        in_specs=[pl.BlockSpec((N_DEV, tr, c), lambda i: (0, i, 0))] + [spec] * 3,
        out_specs=[spec] * 4, out_shape=[shp] * 4,
        compiler_params=pltpu.CompilerParams(dimension_semantics=("parallel",)),
    )(parts, w, m, v)


def _train_step(a):
    small_names, small_axes = [n for n, _ in SMALL], [ax for _, ax in SMALL]
    small_shapes = [a[n].shape for n in small_names]
    rep_shapes = [a[n].shape for n in REPLICATED]

    big_axis = dict(BIG)

    def shards(names):
        return [a[n][0].astype(bf16) for n in names]

    def natural(names, gathered):
        w = {}
        for n, g in zip(names, gathered):
            if big_axis[n] == 1:
                w[n] = g.reshape(-1, g.shape[2])
            else:
                w[n] = jnp.transpose(g, (1, 0, 2)).reshape(g.shape[1], -1)
        return w

    def to_owner(nat):
        blocks = []
        for n, g in nat.items():
            if big_axis[n] == 1:
                g = g.reshape(N_DEV, -1, g.shape[1])
            else:
                g = jnp.transpose(g.reshape(g.shape[0], N_DEV, -1), (1, 0, 2))
            blocks.append(g.astype(bf16))
        return blocks

    small_flat = _pack([a[n] for n in small_names])
    got0 = _run_comm(_Gather(shards(("ssd_w_in",)) + [small_flat]), name="gather_ssd_weights")
    small = dict(zip(small_names, _unpack_gathered(got0[-1], small_shapes, small_axes)))
    for n in REPLICATED:
        small[n] = a[n] if n in ("ln_g", "ln_b") else a[n][0]
    rest_names = LAYER_BIG[2] + LAYER_BIG[3]
    under_scan = LAYER_BIG[1] + ("ssd_w_out",)
    big_of_layer = (lambda got: natural(("ssd_w_in",), got0[:1]),
                    lambda got: natural(LAYER_BIG[1], got[:len(LAYER_BIG[1])]),
                    lambda got: natural(LAYER_BIG[2], got[:len(LAYER_BIG[2])]),
                    lambda got: natural(LAYER_BIG[3], got[len(LAYER_BIG[2]):]))
    def ssd_in_sides(nat_w_in):
        (blocks,) = to_owner({"ssd_w_in": nat_w_in})
        cut = blocks.shape[1] * 5 // 8
        return _Exchange([blocks[:, :cut]]), _Exchange([blocks[:, cut:]])

    loss, grad_x, g_small, nats, hosted_parts = _sequence_step(
        a["x"][0], a["positions"][0], a["loss_target"][0], small, big_of_layer,
        gather_sides=(_Gather(shards(under_scan)), _Gather(shards(rest_names))),
        exchange_side=lambda nat: _Exchange(to_owner(nat)), ssd_in_sides=ssd_in_sides,
        ssd_wout_of=lambda got: natural(("ssd_w_out",), got[-1:])["ssd_w_out"])
    parts_in = hosted_parts[3]
    parts_of = {"ssd_w_in": jnp.concatenate([parts_in[0][0], parts_in[1][0]], axis=1)}
    for names, parts in zip((LAYER_BIG[1] + ("ssd_w_out",), LAYER_BIG[2], LAYER_BIG[3]), hosted_parts[:3]):
        parts_of.update(zip(names, parts, strict=True))
    small_to_owner = jax.linear_transpose(
        lambda s: _unpack_gathered(s, small_shapes, small_axes), jax.ShapeDtypeStruct(got0[-1].shape, f32))
    (g_small_blocks,) = small_to_owner([g_small[n] for n in small_names])
    g_rep_local = _pack([g_small[n] if n in ("ln_g", "ln_b") else g_small[n][None] for n in REPLICATED])
    small_parts, g_rep_all = _run_comm(_Both(_Exchange([g_small_blocks]), _Gather([g_rep_local])),
                                       name="exchange_small_gradients")

    out = {"grad": {}, "delta": {}, "new_m": {}, "new_v": {}}
    kinds = ("grad", "delta", "new_m", "new_v")
    for n, _ in BIG:
        res = _sum_adamw(parts_of[n], a[n][0], a["m_" + n][0], a["v_" + n][0], name="adamw_" + n)
        for kind, r in zip(kinds, res):
            out[kind][n] = r[None]
    upd_small = _sum_adamw(small_parts, small_flat, _pack([a["m_" + n] for n in small_names]),
                           _pack([a["v_" + n] for n in small_names]), name="adamw_small_sharded")
    for kind, r in zip(kinds, upd_small):
        out[kind].update(zip(small_names, _unpack(r, small_shapes)))
    upd_rep = _sum_adamw(g_rep_all, _pack([a[n] for n in REPLICATED]), _pack([a["m_" + n] for n in REPLICATED]),
                         _pack([a["v_" + n] for n in REPLICATED]), name="adamw_replicated")
    for kind, r in zip(kinds, upd_rep):
        out[kind].update(zip(REPLICATED, _unpack(r, rep_shapes)))
    loss_total = lax.psum(loss[0, 0], ("x", "y", "c"))
    return (loss_total, grad_x[None],
            *[out[kind][n] for kind in ("grad", "delta", "new_m", "new_v") for n in WEIGHT_ORDER])


def kernel(x, positions, ln_g, ln_b, ssd_w_in, ssd_conv_w, ssd_conv_b, ssd_dt_bias, ssd_a_log, ssd_d, ssd_norm_w, ssd_w_out, fox_w_in, fox_f_bias, fox_w_out, mla_w_in, mla_q_norm, mla_kv_norm, mla_w_q_up, mla_w_kv_up, mla_w_out, s5_w_in, s5_lambda_re, s5_lambda_im, s5_log_step, s5_b_re, s5_b_im, s5_c_re, s5_c_im, s5_d, s5_w_glu, s5_b_glu, s5_w_out, loss_target, m_ln_g, m_ln_b, m_ssd_w_in, m_ssd_conv_w, m_ssd_conv_b, m_ssd_dt_bias, m_ssd_a_log, m_ssd_d, m_ssd_norm_w, m_ssd_w_out, m_fox_w_in, m_fox_f_bias, m_fox_w_out, m_mla_w_in, m_mla_q_norm, m_mla_kv_norm, m_mla_w_q_up, m_mla_w_kv_up, m_mla_w_out, m_s5_w_in, m_s5_lambda_re, m_s5_lambda_im, m_s5_log_step, m_s5_b_re, m_s5_b_im, m_s5_c_re, m_s5_c_im, m_s5_d, m_s5_w_glu, m_s5_b_glu, m_s5_w_out, v_ln_g, v_ln_b, v_ssd_w_in, v_ssd_conv_w, v_ssd_conv_b, v_ssd_dt_bias, v_ssd_a_log, v_ssd_d, v_ssd_norm_w, v_ssd_w_out, v_fox_w_in, v_fox_f_bias, v_fox_w_out, v_mla_w_in, v_mla_q_norm, v_mla_kv_norm, v_mla_w_q_up, v_mla_w_kv_up, v_mla_w_out, v_s5_w_in, v_s5_lambda_re, v_s5_lambda_im, v_s5_log_step, v_s5_b_re, v_s5_b_im, v_s5_c_re, v_s5_c_im, v_s5_d, v_s5_w_glu, v_s5_b_glu, v_s5_w_out):
    return _train_step(dict(locals()))
```

```python
import functools
import math

import jax
import jax.numpy as jnp
from jax import lax
from jax.experimental import pallas as pl
from jax.experimental.pallas import tpu as pltpu

f32 = jnp.float32
bf16 = jnp.bfloat16

N_DEV = 8
DEPTH = 4
ALPHA = (2.0 * DEPTH) ** 0.25
LN_EPS = 1e-5
RMS_EPS = 1e-6
LANE = 128

SSD_STATE = 128
SSD_HEADDIM = 64
SSD_CHUNK = 128
FOX_HEAD_DIM = 128
MLA_NOPE = 128
MLA_ROPE = 64
MLA_V = 128
ROPE_BASE = 10000.0
S5_GROUP = 16
S5_BLOCK_GROUPS = 8
S5_CHUNK = 128

ADAM_LR = 0.001
ADAM_B1 = 0.9
ADAM_B2 = 0.999
ADAM_EPS = 1e-08
ADAM_WD = 0.01
ADAM_STEP = 10

VMEM_BLOCK_BUDGET = 28 * 1024 * 1024
MM_FULL_K = 2048
HIGHEST = lax.Precision.HIGHEST


def _pick(n, cands):
    for c in cands:
        if n % c == 0:
            return c
    return n


def _dg(a, b, ca, cb):
    return lax.dot_general(a.astype(bf16), b.astype(bf16), (((ca,), (cb,)), ((), ())),
                           preferred_element_type=f32)


@functools.partial(jax.custom_vjp, nondiff_argnums=(2, 3))
def bdot(a, b, ca, cb):
    return _dg(a, b, ca, cb)


def _bdot_fwd(a, b, ca, cb):
    return _dg(a, b, ca, cb), (a, b)


def _bdot_bwd(ca, cb, res, g):
    a, b = res
    nb = 1 - cb
    ma = 1 - ca
    da = _dg(g, b, 1, nb) if ca == 1 else _dg(b, g, nb, 1)
    db = _dg(a, g, ma, 0) if cb == 0 else _dg(g, a, 0, ma)
    return da, db


bdot.defvjp(_bdot_fwd, _bdot_bwd)


def _silu(x):
    return x * jax.nn.sigmoid(x)


def _softplus(x):
    return jnp.maximum(x, 0.0) + jnp.log(1.0 + jnp.exp(-jnp.abs(x)))


def _mm(a, b, *, ta=False, tb=False, add=None, out_dtype=f32, name, side=None):
    m, k = (a.shape[1], a.shape[0]) if ta else a.shape
    n = b.shape[0] if tb else b.shape[1]
    kb = b.shape[1] if tb else b.shape[0]
    assert k == kb, (name, a.shape, b.shape)
    if k <= MM_FULL_K:
        tm = _pick(m, (512, 256, 128))
        tn = _pick(n, (1024, 512, 384, 256, 128))
        tk = k
    else:
        tm = _pick(m, (1024, 512, 256, 128))
        tn = _pick(n, (1024, 512, 384, 256, 128))
        tk = _pick(k, (1024, 512, 256, 128))
    nk = k // tk
    has_add = add is not None

    def body(*refs):
        if has_add:
            a_ref, b_ref, add_ref, o_ref, acc = refs
        else:
            a_ref, b_ref, o_ref, acc = refs
        kk = pl.program_id(2)

        def prod():
            return _dg(a_ref[...], b_ref[...], 0 if ta else 1, 1 if tb else 0)

        def finish(r):
            o_ref[...] = (r + add_ref[...] if has_add else r).astype(o_ref.dtype)

        if nk == 1:
            finish(prod())
            return

        @pl.when(kk == 0)
        def _():
            acc[...] = prod()

        @pl.when((kk > 0) & (kk < nk - 1))
        def _():
            acc[...] += prod()

        @pl.when(kk == nk - 1)
        def _():
            finish(acc[...] + prod())

    a_spec = (pl.BlockSpec((tk, tm), lambda i, j, kk: (kk, i)) if ta
              else pl.BlockSpec((tm, tk), lambda i, j, kk: (i, kk)))
    b_spec = (pl.BlockSpec((tn, tk), lambda i, j, kk: (j, kk)) if tb
              else pl.BlockSpec((tk, tn), lambda i, j, kk: (kk, j)))
    o_spec = pl.BlockSpec((tm, tn), lambda i, j, kk: (i, j))
    in_specs = [a_spec, b_spec] + ([o_spec] if has_add else [])
    args = (a, b) + ((add,) if has_add else ())
    if side is not None:
        (out,), got = _hosted_call(
            body, side, name=name, grid=(m // tm, n // tn, nk), in_specs=in_specs, out_specs=[o_spec],
            out_shape=[jax.ShapeDtypeStruct((m, n), out_dtype)], scratch_shapes=[pltpu.VMEM((tm, tn), f32)],
            args=args)
        return out, got
    return pl.pallas_call(
        body, name=name, grid=(m // tm, n // tn, nk),
        in_specs=in_specs, out_specs=o_spec,
        out_shape=jax.ShapeDtypeStruct((m, n), out_dtype),
        scratch_shapes=[pltpu.VMEM((tm, tn), f32)],
        compiler_params=pltpu.CompilerParams(dimension_semantics=("parallel", "parallel", "arbitrary")),
    )(*args)


def _row_operand(op):
    if isinstance(op, tuple):
        arr, w, cb = op
    else:
        arr, w, cb = op, op.shape[1], 0
    return arr, w, cb


def _rw_tm(t, widths):
    per_row = 2 * 4 * sum(widths)
    tm = 512
    while tm > 8 and (tm * per_row > VMEM_BLOCK_BUDGET or t % tm):
        tm //= 2
    return tm


def _rw_specs(ops, tm):
    specs, arrs = [], []
    for op in ops:
        arr, w, cb = _row_operand(op)
        specs.append(pl.BlockSpec((tm, w), functools.partial(lambda i, cb: (i, cb), cb=cb)))
        arrs.append(arr)
    return specs, arrs


def _param_spec(p):
    nd = p.ndim
    return pl.BlockSpec(p.shape, lambda i: (0,) * nd)


def _rw_fwd(fn, rows, params, out_widths, *, name):
    t = _row_operand(rows[0])[0].shape[0]
    tm = _rw_tm(t, [_row_operand(r)[1] for r in rows] + list(out_widths))
    nr, npar = len(rows), len(params)

    def body(*refs):
        vals = [r[...] for r in refs[:nr + npar]]
        outs = fn(*vals)
        for o_ref, v in zip(refs[nr + npar:], outs):
            o_ref[...] = v

    rspecs, rarrs = _rw_specs(rows, tm)
    return pl.pallas_call(
        body, name=name, grid=(t // tm,),
        in_specs=rspecs + [_param_spec(p) for p in params],
        out_specs=[pl.BlockSpec((tm, w), lambda i: (i, 0)) for w in out_widths],
        out_shape=[jax.ShapeDtypeStruct((t, w), f32) for w in out_widths],
        compiler_params=pltpu.CompilerParams(dimension_semantics=("parallel",)),
    )(*rarrs, *params)


def _rw_bwd(fn, rows, params, cots, *, name, n_const=0):
    t = _row_operand(rows[0])[0].shape[0]
    nr, npar, nc = len(rows), len(params), len(cots)
    nd = nr - n_const
    widths = [_row_operand(r)[1] for r in rows]
    tm = _rw_tm(t, widths + widths[:nd] + [c.shape[1] for c in cots])

    def body(*refs):
        rv = [r[...] for r in refs[:nr]]
        pv = [r[...] for r in refs[nr:nr + npar]]
        cv = [r[...] for r in refs[nr + npar:nr + npar + nc]]
        outs = refs[nr + npar + nc:]
        consts = rv[nd:]

        def f(*diff):
            return fn(*diff[:nd], *consts, *diff[nd:])

        _, vjp = jax.vjp(f, *rv[:nd], *pv)
        g = vjp(tuple(cv))
        for o_ref, v in zip(outs[:nd], g[:nd]):
            o_ref[...] = v
        if npar:
            @pl.when(pl.program_id(0) == 0)
            def _():
                for o_ref in outs[nd:]:
                    o_ref[...] = jnp.zeros_like(o_ref)

            for o_ref, v in zip(outs[nd:], g[nd:]):
                o_ref[...] += v

    rspecs, rarrs = _rw_specs(rows, tm)
    cspecs = [pl.BlockSpec((tm, c.shape[1]), lambda i: (i, 0)) for c in cots]
    return pl.pallas_call(
        body, name=name, grid=(t // tm,),
        in_specs=rspecs + [_param_spec(p) for p in params] + cspecs,
        out_specs=[pl.BlockSpec((tm, w), lambda i: (i, 0)) for w in widths[:nd]]
        + [_param_spec(p) for p in params],
        out_shape=[jax.ShapeDtypeStruct((t, w), f32) for w in widths[:nd]]
        + [jax.ShapeDtypeStruct(p.shape, f32) for p in params],
        compiler_params=pltpu.CompilerParams(dimension_semantics=("arbitrary",)),
    )(*rarrs, *params, *cots)


def _ln_fn(h, o, g, b):
    x = ALPHA * h + o
    mu = jnp.mean(x, -1, keepdims=True)
    var = jnp.mean(jnp.square(x - mu), -1, keepdims=True)
    return ((x - mu) * lax.rsqrt(var + LN_EPS) * g + b,)


def _gate_fn(o, z):
    return (o * _silu(z),)


def _make_ssd_post_fn(groups):
    def fn(y, z, nw):
        yz = y * _silu(z)
        gw = yz.shape[1] // groups
        outs = []
        for g in range(groups):
            blk = yz[:, g * gw:(g + 1) * gw]
            outs.append(blk * lax.rsqrt(jnp.mean(jnp.square(blk), -1, keepdims=True) + RMS_EPS))
        return (jnp.concatenate(outs, axis=1) * nw,)
    return fn


def _logf_fn(f, b):
    return (jax.nn.log_sigmoid(f + b),)


def _make_mla_norm_fn(rank):
    def fn(lat, qn, kvn):
        def rms(x, w):
            return x * lax.rsqrt(jnp.mean(jnp.square(x), -1, keepdims=True) + RMS_EPS) * w
        return rms(lat[:, :rank], qn), rms(lat[:, rank:], kvn)
    return fn


def _rope_block(xb, cos, sin):
    half = MLA_ROPE // 2
    x1, x2 = xb[:, :half], xb[:, half:MLA_ROPE]
    return jnp.concatenate([x1 * cos - x2 * sin, x1 * sin + x2 * cos, xb[:, MLA_ROPE:]], axis=1)


def _make_rope_fn(heads):
    def fn(x, cos, sin):
        return (jnp.concatenate([_rope_block(x[:, h * LANE:(h + 1) * LANE], cos, sin)
                                 for h in range(heads)], axis=1),)
    return fn


def _s5_act_fn(ys, u, d):
    return (jax.nn.gelu(ys + d * u),)


def _s5_glu_fn(y1, gp, z, bg):
    return (y1 * jax.nn.sigmoid(gp + bg) * _silu(z),)


def _loss_head(y, tgt):
    t, d = y.shape
    tm = _rw_tm(t, [d, d, d])

    def body(y_ref, t_ref, dy_ref, l_ref):
        @pl.when(pl.program_id(0) == 0)
        def _():
            l_ref[...] = jnp.zeros_like(l_ref)

        e = y_ref[...] - t_ref[...]
        dy_ref[...] = e * (1.0 / d)
        l_ref[...] += 0.5 * jnp.sum(jnp.mean(jnp.square(e), axis=-1, keepdims=True), axis=0, keepdims=True)

    spec = pl.BlockSpec((tm, d), lambda i: (i, 0))
    return pl.pallas_call(
        body, name="loss_head", grid=(t // tm,), in_specs=[spec, spec],
        out_specs=[spec, pl.BlockSpec((1, 1), lambda i: (0, 0))],
        out_shape=[jax.ShapeDtypeStruct((t, d), f32), jax.ShapeDtypeStruct((1, 1), f32)],
        compiler_params=pltpu.CompilerParams(dimension_semantics=("arbitrary",)),
    )(y, tgt)


def _attn_tile(t):
    return _pick(t, (512, 256, 128))


ATTN_HEADS_PER_STEP = 8
ATTN_BWD_HEADS_PER_STEP = 4
SSD_GROUPS_PER_STEP = 1
S5_BLOCKS_PER_STEP = 2


def _logits_t(k1, q1, k2, q2, ck_col, scale, key0, query0):
    s = _dg(k1, q1, 1, 1)
    if q2 is not None:
        s = s + _dg(k2, q2, 1, 1)
    s = s * scale
    if ck_col is not None:
        s = s - ck_col
    key = key0 + lax.broadcasted_iota(jnp.int32, s.shape, 0)
    query = query0 + lax.broadcasted_iota(jnp.int32, s.shape, 1)
    return jnp.where(key <= query, s, -jnp.inf)


def _lane_bcast(v):
    return jnp.broadcast_to(v[:, :, None], v.shape + (LANE,))


def _attn_fwd(q1, q1o, k1, k1o, v, vo, q2, k2, ck, *, heads, scale, name, side=None):
    t = q1.shape[0]
    tq = tk = _attn_tile(t)
    nq = t // tq
    has2, hasb = q2 is not None, ck is not None
    hp = math.gcd(ATTN_HEADS_PER_STEP, heads)
    wd = hp * LANE
    assert heads % hp == 0 and q1o % hp == 0 and k1o % hp == 0 and vo % hp == 0

    def body(*refs):
        it = iter(refs)
        q1r, k1r, vr = next(it), next(it), next(it)
        q2r, k2r = (next(it), next(it)) if has2 else (None, None)
        ckr = next(it) if hasb else None
        o_r, lse_r, m_s, l_s, acc = next(it), next(it), next(it), next(it), next(it)
        i, j = pl.program_id(1), pl.program_id(2)

        @pl.when(j == 0)
        def _():
            m_s[...] = jnp.full_like(m_s, -jnp.inf)
            l_s[...] = jnp.zeros_like(l_s)
            acc[...] = jnp.zeros_like(acc)

        @pl.when(j <= i)
        def _():
            k2v = k2r[...].astype(bf16) if has2 else None
            for e in range(hp):
                ln = slice(e * LANE, (e + 1) * LANE)
                q1v, k1v, vv = (r[:, ln].astype(bf16) for r in (q1r, k1r, vr))
                q2v = q2r[:, ln].astype(bf16) if has2 else None
                s = _logits_t(k1v, q1v, k2v, q2v, ckr[e][:, 0:1] if hasb else None, scale, j * tk, i * tq)
                m_prev = m_s[e]
                m_new = jnp.maximum(m_prev, jnp.max(s, axis=0, keepdims=True))
                p = jnp.exp(s - m_new)
                a = jnp.exp(m_prev - m_new)
                l_s[e] = a * l_s[e] + jnp.sum(p, axis=0, keepdims=True)
                acc[e] = a * acc[e] + _dg(vv, p, 0, 0)
                m_s[e] = m_new

        @pl.when(j == i)
        def _():
            for e in range(hp):
                o_r[:, e * LANE:(e + 1) * LANE] = (acc[e] / l_s[e]).T
                lse_r[e, 0] = m_s[e] + jnp.log(l_s[e])

    def qmap(off):
        return lambda h, i, j: (i, off // hp + h)

    def kmap(off):
        return lambda h, i, j: (jnp.minimum(j, i), off // hp + h)

    in_specs = [pl.BlockSpec((tq, wd), qmap(q1o)), pl.BlockSpec((tk, wd), kmap(k1o)),
                pl.BlockSpec((tk, wd), kmap(vo))]
    args = [q1, k1, v]
    if has2:
        in_specs += [pl.BlockSpec((tq, wd), qmap(0)),
                     pl.BlockSpec((tk, LANE), lambda h, i, j: (jnp.minimum(j, i), 0))]
        args += [q2, k2]
    if hasb:
        in_specs += [pl.BlockSpec((hp, tk, LANE), lambda h, i, j: (h, jnp.minimum(j, i), 0))]
        args += [ck]
    return _hosted_call(
        body, side, name=name, grid=(heads // hp, nq, nq), in_specs=in_specs,
        out_specs=[pl.BlockSpec((tq, wd), lambda h, i, j: (i, h)),
                   pl.BlockSpec((hp, 1, 1, tq), lambda h, i, j: (h, i, 0, 0))],
        out_shape=[jax.ShapeDtypeStruct((t, heads * LANE), f32),
                   jax.ShapeDtypeStruct((heads, nq, 1, tq), f32)],
        scratch_shapes=[pltpu.VMEM((hp, 1, tq), f32), pltpu.VMEM((hp, 1, tq), f32),
                        pltpu.VMEM((hp, LANE, tq), f32)],
        args=args)


def _row_sums_as_row(x, first_lane_only=False):
    sel = jnp.ones((8, x.shape[1]), f32)
    if first_lane_only:
        sel = (lax.broadcasted_iota(jnp.int32, sel.shape, 1) == 0).astype(f32)
    return lax.dot_general(sel, x, (((1,), (1,)), ((), ())), precision=HIGHEST, preferred_element_type=f32)[0:1]


def _attn_delta(do, o, *, heads, name):
    t = do.shape[0]
    tq = _attn_tile(t)
    hp = math.gcd(ATTN_HEADS_PER_STEP, heads)

    def body(do_ref, o_ref, d_ref):
        for e in range(hp):
            ln = slice(e * LANE, (e + 1) * LANE)
            d_ref[e, 0] = _row_sums_as_row(do_ref[:, ln] * o_ref[:, ln])

    spec = pl.BlockSpec((tq, hp * LANE), lambda i, h: (i, h))
    return pl.pallas_call(
        body, name=name, grid=(t // tq, heads // hp), in_specs=[spec, spec],
        out_specs=pl.BlockSpec((hp, 1, 1, tq), lambda i, h: (h, i, 0, 0)),
        out_shape=jax.ShapeDtypeStruct((heads, t // tq, 1, tq), f32),
        compiler_params=pltpu.CompilerParams(dimension_semantics=("parallel", "parallel")),
    )(do, o)


def _attn_bwd(q1, q1o, k1, k1o, v, vo, q2, k2, ck, do, lse, delta, *, heads, scale, name, side=None):
    t = q1.shape[0]
    tq = tk = _attn_tile(t)
    nq = t // tq
    has2, hasb = q2 is not None, ck is not None
    hp = ATTN_BWD_HEADS_PER_STEP // 2 if has2 else ATTN_BWD_HEADS_PER_STEP
    wd = hp * LANE
    assert heads % hp == 0 and q1o % hp == 0 and k1o % hp == 0 and vo % hp == 0

    def body(*refs):
        it = iter(refs)
        q1r, k1r, vr = next(it), next(it), next(it)
        q2r, k2r = (next(it), next(it)) if has2 else (None, None)
        ckr = next(it) if hasb else None
        dor, lser, dlr = next(it), next(it), next(it)
        dq1r, dk1r, dvr = next(it), next(it), next(it)
        dq2r, dk2r = (next(it), next(it)) if has2 else (None, None)
        drowr, dckr = (next(it), next(it)) if hasb else (None, None)
        ak, av = next(it), next(it)
        ac = next(it) if hasb else None
        h, j, i = pl.program_id(0), pl.program_id(1), pl.program_id(2)

        @pl.when((j == 0) & (i == 0))
        def _():
            dq1r[...] = jnp.zeros_like(dq1r)
            if has2:
                dq2r[...] = jnp.zeros_like(dq2r)
            if hasb:
                drowr[...] = jnp.zeros_like(drowr)

        if has2:
            @pl.when((h == 0) & (j == 0) & (i == 0))
            def _():
                dk2r[...] = jnp.zeros_like(dk2r)

        @pl.when(i == j)
        def _():
            ak[...] = jnp.zeros_like(ak)
            av[...] = jnp.zeros_like(av)
            if hasb:
                ac[...] = jnp.zeros_like(ac)

        @pl.when(i >= j)
        def _():
            rows = pl.ds(pl.multiple_of(i * tq, tq), tq)
            cols = pl.ds(pl.multiple_of(j * tk, tk), tk)
            k2v = k2r[...].astype(bf16) if has2 else None
            for e in range(hp):
                ln = slice(e * LANE, (e + 1) * LANE)
                q1v, k1v, vv, dov = (r[:, ln].astype(bf16) for r in (q1r, k1r, vr, dor))
                q2v = q2r[:, ln].astype(bf16) if has2 else None
                s = _logits_t(k1v, q1v, k2v, q2v, ckr[e][:, 0:1] if hasb else None, scale, j * tk, i * tq)
                p = jnp.exp(s - lser[e, 0])
                dp = _dg(vv, dov, 1, 1)
                dsr = p * (dp - dlr[e, 0])
                ds = (dsr * scale).astype(bf16)
                av[:, ln] += _dg(p, dov, 1, 0)
                ak[:, ln] += _dg(ds, q1v, 1, 0)
                dq1r[rows, ln] += _dg(ds, k1v, 0, 0)
                if has2:
                    dq2r[rows, ln] += _dg(ds, k2v, 0, 0)
                    dk2r[cols, :] += _dg(ds, q2v, 1, 0)
                if hasb:
                    drowr[e, pl.ds(i, 1), :] += jnp.sum(dsr, axis=0, keepdims=True)
                    ac[e] -= jnp.broadcast_to(jnp.sum(dsr, axis=1, keepdims=True), (tk, LANE))

        @pl.when(i == nq - 1)
        def _():
            dk1r[...] = ak[...]
            dvr[...] = av[...]
            if hasb:
                for e in range(hp):
                    dckr[e] = _row_sums_as_row(ac[e], first_lane_only=True)

    def qmap(off):
        return lambda h, j, i: (jnp.maximum(i, j), off // hp + h)

    def kmap(off):
        return lambda h, j, i: (j, off // hp + h)

    in_specs = [pl.BlockSpec((tq, wd), qmap(q1o)), pl.BlockSpec((tk, wd), kmap(k1o)),
                pl.BlockSpec((tk, wd), kmap(vo))]
    args = [q1, k1, v]
    if has2:
        in_specs += [pl.BlockSpec((tq, wd), qmap(0)), pl.BlockSpec((tk, LANE), lambda h, j, i: (j, 0))]
        args += [q2, k2]
    if hasb:
        in_specs += [pl.BlockSpec((hp, tk, LANE), lambda h, j, i: (h, j, 0))]
        args += [ck]
    row3 = pl.BlockSpec((hp, 1, 1, tq), lambda h, j, i: (h, jnp.maximum(i, j), 0, 0))
    in_specs += [pl.BlockSpec((tq, wd), qmap(0)), row3, row3]
    args += [do, lse, delta]
    hd = jax.ShapeDtypeStruct((t, heads * LANE), f32)
    head_cols = pl.BlockSpec((t, wd), lambda h, j, i: (0, h))
    kv_blk = pl.BlockSpec((tk, wd), kmap(0))
    out_specs, out_shape = [head_cols, kv_blk, kv_blk], [hd, hd, hd]
    scratch = [pltpu.VMEM((tk, wd), f32)] * 2
    if has2:
        out_specs += [head_cols, pl.BlockSpec((t, LANE), lambda h, j, i: (0, 0))]
        out_shape += [hd, jax.ShapeDtypeStruct((t, LANE), f32)]
    if hasb:
        out_specs += [pl.BlockSpec((hp, nq, tq), lambda h, j, i: (h, 0, 0)),
                      pl.BlockSpec((hp, 1, tk), lambda h, j, i: (h, 0, j))]
        out_shape += [jax.ShapeDtypeStruct((heads, nq, tq), f32), jax.ShapeDtypeStruct((heads, 1, t), f32)]
        scratch.append(pltpu.VMEM((hp, tk, LANE), f32))
    return _hosted_call(body, side, name=name, grid=(heads // hp, nq, nq), in_specs=in_specs, out_specs=out_specs,
                        out_shape=out_shape, scratch_shapes=scratch, args=args)


def _cumsum_rows(x, *, reverse, name):
    t, w = x.shape
    blk = 128
    nb = t // blk

    def body(x_ref, o_ref):
        r = lax.broadcasted_iota(jnp.int32, (blk, blk), 0)
        c = lax.broadcasted_iota(jnp.int32, (blk, blk), 1)
        tri = ((r <= c) if reverse else (r >= c)).astype(f32)
        carry = jnp.zeros((1, w), f32)
        for b in (reversed(range(nb)) if reverse else range(nb)):
            seg = x_ref[b * blk:(b + 1) * blk, :]
            cs = jnp.dot(tri, seg, precision=HIGHEST, preferred_element_type=f32) + carry
            o_ref[b * blk:(b + 1) * blk, :] = cs
            carry = cs[0:1, :] if reverse else cs[blk - 1:blk, :]

    return pl.pallas_call(body, name=name, out_shape=jax.ShapeDtypeStruct((t, w), f32))(x)


CONV_PAD = 8


def _conv_fwd(x, w, b, *, name):
    t, c = x.shape
    tc = LANE
    taps = w.shape[0]
    rb = _pick(t, (512, 256, 128))
    assert taps - 1 <= CONV_PAD

    def body(x_ref, w_ref, b_ref, o_ref, xpad):
        xpad[0:CONV_PAD, :] = jnp.zeros((CONV_PAD, tc), f32)
        xpad[CONV_PAD:CONV_PAD + t, :] = x_ref[...]
        wv, bv = w_ref[...], b_ref[...]
        for r0 in range(0, t, rb):
            first = r0 + CONV_PAD - (taps - 1)
            y = bv
            for k in range(taps):
                y = y + wv[k:k + 1, :] * xpad[first + k:first + k + rb, :]
            o_ref[r0:r0 + rb, :] = _silu(y)

    xs = pl.BlockSpec((t, tc), lambda i: (0, i))
    return pl.pallas_call(
        body, name=name, grid=(c // tc,),
        in_specs=[xs, pl.BlockSpec((taps, tc), lambda i: (0, i)), pl.BlockSpec((1, tc), lambda i: (0, i))],
        out_specs=xs, out_shape=jax.ShapeDtypeStruct((t, c), f32),
        scratch_shapes=[pltpu.VMEM((t + CONV_PAD, tc), f32)],
        compiler_params=pltpu.CompilerParams(dimension_semantics=("parallel",)),
    )(x, w, b)


def _conv_bwd(x, w, b, ds, *, name):
    t, c = x.shape
    tc = LANE
    taps = w.shape[0]

    rb = _pick(t, (512, 256, 128))
    assert taps - 1 <= CONV_PAD

    def body(x_ref, w_ref, b_ref, ds_ref, dx_ref, dw_ref, db_ref, xpad, dypad):
        xpad[0:CONV_PAD, :] = jnp.zeros((CONV_PAD, tc), f32)
        xpad[CONV_PAD:CONV_PAD + t, :] = x_ref[...]
        dypad[t:t + CONV_PAD, :] = jnp.zeros((CONV_PAD, tc), f32)
        wv, bv = w_ref[...], b_ref[...]
        dw = [jnp.zeros((1, tc), f32) for _ in range(taps)]
        db = jnp.zeros((1, tc), f32)
        for r0 in range(0, t, rb):
            first = r0 + CONV_PAD - (taps - 1)
            xsh = [xpad[first + k:first + k + rb, :] for k in range(taps)]
            y = bv
            for k in range(taps):
                y = y + wv[k:k + 1, :] * xsh[k]
            sig = jax.nn.sigmoid(y)
            dy = ds_ref[r0:r0 + rb, :] * (sig * (1.0 + y * (1.0 - sig)))
            dypad[r0:r0 + rb, :] = dy
            for k in range(taps):
                dw[k] = dw[k] + jnp.sum(dy * xsh[k], axis=0, keepdims=True)
            db = db + jnp.sum(dy, axis=0, keepdims=True)
        for r0 in range(0, t, rb):
            dx = jnp.zeros((rb, tc), f32)
            for k in range(taps):
                first = r0 + (taps - 1) - k
                dx = dx + wv[k:k + 1, :] * dypad[first:first + rb, :]
            dx_ref[r0:r0 + rb, :] = dx
        dw_ref[...] = jnp.concatenate(dw, axis=0)
        db_ref[...] = db

    xs = pl.BlockSpec((t, tc), lambda i: (0, i))
    ws = pl.BlockSpec((taps, tc), lambda i: (0, i))
    bs = pl.BlockSpec((1, tc), lambda i: (0, i))
    return pl.pallas_call(
        body, name=name, grid=(c // tc,), in_specs=[xs, ws, bs, xs], out_specs=[xs, ws, bs],
        out_shape=[jax.ShapeDtypeStruct((t, c), f32), jax.ShapeDtypeStruct((taps, c), f32),
                   jax.ShapeDtypeStruct((1, c), f32)],
        scratch_shapes=[pltpu.VMEM((t + CONV_PAD, tc), f32), pltpu.VMEM((t + CONV_PAD, tc), f32)],
        compiler_params=pltpu.CompilerParams(dimension_semantics=("parallel",)),
    )(x, w, b, ds)


def _ssd_chunk(x, bm, cm, dtr, st, dtb, alog, dsk, *, e_heads):
    ln, p = x.shape[0], SSD_HEADDIM
    dt = _softplus(dtr + dtb)
    da = dt * (-jnp.exp(alog))
    r = lax.broadcasted_iota(jnp.int32, (ln, ln), 0)
    c = lax.broadcasted_iota(jnp.int32, (ln, ln), 1)
    lower = r >= c
    acs = jnp.dot(lower.astype(f32), da, precision=HIGHEST, preferred_element_type=f32)
    acs_t = acs.T

    def per_head(v):
        return jnp.concatenate([jnp.broadcast_to(v[:, e:e + 1], (v.shape[0], p)) for e in range(e_heads)], axis=1)

    dt_x, acs_x = per_head(dt), per_head(acs)
    last_x = acs_x[ln - 1:ln, :]
    xdt = x * dt_x
    cb = bdot(cm, bm, 1, 1)
    y_off = bdot(cm, st, 1, 0) * jnp.exp(acs_x)
    st_new = st * jnp.exp(last_x) + bdot(bm, xdt * jnp.exp(last_x - acs_x), 0, 0)
    ys = []
    for e in range(e_heads):
        seg = acs[:, e:e + 1] - acs_t[e:e + 1, :]
        dec = jnp.exp(jnp.where(lower, seg, -jnp.inf))
        ys.append(bdot(cb * dec, xdt[:, e * p:(e + 1) * p], 1, 0))
    y = jnp.concatenate(ys, axis=1) + y_off + per_head(dsk) * x
    return y, st_new


def _ssd_specs(t, d_inner, groups):
    ln, n = SSD_CHUNK, SSD_STATE
    gw = d_inner // groups
    nc = t // ln
    return ln, n, gw, nc


def _ssd_fwd(xbc, dtp, dtb, alog, dsk, *, d_inner, groups, name, side=None):
    t = xbc.shape[0]
    ln, n, gw, nc = _ssd_specs(t, d_inner, groups)
    e_heads = gw // SSD_HEADDIM
    gp = SSD_GROUPS_PER_STEP
    assert groups % gp == 0 and (d_inner // n) % gp == 0
    boff = d_inner // n // gp
    ng = groups // gp

    def body(x_ref, b_ref, c_ref, dt_ref, dtb_ref, alog_ref, dsk_ref, y_ref, save_ref, st_scr):
        c, g = pl.program_id(0), pl.program_id(1)
        for e in range(gp):
            gi = g * gp + e

            @pl.when(c == 0)
            def _():
                st_scr[gi] = jnp.zeros((n, gw), f32)

            st = st_scr[gi]
            save_ref[e] = st
            y, st_new = _ssd_chunk(x_ref[:, e * gw:(e + 1) * gw], b_ref[:, e * n:(e + 1) * n],
                                   c_ref[:, e * n:(e + 1) * n], dt_ref[:, e * LANE:(e + 1) * LANE], st,
                                   dtb_ref[e], alog_ref[e], dsk_ref[e], e_heads=e_heads)
            y_ref[:, e * gw:(e + 1) * gw] = y
            st_scr[gi] = st_new

    par = pl.BlockSpec((gp, 1, LANE), lambda c, g: (g, 0, 0))
    return _hosted_call(
        body, side, name=name, grid=(nc, ng),
        in_specs=[pl.BlockSpec((ln, gp * gw), lambda c, g: (c, g)),
                  pl.BlockSpec((ln, gp * n), lambda c, g: (c, boff + g)),
                  pl.BlockSpec((ln, gp * n), lambda c, g: (c, boff + ng + g)),
                  pl.BlockSpec((ln, gp * LANE), lambda c, g: (c, g)), par, par, par],
        out_specs=[pl.BlockSpec((ln, gp * gw), lambda c, g: (c, g)),
                   pl.BlockSpec((gp, n, gw), lambda c, g: (c * ng + g, 0, 0))],
        out_shape=[jax.ShapeDtypeStruct((t, d_inner), f32),
                   jax.ShapeDtypeStruct((nc * groups, n, gw), f32)],
        scratch_shapes=[pltpu.VMEM((groups, n, gw), f32)],
        args=(xbc, xbc, xbc, dtp, dtb, alog, dsk))


def _ssd_bwd(xbc, dtp, dtb, alog, dsk, saved, dy, *, d_inner, groups, name, side=None):
    t = xbc.shape[0]
    ln, n, gw, nc = _ssd_specs(t, d_inner, groups)
    e_heads = gw // SSD_HEADDIM
    gp = SSD_GROUPS_PER_STEP
    assert groups % gp == 0 and (d_inner // n) % gp == 0
    boff = d_inner // n // gp
    ng = groups // gp

    def body(x_ref, b_ref, c_ref, dt_ref, dtb_ref, alog_ref, dsk_ref, save_ref, dy_ref,
             dx_ref, db_ref, dc_ref, ddt_ref, dpar_ref, dst_scr):
        c, g = pl.program_id(0), pl.program_id(1)

        @pl.when((c == 0) & (g == 0))
        def _():
            dpar_ref[...] = jnp.zeros_like(dpar_ref)

        fn = functools.partial(_ssd_chunk, e_heads=e_heads)
        for e in range(gp):
            gi = g * gp + e
            xl, nl, dl = (slice(e * w, (e + 1) * w) for w in (gw, n, LANE))

            @pl.when(c == 0)
            def _():
                dst_scr[gi] = jnp.zeros((n, gw), f32)

            _, vjp = jax.vjp(fn, x_ref[:, xl], b_ref[:, nl], c_ref[:, nl], dt_ref[:, dl], save_ref[e],
                             dtb_ref[e], alog_ref[e], dsk_ref[e])
            gx, gb, gc, gdt, gst, gdtb, galog, gdsk = vjp((dy_ref[:, xl], dst_scr[gi]))
            dx_ref[:, xl] = gx
            db_ref[:, nl] = gb
            dc_ref[:, nl] = gc
            ddt_ref[:, dl] = gdt
            dst_scr[gi] = gst
            dpar_ref[gi, 0:1, :] += gdtb
            dpar_ref[gi, 1:2, :] += galog
            dpar_ref[gi, 2:3, :] += gdsk

    def rc(c):
        return nc - 1 - c

    par = pl.BlockSpec((gp, 1, LANE), lambda c, g: (g, 0, 0))
    xs = pl.BlockSpec((ln, gp * gw), lambda c, g: (rc(c), g))
    ns = pl.BlockSpec((ln, gp * n), lambda c, g: (rc(c), g))
    return _hosted_call(
        body, side, name=name, grid=(nc, ng),
        in_specs=[xs,
                  pl.BlockSpec((ln, gp * n), lambda c, g: (rc(c), boff + g)),
                  pl.BlockSpec((ln, gp * n), lambda c, g: (rc(c), boff + ng + g)),
                  pl.BlockSpec((ln, gp * LANE), lambda c, g: (rc(c), g)), par, par, par,
                  pl.BlockSpec((gp, n, gw), lambda c, g: (rc(c) * ng + g, 0, 0)), xs],
        out_specs=[xs, ns, ns, pl.BlockSpec((ln, gp * LANE), lambda c, g: (rc(c), g)),
                   pl.BlockSpec((groups, 8, LANE), lambda c, g: (0, 0, 0))],
        out_shape=[jax.ShapeDtypeStruct((t, d_inner), f32),
                   jax.ShapeDtypeStruct((t, groups * n), f32),
                   jax.ShapeDtypeStruct((t, groups * n), f32),
                   jax.ShapeDtypeStruct((t, groups * LANE), f32),
                   jax.ShapeDtypeStruct((groups, 8, LANE), f32)],
        scratch_shapes=[pltpu.VMEM((groups, n, gw), f32)],
        args=(xbc, xbc, xbc, dtp, dtb, alog, dsk, saved, dy))


S5_TOP = 8


def _cmul(ar, ai, br, bi):
    return ar * br - ai * bi, ar * bi + ai * br


def _s5_scan(scr, base, ln, pw_ref, sp, *, reverse):
    s, k = 1, 0
    while s < ln:
        pr = pw_ref[0, k:k + 1, :sp]
        pi = pw_ref[0, k:k + 1, sp:]
        if reverse:
            pi = -pi
            src, dst = base + s, base
        else:
            src, dst = base, base + s
        ar, ai = scr[src:src + ln - s, :sp], scr[src:src + ln - s, sp:]
        mr, mi = _cmul(ar, ai, pr, pi)
        scr[dst:dst + ln - s, :sp] = scr[dst:dst + ln - s, :sp] + mr
        scr[dst:dst + ln - s, sp:] = scr[dst:dst + ln - s, sp:] + mi
        s *= 2
        k += 1


def _s5_states(u_ref, b_ref, lam_ref, pw_ref, scr, carry_r, carry_i, ln, sp):
    scr[S5_TOP:S5_TOP + ln, :] = _dg(u_ref[...], b_ref[0], 1, 0)
    mr, mi = _cmul(carry_r, carry_i, lam_ref[0][:, :sp], lam_ref[0][:, sp:])
    scr[S5_TOP:S5_TOP + 1, :sp] = scr[S5_TOP:S5_TOP + 1, :sp] + mr
    scr[S5_TOP:S5_TOP + 1, sp:] = scr[S5_TOP:S5_TOP + 1, sp:] + mi
    _s5_scan(scr, S5_TOP, ln, pw_ref, sp, reverse=False)


def _s5_dims(t, u_width):
    cbw = S5_BLOCK_GROUPS * S5_GROUP
    return S5_CHUNK, cbw, u_width // cbw, t // S5_CHUNK


def _s5_fwd(u, lamv, pw, bmat, cmat, *, name):
    t, w = u.shape
    ln, cbw, nb, nc = _s5_dims(t, w)
    sp2 = lamv.shape[2]
    sp = sp2 // 2
    bp = S5_BLOCKS_PER_STEP
    assert nb % bp == 0

    def one_block(u_ref, lam_ref, pw_ref, b_ref, c_ref, ys_ref, xp_ref, st_ref, scr, carry):
        @pl.when(pl.program_id(1) == 0)
        def _():
            carry[...] = jnp.zeros_like(carry)

        xp_ref[0] = carry[0:1, :]
        _s5_states(u_ref, b_ref, lam_ref, pw_ref, scr, carry[0:1, :sp], carry[0:1, sp:], ln, sp)
        carry[0:1, :] = scr[S5_TOP + ln - 1:S5_TOP + ln, :]
        states = scr[S5_TOP:S5_TOP + ln, :]
        st_ref[0] = states
        ys_ref[...] = _dg(states, c_ref[0], 1, 0)

    def body(u_ref, lam_ref, pw_ref, b_ref, c_ref, ys_ref, xp_ref, st_ref, scr, carry):
        for e in range(bp):
            one, cols = pl.ds(e, 1), pl.ds(e * cbw, cbw)
            one_block(u_ref.at[:, cols], lam_ref.at[one], pw_ref.at[one], b_ref.at[one], c_ref.at[one],
                      ys_ref.at[:, cols], xp_ref.at[0, one], st_ref.at[one], scr.at[e], carry.at[e])

    def blk(shape):
        return pl.BlockSpec((bp,) + shape, lambda j, c: (j, 0, 0))

    return pl.pallas_call(
        body, name=name, grid=(nb // bp, nc),
        in_specs=[pl.BlockSpec((ln, bp * cbw), lambda j, c: (c, j)), blk((1, sp2)), blk((8, sp2)),
                  blk((cbw, sp2)), blk((sp2, cbw))],
        out_specs=[pl.BlockSpec((ln, bp * cbw), lambda j, c: (c, j)),
                   pl.BlockSpec((1, bp, 1, sp2), lambda j, c: (c, j, 0, 0)),
                   pl.BlockSpec((bp, ln, sp2), lambda j, c: (j, c, 0))],
        out_shape=[jax.ShapeDtypeStruct((t, w), f32), jax.ShapeDtypeStruct((nc, nb, 1, sp2), f32),
                   jax.ShapeDtypeStruct((nb, t, sp2), f32)],
        scratch_shapes=[pltpu.VMEM((bp, S5_TOP + ln, sp2), f32), pltpu.VMEM((bp, 8, sp2), f32)],
        compiler_params=pltpu.CompilerParams(dimension_semantics=("parallel", "arbitrary")),
    )(u, lamv, pw, bmat, cmat)


def _s5_bwd(u, dy, du_skip, xp, states, lamv, pw, bmat, cmat, *, name):
    t, w = u.shape
    ln, cbw, nb, nc = _s5_dims(t, w)
    sp2 = lamv.shape[2]
    sp = sp2 // 2
    bp = S5_BLOCKS_PER_STEP
    assert nb % bp == 0

    def body(u_ref, dy_ref, dus_ref, xp_ref, st_ref, lam_ref, pw_ref, b_ref, c_ref,
             du_ref, db_ref, dc_ref, dl_ref, scr, gsc, gcarry):
        for e in range(bp):
            one, cols = pl.ds(e, 1), pl.ds(e * cbw, cbw)
            one_block(u_ref.at[:, cols], dy_ref.at[:, cols], dus_ref.at[:, cols], xp_ref.at[0, one],
                      st_ref.at[one], lam_ref.at[one], pw_ref.at[one], b_ref.at[one], c_ref.at[one],
                      du_ref.at[:, cols], db_ref.at[one], dc_ref.at[one], dl_ref.at[one],
                      scr.at[e], gsc.at[e], gcarry.at[e])

    def one_block(u_ref, dy_ref, dus_ref, xp_ref, st_ref, lam_ref, pw_ref, b_ref, c_ref,
                  du_ref, db_ref, dc_ref, dl_ref, scr, gsc, gcarry):
        @pl.when(pl.program_id(1) == 0)
        def _():
            gcarry[...] = jnp.zeros_like(gcarry)
            db_ref[...] = jnp.zeros_like(db_ref)
            dc_ref[...] = jnp.zeros_like(dc_ref)
            dl_ref[...] = jnp.zeros_like(dl_ref)

        lr, li = lam_ref[0][:, :sp], lam_ref[0][:, sp:]
        scr[S5_TOP - 1:S5_TOP, :] = xp_ref[0]
        scr[S5_TOP:S5_TOP + ln, :] = st_ref[0]
        dy = dy_ref[...]
        gsc[0:ln, :] = _dg(dy, c_ref[0], 1, 1)
        mr, mi = _cmul(gcarry[0:1, :sp], gcarry[0:1, sp:], lr, -li)
        gsc[ln - 1:ln, :sp] = gsc[ln - 1:ln, :sp] + mr
        gsc[ln - 1:ln, sp:] = gsc[ln - 1:ln, sp:] + mi
        _s5_scan(gsc, 0, ln, pw_ref, sp, reverse=True)
        gcarry[0:1, :] = gsc[0:1, :]
        g = gsc[0:ln, :]
        du_ref[...] = dus_ref[...] + _dg(g, b_ref[0], 1, 1)
        db_ref[0] += _dg(u_ref[...], g, 0, 0)
        dc_ref[0] += _dg(scr[S5_TOP:S5_TOP + ln, :], dy, 0, 0)
        pr, pi = scr[S5_TOP - 1:S5_TOP - 1 + ln, :sp], scr[S5_TOP - 1:S5_TOP - 1 + ln, sp:]
        gr, gi = g[:, :sp], g[:, sp:]
        dl_ref[0, 0:1, :sp] += jnp.sum(pr * gr + pi * gi, axis=0, keepdims=True)
        dl_ref[0, 0:1, sp:] += jnp.sum(pr * gi - pi * gr, axis=0, keepdims=True)

    def rc(c):
        return nc - 1 - c

    def blk(shape):
        return pl.BlockSpec((bp,) + shape, lambda j, c: (j, 0, 0))

    row = pl.BlockSpec((ln, bp * cbw), lambda j, c: (rc(c), j))
    return pl.pallas_call(
        body, name=name, grid=(nb // bp, nc),
        in_specs=[row, row, row, pl.BlockSpec((1, bp, 1, sp2), lambda j, c: (rc(c), j, 0, 0)),
                  pl.BlockSpec((bp, ln, sp2), lambda j, c: (j, rc(c), 0)),
                  blk((1, sp2)), blk((8, sp2)), blk((cbw, sp2)), blk((sp2, cbw))],
        out_specs=[row, blk((cbw, sp2)), blk((sp2, cbw)), blk((8, sp2))],
        out_shape=[jax.ShapeDtypeStruct((t, w), f32), jax.ShapeDtypeStruct((nb, cbw, sp2), f32),
                   jax.ShapeDtypeStruct((nb, sp2, cbw), f32), jax.ShapeDtypeStruct((nb, 8, sp2), f32)],
        scratch_shapes=[pltpu.VMEM((bp, S5_TOP + ln, sp2), f32), pltpu.VMEM((bp, ln, sp2), f32),
                        pltpu.VMEM((bp, 8, sp2), f32)],
        compiler_params=pltpu.CompilerParams(dimension_semantics=("parallel", "arbitrary")),
    )(u, dy, du_skip, xp, states, lamv, pw, bmat, cmat)


def _s5_discretize(lre, lim, log_step, bre, bim, cre, cim):
    g, p = lre.shape
    gb = S5_BLOCK_GROUPS
    nb = g // gb
    lam = lax.complex(lre, lim)
    lam_bar = jnp.exp(lam * jnp.exp(log_step)[:, None])
    b_bar = ((lam_bar - 1.0) / lam)[..., None] * lax.complex(bre, bim)
    lb = lam_bar.reshape(nb, 1, gb * p)
    lamv = jnp.concatenate([jnp.real(lb), jnp.imag(lb)], axis=-1)
    eye = jnp.eye(gb, dtype=f32)
    bb = b_bar.reshape(nb, gb, p, S5_GROUP)

    def bdiag(v):
        return jnp.einsum('jgpi,gh->jgihp', v, eye).reshape(nb, gb * S5_GROUP, gb * p)

    bmat = jnp.concatenate([bdiag(jnp.real(bb)), bdiag(jnp.imag(bb))], axis=-1)

    def cdiag(v):
        return jnp.einsum('jgip,gh->jhpgi', v, eye).reshape(nb, gb * p, gb * S5_GROUP)

    cmat = jnp.concatenate([cdiag(cre.reshape(nb, gb, S5_GROUP, p)),
                            -cdiag(cim.reshape(nb, gb, S5_GROUP, p))], axis=1)
    return lamv, bmat, cmat


def _s5_powers(lamv):
    sp = lamv.shape[2] // 2
    pr, pi = lamv[:, :, :sp], lamv[:, :, sp:]
    rows = []
    for _ in range(8):
        rows.append(jnp.concatenate([pr, pi], axis=-1))
        pr, pi = pr * pr - pi * pi, 2.0 * pr * pi
    return lax.stop_gradient(jnp.concatenate(rows, axis=1))


def _ln_fwd(h, out, p, tag):
    return _rw_fwd(_ln_fn, [h, out], [p["ln_g"], p["ln_b"]], [h.shape[1]], name=tag + "_ln")[0]


def _ln_bwd(h, out, p, dh2, tag):
    return _rw_bwd(_ln_fn, [h, out], [p["ln_g"], p["ln_b"]], [dh2], name=tag + "_ln_bwd")


def _ssd_layer_fwd(h, p, wout_of, side=None):
    di, groups = p["wz"].shape[1], p["dtb"].shape[0]
    z = _mm(h, p["wz"], name="ssd_z")
    xr = _mm(h, p["wxbc"], name="ssd_xbc")
    dtp = _mm(h, p["wdt"], name="ssd_dt")
    xbc = _conv_fwd(xr, p["conv_w"], p["conv_b"], name="ssd_conv")
    (y, states), got = _ssd_fwd(xbc, dtp, p["dtb"], p["alog"], p["dsk"], d_inner=di, groups=groups,
                                name="ssd_scan", side=side)
    p["wout"] = wout_of(got)
    yn = _rw_fwd(_make_ssd_post_fn(groups), [y, z], [p["norm_w"]], [di], name="ssd_post")[0]
    out = _mm(yn, p["wout"], name="ssd_out")
    return _ln_fwd(h, out, p, "ssd"), (h, z, xr, dtp, xbc, states, y, yn, out), got


def _ssd_layer_bwd(saved, p, dh2, side=None, in_sides=None):
    h, z, xr, dtp, xbc, states, y, yn, out = saved
    di, groups = p["wz"].shape[1], p["dtb"].shape[0]
    dh, dout, dg, db = _ln_bwd(h, out, p, dh2, "ssd")
    dyn = _mm(dout, p["wout"], tb=True, name="ssd_out_dx")
    g = {"ln_g": dg, "ln_b": db, "wout": _mm(yn, dout, ta=True, out_dtype=bf16, name="ssd_out_dw")}
    if callable(side):
        side = side(g["wout"])
    dy, dz, g["norm_w"] = _rw_bwd(_make_ssd_post_fn(groups), [y, z], [p["norm_w"]], [dyn], name="ssd_post_bwd")
    (dx, dbm, dcm, ddt, dpar), got = _ssd_bwd(xbc, dtp, p["dtb"], p["alog"], p["dsk"], states, dy,
                                              d_inner=di, groups=groups, name="ssd_scan_bwd", side=side)
    g["dtb"], g["alog"], g["dsk"] = dpar[:, 0:1, :], dpar[:, 1:2, :], dpar[:, 2:3, :]
    dxr, g["conv_w"], g["conv_b"] = _conv_bwd(xr, p["conv_w"], p["conv_b"],
                                               jnp.concatenate([dx, dbm, dcm], axis=1), name="ssd_conv_bwd")
    g["wz"] = _mm(h, dz, ta=True, name="ssd_z_dw")
    g["wxbc"] = _mm(h, dxr, ta=True, name="ssd_xbc_dw")
    g["wdt"] = _mm(h, ddt, ta=True, name="ssd_dt_dw")
    if in_sides is None:
        dh = _mm(dxr, p["wxbc"], tb=True, add=dh, name="ssd_xbc_dx")
        dh = _mm(dz, p["wz"], tb=True, add=dh, name="ssd_z_dx")
        got_in = (None, None)
    else:
        side_a, side_b = in_sides(g)
        dh, got_a = _mm(dxr, p["wxbc"], tb=True, add=dh, name="ssd_xbc_dx", side=side_a)
        dh, got_b = _mm(dz, p["wz"], tb=True, add=dh, name="ssd_z_dx", side=side_b)
        got_in = (got_a, got_b)
    dh = _mm(ddt, p["wdt"], tb=True, add=dh, name="ssd_dt_dx")
    return dh, g, got, got_in


def _fox_layer_fwd(h, p, side=None):
    w = p["wout"].shape[0]
    heads = w // FOX_HEAD_DIM
    t = h.shape[0]
    qkvz = _mm(h, p["wqkvz"], name="fox_qkvz")
    fr = _mm(h, p["wf"], name="fox_f")
    logf = _rw_fwd(_logf_fn, [fr], [p["fb"]], [LANE], name="fox_logf")[0]
    cum = _cumsum_rows(logf, reverse=False, name="fox_cumsum")
    ck = _lane_bcast(cum[:, :heads].T)
    (o, lse), got = _attn_fwd(qkvz, 0, qkvz, heads, qkvz, 2 * heads, None, None, ck,
                              heads=heads, scale=FOX_HEAD_DIM ** -0.5, name="fox_attn", side=side)
    yg = _rw_fwd(_gate_fn, [o, (qkvz, w, 3)], [], [w], name="fox_gate")[0]
    out = _mm(yg, p["wout"], name="fox_out")
    return _ln_fwd(h, out, p, "fox"), (h, qkvz, fr, ck, o, lse, yg, out), got


def _fox_layer_bwd(saved, p, dh2, side=None):
    h, qkvz, fr, ck, o, lse, yg, out = saved
    w = p["wout"].shape[0]
    heads = w // FOX_HEAD_DIM
    t = h.shape[0]
    dh, dout, dg, db = _ln_bwd(h, out, p, dh2, "fox")
    dyg = _mm(dout, p["wout"], tb=True, name="fox_out_dx")
    g = {"ln_g": dg, "ln_b": db, "wout": _mm(yg, dout, ta=True, out_dtype=bf16, name="fox_out_dw")}
    do, dz = _rw_bwd(_gate_fn, [o, (qkvz, w, 3)], [], [dyg], name="fox_gate_bwd")
    delta = _attn_delta(do, o, heads=heads, name="fox_delta")
    (dq, dk, dv, drow, dck), got = _attn_bwd(qkvz, 0, qkvz, heads, qkvz, 2 * heads, None, None, ck, do, lse, delta,
                                             heads=heads, scale=FOX_HEAD_DIM ** -0.5, name="fox_attn_bwd", side=side)
    dqkvz = jnp.concatenate([dq, dk, dv, dz], axis=1)
    dcum = jnp.pad((drow.reshape(heads, t) + dck.reshape(heads, t)).T, ((0, 0), (0, LANE - heads)))
    dlogf = _cumsum_rows(dcum, reverse=True, name="fox_cumsum_bwd")
    dfr, g["fb"] = _rw_bwd(_logf_fn, [fr], [p["fb"]], [dlogf], name="fox_logf_bwd")
    dh = _mm(dqkvz, p["wqkvz"], tb=True, add=dh, name="fox_qkvz_dx")
    dh = _mm(dfr, p["wf"], tb=True, add=dh, name="fox_f_dx")
    g["wqkvz"] = _mm(h, dqkvz, ta=True, name="fox_qkvz_dw")
    g["wf"] = _mm(h, dfr, ta=True, name="fox_f_dw")
    return dh, g, got


def _mla_layer_fwd(h, p, cos, sin):
    rank = p["qn"].shape[1]
    w = p["wout"].shape[0]
    heads = w // MLA_V
    lat = _mm(h, p["wlat"], name="mla_lat")
    kpr = _mm(h, p["wkpe"], name="mla_kpe")
    z = _mm(h, p["wz"], name="mla_z")
    qnl, kvnl = _rw_fwd(_make_mla_norm_fn(rank), [lat], [p["qn"], p["kvn"]], [rank, rank], name="mla_norm")
    qno = _mm(qnl, p["wqn"], name="mla_qn")
    qpr = _mm(qnl, p["wqp"], name="mla_qp")
    kno = _mm(kvnl, p["wkn"], name="mla_kn")
    v = _mm(kvnl, p["wv"], name="mla_v")
    qp = _rw_fwd(_make_rope_fn(heads), [qpr, cos, sin], [], [heads * LANE], name="mla_rope_q")[0]
    kp = _rw_fwd(_make_rope_fn(1), [kpr, cos, sin], [], [LANE], name="mla_rope_k")[0]
    scale = (MLA_NOPE + MLA_ROPE) ** -0.5
    (o, lse), _ = _attn_fwd(qno, 0, kno, 0, v, 0, qp, kp, None, heads=heads, scale=scale, name="mla_attn")
    yg = _rw_fwd(_gate_fn, [o, z], [], [w], name="mla_gate")[0]
    out = _mm(yg, p["wout"], name="mla_out")
    return _ln_fwd(h, out, p, "mla"), (h, lat, kpr, z, qnl, kvnl, qno, qpr, kno, v, qp, kp, o, lse, yg, out)


def _mla_layer_bwd(saved, p, dh2, cos, sin, side=None):
    h, lat, kpr, z, qnl, kvnl, qno, qpr, kno, v, qp, kp, o, lse, yg, out = saved
    rank = p["qn"].shape[1]
    w = p["wout"].shape[0]
    heads = w // MLA_V
    dh, dout, dg, db = _ln_bwd(h, out, p, dh2, "mla")
    dyg = _mm(dout, p["wout"], tb=True, name="mla_out_dx")
    g = {"ln_g": dg, "ln_b": db, "wout": _mm(yg, dout, ta=True, out_dtype=bf16, name="mla_out_dw")}
    do, dz = _rw_bwd(_gate_fn, [o, z], [], [dyg], name="mla_gate_bwd")
    delta = _attn_delta(do, o, heads=heads, name="mla_delta")
    (dqno, dkno, dv, dqp, dkp), got = _attn_bwd(qno, 0, kno, 0, v, 0, qp, kp, None, do, lse, delta, heads=heads,
                                                scale=(MLA_NOPE + MLA_ROPE) ** -0.5, name="mla_attn_bwd", side=side)
    (dqpr,) = _rw_bwd(_make_rope_fn(heads), [qpr, cos, sin], [], [dqp], n_const=2, name="mla_rope_q_bwd")
    (dkpr,) = _rw_bwd(_make_rope_fn(1), [kpr, cos, sin], [], [dkp], n_const=2, name="mla_rope_k_bwd")
    dqnl = _mm(dqno, p["wqn"], tb=True, name="mla_qn_dx")
    dqnl = _mm(dqpr, p["wqp"], tb=True, add=dqnl, name="mla_qp_dx")
    dkvnl = _mm(dkno, p["wkn"], tb=True, name="mla_kn_dx")
    dkvnl = _mm(dv, p["wv"], tb=True, add=dkvnl, name="mla_v_dx")
    g["wqn"] = _mm(qnl, dqno, ta=True, name="mla_qn_dw")
    g["wqp"] = _mm(qnl, dqpr, ta=True, name="mla_qp_dw")
    g["wkn"] = _mm(kvnl, dkno, ta=True, name="mla_kn_dw")
    g["wv"] = _mm(kvnl, dv, ta=True, name="mla_v_dw")
    dlat, g["qn"], g["kvn"] = _rw_bwd(_make_mla_norm_fn(rank), [lat], [p["qn"], p["kvn"]], [dqnl, dkvnl],
                                     name="mla_norm_bwd")
    dh = _mm(dlat, p["wlat"], tb=True, add=dh, name="mla_lat_dx")
    dh = _mm(dkpr, p["wkpe"], tb=True, add=dh, name="mla_kpe_dx")
    dh = _mm(dz, p["wz"], tb=True, add=dh, name="mla_z_dx")
    g["wlat"] = _mm(h, dlat, ta=True, name="mla_lat_dw")
    g["wkpe"] = _mm(h, dkpr, ta=True, name="mla_kpe_dw")
    g["wz"] = _mm(h, dz, ta=True, name="mla_z_dw")
    return dh, g, got


def _s5_layer_fwd(h, p):
    w = p["wout"].shape[0]
    u = _mm(h, p["wu"], name="s5_u")
    z = _mm(h, p["wz"], name="s5_z")
    ys, xp, states = _s5_fwd(u, p["lamv"], p["pw"], p["bmat"], p["cmat"], name="s5_scan")
    y1 = _rw_fwd(_s5_act_fn, [ys, u], [p["d"]], [w], name="s5_act")[0]
    gp = _mm(y1, p["wglu"], name="s5_glu")
    y2 = _rw_fwd(_s5_glu_fn, [y1, gp, z], [p["bglu"]], [w], name="s5_gate")[0]
    out = _mm(y2, p["wout"], name="s5_out")
    return _ln_fwd(h, out, p, "s5"), (h, u, z, ys, xp, states, y1, gp, y2, out)


def _s5_layer_bwd(saved, p, dh2):
    h, u, z, ys, xp, states, y1, gp, y2, out = saved
    dh, dout, dg, db = _ln_bwd(h, out, p, dh2, "s5")
    dy2 = _mm(dout, p["wout"], tb=True, name="s5_out_dx")
    g = {"ln_g": dg, "ln_b": db, "wout": _mm(y2, dout, ta=True, out_dtype=bf16, name="s5_out_dw")}
    dy1, dgp, dz, g["bglu"] = _rw_bwd(_s5_glu_fn, [y1, gp, z], [p["bglu"]], [dy2], name="s5_gate_bwd")
    dy1 = _mm(dgp, p["wglu"], tb=True, add=dy1, name="s5_glu_dx")
    g["wglu"] = _mm(y1, dgp, ta=True, out_dtype=bf16, name="s5_glu_dw")
    dys, du, g["d"] = _rw_bwd(_s5_act_fn, [ys, u], [p["d"]], [dy1], name="s5_act_bwd")
    du, g["bmat"], g["cmat"], dlam = _s5_bwd(u, dys, du, xp, states, p["lamv"], p["pw"], p["bmat"], p["cmat"],
                                             name="s5_scan_bwd")
    g["lamv"] = dlam[:, 0:1, :]
    dh = _mm(du, p["wu"], tb=True, add=dh, name="s5_u_dx")
    dh = _mm(dz, p["wz"], tb=True, add=dh, name="s5_z_dx")
    g["wu"] = _mm(h, du, ta=True, name="s5_u_dw")
    g["wz"] = _mm(h, dz, ta=True, name="s5_z_dw")
    return dh, g


def _pad_last(a, to):
    return jnp.pad(a, [(0, 0)] * (a.ndim - 1) + [(0, to - a.shape[-1])])


def _row(v):
    return v.reshape(1, -1)


def _prep_ssd(w):
    di, cd, hh = w["ssd_norm_w"].shape[0], w["ssd_conv_b"].shape[0], w["ssd_dt_bias"].shape[0]
    groups = (cd - di) // (2 * SSD_STATE)
    e = hh // groups

    def perm(v):
        lead = v.shape[:-1]
        return _pad_last(v.reshape(lead + (groups, e)), LANE).reshape(lead + (groups * LANE,))

    w_in = w["ssd_w_in"]
    return dict(
        wz=w_in[:, :di], wxbc=w_in[:, di:di + cd], wdt=perm(w_in[:, di + cd:]),
        conv_w=w["ssd_conv_w"], conv_b=_row(w["ssd_conv_b"]),
        dtb=perm(w["ssd_dt_bias"]).reshape(groups, 1, LANE), alog=perm(w["ssd_a_log"]).reshape(groups, 1, LANE),
        dsk=perm(w["ssd_d"]).reshape(groups, 1, LANE), norm_w=_row(w["ssd_norm_w"]),
        ln_g=_row(w["ln_g"][0]), ln_b=_row(w["ln_b"][0]))


def _prep_fox(w):
    wd = w["fox_w_out"].shape[0]
    w_in = w["fox_w_in"]
    return dict(wqkvz=w_in[:, :4 * wd], wf=_pad_last(w_in[:, 4 * wd:], LANE), wout=w["fox_w_out"],
                fb=_pad_last(_row(w["fox_f_bias"]), LANE), ln_g=_row(w["ln_g"][1]), ln_b=_row(w["ln_b"][1]))


def _prep_mla(w):
    rank = w["mla_q_norm"].shape[0]
    wd = w["mla_w_out"].shape[0]
    heads = wd // MLA_V
    w_in = w["mla_w_in"]
    qu = w["mla_w_q_up"].reshape(rank, heads, MLA_NOPE + MLA_ROPE)
    kvu = w["mla_w_kv_up"].reshape(w["mla_w_kv_up"].shape[0], heads, MLA_NOPE + MLA_V)
    return dict(
        wlat=w_in[:, :2 * rank], wkpe=_pad_last(w_in[:, 2 * rank:2 * rank + MLA_ROPE], LANE),
        wz=w_in[:, 2 * rank + MLA_ROPE:],
        wqn=qu[:, :, :MLA_NOPE].reshape(rank, heads * LANE),
        wqp=_pad_last(qu[:, :, MLA_NOPE:], LANE).reshape(rank, heads * LANE),
        wkn=kvu[:, :, :MLA_NOPE].reshape(-1, heads * LANE), wv=kvu[:, :, MLA_NOPE:].reshape(-1, heads * LANE),
        wout=w["mla_w_out"], qn=_row(w["mla_q_norm"]), kvn=_row(w["mla_kv_norm"]),
        ln_g=_row(w["ln_g"][2]), ln_b=_row(w["ln_b"][2]))


def _prep_s5(w):
    wd = w["s5_w_out"].shape[0]
    w_in = w["s5_w_in"]
    lamv, bmat, cmat = _s5_discretize(w["s5_lambda_re"], w["s5_lambda_im"], w["s5_log_step"],
                                      w["s5_b_re"], w["s5_b_im"], w["s5_c_re"], w["s5_c_im"])
    return dict(wu=w_in[:, :wd], wz=w_in[:, wd:], wglu=w["s5_w_glu"], wout=w["s5_w_out"],
                d=_row(w["s5_d"]), bglu=_row(w["s5_b_glu"]), lamv=lamv, bmat=bmat, cmat=cmat,
                ln_g=_row(w["ln_g"][3]), ln_b=_row(w["ln_b"][3]))


def _rope_tables(positions):
    inv_freq = ROPE_BASE ** (-jnp.arange(0, MLA_ROPE, 2, dtype=f32) / MLA_ROPE)
    ang = positions.astype(f32).reshape(-1, 1) * inv_freq
    return jnp.cos(ang), jnp.sin(ang)


def _heads3(v):
    return v.reshape(v.shape[0], -1, LANE)


def _flat2(v):
    return v.reshape(v.shape[0], -1)


def _natural_grads_ssd(g, e_heads):
    return {"ssd_w_in": jnp.concatenate([g["wz"], g["wxbc"], _flat2(_heads3(g["wdt"])[:, :, :e_heads])], axis=1),
            "ssd_w_out": g["wout"]}


def _natural_grads_fox(g, heads):
    return {"fox_w_in": jnp.concatenate([g["wqkvz"], g["wf"][:, :heads]], axis=1), "fox_w_out": g["wout"]}


def _natural_grads_mla(g):
    return {"mla_w_in": jnp.concatenate([g["wlat"], g["wkpe"][:, :MLA_ROPE], g["wz"]], axis=1),
            "mla_w_q_up": _flat2(jnp.concatenate([_heads3(g["wqn"]), _heads3(g["wqp"])[:, :, :MLA_ROPE]], axis=2)),
            "mla_w_kv_up": _flat2(jnp.concatenate([_heads3(g["wkn"]), _heads3(g["wv"])], axis=2)),
            "mla_w_out": g["wout"]}


def _natural_grads_s5(g):
    return {"s5_w_in": jnp.concatenate([g["wu"], g["wz"]], axis=1), "s5_w_glu": g["wglu"], "s5_w_out": g["wout"]}


def _sequence_step(x, positions, tgt, small, big_of_layer, gather_sides=(None, None), exchange_side=None,
                   ssd_in_sides=None, ssd_wout_of=None):
    cos, sin = _rope_tables(positions)
    exchange_side = exchange_side or (lambda nat: None)
    vjps = []

    def prepared(prep, big):
        p, vjp = jax.vjp(lambda s: prep({**big, **s}), small)
        vjps.append((vjp, tuple(p)))
        return p

    big0 = big_of_layer[0](None)
    p0 = prepared(_prep_ssd, big0)
    wout_of = (lambda got: big0["ssd_w_out"]) if "ssd_w_out" in big0 else ssd_wout_of
    h1, s0, got1 = _ssd_layer_fwd(x, p0, wout_of, side=gather_sides[0])
    p1 = prepared(_prep_fox, big_of_layer[1](got1))
    h2, s1, got23 = _fox_layer_fwd(h1, p1, side=gather_sides[1])
    p2 = prepared(_prep_mla, big_of_layer[2](got23))
    p3 = prepared(_prep_s5, big_of_layer[3](got23))
    p3["pw"] = _s5_powers(p3["lamv"])
    h3, s2 = _mla_layer_fwd(h2, p2, cos, sin)
    h4, s3 = _s5_layer_fwd(h3, p3)
    dh, loss = _loss_head(h4, tgt)
    dh, g3 = _s5_layer_bwd(s3, p3, dh)
    nat3 = _natural_grads_s5(g3)
    dh, g2, parts3 = _mla_layer_bwd(s2, p2, dh, cos, sin, side=exchange_side(nat3))
    nat2 = _natural_grads_mla(g2)
    dh, g1, parts2 = _fox_layer_bwd(s1, p1, dh, side=exchange_side(nat2))
    nat1 = _natural_grads_fox(g1, small["fox_f_bias"].shape[0])
    e_heads = small["ssd_dt_bias"].shape[0] // p0["dtb"].shape[0]
    dh, g0, parts1, parts_in = _ssd_layer_bwd(
        s0, p0, dh, side=lambda gw: exchange_side({**nat1, "ssd_w_out": gw}),
        in_sides=None if ssd_in_sides is None else
        lambda g: ssd_in_sides(_natural_grads_ssd(g, e_heads)["ssd_w_in"]))
    nat0 = _natural_grads_ssd(g0, e_heads)
    g_small = None
    for p, (vjp, keys), g in zip((p0, p1, p2, p3), vjps, (g0, g1, g2, g3)):
        (gi,) = vjp({k: g[k].astype(p[k].dtype) for k in keys})
        g_small = gi if g_small is None else jax.tree.map(jnp.add, g_small, gi)
    return loss, dh, g_small, (nat0, nat1, nat2, nat3), (parts1, parts2, parts3, parts_in)


FLAT_COLS = 1024
FLAT_ROW_ALIGN = 128


def _pack(arrs):
    flat = jnp.concatenate([a.reshape(-1) for a in arrs])
    unit = FLAT_COLS * FLAT_ROW_ALIGN
    return jnp.pad(flat, (0, -flat.shape[0] % unit)).reshape(-1, FLAT_COLS)


def _unpack(flat2d, shapes):
    flat = flat2d.reshape(-1)
    out, off = [], 0
    for s in shapes:
        sz = math.prod(s)
        out.append(flat[off:off + sz].reshape(s))
        off += sz
    return out


def _unpack_gathered(gathered, shapes, axes):
    flat = gathered.reshape(N_DEV, -1)
    out, off = [], 0
    for s, ax in zip(shapes, axes):
        sz = math.prod(s)
        seg = flat[:, off:off + sz].reshape((N_DEV,) + tuple(s))
        out.append(jnp.concatenate([seg[d] for d in range(N_DEV)], axis=ax)[0])
        off += sz
    return out


MESH = pl.DeviceIdType.MESH
HBM_SPEC = pl.BlockSpec(memory_space=pl.ANY)


def _comm_scratch(n):
    return [pltpu.SemaphoreType.DMA((n, 7)), pltpu.SemaphoreType.DMA((n, 7)), pltpu.SemaphoreType.DMA((n,))]


class _Gather:
    def __init__(self, xs):
        self.ins = list(xs)
        self.out_shape = [jax.ShapeDtypeStruct((N_DEV,) + v.shape, v.dtype) for v in xs]

    def _plan(self, x_refs, out_refs, sems):
        send_sems, recv_sems, local_sems = sems
        x, y, cc = lax.axis_index("x"), lax.axis_index("y"), lax.axis_index("c")
        me, sibling = (x, y, cc), (x, y, 1 - cc)
        chips = [(1 - x, y), (x, 1 - y), (1 - x, 1 - y)]

        def rows(w, px, py, pc):
            return out_refs[w].at[4 * px + 2 * py + pc]

        def copy(w, k, block, to, src=None):
            return pltpu.make_async_remote_copy(
                src_ref=rows(w, *block) if src is None else src, dst_ref=rows(w, *block),
                send_sem=send_sems.at[w, k], recv_sem=recv_sems.at[w, k], device_id=to, device_id_type=MESH)

        n = len(x_refs)
        mine = [pltpu.make_async_copy(x_refs[w], rows(w, *me), local_sems.at[w]) for w in range(n)]
        first = []
        for w in range(n):
            first.append(copy(w, 0, me, sibling, src=x_refs[w]))
            first += [copy(w, 1 + j, me, (*chip, cc), src=x_refs[w]) for j, chip in enumerate(chips)]
        return n, me, sibling, chips, cc, copy, mine, first

    def start(self, x_refs, out_refs, sems):
        *_, mine, first = self._plan(x_refs, out_refs, sems)
        for cp in mine + first:
            cp.start()

    def finish(self, x_refs, out_refs, sems):
        n, me, sibling, chips, cc, copy, mine, first = self._plan(x_refs, out_refs, sems)
        passed = []
        for w in range(n):
            for j, chip in enumerate(chips):
                copy(w, 1 + j, (*chip, cc), me).wait_recv()
                cp = copy(w, 4 + j, (*chip, cc), sibling)
                cp.start()
                passed.append(cp)
        for w in range(n):
            copy(w, 0, sibling, me).wait_recv()
            for j, chip in enumerate(chips):
                copy(w, 4 + j, (*chip, 1 - cc), me).wait_recv()
        for cp in first + passed:
            cp.wait_send()
        for cp in mine:
            cp.wait()


class _Exchange:
    def __init__(self, gs):
        self.ins = list(gs)
        self.out_shape = [jax.ShapeDtypeStruct(v.shape, v.dtype) for v in gs]

    def _plan(self, g_refs, out_refs, sems):
        send_sems, recv_sems, local_sems = sems
        x, y, cc = lax.axis_index("x"), lax.axis_index("y"), lax.axis_index("c")
        me = 4 * x + 2 * y + cc
        peers = []
        for k in range(1, N_DEV):
            px = 1 - x if (k >> 2) & 1 else x
            py = 1 - y if (k >> 1) & 1 else y
            pc = 1 - cc if k & 1 else cc
            peers.append((k, (px, py, pc), 4 * px + 2 * py + pc))

        def copy(w, k, peer, slot_src, slot_dst):
            return pltpu.make_async_remote_copy(
                src_ref=g_refs[w].at[slot_src], dst_ref=out_refs[w].at[slot_dst],
                send_sem=send_sems.at[w, k - 1], recv_sem=recv_sems.at[w, k - 1], device_id=peer, device_id_type=MESH)

        n = len(g_refs)
        mine = [pltpu.make_async_copy(g_refs[w].at[me], out_refs[w].at[me], local_sems.at[w]) for w in range(n)]
        sends = [copy(w, k, peer, idx, me) for w in range(n) for k, peer, idx in peers]
        return n, peers, copy, mine, sends

    def start(self, g_refs, out_refs, sems):
        *_, mine, sends = self._plan(g_refs, out_refs, sems)
        for cp in mine + sends:
            cp.start()

    def finish(self, g_refs, out_refs, sems):
        n, peers, copy, mine, sends = self._plan(g_refs, out_refs, sems)
        for w in range(n):
            for k, peer, idx in peers:
                copy(w, k, peer, idx, idx).wait_recv()
        for cp in sends:
            cp.wait_send()
        for cp in mine:
            cp.wait()


class _Both:
    def __init__(self, first, second):
        self.parts = (first, second)
        self.ins = first.ins + second.ins
        self.out_shape = first.out_shape + second.out_shape

    def _split(self, ins, outs, sems):
        n0 = len(self.parts[0].ins)
        n1 = len(self.parts[1].ins)
        for side, lo, n in ((self.parts[0], 0, n0), (self.parts[1], n0, n1)):
            yield side, ins[lo:lo + n], outs[lo:lo + n], tuple(s.at[pl.ds(lo, n)] for s in sems)

    def start(self, ins, outs, sems):
        for side, i, o, s in self._split(ins, outs, sems):
            side.start(i, o, s)

    def finish(self, ins, outs, sems):
        for side, i, o, s in self._split(ins, outs, sems):
            side.finish(i, o, s)


def _run_comm(side, *, name):
    n = len(side.ins)

    def body(*refs):
        ins, outs, sems = refs[:n], refs[n:2 * n], refs[2 * n:]
        side.start(ins, outs, sems)
        side.finish(ins, outs, sems)

    return pl.pallas_call(
        body, name=name, out_shape=side.out_shape,
        in_specs=[HBM_SPEC] * n, out_specs=[HBM_SPEC] * n, scratch_shapes=_comm_scratch(n),
    )(*side.ins)


def _hosted_call(body, side, *, name, grid, in_specs, out_specs, out_shape, scratch_shapes, args):
    out_specs, out_shape = list(out_specs), list(out_shape)
    if side is None:
        res = pl.pallas_call(
            body, name=name, grid=grid, in_specs=in_specs, out_specs=out_specs, out_shape=out_shape,
            scratch_shapes=scratch_shapes,
            compiler_params=pltpu.CompilerParams(dimension_semantics=("arbitrary",) * len(grid)))(*args)
        return res, None
    n_in, n_out, n_scr, ns = len(in_specs), len(out_specs), len(scratch_shapes), len(side.ins)

    def wrapped(*refs):
        ins, s_ins = refs[:n_in], refs[n_in:n_in + ns]
        outs = refs[n_in + ns:n_in + ns + n_out]
        s_outs = refs[n_in + ns + n_out:n_in + 2 * ns + n_out]
        scr = refs[n_in + 2 * ns + n_out:n_in + 2 * ns + n_out + n_scr]
        sems = refs[n_in + 2 * ns + n_out + n_scr:]
        ids = [pl.program_id(ax) for ax in range(len(grid))]
        first = functools.reduce(lambda p, q: p & q, [i == 0 for i in ids])
        last = functools.reduce(lambda p, q: p & q, [i == g - 1 for i, g in zip(ids, grid)])

        @pl.when(first)
        def _():
            side.start(s_ins, s_outs, sems)

        body(*ins, *outs, *scr)

        @pl.when(last)
        def _():
            side.finish(s_ins, s_outs, sems)

    res = pl.pallas_call(
        wrapped, name=name, grid=grid, in_specs=list(in_specs) + [HBM_SPEC] * ns,
        out_specs=out_specs + [HBM_SPEC] * ns, out_shape=out_shape + side.out_shape,
        scratch_shapes=list(scratch_shapes) + _comm_scratch(ns),
        compiler_params=pltpu.CompilerParams(dimension_semantics=("arbitrary",) * len(grid)))(*args, *side.ins)
    return res[:n_out], res[n_out:]


BIG = (("ssd_w_in", 2), ("ssd_w_out", 1), ("fox_w_in", 2), ("fox_w_out", 1), ("mla_w_in", 2),
       ("mla_w_q_up", 2), ("mla_w_kv_up", 2), ("mla_w_out", 1), ("s5_w_in", 2), ("s5_w_glu", 1), ("s5_w_out", 1))
LAYER_BIG = (("ssd_w_in", "ssd_w_out"), ("fox_w_in", "fox_w_out"),
             ("mla_w_in", "mla_w_q_up", "mla_w_kv_up", "mla_w_out"), ("s5_w_in", "s5_w_glu", "s5_w_out"))
SMALL = (("ssd_conv_w", 2), ("mla_q_norm", 1), ("mla_kv_norm", 1), ("s5_d", 1), ("s5_b_glu", 1))
REPLICATED = ("ln_g", "ln_b", "ssd_conv_b", "ssd_dt_bias", "ssd_a_log", "ssd_d", "ssd_norm_w", "fox_f_bias",
              "s5_lambda_re", "s5_lambda_im", "s5_log_step", "s5_b_re", "s5_b_im", "s5_c_re", "s5_c_im")
WEIGHT_ORDER = ("ln_g", "ln_b", "ssd_w_in", "ssd_conv_w", "ssd_conv_b", "ssd_dt_bias", "ssd_a_log", "ssd_d",
                "ssd_norm_w", "ssd_w_out", "fox_w_in", "fox_f_bias", "fox_w_out", "mla_w_in", "mla_q_norm",
                "mla_kv_norm", "mla_w_q_up", "mla_w_kv_up", "mla_w_out", "s5_w_in", "s5_lambda_re", "s5_lambda_im",
                "s5_log_step", "s5_b_re", "s5_b_im", "s5_c_re", "s5_c_im", "s5_d", "s5_w_glu", "s5_b_glu", "s5_w_out")


def _sum_adamw(parts, w, m, v, *, name):
    r, c = w.shape
    row_bytes = 2 * c * (N_DEV * parts.dtype.itemsize + 7 * 4)
    tr = _pick(r, [t for t in (256, 128, 64, 32, 16) if t * row_bytes <= VMEM_BLOCK_BUDGET])

    def body(p_ref, w_ref, m_ref, v_ref, g_ref, d_ref, m2_ref, v2_ref):
        gv = p_ref[0].astype(f32)
        for s in range(1, N_DEV):
            gv = gv + p_ref[s].astype(f32)
        g_ref[...] = gv
        m2 = ADAM_B1 * m_ref[...] + (1.0 - ADAM_B1) * gv
        v2 = ADAM_B2 * v_ref[...] + (1.0 - ADAM_B2) * jnp.square(gv)
        m_hat = m2 / (1.0 - ADAM_B1 ** ADAM_STEP)
        v_hat = v2 / (1.0 - ADAM_B2 ** ADAM_STEP)
        d_ref[...] = -ADAM_LR * (m_hat / (jnp.sqrt(v_hat) + ADAM_EPS) + ADAM_WD * w_ref[...])
        m2_ref[...] = m2
        v2_ref[...] = v2

    spec = pl.BlockSpec((tr, c), lambda i: (i, 0))
    shp = jax.ShapeDtypeStruct((r, c), f32)
    return pl.pallas_call(
        body, name=name, grid=(r // tr,),
        in_specs=[pl.BlockSpec((N_DEV, tr, c), lambda i: (0, i, 0))] + [spec] * 3,
        out_specs=[spec] * 4, out_shape=[shp] * 4,
        compiler_params=pltpu.CompilerParams(dimension_semantics=("parallel",)),
    )(parts, w, m, v)


def _train_step(a):
    small_names, small_axes = [n for n, _ in SMALL], [ax for _, ax in SMALL]
    small_shapes = [a[n].shape for n in small_names]
    rep_shapes = [a[n].shape for n in REPLICATED]

    big_axis = dict(BIG)

    def shards(names):
        return [a[n][0].astype(bf16) for n in names]

    def natural(names, gathered):
        w = {}
        for n, g in zip(names, gathered):
            if big_axis[n] == 1:
                w[n] = g.reshape(-1, g.shape[2])
            else:
                w[n] = jnp.transpose(g, (1, 0, 2)).reshape(g.shape[1], -1)
        return w

    def to_owner(nat):
        blocks = []
        for n, g in nat.items():
            if big_axis[n] == 1:
                g = g.reshape(N_DEV, -1, g.shape[1])
            else:
                g = jnp.transpose(g.reshape(g.shape[0], N_DEV, -1), (1, 0, 2))
            blocks.append(g.astype(bf16))
        return blocks

    small_flat = _pack([a[n] for n in small_names])
    got0 = _run_comm(_Gather(shards(("ssd_w_in",)) + [small_flat]), name="gather_ssd_weights")
    small = dict(zip(small_names, _unpack_gathered(got0[-1], small_shapes, small_axes)))
    for n in REPLICATED:
        small[n] = a[n] if n in ("ln_g", "ln_b") else a[n][0]
    rest_names = LAYER_BIG[2] + LAYER_BIG[3]
    under_scan = LAYER_BIG[1] + ("ssd_w_out",)
    big_of_layer = (lambda got: natural(("ssd_w_in",), got0[:1]),
                    lambda got: natural(LAYER_BIG[1], got[:len(LAYER_BIG[1])]),
                    lambda got: natural(LAYER_BIG[2], got[:len(LAYER_BIG[2])]),
                    lambda got: natural(LAYER_BIG[3], got[len(LAYER_BIG[2]):]))
    def ssd_in_sides(nat_w_in):
        (blocks,) = to_owner({"ssd_w_in": nat_w_in})
        cut = blocks.shape[1] * 5 // 8
        return _Exchange([blocks[:, :cut]]), _Exchange([blocks[:, cut:]])

    loss, grad_x, g_small, nats, hosted_parts = _sequence_step(
        a["x"][0], a["positions"][0], a["loss_target"][0], small, big_of_layer,
        gather_sides=(_Gather(shards(under_scan)), _Gather(shards(rest_names))),
        exchange_side=lambda nat: _Exchange(to_owner(nat)), ssd_in_sides=ssd_in_sides,
        ssd_wout_of=lambda got: natural(("ssd_w_out",), got[-1:])["ssd_w_out"])
    parts_in = hosted_parts[3]
    parts_of = {"ssd_w_in": jnp.concatenate([parts_in[0][0], parts_in[1][0]], axis=1)}
    for names, parts in zip((LAYER_BIG[1] + ("ssd_w_out",), LAYER_BIG[2], LAYER_BIG[3]), hosted_parts[:3]):
        parts_of.update(zip(names, parts, strict=True))
    small_to_owner = jax.linear_transpose(
        lambda s: _unpack_gathered(s, small_shapes, small_axes), jax.ShapeDtypeStruct(got0[-1].shape, f32))
    (g_small_blocks,) = small_to_owner([g_small[n] for n in small_names])
    g_rep_local = _pack([g_small[n] if n in ("ln_g", "ln_b") else g_small[n][None] for n in REPLICATED])
    small_parts, g_rep_all = _run_comm(_Both(_Exchange([g_small_blocks]), _Gather([g_rep_local])),
                                       name="exchange_small_gradients")

    out = {"grad": {}, "delta": {}, "new_m": {}, "new_v": {}}
    kinds = ("grad", "delta", "new_m", "new_v")
    for n, _ in BIG:
        res = _sum_adamw(parts_of[n], a[n][0], a["m_" + n][0], a["v_" + n][0], name="adamw_" + n)
        for kind, r in zip(kinds, res):
            out[kind][n] = r[None]
    upd_small = _sum_adamw(small_parts, small_flat, _pack([a["m_" + n] for n in small_names]),
                           _pack([a["v_" + n] for n in small_names]), name="adamw_small_sharded")
    for kind, r in zip(kinds, upd_small):
        out[kind].update(zip(small_names, _unpack(r, small_shapes)))
    upd_rep = _sum_adamw(g_rep_all, _pack([a[n] for n in REPLICATED]), _pack([a["m_" + n] for n in REPLICATED]),
                         _pack([a["v_" + n] for n in REPLICATED]), name="adamw_replicated")
    for kind, r in zip(kinds, upd_rep):
        out[kind].update(zip(REPLICATED, _unpack(r, rep_shapes)))
    loss_total = lax.psum(loss[0, 0], ("x", "y", "c"))
    return (loss_total, grad_x[None],
            *[out[kind][n] for kind in ("grad", "delta", "new_m", "new_v") for n in WEIGHT_ORDER])


def kernel(x, positions, ln_g, ln_b, ssd_w_in, ssd_conv_w, ssd_conv_b, ssd_dt_bias, ssd_a_log, ssd_d, ssd_norm_w, ssd_w_out, fox_w_in, fox_f_bias, fox_w_out, mla_w_in, mla_q_norm, mla_kv_norm, mla_w_q_up, mla_w_kv_up, mla_w_out, s5_w_in, s5_lambda_re, s5_lambda_im, s5_log_step, s5_b_re, s5_b_im, s5_c_re, s5_c_im, s5_d, s5_w_glu, s5_b_glu, s5_w_out, loss_target, m_ln_g, m_ln_b, m_ssd_w_in, m_ssd_conv_w, m_ssd_conv_b, m_ssd_dt_bias, m_ssd_a_log, m_ssd_d, m_ssd_norm_w, m_ssd_w_out, m_fox_w_in, m_fox_f_bias, m_fox_w_out, m_mla_w_in, m_mla_q_norm, m_mla_kv_norm, m_mla_w_q_up, m_mla_w_kv_up, m_mla_w_out, m_s5_w_in, m_s5_lambda_re, m_s5_lambda_im, m_s5_log_step, m_s5_b_re, m_s5_b_im, m_s5_c_re, m_s5_c_im, m_s5_d, m_s5_w_glu, m_s5_b_glu, m_s5_w_out, v_ln_g, v_ln_b, v_ssd_w_in, v_ssd_conv_w, v_ssd_conv_b, v_ssd_dt_bias, v_ssd_a_log, v_ssd_d, v_ssd_norm_w, v_ssd_w_out, v_fox_w_in, v_fox_f_bias, v_fox_w_out, v_mla_w_in, v_mla_q_norm, v_mla_kv_norm, v_mla_w_q_up, v_mla_w_kv_up, v_mla_w_out, v_s5_w_in, v_s5_lambda_re, v_s5_lambda_im, v_s5_log_step, v_s5_b_re, v_s5_b_im, v_s5_c_re, v_s5_c_im, v_s5_d, v_s5_w_glu, v_s5_b_glu, v_s5_w_out):
    return _train_step(dict(locals()))
```
